```python
import math
import jax, jax.numpy as jnp
from jax import lax
import numpy as np

D_MODEL = 1024
BATCH = 8
SEQ = 2048
DEPTH = 1
DEC_BATCH = 128
DEC_SEQ = 8
PAST_LEN = 16384
PAGE_SIZE = 128

HEAD_DIM = 64
N_RET_HEADS = 8
RET_WIDTH = N_RET_HEADS * HEAD_DIM
N_SWA_HEADS = 8
N_SWA_KV = 2
SWA_GROUP = N_SWA_HEADS // N_SWA_KV
SWA_WIDTH = N_SWA_HEADS * HEAD_DIM
SWA_KV_WIDTH = N_SWA_KV * HEAD_DIM
MIX_WIDTH = RET_WIDTH + SWA_WIDTH
IN_WIDTH = 4 * RET_WIDTH + SWA_WIDTH + 2 * SWA_KV_WIDTH
D_FF = 4 * D_MODEL
WINDOW = 128
RET_CHUNK = 128
EPS = 1e-6
NEG_INF = -1e30

kernel_name = "hymba_retention_swa_sink_decode_step"


def rms_norm(x, gain=None):
    xf = x.astype(jnp.float32)
    y = xf * lax.rsqrt(jnp.mean(xf * xf, axis=-1, keepdims=True) + EPS)
    if gain is not None:
        y = y * gain.astype(jnp.float32)
    return y.astype(x.dtype)


def retention_log_decay():
    return jnp.log(1.0 - 2.0 ** (-5.0 - jnp.arange(N_RET_HEADS, dtype=jnp.float32)))


def alibi_slopes():
    return 2.0 ** (-8.0 * jnp.arange(1, N_SWA_HEADS + 1, dtype=jnp.float32) / N_SWA_HEADS)


def retention(q, k, v, s0, chunk):
    b, t, h, d = q.shape
    n = t // chunk
    dt = q.dtype
    log_g = retention_log_decay()
    idx = jnp.arange(chunk, dtype=jnp.float32)
    diff = idx[:, None] - idx[None, :]
    intra = jnp.where(diff >= 0, jnp.exp(log_g[:, None, None] * jnp.maximum(diff, 0.0)), 0.0).astype(dt)
    q_decay = jnp.exp(log_g[:, None] * (idx + 1.0)).astype(dt)
    k_decay = jnp.exp(log_g[:, None] * (chunk - 1.0 - idx)).astype(dt)
    s_decay = jnp.exp(log_g * chunk).astype(dt)
    k = k * (d ** -0.5)

    def to_chunks(a):
        return a.reshape(b, n, chunk, h, a.shape[-1]).transpose(1, 0, 3, 2, 4)

    qc, kc, vc = to_chunks(q), to_chunks(k), to_chunks(v)

    def step(s, inp):
        qi, ki, vi = inp
        scores = jnp.einsum('bhld,bhmd->bhlm', qi, ki) * intra
        o = (jnp.einsum('bhlm,bhme->bhle', scores, vi)
             + jnp.einsum('bhld,bhde->bhle', qi * q_decay[..., None], s))
        s = s * s_decay[:, None, None] + jnp.einsum('bhld,bhle->bhde', ki * k_decay[..., None], vi)
        return s, o

    s, o = lax.scan(step, s0.astype(dt), (qc, kc, vc))
    o = o.transpose(1, 0, 3, 2, 4).reshape(b, t, h, v.shape[-1])
    return o, s


def sink_softmax(scores, sinks):
    sink = sinks.astype(jnp.float32).reshape(N_SWA_KV, SWA_GROUP)[:, :, None, None]
    sink_col = jnp.broadcast_to(sink, scores.shape[:-1] + (1,))
    p = jax.nn.softmax(jnp.concatenate([scores, sink_col], axis=-1), axis=-1)
    return p[..., :-1]


def swa_prompt(q, k, v, sinks):
    b, t, h, d = q.shape
    w = WINDOW
    n = t // w
    qb = q.reshape(b, n, w, N_SWA_KV, SWA_GROUP, d)
    kb = k.reshape(b, n, w, N_SWA_KV, d)
    vb = v.reshape(b, n, w, N_SWA_KV, d)
    pad = jnp.zeros_like(kb[:, :1])
    kk = jnp.concatenate([jnp.concatenate([pad, kb[:, :-1]], axis=1), kb], axis=2)
    vv = jnp.concatenate([jnp.concatenate([pad.astype(vb.dtype), vb[:, :-1]], axis=1), vb], axis=2)
    i = jnp.arange(w)[:, None]
    j = jnp.arange(2 * w)[None, :]
    dist = w + i - j
    key_pos = jnp.arange(n)[:, None, None] * w - w + j[None]
    valid = (dist >= 0)[None] & (dist < w)[None] & (key_pos >= 0)
    slopes = alibi_slopes().reshape(N_SWA_KV, SWA_GROUP)[:, :, None, None]
    scores = jnp.einsum('bnqkgd,bnskd->bnkgqs', qb, kk).astype(jnp.float32) * (d ** -0.5)
    scores = scores - slopes * dist.astype(jnp.float32)
    scores = jnp.where(valid[None, :, None, None], scores, NEG_INF)
    probs = sink_softmax(scores, sinks).astype(v.dtype)
    out = jnp.einsum('bnkgqs,bnskd->bnqkgd', probs, vv)
    return out.reshape(b, t, h * d)


def swa_sample(q, k, v, kbuf, vbuf, sinks):
    b, t, h, d = q.shape
    wb = kbuf.shape[1]
    kk = jnp.concatenate([kbuf.astype(k.dtype), k], axis=1)
    vv = jnp.concatenate([vbuf.astype(v.dtype), v], axis=1)
    i = jnp.arange(t)[:, None]
    j = jnp.arange(wb + t)[None, :]
    dist = wb + i - j
    valid = (dist >= 0) & (dist < WINDOW)
    slopes = alibi_slopes().reshape(N_SWA_KV, SWA_GROUP)[:, :, None, None]
    qg = q.reshape(b, t, N_SWA_KV, SWA_GROUP, d)
    scores = jnp.einsum('bqkgd,bskd->bkgqs', qg, kk).astype(jnp.float32) * (d ** -0.5)
    scores = scores - slopes * dist.astype(jnp.float32)
    scores = jnp.where(valid, scores, NEG_INF)
    probs = sink_softmax(scores, sinks).astype(v.dtype)
    out = jnp.einsum('bkgqs,bskd->bqkgd', probs, vv).reshape(b, t, h * d)
    return out, kk[:, -wb:], vv[:, -wb:]


def decoder_layer(x, ret_state, win_k, win_v, norm_mix_gain, w_in, q_norm_gain, k_norm_gain,
                  attn_sinks, w_out, norm_ffn_gain, w_up, w_down):
    b, t, _ = x.shape
    hn = rms_norm(x, norm_mix_gain)
    proj = hn @ w_in
    cuts = np.cumsum([RET_WIDTH] * 4 + [SWA_WIDTH, SWA_KV_WIDTH]).tolist()
    q_r, k_r, v_r, g_r, q_s, k_s, v_s = jnp.split(proj, cuts, axis=-1)

    q_r = q_r.reshape(b, t, N_RET_HEADS, HEAD_DIM)
    k_r = k_r.reshape(b, t, N_RET_HEADS, HEAD_DIM)
    v_r = v_r.reshape(b, t, N_RET_HEADS, HEAD_DIM)
    if ret_state is None:
        ret_state = jnp.zeros((b, N_RET_HEADS, HEAD_DIM, HEAD_DIM), x.dtype)
    chunk = RET_CHUNK if t % RET_CHUNK == 0 else t
    o_r, new_ret = retention(q_r, k_r, v_r, ret_state, chunk)
    o_r = rms_norm(o_r).reshape(b, t, RET_WIDTH) * jax.nn.silu(g_r)

    q_s = rms_norm(q_s.reshape(b, t, N_SWA_HEADS, HEAD_DIM), q_norm_gain)
    k_s = rms_norm(k_s.reshape(b, t, N_SWA_KV, HEAD_DIM), k_norm_gain)
    v_s = v_s.reshape(b, t, N_SWA_KV, HEAD_DIM)
    if win_k is None:
        o_s = swa_prompt(q_s, k_s, v_s, attn_sinks)
        new_k, new_v = k_s[:, -WINDOW:], v_s[:, -WINDOW:]
    else:
        o_s, new_k, new_v = swa_sample(q_s, k_s, v_s, win_k, win_v, attn_sinks)

    h = x + jnp.concatenate([o_r, o_s], axis=-1) @ w_out
    hf = rms_norm(h, norm_ffn_gain)
    h = h + jnp.square(jax.nn.relu(hf @ w_up)) @ w_down
    return h, new_ret, new_k, new_v


def setup_inputs(seed: int = 0) -> dict:
    key = jax.random.key(seed)
    ks = jax.random.split(key, 14)
    wb = min(WINDOW, PAST_LEN)
    f32 = jnp.float32
    return {
        "x_prompt": jax.random.normal(ks[0], (BATCH, SEQ, D_MODEL), f32),
        "x_sample": jax.random.normal(ks[1], (DEC_BATCH, DEC_SEQ, D_MODEL), f32),
        "state_ret": 0.5 * jax.random.normal(ks[2], (DEC_BATCH, N_RET_HEADS, HEAD_DIM, HEAD_DIM), f32),
        "cache_swa_k": jax.random.normal(ks[3], (DEC_BATCH, wb, N_SWA_KV, HEAD_DIM), f32),
        "cache_swa_v": jax.random.normal(ks[4], (DEC_BATCH, wb, N_SWA_KV, HEAD_DIM), f32),
        "norm_mix_gain": 1.0 + 0.02 * jax.random.normal(ks[5], (D_MODEL,), f32),
        "w_in": jax.random.normal(ks[6], (D_MODEL, IN_WIDTH), f32) * D_MODEL ** -0.5,
        "q_norm_gain": 1.0 + 0.02 * jax.random.normal(ks[7], (HEAD_DIM,), f32),
        "k_norm_gain": 1.0 + 0.02 * jax.random.normal(ks[8], (HEAD_DIM,), f32),
        "attn_sinks": jax.random.normal(ks[9], (N_SWA_HEADS,), f32),
        "w_out": jax.random.normal(ks[10], (MIX_WIDTH, D_MODEL), f32) * MIX_WIDTH ** -0.5,
        "norm_ffn_gain": 1.0 + 0.02 * jax.random.normal(ks[11], (D_MODEL,), f32),
        "w_up": jax.random.normal(ks[12], (D_MODEL, D_FF), f32) * D_MODEL ** -0.5,
        "w_down": jax.random.normal(ks[13], (D_FF, D_MODEL), f32) * D_FF ** -0.5,
    }


def reference(x_prompt, x_sample, state_ret, cache_swa_k, cache_swa_v, norm_mix_gain, w_in,
              q_norm_gain, k_norm_gain, attn_sinks, w_out, norm_ffn_gain, w_up, w_down):
    y_prompt, y_sample = x_prompt, x_sample
    for _ in range(DEPTH):
        y_prompt, ret_p, k_p, v_p = decoder_layer(
            y_prompt, None, None, None, norm_mix_gain, w_in, q_norm_gain, k_norm_gain,
            attn_sinks, w_out, norm_ffn_gain, w_up, w_down)
        y_sample, ret_s, k_s, v_s = decoder_layer(
            y_sample, state_ret, cache_swa_k, cache_swa_v, norm_mix_gain, w_in, q_norm_gain,
            k_norm_gain, attn_sinks, w_out, norm_ffn_gain, w_up, w_down)
    return (y_prompt, y_sample, ret_p, k_p, v_p, ret_s, k_s, v_s)
```

```python
import functools
import math

import jax
import jax.numpy as jnp
from jax import lax
from jax.experimental import pallas as pl
from jax.experimental.pallas import tpu as pltpu

F32 = jnp.float32
BF16 = jnp.bfloat16

D_MODEL = 1024
HEAD_DIM = 64
N_RET_HEADS = 8
N_SWA_HEADS = 8
N_SWA_KV = 2
SWA_GROUP = N_SWA_HEADS // N_SWA_KV
RET_WIDTH = N_RET_HEADS * HEAD_DIM
SWA_WIDTH = N_SWA_HEADS * HEAD_DIM
SWA_KV_WIDTH = N_SWA_KV * HEAD_DIM
MAIN_WIDTH = 4 * RET_WIDTH
IN_WIDTH = MAIN_WIDTH + SWA_WIDTH + 2 * SWA_KV_WIDTH
MIX_WIDTH = RET_WIDTH + SWA_WIDTH
D_FF = 4 * D_MODEL
WINDOW = 128
RET_CHUNK = 128
EPS = 1e-6
NEG_INF = -1e30

LANES = 128
N_PAIRS = N_RET_HEADS // 2
LOG_DECAY = [math.log(1.0 - 2.0 ** (-5.0 - h)) for h in range(N_RET_HEADS)]
ALIBI_SLOPES = [2.0 ** (-8.0 * (h + 1) / N_SWA_HEADS) for h in range(N_SWA_HEADS)]
K_SCALE = HEAD_DIM ** -0.5

ROW_TILE = 512
PROMPT_TILE = 512
DEC_GROUP = 16
VMEM_LIMIT = 56 * 1024 * 1024


def _dot(a, b):
    return jnp.dot(a, b, preferred_element_type=F32)


def _dot_nt(a, b):
    return lax.dot_general(a, b, (((1,), (1,)), ((), ())), preferred_element_type=F32)


def _iota(shape, dim):
    return lax.broadcasted_iota(jnp.int32, shape, dim)


def _ones_block_diag():
    same = (_iota((LANES, LANES), 0) >> 6) == (_iota((LANES, LANES), 1) >> 6)
    return jnp.where(same, 1.0, 0.0).astype(BF16)


def _head_sumsq(x, ones_bd):
    x2 = x * x
    hi = x2.astype(BF16)
    lo = (x2 - hi.astype(F32)).astype(BF16)
    return _dot(hi, ones_bd) + _dot(lo, ones_bd)


def _head_rms_scale(x, ones_bd):
    return lax.rsqrt(_head_sumsq(x, ones_bd) * (1.0 / HEAD_DIM) + EPS)


def _first_half():
    return _iota((1, LANES), 1) < HEAD_DIM


def _pair_const(values, pair, lane_is_second):
    return jnp.where(lane_is_second, values[2 * pair + 1], values[2 * pair]).astype(F32)


def _silu(g):
    return g * (1.0 / (1.0 + jnp.exp(-g)))


def _in_proj_kernel(x_ref, gain_ref, w_ref, qg_ref, kg_ref, main_ref, qn_ref, kn_ref, vs_ref):
    x = x_ref[...]
    ms = jnp.mean(x * x, axis=-1, keepdims=True)
    hb = ((x * lax.rsqrt(ms + EPS)) * gain_ref[...]).astype(BF16)
    ones_bd = _ones_block_diag()
    for c in range(MAIN_WIDTH // 512):
        main_ref[:, c * 512:(c + 1) * 512] = _dot(hb, w_ref[:, c * 512:(c + 1) * 512])
    qs = _dot(hb, w_ref[:, MAIN_WIDTH:MAIN_WIDTH + SWA_WIDTH])
    for c in range(SWA_WIDTH // LANES):
        xc = qs[:, c * LANES:(c + 1) * LANES]
        qn_ref[:, c * LANES:(c + 1) * LANES] = (xc * _head_rms_scale(xc, ones_bd)) * qg_ref[...]
    k0 = MAIN_WIDTH + SWA_WIDTH
    ks = _dot(hb, w_ref[:, k0:k0 + SWA_KV_WIDTH])
    kn_ref[...] = (ks * _head_rms_scale(ks, ones_bd)) * kg_ref[...]
    vs_ref[...] = _dot(hb, w_ref[:, k0 + SWA_KV_WIDTH:k0 + 2 * SWA_KV_WIDTH])


def _in_proj(x2d, gain, w_in_bf, qg, kg):
    m = x2d.shape[0]
    tm = min(ROW_TILE, m)
    row = lambda w: pl.BlockSpec((tm, w), lambda i: (i, 0))
    full = lambda a: pl.BlockSpec(a.shape, lambda i: (0, 0), pipeline_mode=pl.Buffered(1))
    return pl.pallas_call(
        _in_proj_kernel,
        grid=(m // tm,),
        in_specs=[row(D_MODEL), full(gain), full(w_in_bf), full(qg), full(kg)],
        out_specs=[row(MAIN_WIDTH), row(SWA_WIDTH), row(SWA_KV_WIDTH), row(SWA_KV_WIDTH)],
        out_shape=[jax.ShapeDtypeStruct((m, MAIN_WIDTH), F32),
                   jax.ShapeDtypeStruct((m, SWA_WIDTH), F32),
                   jax.ShapeDtypeStruct((m, SWA_KV_WIDTH), F32),
                   jax.ShapeDtypeStruct((m, SWA_KV_WIDTH), F32)],
        compiler_params=pltpu.CompilerParams(
            dimension_semantics=("arbitrary",), vmem_limit_bytes=VMEM_LIMIT),
        name="in_proj",
    )(x2d, gain, w_in_bf, qg, kg)


FF_CHUNK = 1024


def _out_mlp_kernel(mix_ref, x_ref, w_out_ref, gain_ref, w_up_ref, w_down_ref, y_ref):
    h = x_ref[...] + _dot(mix_ref[...], w_out_ref[...])
    ms = jnp.mean(h * h, axis=-1, keepdims=True)
    hf = ((h * lax.rsqrt(ms + EPS)) * gain_ref[...]).astype(BF16)
    ff = None
    for c in range(D_FF // FF_CHUNK):
        u = _dot(hf, w_up_ref[:, c * FF_CHUNK:(c + 1) * FF_CHUNK])
        a = jnp.maximum(u, 0.0)
        d = _dot((a * a).astype(BF16), w_down_ref[c * FF_CHUNK:(c + 1) * FF_CHUNK, :])
        ff = d if ff is None else ff + d
    y_ref[...] = h + ff


def _out_mlp(mix, x2d, w_out_bf, gain, w_up_bf, w_down_bf):
    m = x2d.shape[0]
    tm = min(ROW_TILE, m)
    row = lambda w: pl.BlockSpec((tm, w), lambda i: (i, 0))
    full = lambda a: pl.BlockSpec(a.shape, lambda i: (0, 0), pipeline_mode=pl.Buffered(1))
    return pl.pallas_call(
        _out_mlp_kernel,
        grid=(m // tm,),
        in_specs=[row(MIX_WIDTH), row(D_MODEL), full(w_out_bf), full(gain), full(w_up_bf),
                  full(w_down_bf)],
        out_specs=row(D_MODEL),
        out_shape=jax.ShapeDtypeStruct((m, D_MODEL), F32),
        compiler_params=pltpu.CompilerParams(
            dimension_semantics=("arbitrary",), vmem_limit_bytes=VMEM_LIMIT),
        name="out_mlp",
    )(mix, x2d, w_out_bf, gain, w_up_bf, w_down_bf)


def _split_pair_rows(x, first):
    return jnp.concatenate([jnp.where(first, x, 0.0), jnp.where(first, 0.0, x)], axis=0).astype(BF16)


def _retention_tile(q, k, v, intra, first):
    s = _dot_nt(q.astype(BF16), _split_pair_rows(k, first)) * intra
    return _dot(s.astype(BF16), _split_pair_rows(v, first))


def _softmax_sink_pv(s, sink_rows, v_cat):
    m = jnp.maximum(jnp.max(s, axis=-1, keepdims=True), sink_rows)
    p = jnp.exp(s - m)
    denom = jnp.sum(p, axis=-1, keepdims=True) + jnp.exp(sink_rows - m)
    return _dot(p.astype(BF16), v_cat) / denom


def _prompt_consts(intra_ref, qdec_ref, kdec_ref, sdec_ref, bias_ref):
    r = _iota((LANES, LANES), 0)
    lane2 = _iota((LANES, LANES), 1) >= HEAD_DIM
    rf = r.astype(F32)
    ri = _iota((LANES, 2 * LANES), 0)
    ci = _iota((LANES, 2 * LANES), 1)
    diff = (ri - (ci & (LANES - 1))).astype(F32)
    for p in range(N_PAIRS):
        lg = _pair_const(LOG_DECAY, p, lane2)
        qdec_ref[p] = jnp.exp(lg * (rf + 1.0))
        kdec_ref[p] = jnp.exp(lg * (RET_CHUNK - 1.0 - rf))
        sdec_ref[p] = jnp.exp(_pair_const(LOG_DECAY, p, r >= HEAD_DIM) * float(RET_CHUNK))
        lg2 = _pair_const(LOG_DECAY, p, ci >= LANES)
        intra_ref[p] = jnp.where(diff >= 0.0, jnp.exp(lg2 * jnp.maximum(diff, 0.0)), 0.0)
    rows = SWA_GROUP * WINDOW
    rb = _iota((rows, 2 * WINDOW), 0)
    cb = _iota((rows, 2 * WINDOW), 1)
    grp = rb >> 7
    dist = WINDOW + (rb & (WINDOW - 1)) - cb
    valid = (dist >= 0) & (dist < WINDOW)
    distf = dist.astype(F32)
    for j in range(N_SWA_KV):
        sl = [ALIBI_SLOPES[SWA_GROUP * j + g] for g in range(SWA_GROUP)]
        slope = jnp.where(grp == 0, sl[0], jnp.where(grp == 1, sl[1], jnp.where(grp == 2, sl[2], sl[3])))
        b = jnp.where(valid, -(slope.astype(F32) * distf), NEG_INF)
        bias_ref[0, j] = b
        bias_ref[1, j] = jnp.where(cb >= WINDOW, b, NEG_INF)


def _prompt_mixer_kernel(sinks_ref, main_ref, qn_ref, kn_ref, vs_ref,
                         mix_ref, ret_ref,
                         state_ref, prevk_ref, prevv_ref,
                         intra_ref, qdec_ref, kdec_ref, sdec_ref, bias_ref):
    t = pl.program_id(1)

    @pl.when((pl.program_id(0) == 0) & (t == 0))
    def _():
        _prompt_consts(intra_ref, qdec_ref, kdec_ref, sdec_ref, bias_ref)

    @pl.when(t == 0)
    def _():
        state_ref[...] = jnp.zeros_like(state_ref)
        prevk_ref[...] = jnp.zeros_like(prevk_ref)
        prevv_ref[...] = jnp.zeros_like(prevv_ref)

    first = _first_half()
    ones_bd = _ones_block_diag()
    bd_mask = (_iota((LANES, LANES), 0) >= HEAD_DIM) == (_iota((LANES, LANES), 1) >= HEAD_DIM)

    def chunk(c, carry):
        r0 = pl.multiple_of(c * RET_CHUNK, RET_CHUNK)
        rows = pl.ds(r0, RET_CHUNK)
        for p in range(N_PAIRS):
            cols = lambda base: slice(base + p * LANES, base + (p + 1) * LANES)
            q = main_ref[rows, cols(0)]
            k = main_ref[rows, cols(RET_WIDTH)] * K_SCALE
            v = main_ref[rows, cols(2 * RET_WIDTH)]
            g = main_ref[rows, cols(3 * RET_WIDTH)]
            state = state_ref[p]
            o = _retention_tile(q, k, v, intra_ref[p], first)
            o = o + _dot((q * qdec_ref[p]).astype(BF16), state.astype(BF16))
            kd_t = (k * kdec_ref[p]).T.astype(BF16)
            upd = _dot(kd_t, v.astype(BF16))
            state_ref[p] = state * sdec_ref[p] + jnp.where(bd_mask, upd, 0.0)
            on = o * _head_rms_scale(o, ones_bd)
            mix_ref[rows, p * LANES:(p + 1) * LANES] = (on * _silu(g)).astype(BF16)
        kc = kn_ref[rows, :]
        vc = vs_ref[rows, :]
        k_sw = pltpu.roll(kc, HEAD_DIM, axis=1)
        v_sw = pltpu.roll(vc, HEAD_DIM, axis=1)
        is_first = ((t == 0) & (c == 0)).astype(jnp.int32)
        for j in range(N_SWA_KV):
            if j == 0:
                k_dup = jnp.where(first, kc, k_sw).astype(BF16)
                v_dup = jnp.where(first, vc, v_sw).astype(BF16)
            else:
                k_dup = jnp.where(first, k_sw, kc).astype(BF16)
                v_dup = jnp.where(first, v_sw, vc).astype(BF16)
            k_cat = jnp.concatenate([prevk_ref[j], k_dup], axis=0)
            v_cat = jnp.concatenate([prevv_ref[j], v_dup], axis=0)
            pieces = []
            for g in range(SWA_GROUP):
                col = 2 * j + g // 2
                qc = qn_ref[rows, col * LANES:(col + 1) * LANES]
                pieces.append(jnp.where(first, qc, 0.0) if g % 2 == 0 else jnp.where(first, 0.0, qc))
            q_st = jnp.concatenate(pieces, axis=0).astype(BF16)
            s = _dot_nt(q_st, k_cat) * K_SCALE + bias_ref[is_first, j]
            sink_rows = jnp.concatenate(
                [jnp.full((WINDOW, 1), sinks_ref[SWA_GROUP * j + g], F32) for g in range(SWA_GROUP)],
                axis=0)
            o = _softmax_sink_pv(s, sink_rows, v_cat)
            for half in range(2):
                col = 2 * j + half
                oc = jnp.where(first, o[(2 * half) * WINDOW:(2 * half + 1) * WINDOW],
                               o[(2 * half + 1) * WINDOW:(2 * half + 2) * WINDOW])
                mix_ref[rows, RET_WIDTH + col * LANES:RET_WIDTH + (col + 1) * LANES] = oc.astype(BF16)
            prevk_ref[j] = k_dup
            prevv_ref[j] = v_dup
        return carry

    lax.fori_loop(0, PROMPT_TILE // RET_CHUNK, chunk, 0)

    @pl.when(t == pl.num_programs(1) - 1)
    def _():
        for p in range(N_PAIRS):
            s = state_ref[p]
            ret_ref[2 * p] = s[:HEAD_DIM, :HEAD_DIM]
            ret_ref[2 * p + 1] = s[HEAD_DIM:, HEAD_DIM:]


def _prompt_mixer(sinks, main, qn, kn, vs, batch, seq):
    nt = seq // PROMPT_TILE
    row = lambda w: pl.BlockSpec((PROMPT_TILE, w), lambda b, t: (b * nt + t, 0))
    return pl.pallas_call(
        _prompt_mixer_kernel,
        grid=(batch, nt),
        in_specs=[pl.BlockSpec(memory_space=pltpu.SMEM),
                  row(MAIN_WIDTH), row(SWA_WIDTH), row(SWA_KV_WIDTH), row(SWA_KV_WIDTH)],
        out_specs=[row(MIX_WIDTH),
                   pl.BlockSpec((None, N_RET_HEADS, HEAD_DIM, HEAD_DIM), lambda b, t: (b, 0, 0, 0))],
        out_shape=[jax.ShapeDtypeStruct((batch * seq, MIX_WIDTH), BF16),
                   jax.ShapeDtypeStruct((batch, N_RET_HEADS, HEAD_DIM, HEAD_DIM), F32)],
        scratch_shapes=[
            pltpu.VMEM((N_PAIRS, LANES, LANES), F32),
            pltpu.VMEM((N_SWA_KV, WINDOW, LANES), BF16),
            pltpu.VMEM((N_SWA_KV, WINDOW, LANES), BF16),
            pltpu.VMEM((N_PAIRS, LANES, 2 * LANES), F32),
            pltpu.VMEM((N_PAIRS, LANES, LANES), F32),
            pltpu.VMEM((N_PAIRS, LANES, LANES), F32),
            pltpu.VMEM((N_PAIRS, LANES, LANES), F32),
            pltpu.VMEM((2, N_SWA_KV, SWA_GROUP * WINDOW, 2 * WINDOW), F32),
        ],
        compiler_params=pltpu.CompilerParams(
            dimension_semantics=("arbitrary", "arbitrary"), vmem_limit_bytes=VMEM_LIMIT),
        name="prompt_mixer",
    )(sinks, main, qn, kn, vs)


DEC_ROWS = 128


def _decode_consts(dec_seq, intra_ref, qdec_ref, kdec_ref, bias_ref):
    shift = dec_seq.bit_length() - 1
    r = _iota((LANES, LANES), 0)
    lane2 = _iota((LANES, LANES), 1) >= HEAD_DIM
    lf = (r & (dec_seq - 1)).astype(F32)
    ri = _iota((LANES, 2 * LANES), 0)
    ci = _iota((LANES, 2 * LANES), 1) & (LANES - 1)
    same = (ri >> shift) == (ci >> shift)
    diff = ((ri & (dec_seq - 1)) - (ci & (dec_seq - 1))).astype(F32)
    for p in range(N_PAIRS):
        lg = _pair_const(LOG_DECAY, p, lane2)
        qdec_ref[p] = jnp.exp(lg * (lf + 1.0))
        kdec_ref[p] = jnp.exp(lg * (dec_seq - 1.0 - lf))
        lg2 = _pair_const(LOG_DECAY, p, _iota((LANES, 2 * LANES), 1) >= LANES)
        intra_ref[p] = jnp.where(same & (diff >= 0.0), jnp.exp(lg2 * jnp.maximum(diff, 0.0)), 0.0)
    rows = N_SWA_HEADS * dec_seq
    rb = _iota((rows, 2 * WINDOW), 0)
    cb = _iota((rows, 2 * WINDOW), 1)
    head = rb >> shift
    dist = WINDOW + (rb & (dec_seq - 1)) - cb
    valid = (dist >= 0) & (dist < WINDOW) & (cb < WINDOW + dec_seq)
    slope = jnp.zeros((rows, 2 * WINDOW), F32)
    for h in range(N_SWA_HEADS):
        slope = jnp.where(head == h, ALIBI_SLOPES[h], slope)
    bias_ref[...] = jnp.where(valid, -(slope * dist.astype(F32)), NEG_INF)


def _decode_mixer_kernel(dec_seq, sinks_ref, main_ref, qn_ref, kn_ref, vs_ref, state_ref, kbuf_ref, vbuf_ref,
                         mix_ref, state_out_ref, kout_ref, vout_ref,
                         intra_ref, qdec_ref, kdec_ref, bias_ref,
                         knew_ref, vnew_ref, qd_ref, kdt_ref, vmat_ref, cross_ref, qbd_ref, oblk_ref):
    @pl.when(pl.program_id(0) == 0)
    def _():
        _decode_consts(dec_seq, intra_ref, qdec_ref, kdec_ref, bias_ref)
        knew_ref[...] = jnp.zeros_like(knew_ref)
        vnew_ref[...] = jnp.zeros_like(vnew_ref)

    first = _first_half()
    ones_bd = _ones_block_diag()
    shift = dec_seq.bit_length() - 1
    n_heads_rows = N_SWA_HEADS * dec_seq

    knew_ref[0:DEC_ROWS, :] = kn_ref[...]
    vnew_ref[0:DEC_ROWS, :] = vs_ref[...]
    for p in range(N_PAIRS):
        cols = lambda base: slice(base + p * LANES, base + (p + 1) * LANES)
        q = main_ref[:, cols(0)]
        k = main_ref[:, cols(RET_WIDTH)] * K_SCALE
        v = main_ref[:, cols(2 * RET_WIDTH)]
        qd_ref[p] = q * qdec_ref[p]
        kdt_ref[p] = (k * kdec_ref[p]).T.astype(BF16)
        vmat_ref[0, p] = v
        vmat_ref[1, p] = pltpu.roll(v, HEAD_DIM, axis=1)
    qn = qn_ref[...]
    qn_sw = pltpu.roll(qn, HEAD_DIM, axis=1)
    for h in range(N_SWA_HEADS):
        kv_half = h // SWA_GROUP
        if (h % 2) == kv_half:
            src = qn[:, (h // 2) * LANES:(h // 2 + 1) * LANES]
        else:
            col = (h + 1) // 2
            src = qn_sw[:, col * LANES:(col + 1) * LANES]
        qbd_ref[h] = jnp.where(first, src, 0.0) if kv_half == 0 else jnp.where(first, 0.0, src)

    sink_rows = jnp.concatenate(
        [jnp.full((dec_seq, 1), sinks_ref[h], F32) for h in range(N_SWA_HEADS)], axis=0)
    row_batch = _iota((DEC_ROWS, LANES), 0) >> shift
    s_decay = [math.exp(LOG_DECAY[h] * dec_seq) for h in range(N_RET_HEADS)]

    def per_batch(b, carry):
        r0 = pl.multiple_of(b * dec_seq, dec_seq)
        rows = pl.ds(r0, dec_seq)
        in_batch = row_batch == b
        for p in range(N_PAIRS):
            st = state_ref[b, p * LANES:(p + 1) * LANES, :]
            qd = qd_ref[p, rows, :]
            res = _dot(_split_pair_rows(qd, first), st.astype(BF16))
            cross_ref[p, rows, 0:HEAD_DIM] = res[:dec_seq]
            cross_ref[p, rows, HEAD_DIM:LANES] = res[dec_seq:]
            kdt = kdt_ref[p]
            for half in range(2):
                vm = jnp.where(in_batch, vmat_ref[half, p], 0.0).astype(BF16)
                upd = _dot(kdt[half * HEAD_DIM:(half + 1) * HEAD_DIM, :], vm[:, :HEAD_DIM])
                lo = p * LANES + half * HEAD_DIM
                state_out_ref[b, lo:lo + HEAD_DIM, :] = (
                    st[half * HEAD_DIM:(half + 1) * HEAD_DIM, :] * s_decay[2 * p + half] + upd)
        k_cat = jnp.concatenate([kbuf_ref[b], knew_ref[pl.ds(r0, WINDOW), :]], axis=0).astype(BF16)
        v_cat = jnp.concatenate([vbuf_ref[b], vnew_ref[pl.ds(r0, WINDOW), :]], axis=0).astype(BF16)
        q_st = jnp.concatenate([qbd_ref[h, rows, :] for h in range(N_SWA_HEADS)], axis=0).astype(BF16)
        s = _dot_nt(q_st, k_cat) * K_SCALE + bias_ref[...]
        o = _softmax_sink_pv(s, sink_rows, v_cat)
        for h in range(N_SWA_HEADS):
            oblk_ref[h, rows, :] = o[h * dec_seq:(h + 1) * dec_seq]
        kout_ref[b, 0:WINDOW - dec_seq, :] = kbuf_ref[b, dec_seq:WINDOW, :]
        kout_ref[b, WINDOW - dec_seq:WINDOW, :] = kn_ref[rows, :]
        vout_ref[b, 0:WINDOW - dec_seq, :] = vbuf_ref[b, dec_seq:WINDOW, :]
        vout_ref[b, WINDOW - dec_seq:WINDOW, :] = vs_ref[rows, :]
        return carry

    lax.fori_loop(0, DEC_GROUP, per_batch, 0)

    for p in range(N_PAIRS):
        cols = lambda base: slice(base + p * LANES, base + (p + 1) * LANES)
        q = main_ref[:, cols(0)]
        k = main_ref[:, cols(RET_WIDTH)] * K_SCALE
        v = main_ref[:, cols(2 * RET_WIDTH)]
        g = main_ref[:, cols(3 * RET_WIDTH)]
        o = _retention_tile(q, k, v, intra_ref[p], first) + cross_ref[p]
        on = o * _head_rms_scale(o, ones_bd)
        mix_ref[:, p * LANES:(p + 1) * LANES] = (on * _silu(g)).astype(BF16)
    y1 = jnp.where(first, oblk_ref[3], oblk_ref[4])
    moved = pltpu.roll(jnp.concatenate([oblk_ref[1], y1, oblk_ref[6], oblk_ref[6]], axis=1), HEAD_DIM, axis=1)
    outs = [
        jnp.where(first, oblk_ref[0], moved[:, 0:LANES]),
        jnp.where(first, oblk_ref[2], moved[:, LANES:2 * LANES]),
        jnp.where(first, moved[:, 2 * LANES:3 * LANES], oblk_ref[5]),
        jnp.where(first, moved[:, 3 * LANES:4 * LANES], oblk_ref[7]),
    ]
    for c in range(SWA_WIDTH // LANES):
        mix_ref[:, RET_WIDTH + c * LANES:RET_WIDTH + (c + 1) * LANES] = outs[c].astype(BF16)


def _decode_mixer(sinks, main, qn, kn, vs, state, kbuf, vbuf, dec_seq):
    nb = state.shape[0]
    assert DEC_GROUP * dec_seq == DEC_ROWS and nb % DEC_GROUP == 0 and dec_seq & (dec_seq - 1) == 0
    row = lambda w: pl.BlockSpec((DEC_ROWS, w), lambda i: (i, 0))
    per_b = lambda a: pl.BlockSpec((DEC_GROUP,) + a.shape[1:], lambda i: (i, 0, 0))
    return pl.pallas_call(
        functools.partial(_decode_mixer_kernel, dec_seq),
        grid=(nb // DEC_GROUP,),
        in_specs=[pl.BlockSpec(memory_space=pltpu.SMEM),
                  row(MAIN_WIDTH), row(SWA_WIDTH), row(SWA_KV_WIDTH), row(SWA_KV_WIDTH),
                  per_b(state), per_b(kbuf), per_b(vbuf)],
        out_specs=[row(MIX_WIDTH), per_b(state), per_b(kbuf), per_b(vbuf)],
        out_shape=[jax.ShapeDtypeStruct((nb * dec_seq, MIX_WIDTH), BF16),
                   jax.ShapeDtypeStruct(state.shape, F32),
                   jax.ShapeDtypeStruct(kbuf.shape, F32),
                   jax.ShapeDtypeStruct(vbuf.shape, F32)],
        scratch_shapes=[
            pltpu.VMEM((N_PAIRS, LANES, 2 * LANES), F32),
            pltpu.VMEM((N_PAIRS, LANES, LANES), F32),
            pltpu.VMEM((N_PAIRS, LANES, LANES), F32),
            pltpu.VMEM((N_SWA_HEADS * dec_seq, 2 * WINDOW), F32),
            pltpu.VMEM((2 * DEC_ROWS, LANES), F32),
            pltpu.VMEM((2 * DEC_ROWS, LANES), F32),
            pltpu.VMEM((N_PAIRS, DEC_ROWS, LANES), F32),
            pltpu.VMEM((N_PAIRS, LANES, DEC_ROWS), BF16),
            pltpu.VMEM((2, N_PAIRS, DEC_ROWS, LANES), F32),
            pltpu.VMEM((N_PAIRS, DEC_ROWS, LANES), F32),
            pltpu.VMEM((N_SWA_HEADS, DEC_ROWS, LANES), F32),
            pltpu.VMEM((N_SWA_HEADS, DEC_ROWS, LANES), F32),
        ],
        compiler_params=pltpu.CompilerParams(
            dimension_semantics=("arbitrary",), vmem_limit_bytes=VMEM_LIMIT),
        name="decode_mixer",
    )(sinks, main, qn, kn, vs, state, kbuf, vbuf)


def kernel(x_prompt, x_sample, state_ret, cache_swa_k, cache_swa_v, norm_mix_gain, w_in, q_norm_gain,
           k_norm_gain, attn_sinks, w_out, norm_ffn_gain, w_up, w_down):
    batch, seq, d = x_prompt.shape
    nb, dec_seq, _ = x_sample.shape
    wb = cache_swa_k.shape[1]
    assert d == D_MODEL and seq % PROMPT_TILE == 0 and wb == WINDOW

    w_in_bf = w_in.astype(BF16)
    w_out_bf = w_out.astype(BF16)
    w_up_bf = w_up.astype(BF16)
    w_down_bf = w_down.astype(BF16)
    gain_mix = norm_mix_gain.reshape(1, D_MODEL)
    gain_ffn = norm_ffn_gain.reshape(1, D_MODEL)
    qg = jnp.tile(q_norm_gain, 2).reshape(1, LANES)
    kg = jnp.tile(k_norm_gain, 2).reshape(1, LANES)

    xp = x_prompt.reshape(batch * seq, D_MODEL)
    main_p, qn_p, kn_p, vs_p = _in_proj(xp, gain_mix, w_in_bf, qg, kg)
    mix_p, ret_p = _prompt_mixer(attn_sinks, main_p, qn_p, kn_p, vs_p, batch, seq)
    y_p = _out_mlp(mix_p, xp, w_out_bf, gain_ffn, w_up_bf, w_down_bf).reshape(batch, seq, D_MODEL)
    k_p = kn_p.reshape(batch, seq, N_SWA_KV, HEAD_DIM)[:, -WINDOW:]
    v_p = vs_p.reshape(batch, seq, N_SWA_KV, HEAD_DIM)[:, -WINDOW:]

    xs = x_sample.reshape(nb * dec_seq, D_MODEL)
    main_s, qn_s, kn_s, vs_s = _in_proj(xs, gain_mix, w_in_bf, qg, kg)
    mix_s, ret_s, k_s, v_s = _decode_mixer(
        attn_sinks, main_s, qn_s, kn_s, vs_s,
        state_ret.reshape(nb, N_RET_HEADS * HEAD_DIM, HEAD_DIM),
        cache_swa_k.reshape(nb, wb, SWA_KV_WIDTH), cache_swa_v.reshape(nb, wb, SWA_KV_WIDTH), dec_seq)
    y_s = _out_mlp(mix_s, xs, w_out_bf, gain_ffn, w_up_bf, w_down_bf).reshape(nb, dec_seq, D_MODEL)

    return (y_p, y_s, ret_p, k_p, v_p,
            ret_s.reshape(nb, N_RET_HEADS, HEAD_DIM, HEAD_DIM),
            k_s.reshape(nb, wb, N_SWA_KV, HEAD_DIM), v_s.reshape(nb, wb, N_SWA_KV, HEAD_DIM))
```

```python
import functools
import math

import jax
import jax.numpy as jnp
from jax import lax
from jax.experimental import pallas as pl
from jax.experimental.pallas import tpu as pltpu

F32 = jnp.float32
BF16 = jnp.bfloat16

D_MODEL = 1024
HEAD_DIM = 64
N_RET_HEADS = 8
N_SWA_HEADS = 8
N_SWA_KV = 2
SWA_GROUP = N_SWA_HEADS // N_SWA_KV
RET_WIDTH = N_RET_HEADS * HEAD_DIM
SWA_WIDTH = N_SWA_HEADS * HEAD_DIM
SWA_KV_WIDTH = N_SWA_KV * HEAD_DIM
MAIN_WIDTH = 4 * RET_WIDTH
IN_WIDTH = MAIN_WIDTH + SWA_WIDTH + 2 * SWA_KV_WIDTH
MIX_WIDTH = RET_WIDTH + SWA_WIDTH
D_FF = 4 * D_MODEL
WINDOW = 128
RET_CHUNK = 128
EPS = 1e-6
NEG_INF = -1e30

LANES = 128
N_PAIRS = N_RET_HEADS // 2
LOG_DECAY = [math.log(1.0 - 2.0 ** (-5.0 - h)) for h in range(N_RET_HEADS)]
ALIBI_SLOPES = [2.0 ** (-8.0 * (h + 1) / N_SWA_HEADS) for h in range(N_SWA_HEADS)]
K_SCALE = HEAD_DIM ** -0.5

ROW_TILE = 512
PROMPT_TILE = 512
DEC_GROUP = 16
VMEM_LIMIT = 56 * 1024 * 1024


def _dot(a, b):
    return jnp.dot(a, b, preferred_element_type=F32)


def _dot_nt(a, b):
    return lax.dot_general(a, b, (((1,), (1,)), ((), ())), preferred_element_type=F32)


def _iota(shape, dim):
    return lax.broadcasted_iota(jnp.int32, shape, dim)


def _ones_block_diag():
    same = (_iota((LANES, LANES), 0) >> 6) == (_iota((LANES, LANES), 1) >> 6)
    return jnp.where(same, 1.0, 0.0).astype(BF16)


def _head_sumsq(x, ones_bd):
    x2 = x * x
    hi = x2.astype(BF16)
    lo = (x2 - hi.astype(F32)).astype(BF16)
    return _dot(hi, ones_bd) + _dot(lo, ones_bd)


def _head_rms_scale(x, ones_bd):
    return lax.rsqrt(_head_sumsq(x, ones_bd) * (1.0 / HEAD_DIM) + EPS)


def _first_half():
    return _iota((1, LANES), 1) < HEAD_DIM


def _pair_const(values, pair, lane_is_second):
    return jnp.where(lane_is_second, values[2 * pair + 1], values[2 * pair]).astype(F32)


def _silu(g):
    return g * (1.0 / (1.0 + jnp.exp(-g)))


def _in_proj_kernel(x_ref, gain_ref, w_ref, qg_ref, kg_ref, main_ref, qn_ref, kn_ref, vs_ref):
    x = x_ref[...]
    ms = jnp.mean(x * x, axis=-1, keepdims=True)
    hb = ((x * lax.rsqrt(ms + EPS)) * gain_ref[...]).astype(BF16)
    ones_bd = _ones_block_diag()
    for c in range(MAIN_WIDTH // 512):
        main_ref[:, c * 512:(c + 1) * 512] = _dot(hb, w_ref[:, c * 512:(c + 1) * 512])
    qs = _dot(hb, w_ref[:, MAIN_WIDTH:MAIN_WIDTH + SWA_WIDTH])
    for c in range(SWA_WIDTH // LANES):
        xc = qs[:, c * LANES:(c + 1) * LANES]
        qn_ref[:, c * LANES:(c + 1) * LANES] = (xc * _head_rms_scale(xc, ones_bd)) * qg_ref[...]
    k0 = MAIN_WIDTH + SWA_WIDTH
    ks = _dot(hb, w_ref[:, k0:k0 + SWA_KV_WIDTH])
    kn_ref[...] = (ks * _head_rms_scale(ks, ones_bd)) * kg_ref[...]
    vs_ref[...] = _dot(hb, w_ref[:, k0 + SWA_KV_WIDTH:k0 + 2 * SWA_KV_WIDTH])


def _in_proj(x2d, gain, w_in_bf, qg, kg):
    m = x2d.shape[0]
    tm = min(ROW_TILE, m)
    row = lambda w: pl.BlockSpec((tm, w), lambda i: (i, 0))
    full = lambda a: pl.BlockSpec(a.shape, lambda i: (0, 0), pipeline_mode=pl.Buffered(1))
    return pl.pallas_call(
        _in_proj_kernel,
        grid=(m // tm,),
        in_specs=[row(D_MODEL), full(gain), full(w_in_bf), full(qg), full(kg)],
        out_specs=[row(MAIN_WIDTH), row(SWA_WIDTH), row(SWA_KV_WIDTH), row(SWA_KV_WIDTH)],
        out_shape=[jax.ShapeDtypeStruct((m, MAIN_WIDTH), F32),
                   jax.ShapeDtypeStruct((m, SWA_WIDTH), F32),
                   jax.ShapeDtypeStruct((m, SWA_KV_WIDTH), F32),
                   jax.ShapeDtypeStruct((m, SWA_KV_WIDTH), F32)],
        compiler_params=pltpu.CompilerParams(
            dimension_semantics=("arbitrary",), vmem_limit_bytes=VMEM_LIMIT),
        name="in_proj",
    )(x2d, gain, w_in_bf, qg, kg)


FF_CHUNK = 1024


def _out_mlp_kernel(mix_ref, x_ref, w_out_ref, gain_ref, w_up_ref, w_down_ref, y_ref):
    h = x_ref[...] + _dot(mix_ref[...], w_out_ref[...])
    ms = jnp.mean(h * h, axis=-1, keepdims=True)
    hf = ((h * lax.rsqrt(ms + EPS)) * gain_ref[...]).astype(BF16)
    ff = None
    for c in range(D_FF // FF_CHUNK):
        u = _dot(hf, w_up_ref[:, c * FF_CHUNK:(c + 1) * FF_CHUNK])
        a = jnp.maximum(u, 0.0)
        d = _dot((a * a).astype(BF16), w_down_ref[c * FF_CHUNK:(c + 1) * FF_CHUNK, :])
        ff = d if ff is None else ff + d
    y_ref[...] = h + ff


def _out_mlp(mix, x2d, w_out_bf, gain, w_up_bf, w_down_bf):
    m = x2d.shape[0]
    tm = min(ROW_TILE, m)
    row = lambda w: pl.BlockSpec((tm, w), lambda i: (i, 0))
    full = lambda a: pl.BlockSpec(a.shape, lambda i: (0, 0), pipeline_mode=pl.Buffered(1))
    return pl.pallas_call(
        _out_mlp_kernel,
        grid=(m // tm,),
        in_specs=[row(MIX_WIDTH), row(D_MODEL), full(w_out_bf), full(gain), full(w_up_bf),
                  full(w_down_bf)],
        out_specs=row(D_MODEL),
        out_shape=jax.ShapeDtypeStruct((m, D_MODEL), F32),
        compiler_params=pltpu.CompilerParams(
            dimension_semantics=("arbitrary",), vmem_limit_bytes=VMEM_LIMIT),
        name="out_mlp",
    )(mix, x2d, w_out_bf, gain, w_up_bf, w_down_bf)


def _split_pair_rows(x, first):
    return jnp.concatenate([jnp.where(first, x, 0.0), jnp.where(first, 0.0, x)], axis=0).astype(BF16)


def _retention_tile(q, k, v, intra, first):
    s = _dot_nt(q.astype(BF16), _split_pair_rows(k, first)) * intra
    return _dot(s.astype(BF16), _split_pair_rows(v, first))


def _softmax_sink_pv(s, sink_wide, v_cat):
    m = jnp.maximum(jnp.max(s, axis=-1, keepdims=True), sink_wide)
    p = jnp.exp(s - jnp.concatenate([m, m], axis=1))
    denom = jnp.sum(p, axis=-1, keepdims=True) + jnp.exp(sink_wide - m)
    return _dot(p.astype(BF16), v_cat) / denom


def _softmax_sink_pv_t(s_t, sink_lanes, v_t):
    m = jnp.maximum(jnp.max(s_t, axis=0, keepdims=True), sink_lanes)
    p = jnp.exp(s_t - m)
    denom = jnp.sum(p, axis=0, keepdims=True) + jnp.exp(sink_lanes - m)
    return _dot(v_t, p.astype(BF16)) / denom


def _prompt_consts(intra_ref, qdec_ref, kdec_ref, sdec_ref, bias_ref):
    r = _iota((LANES, LANES), 0)
    lane2 = _iota((LANES, LANES), 1) >= HEAD_DIM
    rf = r.astype(F32)
    ri = _iota((LANES, 2 * LANES), 0)
    ci = _iota((LANES, 2 * LANES), 1)
    diff = (ri - (ci & (LANES - 1))).astype(F32)
    for p in range(N_PAIRS):
        lg = _pair_const(LOG_DECAY, p, lane2)
        qdec_ref[p] = jnp.exp(lg * (rf + 1.0))
        kdec_ref[p] = jnp.exp(lg * (RET_CHUNK - 1.0 - rf))
        sdec_ref[p] = jnp.exp(_pair_const(LOG_DECAY, p, r >= HEAD_DIM) * float(RET_CHUNK))
        lg2 = _pair_const(LOG_DECAY, p, ci >= LANES)
        intra_ref[p] = jnp.where(diff >= 0.0, jnp.exp(lg2 * jnp.maximum(diff, 0.0)), 0.0)
    cols = SWA_GROUP * WINDOW
    kb = _iota((2 * WINDOW, cols), 0)
    cb = _iota((2 * WINDOW, cols), 1)
    grp = cb >> 7
    dist = WINDOW + (cb & (WINDOW - 1)) - kb
    valid = (dist >= 0) & (dist < WINDOW)
    distf = dist.astype(F32)
    for j in range(N_SWA_KV):
        sl = [ALIBI_SLOPES[SWA_GROUP * j + g] for g in range(SWA_GROUP)]
        slope = jnp.where(grp == 0, sl[0], jnp.where(grp == 1, sl[1], jnp.where(grp == 2, sl[2], sl[3])))
        b = jnp.where(valid, -(slope.astype(F32) * distf), NEG_INF)
        bias_ref[0, j] = b
        bias_ref[1, j] = jnp.where(kb >= WINDOW, b, NEG_INF)


def _prompt_mixer_kernel(sinks_ref, main_ref, qn_ref, kn_ref, vs_ref,
                         mix_ref, ret_ref,
                         state_ref, prevk_ref, prevv_ref,
                         intra_ref, qdec_ref, kdec_ref, sdec_ref, bias_ref):
    t = pl.program_id(1)

    @pl.when((pl.program_id(0) == 0) & (t == 0))
    def _():
        _prompt_consts(intra_ref, qdec_ref, kdec_ref, sdec_ref, bias_ref)

    @pl.when(t == 0)
    def _():
        state_ref[...] = jnp.zeros_like(state_ref)
        prevk_ref[...] = jnp.zeros_like(prevk_ref)
        prevv_ref[...] = jnp.zeros_like(prevv_ref)

    first = _first_half()
    ones_bd = _ones_block_diag()
    bd_mask = (_iota((LANES, LANES), 0) >= HEAD_DIM) == (_iota((LANES, LANES), 1) >= HEAD_DIM)

    def chunk(c, carry):
        r0 = pl.multiple_of(c * RET_CHUNK, RET_CHUNK)
        rows = pl.ds(r0, RET_CHUNK)
        for p in range(N_PAIRS):
            cols = lambda base: slice(base + p * LANES, base + (p + 1) * LANES)
            q = main_ref[rows, cols(0)]
            k = main_ref[rows, cols(RET_WIDTH)] * K_SCALE
            v = main_ref[rows, cols(2 * RET_WIDTH)]
            g = main_ref[rows, cols(3 * RET_WIDTH)]
            state = state_ref[p]
            o = _retention_tile(q, k, v, intra_ref[p], first)
            o = o + _dot((q * qdec_ref[p]).astype(BF16), state.astype(BF16))
            kd_t = (k * kdec_ref[p]).T.astype(BF16)
            upd = _dot(kd_t, v.astype(BF16))
            state_ref[p] = state * sdec_ref[p] + jnp.where(bd_mask, upd, 0.0)
            on = o * _head_rms_scale(o, ones_bd)
            mix_ref[rows, p * LANES:(p + 1) * LANES] = (on * _silu(g)).astype(BF16)
        kc = kn_ref[rows, :]
        k_sw = pltpu.roll(kc, HEAD_DIM, axis=1)
        v_t = vs_ref[rows, :].T
        is_first = ((t == 0) & (c == 0)).astype(jnp.int32)
        for j in range(N_SWA_KV):
            k_dup = (jnp.where(first, kc, k_sw) if j == 0 else jnp.where(first, k_sw, kc)).astype(BF16)
            v_tj = v_t[j * HEAD_DIM:(j + 1) * HEAD_DIM].astype(BF16)
            k_cat = jnp.concatenate([prevk_ref[j], k_dup], axis=0)
            v_cat = jnp.concatenate([prevv_ref[j], v_tj], axis=1)
            pieces = []
            for g in range(SWA_GROUP):
                col = 2 * j + g // 2
                qc = qn_ref[rows, col * LANES:(col + 1) * LANES]
                pieces.append(jnp.where(first, qc, 0.0) if g % 2 == 0 else jnp.where(first, 0.0, qc))
            q_st = jnp.concatenate(pieces, axis=0).astype(BF16)
            s_t = _dot_nt(k_cat, q_st) * K_SCALE + bias_ref[is_first, j]
            sink_lanes = jnp.concatenate(
                [jnp.full((1, WINDOW), sinks_ref[SWA_GROUP * j + g], F32) for g in range(SWA_GROUP)],
                axis=1)
            o_t = _softmax_sink_pv_t(s_t, sink_lanes, v_cat)
            for half in range(2):
                col = 2 * j + half
                pair_t = jnp.concatenate([o_t[:, (2 * half) * WINDOW:(2 * half + 1) * WINDOW],
                                          o_t[:, (2 * half + 1) * WINDOW:(2 * half + 2) * WINDOW]], axis=0)
                mix_ref[rows, RET_WIDTH + col * LANES:RET_WIDTH + (col + 1) * LANES] = pair_t.T.astype(BF16)
            prevk_ref[j] = k_dup
            prevv_ref[j] = v_tj
        return carry

    lax.fori_loop(0, PROMPT_TILE // RET_CHUNK, chunk, 0)

    @pl.when(t == pl.num_programs(1) - 1)
    def _():
        for p in range(N_PAIRS):
            s = state_ref[p]
            ret_ref[2 * p] = s[:HEAD_DIM, :HEAD_DIM]
            ret_ref[2 * p + 1] = s[HEAD_DIM:, HEAD_DIM:]


def _prompt_mixer(sinks, main, qn, kn, vs, batch, seq):
    nt = seq // PROMPT_TILE
    row = lambda w: pl.BlockSpec((PROMPT_TILE, w), lambda b, t: (b * nt + t, 0))
    return pl.pallas_call(
        _prompt_mixer_kernel,
        grid=(batch, nt),
        in_specs=[pl.BlockSpec(memory_space=pltpu.SMEM),
                  row(MAIN_WIDTH), row(SWA_WIDTH), row(SWA_KV_WIDTH), row(SWA_KV_WIDTH)],
        out_specs=[row(MIX_WIDTH),
                   pl.BlockSpec((None, N_RET_HEADS, HEAD_DIM, HEAD_DIM), lambda b, t: (b, 0, 0, 0))],
        out_shape=[jax.ShapeDtypeStruct((batch * seq, MIX_WIDTH), BF16),
                   jax.ShapeDtypeStruct((batch, N_RET_HEADS, HEAD_DIM, HEAD_DIM), F32)],
        scratch_shapes=[
            pltpu.VMEM((N_PAIRS, LANES, LANES), F32),
            pltpu.VMEM((N_SWA_KV, WINDOW, LANES), BF16),
            pltpu.VMEM((N_SWA_KV, HEAD_DIM, WINDOW), BF16),
            pltpu.VMEM((N_PAIRS, LANES, 2 * LANES), F32),
            pltpu.VMEM((N_PAIRS, LANES, LANES), F32),
            pltpu.VMEM((N_PAIRS, LANES, LANES), F32),
            pltpu.VMEM((N_PAIRS, LANES, LANES), F32),
            pltpu.VMEM((2, N_SWA_KV, 2 * WINDOW, SWA_GROUP * WINDOW), F32),
        ],
        compiler_params=pltpu.CompilerParams(
            dimension_semantics=("arbitrary", "arbitrary"), vmem_limit_bytes=VMEM_LIMIT),
        name="prompt_mixer",
    )(sinks, main, qn, kn, vs)


DEC_ROWS = 128


def _decode_consts(dec_seq, intra_ref, qdec_ref, kdec_ref, bias_ref):
    shift = dec_seq.bit_length() - 1
    r = _iota((LANES, LANES), 0)
    lane2 = _iota((LANES, LANES), 1) >= HEAD_DIM
    lf = (r & (dec_seq - 1)).astype(F32)
    ri = _iota((LANES, 2 * LANES), 0)
    ci = _iota((LANES, 2 * LANES), 1) & (LANES - 1)
    same = (ri >> shift) == (ci >> shift)
    diff = ((ri & (dec_seq - 1)) - (ci & (dec_seq - 1))).astype(F32)
    for p in range(N_PAIRS):
        lg = _pair_const(LOG_DECAY, p, lane2)
        qdec_ref[p] = jnp.exp(lg * (lf + 1.0))
        kdec_ref[p] = jnp.exp(lg * (dec_seq - 1.0 - lf))
        lg2 = _pair_const(LOG_DECAY, p, _iota((LANES, 2 * LANES), 1) >= LANES)
        intra_ref[p] = jnp.where(same & (diff >= 0.0), jnp.exp(lg2 * jnp.maximum(diff, 0.0)), 0.0)
    rows = N_SWA_HEADS * dec_seq
    rb = _iota((rows, 2 * WINDOW), 0)
    cb = _iota((rows, 2 * WINDOW), 1)
    head = rb >> shift
    dist = WINDOW + (rb & (dec_seq - 1)) - cb
    valid = (dist >= 0) & (dist < WINDOW) & (cb < WINDOW + dec_seq)
    slope = jnp.zeros((rows, 2 * WINDOW), F32)
    for h in range(N_SWA_HEADS):
        slope = jnp.where(head == h, ALIBI_SLOPES[h], slope)
    bias_ref[...] = jnp.where(valid, -(slope * dist.astype(F32)), NEG_INF)


def _decode_mixer_kernel(dec_seq, sinks_ref, main_ref, qn_ref, kn_ref, vs_ref, state_ref, kbuf_ref, vbuf_ref,
                         mix_ref, state_out_ref, kout_ref, vout_ref,
                         intra_ref, qdec_ref, kdec_ref, bias_ref,
                         knew_ref, vnew_ref, qd_ref, kdt_ref, vmat_ref, cross_ref, qbd_ref, oblk_ref):
    @pl.when(pl.program_id(0) == 0)
    def _():
        _decode_consts(dec_seq, intra_ref, qdec_ref, kdec_ref, bias_ref)
        knew_ref[...] = jnp.zeros_like(knew_ref)
        vnew_ref[...] = jnp.zeros_like(vnew_ref)

    first = _first_half()
    ones_bd = _ones_block_diag()
    shift = dec_seq.bit_length() - 1
    n_heads_rows = N_SWA_HEADS * dec_seq

    knew_ref[0:DEC_ROWS, :] = kn_ref[...]
    vnew_ref[0:DEC_ROWS, :] = vs_ref[...]
    for p in range(N_PAIRS):
        cols = lambda base: slice(base + p * LANES, base + (p + 1) * LANES)
        q = main_ref[:, cols(0)]
        k = main_ref[:, cols(RET_WIDTH)] * K_SCALE
        v = main_ref[:, cols(2 * RET_WIDTH)]
        qd_ref[p] = q * qdec_ref[p]
        kdt_ref[p] = (k * kdec_ref[p]).T.astype(BF16)
        vmat_ref[0, p] = v
        vmat_ref[1, p] = pltpu.roll(v, HEAD_DIM, axis=1)
    qn = qn_ref[...]
    qn_sw = pltpu.roll(qn, HEAD_DIM, axis=1)
    for h in range(N_SWA_HEADS):
        kv_half = h // SWA_GROUP
        if (h % 2) == kv_half:
            src = qn[:, (h // 2) * LANES:(h // 2 + 1) * LANES]
        else:
            col = (h + 1) // 2
            src = qn_sw[:, col * LANES:(col + 1) * LANES]
        qbd_ref[h] = jnp.where(first, src, 0.0) if kv_half == 0 else jnp.where(first, 0.0, src)

    sink_rows = jnp.concatenate(
        [jnp.full((dec_seq, LANES), sinks_ref[h], F32) for h in range(N_SWA_HEADS)], axis=0)
    row_batch = _iota((DEC_ROWS, LANES), 0) >> shift
    s_decay = [math.exp(LOG_DECAY[h] * dec_seq) for h in range(N_RET_HEADS)]

    def per_batch(b, carry):
        r0 = pl.multiple_of(b * dec_seq, dec_seq)
        rows = pl.ds(r0, dec_seq)
        in_batch = row_batch == b
        for p in range(N_PAIRS):
            st = state_ref[b, p * LANES:(p + 1) * LANES, :]
            qd = qd_ref[p, rows, :]
            res = _dot(_split_pair_rows(qd, first), st.astype(BF16))
            cross_ref[p, rows, 0:HEAD_DIM] = res[:dec_seq]
            cross_ref[p, rows, HEAD_DIM:LANES] = res[dec_seq:]
            kdt = kdt_ref[p]
            for half in range(2):
                vm = jnp.where(in_batch, vmat_ref[half, p], 0.0).astype(BF16)
                upd = _dot(kdt[half * HEAD_DIM:(half + 1) * HEAD_DIM, :], vm[:, :HEAD_DIM])
                lo = p * LANES + half * HEAD_DIM
                state_out_ref[b, lo:lo + HEAD_DIM, :] = (
                    st[half * HEAD_DIM:(half + 1) * HEAD_DIM, :] * s_decay[2 * p + half] + upd)
        k_cat = jnp.concatenate([kbuf_ref[b], knew_ref[pl.ds(r0, WINDOW), :]], axis=0).astype(BF16)
        v_cat = jnp.concatenate([vbuf_ref[b], vnew_ref[pl.ds(r0, WINDOW), :]], axis=0).astype(BF16)
        q_st = jnp.concatenate([qbd_ref[h, rows, :] for h in range(N_SWA_HEADS)], axis=0).astype(BF16)
        s = _dot_nt(q_st, k_cat) * K_SCALE + bias_ref[...]
        o = _softmax_sink_pv(s, sink_rows, v_cat)
        for h in range(N_SWA_HEADS):
            oblk_ref[h, rows, :] = o[h * dec_seq:(h + 1) * dec_seq]
        kout_ref[b, 0:WINDOW - dec_seq, :] = kbuf_ref[b, dec_seq:WINDOW, :]
        kout_ref[b, WINDOW - dec_seq:WINDOW, :] = kn_ref[rows, :]
        vout_ref[b, 0:WINDOW - dec_seq, :] = vbuf_ref[b, dec_seq:WINDOW, :]
        vout_ref[b, WINDOW - dec_seq:WINDOW, :] = vs_ref[rows, :]
        return carry

    lax.fori_loop(0, DEC_GROUP, per_batch, 0)

    for p in range(N_PAIRS):
        cols = lambda base: slice(base + p * LANES, base + (p + 1) * LANES)
        q = main_ref[:, cols(0)]
        k = main_ref[:, cols(RET_WIDTH)] * K_SCALE
        v = main_ref[:, cols(2 * RET_WIDTH)]
        g = main_ref[:, cols(3 * RET_WIDTH)]
        o = _retention_tile(q, k, v, intra_ref[p], first) + cross_ref[p]
        on = o * _head_rms_scale(o, ones_bd)
        mix_ref[:, p * LANES:(p + 1) * LANES] = (on * _silu(g)).astype(BF16)
    y1 = jnp.where(first, oblk_ref[3], oblk_ref[4])
    moved = pltpu.roll(jnp.concatenate([oblk_ref[1], y1, oblk_ref[6], oblk_ref[6]], axis=1), HEAD_DIM, axis=1)
    outs = [
        jnp.where(first, oblk_ref[0], moved[:, 0:LANES]),
        jnp.where(first, oblk_ref[2], moved[:, LANES:2 * LANES]),
        jnp.where(first, moved[:, 2 * LANES:3 * LANES], oblk_ref[5]),
        jnp.where(first, moved[:, 3 * LANES:4 * LANES], oblk_ref[7]),
    ]
    for c in range(SWA_WIDTH // LANES):
        mix_ref[:, RET_WIDTH + c * LANES:RET_WIDTH + (c + 1) * LANES] = outs[c].astype(BF16)


def _decode_mixer(sinks, main, qn, kn, vs, state, kbuf, vbuf, dec_seq):
    nb = state.shape[0]
    assert DEC_GROUP * dec_seq == DEC_ROWS and nb % DEC_GROUP == 0 and dec_seq & (dec_seq - 1) == 0
    row = lambda w: pl.BlockSpec((DEC_ROWS, w), lambda i: (i, 0))
    per_b = lambda a: pl.BlockSpec((DEC_GROUP,) + a.shape[1:], lambda i: (i, 0, 0))
    return pl.pallas_call(
        functools.partial(_decode_mixer_kernel, dec_seq),
        grid=(nb // DEC_GROUP,),
        in_specs=[pl.BlockSpec(memory_space=pltpu.SMEM),
                  row(MAIN_WIDTH), row(SWA_WIDTH), row(SWA_KV_WIDTH), row(SWA_KV_WIDTH),
                  per_b(state), per_b(kbuf), per_b(vbuf)],
        out_specs=[row(MIX_WIDTH), per_b(state), per_b(kbuf), per_b(vbuf)],
        out_shape=[jax.ShapeDtypeStruct((nb * dec_seq, MIX_WIDTH), BF16),
                   jax.ShapeDtypeStruct(state.shape, F32),
                   jax.ShapeDtypeStruct(kbuf.shape, F32),
                   jax.ShapeDtypeStruct(vbuf.shape, F32)],
        scratch_shapes=[
            pltpu.VMEM((N_PAIRS, LANES, 2 * LANES), F32),
            pltpu.VMEM((N_PAIRS, LANES, LANES), F32),
            pltpu.VMEM((N_PAIRS, LANES, LANES), F32),
            pltpu.VMEM((N_SWA_HEADS * dec_seq, 2 * WINDOW), F32),
            pltpu.VMEM((2 * DEC_ROWS, LANES), F32),
            pltpu.VMEM((2 * DEC_ROWS, LANES), F32),
            pltpu.VMEM((N_PAIRS, DEC_ROWS, LANES), F32),
            pltpu.VMEM((N_PAIRS, LANES, DEC_ROWS), BF16),
            pltpu.VMEM((2, N_PAIRS, DEC_ROWS, LANES), F32),
            pltpu.VMEM((N_PAIRS, DEC_ROWS, LANES), F32),
            pltpu.VMEM((N_SWA_HEADS, DEC_ROWS, LANES), F32),
            pltpu.VMEM((N_SWA_HEADS, DEC_ROWS, LANES), F32),
        ],
        compiler_params=pltpu.CompilerParams(
            dimension_semantics=("arbitrary",), vmem_limit_bytes=VMEM_LIMIT),
        name="decode_mixer",
    )(sinks, main, qn, kn, vs, state, kbuf, vbuf)


def kernel(x_prompt, x_sample, state_ret, cache_swa_k, cache_swa_v, norm_mix_gain, w_in, q_norm_gain,
           k_norm_gain, attn_sinks, w_out, norm_ffn_gain, w_up, w_down):
    batch, seq, d = x_prompt.shape
    nb, dec_seq, _ = x_sample.shape
    wb = cache_swa_k.shape[1]
    assert d == D_MODEL and seq % PROMPT_TILE == 0 and wb == WINDOW

    w_in_bf = w_in.astype(BF16)
    w_out_bf = w_out.astype(BF16)
    w_up_bf = w_up.astype(BF16)
    w_down_bf = w_down.astype(BF16)
    gain_mix = norm_mix_gain.reshape(1, D_MODEL)
    gain_ffn = norm_ffn_gain.reshape(1, D_MODEL)
    qg = jnp.tile(q_norm_gain, 2).reshape(1, LANES)
    kg = jnp.tile(k_norm_gain, 2).reshape(1, LANES)

    xp = x_prompt.reshape(batch * seq, D_MODEL)
    main_p, qn_p, kn_p, vs_p = _in_proj(xp, gain_mix, w_in_bf, qg, kg)
    mix_p, ret_p = _prompt_mixer(attn_sinks, main_p, qn_p, kn_p, vs_p, batch, seq)
    y_p = _out_mlp(mix_p, xp, w_out_bf, gain_ffn, w_up_bf, w_down_bf).reshape(batch, seq, D_MODEL)
    k_p = kn_p.reshape(batch, seq, SWA_KV_WIDTH)[:, -WINDOW:].reshape(batch, WINDOW, N_SWA_KV, HEAD_DIM)
    v_p = vs_p.reshape(batch, seq, SWA_KV_WIDTH)[:, -WINDOW:].reshape(batch, WINDOW, N_SWA_KV, HEAD_DIM)

    xs = x_sample.reshape(nb * dec_seq, D_MODEL)
    main_s, qn_s, kn_s, vs_s = _in_proj(xs, gain_mix, w_in_bf, qg, kg)
    mix_s, ret_s, k_s, v_s = _decode_mixer(
        attn_sinks, main_s, qn_s, kn_s, vs_s,
        state_ret.reshape(nb, N_RET_HEADS * HEAD_DIM, HEAD_DIM),
        cache_swa_k.reshape(nb, wb, SWA_KV_WIDTH), cache_swa_v.reshape(nb, wb, SWA_KV_WIDTH), dec_seq)
    y_s = _out_mlp(mix_s, xs, w_out_bf, gain_ffn, w_up_bf, w_down_bf).reshape(nb, dec_seq, D_MODEL)

    return (y_p, y_s, ret_p, k_p, v_p,
            ret_s.reshape(nb, N_RET_HEADS, HEAD_DIM, HEAD_DIM),
            k_s.reshape(nb, wb, N_SWA_KV, HEAD_DIM), v_s.reshape(nb, wb, N_SWA_KV, HEAD_DIM))
```

```python
import functools
import math

import jax
import jax.numpy as jnp
from jax import lax
from jax.experimental import pallas as pl
from jax.experimental.pallas import tpu as pltpu

F32 = jnp.float32
BF16 = jnp.bfloat16

D_MODEL = 1024
HEAD_DIM = 64
N_RET_HEADS = 8
N_SWA_HEADS = 8
N_SWA_KV = 2
SWA_GROUP = N_SWA_HEADS // N_SWA_KV
RET_WIDTH = N_RET_HEADS * HEAD_DIM
SWA_WIDTH = N_SWA_HEADS * HEAD_DIM
SWA_KV_WIDTH = N_SWA_KV * HEAD_DIM
MAIN_WIDTH = 4 * RET_WIDTH
IN_WIDTH = MAIN_WIDTH + SWA_WIDTH + 2 * SWA_KV_WIDTH
MIX_WIDTH = RET_WIDTH + SWA_WIDTH
D_FF = 4 * D_MODEL
WINDOW = 128
RET_CHUNK = 128
EPS = 1e-6
NEG_INF = -1e30

LANES = 128
N_PAIRS = N_RET_HEADS // 2
LOG_DECAY = [math.log(1.0 - 2.0 ** (-5.0 - h)) for h in range(N_RET_HEADS)]
ALIBI_SLOPES = [2.0 ** (-8.0 * (h + 1) / N_SWA_HEADS) for h in range(N_SWA_HEADS)]
K_SCALE = HEAD_DIM ** -0.5

ROW_TILE = 512
PROMPT_TILE = 512
DEC_GROUP = 16
VMEM_LIMIT = 56 * 1024 * 1024


def _dot(a, b):
    return jnp.dot(a, b, preferred_element_type=F32)


def _dot_nt(a, b):
    return lax.dot_general(a, b, (((1,), (1,)), ((), ())), preferred_element_type=F32)


def _iota(shape, dim):
    return lax.broadcasted_iota(jnp.int32, shape, dim)


def _ones_block_diag():
    same = (_iota((LANES, LANES), 0) >> 6) == (_iota((LANES, LANES), 1) >> 6)
    return jnp.where(same, 1.0, 0.0).astype(BF16)


def _head_sumsq(x, ones_bd):
    x2 = x * x
    hi = x2.astype(BF16)
    lo = (x2 - hi.astype(F32)).astype(BF16)
    return _dot(hi, ones_bd) + _dot(lo, ones_bd)


def _head_rms_scale(x, ones_bd):
    return lax.rsqrt(_head_sumsq(x, ones_bd) * (1.0 / HEAD_DIM) + EPS)


def _first_half():
    return _iota((1, LANES), 1) < HEAD_DIM


def _pair_const(values, pair, lane_is_second):
    return jnp.where(lane_is_second, values[2 * pair + 1], values[2 * pair]).astype(F32)


def _silu(g):
    return g * (1.0 / (1.0 + jnp.exp(-g)))


def _in_proj_kernel(x_ref, gain_ref, w_ref, qg_ref, kg_ref, main_ref, qn_ref, kn_ref, vs_ref):
    x = x_ref[...]
    ms = jnp.mean(x * x, axis=-1, keepdims=True)
    hb = ((x * lax.rsqrt(ms + EPS)) * gain_ref[...]).astype(BF16)
    ones_bd = _ones_block_diag()
    for c in range(MAIN_WIDTH // 512):
        main_ref[:, c * 512:(c + 1) * 512] = _dot(hb, w_ref[:, c * 512:(c + 1) * 512])
    qs = _dot(hb, w_ref[:, MAIN_WIDTH:MAIN_WIDTH + SWA_WIDTH])
    for c in range(SWA_WIDTH // LANES):
        xc = qs[:, c * LANES:(c + 1) * LANES]
        qn_ref[:, c * LANES:(c + 1) * LANES] = (xc * _head_rms_scale(xc, ones_bd)) * qg_ref[...]
    k0 = MAIN_WIDTH + SWA_WIDTH
    ks = _dot(hb, w_ref[:, k0:k0 + SWA_KV_WIDTH])
    kn_ref[...] = (ks * _head_rms_scale(ks, ones_bd)) * kg_ref[...]
    vs_ref[...] = _dot(hb, w_ref[:, k0 + SWA_KV_WIDTH:k0 + 2 * SWA_KV_WIDTH])


def _in_proj(x2d, gain, w_in_bf, qg, kg):
    m = x2d.shape[0]
    tm = min(ROW_TILE, m)
    row = lambda w: pl.BlockSpec((tm, w), lambda i: (i, 0))
    full = lambda a: pl.BlockSpec(a.shape, lambda i: (0, 0), pipeline_mode=pl.Buffered(1))
    return pl.pallas_call(
        _in_proj_kernel,
        grid=(m // tm,),
        in_specs=[row(D_MODEL), full(gain), full(w_in_bf), full(qg), full(kg)],
        out_specs=[row(MAIN_WIDTH), row(SWA_WIDTH), row(SWA_KV_WIDTH), row(SWA_KV_WIDTH)],
        out_shape=[jax.ShapeDtypeStruct((m, MAIN_WIDTH), F32),
                   jax.ShapeDtypeStruct((m, SWA_WIDTH), F32),
                   jax.ShapeDtypeStruct((m, SWA_KV_WIDTH), F32),
                   jax.ShapeDtypeStruct((m, SWA_KV_WIDTH), F32)],
        compiler_params=pltpu.CompilerParams(
            dimension_semantics=("arbitrary",), vmem_limit_bytes=VMEM_LIMIT),
        name="in_proj",
    )(x2d, gain, w_in_bf, qg, kg)


FF_CHUNK = 1024


def _out_mlp_kernel(mix_ret_ref, mix_swa_ref, x_ref, w_out_ref, gain_ref, w_up_ref, w_down_ref, y_ref):
    h = x_ref[...] + (_dot(mix_ret_ref[...].astype(BF16), w_out_ref[:RET_WIDTH, :])
                      + _dot(mix_swa_ref[...].astype(BF16), w_out_ref[RET_WIDTH:, :]))
    ms = jnp.mean(h * h, axis=-1, keepdims=True)
    hf = ((h * lax.rsqrt(ms + EPS)) * gain_ref[...]).astype(BF16)
    ff = None
    for c in range(D_FF // FF_CHUNK):
        u = _dot(hf, w_up_ref[:, c * FF_CHUNK:(c + 1) * FF_CHUNK])
        a = jnp.maximum(u, 0.0)
        d = _dot((a * a).astype(BF16), w_down_ref[c * FF_CHUNK:(c + 1) * FF_CHUNK, :])
        ff = d if ff is None else ff + d
    y_ref[...] = h + ff


def _out_mlp(mix_ret, ret_block, mix_swa, swa_block, x2d, w_out_bf, gain, w_up_bf, w_down_bf):
    m = x2d.shape[0]
    tm = min(ROW_TILE, m)
    row = lambda w: pl.BlockSpec((tm, w), lambda i: (i, 0))
    col_block = lambda c: pl.BlockSpec((tm, RET_WIDTH), lambda i: (i, c))
    full = lambda a: pl.BlockSpec(a.shape, lambda i: (0, 0), pipeline_mode=pl.Buffered(1))
    return pl.pallas_call(
        _out_mlp_kernel,
        grid=(m // tm,),
        in_specs=[col_block(ret_block), col_block(swa_block), row(D_MODEL), full(w_out_bf), full(gain),
                  full(w_up_bf), full(w_down_bf)],
        out_specs=row(D_MODEL),
        out_shape=jax.ShapeDtypeStruct((m, D_MODEL), F32),
        compiler_params=pltpu.CompilerParams(
            dimension_semantics=("arbitrary",), vmem_limit_bytes=VMEM_LIMIT),
        name="out_mlp",
    )(mix_ret, mix_swa, x2d, w_out_bf, gain, w_up_bf, w_down_bf)


def _split_pair_rows(x, first):
    return jnp.concatenate([jnp.where(first, x, 0.0), jnp.where(first, 0.0, x)], axis=0).astype(BF16)


def _retention_tile(q, k, v, intra, first):
    s = _dot_nt(q.astype(BF16), _split_pair_rows(k, first)) * intra
    return _dot(s.astype(BF16), _split_pair_rows(v, first))


def _softmax_sink_pv(s, sink_wide, v_t):
    m = jnp.maximum(jnp.max(s, axis=-1, keepdims=True), sink_wide)
    p = jnp.exp(s - jnp.concatenate([m, m], axis=1))
    denom = jnp.sum(p, axis=-1, keepdims=True) + jnp.exp(sink_wide - m)
    return _dot_nt(p.astype(BF16), v_t) / denom


def _softmax_sink_pv_t(s_t, sink_lanes, v_t):
    m = jnp.maximum(jnp.max(s_t, axis=0, keepdims=True), sink_lanes)
    p = jnp.exp(s_t - m)
    denom = jnp.sum(p, axis=0, keepdims=True) + jnp.exp(sink_lanes - m)
    return _dot(v_t, p.astype(BF16)) / denom


def _prompt_consts(intra_ref, qdec_ref, kdec_ref, sdec_ref, bias_ref):
    r = _iota((LANES, LANES), 0)
    lane2 = _iota((LANES, LANES), 1) >= HEAD_DIM
    rf = r.astype(F32)
    ri = _iota((LANES, 2 * LANES), 0)
    ci = _iota((LANES, 2 * LANES), 1)
    diff = (ri - (ci & (LANES - 1))).astype(F32)
    for p in range(N_PAIRS):
        lg = _pair_const(LOG_DECAY, p, lane2)
        qdec_ref[p] = jnp.exp(lg * (rf + 1.0))
        kdec_ref[p] = jnp.exp(lg * (RET_CHUNK - 1.0 - rf))
        sdec_ref[p] = jnp.exp(_pair_const(LOG_DECAY, p, r >= HEAD_DIM) * float(RET_CHUNK))
        lg2 = _pair_const(LOG_DECAY, p, ci >= LANES)
        intra_ref[p] = jnp.where(diff >= 0.0, jnp.exp(lg2 * jnp.maximum(diff, 0.0)), 0.0)
    cols = SWA_GROUP * WINDOW
    kb = _iota((2 * WINDOW, cols), 0)
    cb = _iota((2 * WINDOW, cols), 1)
    grp = cb >> 7
    dist = WINDOW + (cb & (WINDOW - 1)) - kb
    valid = (dist >= 0) & (dist < WINDOW)
    distf = dist.astype(F32)
    for j in range(N_SWA_KV):
        sl = [ALIBI_SLOPES[SWA_GROUP * j + g] for g in range(SWA_GROUP)]
        slope = jnp.where(grp == 0, sl[0], jnp.where(grp == 1, sl[1], jnp.where(grp == 2, sl[2], sl[3])))
        b = jnp.where(valid, -(slope.astype(F32) * distf), NEG_INF)
        bias_ref[0, j] = b
        bias_ref[1, j] = jnp.where(kb >= WINDOW, b, NEG_INF)


def _prompt_mixer_kernel(sinks_ref, main_ref, qn_ref, kn_ref, vs_ref,
                         mix_ref, ret_ref, kwin_ref, vwin_ref,
                         state_ref, prevk_ref, prevv_ref,
                         intra_ref, qdec_ref, kdec_ref, sdec_ref, bias_ref):
    t = pl.program_id(1)

    @pl.when((pl.program_id(0) == 0) & (t == 0))
    def _():
        _prompt_consts(intra_ref, qdec_ref, kdec_ref, sdec_ref, bias_ref)

    @pl.when(t == 0)
    def _():
        state_ref[...] = jnp.zeros_like(state_ref)
        prevk_ref[...] = jnp.zeros_like(prevk_ref)
        prevv_ref[...] = jnp.zeros_like(prevv_ref)

    first = _first_half()
    ones_bd = _ones_block_diag()
    bd_mask = (_iota((LANES, LANES), 0) >= HEAD_DIM) == (_iota((LANES, LANES), 1) >= HEAD_DIM)

    def chunk(c, carry):
        r0 = pl.multiple_of(c * RET_CHUNK, RET_CHUNK)
        rows = pl.ds(r0, RET_CHUNK)
        for p in range(N_PAIRS):
            cols = lambda base: slice(base + p * LANES, base + (p + 1) * LANES)
            q = main_ref[rows, cols(0)]
            k = main_ref[rows, cols(RET_WIDTH)] * K_SCALE
            v = main_ref[rows, cols(2 * RET_WIDTH)]
            g = main_ref[rows, cols(3 * RET_WIDTH)]
            state = state_ref[p]
            o = _retention_tile(q, k, v, intra_ref[p], first)
            o = o + _dot((q * qdec_ref[p]).astype(BF16), state.astype(BF16))
            kd_t = (k * kdec_ref[p]).T.astype(BF16)
            upd = _dot(kd_t, v.astype(BF16))
            state_ref[p] = state * sdec_ref[p] + jnp.where(bd_mask, upd, 0.0)
            on = o * _head_rms_scale(o, ones_bd)
            mix_ref[rows, p * LANES:(p + 1) * LANES] = (on * _silu(g)).astype(BF16)
        kc = kn_ref[rows, :]
        k_sw = pltpu.roll(kc, HEAD_DIM, axis=1)
        v_t = vs_ref[rows, :].T
        is_first = ((t == 0) & (c == 0)).astype(jnp.int32)
        for j in range(N_SWA_KV):
            k_dup = (jnp.where(first, kc, k_sw) if j == 0 else jnp.where(first, k_sw, kc)).astype(BF16)
            v_tj = v_t[j * HEAD_DIM:(j + 1) * HEAD_DIM].astype(BF16)
            k_cat = jnp.concatenate([prevk_ref[j], k_dup], axis=0)
            v_cat = jnp.concatenate([prevv_ref[j], v_tj], axis=1)
            pieces = []
            for g in range(SWA_GROUP):
                col = 2 * j + g // 2
                qc = qn_ref[rows, col * LANES:(col + 1) * LANES]
                pieces.append(jnp.where(first, qc, 0.0) if g % 2 == 0 else jnp.where(first, 0.0, qc))
            q_st = jnp.concatenate(pieces, axis=0).astype(BF16)
            s_t = _dot_nt(k_cat, q_st) * K_SCALE + bias_ref[is_first, j]
            sink_lanes = jnp.concatenate(
                [jnp.full((1, WINDOW), sinks_ref[SWA_GROUP * j + g], F32) for g in range(SWA_GROUP)],
                axis=1)
            o_t = _softmax_sink_pv_t(s_t, sink_lanes, v_cat)
            for half in range(2):
                col = 2 * j + half
                pair_t = jnp.concatenate([o_t[:, (2 * half) * WINDOW:(2 * half + 1) * WINDOW],
                                          o_t[:, (2 * half + 1) * WINDOW:(2 * half + 2) * WINDOW]], axis=0)
                mix_ref[rows, RET_WIDTH + col * LANES:RET_WIDTH + (col + 1) * LANES] = pair_t.T.astype(BF16)
            prevk_ref[j] = k_dup
            prevv_ref[j] = v_tj
        return carry

    lax.fori_loop(0, PROMPT_TILE // RET_CHUNK, chunk, 0)

    @pl.when(t == pl.num_programs(1) - 1)
    def _():
        for p in range(N_PAIRS):
            s = state_ref[p]
            ret_ref[2 * p] = s[:HEAD_DIM, :HEAD_DIM]
            ret_ref[2 * p + 1] = s[HEAD_DIM:, HEAD_DIM:]
        last = slice(PROMPT_TILE - WINDOW, PROMPT_TILE)
        kwin_ref[...] = kn_ref[last, :].T
        vwin_ref[...] = vs_ref[last, :].T


def _prompt_mixer(sinks, main, qn, kn, vs, batch, seq):
    nt = seq // PROMPT_TILE
    row = lambda w: pl.BlockSpec((PROMPT_TILE, w), lambda b, t: (b * nt + t, 0))
    return pl.pallas_call(
        _prompt_mixer_kernel,
        grid=(batch, nt),
        in_specs=[pl.BlockSpec(memory_space=pltpu.SMEM),
                  row(MAIN_WIDTH), row(SWA_WIDTH), row(SWA_KV_WIDTH), row(SWA_KV_WIDTH)],
        out_specs=[row(MIX_WIDTH),
                   pl.BlockSpec((None, N_RET_HEADS, HEAD_DIM, HEAD_DIM), lambda b, t: (b, 0, 0, 0)),
                   pl.BlockSpec((None, SWA_KV_WIDTH, WINDOW), lambda b, t: (b, 0, 0)),
                   pl.BlockSpec((None, SWA_KV_WIDTH, WINDOW), lambda b, t: (b, 0, 0))],
        out_shape=[jax.ShapeDtypeStruct((batch * seq, MIX_WIDTH), BF16),
                   jax.ShapeDtypeStruct((batch, N_RET_HEADS, HEAD_DIM, HEAD_DIM), F32),
                   jax.ShapeDtypeStruct((batch, SWA_KV_WIDTH, WINDOW), F32),
                   jax.ShapeDtypeStruct((batch, SWA_KV_WIDTH, WINDOW), F32)],
        scratch_shapes=[
            pltpu.VMEM((N_PAIRS, LANES, LANES), F32),
            pltpu.VMEM((N_SWA_KV, WINDOW, LANES), BF16),
            pltpu.VMEM((N_SWA_KV, HEAD_DIM, WINDOW), BF16),
            pltpu.VMEM((N_PAIRS, LANES, 2 * LANES), F32),
            pltpu.VMEM((N_PAIRS, LANES, LANES), F32),
            pltpu.VMEM((N_PAIRS, LANES, LANES), F32),
            pltpu.VMEM((N_PAIRS, LANES, LANES), F32),
            pltpu.VMEM((2, N_SWA_KV, 2 * WINDOW, SWA_GROUP * WINDOW), F32),
        ],
        compiler_params=pltpu.CompilerParams(
            dimension_semantics=("arbitrary", "arbitrary"), vmem_limit_bytes=VMEM_LIMIT),
        name="prompt_mixer",
    )(sinks, main, qn, kn, vs)


def _decode_ret_kernel(dec_seq, nb, qdec_ref, kdec_ref, sdec_ref, intra_ref,
                       q_ref, k_ref, v_ref, g_ref, st_ref,
                       mix_ref, st_out_ref,
                       qt_ref, kt_ref, vt_ref, qdt_ref, kdt_ref, o_ref):
    pair = pl.program_id(0)
    halves = [slice(0, HEAD_DIM), slice(HEAD_DIM, 2 * HEAD_DIM)]
    for l in range(dec_seq):
        rows = pl.ds(l, nb, stride=dec_seq)
        q_t = q_ref[rows, :].T
        k_t = (k_ref[rows, :] * K_SCALE).T
        qt_ref[l] = q_t
        kt_ref[l] = k_t
        vt_ref[l] = v_ref[rows, :].T
        for hh in range(2):
            qdt_ref[l, halves[hh], :] = q_t[halves[hh]] * qdec_ref[2 * pair + hh, l]
            kdt_ref[l, halves[hh], :] = k_t[halves[hh]] * kdec_ref[2 * pair + hh, l]

    e_blk = HEAD_DIM // 2
    for hh in range(2):
        h = 2 * pair + hh
        hs = halves[hh]
        for l in range(dec_seq):
            acc = None
            for m in range(l + 1):
                sc = jnp.sum(qt_ref[l, hs, :] * kt_ref[m, hs, :], axis=0, keepdims=True) * intra_ref[h, l - m]
                term = sc * vt_ref[m, hs, :]
                acc = term if acc is None else acc + term
            o_ref[l, hs, :] = acc
        for eb in range(HEAD_DIM // e_blk):
            es = slice(eb * e_blk, (eb + 1) * e_blk)
            erows = slice(hh * HEAD_DIM + eb * e_blk, hh * HEAD_DIM + (eb + 1) * e_blk)

            def body(d, accs, hh=hh, h=h, es=es, erows=erows):
                s_d = st_ref[hh, d, es, :]
                row = pl.ds(hh * HEAD_DIM + d, 1)
                upd = s_d * sdec_ref[h]
                new_accs = []
                for l in range(dec_seq):
                    new_accs.append(accs[l] + qdt_ref[l, row, :] * s_d)
                    upd = upd + kdt_ref[l, row, :] * vt_ref[l, erows, :]
                st_out_ref[hh, d, es, :] = upd
                return tuple(new_accs)

            zero = jnp.zeros((e_blk, nb), F32)
            accs = lax.fori_loop(0, HEAD_DIM, body, tuple(zero for _ in range(dec_seq)))
            for l in range(dec_seq):
                o_ref[l, erows, :] = o_ref[l, erows, :] + accs[l]

    for l in range(dec_seq):
        o = o_ref[l]
        normed = []
        for hh in range(2):
            oh = o[halves[hh]]
            normed.append(oh * lax.rsqrt(jnp.mean(oh * oh, axis=0, keepdims=True) + EPS))
        rows = pl.ds(l, nb, stride=dec_seq)
        mix_ref[rows, :] = jnp.concatenate(normed, axis=0).T * _silu(g_ref[rows, :])


def _decode_ret(main, state_t, dec_seq):
    nb = state_t.shape[-1]
    m = main.shape[0]
    assert nb == LANES and m == nb * dec_seq
    steps = [j for j in range(dec_seq)]
    tab = lambda f: jnp.asarray([[f(h, j) for j in steps] for h in range(N_RET_HEADS)], F32)
    qdec = tab(lambda h, j: math.exp(LOG_DECAY[h] * (j + 1.0)))
    kdec = tab(lambda h, j: math.exp(LOG_DECAY[h] * (dec_seq - 1.0 - j)))
    intra = tab(lambda h, j: math.exp(LOG_DECAY[h] * j))
    sdec = jnp.asarray([math.exp(LOG_DECAY[h] * dec_seq) for h in range(N_RET_HEADS)], F32)
    smem = pl.BlockSpec(memory_space=pltpu.SMEM)
    col = lambda base: pl.BlockSpec((m, LANES), lambda p: (0, base + p))
    st_spec = pl.BlockSpec((2, HEAD_DIM, HEAD_DIM, nb), lambda p: (p, 0, 0, 0))
    stage = pltpu.VMEM((dec_seq, LANES, nb), F32)
    return pl.pallas_call(
        functools.partial(_decode_ret_kernel, dec_seq, nb),
        grid=(N_PAIRS,),
        in_specs=[smem, smem, smem, smem,
                  col(0), col(N_PAIRS), col(2 * N_PAIRS), col(3 * N_PAIRS), st_spec],
        out_specs=[pl.BlockSpec((m, LANES), lambda p: (0, p)), st_spec],
        out_shape=[jax.ShapeDtypeStruct((m, RET_WIDTH), F32),
                   jax.ShapeDtypeStruct(state_t.shape, F32)],
        scratch_shapes=[stage, stage, stage, stage, stage, stage],
        compiler_params=pltpu.CompilerParams(
            dimension_semantics=("arbitrary",), vmem_limit_bytes=VMEM_LIMIT),
        name="decode_ret",
    )(qdec, kdec, sdec, intra, main, main, main, main, state_t)


DEC_ROWS = 128


def _decode_attn_consts(dec_seq, bias_ref):
    shift = dec_seq.bit_length() - 1
    rows = N_SWA_HEADS * dec_seq
    rb = _iota((rows, WINDOW), 0)
    cb = _iota((rows, WINDOW), 1)
    head = rb >> shift
    i = rb & (dec_seq - 1)
    slope = jnp.zeros((rows, WINDOW), F32)
    for h in range(N_SWA_HEADS):
        slope = jnp.where(head == h, ALIBI_SLOPES[h], slope)
    bias_ref[0] = jnp.where(cb > i, -(slope * (WINDOW + i - cb).astype(F32)), NEG_INF)
    m = cb & (dec_seq - 1)
    bias_ref[1] = jnp.where(m <= i, -(slope * (i - m).astype(F32)), NEG_INF)


def _decode_attn_kernel(dec_seq, sinks_ref, qn_ref, kn_ref, vs_ref, kt_ref, vt_ref,
                        mix_ref, kt_out_ref, vt_out_ref,
                        bias_ref, qbd_ref, oblk_ref, knew_ref, vnew_ref, knt_ref, vst_ref):
    @pl.when(pl.program_id(0) == 0)
    def _():
        _decode_attn_consts(dec_seq, bias_ref)

    first = _first_half()
    shift = dec_seq.bit_length() - 1

    kn_t = kn_ref[...].T
    vs_t = vs_ref[...].T
    knt_ref[...] = kn_t.astype(BF16)
    vst_ref[...] = vs_t.astype(BF16)
    for bb in range(DEC_GROUP):
        sh = (WINDOW - dec_seq - bb * dec_seq) % LANES
        knew_ref[bb] = pltpu.roll(kn_t, sh, axis=1) if sh else kn_t
        vnew_ref[bb] = pltpu.roll(vs_t, sh, axis=1) if sh else vs_t
    qn = qn_ref[...]
    qn_sw = pltpu.roll(qn, HEAD_DIM, axis=1)
    for h in range(N_SWA_HEADS):
        kv_half = h // SWA_GROUP
        if (h % 2) == kv_half:
            src = qn[:, (h // 2) * LANES:(h // 2 + 1) * LANES]
        else:
            col = (h + 1) // 2
            src = qn_sw[:, col * LANES:(col + 1) * LANES]
        qbd_ref[h] = jnp.where(first, src, 0.0) if kv_half == 0 else jnp.where(first, 0.0, src)

    sink_rows = jnp.concatenate(
        [jnp.full((dec_seq, LANES), sinks_ref[h], F32) for h in range(N_SWA_HEADS)], axis=0)
    col_batch = _iota((N_SWA_HEADS * dec_seq, LANES), 1) >> shift
    keep_old = _iota((1, LANES), 1) < WINDOW - dec_seq

    def per_batch(b, carry):
        rows = pl.ds(pl.multiple_of(b * dec_seq, dec_seq), dec_seq)
        k_old = kt_ref[b]
        v_old = vt_ref[b]
        w_k = jnp.concatenate([k_old.astype(BF16), knt_ref[...]], axis=1)
        w_v = jnp.concatenate([v_old.astype(BF16), vst_ref[...]], axis=1)
        q_st = jnp.concatenate([qbd_ref[h, rows, :] for h in range(N_SWA_HEADS)], axis=0).astype(BF16)
        bias = jnp.concatenate([bias_ref[0], jnp.where(col_batch == b, bias_ref[1], NEG_INF)], axis=1)
        s = _dot(q_st, w_k) * K_SCALE + bias
        o = _softmax_sink_pv(s, sink_rows, w_v)
        for h in range(N_SWA_HEADS):
            oblk_ref[h, rows, :] = o[h * dec_seq:(h + 1) * dec_seq]
        kt_out_ref[b] = jnp.where(keep_old, pltpu.roll(k_old, LANES - dec_seq, axis=1), knew_ref[b])
        vt_out_ref[b] = jnp.where(keep_old, pltpu.roll(v_old, LANES - dec_seq, axis=1), vnew_ref[b])
        return carry

    lax.fori_loop(0, DEC_GROUP, per_batch, 0)

    y1 = jnp.where(first, oblk_ref[3], oblk_ref[4])
    moved = pltpu.roll(jnp.concatenate([oblk_ref[1], y1, oblk_ref[6], oblk_ref[6]], axis=1), HEAD_DIM, axis=1)
    outs = [
        jnp.where(first, oblk_ref[0], moved[:, 0:LANES]),
        jnp.where(first, oblk_ref[2], moved[:, LANES:2 * LANES]),
        jnp.where(first, moved[:, 2 * LANES:3 * LANES], oblk_ref[5]),
        jnp.where(first, moved[:, 3 * LANES:4 * LANES], oblk_ref[7]),
    ]
    for c in range(SWA_WIDTH // LANES):
        mix_ref[:, c * LANES:(c + 1) * LANES] = outs[c].astype(BF16)


def _decode_attn(sinks, qn, kn, vs, k_t, v_t, dec_seq):
    nb = k_t.shape[0]
    assert DEC_GROUP * dec_seq == DEC_ROWS and nb % DEC_GROUP == 0 and dec_seq & (dec_seq - 1) == 0
    assert k_t.shape[1:] == (SWA_KV_WIDTH, WINDOW)
    row = lambda w: pl.BlockSpec((DEC_ROWS, w), lambda i: (i, 0))
    cache = pl.BlockSpec((DEC_GROUP, SWA_KV_WIDTH, WINDOW), lambda i: (i, 0, 0))
    return pl.pallas_call(
        functools.partial(_decode_attn_kernel, dec_seq),
        grid=(nb // DEC_GROUP,),
        in_specs=[pl.BlockSpec(memory_space=pltpu.SMEM),
                  row(SWA_WIDTH), row(SWA_KV_WIDTH), row(SWA_KV_WIDTH), cache, cache],
        out_specs=[row(SWA_WIDTH), cache, cache],
        out_shape=[jax.ShapeDtypeStruct((nb * dec_seq, SWA_WIDTH), BF16),
                   jax.ShapeDtypeStruct(k_t.shape, F32),
                   jax.ShapeDtypeStruct(v_t.shape, F32)],
        scratch_shapes=[
            pltpu.VMEM((2, N_SWA_HEADS * dec_seq, WINDOW), F32),
            pltpu.VMEM((N_SWA_HEADS, DEC_ROWS, LANES), F32),
            pltpu.VMEM((N_SWA_HEADS, DEC_ROWS, LANES), F32),
            pltpu.VMEM((DEC_GROUP, SWA_KV_WIDTH, LANES), F32),
            pltpu.VMEM((DEC_GROUP, SWA_KV_WIDTH, LANES), F32),
            pltpu.VMEM((SWA_KV_WIDTH, DEC_ROWS), BF16),
            pltpu.VMEM((SWA_KV_WIDTH, DEC_ROWS), BF16),
        ],
        compiler_params=pltpu.CompilerParams(
            dimension_semantics=("arbitrary",), vmem_limit_bytes=VMEM_LIMIT),
        name="decode_attn",
    )(sinks, qn, kn, vs, k_t, v_t)


def kernel(x_prompt, x_sample, state_ret, cache_swa_k, cache_swa_v, norm_mix_gain, w_in, q_norm_gain,
           k_norm_gain, attn_sinks, w_out, norm_ffn_gain, w_up, w_down):
    batch, seq, d = x_prompt.shape
    nb, dec_seq, _ = x_sample.shape
    wb = cache_swa_k.shape[1]
    assert d == D_MODEL and seq % PROMPT_TILE == 0 and wb == WINDOW

    w_in_bf = w_in.astype(BF16)
    w_out_bf = w_out.astype(BF16)
    w_up_bf = w_up.astype(BF16)
    w_down_bf = w_down.astype(BF16)
    gain_mix = norm_mix_gain.reshape(1, D_MODEL)
    gain_ffn = norm_ffn_gain.reshape(1, D_MODEL)
    qg = jnp.tile(q_norm_gain, 2).reshape(1, LANES)
    kg = jnp.tile(k_norm_gain, 2).reshape(1, LANES)

    xp = x_prompt.reshape(batch * seq, D_MODEL)
    main_p, qn_p, kn_p, vs_p = _in_proj(xp, gain_mix, w_in_bf, qg, kg)
    mix_p, ret_p, kwin_t, vwin_t = _prompt_mixer(attn_sinks, main_p, qn_p, kn_p, vs_p, batch, seq)
    y_p = _out_mlp(mix_p, 0, mix_p, 1, xp, w_out_bf, gain_ffn, w_up_bf, w_down_bf).reshape(batch, seq, D_MODEL)

    def from_key_minor(a_t):
        return jnp.transpose(a_t.reshape(a_t.shape[0], N_SWA_KV, HEAD_DIM, WINDOW), (0, 3, 1, 2))

    def to_key_minor(a):
        return jnp.transpose(a, (0, 2, 3, 1)).reshape(a.shape[0], SWA_KV_WIDTH, WINDOW)

    xs = x_sample.reshape(nb * dec_seq, D_MODEL)
    main_s, qn_s, kn_s, vs_s = _in_proj(xs, gain_mix, w_in_bf, qg, kg)
    mix_ret_s, state_t = _decode_ret(main_s, jnp.transpose(state_ret, (1, 2, 3, 0)), dec_seq)
    mix_swa_s, k_t, v_t = _decode_attn(attn_sinks, qn_s, kn_s, vs_s,
                                       to_key_minor(cache_swa_k), to_key_minor(cache_swa_v), dec_seq)
    y_s = _out_mlp(mix_ret_s, 0, mix_swa_s, 0, xs, w_out_bf, gain_ffn, w_up_bf, w_down_bf)

    return (y_p, y_s.reshape(nb, dec_seq, D_MODEL), ret_p, from_key_minor(kwin_t), from_key_minor(vwin_t),
            jnp.transpose(state_t, (3, 0, 1, 2)), from_key_minor(k_t), from_key_minor(v_t))
```

```python
import functools
import math

import jax
import jax.numpy as jnp
from jax import lax
from jax.experimental import pallas as pl
from jax.experimental.pallas import tpu as pltpu

F32 = jnp.float32
BF16 = jnp.bfloat16

D_MODEL = 1024
HEAD_DIM = 64
N_RET_HEADS = 8
N_SWA_HEADS = 8
N_SWA_KV = 2
SWA_GROUP = N_SWA_HEADS // N_SWA_KV
RET_WIDTH = N_RET_HEADS * HEAD_DIM
SWA_WIDTH = N_SWA_HEADS * HEAD_DIM
SWA_KV_WIDTH = N_SWA_KV * HEAD_DIM
MAIN_WIDTH = 4 * RET_WIDTH
IN_WIDTH = MAIN_WIDTH + SWA_WIDTH + 2 * SWA_KV_WIDTH
MIX_WIDTH = RET_WIDTH + SWA_WIDTH
D_FF = 4 * D_MODEL
WINDOW = 128
RET_CHUNK = 128
EPS = 1e-6
NEG_INF = -1e30

LANES = 128
N_PAIRS = N_RET_HEADS // 2
LOG_DECAY = [math.log(1.0 - 2.0 ** (-5.0 - h)) for h in range(N_RET_HEADS)]
ALIBI_SLOPES = [2.0 ** (-8.0 * (h + 1) / N_SWA_HEADS) for h in range(N_SWA_HEADS)]
K_SCALE = HEAD_DIM ** -0.5

ROW_TILE = 512
PROMPT_TILE = 512
DEC_GROUP = 16
VMEM_LIMIT = 56 * 1024 * 1024


def _dot(a, b):
    return jnp.dot(a, b, preferred_element_type=F32)


def _dot_nt(a, b):
    return lax.dot_general(a, b, (((1,), (1,)), ((), ())), preferred_element_type=F32)


def _iota(shape, dim):
    return lax.broadcasted_iota(jnp.int32, shape, dim)


def _ones_block_diag():
    same = (_iota((LANES, LANES), 0) >> 6) == (_iota((LANES, LANES), 1) >> 6)
    return jnp.where(same, 1.0, 0.0).astype(BF16)


def _head_sumsq(x, ones_bd):
    x2 = x * x
    hi = x2.astype(BF16)
    lo = (x2 - hi.astype(F32)).astype(BF16)
    return _dot(hi, ones_bd) + _dot(lo, ones_bd)


def _head_rms_scale(x, ones_bd):
    return lax.rsqrt(_head_sumsq(x, ones_bd) * (1.0 / HEAD_DIM) + EPS)


def _first_half():
    return _iota((1, LANES), 1) < HEAD_DIM


def _pair_const(values, pair, lane_is_second):
    return jnp.where(lane_is_second, values[2 * pair + 1], values[2 * pair]).astype(F32)


def _silu(g):
    return g * (1.0 / (1.0 + jnp.exp(-g)))


def _in_proj_kernel(x_ref, gain_ref, w_ref, qg_ref, kg_ref, main_ref, qn_ref, kn_ref, vs_ref):
    x = x_ref[...]
    ms = jnp.mean(x * x, axis=-1, keepdims=True)
    hb = ((x * lax.rsqrt(ms + EPS)) * gain_ref[...]).astype(BF16)
    ones_bd = _ones_block_diag()
    for c in range(MAIN_WIDTH // 512):
        main_ref[:, c * 512:(c + 1) * 512] = _dot(hb, w_ref[:, c * 512:(c + 1) * 512])
    qs = _dot(hb, w_ref[:, MAIN_WIDTH:MAIN_WIDTH + SWA_WIDTH])
    for c in range(SWA_WIDTH // LANES):
        xc = qs[:, c * LANES:(c + 1) * LANES]
        qn_ref[:, c * LANES:(c + 1) * LANES] = (xc * _head_rms_scale(xc, ones_bd)) * qg_ref[...]
    k0 = MAIN_WIDTH + SWA_WIDTH
    ks = _dot(hb, w_ref[:, k0:k0 + SWA_KV_WIDTH])
    kn_ref[...] = (ks * _head_rms_scale(ks, ones_bd)) * kg_ref[...]
    vs_ref[...] = _dot(hb, w_ref[:, k0 + SWA_KV_WIDTH:k0 + 2 * SWA_KV_WIDTH])


def _in_proj(x2d, gain, w_in_bf, qg, kg):
    m = x2d.shape[0]
    tm = min(ROW_TILE, m)
    row = lambda w: pl.BlockSpec((tm, w), lambda i: (i, 0))
    full = lambda a: pl.BlockSpec(a.shape, lambda i: (0, 0), pipeline_mode=pl.Buffered(1))
    return pl.pallas_call(
        _in_proj_kernel,
        grid=(m // tm,),
        in_specs=[row(D_MODEL), full(gain), full(w_in_bf), full(qg), full(kg)],
        out_specs=[row(MAIN_WIDTH), row(SWA_WIDTH), row(SWA_KV_WIDTH), row(SWA_KV_WIDTH)],
        out_shape=[jax.ShapeDtypeStruct((m, MAIN_WIDTH), F32),
                   jax.ShapeDtypeStruct((m, SWA_WIDTH), F32),
                   jax.ShapeDtypeStruct((m, SWA_KV_WIDTH), F32),
                   jax.ShapeDtypeStruct((m, SWA_KV_WIDTH), F32)],
        compiler_params=pltpu.CompilerParams(
            dimension_semantics=("arbitrary",), vmem_limit_bytes=VMEM_LIMIT),
        name="in_proj",
    )(x2d, gain, w_in_bf, qg, kg)


FF_CHUNK = 1024


def _out_mlp_kernel(mix_ret_ref, mix_swa_ref, x_ref, w_out_ref, gain_ref, w_up_ref, w_down_ref, y_ref):
    h = x_ref[...] + (_dot(mix_ret_ref[...].astype(BF16), w_out_ref[:RET_WIDTH, :])
                      + _dot(mix_swa_ref[...].astype(BF16), w_out_ref[RET_WIDTH:, :]))
    ms = jnp.mean(h * h, axis=-1, keepdims=True)
    hf = ((h * lax.rsqrt(ms + EPS)) * gain_ref[...]).astype(BF16)
    ff = None
    for c in range(D_FF // FF_CHUNK):
        u = _dot(hf, w_up_ref[:, c * FF_CHUNK:(c + 1) * FF_CHUNK])
        a = jnp.maximum(u, 0.0)
        d = _dot((a * a).astype(BF16), w_down_ref[c * FF_CHUNK:(c + 1) * FF_CHUNK, :])
        ff = d if ff is None else ff + d
    y_ref[...] = h + ff


def _out_mlp(mix_ret, ret_block, mix_swa, swa_block, x2d, w_out_bf, gain, w_up_bf, w_down_bf):
    m = x2d.shape[0]
    tm = min(ROW_TILE, m)
    row = lambda w: pl.BlockSpec((tm, w), lambda i: (i, 0))
    col_block = lambda c: pl.BlockSpec((tm, RET_WIDTH), lambda i: (i, c))
    full = lambda a: pl.BlockSpec(a.shape, lambda i: (0, 0), pipeline_mode=pl.Buffered(1))
    return pl.pallas_call(
        _out_mlp_kernel,
        grid=(m // tm,),
        in_specs=[col_block(ret_block), col_block(swa_block), row(D_MODEL), full(w_out_bf), full(gain),
                  full(w_up_bf), full(w_down_bf)],
        out_specs=row(D_MODEL),
        out_shape=jax.ShapeDtypeStruct((m, D_MODEL), F32),
        compiler_params=pltpu.CompilerParams(
            dimension_semantics=("arbitrary",), vmem_limit_bytes=VMEM_LIMIT),
        name="out_mlp",
    )(mix_ret, mix_swa, x2d, w_out_bf, gain, w_up_bf, w_down_bf)


def _split_pair_rows(x, first):
    return jnp.concatenate([jnp.where(first, x, 0.0), jnp.where(first, 0.0, x)], axis=0).astype(BF16)


def _softmax_sink_pv(s, sink_wide, v_t):
    m = jnp.maximum(jnp.max(s, axis=-1, keepdims=True), sink_wide)
    p = jnp.exp(s - jnp.concatenate([m, m], axis=1))
    denom = jnp.sum(p, axis=-1, keepdims=True) + jnp.exp(sink_wide - m)
    return _dot_nt(p.astype(BF16), v_t) / denom


def _softmax_sink_pv_t(s_t, sink_lanes, v_t):
    m = jnp.maximum(jnp.max(s_t, axis=0, keepdims=True), sink_lanes)
    p = jnp.exp(s_t - m)
    denom = jnp.sum(p, axis=0, keepdims=True) + jnp.exp(sink_lanes - m)
    return _dot(v_t, p.astype(BF16)) / denom


def _prompt_consts(intra_ref, qdec_ref, kdec_ref, sdec_ref, bias_ref):
    r = _iota((LANES, LANES), 0)
    lane2 = _iota((LANES, LANES), 1) >= HEAD_DIM
    rf = r.astype(F32)
    ri = _iota((LANES, 2 * LANES), 0)
    ci = _iota((LANES, 2 * LANES), 1)
    diff = (ri - (ci & (LANES - 1))).astype(F32)
    for p in range(N_PAIRS):
        lg = _pair_const(LOG_DECAY, p, lane2)
        qdec_ref[p] = jnp.exp(lg * (rf + 1.0))
        kdec_ref[p] = jnp.exp(lg * (RET_CHUNK - 1.0 - rf))
        sdec_ref[p] = jnp.exp(_pair_const(LOG_DECAY, p, r >= HEAD_DIM) * float(RET_CHUNK))
        lg2 = _pair_const(LOG_DECAY, p, ci >= LANES)
        intra_ref[p] = jnp.where(diff >= 0.0, jnp.exp(lg2 * jnp.maximum(diff, 0.0)), 0.0)
    cols = SWA_GROUP * WINDOW
    kb = _iota((2 * WINDOW, cols), 0)
    cb = _iota((2 * WINDOW, cols), 1)
    grp = cb >> 7
    dist = WINDOW + (cb & (WINDOW - 1)) - kb
    valid = (dist >= 0) & (dist < WINDOW)
    distf = dist.astype(F32)
    for j in range(N_SWA_KV):
        sl = [ALIBI_SLOPES[SWA_GROUP * j + g] for g in range(SWA_GROUP)]
        slope = jnp.where(grp == 0, sl[0], jnp.where(grp == 1, sl[1], jnp.where(grp == 2, sl[2], sl[3])))
        b = jnp.where(valid, -(slope.astype(F32) * distf), NEG_INF)
        bias_ref[0, j] = b
        bias_ref[1, j] = jnp.where(kb >= WINDOW, b, NEG_INF)


def _prompt_mixer_kernel(sinks_ref, main_ref, qn_ref, kn_ref, vs_ref,
                         mix_ref, ret_ref, kwin_ref, vwin_ref,
                         state_ref, prevk_ref, prevv_ref,
                         intra_ref, qdec_ref, kdec_ref, sdec_ref, bias_ref):
    t = pl.program_id(1)

    @pl.when((pl.program_id(0) == 0) & (t == 0))
    def _():
        _prompt_consts(intra_ref, qdec_ref, kdec_ref, sdec_ref, bias_ref)

    @pl.when(t == 0)
    def _():
        state_ref[...] = jnp.zeros_like(state_ref)
        prevk_ref[...] = jnp.zeros_like(prevk_ref)
        prevv_ref[...] = jnp.zeros_like(prevv_ref)

    first = _first_half()
    ones_bd = _ones_block_diag()
    bd_mask = (_iota((LANES, LANES), 0) >= HEAD_DIM) == (_iota((LANES, LANES), 1) >= HEAD_DIM)

    def chunk(c, carry):
        r0 = pl.multiple_of(c * RET_CHUNK, RET_CHUNK)
        rows = pl.ds(r0, RET_CHUNK)
        pairs = range(N_PAIRS)
        kvs = range(N_SWA_KV)
        cols = lambda base, p: slice(base + p * LANES, base + (p + 1) * LANES)
        q = [main_ref[rows, cols(0, p)] for p in pairs]
        k = [main_ref[rows, cols(RET_WIDTH, p)] * K_SCALE for p in pairs]
        v = [main_ref[rows, cols(2 * RET_WIDTH, p)] for p in pairs]
        state = [state_ref[p] for p in pairs]
        kc = kn_ref[rows, :]
        k_sw = pltpu.roll(kc, HEAD_DIM, axis=1)
        v_t = vs_ref[rows, :].T
        is_first = ((t == 0) & (c == 0)).astype(jnp.int32)
        k_dup = [(jnp.where(first, kc, k_sw) if j == 0 else jnp.where(first, k_sw, kc)).astype(BF16)
                 for j in kvs]
        v_tj = [v_t[j * HEAD_DIM:(j + 1) * HEAD_DIM].astype(BF16) for j in kvs]
        q_st = []
        for j in kvs:
            pieces = []
            for g in range(SWA_GROUP):
                qc = qn_ref[rows, cols(0, 2 * j + g // 2)]
                pieces.append(jnp.where(first, qc, 0.0) if g % 2 == 0 else jnp.where(first, 0.0, qc))
            q_st.append(jnp.concatenate(pieces, axis=0).astype(BF16))

        s = [_dot_nt(q[p].astype(BF16), _split_pair_rows(k[p], first)) for p in pairs]
        s_t = [_dot_nt(jnp.concatenate([prevk_ref[j], k_dup[j]], axis=0), q_st[j]) for j in kvs]
        cross = [_dot((q[p] * qdec_ref[p]).astype(BF16), state[p].astype(BF16)) for p in pairs]
        upd = [_dot((k[p] * kdec_ref[p]).T.astype(BF16), v[p].astype(BF16)) for p in pairs]

        o = [_dot((s[p] * intra_ref[p]).astype(BF16), _split_pair_rows(v[p], first)) + cross[p] for p in pairs]
        o_t = []
        for j in kvs:
            sink_lanes = jnp.concatenate(
                [jnp.full((1, WINDOW), sinks_ref[SWA_GROUP * j + g], F32) for g in range(SWA_GROUP)],
                axis=1)
            v_cat = jnp.concatenate([prevv_ref[j], v_tj[j]], axis=1)
            o_t.append(_softmax_sink_pv_t(s_t[j] * K_SCALE + bias_ref[is_first, j], sink_lanes, v_cat))
        for p in pairs:
            state_ref[p] = state[p] * sdec_ref[p] + jnp.where(bd_mask, upd[p], 0.0)
        for j in kvs:
            prevk_ref[j] = k_dup[j]
            prevv_ref[j] = v_tj[j]

        scale = [_head_rms_scale(o[p], ones_bd) for p in pairs]
        for p in pairs:
            g = main_ref[rows, cols(3 * RET_WIDTH, p)]
            mix_ref[rows, cols(0, p)] = (o[p] * scale[p] * _silu(g)).astype(BF16)
        for j in kvs:
            for half in range(2):
                pair_t = jnp.concatenate([o_t[j][:, (2 * half) * WINDOW:(2 * half + 1) * WINDOW],
                                          o_t[j][:, (2 * half + 1) * WINDOW:(2 * half + 2) * WINDOW]], axis=0)
                mix_ref[rows, cols(RET_WIDTH, 2 * j + half)] = pair_t.T.astype(BF16)
        return carry

    lax.fori_loop(0, PROMPT_TILE // RET_CHUNK, chunk, 0)

    @pl.when(t == pl.num_programs(1) - 1)
    def _():
        for p in range(N_PAIRS):
            s = state_ref[p]
            ret_ref[2 * p] = s[:HEAD_DIM, :HEAD_DIM]
            ret_ref[2 * p + 1] = s[HEAD_DIM:, HEAD_DIM:]
        last = slice(PROMPT_TILE - WINDOW, PROMPT_TILE)
        kwin_ref[...] = kn_ref[last, :].T
        vwin_ref[...] = vs_ref[last, :].T


def _prompt_mixer(sinks, main, qn, kn, vs, batch, seq):
    nt = seq // PROMPT_TILE
    row = lambda w: pl.BlockSpec((PROMPT_TILE, w), lambda b, t: (b * nt + t, 0))
    return pl.pallas_call(
        _prompt_mixer_kernel,
        grid=(batch, nt),
        in_specs=[pl.BlockSpec(memory_space=pltpu.SMEM),
                  row(MAIN_WIDTH), row(SWA_WIDTH), row(SWA_KV_WIDTH), row(SWA_KV_WIDTH)],
        out_specs=[row(MIX_WIDTH),
                   pl.BlockSpec((None, N_RET_HEADS, HEAD_DIM, HEAD_DIM), lambda b, t: (b, 0, 0, 0)),
                   pl.BlockSpec((None, SWA_KV_WIDTH, WINDOW), lambda b, t: (b, 0, 0)),
                   pl.BlockSpec((None, SWA_KV_WIDTH, WINDOW), lambda b, t: (b, 0, 0))],
        out_shape=[jax.ShapeDtypeStruct((batch * seq, MIX_WIDTH), BF16),
                   jax.ShapeDtypeStruct((batch, N_RET_HEADS, HEAD_DIM, HEAD_DIM), F32),
                   jax.ShapeDtypeStruct((batch, SWA_KV_WIDTH, WINDOW), F32),
                   jax.ShapeDtypeStruct((batch, SWA_KV_WIDTH, WINDOW), F32)],
        scratch_shapes=[
            pltpu.VMEM((N_PAIRS, LANES, LANES), F32),
            pltpu.VMEM((N_SWA_KV, WINDOW, LANES), BF16),
            pltpu.VMEM((N_SWA_KV, HEAD_DIM, WINDOW), BF16),
            pltpu.VMEM((N_PAIRS, LANES, 2 * LANES), F32),
            pltpu.VMEM((N_PAIRS, LANES, LANES), F32),
            pltpu.VMEM((N_PAIRS, LANES, LANES), F32),
            pltpu.VMEM((N_PAIRS, LANES, LANES), F32),
            pltpu.VMEM((2, N_SWA_KV, 2 * WINDOW, SWA_GROUP * WINDOW), F32),
        ],
        compiler_params=pltpu.CompilerParams(
            dimension_semantics=("arbitrary", "arbitrary"), vmem_limit_bytes=VMEM_LIMIT),
        name="prompt_mixer",
    )(sinks, main, qn, kn, vs)


def _decode_ret_kernel(dec_seq, nb, qdec_ref, kdec_ref, sdec_ref, intra_ref,
                       q_ref, k_ref, v_ref, g_ref, st_ref,
                       mix_ref, st_out_ref,
                       qt_ref, kt_ref, vt_ref, qdt_ref, kdt_ref, o_ref):
    pair = pl.program_id(0)
    halves = [slice(0, HEAD_DIM), slice(HEAD_DIM, 2 * HEAD_DIM)]
    for l in range(dec_seq):
        rows = pl.ds(l, nb, stride=dec_seq)
        q_t = q_ref[rows, :].T
        k_t = (k_ref[rows, :] * K_SCALE).T
        qt_ref[l] = q_t
        kt_ref[l] = k_t
        vt_ref[l] = v_ref[rows, :].T
        for hh in range(2):
            qdt_ref[l, halves[hh], :] = q_t[halves[hh]] * qdec_ref[2 * pair + hh, l]
            kdt_ref[l, halves[hh], :] = k_t[halves[hh]] * kdec_ref[2 * pair + hh, l]

    e_blk = HEAD_DIM // 2
    for hh in range(2):
        h = 2 * pair + hh
        hs = halves[hh]
        for l in range(dec_seq):
            acc = None
            for m in range(l + 1):
                sc = jnp.sum(qt_ref[l, hs, :] * kt_ref[m, hs, :], axis=0, keepdims=True) * intra_ref[h, l - m]
                term = sc * vt_ref[m, hs, :]
                acc = term if acc is None else acc + term
            o_ref[l, hs, :] = acc
        for eb in range(HEAD_DIM // e_blk):
            es = slice(eb * e_blk, (eb + 1) * e_blk)
            erows = slice(hh * HEAD_DIM + eb * e_blk, hh * HEAD_DIM + (eb + 1) * e_blk)

            def body(d, accs, hh=hh, h=h, es=es, erows=erows):
                s_d = st_ref[hh, d, es, :]
                row = pl.ds(hh * HEAD_DIM + d, 1)
                upd = s_d * sdec_ref[h]
                new_accs = []
                for l in range(dec_seq):
                    new_accs.append(accs[l] + qdt_ref[l, row, :] * s_d)
                    upd = upd + kdt_ref[l, row, :] * vt_ref[l, erows, :]
                st_out_ref[hh, d, es, :] = upd
                return tuple(new_accs)

            zero = jnp.zeros((e_blk, nb), F32)
            accs = lax.fori_loop(0, HEAD_DIM, body, tuple(zero for _ in range(dec_seq)))
            for l in range(dec_seq):
                o_ref[l, erows, :] = o_ref[l, erows, :] + accs[l]

    for l in range(dec_seq):
        o = o_ref[l]
        normed = []
        for hh in range(2):
            oh = o[halves[hh]]
            normed.append(oh * lax.rsqrt(jnp.mean(oh * oh, axis=0, keepdims=True) + EPS))
        rows = pl.ds(l, nb, stride=dec_seq)
        mix_ref[rows, :] = jnp.concatenate(normed, axis=0).T * _silu(g_ref[rows, :])


def _decode_ret(main, state_t, dec_seq):
    nb = state_t.shape[-1]
    m = main.shape[0]
    assert nb == LANES and m == nb * dec_seq
    steps = [j for j in range(dec_seq)]
    tab = lambda f: jnp.asarray([[f(h, j) for j in steps] for h in range(N_RET_HEADS)], F32)
    qdec = tab(lambda h, j: math.exp(LOG_DECAY[h] * (j + 1.0)))
    kdec = tab(lambda h, j: math.exp(LOG_DECAY[h] * (dec_seq - 1.0 - j)))
    intra = tab(lambda h, j: math.exp(LOG_DECAY[h] * j))
    sdec = jnp.asarray([math.exp(LOG_DECAY[h] * dec_seq) for h in range(N_RET_HEADS)], F32)
    smem = pl.BlockSpec(memory_space=pltpu.SMEM)
    col = lambda base: pl.BlockSpec((m, LANES), lambda p: (0, base + p))
    st_spec = pl.BlockSpec((2, HEAD_DIM, HEAD_DIM, nb), lambda p: (p, 0, 0, 0))
    stage = pltpu.VMEM((dec_seq, LANES, nb), F32)
    return pl.pallas_call(
        functools.partial(_decode_ret_kernel, dec_seq, nb),
        grid=(N_PAIRS,),
        in_specs=[smem, smem, smem, smem,
                  col(0), col(N_PAIRS), col(2 * N_PAIRS), col(3 * N_PAIRS), st_spec],
        out_specs=[pl.BlockSpec((m, LANES), lambda p: (0, p)), st_spec],
        out_shape=[jax.ShapeDtypeStruct((m, RET_WIDTH), F32),
                   jax.ShapeDtypeStruct(state_t.shape, F32)],
        scratch_shapes=[stage, stage, stage, stage, stage, stage],
        compiler_params=pltpu.CompilerParams(
            dimension_semantics=("arbitrary",), vmem_limit_bytes=VMEM_LIMIT),
        name="decode_ret",
    )(qdec, kdec, sdec, intra, main, main, main, main, state_t)


DEC_ROWS = 128
DEC_UNROLL = 4


def _decode_attn_consts(dec_seq, bias_ref):
    shift = dec_seq.bit_length() - 1
    rows = N_SWA_HEADS * dec_seq
    rb = _iota((rows, WINDOW), 0)
    cb = _iota((rows, WINDOW), 1)
    head = rb >> shift
    i = rb & (dec_seq - 1)
    slope = jnp.zeros((rows, WINDOW), F32)
    for h in range(N_SWA_HEADS):
        slope = jnp.where(head == h, ALIBI_SLOPES[h], slope)
    bias_ref[0] = jnp.where(cb > i, -(slope * (WINDOW + i - cb).astype(F32)), NEG_INF)
    m = cb & (dec_seq - 1)
    bias_ref[1] = jnp.where(m <= i, -(slope * (i - m).astype(F32)), NEG_INF)


def _decode_attn_kernel(dec_seq, sinks_ref, qn_ref, kn_ref, vs_ref, kt_ref, vt_ref,
                        mix_ref, kt_out_ref, vt_out_ref,
                        bias_ref, qbd_ref, oblk_ref, knew_ref, vnew_ref, knt_ref, vst_ref):
    @pl.when(pl.program_id(0) == 0)
    def _():
        _decode_attn_consts(dec_seq, bias_ref)

    first = _first_half()
    shift = dec_seq.bit_length() - 1

    kn_t = kn_ref[...].T
    vs_t = vs_ref[...].T
    knt_ref[...] = kn_t.astype(BF16)
    vst_ref[...] = vs_t.astype(BF16)
    for bb in range(DEC_GROUP):
        sh = (WINDOW - dec_seq - bb * dec_seq) % LANES
        knew_ref[bb] = pltpu.roll(kn_t, sh, axis=1) if sh else kn_t
        vnew_ref[bb] = pltpu.roll(vs_t, sh, axis=1) if sh else vs_t
    qn = qn_ref[...]
    qn_sw = pltpu.roll(qn, HEAD_DIM, axis=1)
    for h in range(N_SWA_HEADS):
        kv_half = h // SWA_GROUP
        if (h % 2) == kv_half:
            src = qn[:, (h // 2) * LANES:(h // 2 + 1) * LANES]
        else:
            col = (h + 1) // 2
            src = qn_sw[:, col * LANES:(col + 1) * LANES]
        qbd_ref[h] = jnp.where(first, src, 0.0) if kv_half == 0 else jnp.where(first, 0.0, src)

    sink_rows = jnp.concatenate(
        [jnp.full((dec_seq, LANES), sinks_ref[h], F32) for h in range(N_SWA_HEADS)], axis=0)
    col_batch = _iota((N_SWA_HEADS * dec_seq, LANES), 1) >> shift
    keep_old = _iota((1, LANES), 1) < WINDOW - dec_seq

    def per_batches(i, carry):
        bs = [i * DEC_UNROLL + u for u in range(DEC_UNROLL)]
        rows = [pl.ds(pl.multiple_of(b * dec_seq, dec_seq), dec_seq) for b in bs]
        k_old = [kt_ref[b] for b in bs]
        v_old = [vt_ref[b] for b in bs]
        q_st = [jnp.concatenate([qbd_ref[h, r, :] for h in range(N_SWA_HEADS)], axis=0).astype(BF16)
                for r in rows]
        s = [_dot(q_st[u], jnp.concatenate([k_old[u].astype(BF16), knt_ref[...]], axis=1))
             for u in range(DEC_UNROLL)]
        o = []
        for u, b in enumerate(bs):
            bias = jnp.concatenate([bias_ref[0], jnp.where(col_batch == b, bias_ref[1], NEG_INF)], axis=1)
            w_v = jnp.concatenate([v_old[u].astype(BF16), vst_ref[...]], axis=1)
            o.append(_softmax_sink_pv(s[u] * K_SCALE + bias, sink_rows, w_v))
        for u, b in enumerate(bs):
            for h in range(N_SWA_HEADS):
                oblk_ref[h, rows[u], :] = o[u][h * dec_seq:(h + 1) * dec_seq]
            kt_out_ref[b] = jnp.where(keep_old, pltpu.roll(k_old[u], LANES - dec_seq, axis=1), knew_ref[b])
            vt_out_ref[b] = jnp.where(keep_old, pltpu.roll(v_old[u], LANES - dec_seq, axis=1), vnew_ref[b])
        return carry

    lax.fori_loop(0, DEC_GROUP // DEC_UNROLL, per_batches, 0)

    y1 = jnp.where(first, oblk_ref[3], oblk_ref[4])
    moved = pltpu.roll(jnp.concatenate([oblk_ref[1], y1, oblk_ref[6], oblk_ref[6]], axis=1), HEAD_DIM, axis=1)
    outs = [
        jnp.where(first, oblk_ref[0], moved[:, 0:LANES]),
        jnp.where(first, oblk_ref[2], moved[:, LANES:2 * LANES]),
        jnp.where(first, moved[:, 2 * LANES:3 * LANES], oblk_ref[5]),
        jnp.where(first, moved[:, 3 * LANES:4 * LANES], oblk_ref[7]),
    ]
    for c in range(SWA_WIDTH // LANES):
        mix_ref[:, c * LANES:(c + 1) * LANES] = outs[c].astype(BF16)


def _decode_attn(sinks, qn, kn, vs, k_t, v_t, dec_seq):
    nb = k_t.shape[0]
    assert DEC_GROUP * dec_seq == DEC_ROWS and nb % DEC_GROUP == 0 and dec_seq & (dec_seq - 1) == 0
    assert k_t.shape[1:] == (SWA_KV_WIDTH, WINDOW)
    row = lambda w: pl.BlockSpec((DEC_ROWS, w), lambda i: (i, 0))
    cache = pl.BlockSpec((DEC_GROUP, SWA_KV_WIDTH, WINDOW), lambda i: (i, 0, 0))
    return pl.pallas_call(
        functools.partial(_decode_attn_kernel, dec_seq),
        grid=(nb // DEC_GROUP,),
        in_specs=[pl.BlockSpec(memory_space=pltpu.SMEM),
                  row(SWA_WIDTH), row(SWA_KV_WIDTH), row(SWA_KV_WIDTH), cache, cache],
        out_specs=[row(SWA_WIDTH), cache, cache],
        out_shape=[jax.ShapeDtypeStruct((nb * dec_seq, SWA_WIDTH), BF16),
                   jax.ShapeDtypeStruct(k_t.shape, F32),
                   jax.ShapeDtypeStruct(v_t.shape, F32)],
        scratch_shapes=[
            pltpu.VMEM((2, N_SWA_HEADS * dec_seq, WINDOW), F32),
            pltpu.VMEM((N_SWA_HEADS, DEC_ROWS, LANES), F32),
            pltpu.VMEM((N_SWA_HEADS, DEC_ROWS, LANES), F32),
            pltpu.VMEM((DEC_GROUP, SWA_KV_WIDTH, LANES), F32),
            pltpu.VMEM((DEC_GROUP, SWA_KV_WIDTH, LANES), F32),
            pltpu.VMEM((SWA_KV_WIDTH, DEC_ROWS), BF16),
            pltpu.VMEM((SWA_KV_WIDTH, DEC_ROWS), BF16),
        ],
        compiler_params=pltpu.CompilerParams(
            dimension_semantics=("arbitrary",), vmem_limit_bytes=VMEM_LIMIT),
        name="decode_attn",
    )(sinks, qn, kn, vs, k_t, v_t)


def kernel(x_prompt, x_sample, state_ret, cache_swa_k, cache_swa_v, norm_mix_gain, w_in, q_norm_gain,
           k_norm_gain, attn_sinks, w_out, norm_ffn_gain, w_up, w_down):
    batch, seq, d = x_prompt.shape
    nb, dec_seq, _ = x_sample.shape
    wb = cache_swa_k.shape[1]
    assert d == D_MODEL and seq % PROMPT_TILE == 0 and wb == WINDOW

    w_in_bf = w_in.astype(BF16)
    w_out_bf = w_out.astype(BF16)
    w_up_bf = w_up.astype(BF16)
    w_down_bf = w_down.astype(BF16)
    gain_mix = norm_mix_gain.reshape(1, D_MODEL)
    gain_ffn = norm_ffn_gain.reshape(1, D_MODEL)
    qg = jnp.tile(q_norm_gain, 2).reshape(1, LANES)
    kg = jnp.tile(k_norm_gain, 2).reshape(1, LANES)

    xp = x_prompt.reshape(batch * seq, D_MODEL)
    main_p, qn_p, kn_p, vs_p = _in_proj(xp, gain_mix, w_in_bf, qg, kg)
    mix_p, ret_p, kwin_t, vwin_t = _prompt_mixer(attn_sinks, main_p, qn_p, kn_p, vs_p, batch, seq)
    y_p = _out_mlp(mix_p, 0, mix_p, 1, xp, w_out_bf, gain_ffn, w_up_bf, w_down_bf).reshape(batch, seq, D_MODEL)

    def from_key_minor(a_t):
        return jnp.transpose(a_t.reshape(a_t.shape[0], N_SWA_KV, HEAD_DIM, WINDOW), (0, 3, 1, 2))

    def to_key_minor(a):
        return jnp.transpose(a, (0, 2, 3, 1)).reshape(a.shape[0], SWA_KV_WIDTH, WINDOW)

    xs = x_sample.reshape(nb * dec_seq, D_MODEL)
    main_s, qn_s, kn_s, vs_s = _in_proj(xs, gain_mix, w_in_bf, qg, kg)
    mix_ret_s, state_t = _decode_ret(main_s, jnp.transpose(state_ret, (1, 2, 3, 0)), dec_seq)
    mix_swa_s, k_t, v_t = _decode_attn(attn_sinks, qn_s, kn_s, vs_s,
                                       to_key_minor(cache_swa_k), to_key_minor(cache_swa_v), dec_seq)
    y_s = _out_mlp(mix_ret_s, 0, mix_swa_s, 0, xs, w_out_bf, gain_ffn, w_up_bf, w_down_bf)

    return (y_p, y_s.reshape(nb, dec_seq, D_MODEL), ret_p, from_key_minor(kwin_t), from_key_minor(vwin_t),
            jnp.transpose(state_t, (3, 0, 1, 2)), from_key_minor(k_t), from_key_minor(v_t))
```

```python
import functools
import math

import jax
import jax.numpy as jnp
from jax import lax
from jax.experimental import pallas as pl
from jax.experimental.pallas import tpu as pltpu

F32 = jnp.float32
BF16 = jnp.bfloat16

D_MODEL = 1024
HEAD_DIM = 64
N_RET_HEADS = 8
N_SWA_HEADS = 8
N_SWA_KV = 2
SWA_GROUP = N_SWA_HEADS // N_SWA_KV
RET_WIDTH = N_RET_HEADS * HEAD_DIM
SWA_WIDTH = N_SWA_HEADS * HEAD_DIM
SWA_KV_WIDTH = N_SWA_KV * HEAD_DIM
MAIN_WIDTH = 4 * RET_WIDTH
IN_WIDTH = MAIN_WIDTH + SWA_WIDTH + 2 * SWA_KV_WIDTH
MIX_WIDTH = RET_WIDTH + SWA_WIDTH
D_FF = 4 * D_MODEL
WINDOW = 128
RET_CHUNK = 128
EPS = 1e-6
NEG_INF = -1e30

LANES = 128
N_PAIRS = N_RET_HEADS // 2
LOG_DECAY = [math.log(1.0 - 2.0 ** (-5.0 - h)) for h in range(N_RET_HEADS)]
ALIBI_SLOPES = [2.0 ** (-8.0 * (h + 1) / N_SWA_HEADS) for h in range(N_SWA_HEADS)]
K_SCALE = HEAD_DIM ** -0.5

ROW_TILE = 512
PROMPT_TILE = 512
DEC_GROUP = 16
VMEM_LIMIT = 56 * 1024 * 1024


def _dot(a, b):
    return jnp.dot(a, b, preferred_element_type=F32)


def _dot_nt(a, b):
    return lax.dot_general(a, b, (((1,), (1,)), ((), ())), preferred_element_type=F32)


def _iota(shape, dim):
    return lax.broadcasted_iota(jnp.int32, shape, dim)


def _ones_block_diag():
    same = (_iota((LANES, LANES), 0) >> 6) == (_iota((LANES, LANES), 1) >> 6)
    return jnp.where(same, 1.0, 0.0).astype(BF16)


def _head_sumsq(x, ones_bd):
    x2 = x * x
    hi = x2.astype(BF16)
    lo = (x2 - hi.astype(F32)).astype(BF16)
    return _dot(hi, ones_bd) + _dot(lo, ones_bd)


def _head_rms_scale(x, ones_bd):
    return lax.rsqrt(_head_sumsq(x, ones_bd) * (1.0 / HEAD_DIM) + EPS)


def _first_half():
    return _iota((1, LANES), 1) < HEAD_DIM


def _pair_const(values, pair, lane_is_second):
    return jnp.where(lane_is_second, values[2 * pair + 1], values[2 * pair]).astype(F32)


def _silu(g):
    return g * (1.0 / (1.0 + jnp.exp(-g)))


def _in_proj_kernel(x_ref, gain_ref, w_ref, qg_ref, kg_ref, main_ref, qn_ref, kn_ref, vs_ref):
    x = x_ref[...]
    ms = jnp.mean(x * x, axis=-1, keepdims=True)
    hb = ((x * lax.rsqrt(ms + EPS)) * gain_ref[...]).astype(BF16)
    ones_bd = _ones_block_diag()
    for c in range(MAIN_WIDTH // 512):
        main_ref[:, c * 512:(c + 1) * 512] = _dot(hb, w_ref[:, c * 512:(c + 1) * 512])
    qs = _dot(hb, w_ref[:, MAIN_WIDTH:MAIN_WIDTH + SWA_WIDTH])
    for c in range(SWA_WIDTH // LANES):
        xc = qs[:, c * LANES:(c + 1) * LANES]
        qn_ref[:, c * LANES:(c + 1) * LANES] = (xc * _head_rms_scale(xc, ones_bd)) * qg_ref[...]
    k0 = MAIN_WIDTH + SWA_WIDTH
    ks = _dot(hb, w_ref[:, k0:k0 + SWA_KV_WIDTH])
    kn_ref[...] = (ks * _head_rms_scale(ks, ones_bd)) * kg_ref[...]
    vs_ref[...] = _dot(hb, w_ref[:, k0 + SWA_KV_WIDTH:k0 + 2 * SWA_KV_WIDTH])


def _in_proj(x2d, gain, w_in_bf, qg, kg):
    m = x2d.shape[0]
    tm = min(ROW_TILE, m)
    row = lambda w: pl.BlockSpec((tm, w), lambda i: (i, 0))
    full = lambda a: pl.BlockSpec(a.shape, lambda i: (0, 0), pipeline_mode=pl.Buffered(1))
    return pl.pallas_call(
        _in_proj_kernel,
        grid=(m // tm,),
        in_specs=[row(D_MODEL), full(gain), full(w_in_bf), full(qg), full(kg)],
        out_specs=[row(MAIN_WIDTH), row(SWA_WIDTH), row(SWA_KV_WIDTH), row(SWA_KV_WIDTH)],
        out_shape=[jax.ShapeDtypeStruct((m, MAIN_WIDTH), F32),
                   jax.ShapeDtypeStruct((m, SWA_WIDTH), F32),
                   jax.ShapeDtypeStruct((m, SWA_KV_WIDTH), F32),
                   jax.ShapeDtypeStruct((m, SWA_KV_WIDTH), F32)],
        compiler_params=pltpu.CompilerParams(
            dimension_semantics=("arbitrary",), vmem_limit_bytes=VMEM_LIMIT),
        name="in_proj",
    )(x2d, gain, w_in_bf, qg, kg)


FF_CHUNK = 1024


def _out_mlp_kernel(mix_ret_ref, mix_swa_ref, x_ref, w_out_ref, gain_ref, w_up_ref, w_down_ref, y_ref):
    h = x_ref[...] + (_dot(mix_ret_ref[...].astype(BF16), w_out_ref[:RET_WIDTH, :])
                      + _dot(mix_swa_ref[...].astype(BF16), w_out_ref[RET_WIDTH:, :]))
    ms = jnp.mean(h * h, axis=-1, keepdims=True)
    hf = ((h * lax.rsqrt(ms + EPS)) * gain_ref[...]).astype(BF16)
    ff = None
    for c in range(D_FF // FF_CHUNK):
        u = _dot(hf, w_up_ref[:, c * FF_CHUNK:(c + 1) * FF_CHUNK])
        a = jnp.maximum(u, 0.0)
        d = _dot((a * a).astype(BF16), w_down_ref[c * FF_CHUNK:(c + 1) * FF_CHUNK, :])
        ff = d if ff is None else ff + d
    y_ref[...] = h + ff


def _out_mlp(mix_ret, ret_block, mix_swa, swa_block, x2d, w_out_bf, gain, w_up_bf, w_down_bf):
    m = x2d.shape[0]
    tm = min(ROW_TILE, m)
    row = lambda w: pl.BlockSpec((tm, w), lambda i: (i, 0))
    col_block = lambda c: pl.BlockSpec((tm, RET_WIDTH), lambda i: (i, c))
    full = lambda a: pl.BlockSpec(a.shape, lambda i: (0, 0), pipeline_mode=pl.Buffered(1))
    return pl.pallas_call(
        _out_mlp_kernel,
        grid=(m // tm,),
        in_specs=[col_block(ret_block), col_block(swa_block), row(D_MODEL), full(w_out_bf), full(gain),
                  full(w_up_bf), full(w_down_bf)],
        out_specs=row(D_MODEL),
        out_shape=jax.ShapeDtypeStruct((m, D_MODEL), F32),
        compiler_params=pltpu.CompilerParams(
            dimension_semantics=("arbitrary",), vmem_limit_bytes=VMEM_LIMIT),
        name="out_mlp",
    )(mix_ret, mix_swa, x2d, w_out_bf, gain, w_up_bf, w_down_bf)


def _split_pair_rows(x, first):
    return jnp.concatenate([jnp.where(first, x, 0.0), jnp.where(first, 0.0, x)], axis=0).astype(BF16)


def _softmax_sink_pv(s, sink_wide, v_t):
    m = jnp.maximum(jnp.max(s, axis=-1, keepdims=True), sink_wide)
    p = jnp.exp(s - jnp.concatenate([m, m], axis=1))
    denom = jnp.sum(p, axis=-1, keepdims=True) + jnp.exp(sink_wide - m)
    return _dot_nt(p.astype(BF16), v_t) / denom


def _softmax_sink_pv_t(s_t, sink_lanes, v_t):
    m = jnp.maximum(jnp.max(s_t, axis=0, keepdims=True), sink_lanes)
    p = jnp.exp(s_t - m)
    denom = jnp.sum(p, axis=0, keepdims=True) + jnp.exp(sink_lanes - m)
    return _dot(v_t, p.astype(BF16)) / denom


def _prompt_consts(intra_ref, qdec_ref, kdec_ref, sdec_ref, bias_ref):
    r = _iota((LANES, LANES), 0)
    lane2 = _iota((LANES, LANES), 1) >= HEAD_DIM
    rf = r.astype(F32)
    ri = _iota((LANES, 2 * LANES), 0)
    ci = _iota((LANES, 2 * LANES), 1)
    diff = (ri - (ci & (LANES - 1))).astype(F32)
    for p in range(N_PAIRS):
        lg = _pair_const(LOG_DECAY, p, lane2)
        qdec_ref[p] = jnp.exp(lg * (rf + 1.0))
        kdec_ref[p] = jnp.exp(lg * (RET_CHUNK - 1.0 - rf))
        sdec_ref[p] = jnp.exp(_pair_const(LOG_DECAY, p, r >= HEAD_DIM) * float(RET_CHUNK))
        lg2 = _pair_const(LOG_DECAY, p, ci >= LANES)
        intra_ref[p] = jnp.where(diff >= 0.0, jnp.exp(lg2 * jnp.maximum(diff, 0.0)), 0.0)
    cols = SWA_GROUP * WINDOW
    kb = _iota((2 * WINDOW, cols), 0)
    cb = _iota((2 * WINDOW, cols), 1)
    grp = cb >> 7
    dist = WINDOW + (cb & (WINDOW - 1)) - kb
    valid = (dist >= 0) & (dist < WINDOW)
    distf = dist.astype(F32)
    for j in range(N_SWA_KV):
        sl = [ALIBI_SLOPES[SWA_GROUP * j + g] for g in range(SWA_GROUP)]
        slope = jnp.where(grp == 0, sl[0], jnp.where(grp == 1, sl[1], jnp.where(grp == 2, sl[2], sl[3])))
        b = jnp.where(valid, -(slope.astype(F32) * distf), NEG_INF)
        bias_ref[0, j] = b
        bias_ref[1, j] = jnp.where(kb >= WINDOW, b, NEG_INF)


def _prompt_layer_kernel(sinks_ref, x_ref, gain_mix_ref, w_in_ref, qg_ref, kg_ref,
                         w_out_ref, gain_ffn_ref, w_up_ref, w_down_ref,
                         y_ref, ret_ref, kwin_ref, vwin_ref,
                         main_ref, qn_ref, kn_ref, vs_ref, mix_ref,
                         state_ref, prevk_ref, prevv_ref,
                         intra_ref, qdec_ref, kdec_ref, sdec_ref, bias_ref):
    t = pl.program_id(1)

    @pl.when((pl.program_id(0) == 0) & (t == 0))
    def _():
        _prompt_consts(intra_ref, qdec_ref, kdec_ref, sdec_ref, bias_ref)

    _in_proj_kernel(x_ref, gain_mix_ref, w_in_ref, qg_ref, kg_ref, main_ref, qn_ref, kn_ref, vs_ref)

    @pl.when(t == 0)
    def _():
        state_ref[...] = jnp.zeros_like(state_ref)
        prevk_ref[...] = jnp.zeros_like(prevk_ref)
        prevv_ref[...] = jnp.zeros_like(prevv_ref)

    first = _first_half()
    ones_bd = _ones_block_diag()
    bd_mask = (_iota((LANES, LANES), 0) >= HEAD_DIM) == (_iota((LANES, LANES), 1) >= HEAD_DIM)

    def chunk(c, carry):
        r0 = pl.multiple_of(c * RET_CHUNK, RET_CHUNK)
        rows = pl.ds(r0, RET_CHUNK)
        pairs = range(N_PAIRS)
        kvs = range(N_SWA_KV)
        cols = lambda base, p: slice(base + p * LANES, base + (p + 1) * LANES)
        q = [main_ref[rows, cols(0, p)] for p in pairs]
        k = [main_ref[rows, cols(RET_WIDTH, p)] * K_SCALE for p in pairs]
        v = [main_ref[rows, cols(2 * RET_WIDTH, p)] for p in pairs]
        state = [state_ref[p] for p in pairs]
        kc = kn_ref[rows, :]
        k_sw = pltpu.roll(kc, HEAD_DIM, axis=1)
        v_t = vs_ref[rows, :].T
        is_first = ((t == 0) & (c == 0)).astype(jnp.int32)
        k_dup = [(jnp.where(first, kc, k_sw) if j == 0 else jnp.where(first, k_sw, kc)).astype(BF16)
                 for j in kvs]
        v_tj = [v_t[j * HEAD_DIM:(j + 1) * HEAD_DIM].astype(BF16) for j in kvs]
        q_st = []
        for j in kvs:
            pieces = []
            for g in range(SWA_GROUP):
                qc = qn_ref[rows, cols(0, 2 * j + g // 2)]
                pieces.append(jnp.where(first, qc, 0.0) if g % 2 == 0 else jnp.where(first, 0.0, qc))
            q_st.append(jnp.concatenate(pieces, axis=0).astype(BF16))

        s = [_dot_nt(q[p].astype(BF16), _split_pair_rows(k[p], first)) for p in pairs]
        s_t = [_dot_nt(jnp.concatenate([prevk_ref[j], k_dup[j]], axis=0), q_st[j]) for j in kvs]
        cross = [_dot((q[p] * qdec_ref[p]).astype(BF16), state[p].astype(BF16)) for p in pairs]
        upd = [_dot((k[p] * kdec_ref[p]).T.astype(BF16), v[p].astype(BF16)) for p in pairs]

        o = [_dot((s[p] * intra_ref[p]).astype(BF16), _split_pair_rows(v[p], first)) + cross[p] for p in pairs]
        o_t = []
        for j in kvs:
            sink_lanes = jnp.concatenate(
                [jnp.full((1, WINDOW), sinks_ref[SWA_GROUP * j + g], F32) for g in range(SWA_GROUP)],
                axis=1)
            v_cat = jnp.concatenate([prevv_ref[j], v_tj[j]], axis=1)
            o_t.append(_softmax_sink_pv_t(s_t[j] * K_SCALE + bias_ref[is_first, j], sink_lanes, v_cat))
        for p in pairs:
            state_ref[p] = state[p] * sdec_ref[p] + jnp.where(bd_mask, upd[p], 0.0)
        for j in kvs:
            prevk_ref[j] = k_dup[j]
            prevv_ref[j] = v_tj[j]

        scale = [_head_rms_scale(o[p], ones_bd) for p in pairs]
        for p in pairs:
            g = main_ref[rows, cols(3 * RET_WIDTH, p)]
            mix_ref[rows, cols(0, p)] = (o[p] * scale[p] * _silu(g)).astype(BF16)
        for j in kvs:
            for half in range(2):
                pair_t = jnp.concatenate([o_t[j][:, (2 * half) * WINDOW:(2 * half + 1) * WINDOW],
                                          o_t[j][:, (2 * half + 1) * WINDOW:(2 * half + 2) * WINDOW]], axis=0)
                mix_ref[rows, cols(RET_WIDTH, 2 * j + half)] = pair_t.T.astype(BF16)
        return carry

    lax.fori_loop(0, PROMPT_TILE // RET_CHUNK, chunk, 0)

    @pl.when(t == pl.num_programs(1) - 1)
    def _():
        for p in range(N_PAIRS):
            s = state_ref[p]
            ret_ref[2 * p] = s[:HEAD_DIM, :HEAD_DIM]
            ret_ref[2 * p + 1] = s[HEAD_DIM:, HEAD_DIM:]
        last = slice(PROMPT_TILE - WINDOW, PROMPT_TILE)
        kwin_ref[...] = kn_ref[last, :].T
        vwin_ref[...] = vs_ref[last, :].T

    _out_mlp_kernel(mix_ref.at[:, pl.ds(0, RET_WIDTH)], mix_ref.at[:, pl.ds(RET_WIDTH, SWA_WIDTH)], x_ref,
                    w_out_ref, gain_ffn_ref, w_up_ref, w_down_ref, y_ref)


def _prompt_layer(sinks, x2d, gain_mix, w_in_bf, qg, kg, w_out_bf, gain_ffn, w_up_bf, w_down_bf, batch, seq):
    nt = seq // PROMPT_TILE
    row = lambda w: pl.BlockSpec((PROMPT_TILE, w), lambda b, t: (b * nt + t, 0))
    full = lambda a: pl.BlockSpec(a.shape, lambda b, t: (0, 0), pipeline_mode=pl.Buffered(1))
    return pl.pallas_call(
        _prompt_layer_kernel,
        grid=(batch, nt),
        in_specs=[pl.BlockSpec(memory_space=pltpu.SMEM), row(D_MODEL),
                  full(gain_mix), full(w_in_bf), full(qg), full(kg),
                  full(w_out_bf), full(gain_ffn), full(w_up_bf), full(w_down_bf)],
        out_specs=[row(D_MODEL),
                   pl.BlockSpec((None, N_RET_HEADS, HEAD_DIM, HEAD_DIM), lambda b, t: (b, 0, 0, 0)),
                   pl.BlockSpec((None, SWA_KV_WIDTH, WINDOW), lambda b, t: (b, 0, 0)),
                   pl.BlockSpec((None, SWA_KV_WIDTH, WINDOW), lambda b, t: (b, 0, 0))],
        out_shape=[jax.ShapeDtypeStruct((batch * seq, D_MODEL), F32),
                   jax.ShapeDtypeStruct((batch, N_RET_HEADS, HEAD_DIM, HEAD_DIM), F32),
                   jax.ShapeDtypeStruct((batch, SWA_KV_WIDTH, WINDOW), F32),
                   jax.ShapeDtypeStruct((batch, SWA_KV_WIDTH, WINDOW), F32)],
        scratch_shapes=[
            pltpu.VMEM((PROMPT_TILE, MAIN_WIDTH), F32),
            pltpu.VMEM((PROMPT_TILE, SWA_WIDTH), F32),
            pltpu.VMEM((PROMPT_TILE, SWA_KV_WIDTH), F32),
            pltpu.VMEM((PROMPT_TILE, SWA_KV_WIDTH), F32),
            pltpu.VMEM((PROMPT_TILE, MIX_WIDTH), BF16),
            pltpu.VMEM((N_PAIRS, LANES, LANES), F32),
            pltpu.VMEM((N_SWA_KV, WINDOW, LANES), BF16),
            pltpu.VMEM((N_SWA_KV, HEAD_DIM, WINDOW), BF16),
            pltpu.VMEM((N_PAIRS, LANES, 2 * LANES), F32),
            pltpu.VMEM((N_PAIRS, LANES, LANES), F32),
            pltpu.VMEM((N_PAIRS, LANES, LANES), F32),
            pltpu.VMEM((N_PAIRS, LANES, LANES), F32),
            pltpu.VMEM((2, N_SWA_KV, 2 * WINDOW, SWA_GROUP * WINDOW), F32),
        ],
        compiler_params=pltpu.CompilerParams(
            dimension_semantics=("arbitrary", "arbitrary"), vmem_limit_bytes=VMEM_LIMIT),
        name="prompt_layer",
    )(sinks, x2d, gain_mix, w_in_bf, qg, kg, w_out_bf, gain_ffn, w_up_bf, w_down_bf)


def _decode_ret_kernel(dec_seq, nb, qdec_ref, kdec_ref, sdec_ref, intra_ref,
                       q_ref, k_ref, v_ref, g_ref, st_ref,
                       mix_ref, st_out_ref,
                       qt_ref, kt_ref, vt_ref, qdt_ref, kdt_ref, o_ref):
    pair = pl.program_id(0)
    halves = [slice(0, HEAD_DIM), slice(HEAD_DIM, 2 * HEAD_DIM)]
    for l in range(dec_seq):
        rows = pl.ds(l, nb, stride=dec_seq)
        q_t = q_ref[rows, :].T
        k_t = (k_ref[rows, :] * K_SCALE).T
        qt_ref[l] = q_t
        kt_ref[l] = k_t
        vt_ref[l] = v_ref[rows, :].T
        for hh in range(2):
            qdt_ref[l, halves[hh], :] = q_t[halves[hh]] * qdec_ref[2 * pair + hh, l]
            kdt_ref[l, halves[hh], :] = k_t[halves[hh]] * kdec_ref[2 * pair + hh, l]

    e_blk = HEAD_DIM // 2
    for hh in range(2):
        h = 2 * pair + hh
        hs = halves[hh]
        for l in range(dec_seq):
            acc = None
            for m in range(l + 1):
                sc = jnp.sum(qt_ref[l, hs, :] * kt_ref[m, hs, :], axis=0, keepdims=True) * intra_ref[h, l - m]
                term = sc * vt_ref[m, hs, :]
                acc = term if acc is None else acc + term
            o_ref[l, hs, :] = acc
        for eb in range(HEAD_DIM // e_blk):
            es = slice(eb * e_blk, (eb + 1) * e_blk)
            erows = slice(hh * HEAD_DIM + eb * e_blk, hh * HEAD_DIM + (eb + 1) * e_blk)

            def body(d, accs, hh=hh, h=h, es=es, erows=erows):
                s_d = st_ref[hh, d, es, :]
                row = pl.ds(hh * HEAD_DIM + d, 1)
                upd = s_d * sdec_ref[h]
                new_accs = []
                for l in range(dec_seq):
                    new_accs.append(accs[l] + qdt_ref[l, row, :] * s_d)
                    upd = upd + kdt_ref[l, row, :] * vt_ref[l, erows, :]
                st_out_ref[hh, d, es, :] = upd
                return tuple(new_accs)

            zero = jnp.zeros((e_blk, nb), F32)
            accs = lax.fori_loop(0, HEAD_DIM, body, tuple(zero for _ in range(dec_seq)))
            for l in range(dec_seq):
                o_ref[l, erows, :] = o_ref[l, erows, :] + accs[l]

    for l in range(dec_seq):
        o = o_ref[l]
        normed = []
        for hh in range(2):
            oh = o[halves[hh]]
            normed.append(oh * lax.rsqrt(jnp.mean(oh * oh, axis=0, keepdims=True) + EPS))
        rows = pl.ds(l, nb, stride=dec_seq)
        mix_ref[rows, :] = jnp.concatenate(normed, axis=0).T * _silu(g_ref[rows, :])


def _decode_ret(main, state_t, dec_seq):
    nb = state_t.shape[-1]
    m = main.shape[0]
    assert nb == LANES and m == nb * dec_seq
    steps = [j for j in range(dec_seq)]
    tab = lambda f: jnp.asarray([[f(h, j) for j in steps] for h in range(N_RET_HEADS)], F32)
    qdec = tab(lambda h, j: math.exp(LOG_DECAY[h] * (j + 1.0)))
    kdec = tab(lambda h, j: math.exp(LOG_DECAY[h] * (dec_seq - 1.0 - j)))
    intra = tab(lambda h, j: math.exp(LOG_DECAY[h] * j))
    sdec = jnp.asarray([math.exp(LOG_DECAY[h] * dec_seq) for h in range(N_RET_HEADS)], F32)
    smem = pl.BlockSpec(memory_space=pltpu.SMEM)
    col = lambda base: pl.BlockSpec((m, LANES), lambda p: (0, base + p))
    st_spec = pl.BlockSpec((2, HEAD_DIM, HEAD_DIM, nb), lambda p: (p, 0, 0, 0))
    stage = pltpu.VMEM((dec_seq, LANES, nb), F32)
    return pl.pallas_call(
        functools.partial(_decode_ret_kernel, dec_seq, nb),
        grid=(N_PAIRS,),
        in_specs=[smem, smem, smem, smem,
                  col(0), col(N_PAIRS), col(2 * N_PAIRS), col(3 * N_PAIRS), st_spec],
        out_specs=[pl.BlockSpec((m, LANES), lambda p: (0, p)), st_spec],
        out_shape=[jax.ShapeDtypeStruct((m, RET_WIDTH), F32),
                   jax.ShapeDtypeStruct(state_t.shape, F32)],
        scratch_shapes=[stage, stage, stage, stage, stage, stage],
        compiler_params=pltpu.CompilerParams(
            dimension_semantics=("arbitrary",), vmem_limit_bytes=VMEM_LIMIT),
        name="decode_ret",
    )(qdec, kdec, sdec, intra, main, main, main, main, state_t)


DEC_ROWS = 128
DEC_UNROLL = 4


def _decode_attn_consts(dec_seq, bias_ref):
    shift = dec_seq.bit_length() - 1
    rows = N_SWA_HEADS * dec_seq
    rb = _iota((rows, WINDOW), 0)
    cb = _iota((rows, WINDOW), 1)
    head = rb >> shift
    i = rb & (dec_seq - 1)
    slope = jnp.zeros((rows, WINDOW), F32)
    for h in range(N_SWA_HEADS):
        slope = jnp.where(head == h, ALIBI_SLOPES[h], slope)
    bias_ref[0] = jnp.where(cb > i, -(slope * (WINDOW + i - cb).astype(F32)), NEG_INF)
    m = cb & (dec_seq - 1)
    bias_ref[1] = jnp.where(m <= i, -(slope * (i - m).astype(F32)), NEG_INF)


def _decode_attn_kernel(dec_seq, sinks_ref, qn_ref, kn_ref, vs_ref, kt_ref, vt_ref,
                        mix_ref, kt_out_ref, vt_out_ref,
                        bias_ref, qbd_ref, oblk_ref, knew_ref, vnew_ref, knt_ref, vst_ref):
    @pl.when(pl.program_id(0) == 0)
    def _():
        _decode_attn_consts(dec_seq, bias_ref)

    first = _first_half()
    shift = dec_seq.bit_length() - 1

    kn_t = kn_ref[...].T
    vs_t = vs_ref[...].T
    knt_ref[...] = kn_t.astype(BF16)
    vst_ref[...] = vs_t.astype(BF16)
    for bb in range(DEC_GROUP):
        sh = (WINDOW - dec_seq - bb * dec_seq) % LANES
        knew_ref[bb] = pltpu.roll(kn_t, sh, axis=1) if sh else kn_t
        vnew_ref[bb] = pltpu.roll(vs_t, sh, axis=1) if sh else vs_t
    qn = qn_ref[...]
    qn_sw = pltpu.roll(qn, HEAD_DIM, axis=1)
    for h in range(N_SWA_HEADS):
        kv_half = h // SWA_GROUP
        if (h % 2) == kv_half:
            src = qn[:, (h // 2) * LANES:(h // 2 + 1) * LANES]
        else:
            col = (h + 1) // 2
            src = qn_sw[:, col * LANES:(col + 1) * LANES]
        qbd_ref[h] = jnp.where(first, src, 0.0) if kv_half == 0 else jnp.where(first, 0.0, src)

    sink_rows = jnp.concatenate(
        [jnp.full((dec_seq, LANES), sinks_ref[h], F32) for h in range(N_SWA_HEADS)], axis=0)
    col_batch = _iota((N_SWA_HEADS * dec_seq, LANES), 1) >> shift
    keep_old = _iota((1, LANES), 1) < WINDOW - dec_seq

    def per_batches(i, carry):
        bs = [i * DEC_UNROLL + u for u in range(DEC_UNROLL)]
        rows = [pl.ds(pl.multiple_of(b * dec_seq, dec_seq), dec_seq) for b in bs]
        k_old = [kt_ref[b] for b in bs]
        v_old = [vt_ref[b] for b in bs]
        q_st = [jnp.concatenate([qbd_ref[h, r, :] for h in range(N_SWA_HEADS)], axis=0).astype(BF16)
                for r in rows]
        s = [_dot(q_st[u], jnp.concatenate([k_old[u].astype(BF16), knt_ref[...]], axis=1))
             for u in range(DEC_UNROLL)]
        o = []
        for u, b in enumerate(bs):
            bias = jnp.concatenate([bias_ref[0], jnp.where(col_batch == b, bias_ref[1], NEG_INF)], axis=1)
            w_v = jnp.concatenate([v_old[u].astype(BF16), vst_ref[...]], axis=1)
            o.append(_softmax_sink_pv(s[u] * K_SCALE + bias, sink_rows, w_v))
        for u, b in enumerate(bs):
            for h in range(N_SWA_HEADS):
                oblk_ref[h, rows[u], :] = o[u][h * dec_seq:(h + 1) * dec_seq]
            kt_out_ref[b] = jnp.where(keep_old, pltpu.roll(k_old[u], LANES - dec_seq, axis=1), knew_ref[b])
            vt_out_ref[b] = jnp.where(keep_old, pltpu.roll(v_old[u], LANES - dec_seq, axis=1), vnew_ref[b])
        return carry

    lax.fori_loop(0, DEC_GROUP // DEC_UNROLL, per_batches, 0)

    y1 = jnp.where(first, oblk_ref[3], oblk_ref[4])
    moved = pltpu.roll(jnp.concatenate([oblk_ref[1], y1, oblk_ref[6], oblk_ref[6]], axis=1), HEAD_DIM, axis=1)
    outs = [
        jnp.where(first, oblk_ref[0], moved[:, 0:LANES]),
        jnp.where(first, oblk_ref[2], moved[:, LANES:2 * LANES]),
        jnp.where(first, moved[:, 2 * LANES:3 * LANES], oblk_ref[5]),
        jnp.where(first, moved[:, 3 * LANES:4 * LANES], oblk_ref[7]),
    ]
    for c in range(SWA_WIDTH // LANES):
        mix_ref[:, c * LANES:(c + 1) * LANES] = outs[c].astype(BF16)


def _decode_attn(sinks, qn, kn, vs, k_t, v_t, dec_seq):
    nb = k_t.shape[0]
    assert DEC_GROUP * dec_seq == DEC_ROWS and nb % DEC_GROUP == 0 and dec_seq & (dec_seq - 1) == 0
    assert k_t.shape[1:] == (SWA_KV_WIDTH, WINDOW)
    row = lambda w: pl.BlockSpec((DEC_ROWS, w), lambda i: (i, 0))
    cache = pl.BlockSpec((DEC_GROUP, SWA_KV_WIDTH, WINDOW), lambda i: (i, 0, 0))
    return pl.pallas_call(
        functools.partial(_decode_attn_kernel, dec_seq),
        grid=(nb // DEC_GROUP,),
        in_specs=[pl.BlockSpec(memory_space=pltpu.SMEM),
                  row(SWA_WIDTH), row(SWA_KV_WIDTH), row(SWA_KV_WIDTH), cache, cache],
        out_specs=[row(SWA_WIDTH), cache, cache],
        out_shape=[jax.ShapeDtypeStruct((nb * dec_seq, SWA_WIDTH), BF16),
                   jax.ShapeDtypeStruct(k_t.shape, F32),
                   jax.ShapeDtypeStruct(v_t.shape, F32)],
        scratch_shapes=[
            pltpu.VMEM((2, N_SWA_HEADS * dec_seq, WINDOW), F32),
            pltpu.VMEM((N_SWA_HEADS, DEC_ROWS, LANES), F32),
            pltpu.VMEM((N_SWA_HEADS, DEC_ROWS, LANES), F32),
            pltpu.VMEM((DEC_GROUP, SWA_KV_WIDTH, LANES), F32),
            pltpu.VMEM((DEC_GROUP, SWA_KV_WIDTH, LANES), F32),
            pltpu.VMEM((SWA_KV_WIDTH, DEC_ROWS), BF16),
            pltpu.VMEM((SWA_KV_WIDTH, DEC_ROWS), BF16),
        ],
        compiler_params=pltpu.CompilerParams(
            dimension_semantics=("arbitrary",), vmem_limit_bytes=VMEM_LIMIT),
        name="decode_attn",
    )(sinks, qn, kn, vs, k_t, v_t)


def kernel(x_prompt, x_sample, state_ret, cache_swa_k, cache_swa_v, norm_mix_gain, w_in, q_norm_gain,
           k_norm_gain, attn_sinks, w_out, norm_ffn_gain, w_up, w_down):
    batch, seq, d = x_prompt.shape
    nb, dec_seq, _ = x_sample.shape
    wb = cache_swa_k.shape[1]
    assert d == D_MODEL and seq % PROMPT_TILE == 0 and wb == WINDOW

    w_in_bf = w_in.astype(BF16)
    w_out_bf = w_out.astype(BF16)
    w_up_bf = w_up.astype(BF16)
    w_down_bf = w_down.astype(BF16)
    gain_mix = norm_mix_gain.reshape(1, D_MODEL)
    gain_ffn = norm_ffn_gain.reshape(1, D_MODEL)
    qg = jnp.tile(q_norm_gain, 2).reshape(1, LANES)
    kg = jnp.tile(k_norm_gain, 2).reshape(1, LANES)

    xp = x_prompt.reshape(batch * seq, D_MODEL)
    y_p, ret_p, kwin_t, vwin_t = _prompt_layer(attn_sinks, xp, gain_mix, w_in_bf, qg, kg, w_out_bf, gain_ffn,
                                               w_up_bf, w_down_bf, batch, seq)
    y_p = y_p.reshape(batch, seq, D_MODEL)

    def from_key_minor(a_t):
        return jnp.transpose(a_t.reshape(a_t.shape[0], N_SWA_KV, HEAD_DIM, WINDOW), (0, 3, 1, 2))

    def to_key_minor(a):
        return jnp.transpose(a, (0, 2, 3, 1)).reshape(a.shape[0], SWA_KV_WIDTH, WINDOW)

    xs = x_sample.reshape(nb * dec_seq, D_MODEL)
    main_s, qn_s, kn_s, vs_s = _in_proj(xs, gain_mix, w_in_bf, qg, kg)
    mix_ret_s, state_t = _decode_ret(main_s, jnp.transpose(state_ret, (1, 2, 3, 0)), dec_seq)
    mix_swa_s, k_t, v_t = _decode_attn(attn_sinks, qn_s, kn_s, vs_s,
                                       to_key_minor(cache_swa_k), to_key_minor(cache_swa_v), dec_seq)
    y_s = _out_mlp(mix_ret_s, 0, mix_swa_s, 0, xs, w_out_bf, gain_ffn, w_up_bf, w_down_bf)

    return (y_p, y_s.reshape(nb, dec_seq, D_MODEL), ret_p, from_key_minor(kwin_t), from_key_minor(vwin_t),
            jnp.transpose(state_t, (3, 0, 1, 2)), from_key_minor(k_t), from_key_minor(v_t))
```

```python
import functools
import math

import jax
import jax.numpy as jnp
from jax import lax
from jax.experimental import pallas as pl
from jax.experimental.pallas import tpu as pltpu

F32 = jnp.float32
BF16 = jnp.bfloat16

D_MODEL = 1024
HEAD_DIM = 64
N_RET_HEADS = 8
N_SWA_HEADS = 8
N_SWA_KV = 2
SWA_GROUP = N_SWA_HEADS // N_SWA_KV
RET_WIDTH = N_RET_HEADS * HEAD_DIM
SWA_WIDTH = N_SWA_HEADS * HEAD_DIM
SWA_KV_WIDTH = N_SWA_KV * HEAD_DIM
MAIN_WIDTH = 4 * RET_WIDTH
IN_WIDTH = MAIN_WIDTH + SWA_WIDTH + 2 * SWA_KV_WIDTH
MIX_WIDTH = RET_WIDTH + SWA_WIDTH
D_FF = 4 * D_MODEL
WINDOW = 128
RET_CHUNK = 128
EPS = 1e-6
NEG_INF = -1e30

LANES = 128
N_PAIRS = N_RET_HEADS // 2
LOG_DECAY = [math.log(1.0 - 2.0 ** (-5.0 - h)) for h in range(N_RET_HEADS)]
ALIBI_SLOPES = [2.0 ** (-8.0 * (h + 1) / N_SWA_HEADS) for h in range(N_SWA_HEADS)]
K_SCALE = HEAD_DIM ** -0.5

ROW_TILE = 512
PROMPT_TILE = 256
DEC_GROUP = 16
VMEM_LIMIT = 56 * 1024 * 1024


def _dot(a, b):
    return jnp.dot(a, b, preferred_element_type=F32)


def _dot_nt(a, b):
    return lax.dot_general(a, b, (((1,), (1,)), ((), ())), preferred_element_type=F32)


def _iota(shape, dim):
    return lax.broadcasted_iota(jnp.int32, shape, dim)


def _ones_block_diag():
    same = (_iota((LANES, LANES), 0) >> 6) == (_iota((LANES, LANES), 1) >> 6)
    return jnp.where(same, 1.0, 0.0).astype(BF16)


def _head_sumsq(x, ones_bd):
    x2 = x * x
    hi = x2.astype(BF16)
    lo = (x2 - hi.astype(F32)).astype(BF16)
    return _dot(hi, ones_bd) + _dot(lo, ones_bd)


def _head_rms_scale(x, ones_bd):
    return lax.rsqrt(_head_sumsq(x, ones_bd) * (1.0 / HEAD_DIM) + EPS)


def _first_half():
    return _iota((1, LANES), 1) < HEAD_DIM


def _pair_const(values, pair, lane_is_second):
    return jnp.where(lane_is_second, values[2 * pair + 1], values[2 * pair]).astype(F32)


def _silu(g):
    return g * (1.0 / (1.0 + jnp.exp(-g)))


def _in_proj_kernel(x_ref, gain_ref, w_ref, qg_ref, kg_ref, main_ref, qn_ref, kn_ref, vs_ref):
    x = x_ref[...]
    ms = jnp.mean(x * x, axis=-1, keepdims=True)
    hb = ((x * lax.rsqrt(ms + EPS)) * gain_ref[...]).astype(BF16)
    ones_bd = _ones_block_diag()
    for c in range(MAIN_WIDTH // 512):
        main_ref[:, c * 512:(c + 1) * 512] = _dot(hb, w_ref[:, c * 512:(c + 1) * 512])
    qs = _dot(hb, w_ref[:, MAIN_WIDTH:MAIN_WIDTH + SWA_WIDTH])
    for c in range(SWA_WIDTH // LANES):
        xc = qs[:, c * LANES:(c + 1) * LANES]
        qn_ref[:, c * LANES:(c + 1) * LANES] = (xc * _head_rms_scale(xc, ones_bd)) * qg_ref[...]
    k0 = MAIN_WIDTH + SWA_WIDTH
    ks = _dot(hb, w_ref[:, k0:k0 + SWA_KV_WIDTH])
    kn_ref[...] = (ks * _head_rms_scale(ks, ones_bd)) * kg_ref[...]
    vs_ref[...] = _dot(hb, w_ref[:, k0 + SWA_KV_WIDTH:k0 + 2 * SWA_KV_WIDTH])


def _in_proj(x2d, gain, w_in_bf, qg, kg):
    m = x2d.shape[0]
    tm = min(ROW_TILE, m)
    row = lambda w: pl.BlockSpec((tm, w), lambda i: (i, 0))
    full = lambda a: pl.BlockSpec(a.shape, lambda i: (0, 0), pipeline_mode=pl.Buffered(1))
    return pl.pallas_call(
        _in_proj_kernel,
        grid=(m // tm,),
        in_specs=[row(D_MODEL), full(gain), full(w_in_bf), full(qg), full(kg)],
        out_specs=[row(MAIN_WIDTH), row(SWA_WIDTH), row(SWA_KV_WIDTH), row(SWA_KV_WIDTH)],
        out_shape=[jax.ShapeDtypeStruct((m, MAIN_WIDTH), F32),
                   jax.ShapeDtypeStruct((m, SWA_WIDTH), F32),
                   jax.ShapeDtypeStruct((m, SWA_KV_WIDTH), F32),
                   jax.ShapeDtypeStruct((m, SWA_KV_WIDTH), F32)],
        compiler_params=pltpu.CompilerParams(
            dimension_semantics=("arbitrary",), vmem_limit_bytes=VMEM_LIMIT),
        name="in_proj",
    )(x2d, gain, w_in_bf, qg, kg)


FF_CHUNK = 1024


def _out_mlp_kernel(mix_ret_ref, mix_swa_ref, x_ref, w_out_ref, gain_ref, w_up_ref, w_down_ref, y_ref):
    h = x_ref[...] + (_dot(mix_ret_ref[...].astype(BF16), w_out_ref[:RET_WIDTH, :])
                      + _dot(mix_swa_ref[...].astype(BF16), w_out_ref[RET_WIDTH:, :]))
    ms = jnp.mean(h * h, axis=-1, keepdims=True)
    hf = ((h * lax.rsqrt(ms + EPS)) * gain_ref[...]).astype(BF16)
    ff = None
    for c in range(D_FF // FF_CHUNK):
        u = _dot(hf, w_up_ref[:, c * FF_CHUNK:(c + 1) * FF_CHUNK])
        a = jnp.maximum(u, 0.0)
        d = _dot((a * a).astype(BF16), w_down_ref[c * FF_CHUNK:(c + 1) * FF_CHUNK, :])
        ff = d if ff is None else ff + d
    y_ref[...] = h + ff


def _out_mlp(mix_ret, ret_block, mix_swa, swa_block, x2d, w_out_bf, gain, w_up_bf, w_down_bf):
    m = x2d.shape[0]
    tm = min(ROW_TILE, m)
    row = lambda w: pl.BlockSpec((tm, w), lambda i: (i, 0))
    col_block = lambda c: pl.BlockSpec((tm, RET_WIDTH), lambda i: (i, c))
    full = lambda a: pl.BlockSpec(a.shape, lambda i: (0, 0), pipeline_mode=pl.Buffered(1))
    return pl.pallas_call(
        _out_mlp_kernel,
        grid=(m // tm,),
        in_specs=[col_block(ret_block), col_block(swa_block), row(D_MODEL), full(w_out_bf), full(gain),
                  full(w_up_bf), full(w_down_bf)],
        out_specs=row(D_MODEL),
        out_shape=jax.ShapeDtypeStruct((m, D_MODEL), F32),
        compiler_params=pltpu.CompilerParams(
            dimension_semantics=("arbitrary",), vmem_limit_bytes=VMEM_LIMIT),
        name="out_mlp",
    )(mix_ret, mix_swa, x2d, w_out_bf, gain, w_up_bf, w_down_bf)


def _split_pair_rows(x, first):
    return jnp.concatenate([jnp.where(first, x, 0.0), jnp.where(first, 0.0, x)], axis=0).astype(BF16)


def _softmax_sink_pv(s, sink_wide, v_t):
    m = jnp.maximum(jnp.max(s, axis=-1, keepdims=True), sink_wide)
    p = jnp.exp(s - jnp.concatenate([m, m], axis=1))
    denom = jnp.sum(p, axis=-1, keepdims=True) + jnp.exp(sink_wide - m)
    return _dot_nt(p.astype(BF16), v_t) / denom


def _softmax_sink_pv_t(s_t, sink_lanes, v_t):
    m = jnp.maximum(jnp.max(s_t, axis=0, keepdims=True), sink_lanes)
    p = jnp.exp(s_t - m)
    denom = jnp.sum(p, axis=0, keepdims=True) + jnp.exp(sink_lanes - m)
    return _dot(v_t, p.astype(BF16)) / denom


def _prompt_consts(intra_ref, qdec_ref, kdec_ref, sdec_ref, bias_ref):
    r = _iota((LANES, LANES), 0)
    lane2 = _iota((LANES, LANES), 1) >= HEAD_DIM
    rf = r.astype(F32)
    ri = _iota((LANES, 2 * LANES), 0)
    ci = _iota((LANES, 2 * LANES), 1)
    diff = (ri - (ci & (LANES - 1))).astype(F32)
    for p in range(N_PAIRS):
        lg = _pair_const(LOG_DECAY, p, lane2)
        qdec_ref[p] = jnp.exp(lg * (rf + 1.0))
        kdec_ref[p] = jnp.exp(lg * (RET_CHUNK - 1.0 - rf))
        sdec_ref[p] = jnp.exp(_pair_const(LOG_DECAY, p, r >= HEAD_DIM) * float(RET_CHUNK))
        lg2 = _pair_const(LOG_DECAY, p, ci >= LANES)
        intra_ref[p] = jnp.where(diff >= 0.0, jnp.exp(lg2 * jnp.maximum(diff, 0.0)), 0.0)
    cols = SWA_GROUP * WINDOW
    kb = _iota((2 * WINDOW, cols), 0)
    cb = _iota((2 * WINDOW, cols), 1)
    grp = cb >> 7
    dist = WINDOW + (cb & (WINDOW - 1)) - kb
    valid = (dist >= 0) & (dist < WINDOW)
    distf = dist.astype(F32)
    for j in range(N_SWA_KV):
        sl = [ALIBI_SLOPES[SWA_GROUP * j + g] for g in range(SWA_GROUP)]
        slope = jnp.where(grp == 0, sl[0], jnp.where(grp == 1, sl[1], jnp.where(grp == 2, sl[2], sl[3])))
        b = jnp.where(valid, -(slope.astype(F32) * distf), NEG_INF)
        bias_ref[0, j] = b
        bias_ref[1, j] = jnp.where(kb >= WINDOW, b, NEG_INF)


def _prompt_layer_kernel(sinks_ref, x_ref, x_next_ref, gain_mix_ref, w_in_ref, qg_ref, kg_ref,
                         w_out_ref, gain_ffn_ref, w_up_ref, w_down_ref,
                         y_ref, ret_ref, kwin_ref, vwin_ref,
                         main_ref, qn_ref, kn_ref, vs_ref, mix_ref,
                         state_ref, prevk_ref, prevv_ref,
                         intra_ref, qdec_ref, kdec_ref, sdec_ref, bias_ref):
    t = pl.program_id(1)
    step = pl.program_id(0) * pl.num_programs(1) + t
    cur = step % 2
    nxt = 1 - cur

    @pl.when(step == 0)
    def _():
        _prompt_consts(intra_ref, qdec_ref, kdec_ref, sdec_ref, bias_ref)
        _in_proj_kernel(x_ref, gain_mix_ref, w_in_ref, qg_ref, kg_ref,
                        main_ref.at[0], qn_ref.at[0], kn_ref.at[0], vs_ref.at[0])

    @pl.when(t == 0)
    def _():
        state_ref[...] = jnp.zeros_like(state_ref)
        prevk_ref[...] = jnp.zeros_like(prevk_ref)
        prevv_ref[...] = jnp.zeros_like(prevv_ref)

    first = _first_half()
    ones_bd = _ones_block_diag()
    bd_mask = (_iota((LANES, LANES), 0) >= HEAD_DIM) == (_iota((LANES, LANES), 1) >= HEAD_DIM)

    def chunk(c, carry):
        r0 = pl.multiple_of(c * RET_CHUNK, RET_CHUNK)
        rows = pl.ds(r0, RET_CHUNK)
        pairs = range(N_PAIRS)
        kvs = range(N_SWA_KV)
        cols = lambda base, p: slice(base + p * LANES, base + (p + 1) * LANES)
        xn = x_next_ref[rows, :]
        hb = ((xn * lax.rsqrt(jnp.mean(xn * xn, axis=-1, keepdims=True) + EPS)) * gain_mix_ref[...]).astype(BF16)

        q = [main_ref[cur, rows, cols(0, p)] for p in pairs]
        k = [main_ref[cur, rows, cols(RET_WIDTH, p)] * K_SCALE for p in pairs]
        v = [main_ref[cur, rows, cols(2 * RET_WIDTH, p)] for p in pairs]
        state = [state_ref[p] for p in pairs]
        kc = kn_ref[cur, rows, :]
        k_sw = pltpu.roll(kc, HEAD_DIM, axis=1)
        v_t = vs_ref[cur, rows, :].T
        is_first = ((t == 0) & (c == 0)).astype(jnp.int32)
        k_dup = [(jnp.where(first, kc, k_sw) if j == 0 else jnp.where(first, k_sw, kc)).astype(BF16)
                 for j in kvs]
        v_tj = [v_t[j * HEAD_DIM:(j + 1) * HEAD_DIM].astype(BF16) for j in kvs]
        q_st = []
        for j in kvs:
            pieces = []
            for g in range(SWA_GROUP):
                qc = qn_ref[cur, rows, cols(0, 2 * j + g // 2)]
                pieces.append(jnp.where(first, qc, 0.0) if g % 2 == 0 else jnp.where(first, 0.0, qc))
            q_st.append(jnp.concatenate(pieces, axis=0).astype(BF16))

        s = [_dot_nt(q[p].astype(BF16), _split_pair_rows(k[p], first)) for p in pairs]
        s_t = [_dot_nt(jnp.concatenate([prevk_ref[j], k_dup[j]], axis=0), q_st[j]) for j in kvs]
        cross = [_dot((q[p] * qdec_ref[p]).astype(BF16), state[p].astype(BF16)) for p in pairs]
        upd = [_dot((k[p] * kdec_ref[p]).T.astype(BF16), v[p].astype(BF16)) for p in pairs]
        for cb in range(MAIN_WIDTH // 512):
            main_ref[nxt, rows, cb * 512:(cb + 1) * 512] = _dot(hb, w_in_ref[:, cb * 512:(cb + 1) * 512])

        o = [_dot((s[p] * intra_ref[p]).astype(BF16), _split_pair_rows(v[p], first)) + cross[p] for p in pairs]
        o_t = []
        for j in kvs:
            sink_lanes = jnp.concatenate(
                [jnp.full((1, WINDOW), sinks_ref[SWA_GROUP * j + g], F32) for g in range(SWA_GROUP)],
                axis=1)
            v_cat = jnp.concatenate([prevv_ref[j], v_tj[j]], axis=1)
            o_t.append(_softmax_sink_pv_t(s_t[j] * K_SCALE + bias_ref[is_first, j], sink_lanes, v_cat))
        for p in pairs:
            state_ref[p] = state[p] * sdec_ref[p] + jnp.where(bd_mask, upd[p], 0.0)
        for j in kvs:
            prevk_ref[j] = k_dup[j]
            prevv_ref[j] = v_tj[j]
        a0 = MAIN_WIDTH
        qs = _dot(hb, w_in_ref[:, a0:a0 + SWA_WIDTH])
        ks = _dot(hb, w_in_ref[:, a0 + SWA_WIDTH:a0 + SWA_WIDTH + SWA_KV_WIDTH])
        vs_ref[nxt, rows, :] = _dot(hb, w_in_ref[:, a0 + SWA_WIDTH + SWA_KV_WIDTH:a0 + SWA_WIDTH + 2 * SWA_KV_WIDTH])

        scale = [_head_rms_scale(o[p], ones_bd) for p in pairs]
        q_scale = [_head_rms_scale(qs[:, cols(0, cq)], ones_bd) for cq in range(SWA_WIDTH // LANES)]
        k_scale = _head_rms_scale(ks, ones_bd)
        for p in pairs:
            g = main_ref[cur, rows, cols(3 * RET_WIDTH, p)]
            mix_ref[rows, cols(0, p)] = (o[p] * scale[p] * _silu(g)).astype(BF16)
        for cq in range(SWA_WIDTH // LANES):
            qn_ref[nxt, rows, cols(0, cq)] = (qs[:, cols(0, cq)] * q_scale[cq]) * qg_ref[...]
        kn_ref[nxt, rows, :] = (ks * k_scale) * kg_ref[...]
        for j in kvs:
            for half in range(2):
                pair_t = jnp.concatenate([o_t[j][:, (2 * half) * WINDOW:(2 * half + 1) * WINDOW],
                                          o_t[j][:, (2 * half + 1) * WINDOW:(2 * half + 2) * WINDOW]], axis=0)
                mix_ref[rows, cols(RET_WIDTH, 2 * j + half)] = pair_t.T.astype(BF16)
        return carry

    lax.fori_loop(0, PROMPT_TILE // RET_CHUNK, chunk, 0)

    @pl.when(t == pl.num_programs(1) - 1)
    def _():
        for p in range(N_PAIRS):
            s = state_ref[p]
            ret_ref[2 * p] = s[:HEAD_DIM, :HEAD_DIM]
            ret_ref[2 * p + 1] = s[HEAD_DIM:, HEAD_DIM:]
        last = slice(PROMPT_TILE - WINDOW, PROMPT_TILE)
        kwin_ref[...] = kn_ref[cur, last, :].T
        vwin_ref[...] = vs_ref[cur, last, :].T

    _out_mlp_kernel(mix_ref.at[:, pl.ds(0, RET_WIDTH)], mix_ref.at[:, pl.ds(RET_WIDTH, SWA_WIDTH)], x_ref,
                    w_out_ref, gain_ffn_ref, w_up_ref, w_down_ref, y_ref)


def _prompt_layer(sinks, x2d, gain_mix, w_in_bf, qg, kg, w_out_bf, gain_ffn, w_up_bf, w_down_bf, batch, seq):
    nt = seq // PROMPT_TILE
    last_tile = batch * nt - 1
    row = lambda w: pl.BlockSpec((PROMPT_TILE, w), lambda b, t: (b * nt + t, 0))
    next_row = pl.BlockSpec((PROMPT_TILE, D_MODEL), lambda b, t: (jnp.minimum(b * nt + t + 1, last_tile), 0))
    full = lambda a: pl.BlockSpec(a.shape, lambda b, t: (0, 0), pipeline_mode=pl.Buffered(1))
    return pl.pallas_call(
        _prompt_layer_kernel,
        grid=(batch, nt),
        in_specs=[pl.BlockSpec(memory_space=pltpu.SMEM), row(D_MODEL), next_row,
                  full(gain_mix), full(w_in_bf), full(qg), full(kg),
                  full(w_out_bf), full(gain_ffn), full(w_up_bf), full(w_down_bf)],
        out_specs=[row(D_MODEL),
                   pl.BlockSpec((None, N_RET_HEADS, HEAD_DIM, HEAD_DIM), lambda b, t: (b, 0, 0, 0)),
                   pl.BlockSpec((None, SWA_KV_WIDTH, WINDOW), lambda b, t: (b, 0, 0)),
                   pl.BlockSpec((None, SWA_KV_WIDTH, WINDOW), lambda b, t: (b, 0, 0))],
        out_shape=[jax.ShapeDtypeStruct((batch * seq, D_MODEL), F32),
                   jax.ShapeDtypeStruct((batch, N_RET_HEADS, HEAD_DIM, HEAD_DIM), F32),
                   jax.ShapeDtypeStruct((batch, SWA_KV_WIDTH, WINDOW), F32),
                   jax.ShapeDtypeStruct((batch, SWA_KV_WIDTH, WINDOW), F32)],
        scratch_shapes=[
            pltpu.VMEM((2, PROMPT_TILE, MAIN_WIDTH), F32),
            pltpu.VMEM((2, PROMPT_TILE, SWA_WIDTH), F32),
            pltpu.VMEM((2, PROMPT_TILE, SWA_KV_WIDTH), F32),
            pltpu.VMEM((2, PROMPT_TILE, SWA_KV_WIDTH), F32),
            pltpu.VMEM((PROMPT_TILE, MIX_WIDTH), BF16),
            pltpu.VMEM((N_PAIRS, LANES, LANES), F32),
            pltpu.VMEM((N_SWA_KV, WINDOW, LANES), BF16),
            pltpu.VMEM((N_SWA_KV, HEAD_DIM, WINDOW), BF16),
            pltpu.VMEM((N_PAIRS, LANES, 2 * LANES), F32),
            pltpu.VMEM((N_PAIRS, LANES, LANES), F32),
            pltpu.VMEM((N_PAIRS, LANES, LANES), F32),
            pltpu.VMEM((N_PAIRS, LANES, LANES), F32),
            pltpu.VMEM((2, N_SWA_KV, 2 * WINDOW, SWA_GROUP * WINDOW), F32),
        ],
        compiler_params=pltpu.CompilerParams(
            dimension_semantics=("arbitrary", "arbitrary"), vmem_limit_bytes=VMEM_LIMIT),
        name="prompt_layer",
    )(sinks, x2d, x2d, gain_mix, w_in_bf, qg, kg, w_out_bf, gain_ffn, w_up_bf, w_down_bf)


def _decode_ret_kernel(dec_seq, nb, qdec_ref, kdec_ref, sdec_ref, intra_ref,
                       q_ref, k_ref, v_ref, g_ref, st_ref,
                       mix_ref, st_out_ref,
                       qt_ref, kt_ref, vt_ref, qdt_ref, kdt_ref, o_ref):
    pair = pl.program_id(0)
    halves = [slice(0, HEAD_DIM), slice(HEAD_DIM, 2 * HEAD_DIM)]
    for l in range(dec_seq):
        rows = pl.ds(l, nb, stride=dec_seq)
        q_t = q_ref[rows, :].T
        k_t = (k_ref[rows, :] * K_SCALE).T
        qt_ref[l] = q_t
        kt_ref[l] = k_t
        vt_ref[l] = v_ref[rows, :].T
        for hh in range(2):
            qdt_ref[l, halves[hh], :] = q_t[halves[hh]] * qdec_ref[2 * pair + hh, l]
            kdt_ref[l, halves[hh], :] = k_t[halves[hh]] * kdec_ref[2 * pair + hh, l]

    e_blk = HEAD_DIM // 2
    for hh in range(2):
        h = 2 * pair + hh
        hs = halves[hh]
        for l in range(dec_seq):
            acc = None
            for m in range(l + 1):
                sc = jnp.sum(qt_ref[l, hs, :] * kt_ref[m, hs, :], axis=0, keepdims=True) * intra_ref[h, l - m]
                term = sc * vt_ref[m, hs, :]
                acc = term if acc is None else acc + term
            o_ref[l, hs, :] = acc
        for eb in range(HEAD_DIM // e_blk):
            es = slice(eb * e_blk, (eb + 1) * e_blk)
            erows = slice(hh * HEAD_DIM + eb * e_blk, hh * HEAD_DIM + (eb + 1) * e_blk)

            def body(d, accs, hh=hh, h=h, es=es, erows=erows):
                s_d = st_ref[hh, d, es, :]
                row = pl.ds(hh * HEAD_DIM + d, 1)
                upd = s_d * sdec_ref[h]
                new_accs = []
                for l in range(dec_seq):
                    new_accs.append(accs[l] + qdt_ref[l, row, :] * s_d)
                    upd = upd + kdt_ref[l, row, :] * vt_ref[l, erows, :]
                st_out_ref[hh, d, es, :] = upd
                return tuple(new_accs)

            zero = jnp.zeros((e_blk, nb), F32)
            accs = lax.fori_loop(0, HEAD_DIM, body, tuple(zero for _ in range(dec_seq)))
            for l in range(dec_seq):
                o_ref[l, erows, :] = o_ref[l, erows, :] + accs[l]

    for l in range(dec_seq):
        o = o_ref[l]
        normed = []
        for hh in range(2):
            oh = o[halves[hh]]
            normed.append(oh * lax.rsqrt(jnp.mean(oh * oh, axis=0, keepdims=True) + EPS))
        rows = pl.ds(l, nb, stride=dec_seq)
        mix_ref[rows, :] = jnp.concatenate(normed, axis=0).T * _silu(g_ref[rows, :])


def _decode_ret(main, state_t, dec_seq):
    nb = state_t.shape[-1]
    m = main.shape[0]
    assert nb == LANES and m == nb * dec_seq
    steps = [j for j in range(dec_seq)]
    tab = lambda f: jnp.asarray([[f(h, j) for j in steps] for h in range(N_RET_HEADS)], F32)
    qdec = tab(lambda h, j: math.exp(LOG_DECAY[h] * (j + 1.0)))
    kdec = tab(lambda h, j: math.exp(LOG_DECAY[h] * (dec_seq - 1.0 - j)))
    intra = tab(lambda h, j: math.exp(LOG_DECAY[h] * j))
    sdec = jnp.asarray([math.exp(LOG_DECAY[h] * dec_seq) for h in range(N_RET_HEADS)], F32)
    smem = pl.BlockSpec(memory_space=pltpu.SMEM)
    col = lambda base: pl.BlockSpec((m, LANES), lambda p: (0, base + p))
    st_spec = pl.BlockSpec((2, HEAD_DIM, HEAD_DIM, nb), lambda p: (p, 0, 0, 0))
    stage = pltpu.VMEM((dec_seq, LANES, nb), F32)
    return pl.pallas_call(
        functools.partial(_decode_ret_kernel, dec_seq, nb),
        grid=(N_PAIRS,),
        in_specs=[smem, smem, smem, smem,
                  col(0), col(N_PAIRS), col(2 * N_PAIRS), col(3 * N_PAIRS), st_spec],
        out_specs=[pl.BlockSpec((m, LANES), lambda p: (0, p)), st_spec],
        out_shape=[jax.ShapeDtypeStruct((m, RET_WIDTH), F32),
                   jax.ShapeDtypeStruct(state_t.shape, F32)],
        scratch_shapes=[stage, stage, stage, stage, stage, stage],
        compiler_params=pltpu.CompilerParams(
            dimension_semantics=("arbitrary",), vmem_limit_bytes=VMEM_LIMIT),
        name="decode_ret",
    )(qdec, kdec, sdec, intra, main, main, main, main, state_t)


DEC_ROWS = 128
DEC_UNROLL = 4


def _decode_attn_consts(dec_seq, bias_ref):
    shift = dec_seq.bit_length() - 1
    rows = N_SWA_HEADS * dec_seq
    rb = _iota((rows, WINDOW), 0)
    cb = _iota((rows, WINDOW), 1)
    head = rb >> shift
    i = rb & (dec_seq - 1)
    slope = jnp.zeros((rows, WINDOW), F32)
    for h in range(N_SWA_HEADS):
        slope = jnp.where(head == h, ALIBI_SLOPES[h], slope)
    bias_ref[0] = jnp.where(cb > i, -(slope * (WINDOW + i - cb).astype(F32)), NEG_INF)
    m = cb & (dec_seq - 1)
    bias_ref[1] = jnp.where(m <= i, -(slope * (i - m).astype(F32)), NEG_INF)


def _decode_attn_kernel(dec_seq, sinks_ref, qn_ref, kn_ref, vs_ref, kt_ref, vt_ref,
                        mix_ref, kt_out_ref, vt_out_ref,
                        bias_ref, qbd_ref, oblk_ref, knew_ref, vnew_ref, knt_ref, vst_ref):
    @pl.when(pl.program_id(0) == 0)
    def _():
        _decode_attn_consts(dec_seq, bias_ref)

    first = _first_half()
    shift = dec_seq.bit_length() - 1

    kn_t = kn_ref[...].T
    vs_t = vs_ref[...].T
    knt_ref[...] = kn_t.astype(BF16)
    vst_ref[...] = vs_t.astype(BF16)
    for bb in range(DEC_GROUP):
        sh = (WINDOW - dec_seq - bb * dec_seq) % LANES
        knew_ref[bb] = pltpu.roll(kn_t, sh, axis=1) if sh else kn_t
        vnew_ref[bb] = pltpu.roll(vs_t, sh, axis=1) if sh else vs_t
    qn = qn_ref[...]
    qn_sw = pltpu.roll(qn, HEAD_DIM, axis=1)
    for h in range(N_SWA_HEADS):
        kv_half = h // SWA_GROUP
        if (h % 2) == kv_half:
            src = qn[:, (h // 2) * LANES:(h // 2 + 1) * LANES]
        else:
            col = (h + 1) // 2
            src = qn_sw[:, col * LANES:(col + 1) * LANES]
        qbd_ref[h] = jnp.where(first, src, 0.0) if kv_half == 0 else jnp.where(first, 0.0, src)

    sink_rows = jnp.concatenate(
        [jnp.full((dec_seq, LANES), sinks_ref[h], F32) for h in range(N_SWA_HEADS)], axis=0)
    col_batch = _iota((N_SWA_HEADS * dec_seq, LANES), 1) >> shift
    keep_old = _iota((1, LANES), 1) < WINDOW - dec_seq

    def per_batches(i, carry):
        bs = [i * DEC_UNROLL + u for u in range(DEC_UNROLL)]
        rows = [pl.ds(pl.multiple_of(b * dec_seq, dec_seq), dec_seq) for b in bs]
        k_old = [kt_ref[b] for b in bs]
        v_old = [vt_ref[b] for b in bs]
        q_st = [jnp.concatenate([qbd_ref[h, r, :] for h in range(N_SWA_HEADS)], axis=0).astype(BF16)
                for r in rows]
        s = [_dot(q_st[u], jnp.concatenate([k_old[u].astype(BF16), knt_ref[...]], axis=1))
             for u in range(DEC_UNROLL)]
        o = []
        for u, b in enumerate(bs):
            bias = jnp.concatenate([bias_ref[0], jnp.where(col_batch == b, bias_ref[1], NEG_INF)], axis=1)
            w_v = jnp.concatenate([v_old[u].astype(BF16), vst_ref[...]], axis=1)
            o.append(_softmax_sink_pv(s[u] * K_SCALE + bias, sink_rows, w_v))
        for u, b in enumerate(bs):
            for h in range(N_SWA_HEADS):
                oblk_ref[h, rows[u], :] = o[u][h * dec_seq:(h + 1) * dec_seq]
            kt_out_ref[b] = jnp.where(keep_old, pltpu.roll(k_old[u], LANES - dec_seq, axis=1), knew_ref[b])
            vt_out_ref[b] = jnp.where(keep_old, pltpu.roll(v_old[u], LANES - dec_seq, axis=1), vnew_ref[b])
        return carry

    lax.fori_loop(0, DEC_GROUP // DEC_UNROLL, per_batches, 0)

    y1 = jnp.where(first, oblk_ref[3], oblk_ref[4])
    moved = pltpu.roll(jnp.concatenate([oblk_ref[1], y1, oblk_ref[6], oblk_ref[6]], axis=1), HEAD_DIM, axis=1)
    outs = [
        jnp.where(first, oblk_ref[0], moved[:, 0:LANES]),
        jnp.where(first, oblk_ref[2], moved[:, LANES:2 * LANES]),
        jnp.where(first, moved[:, 2 * LANES:3 * LANES], oblk_ref[5]),
        jnp.where(first, moved[:, 3 * LANES:4 * LANES], oblk_ref[7]),
    ]
    for c in range(SWA_WIDTH // LANES):
        mix_ref[:, c * LANES:(c + 1) * LANES] = outs[c].astype(BF16)


def _decode_attn(sinks, qn, kn, vs, k_t, v_t, dec_seq):
    nb = k_t.shape[0]
    assert DEC_GROUP * dec_seq == DEC_ROWS and nb % DEC_GROUP == 0 and dec_seq & (dec_seq - 1) == 0
    assert k_t.shape[1:] == (SWA_KV_WIDTH, WINDOW)
    row = lambda w: pl.BlockSpec((DEC_ROWS, w), lambda i: (i, 0))
    cache = pl.BlockSpec((DEC_GROUP, SWA_KV_WIDTH, WINDOW), lambda i: (i, 0, 0))
    return pl.pallas_call(
        functools.partial(_decode_attn_kernel, dec_seq),
        grid=(nb // DEC_GROUP,),
        in_specs=[pl.BlockSpec(memory_space=pltpu.SMEM),
                  row(SWA_WIDTH), row(SWA_KV_WIDTH), row(SWA_KV_WIDTH), cache, cache],
        out_specs=[row(SWA_WIDTH), cache, cache],
        out_shape=[jax.ShapeDtypeStruct((nb * dec_seq, SWA_WIDTH), BF16),
                   jax.ShapeDtypeStruct(k_t.shape, F32),
                   jax.ShapeDtypeStruct(v_t.shape, F32)],
        scratch_shapes=[
            pltpu.VMEM((2, N_SWA_HEADS * dec_seq, WINDOW), F32),
            pltpu.VMEM((N_SWA_HEADS, DEC_ROWS, LANES), F32),
            pltpu.VMEM((N_SWA_HEADS, DEC_ROWS, LANES), F32),
            pltpu.VMEM((DEC_GROUP, SWA_KV_WIDTH, LANES), F32),
            pltpu.VMEM((DEC_GROUP, SWA_KV_WIDTH, LANES), F32),
            pltpu.VMEM((SWA_KV_WIDTH, DEC_ROWS), BF16),
            pltpu.VMEM((SWA_KV_WIDTH, DEC_ROWS), BF16),
        ],
        compiler_params=pltpu.CompilerParams(
            dimension_semantics=("arbitrary",), vmem_limit_bytes=VMEM_LIMIT),
        name="decode_attn",
    )(sinks, qn, kn, vs, k_t, v_t)


def kernel(x_prompt, x_sample, state_ret, cache_swa_k, cache_swa_v, norm_mix_gain, w_in, q_norm_gain,
           k_norm_gain, attn_sinks, w_out, norm_ffn_gain, w_up, w_down):
    batch, seq, d = x_prompt.shape
    nb, dec_seq, _ = x_sample.shape
    wb = cache_swa_k.shape[1]
    assert d == D_MODEL and seq % PROMPT_TILE == 0 and wb == WINDOW

    w_in_bf = w_in.astype(BF16)
    w_out_bf = w_out.astype(BF16)
    w_up_bf = w_up.astype(BF16)
    w_down_bf = w_down.astype(BF16)
    gain_mix = norm_mix_gain.reshape(1, D_MODEL)
    gain_ffn = norm_ffn_gain.reshape(1, D_MODEL)
    qg = jnp.tile(q_norm_gain, 2).reshape(1, LANES)
    kg = jnp.tile(k_norm_gain, 2).reshape(1, LANES)

    xp = x_prompt.reshape(batch * seq, D_MODEL)
    y_p, ret_p, kwin_t, vwin_t = _prompt_layer(attn_sinks, xp, gain_mix, w_in_bf, qg, kg, w_out_bf, gain_ffn,
                                               w_up_bf, w_down_bf, batch, seq)
    y_p = y_p.reshape(batch, seq, D_MODEL)

    def from_key_minor(a_t):
        return jnp.transpose(a_t.reshape(a_t.shape[0], N_SWA_KV, HEAD_DIM, WINDOW), (0, 3, 1, 2))

    def to_key_minor(a):
        return jnp.transpose(a, (0, 2, 3, 1)).reshape(a.shape[0], SWA_KV_WIDTH, WINDOW)

    xs = x_sample.reshape(nb * dec_seq, D_MODEL)
    main_s, qn_s, kn_s, vs_s = _in_proj(xs, gain_mix, w_in_bf, qg, kg)
    mix_ret_s, state_t = _decode_ret(main_s, jnp.transpose(state_ret, (1, 2, 3, 0)), dec_seq)
    mix_swa_s, k_t, v_t = _decode_attn(attn_sinks, qn_s, kn_s, vs_s,
                                       to_key_minor(cache_swa_k), to_key_minor(cache_swa_v), dec_seq)
    y_s = _out_mlp(mix_ret_s, 0, mix_swa_s, 0, xs, w_out_bf, gain_ffn, w_up_bf, w_down_bf)

    return (y_p, y_s.reshape(nb, dec_seq, D_MODEL), ret_p, from_key_minor(kwin_t), from_key_minor(vwin_t),
            jnp.transpose(state_t, (3, 0, 1, 2)), from_key_minor(k_t), from_key_minor(v_t))
```

```python
import functools
import math

import jax
import jax.numpy as jnp
from jax import lax
from jax.experimental import pallas as pl
from jax.experimental.pallas import tpu as pltpu

F32 = jnp.float32
BF16 = jnp.bfloat16

D_MODEL = 1024
HEAD_DIM = 64
N_RET_HEADS = 8
N_SWA_HEADS = 8
N_SWA_KV = 2
SWA_GROUP = N_SWA_HEADS // N_SWA_KV
RET_WIDTH = N_RET_HEADS * HEAD_DIM
SWA_WIDTH = N_SWA_HEADS * HEAD_DIM
SWA_KV_WIDTH = N_SWA_KV * HEAD_DIM
MAIN_WIDTH = 4 * RET_WIDTH
IN_WIDTH = MAIN_WIDTH + SWA_WIDTH + 2 * SWA_KV_WIDTH
MIX_WIDTH = RET_WIDTH + SWA_WIDTH
D_FF = 4 * D_MODEL
WINDOW = 128
RET_CHUNK = 128
EPS = 1e-6
NEG_INF = -1e30

LANES = 128
N_PAIRS = N_RET_HEADS // 2
LOG_DECAY = [math.log(1.0 - 2.0 ** (-5.0 - h)) for h in range(N_RET_HEADS)]
ALIBI_SLOPES = [2.0 ** (-8.0 * (h + 1) / N_SWA_HEADS) for h in range(N_SWA_HEADS)]
K_SCALE = HEAD_DIM ** -0.5

ROW_TILE = 512
PROMPT_TILE = 256
DEC_GROUP = 16
VMEM_LIMIT = 62 * 1024 * 1024


def _dot(a, b):
    return jnp.dot(a, b, preferred_element_type=F32)


def _dot_nt(a, b):
    return lax.dot_general(a, b, (((1,), (1,)), ((), ())), preferred_element_type=F32)


def _iota(shape, dim):
    return lax.broadcasted_iota(jnp.int32, shape, dim)


def _ones_block_diag():
    same = ((_iota((2 * LANES, LANES), 0) >> 6) & 1) == (_iota((2 * LANES, LANES), 1) >> 6)
    return jnp.where(same, 1.0, 0.0).astype(BF16)


def _head_sumsq(x, ones_bd):
    x2 = x * x
    hi = x2.astype(BF16)
    lo = (x2 - hi.astype(F32)).astype(BF16)
    return _dot(jnp.concatenate([hi, lo], axis=1), ones_bd)


def _head_rms_scale(x, ones_bd):
    return lax.rsqrt(_head_sumsq(x, ones_bd) * (1.0 / HEAD_DIM) + EPS)


def _first_half():
    return _iota((1, LANES), 1) < HEAD_DIM


def _pair_const(values, pair, lane_is_second):
    return jnp.where(lane_is_second, values[2 * pair + 1], values[2 * pair]).astype(F32)


def _silu(g):
    return g * (1.0 / (1.0 + jnp.exp(-g)))


def _in_proj_kernel(x_ref, gain_ref, w_ref, qg_ref, kg_ref, main_ref, qn_ref, kn_ref, vs_ref):
    x = x_ref[...]
    ms = jnp.mean(x * x, axis=-1, keepdims=True)
    hb = ((x * lax.rsqrt(ms + EPS)) * gain_ref[...]).astype(BF16)
    ones_bd = _ones_block_diag()
    for c in range(MAIN_WIDTH // 512):
        main_ref[:, c * 512:(c + 1) * 512] = _dot(hb, w_ref[:, c * 512:(c + 1) * 512])
    qs = _dot(hb, w_ref[:, MAIN_WIDTH:MAIN_WIDTH + SWA_WIDTH])
    for c in range(SWA_WIDTH // LANES):
        xc = qs[:, c * LANES:(c + 1) * LANES]
        qn_ref[:, c * LANES:(c + 1) * LANES] = (xc * _head_rms_scale(xc, ones_bd)) * qg_ref[...]
    k0 = MAIN_WIDTH + SWA_WIDTH
    ks = _dot(hb, w_ref[:, k0:k0 + SWA_KV_WIDTH])
    kn_ref[...] = (ks * _head_rms_scale(ks, ones_bd)) * kg_ref[...]
    vs_ref[...] = _dot(hb, w_ref[:, k0 + SWA_KV_WIDTH:k0 + 2 * SWA_KV_WIDTH])


def _in_proj(x2d, gain, w_in_bf, qg, kg):
    m = x2d.shape[0]
    tm = min(ROW_TILE, m)
    row = lambda w: pl.BlockSpec((tm, w), lambda i: (i, 0))
    full = lambda a: pl.BlockSpec(a.shape, lambda i: (0, 0), pipeline_mode=pl.Buffered(1))
    return pl.pallas_call(
        _in_proj_kernel,
        grid=(m // tm,),
        in_specs=[row(D_MODEL), full(gain), full(w_in_bf), full(qg), full(kg)],
        out_specs=[row(MAIN_WIDTH), row(SWA_WIDTH), row(SWA_KV_WIDTH), row(SWA_KV_WIDTH)],
        out_shape=[jax.ShapeDtypeStruct((m, MAIN_WIDTH), F32),
                   jax.ShapeDtypeStruct((m, SWA_WIDTH), F32),
                   jax.ShapeDtypeStruct((m, SWA_KV_WIDTH), F32),
                   jax.ShapeDtypeStruct((m, SWA_KV_WIDTH), F32)],
        compiler_params=pltpu.CompilerParams(
            dimension_semantics=("arbitrary",), vmem_limit_bytes=VMEM_LIMIT),
        name="in_proj",
    )(x2d, gain, w_in_bf, qg, kg)


FF_CHUNK = 1024


def _out_mlp_kernel(mix_ret_ref, mix_swa_ref, x_ref, w_out_ref, gain_ref, w_up_ref, w_down_ref, y_ref):
    h = x_ref[...] + (_dot(mix_ret_ref[...].astype(BF16), w_out_ref[:RET_WIDTH, :])
                      + _dot(mix_swa_ref[...].astype(BF16), w_out_ref[RET_WIDTH:, :]))
    ms = jnp.mean(h * h, axis=-1, keepdims=True)
    hf = ((h * lax.rsqrt(ms + EPS)) * gain_ref[...]).astype(BF16)
    ff = None
    for c in range(D_FF // FF_CHUNK):
        u = _dot(hf, w_up_ref[:, c * FF_CHUNK:(c + 1) * FF_CHUNK])
        a = jnp.maximum(u, 0.0)
        d = _dot((a * a).astype(BF16), w_down_ref[c * FF_CHUNK:(c + 1) * FF_CHUNK, :])
        ff = d if ff is None else ff + d
    y_ref[...] = h + ff


def _out_mlp(mix_ret, ret_block, mix_swa, swa_block, x2d, w_out_bf, gain, w_up_bf, w_down_bf):
    m = x2d.shape[0]
    tm = min(ROW_TILE, m)
    row = lambda w: pl.BlockSpec((tm, w), lambda i: (i, 0))
    col_block = lambda c: pl.BlockSpec((tm, RET_WIDTH), lambda i: (i, c))
    full = lambda a: pl.BlockSpec(a.shape, lambda i: (0, 0), pipeline_mode=pl.Buffered(1))
    return pl.pallas_call(
        _out_mlp_kernel,
        grid=(m // tm,),
        in_specs=[col_block(ret_block), col_block(swa_block), row(D_MODEL), full(w_out_bf), full(gain),
                  full(w_up_bf), full(w_down_bf)],
        out_specs=row(D_MODEL),
        out_shape=jax.ShapeDtypeStruct((m, D_MODEL), F32),
        compiler_params=pltpu.CompilerParams(
            dimension_semantics=("arbitrary",), vmem_limit_bytes=VMEM_LIMIT),
        name="out_mlp",
    )(mix_ret, mix_swa, x2d, w_out_bf, gain, w_up_bf, w_down_bf)


def _split_pair_rows(x, first):
    return jnp.concatenate([jnp.where(first, x, 0.0), jnp.where(first, 0.0, x)], axis=0).astype(BF16)


def _softmax_sink_pv(s, sink_wide, v_t):
    m = jnp.maximum(jnp.max(s, axis=-1, keepdims=True), sink_wide)
    p = jnp.exp(s - jnp.concatenate([m, m], axis=1))
    denom = jnp.sum(p, axis=-1, keepdims=True) + jnp.exp(sink_wide - m)
    return _dot_nt(p.astype(BF16), v_t) / denom


def _softmax_sink_t(s_t, sink_lanes):
    m = jnp.maximum(jnp.max(s_t, axis=0, keepdims=True), sink_lanes)
    p = jnp.exp(s_t - m)
    denom = jnp.sum(p, axis=0, keepdims=True) + jnp.exp(sink_lanes - m)
    return p.astype(BF16), denom


def _prompt_consts(intra_ref, qdec_ref, kdec_ref, sdec_ref, bias_ref):
    r = _iota((LANES, LANES), 0)
    lane2 = _iota((LANES, LANES), 1) >= HEAD_DIM
    rf = r.astype(F32)
    ri = _iota((LANES, 2 * LANES), 0)
    ci = _iota((LANES, 2 * LANES), 1)
    diff = (ri - (ci & (LANES - 1))).astype(F32)
    for p in range(N_PAIRS):
        lg = _pair_const(LOG_DECAY, p, lane2)
        qdec_ref[p] = jnp.exp(lg * (rf + 1.0))
        kdec_ref[p] = jnp.exp(lg * (RET_CHUNK - 1.0 - rf))
        sdec_ref[p] = jnp.exp(_pair_const(LOG_DECAY, p, r >= HEAD_DIM) * float(RET_CHUNK))
        lg2 = _pair_const(LOG_DECAY, p, ci >= LANES)
        intra_ref[p] = jnp.where(diff >= 0.0, jnp.exp(lg2 * jnp.maximum(diff, 0.0)), 0.0)
    cols = SWA_GROUP * WINDOW
    kb = _iota((2 * WINDOW, cols), 0)
    cb = _iota((2 * WINDOW, cols), 1)
    grp = cb >> 7
    dist = WINDOW + (cb & (WINDOW - 1)) - kb
    valid = (dist >= 0) & (dist < WINDOW)
    distf = dist.astype(F32)
    for j in range(N_SWA_KV):
        sl = [ALIBI_SLOPES[SWA_GROUP * j + g] for g in range(SWA_GROUP)]
        slope = jnp.where(grp == 0, sl[0], jnp.where(grp == 1, sl[1], jnp.where(grp == 2, sl[2], sl[3])))
        b = jnp.where(valid, -(slope.astype(F32) * distf), NEG_INF)
        bias_ref[0, j] = b
        bias_ref[1, j] = jnp.where(kb >= WINDOW, b, NEG_INF)


def _prompt_layer_kernel(sinks_ref, x_ref, x_next_ref, gain_mix_ref, w_in_ref, qg_ref, kg_ref,
                         w_out_ref, gain_ffn_ref, w_up_ref, w_down_ref,
                         y_ref, ret_ref, kwin_ref, vwin_ref,
                         main_ref, qn_ref, kn_ref, vs_ref, mix_ref, hb_ref,
                         state_ref, prevk_ref, prevv_ref,
                         intra_ref, qdec_ref, kdec_ref, sdec_ref, bias_ref):
    t = pl.program_id(1)
    step = pl.program_id(0) * pl.num_programs(1) + t
    cur = step % 2
    nxt = 1 - cur

    @pl.when(step == 0)
    def _():
        _prompt_consts(intra_ref, qdec_ref, kdec_ref, sdec_ref, bias_ref)
        _in_proj_kernel(x_ref, gain_mix_ref, w_in_ref, qg_ref, kg_ref,
                        main_ref.at[0], qn_ref.at[0], kn_ref.at[0], vs_ref.at[0])

    @pl.when(t == 0)
    def _():
        state_ref[...] = jnp.zeros_like(state_ref)
        prevk_ref[...] = jnp.zeros_like(prevk_ref)
        prevv_ref[...] = jnp.zeros_like(prevv_ref)

    first = _first_half()
    ones_bd = _ones_block_diag()
    bd_mask = (_iota((LANES, LANES), 0) >= HEAD_DIM) == (_iota((LANES, LANES), 1) >= HEAD_DIM)

    n_chunks = PROMPT_TILE // RET_CHUNK
    assert n_chunks == 2
    plan = [dict(pre=[], a=["m0", "m1"], b=["v", "k"], c=[]),
            dict(pre=[], a=["m2", "m3"], b=["q0", "q1"], c=[])]
    a0 = MAIN_WIDTH
    attn_cols = {"v": (a0 + SWA_WIDTH + SWA_KV_WIDTH, SWA_KV_WIDTH), "k": (a0 + SWA_WIDTH, SWA_KV_WIDTH)}
    attn_cols.update({"q%d" % i: (a0 + 256 * i, 256) for i in range(SWA_WIDTH // 256)})
    lane_col = lambda base, p: slice(base + p * LANES, base + (p + 1) * LANES)

    def project(items):
        raw = {}
        for it in items:
            if it[0] == "m":
                cb = int(it[1:])
                main_ref[nxt, :, cb * 512:(cb + 1) * 512] = _dot(hb_ref[...], w_in_ref[:, cb * 512:(cb + 1) * 512])
            else:
                lo, width = attn_cols[it]
                raw[it] = _dot(hb_ref[...], w_in_ref[:, lo:lo + width])
        return raw

    def attn_scales(raw):
        return {name: [_head_rms_scale(val[:, lane_col(0, cq)], ones_bd) for cq in range(val.shape[1] // LANES)]
                for name, val in raw.items() if name != "v"}

    def store_attn(raw, scales):
        for name, val in raw.items():
            if name == "v":
                vs_ref[nxt] = val
            elif name == "k":
                kn_ref[nxt] = (val * scales[name][0]) * kg_ref[...]
            else:
                base = attn_cols[name][0] - a0
                for cq in range(val.shape[1] // LANES):
                    qn_ref[nxt, :, lane_col(base, cq)] = (val[:, lane_col(0, cq)] * scales[name][cq]) * qg_ref[...]

    def chunk(c):
        rows = slice(c * RET_CHUNK, (c + 1) * RET_CHUNK)
        pairs = range(N_PAIRS)
        kvs = range(N_SWA_KV)
        cols = lane_col
        if c == 0:
            xn = x_next_ref[...]
            hb_ref[...] = ((xn * lax.rsqrt(jnp.mean(xn * xn, axis=-1, keepdims=True) + EPS))
                           * gain_mix_ref[...]).astype(BF16)
        raw = project(plan[c]["pre"])

        q = [main_ref[cur, rows, cols(0, p)] for p in pairs]
        k = [main_ref[cur, rows, cols(RET_WIDTH, p)] * K_SCALE for p in pairs]
        v = [main_ref[cur, rows, cols(2 * RET_WIDTH, p)] for p in pairs]
        state = [state_ref[p] for p in pairs]
        kc = kn_ref[cur, rows, :]
        k_sw = pltpu.roll(kc, HEAD_DIM, axis=1)
        v_t = vs_ref[cur, rows, :].T
        is_first = (t == 0).astype(jnp.int32) if c == 0 else 0
        k_dup =[(jnp.where(first, kc, k_sw) if j == 0 else jnp.where(first, k_sw, kc)).astype(BF16)
                 for j in kvs]
        v_tj = [v_t[j * HEAD_DIM:(j + 1) * HEAD_DIM].astype(BF16) for j in kvs]
        q_st = []
        for j in kvs:
            pieces = []
            for g in range(SWA_GROUP):
                qc = qn_ref[cur, rows, cols(0, 2 * j + g // 2)]
                pieces.append(jnp.where(first, qc, 0.0) if g % 2 == 0 else jnp.where(first, 0.0, qc))
            q_st.append(jnp.concatenate(pieces, axis=0).astype(BF16))

        s = [_dot_nt(q[p].astype(BF16), _split_pair_rows(k[p], first)) for p in pairs]
        s_t = [_dot_nt(jnp.concatenate([prevk_ref[j], k_dup[j]], axis=0), q_st[j]) for j in kvs]
        cross = [_dot((q[p] * qdec_ref[p]).astype(BF16), state[p].astype(BF16)) for p in pairs]
        raw.update(project(plan[c]["a"]))
        upd = [_dot((k[p] * kdec_ref[p]).T.astype(BF16), v[p].astype(BF16)) for p in pairs]

        o = [_dot((s[p] * intra_ref[p]).astype(BF16), _split_pair_rows(v[p], first)) + cross[p] for p in pairs]
        probs = []
        for j in kvs:
            sink_lanes = jnp.concatenate(
                [jnp.full((1, WINDOW), sinks_ref[SWA_GROUP * j + g], F32) for g in range(SWA_GROUP)],
                axis=1)
            probs.append(_softmax_sink_t(s_t[j] * K_SCALE + bias_ref[is_first, j], sink_lanes))
        raw.update(project(plan[c]["b"]))
        o_t = [_dot(jnp.concatenate([prevv_ref[j], v_tj[j]], axis=1), probs[j][0]) / probs[j][1]
               for j in kvs]
        for p in pairs:
            state_ref[p] = state[p] * sdec_ref[p] + jnp.where(bd_mask, upd[p], 0.0)
        for j in kvs:
            prevk_ref[j] = k_dup[j]
            prevv_ref[j] = v_tj[j]

        scale = [_head_rms_scale(o[p], ones_bd) for p in pairs]
        raw_scales = attn_scales(raw)
        raw_c = project(plan[c]["c"])
        raw_c_scales = attn_scales(raw_c)
        for p in pairs:
            g = main_ref[cur, rows, cols(3 * RET_WIDTH, p)]
            mix_ref[rows, cols(0, p)] = (o[p] * scale[p] * _silu(g)).astype(BF16)
        store_attn(raw, raw_scales)
        store_attn(raw_c, raw_c_scales)
        for j in kvs:
            for half in range(2):
                pair_t = jnp.concatenate([o_t[j][:, (2 * half) * WINDOW:(2 * half + 1) * WINDOW],
                                          o_t[j][:, (2 * half + 1) * WINDOW:(2 * half + 2) * WINDOW]], axis=0)
                mix_ref[rows, cols(RET_WIDTH, 2 * j + half)] = pair_t.T.astype(BF16)

    for c in range(n_chunks):
        chunk(c)

    @pl.when(t == pl.num_programs(1) - 1)
    def _():
        for p in range(N_PAIRS):
            s = state_ref[p]
            ret_ref[2 * p] = s[:HEAD_DIM, :HEAD_DIM]
            ret_ref[2 * p + 1] = s[HEAD_DIM:, HEAD_DIM:]
        last = slice(PROMPT_TILE - WINDOW, PROMPT_TILE)
        kwin_ref[...] = kn_ref[cur, last, :].T
        vwin_ref[...] = vs_ref[cur, last, :].T

    _out_mlp_kernel(mix_ref.at[:, pl.ds(0, RET_WIDTH)], mix_ref.at[:, pl.ds(RET_WIDTH, SWA_WIDTH)], x_ref,
                    w_out_ref, gain_ffn_ref, w_up_ref, w_down_ref, y_ref)


def _prompt_layer(sinks, x2d, gain_mix, w_in_bf, qg, kg, w_out_bf, gain_ffn, w_up_bf, w_down_bf, batch, seq):
    nt = seq // PROMPT_TILE
    last_tile = batch * nt - 1
    row = lambda w: pl.BlockSpec((PROMPT_TILE, w), lambda b, t: (b * nt + t, 0))
    next_row = pl.BlockSpec((PROMPT_TILE, D_MODEL), lambda b, t: (jnp.minimum(b * nt + t + 1, last_tile), 0))
    full = lambda a: pl.BlockSpec(a.shape, lambda b, t: (0, 0), pipeline_mode=pl.Buffered(1))
    return pl.pallas_call(
        _prompt_layer_kernel,
        grid=(batch, nt),
        in_specs=[pl.BlockSpec(memory_space=pltpu.SMEM), row(D_MODEL), next_row,
                  full(gain_mix), full(w_in_bf), full(qg), full(kg),
                  full(w_out_bf), full(gain_ffn), full(w_up_bf), full(w_down_bf)],
        out_specs=[row(D_MODEL),
                   pl.BlockSpec((None, N_RET_HEADS, HEAD_DIM, HEAD_DIM), lambda b, t: (b, 0, 0, 0)),
                   pl.BlockSpec((None, SWA_KV_WIDTH, WINDOW), lambda b, t: (b, 0, 0)),
                   pl.BlockSpec((None, SWA_KV_WIDTH, WINDOW), lambda b, t: (b, 0, 0))],
        out_shape=[jax.ShapeDtypeStruct((batch * seq, D_MODEL), F32),
                   jax.ShapeDtypeStruct((batch, N_RET_HEADS, HEAD_DIM, HEAD_DIM), F32),
                   jax.ShapeDtypeStruct((batch, SWA_KV_WIDTH, WINDOW), F32),
                   jax.ShapeDtypeStruct((batch, SWA_KV_WIDTH, WINDOW), F32)],
        scratch_shapes=[
            pltpu.VMEM((2, PROMPT_TILE, MAIN_WIDTH), F32),
            pltpu.VMEM((2, PROMPT_TILE, SWA_WIDTH), F32),
            pltpu.VMEM((2, PROMPT_TILE, SWA_KV_WIDTH), F32),
            pltpu.VMEM((2, PROMPT_TILE, SWA_KV_WIDTH), F32),
            pltpu.VMEM((PROMPT_TILE, MIX_WIDTH), BF16),
            pltpu.VMEM((PROMPT_TILE, D_MODEL), BF16),
            pltpu.VMEM((N_PAIRS, LANES, LANES), F32),
            pltpu.VMEM((N_SWA_KV, WINDOW, LANES), BF16),
            pltpu.VMEM((N_SWA_KV, HEAD_DIM, WINDOW), BF16),
            pltpu.VMEM((N_PAIRS, LANES, 2 * LANES), F32),
            pltpu.VMEM((N_PAIRS, LANES, LANES), F32),
            pltpu.VMEM((N_PAIRS, LANES, LANES), F32),
            pltpu.VMEM((N_PAIRS, LANES, LANES), F32),
            pltpu.VMEM((2, N_SWA_KV, 2 * WINDOW, SWA_GROUP * WINDOW), F32),
        ],
        compiler_params=pltpu.CompilerParams(
            dimension_semantics=("arbitrary", "arbitrary"), vmem_limit_bytes=VMEM_LIMIT),
        name="prompt_layer",
    )(sinks, x2d, x2d, gain_mix, w_in_bf, qg, kg, w_out_bf, gain_ffn, w_up_bf, w_down_bf)


def _decode_ret_kernel(dec_seq, nb, qdec_ref, kdec_ref, sdec_ref, intra_ref,
                       q_ref, k_ref, v_ref, g_ref, st_ref,
                       mix_ref, st_out_ref,
                       qt_ref, kt_ref, vt_ref, qdt_ref, kdt_ref, o_ref):
    pair = pl.program_id(0)
    halves = [slice(0, HEAD_DIM), slice(HEAD_DIM, 2 * HEAD_DIM)]
    for l in range(dec_seq):
        rows = pl.ds(l, nb, stride=dec_seq)
        q_t = q_ref[rows, :].T
        k_t = (k_ref[rows, :] * K_SCALE).T
        qt_ref[l] = q_t
        kt_ref[l] = k_t
        vt_ref[l] = v_ref[rows, :].T
        for hh in range(2):
            qdt_ref[l, halves[hh], :] = q_t[halves[hh]] * qdec_ref[2 * pair + hh, l]
            kdt_ref[l, halves[hh], :] = k_t[halves[hh]] * kdec_ref[2 * pair + hh, l]

    e_blk = HEAD_DIM // 2
    for hh in range(2):
        h = 2 * pair + hh
        hs = halves[hh]
        for l in range(dec_seq):
            acc = None
            for m in range(l + 1):
                sc = jnp.sum(qt_ref[l, hs, :] * kt_ref[m, hs, :], axis=0, keepdims=True) * intra_ref[h, l - m]
                term = sc * vt_ref[m, hs, :]
                acc = term if acc is None else acc + term
            o_ref[l, hs, :] = acc
        for eb in range(HEAD_DIM // e_blk):
            es = slice(eb * e_blk, (eb + 1) * e_blk)
            erows = slice(hh * HEAD_DIM + eb * e_blk, hh * HEAD_DIM + (eb + 1) * e_blk)

            def body(d, accs, hh=hh, h=h, es=es, erows=erows):
                s_d = st_ref[hh, d, es, :]
                row = pl.ds(hh * HEAD_DIM + d, 1)
                upd = s_d * sdec_ref[h]
                new_accs = []
                for l in range(dec_seq):
                    new_accs.append(accs[l] + qdt_ref[l, row, :] * s_d)
                    upd = upd + kdt_ref[l, row, :] * vt_ref[l, erows, :]
                st_out_ref[hh, d, es, :] = upd
                return tuple(new_accs)

            zero = jnp.zeros((e_blk, nb), F32)
            accs = lax.fori_loop(0, HEAD_DIM, body, tuple(zero for _ in range(dec_seq)))
            for l in range(dec_seq):
                o_ref[l, erows, :] = o_ref[l, erows, :] + accs[l]

    for l in range(dec_seq):
        o = o_ref[l]
        normed = []
        for hh in range(2):
            oh = o[halves[hh]]
            normed.append(oh * lax.rsqrt(jnp.mean(oh * oh, axis=0, keepdims=True) + EPS))
        rows = pl.ds(l, nb, stride=dec_seq)
        mix_ref[rows, :] = jnp.concatenate(normed, axis=0).T * _silu(g_ref[rows, :])


def _decode_ret(main, state_t, dec_seq):
    nb = state_t.shape[-1]
    m = main.shape[0]
    assert nb == LANES and m == nb * dec_seq
    steps = [j for j in range(dec_seq)]
    tab = lambda f: jnp.asarray([[f(h, j) for j in steps] for h in range(N_RET_HEADS)], F32)
    qdec = tab(lambda h, j: math.exp(LOG_DECAY[h] * (j + 1.0)))
    kdec = tab(lambda h, j: math.exp(LOG_DECAY[h] * (dec_seq - 1.0 - j)))
    intra = tab(lambda h, j: math.exp(LOG_DECAY[h] * j))
    sdec = jnp.asarray([math.exp(LOG_DECAY[h] * dec_seq) for h in range(N_RET_HEADS)], F32)
    smem = pl.BlockSpec(memory_space=pltpu.SMEM)
    col = lambda base: pl.BlockSpec((m, LANES), lambda p: (0, base + p))
    st_spec = pl.BlockSpec((2, HEAD_DIM, HEAD_DIM, nb), lambda p: (p, 0, 0, 0))
    stage = pltpu.VMEM((dec_seq, LANES, nb), F32)
    return pl.pallas_call(
        functools.partial(_decode_ret_kernel, dec_seq, nb),
        grid=(N_PAIRS,),
        in_specs=[smem, smem, smem, smem,
                  col(0), col(N_PAIRS), col(2 * N_PAIRS), col(3 * N_PAIRS), st_spec],
        out_specs=[pl.BlockSpec((m, LANES), lambda p: (0, p)), st_spec],
        out_shape=[jax.ShapeDtypeStruct((m, RET_WIDTH), F32),
                   jax.ShapeDtypeStruct(state_t.shape, F32)],
        scratch_shapes=[stage, stage, stage, stage, stage, stage],
        compiler_params=pltpu.CompilerParams(
            dimension_semantics=("arbitrary",), vmem_limit_bytes=VMEM_LIMIT),
        name="decode_ret",
    )(qdec, kdec, sdec, intra, main, main, main, main, state_t)


DEC_ROWS = 128
DEC_UNROLL = 4


def _decode_attn_consts(dec_seq, bias_ref):
    shift = dec_seq.bit_length() - 1
    rows = N_SWA_HEADS * dec_seq
    rb = _iota((rows, WINDOW), 0)
    cb = _iota((rows, WINDOW), 1)
    head = rb >> shift
    i = rb & (dec_seq - 1)
    slope = jnp.zeros((rows, WINDOW), F32)
    for h in range(N_SWA_HEADS):
        slope = jnp.where(head == h, ALIBI_SLOPES[h], slope)
    bias_ref[0] = jnp.where(cb > i, -(slope * (WINDOW + i - cb).astype(F32)), NEG_INF)
    m = cb & (dec_seq - 1)
    bias_ref[1] = jnp.where(m <= i, -(slope * (i - m).astype(F32)), NEG_INF)


def _decode_attn_kernel(dec_seq, sinks_ref, qn_ref, kn_ref, vs_ref, kt_ref, vt_ref,
                        mix_ref, kt_out_ref, vt_out_ref,
                        bias_ref, qbd_ref, oblk_ref, knew_ref, vnew_ref, knt_ref, vst_ref):
    @pl.when(pl.program_id(0) == 0)
    def _():
        _decode_attn_consts(dec_seq, bias_ref)

    first = _first_half()
    shift = dec_seq.bit_length() - 1

    kn_t = kn_ref[...].T
    vs_t = vs_ref[...].T
    knt_ref[...] = kn_t.astype(BF16)
    vst_ref[...] = vs_t.astype(BF16)
    for bb in range(DEC_GROUP):
        sh = (WINDOW - dec_seq - bb * dec_seq) % LANES
        knew_ref[bb] = pltpu.roll(kn_t, sh, axis=1) if sh else kn_t
        vnew_ref[bb] = pltpu.roll(vs_t, sh, axis=1) if sh else vs_t
    qn = qn_ref[...]
    qn_sw = pltpu.roll(qn, HEAD_DIM, axis=1)
    for h in range(N_SWA_HEADS):
        kv_half = h // SWA_GROUP
        if (h % 2) == kv_half:
            src = qn[:, (h // 2) * LANES:(h // 2 + 1) * LANES]
        else:
            col = (h + 1) // 2
            src = qn_sw[:, col * LANES:(col + 1) * LANES]
        qbd_ref[h] = jnp.where(first, src, 0.0) if kv_half == 0 else jnp.where(first, 0.0, src)

    sink_rows = jnp.concatenate(
        [jnp.full((dec_seq, LANES), sinks_ref[h], F32) for h in range(N_SWA_HEADS)], axis=0)
    col_batch = _iota((N_SWA_HEADS * dec_seq, LANES), 1) >> shift
    keep_old = _iota((1, LANES), 1) < WINDOW - dec_seq

    def per_batches(i, carry):
        bs = [i * DEC_UNROLL + u for u in range(DEC_UNROLL)]
        rows = [pl.ds(pl.multiple_of(b * dec_seq, dec_seq), dec_seq) for b in bs]
        k_old = [kt_ref[b] for b in bs]
        v_old = [vt_ref[b] for b in bs]
        q_st = [jnp.concatenate([qbd_ref[h, r, :] for h in range(N_SWA_HEADS)], axis=0).astype(BF16)
                for r in rows]
        s = [_dot(q_st[u], jnp.concatenate([k_old[u].astype(BF16), knt_ref[...]], axis=1))
             for u in range(DEC_UNROLL)]
        o = []
        for u, b in enumerate(bs):
            bias = jnp.concatenate([bias_ref[0], jnp.where(col_batch == b, bias_ref[1], NEG_INF)], axis=1)
            w_v = jnp.concatenate([v_old[u].astype(BF16), vst_ref[...]], axis=1)
            o.append(_softmax_sink_pv(s[u] * K_SCALE + bias, sink_rows, w_v))
        for u, b in enumerate(bs):
            for h in range(N_SWA_HEADS):
                oblk_ref[h, rows[u], :] = o[u][h * dec_seq:(h + 1) * dec_seq]
            kt_out_ref[b] = jnp.where(keep_old, pltpu.roll(k_old[u], LANES - dec_seq, axis=1), knew_ref[b])
            vt_out_ref[b] = jnp.where(keep_old, pltpu.roll(v_old[u], LANES - dec_seq, axis=1), vnew_ref[b])
        return carry

    lax.fori_loop(0, DEC_GROUP // DEC_UNROLL, per_batches, 0)

    y1 = jnp.where(first, oblk_ref[3], oblk_ref[4])
    moved = pltpu.roll(jnp.concatenate([oblk_ref[1], y1, oblk_ref[6], oblk_ref[6]], axis=1), HEAD_DIM, axis=1)
    outs = [
        jnp.where(first, oblk_ref[0], moved[:, 0:LANES]),
        jnp.where(first, oblk_ref[2], moved[:, LANES:2 * LANES]),
        jnp.where(first, moved[:, 2 * LANES:3 * LANES], oblk_ref[5]),
        jnp.where(first, moved[:, 3 * LANES:4 * LANES], oblk_ref[7]),
    ]
    for c in range(SWA_WIDTH // LANES):
        mix_ref[:, c * LANES:(c + 1) * LANES] = outs[c].astype(BF16)


def _decode_attn(sinks, qn, kn, vs, k_t, v_t, dec_seq):
    nb = k_t.shape[0]
    assert DEC_GROUP * dec_seq == DEC_ROWS and nb % DEC_GROUP == 0 and dec_seq & (dec_seq - 1) == 0
    assert k_t.shape[1:] == (SWA_KV_WIDTH, WINDOW)
    row = lambda w: pl.BlockSpec((DEC_ROWS, w), lambda i: (i, 0))
    cache = pl.BlockSpec((DEC_GROUP, SWA_KV_WIDTH, WINDOW), lambda i: (i, 0, 0))
    return pl.pallas_call(
        functools.partial(_decode_attn_kernel, dec_seq),
        grid=(nb // DEC_GROUP,),
        in_specs=[pl.BlockSpec(memory_space=pltpu.SMEM),
                  row(SWA_WIDTH), row(SWA_KV_WIDTH), row(SWA_KV_WIDTH), cache, cache],
        out_specs=[row(SWA_WIDTH), cache, cache],
        out_shape=[jax.ShapeDtypeStruct((nb * dec_seq, SWA_WIDTH), BF16),
                   jax.ShapeDtypeStruct(k_t.shape, F32),
                   jax.ShapeDtypeStruct(v_t.shape, F32)],
        scratch_shapes=[
            pltpu.VMEM((2, N_SWA_HEADS * dec_seq, WINDOW), F32),
            pltpu.VMEM((N_SWA_HEADS, DEC_ROWS, LANES), F32),
            pltpu.VMEM((N_SWA_HEADS, DEC_ROWS, LANES), F32),
            pltpu.VMEM((DEC_GROUP, SWA_KV_WIDTH, LANES), F32),
            pltpu.VMEM((DEC_GROUP, SWA_KV_WIDTH, LANES), F32),
            pltpu.VMEM((SWA_KV_WIDTH, DEC_ROWS), BF16),
            pltpu.VMEM((SWA_KV_WIDTH, DEC_ROWS), BF16),
        ],
        compiler_params=pltpu.CompilerParams(
            dimension_semantics=("arbitrary",), vmem_limit_bytes=VMEM_LIMIT),
        name="decode_attn",
    )(sinks, qn, kn, vs, k_t, v_t)


def kernel(x_prompt, x_sample, state_ret, cache_swa_k, cache_swa_v, norm_mix_gain, w_in, q_norm_gain,
           k_norm_gain, attn_sinks, w_out, norm_ffn_gain, w_up, w_down):
    batch, seq, d = x_prompt.shape
    nb, dec_seq, _ = x_sample.shape
    wb = cache_swa_k.shape[1]
    assert d == D_MODEL and seq % PROMPT_TILE == 0 and wb == WINDOW

    w_in_bf = w_in.astype(BF16)
    w_out_bf = w_out.astype(BF16)
    w_up_bf = w_up.astype(BF16)
    w_down_bf = w_down.astype(BF16)
    gain_mix = norm_mix_gain.reshape(1, D_MODEL)
    gain_ffn = norm_ffn_gain.reshape(1, D_MODEL)
    qg = jnp.tile(q_norm_gain, 2).reshape(1, LANES)
    kg = jnp.tile(k_norm_gain, 2).reshape(1, LANES)

    xp = x_prompt.reshape(batch * seq, D_MODEL)
    y_p, ret_p, kwin_t, vwin_t = _prompt_layer(attn_sinks, xp, gain_mix, w_in_bf, qg, kg, w_out_bf, gain_ffn,
                                               w_up_bf, w_down_bf, batch, seq)
    y_p = y_p.reshape(batch, seq, D_MODEL)

    def from_key_minor(a_t):
        return jnp.transpose(a_t.reshape(a_t.shape[0], N_SWA_KV, HEAD_DIM, WINDOW), (0, 3, 1, 2))

    def to_key_minor(a):
        return jnp.transpose(a, (0, 2, 3, 1)).reshape(a.shape[0], SWA_KV_WIDTH, WINDOW)

    xs = x_sample.reshape(nb * dec_seq, D_MODEL)
    main_s, qn_s, kn_s, vs_s = _in_proj(xs, gain_mix, w_in_bf, qg, kg)
    mix_ret_s, state_t = _decode_ret(main_s, jnp.transpose(state_ret, (1, 2, 3, 0)), dec_seq)
    mix_swa_s, k_t, v_t = _decode_attn(attn_sinks, qn_s, kn_s, vs_s,
                                       to_key_minor(cache_swa_k), to_key_minor(cache_swa_v), dec_seq)
    y_s = _out_mlp(mix_ret_s, 0, mix_swa_s, 0, xs, w_out_bf, gain_ffn, w_up_bf, w_down_bf)

    return (y_p, y_s.reshape(nb, dec_seq, D_MODEL), ret_p, from_key_minor(kwin_t), from_key_minor(vwin_t),
            jnp.transpose(state_t, (3, 0, 1, 2)), from_key_minor(k_t), from_key_minor(v_t))
```

```python
import functools
import math

import jax
import jax.numpy as jnp
from jax import lax
from jax.experimental import pallas as pl
from jax.experimental.pallas import tpu as pltpu

F32 = jnp.float32
BF16 = jnp.bfloat16

D_MODEL = 1024
HEAD_DIM = 64
N_RET_HEADS = 8
N_SWA_HEADS = 8
N_SWA_KV = 2
SWA_GROUP = N_SWA_HEADS // N_SWA_KV
RET_WIDTH = N_RET_HEADS * HEAD_DIM
SWA_WIDTH = N_SWA_HEADS * HEAD_DIM
SWA_KV_WIDTH = N_SWA_KV * HEAD_DIM
MAIN_WIDTH = 4 * RET_WIDTH
IN_WIDTH = MAIN_WIDTH + SWA_WIDTH + 2 * SWA_KV_WIDTH
MIX_WIDTH = RET_WIDTH + SWA_WIDTH
D_FF = 4 * D_MODEL
WINDOW = 128
RET_CHUNK = 128
EPS = 1e-6
NEG_INF = -1e30

LANES = 128
N_PAIRS = N_RET_HEADS // 2
LOG_DECAY = [math.log(1.0 - 2.0 ** (-5.0 - h)) for h in range(N_RET_HEADS)]
ALIBI_SLOPES = [2.0 ** (-8.0 * (h + 1) / N_SWA_HEADS) for h in range(N_SWA_HEADS)]
K_SCALE = HEAD_DIM ** -0.5

ROW_TILE = 512
PROMPT_TILE = 512
DEC_GROUP = 16
VMEM_LIMIT = 56 * 1024 * 1024
PROMPT_VMEM_LIMIT = 62 * 1024 * 1024


def _dot(a, b):
    return jnp.dot(a, b, preferred_element_type=F32)


def _dot_nt(a, b):
    return lax.dot_general(a, b, (((1,), (1,)), ((), ())), preferred_element_type=F32)


def _iota(shape, dim):
    return lax.broadcasted_iota(jnp.int32, shape, dim)


def _ones_block_diag():
    same = ((_iota((2 * LANES, LANES), 0) >> 6) & 1) == (_iota((2 * LANES, LANES), 1) >> 6)
    return jnp.where(same, 1.0, 0.0).astype(BF16)


def _head_sumsq(x, ones_bd):
    x2 = x * x
    hi = x2.astype(BF16)
    lo = (x2 - hi.astype(F32)).astype(BF16)
    return _dot(jnp.concatenate([hi, lo], axis=1), ones_bd)


def _head_rms_scale(x, ones_bd):
    return lax.rsqrt(_head_sumsq(x, ones_bd) * (1.0 / HEAD_DIM) + EPS)


def _first_half():
    return _iota((1, LANES), 1) < HEAD_DIM


def _pair_const(values, pair, lane_is_second):
    return jnp.where(lane_is_second, values[2 * pair + 1], values[2 * pair]).astype(F32)


def _silu(g):
    return g * (1.0 / (1.0 + jnp.exp(-g)))


def _in_proj_kernel(x_ref, gain_ref, w_ref, qg_ref, kg_ref, main_ref, qn_ref, kn_ref, vs_ref):
    x = x_ref[...]
    ms = jnp.mean(x * x, axis=-1, keepdims=True)
    hb = ((x * lax.rsqrt(ms + EPS)) * gain_ref[...]).astype(BF16)
    ones_bd = _ones_block_diag()
    for c in range(MAIN_WIDTH // 512):
        main_ref[:, c * 512:(c + 1) * 512] = _dot(hb, w_ref[:, c * 512:(c + 1) * 512])
    qs = _dot(hb, w_ref[:, MAIN_WIDTH:MAIN_WIDTH + SWA_WIDTH])
    for c in range(SWA_WIDTH // LANES):
        xc = qs[:, c * LANES:(c + 1) * LANES]
        qn_ref[:, c * LANES:(c + 1) * LANES] = (xc * _head_rms_scale(xc, ones_bd)) * qg_ref[...]
    k0 = MAIN_WIDTH + SWA_WIDTH
    ks = _dot(hb, w_ref[:, k0:k0 + SWA_KV_WIDTH])
    kn_ref[...] = (ks * _head_rms_scale(ks, ones_bd)) * kg_ref[...]
    vs_ref[...] = _dot(hb, w_ref[:, k0 + SWA_KV_WIDTH:k0 + 2 * SWA_KV_WIDTH])


def _in_proj(x2d, gain, w_in_bf, qg, kg):
    m = x2d.shape[0]
    tm = min(ROW_TILE, m)
    row = lambda w: pl.BlockSpec((tm, w), lambda i: (i, 0))
    full = lambda a: pl.BlockSpec(a.shape, lambda i: (0, 0), pipeline_mode=pl.Buffered(1))
    return pl.pallas_call(
        _in_proj_kernel,
        grid=(m // tm,),
        in_specs=[row(D_MODEL), full(gain), full(w_in_bf), full(qg), full(kg)],
        out_specs=[row(MAIN_WIDTH), row(SWA_WIDTH), row(SWA_KV_WIDTH), row(SWA_KV_WIDTH)],
        out_shape=[jax.ShapeDtypeStruct((m, MAIN_WIDTH), F32),
                   jax.ShapeDtypeStruct((m, SWA_WIDTH), F32),
                   jax.ShapeDtypeStruct((m, SWA_KV_WIDTH), F32),
                   jax.ShapeDtypeStruct((m, SWA_KV_WIDTH), F32)],
        compiler_params=pltpu.CompilerParams(
            dimension_semantics=("arbitrary",), vmem_limit_bytes=VMEM_LIMIT),
        name="in_proj",
    )(x2d, gain, w_in_bf, qg, kg)


FF_CHUNK = 1024


def _out_mlp_kernel(mix_ret_ref, mix_swa_ref, x_ref, w_out_ref, gain_ref, w_up_ref, w_down_ref, y_ref):
    h = x_ref[...] + (_dot(mix_ret_ref[...].astype(BF16), w_out_ref[:RET_WIDTH, :])
                      + _dot(mix_swa_ref[...].astype(BF16), w_out_ref[RET_WIDTH:, :]))
    ms = jnp.mean(h * h, axis=-1, keepdims=True)
    hf = ((h * lax.rsqrt(ms + EPS)) * gain_ref[...]).astype(BF16)
    ff = None
    for c in range(D_FF // FF_CHUNK):
        u = _dot(hf, w_up_ref[:, c * FF_CHUNK:(c + 1) * FF_CHUNK])
        a = jnp.maximum(u, 0.0)
        d = _dot((a * a).astype(BF16), w_down_ref[c * FF_CHUNK:(c + 1) * FF_CHUNK, :])
        ff = d if ff is None else ff + d
    y_ref[...] = h + ff


def _out_mlp(mix_ret, ret_block, mix_swa, swa_block, x2d, w_out_bf, gain, w_up_bf, w_down_bf):
    m = x2d.shape[0]
    tm = min(ROW_TILE, m)
    row = lambda w: pl.BlockSpec((tm, w), lambda i: (i, 0))
    col_block = lambda c: pl.BlockSpec((tm, RET_WIDTH), lambda i: (i, c))
    full = lambda a: pl.BlockSpec(a.shape, lambda i: (0, 0), pipeline_mode=pl.Buffered(1))
    return pl.pallas_call(
        _out_mlp_kernel,
        grid=(m // tm,),
        in_specs=[col_block(ret_block), col_block(swa_block), row(D_MODEL), full(w_out_bf), full(gain),
                  full(w_up_bf), full(w_down_bf)],
        out_specs=row(D_MODEL),
        out_shape=jax.ShapeDtypeStruct((m, D_MODEL), F32),
        compiler_params=pltpu.CompilerParams(
            dimension_semantics=("arbitrary",), vmem_limit_bytes=VMEM_LIMIT),
        name="out_mlp",
    )(mix_ret, mix_swa, x2d, w_out_bf, gain, w_up_bf, w_down_bf)


def _split_pair_rows(x, first):
    return jnp.concatenate([jnp.where(first, x, 0.0), jnp.where(first, 0.0, x)], axis=0).astype(BF16)


def _softmax_sink_pv(s, sink_wide, v_t):
    m = jnp.maximum(jnp.max(s, axis=-1, keepdims=True), sink_wide)
    p = jnp.exp(s - jnp.concatenate([m, m], axis=1))
    denom = jnp.sum(p, axis=-1, keepdims=True) + jnp.exp(sink_wide - m)
    return _dot_nt(p.astype(BF16), v_t) / denom


def _softmax_sink_pv_t(s_t, sink_lanes, v_t):
    m = jnp.maximum(jnp.max(s_t, axis=0, keepdims=True), sink_lanes)
    p = jnp.exp(s_t - m)
    denom = jnp.sum(p, axis=0, keepdims=True) + jnp.exp(sink_lanes - m)
    return _dot(v_t, p.astype(BF16)) / denom


def _prompt_consts(intra_ref, qdec_ref, kdec_ref, sdec_ref, bias_ref):
    r = _iota((LANES, LANES), 0)
    lane2 = _iota((LANES, LANES), 1) >= HEAD_DIM
    rf = r.astype(F32)
    ri = _iota((LANES, 2 * LANES), 0)
    ci = _iota((LANES, 2 * LANES), 1)
    diff = (ri - (ci & (LANES - 1))).astype(F32)
    for p in range(N_PAIRS):
        lg = _pair_const(LOG_DECAY, p, lane2)
        qdec_ref[p] = jnp.exp(lg * (rf + 1.0))
        kdec_ref[p] = jnp.exp(lg * (RET_CHUNK - 1.0 - rf))
        sdec_ref[p] = jnp.exp(_pair_const(LOG_DECAY, p, r >= HEAD_DIM) * float(RET_CHUNK))
        lg2 = _pair_const(LOG_DECAY, p, ci >= LANES)
        intra_ref[p] = jnp.where(diff >= 0.0, jnp.exp(lg2 * jnp.maximum(diff, 0.0)), 0.0)
    cols = SWA_GROUP * WINDOW
    kb = _iota((2 * WINDOW, cols), 0)
    cb = _iota((2 * WINDOW, cols), 1)
    grp = cb >> 7
    dist = WINDOW + (cb & (WINDOW - 1)) - kb
    valid = (dist >= 0) & (dist < WINDOW)
    distf = dist.astype(F32)
    for j in range(N_SWA_KV):
        sl = [ALIBI_SLOPES[SWA_GROUP * j + g] for g in range(SWA_GROUP)]
        slope = jnp.where(grp == 0, sl[0], jnp.where(grp == 1, sl[1], jnp.where(grp == 2, sl[2], sl[3])))
        b = jnp.where(valid, -(slope.astype(F32) * distf), NEG_INF)
        bias_ref[0, j] = b
        bias_ref[1, j] = jnp.where(kb >= WINDOW, b, NEG_INF)


def _prompt_layer_kernel(sinks_ref, x_ref, x_next_ref, gain_mix_ref, w_in_ref, qg_ref, kg_ref,
                         w_out_ref, gain_ffn_ref, w_up_ref, w_down_ref,
                         y_ref, ret_ref, kwin_ref, vwin_ref,
                         main_ref, qn_ref, kn_ref, vs_ref, mix_ref,
                         state_ref, prevk_ref, prevv_ref,
                         intra_ref, qdec_ref, kdec_ref, sdec_ref, bias_ref):
    t = pl.program_id(1)
    step = pl.program_id(0) * pl.num_programs(1) + t
    cur = step % 2
    nxt = 1 - cur

    @pl.when(step == 0)
    def _():
        _prompt_consts(intra_ref, qdec_ref, kdec_ref, sdec_ref, bias_ref)
        _in_proj_kernel(x_ref, gain_mix_ref, w_in_ref, qg_ref, kg_ref,
                        main_ref.at[0], qn_ref.at[0], kn_ref.at[0], vs_ref.at[0])

    @pl.when(t == 0)
    def _():
        state_ref[...] = jnp.zeros_like(state_ref)
        prevk_ref[...] = jnp.zeros_like(prevk_ref)
        prevv_ref[...] = jnp.zeros_like(prevv_ref)

    first = _first_half()
    ones_bd = _ones_block_diag()
    bd_mask = (_iota((LANES, LANES), 0) >= HEAD_DIM) == (_iota((LANES, LANES), 1) >= HEAD_DIM)

    def chunk(c, carry):
        r0 = pl.multiple_of(c * RET_CHUNK, RET_CHUNK)
        rows = pl.ds(r0, RET_CHUNK)
        pairs = range(N_PAIRS)
        kvs = range(N_SWA_KV)
        cols = lambda base, p: slice(base + p * LANES, base + (p + 1) * LANES)
        xn = x_next_ref[rows, :]
        hb = ((xn * lax.rsqrt(jnp.mean(xn * xn, axis=-1, keepdims=True) + EPS)) * gain_mix_ref[...]).astype(BF16)

        q = [main_ref[cur, rows, cols(0, p)] for p in pairs]
        k = [main_ref[cur, rows, cols(RET_WIDTH, p)] * K_SCALE for p in pairs]
        v = [main_ref[cur, rows, cols(2 * RET_WIDTH, p)] for p in pairs]
        state = [state_ref[p] for p in pairs]
        kc = kn_ref[cur, rows, :]
        k_sw = pltpu.roll(kc, HEAD_DIM, axis=1)
        v_t = vs_ref[cur, rows, :].T
        is_first = ((t == 0) & (c == 0)).astype(jnp.int32)
        k_dup = [(jnp.where(first, kc, k_sw) if j == 0 else jnp.where(first, k_sw, kc)).astype(BF16)
                 for j in kvs]
        v_tj = [v_t[j * HEAD_DIM:(j + 1) * HEAD_DIM].astype(BF16) for j in kvs]
        q_st = []
        for j in kvs:
            pieces = []
            for g in range(SWA_GROUP):
                qc = qn_ref[cur, rows, cols(0, 2 * j + g // 2)]
                pieces.append(jnp.where(first, qc, 0.0) if g % 2 == 0 else jnp.where(first, 0.0, qc))
            q_st.append(jnp.concatenate(pieces, axis=0).astype(BF16))

        s = [_dot_nt(q[p].astype(BF16), _split_pair_rows(k[p], first)) for p in pairs]
        s_t = [_dot_nt(jnp.concatenate([prevk_ref[j], k_dup[j]], axis=0), q_st[j]) for j in kvs]
        cross = [_dot((q[p] * qdec_ref[p]).astype(BF16), state[p].astype(BF16)) for p in pairs]
        upd = [_dot((k[p] * kdec_ref[p]).T.astype(BF16), v[p].astype(BF16)) for p in pairs]
        for cb in range(MAIN_WIDTH // 512):
            main_ref[nxt, rows, cb * 512:(cb + 1) * 512] = _dot(hb, w_in_ref[:, cb * 512:(cb + 1) * 512])

        o = [_dot((s[p] * intra_ref[p]).astype(BF16), _split_pair_rows(v[p], first)) + cross[p] for p in pairs]
        o_t = []
        for j in kvs:
            sink_lanes = jnp.concatenate(
                [jnp.full((1, WINDOW), sinks_ref[SWA_GROUP * j + g], F32) for g in range(SWA_GROUP)],
                axis=1)
            v_cat = jnp.concatenate([prevv_ref[j], v_tj[j]], axis=1)
            o_t.append(_softmax_sink_pv_t(s_t[j] * K_SCALE + bias_ref[is_first, j], sink_lanes, v_cat))
        for p in pairs:
            state_ref[p] = state[p] * sdec_ref[p] + jnp.where(bd_mask, upd[p], 0.0)
        for j in kvs:
            prevk_ref[j] = k_dup[j]
            prevv_ref[j] = v_tj[j]
        a0 = MAIN_WIDTH
        qs = _dot(hb, w_in_ref[:, a0:a0 + SWA_WIDTH])
        ks = _dot(hb, w_in_ref[:, a0 + SWA_WIDTH:a0 + SWA_WIDTH + SWA_KV_WIDTH])
        vs_ref[nxt, rows, :] = _dot(hb, w_in_ref[:, a0 + SWA_WIDTH + SWA_KV_WIDTH:a0 + SWA_WIDTH + 2 * SWA_KV_WIDTH])

        scale = [_head_rms_scale(o[p], ones_bd) for p in pairs]
        q_scale = [_head_rms_scale(qs[:, cols(0, cq)], ones_bd) for cq in range(SWA_WIDTH // LANES)]
        k_scale = _head_rms_scale(ks, ones_bd)
        for p in pairs:
            g = main_ref[cur, rows, cols(3 * RET_WIDTH, p)]
            mix_ref[rows, cols(0, p)] = (o[p] * scale[p] * _silu(g)).astype(BF16)
        for cq in range(SWA_WIDTH // LANES):
            qn_ref[nxt, rows, cols(0, cq)] = (qs[:, cols(0, cq)] * q_scale[cq]) * qg_ref[...]
        kn_ref[nxt, rows, :] = (ks * k_scale) * kg_ref[...]
        for j in kvs:
            for half in range(2):
                pair_t = jnp.concatenate([o_t[j][:, (2 * half) * WINDOW:(2 * half + 1) * WINDOW],
                                          o_t[j][:, (2 * half + 1) * WINDOW:(2 * half + 2) * WINDOW]], axis=0)
                mix_ref[rows, cols(RET_WIDTH, 2 * j + half)] = pair_t.T.astype(BF16)
        return carry

    lax.fori_loop(0, PROMPT_TILE // RET_CHUNK, chunk, 0)

    @pl.when(t == pl.num_programs(1) - 1)
    def _():
        for p in range(N_PAIRS):
            s = state_ref[p]
            ret_ref[2 * p] = s[:HEAD_DIM, :HEAD_DIM]
            ret_ref[2 * p + 1] = s[HEAD_DIM:, HEAD_DIM:]
        last = slice(PROMPT_TILE - WINDOW, PROMPT_TILE)
        kwin_ref[...] = kn_ref[cur, last, :].T
        vwin_ref[...] = vs_ref[cur, last, :].T

    _out_mlp_kernel(mix_ref.at[:, pl.ds(0, RET_WIDTH)], mix_ref.at[:, pl.ds(RET_WIDTH, SWA_WIDTH)], x_ref,
                    w_out_ref, gain_ffn_ref, w_up_ref, w_down_ref, y_ref)


def _prompt_layer(sinks, x2d, gain_mix, w_in_bf, qg, kg, w_out_bf, gain_ffn, w_up_bf, w_down_bf, batch, seq):
    nt = seq // PROMPT_TILE
    last_tile = batch * nt - 1
    row = lambda w: pl.BlockSpec((PROMPT_TILE, w), lambda b, t: (b * nt + t, 0))
    next_row = pl.BlockSpec((PROMPT_TILE, D_MODEL), lambda b, t: (jnp.minimum(b * nt + t + 1, last_tile), 0))
    full = lambda a: pl.BlockSpec(a.shape, lambda b, t: (0, 0), pipeline_mode=pl.Buffered(1))
    return pl.pallas_call(
        _prompt_layer_kernel,
        grid=(batch, nt),
        in_specs=[pl.BlockSpec(memory_space=pltpu.SMEM), row(D_MODEL), next_row,
                  full(gain_mix), full(w_in_bf), full(qg), full(kg),
                  full(w_out_bf), full(gain_ffn), full(w_up_bf), full(w_down_bf)],
        out_specs=[row(D_MODEL),
                   pl.BlockSpec((None, N_RET_HEADS, HEAD_DIM, HEAD_DIM), lambda b, t: (b, 0, 0, 0)),
                   pl.BlockSpec((None, SWA_KV_WIDTH, WINDOW), lambda b, t: (b, 0, 0)),
                   pl.BlockSpec((None, SWA_KV_WIDTH, WINDOW), lambda b, t: (b, 0, 0))],
        out_shape=[jax.ShapeDtypeStruct((batch * seq, D_MODEL), F32),
                   jax.ShapeDtypeStruct((batch, N_RET_HEADS, HEAD_DIM, HEAD_DIM), F32),
                   jax.ShapeDtypeStruct((batch, SWA_KV_WIDTH, WINDOW), F32),
                   jax.ShapeDtypeStruct((batch, SWA_KV_WIDTH, WINDOW), F32)],
        scratch_shapes=[
            pltpu.VMEM((2, PROMPT_TILE, MAIN_WIDTH), F32),
            pltpu.VMEM((2, PROMPT_TILE, SWA_WIDTH), F32),
            pltpu.VMEM((2, PROMPT_TILE, SWA_KV_WIDTH), F32),
            pltpu.VMEM((2, PROMPT_TILE, SWA_KV_WIDTH), F32),
            pltpu.VMEM((PROMPT_TILE, MIX_WIDTH), BF16),
            pltpu.VMEM((N_PAIRS, LANES, LANES), F32),
            pltpu.VMEM((N_SWA_KV, WINDOW, LANES), BF16),
            pltpu.VMEM((N_SWA_KV, HEAD_DIM, WINDOW), BF16),
            pltpu.VMEM((N_PAIRS, LANES, 2 * LANES), F32),
            pltpu.VMEM((N_PAIRS, LANES, LANES), F32),
            pltpu.VMEM((N_PAIRS, LANES, LANES), F32),
            pltpu.VMEM((N_PAIRS, LANES, LANES), F32),
            pltpu.VMEM((2, N_SWA_KV, 2 * WINDOW, SWA_GROUP * WINDOW), F32),
        ],
        compiler_params=pltpu.CompilerParams(
            dimension_semantics=("arbitrary", "arbitrary"), vmem_limit_bytes=PROMPT_VMEM_LIMIT),
        name="prompt_layer",
    )(sinks, x2d, x2d, gain_mix, w_in_bf, qg, kg, w_out_bf, gain_ffn, w_up_bf, w_down_bf)


def _decode_ret_kernel(dec_seq, nb, qdec_ref, kdec_ref, sdec_ref, intra_ref,
                       q_ref, k_ref, v_ref, g_ref, st_ref,
                       mix_ref, st_out_ref,
                       qt_ref, kt_ref, vt_ref, qdt_ref, kdt_ref, o_ref):
    pair = pl.program_id(0)
    halves = [slice(0, HEAD_DIM), slice(HEAD_DIM, 2 * HEAD_DIM)]
    for l in range(dec_seq):
        rows = pl.ds(l, nb, stride=dec_seq)
        q_t = q_ref[rows, :].T
        k_t = (k_ref[rows, :] * K_SCALE).T
        qt_ref[l] = q_t
        kt_ref[l] = k_t
        vt_ref[l] = v_ref[rows, :].T
        for hh in range(2):
            qdt_ref[l, halves[hh], :] = q_t[halves[hh]] * qdec_ref[2 * pair + hh, l]
            kdt_ref[l, halves[hh], :] = k_t[halves[hh]] * kdec_ref[2 * pair + hh, l]

    e_blk = HEAD_DIM // 2
    for hh in range(2):
        h = 2 * pair + hh
        hs = halves[hh]
        for l in range(dec_seq):
            acc = None
            for m in range(l + 1):
                sc = jnp.sum(qt_ref[l, hs, :] * kt_ref[m, hs, :], axis=0, keepdims=True) * intra_ref[h, l - m]
                term = sc * vt_ref[m, hs, :]
                acc = term if acc is None else acc + term
            o_ref[l, hs, :] = acc
        for eb in range(HEAD_DIM // e_blk):
            es = slice(eb * e_blk, (eb + 1) * e_blk)
            erows = slice(hh * HEAD_DIM + eb * e_blk, hh * HEAD_DIM + (eb + 1) * e_blk)

            def body(d, accs, hh=hh, h=h, es=es, erows=erows):
                s_d = st_ref[hh, d, es, :]
                row = pl.ds(hh * HEAD_DIM + d, 1)
                upd = s_d * sdec_ref[h]
                new_accs = []
                for l in range(dec_seq):
                    new_accs.append(accs[l] + qdt_ref[l, row, :] * s_d)
                    upd = upd + kdt_ref[l, row, :] * vt_ref[l, erows, :]
                st_out_ref[hh, d, es, :] = upd
                return tuple(new_accs)

            zero = jnp.zeros((e_blk, nb), F32)
            accs = lax.fori_loop(0, HEAD_DIM, body, tuple(zero for _ in range(dec_seq)))
            for l in range(dec_seq):
                o_ref[l, erows, :] = o_ref[l, erows, :] + accs[l]

    for l in range(dec_seq):
        o = o_ref[l]
        normed = []
        for hh in range(2):
            oh = o[halves[hh]]
            normed.append(oh * lax.rsqrt(jnp.mean(oh * oh, axis=0, keepdims=True) + EPS))
        rows = pl.ds(l, nb, stride=dec_seq)
        mix_ref[rows, :] = jnp.concatenate(normed, axis=0).T * _silu(g_ref[rows, :])


def _decode_ret(main, state_t, dec_seq):
    nb = state_t.shape[-1]
    m = main.shape[0]
    assert nb == LANES and m == nb * dec_seq
    steps = [j for j in range(dec_seq)]
    tab = lambda f: jnp.asarray([[f(h, j) for j in steps] for h in range(N_RET_HEADS)], F32)
    qdec = tab(lambda h, j: math.exp(LOG_DECAY[h] * (j + 1.0)))
    kdec = tab(lambda h, j: math.exp(LOG_DECAY[h] * (dec_seq - 1.0 - j)))
    intra = tab(lambda h, j: math.exp(LOG_DECAY[h] * j))
    sdec = jnp.asarray([math.exp(LOG_DECAY[h] * dec_seq) for h in range(N_RET_HEADS)], F32)
    smem = pl.BlockSpec(memory_space=pltpu.SMEM)
    col = lambda base: pl.BlockSpec((m, LANES), lambda p: (0, base + p))
    st_spec = pl.BlockSpec((2, HEAD_DIM, HEAD_DIM, nb), lambda p: (p, 0, 0, 0))
    stage = pltpu.VMEM((dec_seq, LANES, nb), F32)
    return pl.pallas_call(
        functools.partial(_decode_ret_kernel, dec_seq, nb),
        grid=(N_PAIRS,),
        in_specs=[smem, smem, smem, smem,
                  col(0), col(N_PAIRS), col(2 * N_PAIRS), col(3 * N_PAIRS), st_spec],
        out_specs=[pl.BlockSpec((m, LANES), lambda p: (0, p)), st_spec],
        out_shape=[jax.ShapeDtypeStruct((m, RET_WIDTH), F32),
                   jax.ShapeDtypeStruct(state_t.shape, F32)],
        scratch_shapes=[stage, stage, stage, stage, stage, stage],
        compiler_params=pltpu.CompilerParams(
            dimension_semantics=("arbitrary",), vmem_limit_bytes=VMEM_LIMIT),
        name="decode_ret",
    )(qdec, kdec, sdec, intra, main, main, main, main, state_t)


DEC_ROWS = 128
DEC_UNROLL = 4


def _decode_attn_consts(dec_seq, bias_ref):
    shift = dec_seq.bit_length() - 1
    rows = N_SWA_HEADS * dec_seq
    rb = _iota((rows, WINDOW), 0)
    cb = _iota((rows, WINDOW), 1)
    head = rb >> shift
    i = rb & (dec_seq - 1)
    slope = jnp.zeros((rows, WINDOW), F32)
    for h in range(N_SWA_HEADS):
        slope = jnp.where(head == h, ALIBI_SLOPES[h], slope)
    bias_ref[0] = jnp.where(cb > i, -(slope * (WINDOW + i - cb).astype(F32)), NEG_INF)
    m = cb & (dec_seq - 1)
    bias_ref[1] = jnp.where(m <= i, -(slope * (i - m).astype(F32)), NEG_INF)


def _decode_attn_kernel(dec_seq, sinks_ref, qn_ref, kn_ref, vs_ref, kt_ref, vt_ref,
                        mix_ref, kt_out_ref, vt_out_ref,
                        bias_ref, qbd_ref, oblk_ref, knew_ref, vnew_ref, knt_ref, vst_ref):
    @pl.when(pl.program_id(0) == 0)
    def _():
        _decode_attn_consts(dec_seq, bias_ref)

    first = _first_half()
    shift = dec_seq.bit_length() - 1

    kn_t = kn_ref[...].T
    vs_t = vs_ref[...].T
    knt_ref[...] = kn_t.astype(BF16)
    vst_ref[...] = vs_t.astype(BF16)
    for bb in range(DEC_GROUP):
        sh = (WINDOW - dec_seq - bb * dec_seq) % LANES
        knew_ref[bb] = pltpu.roll(kn_t, sh, axis=1) if sh else kn_t
        vnew_ref[bb] = pltpu.roll(vs_t, sh, axis=1) if sh else vs_t
    qn = qn_ref[...]
    qn_sw = pltpu.roll(qn, HEAD_DIM, axis=1)
    for h in range(N_SWA_HEADS):
        kv_half = h // SWA_GROUP
        if (h % 2) == kv_half:
            src = qn[:, (h // 2) * LANES:(h // 2 + 1) * LANES]
        else:
            col = (h + 1) // 2
            src = qn_sw[:, col * LANES:(col + 1) * LANES]
        qbd_ref[h] = jnp.where(first, src, 0.0) if kv_half == 0 else jnp.where(first, 0.0, src)

    sink_rows = jnp.concatenate(
        [jnp.full((dec_seq, LANES), sinks_ref[h], F32) for h in range(N_SWA_HEADS)], axis=0)
    col_batch = _iota((N_SWA_HEADS * dec_seq, LANES), 1) >> shift
    keep_old = _iota((1, LANES), 1) < WINDOW - dec_seq

    def per_batches(i, carry):
        bs = [i * DEC_UNROLL + u for u in range(DEC_UNROLL)]
        rows = [pl.ds(pl.multiple_of(b * dec_seq, dec_seq), dec_seq) for b in bs]
        k_old = [kt_ref[b] for b in bs]
        v_old = [vt_ref[b] for b in bs]
        q_st = [jnp.concatenate([qbd_ref[h, r, :] for h in range(N_SWA_HEADS)], axis=0).astype(BF16)
                for r in rows]
        s = [_dot(q_st[u], jnp.concatenate([k_old[u].astype(BF16), knt_ref[...]], axis=1))
             for u in range(DEC_UNROLL)]
        o = []
        for u, b in enumerate(bs):
            bias = jnp.concatenate([bias_ref[0], jnp.where(col_batch == b, bias_ref[1], NEG_INF)], axis=1)
            w_v = jnp.concatenate([v_old[u].astype(BF16), vst_ref[...]], axis=1)
            o.append(_softmax_sink_pv(s[u] * K_SCALE + bias, sink_rows, w_v))
        for u, b in enumerate(bs):
            for h in range(N_SWA_HEADS):
                oblk_ref[h, rows[u], :] = o[u][h * dec_seq:(h + 1) * dec_seq]
            kt_out_ref[b] = jnp.where(keep_old, pltpu.roll(k_old[u], LANES - dec_seq, axis=1), knew_ref[b])
            vt_out_ref[b] = jnp.where(keep_old, pltpu.roll(v_old[u], LANES - dec_seq, axis=1), vnew_ref[b])
        return carry

    lax.fori_loop(0, DEC_GROUP // DEC_UNROLL, per_batches, 0)

    y1 = jnp.where(first, oblk_ref[3], oblk_ref[4])
    moved = pltpu.roll(jnp.concatenate([oblk_ref[1], y1, oblk_ref[6], oblk_ref[6]], axis=1), HEAD_DIM, axis=1)
    outs = [
        jnp.where(first, oblk_ref[0], moved[:, 0:LANES]),
        jnp.where(first, oblk_ref[2], moved[:, LANES:2 * LANES]),
        jnp.where(first, moved[:, 2 * LANES:3 * LANES], oblk_ref[5]),
        jnp.where(first, moved[:, 3 * LANES:4 * LANES], oblk_ref[7]),
    ]
    for c in range(SWA_WIDTH // LANES):
        mix_ref[:, c * LANES:(c + 1) * LANES] = outs[c].astype(BF16)


def _decode_attn(sinks, qn, kn, vs, k_t, v_t, dec_seq):
    nb = k_t.shape[0]
    assert DEC_GROUP * dec_seq == DEC_ROWS and nb % DEC_GROUP == 0 and dec_seq & (dec_seq - 1) == 0
    assert k_t.shape[1:] == (SWA_KV_WIDTH, WINDOW)
    row = lambda w: pl.BlockSpec((DEC_ROWS, w), lambda i: (i, 0))
    cache = pl.BlockSpec((DEC_GROUP, SWA_KV_WIDTH, WINDOW), lambda i: (i, 0, 0))
    return pl.pallas_call(
        functools.partial(_decode_attn_kernel, dec_seq),
        grid=(nb // DEC_GROUP,),
        in_specs=[pl.BlockSpec(memory_space=pltpu.SMEM),
                  row(SWA_WIDTH), row(SWA_KV_WIDTH), row(SWA_KV_WIDTH), cache, cache],
        out_specs=[row(SWA_WIDTH), cache, cache],
        out_shape=[jax.ShapeDtypeStruct((nb * dec_seq, SWA_WIDTH), BF16),
                   jax.ShapeDtypeStruct(k_t.shape, F32),
                   jax.ShapeDtypeStruct(v_t.shape, F32)],
        scratch_shapes=[
            pltpu.VMEM((2, N_SWA_HEADS * dec_seq, WINDOW), F32),
            pltpu.VMEM((N_SWA_HEADS, DEC_ROWS, LANES), F32),
            pltpu.VMEM((N_SWA_HEADS, DEC_ROWS, LANES), F32),
            pltpu.VMEM((DEC_GROUP, SWA_KV_WIDTH, LANES), F32),
            pltpu.VMEM((DEC_GROUP, SWA_KV_WIDTH, LANES), F32),
            pltpu.VMEM((SWA_KV_WIDTH, DEC_ROWS), BF16),
            pltpu.VMEM((SWA_KV_WIDTH, DEC_ROWS), BF16),
        ],
        compiler_params=pltpu.CompilerParams(
            dimension_semantics=("arbitrary",), vmem_limit_bytes=VMEM_LIMIT),
        name="decode_attn",
    )(sinks, qn, kn, vs, k_t, v_t)


def kernel(x_prompt, x_sample, state_ret, cache_swa_k, cache_swa_v, norm_mix_gain, w_in, q_norm_gain,
           k_norm_gain, attn_sinks, w_out, norm_ffn_gain, w_up, w_down):
    batch, seq, d = x_prompt.shape
    nb, dec_seq, _ = x_sample.shape
    wb = cache_swa_k.shape[1]
    assert d == D_MODEL and seq % PROMPT_TILE == 0 and wb == WINDOW

    w_in_bf = w_in.astype(BF16)
    w_out_bf = w_out.astype(BF16)
    w_up_bf = w_up.astype(BF16)
    w_down_bf = w_down.astype(BF16)
    gain_mix = norm_mix_gain.reshape(1, D_MODEL)
    gain_ffn = norm_ffn_gain.reshape(1, D_MODEL)
    qg = jnp.tile(q_norm_gain, 2).reshape(1, LANES)
    kg = jnp.tile(k_norm_gain, 2).reshape(1, LANES)

    xp = x_prompt.reshape(batch * seq, D_MODEL)
    y_p, ret_p, kwin_t, vwin_t = _prompt_layer(attn_sinks, xp, gain_mix, w_in_bf, qg, kg, w_out_bf, gain_ffn,
                                               w_up_bf, w_down_bf, batch, seq)
    y_p = y_p.reshape(batch, seq, D_MODEL)

    def from_key_minor(a_t):
        return jnp.transpose(a_t.reshape(a_t.shape[0], N_SWA_KV, HEAD_DIM, WINDOW), (0, 3, 1, 2))

    def to_key_minor(a):
        return jnp.transpose(a, (0, 2, 3, 1)).reshape(a.shape[0], SWA_KV_WIDTH, WINDOW)

    xs = x_sample.reshape(nb * dec_seq, D_MODEL)
    main_s, qn_s, kn_s, vs_s = _in_proj(xs, gain_mix, w_in_bf, qg, kg)
    mix_ret_s, state_t = _decode_ret(main_s, jnp.transpose(state_ret, (1, 2, 3, 0)), dec_seq)
    mix_swa_s, k_t, v_t = _decode_attn(attn_sinks, qn_s, kn_s, vs_s,
                                       to_key_minor(cache_swa_k), to_key_minor(cache_swa_v), dec_seq)
    y_s = _out_mlp(mix_ret_s, 0, mix_swa_s, 0, xs, w_out_bf, gain_ffn, w_up_bf, w_down_bf)

    return (y_p, y_s.reshape(nb, dec_seq, D_MODEL), ret_p, from_key_minor(kwin_t), from_key_minor(vwin_t),
            jnp.transpose(state_t, (3, 0, 1, 2)), from_key_minor(k_t), from_key_minor(v_t))
```

```python
import functools
import math

import jax
import jax.numpy as jnp
from jax import lax
from jax.experimental import pallas as pl
from jax.experimental.pallas import tpu as pltpu

F32 = jnp.float32
BF16 = jnp.bfloat16

D_MODEL = 1024
HEAD_DIM = 64
N_RET_HEADS = 8
N_SWA_HEADS = 8
N_SWA_KV = 2
SWA_GROUP = N_SWA_HEADS // N_SWA_KV
RET_WIDTH = N_RET_HEADS * HEAD_DIM
SWA_WIDTH = N_SWA_HEADS * HEAD_DIM
SWA_KV_WIDTH = N_SWA_KV * HEAD_DIM
MAIN_WIDTH = 4 * RET_WIDTH
IN_WIDTH = MAIN_WIDTH + SWA_WIDTH + 2 * SWA_KV_WIDTH
MIX_WIDTH = RET_WIDTH + SWA_WIDTH
D_FF = 4 * D_MODEL
WINDOW = 128
RET_CHUNK = 128
EPS = 1e-6
NEG_INF = -1e30

LANES = 128
N_PAIRS = N_RET_HEADS // 2
LOG_DECAY = [math.log(1.0 - 2.0 ** (-5.0 - h)) for h in range(N_RET_HEADS)]
ALIBI_SLOPES = [2.0 ** (-8.0 * (h + 1) / N_SWA_HEADS) for h in range(N_SWA_HEADS)]
K_SCALE = HEAD_DIM ** -0.5

PROMPT_TILE = 512
DEC_GROUP = 16
VMEM_LIMIT = 56 * 1024 * 1024
PROMPT_VMEM_LIMIT = 62 * 1024 * 1024


def _dot(a, b):
    return jnp.dot(a, b, preferred_element_type=F32)


def _dot_nt(a, b):
    return lax.dot_general(a, b, (((1,), (1,)), ((), ())), preferred_element_type=F32)


def _iota(shape, dim):
    return lax.broadcasted_iota(jnp.int32, shape, dim)


def _ones_block_diag():
    same = ((_iota((2 * LANES, LANES), 0) >> 6) & 1) == (_iota((2 * LANES, LANES), 1) >> 6)
    return jnp.where(same, 1.0, 0.0).astype(BF16)


def _head_sumsq(x, ones_bd):
    x2 = x * x
    hi = x2.astype(BF16)
    lo = (x2 - hi.astype(F32)).astype(BF16)
    return _dot(jnp.concatenate([hi, lo], axis=1), ones_bd)


def _head_rms_scale(x, ones_bd):
    return lax.rsqrt(_head_sumsq(x, ones_bd) * (1.0 / HEAD_DIM) + EPS)


def _first_half():
    return _iota((1, LANES), 1) < HEAD_DIM


def _pair_const(values, pair, lane_is_second):
    return jnp.where(lane_is_second, values[2 * pair + 1], values[2 * pair]).astype(F32)


def _silu(g):
    return g * (1.0 / (1.0 + jnp.exp(-g)))


def _in_proj_kernel(x_ref, gain_ref, w_ref, qg_ref, kg_ref, main_ref, qn_ref, kn_ref, vs_ref):
    x = x_ref[...]
    ms = jnp.mean(x * x, axis=-1, keepdims=True)
    hb = ((x * lax.rsqrt(ms + EPS)) * gain_ref[...]).astype(BF16)
    ones_bd = _ones_block_diag()
    for c in range(MAIN_WIDTH // 512):
        main_ref[:, c * 512:(c + 1) * 512] = _dot(hb, w_ref[:, c * 512:(c + 1) * 512])
    qs = _dot(hb, w_ref[:, MAIN_WIDTH:MAIN_WIDTH + SWA_WIDTH])
    for c in range(SWA_WIDTH // LANES):
        xc = qs[:, c * LANES:(c + 1) * LANES]
        qn_ref[:, c * LANES:(c + 1) * LANES] = (xc * _head_rms_scale(xc, ones_bd)) * qg_ref[...]
    k0 = MAIN_WIDTH + SWA_WIDTH
    ks = _dot(hb, w_ref[:, k0:k0 + SWA_KV_WIDTH])
    kn_ref[...] = (ks * _head_rms_scale(ks, ones_bd)) * kg_ref[...]
    vs_ref[...] = _dot(hb, w_ref[:, k0 + SWA_KV_WIDTH:k0 + 2 * SWA_KV_WIDTH])


DEC_COL_BLOCK = 256
assert MAIN_WIDTH % DEC_COL_BLOCK == 0 and SWA_WIDTH % DEC_COL_BLOCK == 0 and 2 * SWA_KV_WIDTH == DEC_COL_BLOCK


def _decode_in_proj_kernel(x_ref, gain_ref, w_ref, qg_ref, kg_ref, main_ref, qn_ref, kn_ref, vs_ref, hb_ref):
    s = pl.program_id(0)
    n_main = MAIN_WIDTH // DEC_COL_BLOCK
    n_q = SWA_WIDTH // DEC_COL_BLOCK

    @pl.when(s == 0)
    def _():
        x = x_ref[...]
        ms = jnp.mean(x * x, axis=-1, keepdims=True)
        hb_ref[...] = ((x * lax.rsqrt(ms + EPS)) * gain_ref[...]).astype(BF16)

    proj = _dot(hb_ref[...], w_ref[...].astype(BF16))

    @pl.when(s < n_main)
    def _():
        main_ref[...] = proj

    @pl.when((s >= n_main) & (s < n_main + n_q))
    def _():
        ones_bd = _ones_block_diag()
        for c in range(DEC_COL_BLOCK // LANES):
            xc = proj[:, c * LANES:(c + 1) * LANES]
            qn_ref[:, c * LANES:(c + 1) * LANES] = (xc * _head_rms_scale(xc, ones_bd)) * qg_ref[...]

    @pl.when(s == n_main + n_q)
    def _():
        ks = proj[:, :SWA_KV_WIDTH]
        kn_ref[...] = (ks * _head_rms_scale(ks, _ones_block_diag())) * kg_ref[...]
        vs_ref[...] = proj[:, SWA_KV_WIDTH:]


def _decode_in_proj(x2d, gain, w_in, qg, kg):
    m = x2d.shape[0]
    n_main = MAIN_WIDTH // DEC_COL_BLOCK
    n_q = SWA_WIDTH // DEC_COL_BLOCK
    once = lambda a: pl.BlockSpec(a.shape, lambda s: (0, 0), pipeline_mode=pl.Buffered(1))
    return pl.pallas_call(
        _decode_in_proj_kernel,
        grid=(IN_WIDTH // DEC_COL_BLOCK,),
        in_specs=[once(x2d), once(gain), pl.BlockSpec((D_MODEL, DEC_COL_BLOCK), lambda s: (0, s)),
                  once(qg), once(kg)],
        out_specs=[pl.BlockSpec((m, DEC_COL_BLOCK), lambda s: (0, jnp.minimum(s, n_main - 1))),
                   pl.BlockSpec((m, DEC_COL_BLOCK), lambda s: (0, jnp.clip(s - n_main, 0, n_q - 1))),
                   pl.BlockSpec((m, SWA_KV_WIDTH), lambda s: (0, 0)),
                   pl.BlockSpec((m, SWA_KV_WIDTH), lambda s: (0, 0))],
        out_shape=[jax.ShapeDtypeStruct((m, MAIN_WIDTH), F32),
                   jax.ShapeDtypeStruct((m, SWA_WIDTH), F32),
                   jax.ShapeDtypeStruct((m, SWA_KV_WIDTH), F32),
                   jax.ShapeDtypeStruct((m, SWA_KV_WIDTH), F32)],
        scratch_shapes=[pltpu.VMEM((m, D_MODEL), BF16)],
        compiler_params=pltpu.CompilerParams(
            dimension_semantics=("arbitrary",), vmem_limit_bytes=VMEM_LIMIT),
        name="decode_in_proj",
    )(x2d, gain, w_in, qg, kg)


FF_CHUNK = 1024


def _out_mlp_kernel(mix_ret_ref, mix_swa_ref, x_ref, w_out_ref, gain_ref, w_up_ref, w_down_ref, y_ref):
    h = x_ref[...] + (_dot(mix_ret_ref[...].astype(BF16), w_out_ref[:RET_WIDTH, :])
                      + _dot(mix_swa_ref[...].astype(BF16), w_out_ref[RET_WIDTH:, :]))
    ms = jnp.mean(h * h, axis=-1, keepdims=True)
    hf = ((h * lax.rsqrt(ms + EPS)) * gain_ref[...]).astype(BF16)
    ff = None
    for c in range(D_FF // FF_CHUNK):
        u = _dot(hf, w_up_ref[:, c * FF_CHUNK:(c + 1) * FF_CHUNK])
        a = jnp.maximum(u, 0.0)
        d = _dot((a * a).astype(BF16), w_down_ref[c * FF_CHUNK:(c + 1) * FF_CHUNK, :])
        ff = d if ff is None else ff + d
    y_ref[...] = h + ff


DEC_FF_CHUNK = 512


def _decode_out_mlp_kernel(mix_ret_ref, mix_swa_ref, x_ref, w_out_ref, gain_ref, w_up_ref, w_down_ref,
                           y_ref, hf_ref):
    s = pl.program_id(0)

    @pl.when(s == 0)
    def _():
        y_ref[...] = x_ref[...] + _dot(mix_ret_ref[...].astype(BF16), w_out_ref[...].astype(BF16))

    @pl.when(s == 1)
    def _():
        h = y_ref[...] + _dot(mix_swa_ref[...].astype(BF16), w_out_ref[...].astype(BF16))
        ms = jnp.mean(h * h, axis=-1, keepdims=True)
        hf_ref[...] = ((h * lax.rsqrt(ms + EPS)) * gain_ref[...]).astype(BF16)
        y_ref[...] = h

    @pl.when(s >= 2)
    def _():
        a = jnp.maximum(_dot(hf_ref[...], w_up_ref[...].astype(BF16)), 0.0)
        y_ref[...] += _dot((a * a).astype(BF16), w_down_ref[...].astype(BF16))


def _decode_out_mlp(mix_ret, mix_swa, x2d, w_out, gain, w_up, w_down):
    m = x2d.shape[0]
    once = lambda a: pl.BlockSpec(a.shape, lambda s: (0, 0), pipeline_mode=pl.Buffered(1))
    ff = lambda s: jnp.maximum(s - 2, 0)
    return pl.pallas_call(
        _decode_out_mlp_kernel,
        grid=(2 + D_FF // DEC_FF_CHUNK,),
        in_specs=[once(mix_ret), once(mix_swa), once(x2d),
                  pl.BlockSpec((RET_WIDTH, D_MODEL), lambda s: (jnp.minimum(s, 1), 0)),
                  once(gain),
                  pl.BlockSpec((D_MODEL, DEC_FF_CHUNK), lambda s: (0, ff(s))),
                  pl.BlockSpec((DEC_FF_CHUNK, D_MODEL), lambda s: (ff(s), 0))],
        out_specs=pl.BlockSpec((m, D_MODEL), lambda s: (0, 0)),
        out_shape=jax.ShapeDtypeStruct((m, D_MODEL), F32),
        scratch_shapes=[pltpu.VMEM((m, D_MODEL), BF16)],
        compiler_params=pltpu.CompilerParams(
            dimension_semantics=("arbitrary",), vmem_limit_bytes=VMEM_LIMIT),
        name="decode_out_mlp",
    )(mix_ret, mix_swa, x2d, w_out, gain, w_up, w_down)


def _split_pair_rows(x, first):
    return jnp.concatenate([jnp.where(first, x, 0.0), jnp.where(first, 0.0, x)], axis=0).astype(BF16)


def _softmax_sink_pv(s, sink_wide, v_t):
    m = jnp.maximum(jnp.max(s, axis=-1, keepdims=True), sink_wide)
    p = jnp.exp(s - jnp.concatenate([m, m], axis=1))
    denom = jnp.sum(p, axis=-1, keepdims=True) + jnp.exp(sink_wide - m)
    return _dot_nt(p.astype(BF16), v_t) / denom


def _softmax_sink_pv_t(s_t, sink_lanes, v_t):
    m = jnp.maximum(jnp.max(s_t, axis=0, keepdims=True), sink_lanes)
    p = jnp.exp(s_t - m)
    denom = jnp.sum(p, axis=0, keepdims=True) + jnp.exp(sink_lanes - m)
    return _dot(v_t, p.astype(BF16)) / denom


def _prompt_consts(intra_ref, qdec_ref, kdec_ref, sdec_ref, bias_ref):
    r = _iota((LANES, LANES), 0)
    lane2 = _iota((LANES, LANES), 1) >= HEAD_DIM
    rf = r.astype(F32)
    ri = _iota((LANES, 2 * LANES), 0)
    ci = _iota((LANES, 2 * LANES), 1)
    diff = (ri - (ci & (LANES - 1))).astype(F32)
    for p in range(N_PAIRS):
        lg = _pair_const(LOG_DECAY, p, lane2)
        qdec_ref[p] = jnp.exp(lg * (rf + 1.0))
        kdec_ref[p] = jnp.exp(lg * (RET_CHUNK - 1.0 - rf))
        sdec_ref[p] = jnp.exp(_pair_const(LOG_DECAY, p, r >= HEAD_DIM) * float(RET_CHUNK))
        lg2 = _pair_const(LOG_DECAY, p, ci >= LANES)
        intra_ref[p] = jnp.where(diff >= 0.0, jnp.exp(lg2 * jnp.maximum(diff, 0.0)), 0.0)
    cols = SWA_GROUP * WINDOW
    kb = _iota((2 * WINDOW, cols), 0)
    cb = _iota((2 * WINDOW, cols), 1)
    grp = cb >> 7
    dist = WINDOW + (cb & (WINDOW - 1)) - kb
    valid = (dist >= 0) & (dist < WINDOW)
    distf = dist.astype(F32)
    for j in range(N_SWA_KV):
        sl = [ALIBI_SLOPES[SWA_GROUP * j + g] for g in range(SWA_GROUP)]
        slope = jnp.where(grp == 0, sl[0], jnp.where(grp == 1, sl[1], jnp.where(grp == 2, sl[2], sl[3])))
        b = jnp.where(valid, -(slope.astype(F32) * distf), NEG_INF)
        bias_ref[0, j] = b
        bias_ref[1, j] = jnp.where(kb >= WINDOW, b, NEG_INF)


def _prompt_layer_kernel(sinks_ref, x_ref, x_next_ref, gain_mix_ref, w_in_ref, qg_ref, kg_ref,
                         w_out_ref, gain_ffn_ref, w_up_ref, w_down_ref,
                         y_ref, ret_ref, kwin_ref, vwin_ref,
                         main_ref, qn_ref, kn_ref, vs_ref, mix_ref,
                         state_ref, prevk_ref, prevv_ref,
                         intra_ref, qdec_ref, kdec_ref, sdec_ref, bias_ref):
    t = pl.program_id(1)
    step = pl.program_id(0) * pl.num_programs(1) + t
    cur = step % 2
    nxt = 1 - cur

    @pl.when(step == 0)
    def _():
        _prompt_consts(intra_ref, qdec_ref, kdec_ref, sdec_ref, bias_ref)
        _in_proj_kernel(x_ref, gain_mix_ref, w_in_ref, qg_ref, kg_ref,
                        main_ref.at[0], qn_ref.at[0], kn_ref.at[0], vs_ref.at[0])

    @pl.when(t == 0)
    def _():
        state_ref[...] = jnp.zeros_like(state_ref)
        prevk_ref[...] = jnp.zeros_like(prevk_ref)
        prevv_ref[...] = jnp.zeros_like(prevv_ref)

    first = _first_half()
    ones_bd = _ones_block_diag()
    bd_mask = (_iota((LANES, LANES), 0) >= HEAD_DIM) == (_iota((LANES, LANES), 1) >= HEAD_DIM)

    def chunk(c, carry):
        r0 = pl.multiple_of(c * RET_CHUNK, RET_CHUNK)
        rows = pl.ds(r0, RET_CHUNK)
        pairs = range(N_PAIRS)
        kvs = range(N_SWA_KV)
        cols = lambda base, p: slice(base + p * LANES, base + (p + 1) * LANES)
        xn = x_next_ref[rows, :]
        hb = ((xn * lax.rsqrt(jnp.mean(xn * xn, axis=-1, keepdims=True) + EPS)) * gain_mix_ref[...]).astype(BF16)

        q = [main_ref[cur, rows, cols(0, p)] for p in pairs]
        k = [main_ref[cur, rows, cols(RET_WIDTH, p)] * K_SCALE for p in pairs]
        v = [main_ref[cur, rows, cols(2 * RET_WIDTH, p)] for p in pairs]
        state = [state_ref[p] for p in pairs]
        kc = kn_ref[cur, rows, :]
        k_sw = pltpu.roll(kc, HEAD_DIM, axis=1)
        v_t = vs_ref[cur, rows, :].T
        is_first = ((t == 0) & (c == 0)).astype(jnp.int32)
        k_dup = [(jnp.where(first, kc, k_sw) if j == 0 else jnp.where(first, k_sw, kc)).astype(BF16)
                 for j in kvs]
        v_tj = [v_t[j * HEAD_DIM:(j + 1) * HEAD_DIM].astype(BF16) for j in kvs]
        q_st = []
        for j in kvs:
            pieces = []
            for g in range(SWA_GROUP):
                qc = qn_ref[cur, rows, cols(0, 2 * j + g // 2)]
                pieces.append(jnp.where(first, qc, 0.0) if g % 2 == 0 else jnp.where(first, 0.0, qc))
            q_st.append(jnp.concatenate(pieces, axis=0).astype(BF16))

        s = [_dot_nt(q[p].astype(BF16), _split_pair_rows(k[p], first)) for p in pairs]
        s_t = [_dot_nt(jnp.concatenate([prevk_ref[j], k_dup[j]], axis=0), q_st[j]) for j in kvs]
        cross = [_dot((q[p] * qdec_ref[p]).astype(BF16), state[p].astype(BF16)) for p in pairs]
        upd = [_dot((k[p] * kdec_ref[p]).T.astype(BF16), v[p].astype(BF16)) for p in pairs]
        for cb in range(MAIN_WIDTH // 512):
            main_ref[nxt, rows, cb * 512:(cb + 1) * 512] = _dot(hb, w_in_ref[:, cb * 512:(cb + 1) * 512])

        o = [_dot((s[p] * intra_ref[p]).astype(BF16), _split_pair_rows(v[p], first)) + cross[p] for p in pairs]
        o_t = []
        for j in kvs:
            sink_lanes = jnp.concatenate(
                [jnp.full((1, WINDOW), sinks_ref[SWA_GROUP * j + g], F32) for g in range(SWA_GROUP)],
                axis=1)
            v_cat = jnp.concatenate([prevv_ref[j], v_tj[j]], axis=1)
            o_t.append(_softmax_sink_pv_t(s_t[j] * K_SCALE + bias_ref[is_first, j], sink_lanes, v_cat))
        for p in pairs:
            state_ref[p] = state[p] * sdec_ref[p] + jnp.where(bd_mask, upd[p], 0.0)
        for j in kvs:
            prevk_ref[j] = k_dup[j]
            prevv_ref[j] = v_tj[j]
        a0 = MAIN_WIDTH
        qs = _dot(hb, w_in_ref[:, a0:a0 + SWA_WIDTH])
        ks = _dot(hb, w_in_ref[:, a0 + SWA_WIDTH:a0 + SWA_WIDTH + SWA_KV_WIDTH])
        vs_ref[nxt, rows, :] = _dot(hb, w_in_ref[:, a0 + SWA_WIDTH + SWA_KV_WIDTH:a0 + SWA_WIDTH + 2 * SWA_KV_WIDTH])

        scale = [_head_rms_scale(o[p], ones_bd) for p in pairs]
        q_scale = [_head_rms_scale(qs[:, cols(0, cq)], ones_bd) for cq in range(SWA_WIDTH // LANES)]
        k_scale = _head_rms_scale(ks, ones_bd)
        for p in pairs:
            g = main_ref[cur, rows, cols(3 * RET_WIDTH, p)]
            mix_ref[rows, cols(0, p)] = (o[p] * scale[p] * _silu(g)).astype(BF16)
        for cq in range(SWA_WIDTH // LANES):
            qn_ref[nxt, rows, cols(0, cq)] = (qs[:, cols(0, cq)] * q_scale[cq]) * qg_ref[...]
        kn_ref[nxt, rows, :] = (ks * k_scale) * kg_ref[...]
        for j in kvs:
            for half in range(2):
                pair_t = jnp.concatenate([o_t[j][:, (2 * half) * WINDOW:(2 * half + 1) * WINDOW],
                                          o_t[j][:, (2 * half + 1) * WINDOW:(2 * half + 2) * WINDOW]], axis=0)
                mix_ref[rows, cols(RET_WIDTH, 2 * j + half)] = pair_t.T.astype(BF16)
        return carry

    lax.fori_loop(0, PROMPT_TILE // RET_CHUNK, chunk, 0)

    @pl.when(t == pl.num_programs(1) - 1)
    def _():
        for p in range(N_PAIRS):
            s = state_ref[p]
            ret_ref[2 * p] = s[:HEAD_DIM, :HEAD_DIM]
            ret_ref[2 * p + 1] = s[HEAD_DIM:, HEAD_DIM:]
        last = slice(PROMPT_TILE - WINDOW, PROMPT_TILE)
        kwin_ref[...] = kn_ref[cur, last, :].T
        vwin_ref[...] = vs_ref[cur, last, :].T

    _out_mlp_kernel(mix_ref.at[:, pl.ds(0, RET_WIDTH)], mix_ref.at[:, pl.ds(RET_WIDTH, SWA_WIDTH)], x_ref,
                    w_out_ref, gain_ffn_ref, w_up_ref, w_down_ref, y_ref)


def _prompt_layer(sinks, x2d, gain_mix, w_in_bf, qg, kg, w_out_bf, gain_ffn, w_up_bf, w_down_bf, batch, seq):
    nt = seq // PROMPT_TILE
    last_tile = batch * nt - 1
    row = lambda w: pl.BlockSpec((PROMPT_TILE, w), lambda b, t: (b * nt + t, 0))
    next_row = pl.BlockSpec((PROMPT_TILE, D_MODEL), lambda b, t: (jnp.minimum(b * nt + t + 1, last_tile), 0))
    full = lambda a: pl.BlockSpec(a.shape, lambda b, t: (0, 0), pipeline_mode=pl.Buffered(1))
    return pl.pallas_call(
        _prompt_layer_kernel,
        grid=(batch, nt),
        in_specs=[pl.BlockSpec(memory_space=pltpu.SMEM), row(D_MODEL), next_row,
                  full(gain_mix), full(w_in_bf), full(qg), full(kg),
                  full(w_out_bf), full(gain_ffn), full(w_up_bf), full(w_down_bf)],
        out_specs=[row(D_MODEL),
                   pl.BlockSpec((None, N_RET_HEADS, HEAD_DIM, HEAD_DIM), lambda b, t: (b, 0, 0, 0)),
                   pl.BlockSpec((None, SWA_KV_WIDTH, WINDOW), lambda b, t: (b, 0, 0)),
                   pl.BlockSpec((None, SWA_KV_WIDTH, WINDOW), lambda b, t: (b, 0, 0))],
        out_shape=[jax.ShapeDtypeStruct((batch * seq, D_MODEL), F32),
                   jax.ShapeDtypeStruct((batch, N_RET_HEADS, HEAD_DIM, HEAD_DIM), F32),
                   jax.ShapeDtypeStruct((batch, SWA_KV_WIDTH, WINDOW), F32),
                   jax.ShapeDtypeStruct((batch, SWA_KV_WIDTH, WINDOW), F32)],
        scratch_shapes=[
            pltpu.VMEM((2, PROMPT_TILE, MAIN_WIDTH), F32),
            pltpu.VMEM((2, PROMPT_TILE, SWA_WIDTH), F32),
            pltpu.VMEM((2, PROMPT_TILE, SWA_KV_WIDTH), F32),
            pltpu.VMEM((2, PROMPT_TILE, SWA_KV_WIDTH), F32),
            pltpu.VMEM((PROMPT_TILE, MIX_WIDTH), BF16),
            pltpu.VMEM((N_PAIRS, LANES, LANES), F32),
            pltpu.VMEM((N_SWA_KV, WINDOW, LANES), BF16),
            pltpu.VMEM((N_SWA_KV, HEAD_DIM, WINDOW), BF16),
            pltpu.VMEM((N_PAIRS, LANES, 2 * LANES), F32),
            pltpu.VMEM((N_PAIRS, LANES, LANES), F32),
            pltpu.VMEM((N_PAIRS, LANES, LANES), F32),
            pltpu.VMEM((N_PAIRS, LANES, LANES), F32),
            pltpu.VMEM((2, N_SWA_KV, 2 * WINDOW, SWA_GROUP * WINDOW), F32),
        ],
        compiler_params=pltpu.CompilerParams(
            dimension_semantics=("arbitrary", "arbitrary"), vmem_limit_bytes=PROMPT_VMEM_LIMIT),
        name="prompt_layer",
    )(sinks, x2d, x2d, gain_mix, w_in_bf, qg, kg, w_out_bf, gain_ffn, w_up_bf, w_down_bf)


def _decode_ret_kernel(dec_seq, nb, qdec_ref, kdec_ref, sdec_ref, intra_ref,
                       q_ref, k_ref, v_ref, g_ref, st_ref,
                       mix_ref, st_out_ref,
                       qt_ref, kt_ref, vt_ref, qdt_ref, kdt_ref, o_ref):
    pair = pl.program_id(0)
    halves = [slice(0, HEAD_DIM), slice(HEAD_DIM, 2 * HEAD_DIM)]
    for l in range(dec_seq):
        rows = pl.ds(l, nb, stride=dec_seq)
        q_t = q_ref[rows, :].T
        k_t = (k_ref[rows, :] * K_SCALE).T
        qt_ref[l] = q_t
        kt_ref[l] = k_t
        vt_ref[l] = v_ref[rows, :].T
        for hh in range(2):
            qdt_ref[l, halves[hh], :] = q_t[halves[hh]] * qdec_ref[2 * pair + hh, l]
            kdt_ref[l, halves[hh], :] = k_t[halves[hh]] * kdec_ref[2 * pair + hh, l]

    e_blk = HEAD_DIM // 2
    for hh in range(2):
        h = 2 * pair + hh
        hs = halves[hh]
        for l in range(dec_seq):
            acc = None
            for m in range(l + 1):
                sc = jnp.sum(qt_ref[l, hs, :] * kt_ref[m, hs, :], axis=0, keepdims=True) * intra_ref[h, l - m]
                term = sc * vt_ref[m, hs, :]
                acc = term if acc is None else acc + term
            o_ref[l, hs, :] = acc
        for eb in range(HEAD_DIM // e_blk):
            es = slice(eb * e_blk, (eb + 1) * e_blk)
            erows = slice(hh * HEAD_DIM + eb * e_blk, hh * HEAD_DIM + (eb + 1) * e_blk)

            def body(d, accs, hh=hh, h=h, es=es, erows=erows):
                s_d = st_ref[hh, d, es, :]
                row = pl.ds(hh * HEAD_DIM + d, 1)
                upd = s_d * sdec_ref[h]
                new_accs = []
                for l in range(dec_seq):
                    new_accs.append(accs[l] + qdt_ref[l, row, :] * s_d)
                    upd = upd + kdt_ref[l, row, :] * vt_ref[l, erows, :]
                st_out_ref[hh, d, es, :] = upd
                return tuple(new_accs)

            zero = jnp.zeros((e_blk, nb), F32)
            accs = lax.fori_loop(0, HEAD_DIM, body, tuple(zero for _ in range(dec_seq)), unroll=2)
            for l in range(dec_seq):
                o_ref[l, erows, :] = o_ref[l, erows, :] + accs[l]

    for l in range(dec_seq):
        o = o_ref[l]
        normed = []
        for hh in range(2):
            oh = o[halves[hh]]
            normed.append(oh * lax.rsqrt(jnp.mean(oh * oh, axis=0, keepdims=True) + EPS))
        rows = pl.ds(l, nb, stride=dec_seq)
        mix_ref[rows, :] = jnp.concatenate(normed, axis=0).T * _silu(g_ref[rows, :])


def _decode_ret(main, state_t, dec_seq):
    nb = state_t.shape[-1]
    m = main.shape[0]
    assert nb == LANES and m == nb * dec_seq
    steps = [j for j in range(dec_seq)]
    tab = lambda f: jnp.asarray([[f(h, j) for j in steps] for h in range(N_RET_HEADS)], F32)
    qdec = tab(lambda h, j: math.exp(LOG_DECAY[h] * (j + 1.0)))
    kdec = tab(lambda h, j: math.exp(LOG_DECAY[h] * (dec_seq - 1.0 - j)))
    intra = tab(lambda h, j: math.exp(LOG_DECAY[h] * j))
    sdec = jnp.asarray([math.exp(LOG_DECAY[h] * dec_seq) for h in range(N_RET_HEADS)], F32)
    smem = pl.BlockSpec(memory_space=pltpu.SMEM)
    col = lambda base: pl.BlockSpec((m, LANES), lambda p: (0, base + p))
    st_spec = pl.BlockSpec((2, HEAD_DIM, HEAD_DIM, nb), lambda p: (p, 0, 0, 0))
    stage = pltpu.VMEM((dec_seq, LANES, nb), F32)
    return pl.pallas_call(
        functools.partial(_decode_ret_kernel, dec_seq, nb),
        grid=(N_PAIRS,),
        in_specs=[smem, smem, smem, smem,
                  col(0), col(N_PAIRS), col(2 * N_PAIRS), col(3 * N_PAIRS), st_spec],
        out_specs=[pl.BlockSpec((m, LANES), lambda p: (0, p)), st_spec],
        out_shape=[jax.ShapeDtypeStruct((m, RET_WIDTH), F32),
                   jax.ShapeDtypeStruct(state_t.shape, F32)],
        scratch_shapes=[stage, stage, stage, stage, stage, stage],
        compiler_params=pltpu.CompilerParams(
            dimension_semantics=("arbitrary",), vmem_limit_bytes=VMEM_LIMIT),
        name="decode_ret",
    )(qdec, kdec, sdec, intra, main, main, main, main, state_t)


DEC_ROWS = 128
DEC_UNROLL = 4


def _decode_attn_consts(dec_seq, bias_ref):
    shift = dec_seq.bit_length() - 1
    rows = N_SWA_HEADS * dec_seq
    rb = _iota((rows, WINDOW), 0)
    cb = _iota((rows, WINDOW), 1)
    head = rb >> shift
    i = rb & (dec_seq - 1)
    slope = jnp.zeros((rows, WINDOW), F32)
    for h in range(N_SWA_HEADS):
        slope = jnp.where(head == h, ALIBI_SLOPES[h], slope)
    bias_ref[0] = jnp.where(cb > i, -(slope * (WINDOW + i - cb).astype(F32)), NEG_INF)
    m = cb & (dec_seq - 1)
    bias_ref[1] = jnp.where(m <= i, -(slope * (i - m).astype(F32)), NEG_INF)


def _decode_attn_kernel(dec_seq, sinks_ref, qn_ref, kn_ref, vs_ref, kt_ref, vt_ref,
                        mix_ref, kt_out_ref, vt_out_ref,
                        bias_ref, qbd_ref, oblk_ref, knew_ref, vnew_ref, knt_ref, vst_ref):
    @pl.when(pl.program_id(0) == 0)
    def _():
        _decode_attn_consts(dec_seq, bias_ref)

    first = _first_half()
    shift = dec_seq.bit_length() - 1

    kn_t = kn_ref[...].T
    vs_t = vs_ref[...].T
    knt_ref[...] = kn_t.astype(BF16)
    vst_ref[...] = vs_t.astype(BF16)
    for bb in range(DEC_GROUP):
        sh = (WINDOW - dec_seq - bb * dec_seq) % LANES
        knew_ref[bb] = pltpu.roll(kn_t, sh, axis=1) if sh else kn_t
        vnew_ref[bb] = pltpu.roll(vs_t, sh, axis=1) if sh else vs_t
    qn = qn_ref[...]
    qn_sw = pltpu.roll(qn, HEAD_DIM, axis=1)
    for h in range(N_SWA_HEADS):
        kv_half = h // SWA_GROUP
        if (h % 2) == kv_half:
            src = qn[:, (h // 2) * LANES:(h // 2 + 1) * LANES]
        else:
            col = (h + 1) // 2
            src = qn_sw[:, col * LANES:(col + 1) * LANES]
        qbd_ref[h] = jnp.where(first, src, 0.0) if kv_half == 0 else jnp.where(first, 0.0, src)

    sink_rows = jnp.concatenate(
        [jnp.full((dec_seq, LANES), sinks_ref[h], F32) for h in range(N_SWA_HEADS)], axis=0)
    col_batch = _iota((N_SWA_HEADS * dec_seq, LANES), 1) >> shift
    keep_old = _iota((1, LANES), 1) < WINDOW - dec_seq

    def per_batches(i, carry):
        bs = [i * DEC_UNROLL + u for u in range(DEC_UNROLL)]
        rows = [pl.ds(pl.multiple_of(b * dec_seq, dec_seq), dec_seq) for b in bs]
        k_old = [kt_ref[b] for b in bs]
        v_old = [vt_ref[b] for b in bs]
        q_st = [jnp.concatenate([qbd_ref[h, r, :] for h in range(N_SWA_HEADS)], axis=0).astype(BF16)
                for r in rows]
        s = [_dot(q_st[u], jnp.concatenate([k_old[u].astype(BF16), knt_ref[...]], axis=1))
             for u in range(DEC_UNROLL)]
        o = []
        for u, b in enumerate(bs):
            bias = jnp.concatenate([bias_ref[0], jnp.where(col_batch == b, bias_ref[1], NEG_INF)], axis=1)
            w_v = jnp.concatenate([v_old[u].astype(BF16), vst_ref[...]], axis=1)
            o.append(_softmax_sink_pv(s[u] * K_SCALE + bias, sink_rows, w_v))
        for u, b in enumerate(bs):
            for h in range(N_SWA_HEADS):
                oblk_ref[h, rows[u], :] = o[u][h * dec_seq:(h + 1) * dec_seq]
            kt_out_ref[b] = jnp.where(keep_old, pltpu.roll(k_old[u], LANES - dec_seq, axis=1), knew_ref[b])
            vt_out_ref[b] = jnp.where(keep_old, pltpu.roll(v_old[u], LANES - dec_seq, axis=1), vnew_ref[b])
        return carry

    lax.fori_loop(0, DEC_GROUP // DEC_UNROLL, per_batches, 0)

    y1 = jnp.where(first, oblk_ref[3], oblk_ref[4])
    moved = pltpu.roll(jnp.concatenate([oblk_ref[1], y1, oblk_ref[6], oblk_ref[6]], axis=1), HEAD_DIM, axis=1)
    outs = [
        jnp.where(first, oblk_ref[0], moved[:, 0:LANES]),
        jnp.where(first, oblk_ref[2], moved[:, LANES:2 * LANES]),
        jnp.where(first, moved[:, 2 * LANES:3 * LANES], oblk_ref[5]),
        jnp.where(first, moved[:, 3 * LANES:4 * LANES], oblk_ref[7]),
    ]
    for c in range(SWA_WIDTH // LANES):
        mix_ref[:, c * LANES:(c + 1) * LANES] = outs[c].astype(BF16)


def _decode_attn(sinks, qn, kn, vs, k_t, v_t, dec_seq):
    nb = k_t.shape[0]
    assert DEC_GROUP * dec_seq == DEC_ROWS and nb % DEC_GROUP == 0 and dec_seq & (dec_seq - 1) == 0
    assert k_t.shape[1:] == (SWA_KV_WIDTH, WINDOW)
    row = lambda w: pl.BlockSpec((DEC_ROWS, w), lambda i: (i, 0))
    cache = pl.BlockSpec((DEC_GROUP, SWA_KV_WIDTH, WINDOW), lambda i: (i, 0, 0))
    return pl.pallas_call(
        functools.partial(_decode_attn_kernel, dec_seq),
        grid=(nb // DEC_GROUP,),
        in_specs=[pl.BlockSpec(memory_space=pltpu.SMEM),
                  row(SWA_WIDTH), row(SWA_KV_WIDTH), row(SWA_KV_WIDTH), cache, cache],
        out_specs=[row(SWA_WIDTH), cache, cache],
        out_shape=[jax.ShapeDtypeStruct((nb * dec_seq, SWA_WIDTH), BF16),
                   jax.ShapeDtypeStruct(k_t.shape, F32),
                   jax.ShapeDtypeStruct(v_t.shape, F32)],
        scratch_shapes=[
            pltpu.VMEM((2, N_SWA_HEADS * dec_seq, WINDOW), F32),
            pltpu.VMEM((N_SWA_HEADS, DEC_ROWS, LANES), F32),
            pltpu.VMEM((N_SWA_HEADS, DEC_ROWS, LANES), F32),
            pltpu.VMEM((DEC_GROUP, SWA_KV_WIDTH, LANES), F32),
            pltpu.VMEM((DEC_GROUP, SWA_KV_WIDTH, LANES), F32),
            pltpu.VMEM((SWA_KV_WIDTH, DEC_ROWS), BF16),
            pltpu.VMEM((SWA_KV_WIDTH, DEC_ROWS), BF16),
        ],
        compiler_params=pltpu.CompilerParams(
            dimension_semantics=("arbitrary",), vmem_limit_bytes=VMEM_LIMIT),
        name="decode_attn",
    )(sinks, qn, kn, vs, k_t, v_t)


def kernel(x_prompt, x_sample, state_ret, cache_swa_k, cache_swa_v, norm_mix_gain, w_in, q_norm_gain,
           k_norm_gain, attn_sinks, w_out, norm_ffn_gain, w_up, w_down):
    batch, seq, d = x_prompt.shape
    nb, dec_seq, _ = x_sample.shape
    wb = cache_swa_k.shape[1]
    assert d == D_MODEL and seq % PROMPT_TILE == 0 and wb == WINDOW

    w_in_bf = w_in.astype(BF16)
    w_out_bf = w_out.astype(BF16)
    w_up_bf = w_up.astype(BF16)
    w_down_bf = w_down.astype(BF16)
    gain_mix = norm_mix_gain.reshape(1, D_MODEL)
    gain_ffn = norm_ffn_gain.reshape(1, D_MODEL)
    qg = jnp.tile(q_norm_gain, 2).reshape(1, LANES)
    kg = jnp.tile(k_norm_gain, 2).reshape(1, LANES)

    xp = x_prompt.reshape(batch * seq, D_MODEL)
    y_p, ret_p, kwin_t, vwin_t = _prompt_layer(attn_sinks, xp, gain_mix, w_in_bf, qg, kg, w_out_bf, gain_ffn,
                                               w_up_bf, w_down_bf, batch, seq)
    y_p = y_p.reshape(batch, seq, D_MODEL)

    def from_key_minor(a_t):
        return jnp.transpose(a_t.reshape(a_t.shape[0], N_SWA_KV, HEAD_DIM, WINDOW), (0, 3, 1, 2))

    def to_key_minor(a):
        return jnp.transpose(a, (0, 2, 3, 1)).reshape(a.shape[0], SWA_KV_WIDTH, WINDOW)

    xs = x_sample.reshape(nb * dec_seq, D_MODEL)
    main_s, qn_s, kn_s, vs_s = _decode_in_proj(xs, gain_mix, w_in, qg, kg)
    mix_ret_s, state_t = _decode_ret(main_s, jnp.transpose(state_ret, (1, 2, 3, 0)), dec_seq)
    mix_swa_s, k_t, v_t = _decode_attn(attn_sinks, qn_s, kn_s, vs_s,
                                       to_key_minor(cache_swa_k), to_key_minor(cache_swa_v), dec_seq)
    y_s = _decode_out_mlp(mix_ret_s, mix_swa_s, xs, w_out, gain_ffn, w_up, w_down)

    return (y_p, y_s.reshape(nb, dec_seq, D_MODEL), ret_p, from_key_minor(kwin_t), from_key_minor(vwin_t),
            jnp.transpose(state_t, (3, 0, 1, 2)), from_key_minor(k_t), from_key_minor(v_t))
```

```python
import functools
import math

import jax
import jax.numpy as jnp
from jax import lax
from jax.experimental import pallas as pl
from jax.experimental.pallas import tpu as pltpu

F32 = jnp.float32
BF16 = jnp.bfloat16

D_MODEL = 1024
HEAD_DIM = 64
N_RET_HEADS = 8
N_SWA_HEADS = 8
N_SWA_KV = 2
SWA_GROUP = N_SWA_HEADS // N_SWA_KV
RET_WIDTH = N_RET_HEADS * HEAD_DIM
SWA_WIDTH = N_SWA_HEADS * HEAD_DIM
SWA_KV_WIDTH = N_SWA_KV * HEAD_DIM
MAIN_WIDTH = 4 * RET_WIDTH
IN_WIDTH = MAIN_WIDTH + SWA_WIDTH + 2 * SWA_KV_WIDTH
MIX_WIDTH = RET_WIDTH + SWA_WIDTH
D_FF = 4 * D_MODEL
WINDOW = 128
RET_CHUNK = 128
EPS = 1e-6
NEG_INF = -1e30

LANES = 128
N_PAIRS = N_RET_HEADS // 2
LOG_DECAY = [math.log(1.0 - 2.0 ** (-5.0 - h)) for h in range(N_RET_HEADS)]
ALIBI_SLOPES = [2.0 ** (-8.0 * (h + 1) / N_SWA_HEADS) for h in range(N_SWA_HEADS)]
K_SCALE = HEAD_DIM ** -0.5

PROMPT_TILE = 512
DEC_GROUP = 16
VMEM_LIMIT = 56 * 1024 * 1024
PROMPT_VMEM_LIMIT = 62 * 1024 * 1024


def _dot(a, b):
    return jnp.dot(a, b, preferred_element_type=F32)


def _dot_nt(a, b):
    return lax.dot_general(a, b, (((1,), (1,)), ((), ())), preferred_element_type=F32)


def _iota(shape, dim):
    return lax.broadcasted_iota(jnp.int32, shape, dim)


def _ones_block_diag():
    same = ((_iota((2 * LANES, LANES), 0) >> 6) & 1) == (_iota((2 * LANES, LANES), 1) >> 6)
    return jnp.where(same, 1.0, 0.0).astype(BF16)


def _head_sumsq(x, ones_bd):
    x2 = x * x
    hi = x2.astype(BF16)
    lo = (x2 - hi.astype(F32)).astype(BF16)
    return _dot(jnp.concatenate([hi, lo], axis=1), ones_bd)


def _head_rms_scale(x, ones_bd):
    return lax.rsqrt(_head_sumsq(x, ones_bd) * (1.0 / HEAD_DIM) + EPS)


def _first_half():
    return _iota((1, LANES), 1) < HEAD_DIM


def _pair_const(values, pair, lane_is_second):
    return jnp.where(lane_is_second, values[2 * pair + 1], values[2 * pair]).astype(F32)


def _silu(g):
    return g * (1.0 / (1.0 + jnp.exp(-g)))


def _in_proj_kernel(x_ref, gain_ref, w_ref, qg_ref, kg_ref, main_ref, qn_ref, kn_ref, vs_ref):
    x = x_ref[...]
    ms = jnp.mean(x * x, axis=-1, keepdims=True)
    hb = ((x * lax.rsqrt(ms + EPS)) * gain_ref[...]).astype(BF16)
    ones_bd = _ones_block_diag()
    for c in range(MAIN_WIDTH // 512):
        main_ref[:, c * 512:(c + 1) * 512] = _dot(hb, w_ref[:, c * 512:(c + 1) * 512])
    qs = _dot(hb, w_ref[:, MAIN_WIDTH:MAIN_WIDTH + SWA_WIDTH])
    for c in range(SWA_WIDTH // LANES):
        xc = qs[:, c * LANES:(c + 1) * LANES]
        qn_ref[:, c * LANES:(c + 1) * LANES] = (xc * _head_rms_scale(xc, ones_bd)) * qg_ref[...]
    k0 = MAIN_WIDTH + SWA_WIDTH
    ks = _dot(hb, w_ref[:, k0:k0 + SWA_KV_WIDTH])
    kn_ref[...] = (ks * _head_rms_scale(ks, ones_bd)) * kg_ref[...]
    vs_ref[...] = _dot(hb, w_ref[:, k0 + SWA_KV_WIDTH:k0 + 2 * SWA_KV_WIDTH])


ROW_TILE = 512


def _in_proj(x2d, gain, w_in_bf, qg, kg):
    m = x2d.shape[0]
    tm = min(ROW_TILE, m)
    row = lambda w: pl.BlockSpec((tm, w), lambda i: (i, 0))
    full = lambda a: pl.BlockSpec(a.shape, lambda i: (0, 0), pipeline_mode=pl.Buffered(1))
    return pl.pallas_call(
        _in_proj_kernel,
        grid=(m // tm,),
        in_specs=[row(D_MODEL), full(gain), full(w_in_bf), full(qg), full(kg)],
        out_specs=[row(MAIN_WIDTH), row(SWA_WIDTH), row(SWA_KV_WIDTH), row(SWA_KV_WIDTH)],
        out_shape=[jax.ShapeDtypeStruct((m, MAIN_WIDTH), F32),
                   jax.ShapeDtypeStruct((m, SWA_WIDTH), F32),
                   jax.ShapeDtypeStruct((m, SWA_KV_WIDTH), F32),
                   jax.ShapeDtypeStruct((m, SWA_KV_WIDTH), F32)],
        compiler_params=pltpu.CompilerParams(
            dimension_semantics=("arbitrary",), vmem_limit_bytes=VMEM_LIMIT),
        name="in_proj",
    )(x2d, gain, w_in_bf, qg, kg)


FF_CHUNK = 1024


def _out_mlp_kernel(mix_ret_ref, mix_swa_ref, x_ref, w_out_ref, gain_ref, w_up_ref, w_down_ref, y_ref):
    h = x_ref[...] + (_dot(mix_ret_ref[...].astype(BF16), w_out_ref[:RET_WIDTH, :])
                      + _dot(mix_swa_ref[...].astype(BF16), w_out_ref[RET_WIDTH:, :]))
    ms = jnp.mean(h * h, axis=-1, keepdims=True)
    hf = ((h * lax.rsqrt(ms + EPS)) * gain_ref[...]).astype(BF16)
    ff = None
    for c in range(D_FF // FF_CHUNK):
        u = _dot(hf, w_up_ref[:, c * FF_CHUNK:(c + 1) * FF_CHUNK])
        a = jnp.maximum(u, 0.0)
        d = _dot((a * a).astype(BF16), w_down_ref[c * FF_CHUNK:(c + 1) * FF_CHUNK, :])
        ff = d if ff is None else ff + d
    y_ref[...] = h + ff


def _out_mlp(mix_ret, ret_block, mix_swa, swa_block, x2d, w_out_bf, gain, w_up_bf, w_down_bf):
    m = x2d.shape[0]
    tm = min(ROW_TILE, m)
    row = lambda w: pl.BlockSpec((tm, w), lambda i: (i, 0))
    col_block = lambda c: pl.BlockSpec((tm, RET_WIDTH), lambda i: (i, c))
    full = lambda a: pl.BlockSpec(a.shape, lambda i: (0, 0), pipeline_mode=pl.Buffered(1))
    return pl.pallas_call(
        _out_mlp_kernel,
        grid=(m // tm,),
        in_specs=[col_block(ret_block), col_block(swa_block), row(D_MODEL), full(w_out_bf), full(gain),
                  full(w_up_bf), full(w_down_bf)],
        out_specs=row(D_MODEL),
        out_shape=jax.ShapeDtypeStruct((m, D_MODEL), F32),
        compiler_params=pltpu.CompilerParams(
            dimension_semantics=("arbitrary",), vmem_limit_bytes=VMEM_LIMIT),
        name="out_mlp",
    )(mix_ret, mix_swa, x2d, w_out_bf, gain, w_up_bf, w_down_bf)


def _split_pair_rows(x, first):
    return jnp.concatenate([jnp.where(first, x, 0.0), jnp.where(first, 0.0, x)], axis=0).astype(BF16)


def _softmax_sink_pv(s, sink_wide, v_t):
    m = jnp.maximum(jnp.max(s, axis=-1, keepdims=True), sink_wide)
    p = jnp.exp(s - jnp.concatenate([m, m], axis=1))
    denom = jnp.sum(p, axis=-1, keepdims=True) + jnp.exp(sink_wide - m)
    return _dot_nt(p.astype(BF16), v_t) / denom


def _softmax_sink_pv_t(s_t, sink_lanes, v_t):
    m = jnp.maximum(jnp.max(s_t, axis=0, keepdims=True), sink_lanes)
    p = jnp.exp(s_t - m)
    denom = jnp.sum(p, axis=0, keepdims=True) + jnp.exp(sink_lanes - m)
    return _dot(v_t, p.astype(BF16)) / denom


def _prompt_consts(intra_ref, qdec_ref, kdec_ref, sdec_ref, bias_ref):
    r = _iota((LANES, LANES), 0)
    lane2 = _iota((LANES, LANES), 1) >= HEAD_DIM
    rf = r.astype(F32)
    ri = _iota((LANES, 2 * LANES), 0)
    ci = _iota((LANES, 2 * LANES), 1)
    diff = (ri - (ci & (LANES - 1))).astype(F32)
    for p in range(N_PAIRS):
        lg = _pair_const(LOG_DECAY, p, lane2)
        qdec_ref[p] = jnp.exp(lg * (rf + 1.0))
        kdec_ref[p] = jnp.exp(lg * (RET_CHUNK - 1.0 - rf))
        sdec_ref[p] = jnp.exp(_pair_const(LOG_DECAY, p, r >= HEAD_DIM) * float(RET_CHUNK))
        lg2 = _pair_const(LOG_DECAY, p, ci >= LANES)
        intra_ref[p] = jnp.where(diff >= 0.0, jnp.exp(lg2 * jnp.maximum(diff, 0.0)), 0.0)
    cols = SWA_GROUP * WINDOW
    kb = _iota((2 * WINDOW, cols), 0)
    cb = _iota((2 * WINDOW, cols), 1)
    grp = cb >> 7
    dist = WINDOW + (cb & (WINDOW - 1)) - kb
    valid = (dist >= 0) & (dist < WINDOW)
    distf = dist.astype(F32)
    for j in range(N_SWA_KV):
        sl = [ALIBI_SLOPES[SWA_GROUP * j + g] for g in range(SWA_GROUP)]
        slope = jnp.where(grp == 0, sl[0], jnp.where(grp == 1, sl[1], jnp.where(grp == 2, sl[2], sl[3])))
        b = jnp.where(valid, -(slope.astype(F32) * distf), NEG_INF)
        bias_ref[0, j] = b
        bias_ref[1, j] = jnp.where(kb >= WINDOW, b, NEG_INF)


def _prompt_layer_kernel(sinks_ref, x_ref, x_next_ref, gain_mix_ref, w_in_ref, qg_ref, kg_ref,
                         w_out_ref, gain_ffn_ref, w_up_ref, w_down_ref,
                         y_ref, ret_ref, kwin_ref, vwin_ref,
                         main_ref, qn_ref, kn_ref, vs_ref, mix_ref, hb_ref,
                         state_ref, prevk_ref, prevv_ref,
                         intra_ref, qdec_ref, kdec_ref, sdec_ref, bias_ref):
    t = pl.program_id(1)
    step = pl.program_id(0) * pl.num_programs(1) + t
    cur = step % 2
    nxt = 1 - cur

    @pl.when(step == 0)
    def _():
        _prompt_consts(intra_ref, qdec_ref, kdec_ref, sdec_ref, bias_ref)
        _in_proj_kernel(x_ref, gain_mix_ref, w_in_ref, qg_ref, kg_ref,
                        main_ref.at[0], qn_ref.at[0], kn_ref.at[0], vs_ref.at[0])

    @pl.when(t == 0)
    def _():
        state_ref[...] = jnp.zeros_like(state_ref)
        prevk_ref[...] = jnp.zeros_like(prevk_ref)
        prevv_ref[...] = jnp.zeros_like(prevv_ref)

    first = _first_half()
    ones_bd = _ones_block_diag()
    bd_mask = (_iota((LANES, LANES), 0) >= HEAD_DIM) == (_iota((LANES, LANES), 1) >= HEAD_DIM)

    n_chunks = PROMPT_TILE // RET_CHUNK
    assert n_chunks == 4
    plan = [dict(a=[], b=[]),
            dict(a=["m0", "m1"], b=[]),
            dict(a=["m2", "m3"], b=[]),
            dict(a=["v", "k"], b=["q0", "q1"])]
    a0 = MAIN_WIDTH
    attn_cols = {"v": (a0 + SWA_WIDTH + SWA_KV_WIDTH, SWA_KV_WIDTH), "k": (a0 + SWA_WIDTH, SWA_KV_WIDTH)}
    attn_cols.update({"q%d" % i: (a0 + 256 * i, 256) for i in range(SWA_WIDTH // 256)})
    cols = lambda base, p: slice(base + p * LANES, base + (p + 1) * LANES)

    def project(items):
        raw = {}
        for it in items:
            if it[0] == "m":
                cb = int(it[1:])
                main_ref[nxt, :, cb * 512:(cb + 1) * 512] = _dot(hb_ref[...], w_in_ref[:, cb * 512:(cb + 1) * 512])
            else:
                lo, width = attn_cols[it]
                raw[it] = _dot(hb_ref[...], w_in_ref[:, lo:lo + width])
        return raw

    def attn_scales(raw):
        return {name: [_head_rms_scale(val[:, cols(0, cq)], ones_bd) for cq in range(val.shape[1] // LANES)]
                for name, val in raw.items() if name != "v"}

    def store_attn(raw, scales):
        for name, val in raw.items():
            if name == "v":
                vs_ref[nxt] = val
            elif name == "k":
                kn_ref[nxt] = (val * scales[name][0]) * kg_ref[...]
            else:
                base = attn_cols[name][0] - a0
                for cq in range(val.shape[1] // LANES):
                    qn_ref[nxt, :, cols(base, cq)] = (val[:, cols(0, cq)] * scales[name][cq]) * qg_ref[...]

    def chunk(c):
        rows = slice(c * RET_CHUNK, (c + 1) * RET_CHUNK)
        pairs = range(N_PAIRS)
        kvs = range(N_SWA_KV)
        if c == 0:
            xn = x_next_ref[...]
            hb_ref[...] = ((xn * lax.rsqrt(jnp.mean(xn * xn, axis=-1, keepdims=True) + EPS))
                           * gain_mix_ref[...]).astype(BF16)

        q = [main_ref[cur, rows, cols(0, p)] for p in pairs]
        k = [main_ref[cur, rows, cols(RET_WIDTH, p)] * K_SCALE for p in pairs]
        v = [main_ref[cur, rows, cols(2 * RET_WIDTH, p)] for p in pairs]
        state = [state_ref[p] for p in pairs]
        kc = kn_ref[cur, rows, :]
        k_sw = pltpu.roll(kc, HEAD_DIM, axis=1)
        v_t = vs_ref[cur, rows, :].T
        is_first = (t == 0).astype(jnp.int32) if c == 0 else 0
        k_dup =[(jnp.where(first, kc, k_sw) if j == 0 else jnp.where(first, k_sw, kc)).astype(BF16)
                 for j in kvs]
        v_tj = [v_t[j * HEAD_DIM:(j + 1) * HEAD_DIM].astype(BF16) for j in kvs]
        q_st = []
        for j in kvs:
            pieces = []
            for g in range(SWA_GROUP):
                qc = qn_ref[cur, rows, cols(0, 2 * j + g // 2)]
                pieces.append(jnp.where(first, qc, 0.0) if g % 2 == 0 else jnp.where(first, 0.0, qc))
            q_st.append(jnp.concatenate(pieces, axis=0).astype(BF16))

        s = [_dot_nt(q[p].astype(BF16), _split_pair_rows(k[p], first)) for p in pairs]
        s_t = [_dot_nt(jnp.concatenate([prevk_ref[j], k_dup[j]], axis=0), q_st[j]) for j in kvs]
        cross = [_dot((q[p] * qdec_ref[p]).astype(BF16), state[p].astype(BF16)) for p in pairs]
        upd = [_dot((k[p] * kdec_ref[p]).T.astype(BF16), v[p].astype(BF16)) for p in pairs]
        raw = project(plan[c]["a"])

        o = [_dot((s[p] * intra_ref[p]).astype(BF16), _split_pair_rows(v[p], first)) + cross[p] for p in pairs]
        o_t = []
        for j in kvs:
            sink_lanes = jnp.concatenate(
                [jnp.full((1, WINDOW), sinks_ref[SWA_GROUP * j + g], F32) for g in range(SWA_GROUP)],
                axis=1)
            v_cat = jnp.concatenate([prevv_ref[j], v_tj[j]], axis=1)
            o_t.append(_softmax_sink_pv_t(s_t[j] * K_SCALE + bias_ref[is_first, j], sink_lanes, v_cat))
        for p in pairs:
            state_ref[p] = state[p] * sdec_ref[p] + jnp.where(bd_mask, upd[p], 0.0)
        for j in kvs:
            prevk_ref[j] = k_dup[j]
            prevv_ref[j] = v_tj[j]
        raw.update(project(plan[c]["b"]))

        scale = [_head_rms_scale(o[p], ones_bd) for p in pairs]
        raw_scales = attn_scales(raw)
        for p in pairs:
            g = main_ref[cur, rows, cols(3 * RET_WIDTH, p)]
            mix_ref[rows, cols(0, p)] = (o[p] * scale[p] * _silu(g)).astype(BF16)
        store_attn(raw, raw_scales)
        for j in kvs:
            for half in range(2):
                pair_t = jnp.concatenate([o_t[j][:, (2 * half) * WINDOW:(2 * half + 1) * WINDOW],
                                          o_t[j][:, (2 * half + 1) * WINDOW:(2 * half + 2) * WINDOW]], axis=0)
                mix_ref[rows, cols(RET_WIDTH, 2 * j + half)] = pair_t.T.astype(BF16)

    for c in range(n_chunks):
        chunk(c)

    @pl.when(t == pl.num_programs(1) - 1)
    def _():
        for p in range(N_PAIRS):
            s = state_ref[p]
            ret_ref[2 * p] = s[:HEAD_DIM, :HEAD_DIM]
            ret_ref[2 * p + 1] = s[HEAD_DIM:, HEAD_DIM:]
        last = slice(PROMPT_TILE - WINDOW, PROMPT_TILE)
        kwin_ref[...] = kn_ref[cur, last, :].T
        vwin_ref[...] = vs_ref[cur, last, :].T

    _out_mlp_kernel(mix_ref.at[:, pl.ds(0, RET_WIDTH)], mix_ref.at[:, pl.ds(RET_WIDTH, SWA_WIDTH)], x_ref,
                    w_out_ref, gain_ffn_ref, w_up_ref, w_down_ref, y_ref)


def _prompt_layer(sinks, x2d, gain_mix, w_in_bf, qg, kg, w_out_bf, gain_ffn, w_up_bf, w_down_bf, batch, seq):
    nt = seq // PROMPT_TILE
    last_tile = batch * nt - 1
    row = lambda w: pl.BlockSpec((PROMPT_TILE, w), lambda b, t: (b * nt + t, 0))
    next_row = pl.BlockSpec((PROMPT_TILE, D_MODEL), lambda b, t: (jnp.minimum(b * nt + t + 1, last_tile), 0))
    full = lambda a: pl.BlockSpec(a.shape, lambda b, t: (0, 0), pipeline_mode=pl.Buffered(1))
    return pl.pallas_call(
        _prompt_layer_kernel,
        grid=(batch, nt),
        in_specs=[pl.BlockSpec(memory_space=pltpu.SMEM), row(D_MODEL), next_row,
                  full(gain_mix), full(w_in_bf), full(qg), full(kg),
                  full(w_out_bf), full(gain_ffn), full(w_up_bf), full(w_down_bf)],
        out_specs=[row(D_MODEL),
                   pl.BlockSpec((None, N_RET_HEADS, HEAD_DIM, HEAD_DIM), lambda b, t: (b, 0, 0, 0)),
                   pl.BlockSpec((None, SWA_KV_WIDTH, WINDOW), lambda b, t: (b, 0, 0)),
                   pl.BlockSpec((None, SWA_KV_WIDTH, WINDOW), lambda b, t: (b, 0, 0))],
        out_shape=[jax.ShapeDtypeStruct((batch * seq, D_MODEL), F32),
                   jax.ShapeDtypeStruct((batch, N_RET_HEADS, HEAD_DIM, HEAD_DIM), F32),
                   jax.ShapeDtypeStruct((batch, SWA_KV_WIDTH, WINDOW), F32),
                   jax.ShapeDtypeStruct((batch, SWA_KV_WIDTH, WINDOW), F32)],
        scratch_shapes=[
            pltpu.VMEM((2, PROMPT_TILE, MAIN_WIDTH), F32),
            pltpu.VMEM((2, PROMPT_TILE, SWA_WIDTH), F32),
            pltpu.VMEM((2, PROMPT_TILE, SWA_KV_WIDTH), F32),
            pltpu.VMEM((2, PROMPT_TILE, SWA_KV_WIDTH), F32),
            pltpu.VMEM((PROMPT_TILE, MIX_WIDTH), BF16),
            pltpu.VMEM((PROMPT_TILE, D_MODEL), BF16),
            pltpu.VMEM((N_PAIRS, LANES, LANES), F32),
            pltpu.VMEM((N_SWA_KV, WINDOW, LANES), BF16),
            pltpu.VMEM((N_SWA_KV, HEAD_DIM, WINDOW), BF16),
            pltpu.VMEM((N_PAIRS, LANES, 2 * LANES), F32),
            pltpu.VMEM((N_PAIRS, LANES, LANES), F32),
            pltpu.VMEM((N_PAIRS, LANES, LANES), F32),
            pltpu.VMEM((N_PAIRS, LANES, LANES), F32),
            pltpu.VMEM((2, N_SWA_KV, 2 * WINDOW, SWA_GROUP * WINDOW), F32),
        ],
        compiler_params=pltpu.CompilerParams(
            dimension_semantics=("arbitrary", "arbitrary"), vmem_limit_bytes=PROMPT_VMEM_LIMIT),
        name="prompt_layer",
    )(sinks, x2d, x2d, gain_mix, w_in_bf, qg, kg, w_out_bf, gain_ffn, w_up_bf, w_down_bf)


def _decode_ret_kernel(dec_seq, nb, qdec_ref, kdec_ref, sdec_ref, intra_ref,
                       q_ref, k_ref, v_ref, g_ref, st_ref,
                       mix_ref, st_out_ref,
                       qt_ref, kt_ref, vt_ref, qdt_ref, kdt_ref, o_ref):
    pair = pl.program_id(0)
    halves = [slice(0, HEAD_DIM), slice(HEAD_DIM, 2 * HEAD_DIM)]
    for l in range(dec_seq):
        rows = pl.ds(l, nb, stride=dec_seq)
        q_t = q_ref[rows, :].T
        k_t = (k_ref[rows, :] * K_SCALE).T
        qt_ref[l] = q_t
        kt_ref[l] = k_t
        vt_ref[l] = v_ref[rows, :].T
        for hh in range(2):
            qdt_ref[l, halves[hh], :] = q_t[halves[hh]] * qdec_ref[2 * pair + hh, l]
            kdt_ref[l, halves[hh], :] = k_t[halves[hh]] * kdec_ref[2 * pair + hh, l]

    e_blk = HEAD_DIM // 2
    for hh in range(2):
        h = 2 * pair + hh
        hs = halves[hh]
        for l in range(dec_seq):
            acc = None
            for m in range(l + 1):
                sc = jnp.sum(qt_ref[l, hs, :] * kt_ref[m, hs, :], axis=0, keepdims=True) * intra_ref[h, l - m]
                term = sc * vt_ref[m, hs, :]
                acc = term if acc is None else acc + term
            o_ref[l, hs, :] = acc
        for eb in range(HEAD_DIM // e_blk):
            es = slice(eb * e_blk, (eb + 1) * e_blk)
            erows = slice(hh * HEAD_DIM + eb * e_blk, hh * HEAD_DIM + (eb + 1) * e_blk)

            def body(d, accs, hh=hh, h=h, es=es, erows=erows):
                s_d = st_ref[hh, d, es, :]
                row = pl.ds(hh * HEAD_DIM + d, 1)
                upd = s_d * sdec_ref[h]
                new_accs = []
                for l in range(dec_seq):
                    new_accs.append(accs[l] + qdt_ref[l, row, :] * s_d)
                    upd = upd + kdt_ref[l, row, :] * vt_ref[l, erows, :]
                st_out_ref[hh, d, es, :] = upd
                return tuple(new_accs)

            zero = jnp.zeros((e_blk, nb), F32)
            accs = lax.fori_loop(0, HEAD_DIM, body, tuple(zero for _ in range(dec_seq)), unroll=2)
            for l in range(dec_seq):
                o_ref[l, erows, :] = o_ref[l, erows, :] + accs[l]

    for l in range(dec_seq):
        o = o_ref[l]
        normed = []
        for hh in range(2):
            oh = o[halves[hh]]
            normed.append(oh * lax.rsqrt(jnp.mean(oh * oh, axis=0, keepdims=True) + EPS))
        rows = pl.ds(l, nb, stride=dec_seq)
        mix_ref[rows, :] = jnp.concatenate(normed, axis=0).T * _silu(g_ref[rows, :])


def _decode_ret(main, state_t, dec_seq):
    nb = state_t.shape[-1]
    m = main.shape[0]
    assert nb == LANES and m == nb * dec_seq
    steps = [j for j in range(dec_seq)]
    tab = lambda f: jnp.asarray([[f(h, j) for j in steps] for h in range(N_RET_HEADS)], F32)
    qdec = tab(lambda h, j: math.exp(LOG_DECAY[h] * (j + 1.0)))
    kdec = tab(lambda h, j: math.exp(LOG_DECAY[h] * (dec_seq - 1.0 - j)))
    intra = tab(lambda h, j: math.exp(LOG_DECAY[h] * j))
    sdec = jnp.asarray([math.exp(LOG_DECAY[h] * dec_seq) for h in range(N_RET_HEADS)], F32)
    smem = pl.BlockSpec(memory_space=pltpu.SMEM)
    col = lambda base: pl.BlockSpec((m, LANES), lambda p: (0, base + p))
    st_spec = pl.BlockSpec((2, HEAD_DIM, HEAD_DIM, nb), lambda p: (p, 0, 0, 0))
    stage = pltpu.VMEM((dec_seq, LANES, nb), F32)
    return pl.pallas_call(
        functools.partial(_decode_ret_kernel, dec_seq, nb),
        grid=(N_PAIRS,),
        in_specs=[smem, smem, smem, smem,
                  col(0), col(N_PAIRS), col(2 * N_PAIRS), col(3 * N_PAIRS), st_spec],
        out_specs=[pl.BlockSpec((m, LANES), lambda p: (0, p)), st_spec],
        out_shape=[jax.ShapeDtypeStruct((m, RET_WIDTH), F32),
                   jax.ShapeDtypeStruct(state_t.shape, F32)],
        scratch_shapes=[stage, stage, stage, stage, stage, stage],
        compiler_params=pltpu.CompilerParams(
            dimension_semantics=("arbitrary",), vmem_limit_bytes=VMEM_LIMIT),
        name="decode_ret",
    )(qdec, kdec, sdec, intra, main, main, main, main, state_t)


DEC_ROWS = 128
DEC_UNROLL = 4


def _decode_attn_consts(dec_seq, bias_ref):
    shift = dec_seq.bit_length() - 1
    rows = N_SWA_HEADS * dec_seq
    rb = _iota((rows, WINDOW), 0)
    cb = _iota((rows, WINDOW), 1)
    head = rb >> shift
    i = rb & (dec_seq - 1)
    slope = jnp.zeros((rows, WINDOW), F32)
    for h in range(N_SWA_HEADS):
        slope = jnp.where(head == h, ALIBI_SLOPES[h], slope)
    bias_ref[0] = jnp.where(cb > i, -(slope * (WINDOW + i - cb).astype(F32)), NEG_INF)
    m = cb & (dec_seq - 1)
    bias_ref[1] = jnp.where(m <= i, -(slope * (i - m).astype(F32)), NEG_INF)


def _decode_attn_kernel(dec_seq, sinks_ref, qn_ref, kn_ref, vs_ref, kt_ref, vt_ref,
                        mix_ref, kt_out_ref, vt_out_ref,
                        bias_ref, qbd_ref, oblk_ref, knew_ref, vnew_ref, knt_ref, vst_ref):
    @pl.when(pl.program_id(0) == 0)
    def _():
        _decode_attn_consts(dec_seq, bias_ref)

    first = _first_half()
    shift = dec_seq.bit_length() - 1

    kn_t = kn_ref[...].T
    vs_t = vs_ref[...].T
    knt_ref[...] = kn_t.astype(BF16)
    vst_ref[...] = vs_t.astype(BF16)
    for bb in range(DEC_GROUP):
        sh = (WINDOW - dec_seq - bb * dec_seq) % LANES
        knew_ref[bb] = pltpu.roll(kn_t, sh, axis=1) if sh else kn_t
        vnew_ref[bb] = pltpu.roll(vs_t, sh, axis=1) if sh else vs_t
    qn = qn_ref[...]
    qn_sw = pltpu.roll(qn, HEAD_DIM, axis=1)
    for h in range(N_SWA_HEADS):
        kv_half = h // SWA_GROUP
        if (h % 2) == kv_half:
            src = qn[:, (h // 2) * LANES:(h // 2 + 1) * LANES]
        else:
            col = (h + 1) // 2
            src = qn_sw[:, col * LANES:(col + 1) * LANES]
        qbd_ref[h] = jnp.where(first, src, 0.0) if kv_half == 0 else jnp.where(first, 0.0, src)

    sink_rows = jnp.concatenate(
        [jnp.full((dec_seq, LANES), sinks_ref[h], F32) for h in range(N_SWA_HEADS)], axis=0)
    col_batch = _iota((N_SWA_HEADS * dec_seq, LANES), 1) >> shift
    keep_old = _iota((1, LANES), 1) < WINDOW - dec_seq

    def per_batches(i, carry):
        bs = [i * DEC_UNROLL + u for u in range(DEC_UNROLL)]
        rows = [pl.ds(pl.multiple_of(b * dec_seq, dec_seq), dec_seq) for b in bs]
        k_old = [kt_ref[b] for b in bs]
        v_old = [vt_ref[b] for b in bs]
        q_st = [jnp.concatenate([qbd_ref[h, r, :] for h in range(N_SWA_HEADS)], axis=0).astype(BF16)
                for r in rows]
        s = [_dot(q_st[u], jnp.concatenate([k_old[u].astype(BF16), knt_ref[...]], axis=1))
             for u in range(DEC_UNROLL)]
        o = []
        for u, b in enumerate(bs):
            bias = jnp.concatenate([bias_ref[0], jnp.where(col_batch == b, bias_ref[1], NEG_INF)], axis=1)
            w_v = jnp.concatenate([v_old[u].astype(BF16), vst_ref[...]], axis=1)
            o.append(_softmax_sink_pv(s[u] * K_SCALE + bias, sink_rows, w_v))
        for u, b in enumerate(bs):
            for h in range(N_SWA_HEADS):
                oblk_ref[h, rows[u], :] = o[u][h * dec_seq:(h + 1) * dec_seq]
            kt_out_ref[b] = jnp.where(keep_old, pltpu.roll(k_old[u], LANES - dec_seq, axis=1), knew_ref[b])
            vt_out_ref[b] = jnp.where(keep_old, pltpu.roll(v_old[u], LANES - dec_seq, axis=1), vnew_ref[b])
        return carry

    lax.fori_loop(0, DEC_GROUP // DEC_UNROLL, per_batches, 0)

    y1 = jnp.where(first, oblk_ref[3], oblk_ref[4])
    moved = pltpu.roll(jnp.concatenate([oblk_ref[1], y1, oblk_ref[6], oblk_ref[6]], axis=1), HEAD_DIM, axis=1)
    outs = [
        jnp.where(first, oblk_ref[0], moved[:, 0:LANES]),
        jnp.where(first, oblk_ref[2], moved[:, LANES:2 * LANES]),
        jnp.where(first, moved[:, 2 * LANES:3 * LANES], oblk_ref[5]),
        jnp.where(first, moved[:, 3 * LANES:4 * LANES], oblk_ref[7]),
    ]
    for c in range(SWA_WIDTH // LANES):
        mix_ref[:, c * LANES:(c + 1) * LANES] = outs[c].astype(BF16)


def _decode_attn(sinks, qn, kn, vs, k_t, v_t, dec_seq):
    nb = k_t.shape[0]
    assert DEC_GROUP * dec_seq == DEC_ROWS and nb % DEC_GROUP == 0 and dec_seq & (dec_seq - 1) == 0
    assert k_t.shape[1:] == (SWA_KV_WIDTH, WINDOW)
    row = lambda w: pl.BlockSpec((DEC_ROWS, w), lambda i: (i, 0))
    cache = pl.BlockSpec((DEC_GROUP, SWA_KV_WIDTH, WINDOW), lambda i: (i, 0, 0))
    return pl.pallas_call(
        functools.partial(_decode_attn_kernel, dec_seq),
        grid=(nb // DEC_GROUP,),
        in_specs=[pl.BlockSpec(memory_space=pltpu.SMEM),
                  row(SWA_WIDTH), row(SWA_KV_WIDTH), row(SWA_KV_WIDTH), cache, cache],
        out_specs=[row(SWA_WIDTH), cache, cache],
        out_shape=[jax.ShapeDtypeStruct((nb * dec_seq, SWA_WIDTH), BF16),
                   jax.ShapeDtypeStruct(k_t.shape, F32),
                   jax.ShapeDtypeStruct(v_t.shape, F32)],
        scratch_shapes=[
            pltpu.VMEM((2, N_SWA_HEADS * dec_seq, WINDOW), F32),
            pltpu.VMEM((N_SWA_HEADS, DEC_ROWS, LANES), F32),
            pltpu.VMEM((N_SWA_HEADS, DEC_ROWS, LANES), F32),
            pltpu.VMEM((DEC_GROUP, SWA_KV_WIDTH, LANES), F32),
            pltpu.VMEM((DEC_GROUP, SWA_KV_WIDTH, LANES), F32),
            pltpu.VMEM((SWA_KV_WIDTH, DEC_ROWS), BF16),
            pltpu.VMEM((SWA_KV_WIDTH, DEC_ROWS), BF16),
        ],
        compiler_params=pltpu.CompilerParams(
            dimension_semantics=("arbitrary",), vmem_limit_bytes=VMEM_LIMIT),
        name="decode_attn",
    )(sinks, qn, kn, vs, k_t, v_t)


def kernel(x_prompt, x_sample, state_ret, cache_swa_k, cache_swa_v, norm_mix_gain, w_in, q_norm_gain,
           k_norm_gain, attn_sinks, w_out, norm_ffn_gain, w_up, w_down):
    batch, seq, d = x_prompt.shape
    nb, dec_seq, _ = x_sample.shape
    wb = cache_swa_k.shape[1]
    assert d == D_MODEL and seq % PROMPT_TILE == 0 and wb == WINDOW

    w_in_bf = w_in.astype(BF16)
    w_out_bf = w_out.astype(BF16)
    w_up_bf = w_up.astype(BF16)
    w_down_bf = w_down.astype(BF16)
    gain_mix = norm_mix_gain.reshape(1, D_MODEL)
    gain_ffn = norm_ffn_gain.reshape(1, D_MODEL)
    qg = jnp.tile(q_norm_gain, 2).reshape(1, LANES)
    kg = jnp.tile(k_norm_gain, 2).reshape(1, LANES)

    xp = x_prompt.reshape(batch * seq, D_MODEL)
    y_p, ret_p, kwin_t, vwin_t = _prompt_layer(attn_sinks, xp, gain_mix, w_in_bf, qg, kg, w_out_bf, gain_ffn,
                                               w_up_bf, w_down_bf, batch, seq)
    y_p = y_p.reshape(batch, seq, D_MODEL)

    def from_key_minor(a_t):
        return jnp.transpose(a_t.reshape(a_t.shape[0], N_SWA_KV, HEAD_DIM, WINDOW), (0, 3, 1, 2))

    def to_key_minor(a):
        return jnp.transpose(a, (0, 2, 3, 1)).reshape(a.shape[0], SWA_KV_WIDTH, WINDOW)

    xs = x_sample.reshape(nb * dec_seq, D_MODEL)
    main_s, qn_s, kn_s, vs_s = _in_proj(xs, gain_mix, w_in_bf, qg, kg)
    mix_ret_s, state_t = _decode_ret(main_s, jnp.transpose(state_ret, (1, 2, 3, 0)), dec_seq)
    mix_swa_s, k_t, v_t = _decode_attn(attn_sinks, qn_s, kn_s, vs_s,
                                       to_key_minor(cache_swa_k), to_key_minor(cache_swa_v), dec_seq)
    y_s = _out_mlp(mix_ret_s, 0, mix_swa_s, 0, xs, w_out_bf, gain_ffn, w_up_bf, w_down_bf)

    return (y_p, y_s.reshape(nb, dec_seq, D_MODEL), ret_p, from_key_minor(kwin_t), from_key_minor(vwin_t),
            jnp.transpose(state_t, (3, 0, 1, 2)), from_key_minor(k_t), from_key_minor(v_t))
```

```python
import functools
import math

import jax
import jax.numpy as jnp
from jax import lax
from jax.experimental import pallas as pl
from jax.experimental.pallas import tpu as pltpu

F32 = jnp.float32
BF16 = jnp.bfloat16

D_MODEL = 1024
HEAD_DIM = 64
N_RET_HEADS = 8
N_SWA_HEADS = 8
N_SWA_KV = 2
SWA_GROUP = N_SWA_HEADS // N_SWA_KV
RET_WIDTH = N_RET_HEADS * HEAD_DIM
SWA_WIDTH = N_SWA_HEADS * HEAD_DIM
SWA_KV_WIDTH = N_SWA_KV * HEAD_DIM
MAIN_WIDTH = 4 * RET_WIDTH
IN_WIDTH = MAIN_WIDTH + SWA_WIDTH + 2 * SWA_KV_WIDTH
MIX_WIDTH = RET_WIDTH + SWA_WIDTH
D_FF = 4 * D_MODEL
WINDOW = 128
RET_CHUNK = 128
EPS = 1e-6
NEG_INF = -1e30

LANES = 128
N_PAIRS = N_RET_HEADS // 2
LOG_DECAY = [math.log(1.0 - 2.0 ** (-5.0 - h)) for h in range(N_RET_HEADS)]
ALIBI_SLOPES = [2.0 ** (-8.0 * (h + 1) / N_SWA_HEADS) for h in range(N_SWA_HEADS)]
K_SCALE = HEAD_DIM ** -0.5

PROMPT_TILE = 512
DEC_GROUP = 16
VMEM_LIMIT = 56 * 1024 * 1024
PROMPT_VMEM_LIMIT = 62 * 1024 * 1024


def _dot(a, b):
    return jnp.dot(a, b, preferred_element_type=F32)


def _dot_nt(a, b):
    return lax.dot_general(a, b, (((1,), (1,)), ((), ())), preferred_element_type=F32)


def _iota(shape, dim):
    return lax.broadcasted_iota(jnp.int32, shape, dim)


def _ones_block_diag():
    same = ((_iota((2 * LANES, LANES), 0) >> 6) & 1) == (_iota((2 * LANES, LANES), 1) >> 6)
    return jnp.where(same, 1.0, 0.0).astype(BF16)


def _head_sumsq(x, ones_bd):
    x2 = x * x
    hi = x2.astype(BF16)
    lo = (x2 - hi.astype(F32)).astype(BF16)
    return _dot(jnp.concatenate([hi, lo], axis=1), ones_bd)


def _head_rms_scale(x, ones_bd):
    return lax.rsqrt(_head_sumsq(x, ones_bd) * (1.0 / HEAD_DIM) + EPS)


def _head_rms_scales(xs, ones_bd):
    if not xs:
        return []
    parts = []
    for x in xs:
        x2 = x * x
        hi = x2.astype(BF16)
        parts.append(jnp.concatenate([hi, (x2 - hi.astype(F32)).astype(BF16)], axis=1))
    total = _dot(jnp.concatenate(parts, axis=0) if len(parts) > 1 else parts[0], ones_bd)
    out, lo = [], 0
    for x in xs:
        out.append(lax.rsqrt(total[lo:lo + x.shape[0]] * (1.0 / HEAD_DIM) + EPS))
        lo += x.shape[0]
    return out


def _first_half():
    return _iota((1, LANES), 1) < HEAD_DIM


def _pair_const(values, pair, lane_is_second):
    return jnp.where(lane_is_second, values[2 * pair + 1], values[2 * pair]).astype(F32)


def _silu(g):
    return g * (1.0 / (1.0 + jnp.exp(-g)))


def _in_proj_kernel(x_ref, gain_ref, w_ref, qg_ref, kg_ref, main_ref, qn_ref, kn_ref, vs_ref):
    x = x_ref[...]
    ms = jnp.mean(x * x, axis=-1, keepdims=True)
    hb = ((x * lax.rsqrt(ms + EPS)) * gain_ref[...]).astype(BF16)
    ones_bd = _ones_block_diag()
    for c in range(MAIN_WIDTH // 512):
        main_ref[:, c * 512:(c + 1) * 512] = _dot(hb, w_ref[:, c * 512:(c + 1) * 512])
    qs = _dot(hb, w_ref[:, MAIN_WIDTH:MAIN_WIDTH + SWA_WIDTH])
    k0 = MAIN_WIDTH + SWA_WIDTH
    ks = _dot(hb, w_ref[:, k0:k0 + SWA_KV_WIDTH])
    vs_ref[...] = _dot(hb, w_ref[:, k0 + SWA_KV_WIDTH:k0 + 2 * SWA_KV_WIDTH])
    q_cols = [qs[:, c * LANES:(c + 1) * LANES] for c in range(SWA_WIDTH // LANES)]
    scales = _head_rms_scales(q_cols + [ks], ones_bd)
    for c, xc in enumerate(q_cols):
        qn_ref[:, c * LANES:(c + 1) * LANES] = (xc * scales[c]) * qg_ref[...]
    kn_ref[...] = (ks * scales[-1]) * kg_ref[...]


ROW_TILE = 512


def _in_proj(x2d, gain, w_in_bf, qg, kg):
    m = x2d.shape[0]
    tm = min(ROW_TILE, m)
    row = lambda w: pl.BlockSpec((tm, w), lambda i: (i, 0))
    full = lambda a: pl.BlockSpec(a.shape, lambda i: (0, 0), pipeline_mode=pl.Buffered(1))
    return pl.pallas_call(
        _in_proj_kernel,
        grid=(m // tm,),
        in_specs=[row(D_MODEL), full(gain), full(w_in_bf), full(qg), full(kg)],
        out_specs=[row(MAIN_WIDTH), row(SWA_WIDTH), row(SWA_KV_WIDTH), row(SWA_KV_WIDTH)],
        out_shape=[jax.ShapeDtypeStruct((m, MAIN_WIDTH), F32),
                   jax.ShapeDtypeStruct((m, SWA_WIDTH), F32),
                   jax.ShapeDtypeStruct((m, SWA_KV_WIDTH), F32),
                   jax.ShapeDtypeStruct((m, SWA_KV_WIDTH), F32)],
        compiler_params=pltpu.CompilerParams(
            dimension_semantics=("arbitrary",), vmem_limit_bytes=VMEM_LIMIT),
        name="in_proj",
    )(x2d, gain, w_in_bf, qg, kg)


FF_CHUNK = 1024


def _out_mlp_kernel(mix_ret_ref, mix_swa_ref, x_ref, w_out_ref, gain_ref, w_up_ref, w_down_ref, y_ref):
    h = x_ref[...] + (_dot(mix_ret_ref[...].astype(BF16), w_out_ref[:RET_WIDTH, :])
                      + _dot(mix_swa_ref[...].astype(BF16), w_out_ref[RET_WIDTH:, :]))
    ms = jnp.mean(h * h, axis=-1, keepdims=True)
    hf = ((h * lax.rsqrt(ms + EPS)) * gain_ref[...]).astype(BF16)
    ff = None
    for c in range(D_FF // FF_CHUNK):
        u = _dot(hf, w_up_ref[:, c * FF_CHUNK:(c + 1) * FF_CHUNK])
        a = jnp.maximum(u, 0.0)
        d = _dot((a * a).astype(BF16), w_down_ref[c * FF_CHUNK:(c + 1) * FF_CHUNK, :])
        ff = d if ff is None else ff + d
    y_ref[...] = h + ff


def _out_mlp(mix_ret, ret_block, mix_swa, swa_block, x2d, w_out_bf, gain, w_up_bf, w_down_bf):
    m = x2d.shape[0]
    tm = min(ROW_TILE, m)
    row = lambda w: pl.BlockSpec((tm, w), lambda i: (i, 0))
    col_block = lambda c: pl.BlockSpec((tm, RET_WIDTH), lambda i: (i, c))
    full = lambda a: pl.BlockSpec(a.shape, lambda i: (0, 0), pipeline_mode=pl.Buffered(1))
    return pl.pallas_call(
        _out_mlp_kernel,
        grid=(m // tm,),
        in_specs=[col_block(ret_block), col_block(swa_block), row(D_MODEL), full(w_out_bf), full(gain),
                  full(w_up_bf), full(w_down_bf)],
        out_specs=row(D_MODEL),
        out_shape=jax.ShapeDtypeStruct((m, D_MODEL), F32),
        compiler_params=pltpu.CompilerParams(
            dimension_semantics=("arbitrary",), vmem_limit_bytes=VMEM_LIMIT),
        name="out_mlp",
    )(mix_ret, mix_swa, x2d, w_out_bf, gain, w_up_bf, w_down_bf)


def _split_pair_rows(x, first):
    return jnp.concatenate([jnp.where(first, x, 0.0), jnp.where(first, 0.0, x)], axis=0).astype(BF16)


def _softmax_sink_pv(s, sink_wide, v_t):
    m = jnp.maximum(jnp.max(s, axis=-1, keepdims=True), sink_wide)
    p = jnp.exp(s - jnp.concatenate([m, m], axis=1))
    denom = jnp.sum(p, axis=-1, keepdims=True) + jnp.exp(sink_wide - m)
    return _dot_nt(p.astype(BF16), v_t) / denom


def _softmax_sink_pv_t(s_t, sink_lanes, v_t):
    m = jnp.maximum(jnp.max(s_t, axis=0, keepdims=True), sink_lanes)
    p = jnp.exp(s_t - m)
    denom = jnp.sum(p, axis=0, keepdims=True) + jnp.exp(sink_lanes - m)
    return _dot(v_t, p.astype(BF16)) / denom


def _prompt_consts(intra_ref, qdec_ref, kdec_ref, sdec_ref, bias_ref):
    r = _iota((LANES, LANES), 0)
    lane2 = _iota((LANES, LANES), 1) >= HEAD_DIM
    rf = r.astype(F32)
    ri = _iota((LANES, 2 * LANES), 0)
    ci = _iota((LANES, 2 * LANES), 1)
    diff = (ri - (ci & (LANES - 1))).astype(F32)
    for p in range(N_PAIRS):
        lg = _pair_const(LOG_DECAY, p, lane2)
        qdec_ref[p] = jnp.exp(lg * (rf + 1.0))
        kdec_ref[p] = jnp.exp(lg * (RET_CHUNK - 1.0 - rf))
        sdec_ref[p] = jnp.exp(_pair_const(LOG_DECAY, p, r >= HEAD_DIM) * float(RET_CHUNK))
        lg2 = _pair_const(LOG_DECAY, p, ci >= LANES)
        intra_ref[p] = jnp.where(diff >= 0.0, jnp.exp(lg2 * jnp.maximum(diff, 0.0)), 0.0)
    cols = SWA_GROUP * WINDOW
    kb = _iota((2 * WINDOW, cols), 0)
    cb = _iota((2 * WINDOW, cols), 1)
    grp = cb >> 7
    dist = WINDOW + (cb & (WINDOW - 1)) - kb
    valid = (dist >= 0) & (dist < WINDOW)
    distf = dist.astype(F32)
    for j in range(N_SWA_KV):
        sl = [ALIBI_SLOPES[SWA_GROUP * j + g] for g in range(SWA_GROUP)]
        slope = jnp.where(grp == 0, sl[0], jnp.where(grp == 1, sl[1], jnp.where(grp == 2, sl[2], sl[3])))
        b = jnp.where(valid, -(slope.astype(F32) * distf), NEG_INF)
        bias_ref[0, j] = b
        bias_ref[1, j] = jnp.where(kb >= WINDOW, b, NEG_INF)


def _prompt_layer_kernel(sinks_ref, x_ref, x_next_ref, gain_mix_ref, w_in_ref, qg_ref, kg_ref,
                         w_out_hbm, gain_ffn_ref, w_up_hbm, w_down_hbm,
                         y_ref, ret_ref, kwin_ref, vwin_ref,
                         main_ref, qn_ref, kn_ref, vs_ref, mix_ref, hb_ref,
                         state_ref, prevk_ref, prevv_ref,
                         intra_ref, qdec_ref, kdec_ref, sdec_ref, bias_ref,
                         w_out_ref, w_up_ref, w_down_ref, w_sem):
    t = pl.program_id(1)
    step = pl.program_id(0) * pl.num_programs(1) + t
    cur = step % 2
    nxt = 1 - cur

    late_weights = [pltpu.make_async_copy(src, dst, w_sem.at[i]) for i, (src, dst) in enumerate(
        [(w_out_hbm, w_out_ref), (w_up_hbm, w_up_ref), (w_down_hbm, w_down_ref)])]

    @pl.when(step == 0)
    def _():
        for copy in late_weights:
            copy.start()
        _prompt_consts(intra_ref, qdec_ref, kdec_ref, sdec_ref, bias_ref)
        _in_proj_kernel(x_ref, gain_mix_ref, w_in_ref, qg_ref, kg_ref,
                        main_ref.at[0], qn_ref.at[0], kn_ref.at[0], vs_ref.at[0])

    @pl.when(t == 0)
    def _():
        state_ref[...] = jnp.zeros_like(state_ref)
        prevk_ref[...] = jnp.zeros_like(prevk_ref)
        prevv_ref[...] = jnp.zeros_like(prevv_ref)

    first = _first_half()
    ones_bd = _ones_block_diag()
    bd_mask = (_iota((LANES, LANES), 0) >= HEAD_DIM) == (_iota((LANES, LANES), 1) >= HEAD_DIM)

    n_chunks = PROMPT_TILE // RET_CHUNK
    assert n_chunks == 4
    plan = [dict(a=[], b=[]),
            dict(a=["m0", "m1"], b=[]),
            dict(a=["m2", "m3"], b=[]),
            dict(a=["v", "k"], b=["q0", "q1"])]
    a0 = MAIN_WIDTH
    attn_cols = {"v": (a0 + SWA_WIDTH + SWA_KV_WIDTH, SWA_KV_WIDTH), "k": (a0 + SWA_WIDTH, SWA_KV_WIDTH)}
    attn_cols.update({"q%d" % i: (a0 + 256 * i, 256) for i in range(SWA_WIDTH // 256)})
    cols = lambda base, p: slice(base + p * LANES, base + (p + 1) * LANES)

    def project(items):
        raw = {}
        for it in items:
            if it[0] == "m":
                cb = int(it[1:])
                main_ref[nxt, :, cb * 512:(cb + 1) * 512] = _dot(hb_ref[...], w_in_ref[:, cb * 512:(cb + 1) * 512])
            else:
                lo, width = attn_cols[it]
                raw[it] = _dot(hb_ref[...], w_in_ref[:, lo:lo + width])
        return raw

    def attn_scales(raw):
        return {name: [_head_rms_scale(val[:, cols(0, cq)], ones_bd) for cq in range(val.shape[1] // LANES)]
                for name, val in raw.items() if name != "v"}

    def store_attn(raw, scales):
        for name, val in raw.items():
            if name == "v":
                vs_ref[nxt] = val
            elif name == "k":
                kn_ref[nxt] = (val * scales[name][0]) * kg_ref[...]
            else:
                base = attn_cols[name][0] - a0
                for cq in range(val.shape[1] // LANES):
                    qn_ref[nxt, :, cols(base, cq)] = (val[:, cols(0, cq)] * scales[name][cq]) * qg_ref[...]

    def chunk(c):
        rows = slice(c * RET_CHUNK, (c + 1) * RET_CHUNK)
        pairs = range(N_PAIRS)
        kvs = range(N_SWA_KV)
        if c == 0:
            xn = x_next_ref[...]
            hb_ref[...] = ((xn * lax.rsqrt(jnp.mean(xn * xn, axis=-1, keepdims=True) + EPS))
                           * gain_mix_ref[...]).astype(BF16)

        q = [main_ref[cur, rows, cols(0, p)] for p in pairs]
        k = [main_ref[cur, rows, cols(RET_WIDTH, p)] * K_SCALE for p in pairs]
        v = [main_ref[cur, rows, cols(2 * RET_WIDTH, p)] for p in pairs]
        state = [state_ref[p] for p in pairs]
        kc = kn_ref[cur, rows, :]
        k_sw = pltpu.roll(kc, HEAD_DIM, axis=1)
        v_t = vs_ref[cur, rows, :].T
        is_first = (t == 0).astype(jnp.int32) if c == 0 else 0
        k_dup =[(jnp.where(first, kc, k_sw) if j == 0 else jnp.where(first, k_sw, kc)).astype(BF16)
                 for j in kvs]
        v_tj = [v_t[j * HEAD_DIM:(j + 1) * HEAD_DIM].astype(BF16) for j in kvs]
        q_st = []
        for j in kvs:
            pieces = []
            for g in range(SWA_GROUP):
                qc = qn_ref[cur, rows, cols(0, 2 * j + g // 2)]
                pieces.append(jnp.where(first, qc, 0.0) if g % 2 == 0 else jnp.where(first, 0.0, qc))
            q_st.append(jnp.concatenate(pieces, axis=0).astype(BF16))

        s = [_dot_nt(q[p].astype(BF16), _split_pair_rows(k[p], first)) for p in pairs]
        s_t = [_dot_nt(jnp.concatenate([prevk_ref[j], k_dup[j]], axis=0), q_st[j]) for j in kvs]
        cross = [_dot((q[p] * qdec_ref[p]).astype(BF16), state[p].astype(BF16)) for p in pairs]
        upd = [_dot((k[p] * kdec_ref[p]).T.astype(BF16), v[p].astype(BF16)) for p in pairs]
        raw = project(plan[c]["a"])

        o = [_dot((s[p] * intra_ref[p]).astype(BF16), _split_pair_rows(v[p], first)) + cross[p] for p in pairs]
        o_t = []
        for j in kvs:
            sink_lanes = jnp.concatenate(
                [jnp.full((1, WINDOW), sinks_ref[SWA_GROUP * j + g], F32) for g in range(SWA_GROUP)],
                axis=1)
            v_cat = jnp.concatenate([prevv_ref[j], v_tj[j]], axis=1)
            o_t.append(_softmax_sink_pv_t(s_t[j] * K_SCALE + bias_ref[is_first, j], sink_lanes, v_cat))
        for p in pairs:
            state_ref[p] = state[p] * sdec_ref[p] + jnp.where(bd_mask, upd[p], 0.0)
        for j in kvs:
            prevk_ref[j] = k_dup[j]
            prevv_ref[j] = v_tj[j]
        raw.update(project(plan[c]["b"]))

        scale = [_head_rms_scale(o[p], ones_bd) for p in pairs]
        raw_scales = attn_scales(raw)
        for p in pairs:
            g = main_ref[cur, rows, cols(3 * RET_WIDTH, p)]
            mix_ref[rows, cols(0, p)] = (o[p] * scale[p] * _silu(g)).astype(BF16)
        store_attn(raw, raw_scales)
        for j in kvs:
            for half in range(2):
                pair_t = jnp.concatenate([o_t[j][:, (2 * half) * WINDOW:(2 * half + 1) * WINDOW],
                                          o_t[j][:, (2 * half + 1) * WINDOW:(2 * half + 2) * WINDOW]], axis=0)
                mix_ref[rows, cols(RET_WIDTH, 2 * j + half)] = pair_t.T.astype(BF16)

    for c in range(n_chunks):
        chunk(c)

    @pl.when(t == pl.num_programs(1) - 1)
    def _():
        for p in range(N_PAIRS):
            s = state_ref[p]
            ret_ref[2 * p] = s[:HEAD_DIM, :HEAD_DIM]
            ret_ref[2 * p + 1] = s[HEAD_DIM:, HEAD_DIM:]
        last = slice(PROMPT_TILE - WINDOW, PROMPT_TILE)
        kwin_ref[...] = kn_ref[cur, last, :].T
        vwin_ref[...] = vs_ref[cur, last, :].T

    @pl.when(step == 0)
    def _():
        for copy in late_weights:
            copy.wait()

    _out_mlp_kernel(mix_ref.at[:, pl.ds(0, RET_WIDTH)], mix_ref.at[:, pl.ds(RET_WIDTH, SWA_WIDTH)], x_ref,
                    w_out_ref, gain_ffn_ref, w_up_ref, w_down_ref, y_ref)


def _prompt_layer(sinks, x2d, gain_mix, w_in_bf, qg, kg, w_out_bf, gain_ffn, w_up_bf, w_down_bf, batch, seq):
    nt = seq // PROMPT_TILE
    last_tile = batch * nt - 1
    row = lambda w: pl.BlockSpec((PROMPT_TILE, w), lambda b, t: (b * nt + t, 0))
    next_row = pl.BlockSpec((PROMPT_TILE, D_MODEL), lambda b, t: (jnp.minimum(b * nt + t + 1, last_tile), 0))
    full = lambda a: pl.BlockSpec(a.shape, lambda b, t: (0, 0), pipeline_mode=pl.Buffered(1))
    in_hbm = pl.BlockSpec(memory_space=pl.ANY)
    return pl.pallas_call(
        _prompt_layer_kernel,
        grid=(batch, nt),
        in_specs=[pl.BlockSpec(memory_space=pltpu.SMEM), row(D_MODEL), next_row,
                  full(gain_mix), full(w_in_bf), full(qg), full(kg),
                  in_hbm, full(gain_ffn), in_hbm, in_hbm],
        out_specs=[row(D_MODEL),
                   pl.BlockSpec((None, N_RET_HEADS, HEAD_DIM, HEAD_DIM), lambda b, t: (b, 0, 0, 0)),
                   pl.BlockSpec((None, SWA_KV_WIDTH, WINDOW), lambda b, t: (b, 0, 0)),
                   pl.BlockSpec((None, SWA_KV_WIDTH, WINDOW), lambda b, t: (b, 0, 0))],
        out_shape=[jax.ShapeDtypeStruct((batch * seq, D_MODEL), F32),
                   jax.ShapeDtypeStruct((batch, N_RET_HEADS, HEAD_DIM, HEAD_DIM), F32),
                   jax.ShapeDtypeStruct((batch, SWA_KV_WIDTH, WINDOW), F32),
                   jax.ShapeDtypeStruct((batch, SWA_KV_WIDTH, WINDOW), F32)],
        scratch_shapes=[
            pltpu.VMEM((2, PROMPT_TILE, MAIN_WIDTH), F32),
            pltpu.VMEM((2, PROMPT_TILE, SWA_WIDTH), F32),
            pltpu.VMEM((2, PROMPT_TILE, SWA_KV_WIDTH), F32),
            pltpu.VMEM((2, PROMPT_TILE, SWA_KV_WIDTH), F32),
            pltpu.VMEM((PROMPT_TILE, MIX_WIDTH), BF16),
            pltpu.VMEM((PROMPT_TILE, D_MODEL), BF16),
            pltpu.VMEM((N_PAIRS, LANES, LANES), F32),
            pltpu.VMEM((N_SWA_KV, WINDOW, LANES), BF16),
            pltpu.VMEM((N_SWA_KV, HEAD_DIM, WINDOW), BF16),
            pltpu.VMEM((N_PAIRS, LANES, 2 * LANES), F32),
            pltpu.VMEM((N_PAIRS, LANES, LANES), F32),
            pltpu.VMEM((N_PAIRS, LANES, LANES), F32),
            pltpu.VMEM((N_PAIRS, LANES, LANES), F32),
            pltpu.VMEM((2, N_SWA_KV, 2 * WINDOW, SWA_GROUP * WINDOW), F32),
            pltpu.VMEM(w_out_bf.shape, BF16),
            pltpu.VMEM(w_up_bf.shape, BF16),
            pltpu.VMEM(w_down_bf.shape, BF16),
            pltpu.SemaphoreType.DMA((3,)),
        ],
        compiler_params=pltpu.CompilerParams(
            dimension_semantics=("arbitrary", "arbitrary"), vmem_limit_bytes=PROMPT_VMEM_LIMIT),
        name="prompt_layer",
    )(sinks, x2d, x2d, gain_mix, w_in_bf, qg, kg, w_out_bf, gain_ffn, w_up_bf, w_down_bf)


def _decode_ret_kernel(dec_seq, nb, qdec_ref, kdec_ref, sdec_ref, intra_ref,
                       q_ref, k_ref, v_ref, g_ref, st_ref,
                       mix_ref, st_out_ref,
                       qt_ref, kt_ref, vt_ref, qdt_ref, kdt_ref, o_ref):
    pair = pl.program_id(0)
    halves = [slice(0, HEAD_DIM), slice(HEAD_DIM, 2 * HEAD_DIM)]
    for l in range(dec_seq):
        rows = pl.ds(l, nb, stride=dec_seq)
        q_t = q_ref[rows, :].T
        k_t = (k_ref[rows, :] * K_SCALE).T
        qt_ref[l] = q_t
        kt_ref[l] = k_t
        vt_ref[l] = v_ref[rows, :].T
        for hh in range(2):
            qdt_ref[l, halves[hh], :] = q_t[halves[hh]] * qdec_ref[2 * pair + hh, l]
            kdt_ref[l, halves[hh], :] = k_t[halves[hh]] * kdec_ref[2 * pair + hh, l]

    e_blk = HEAD_DIM // 2
    for hh in range(2):
        h = 2 * pair + hh
        hs = halves[hh]
        for l in range(dec_seq):
            acc = None
            for m in range(l + 1):
                sc = jnp.sum(qt_ref[l, hs, :] * kt_ref[m, hs, :], axis=0, keepdims=True) * intra_ref[h, l - m]
                term = sc * vt_ref[m, hs, :]
                acc = term if acc is None else acc + term
            o_ref[l, hs, :] = acc
        for eb in range(HEAD_DIM // e_blk):
            es = slice(eb * e_blk, (eb + 1) * e_blk)
            erows = slice(hh * HEAD_DIM + eb * e_blk, hh * HEAD_DIM + (eb + 1) * e_blk)

            def body(d, accs, hh=hh, h=h, es=es, erows=erows):
                s_d = st_ref[hh, d, es, :]
                row = pl.ds(hh * HEAD_DIM + d, 1)
                upd = s_d * sdec_ref[h]
                new_accs = []
                for l in range(dec_seq):
                    new_accs.append(accs[l] + qdt_ref[l, row, :] * s_d)
                    upd = upd + kdt_ref[l, row, :] * vt_ref[l, erows, :]
                st_out_ref[hh, d, es, :] = upd
                return tuple(new_accs)

            zero = jnp.zeros((e_blk, nb), F32)
            accs = lax.fori_loop(0, HEAD_DIM, body, tuple(zero for _ in range(dec_seq)), unroll=2)
            for l in range(dec_seq):
                o_ref[l, erows, :] = o_ref[l, erows, :] + accs[l]

    for l in range(dec_seq):
        o = o_ref[l]
        normed = []
        for hh in range(2):
            oh = o[halves[hh]]
            normed.append(oh * lax.rsqrt(jnp.mean(oh * oh, axis=0, keepdims=True) + EPS))
        rows = pl.ds(l, nb, stride=dec_seq)
        mix_ref[rows, :] = jnp.concatenate(normed, axis=0).T * _silu(g_ref[rows, :])


def _decode_ret(main, state_t, dec_seq):
    nb = state_t.shape[-1]
    m = main.shape[0]
    assert nb == LANES and m == nb * dec_seq
    steps = [j for j in range(dec_seq)]
    tab = lambda f: jnp.asarray([[f(h, j) for j in steps] for h in range(N_RET_HEADS)], F32)
    qdec = tab(lambda h, j: math.exp(LOG_DECAY[h] * (j + 1.0)))
    kdec = tab(lambda h, j: math.exp(LOG_DECAY[h] * (dec_seq - 1.0 - j)))
    intra = tab(lambda h, j: math.exp(LOG_DECAY[h] * j))
    sdec = jnp.asarray([math.exp(LOG_DECAY[h] * dec_seq) for h in range(N_RET_HEADS)], F32)
    smem = pl.BlockSpec(memory_space=pltpu.SMEM)
    col = lambda base: pl.BlockSpec((m, LANES), lambda p: (0, base + p))
    st_spec = pl.BlockSpec((2, HEAD_DIM, HEAD_DIM, nb), lambda p: (p, 0, 0, 0))
    stage = pltpu.VMEM((dec_seq, LANES, nb), F32)
    return pl.pallas_call(
        functools.partial(_decode_ret_kernel, dec_seq, nb),
        grid=(N_PAIRS,),
        in_specs=[smem, smem, smem, smem,
                  col(0), col(N_PAIRS), col(2 * N_PAIRS), col(3 * N_PAIRS), st_spec],
        out_specs=[pl.BlockSpec((m, LANES), lambda p: (0, p)), st_spec],
        out_shape=[jax.ShapeDtypeStruct((m, RET_WIDTH), F32),
                   jax.ShapeDtypeStruct(state_t.shape, F32)],
        scratch_shapes=[stage, stage, stage, stage, stage, stage],
        compiler_params=pltpu.CompilerParams(
            dimension_semantics=("arbitrary",), vmem_limit_bytes=VMEM_LIMIT),
        name="decode_ret",
    )(qdec, kdec, sdec, intra, main, main, main, main, state_t)


DEC_ROWS = 128
DEC_UNROLL = 4


def _decode_attn_consts(dec_seq, bias_ref):
    shift = dec_seq.bit_length() - 1
    rows = N_SWA_HEADS * dec_seq
    rb = _iota((rows, WINDOW), 0)
    cb = _iota((rows, WINDOW), 1)
    head = rb >> shift
    i = rb & (dec_seq - 1)
    slope = jnp.zeros((rows, WINDOW), F32)
    for h in range(N_SWA_HEADS):
        slope = jnp.where(head == h, ALIBI_SLOPES[h], slope)
    bias_ref[0] = jnp.where(cb > i, -(slope * (WINDOW + i - cb).astype(F32)), NEG_INF)
    m = cb & (dec_seq - 1)
    bias_ref[1] = jnp.where(m <= i, -(slope * (i - m).astype(F32)), NEG_INF)


def _decode_attn_kernel(dec_seq, sinks_ref, qn_ref, kn_ref, vs_ref, kt_ref, vt_ref,
                        mix_ref, kt_out_ref, vt_out_ref,
                        bias_ref, qbd_ref, oblk_ref, knew_ref, vnew_ref, knt_ref, vst_ref):
    @pl.when(pl.program_id(0) == 0)
    def _():
        _decode_attn_consts(dec_seq, bias_ref)

    first = _first_half()
    shift = dec_seq.bit_length() - 1

    kn_t = kn_ref[...].T
    vs_t = vs_ref[...].T
    knt_ref[...] = kn_t.astype(BF16)
    vst_ref[...] = vs_t.astype(BF16)
    for bb in range(DEC_GROUP):
        sh = (WINDOW - dec_seq - bb * dec_seq) % LANES
        knew_ref[bb] = pltpu.roll(kn_t, sh, axis=1) if sh else kn_t
        vnew_ref[bb] = pltpu.roll(vs_t, sh, axis=1) if sh else vs_t
    qn = qn_ref[...]
    qn_sw = pltpu.roll(qn, HEAD_DIM, axis=1)
    for h in range(N_SWA_HEADS):
        kv_half = h // SWA_GROUP
        if (h % 2) == kv_half:
            src = qn[:, (h // 2) * LANES:(h // 2 + 1) * LANES]
        else:
            col = (h + 1) // 2
            src = qn_sw[:, col * LANES:(col + 1) * LANES]
        qbd_ref[h] = jnp.where(first, src, 0.0) if kv_half == 0 else jnp.where(first, 0.0, src)

    sink_rows = jnp.concatenate(
        [jnp.full((dec_seq, LANES), sinks_ref[h], F32) for h in range(N_SWA_HEADS)], axis=0)
    col_batch = _iota((N_SWA_HEADS * dec_seq, LANES), 1) >> shift
    keep_old = _iota((1, LANES), 1) < WINDOW - dec_seq

    def per_batches(i, carry):
        bs = [i * DEC_UNROLL + u for u in range(DEC_UNROLL)]
        rows = [pl.ds(pl.multiple_of(b * dec_seq, dec_seq), dec_seq) for b in bs]
        k_old = [kt_ref[b] for b in bs]
        v_old = [vt_ref[b] for b in bs]
        q_st = [jnp.concatenate([qbd_ref[h, r, :] for h in range(N_SWA_HEADS)], axis=0).astype(BF16)
                for r in rows]
        s = [_dot(q_st[u], jnp.concatenate([k_old[u].astype(BF16), knt_ref[...]], axis=1))
             for u in range(DEC_UNROLL)]
        o = []
        for u, b in enumerate(bs):
            bias = jnp.concatenate([bias_ref[0], jnp.where(col_batch == b, bias_ref[1], NEG_INF)], axis=1)
            w_v = jnp.concatenate([v_old[u].astype(BF16), vst_ref[...]], axis=1)
            o.append(_softmax_sink_pv(s[u] * K_SCALE + bias, sink_rows, w_v))
        for u, b in enumerate(bs):
            for h in range(N_SWA_HEADS):
                oblk_ref[h, rows[u], :] = o[u][h * dec_seq:(h + 1) * dec_seq]
            kt_out_ref[b] = jnp.where(keep_old, pltpu.roll(k_old[u], LANES - dec_seq, axis=1), knew_ref[b])
            vt_out_ref[b] = jnp.where(keep_old, pltpu.roll(v_old[u], LANES - dec_seq, axis=1), vnew_ref[b])
        return carry

    lax.fori_loop(0, DEC_GROUP // DEC_UNROLL, per_batches, 0)

    y1 = jnp.where(first, oblk_ref[3], oblk_ref[4])
    moved = pltpu.roll(jnp.concatenate([oblk_ref[1], y1, oblk_ref[6], oblk_ref[6]], axis=1), HEAD_DIM, axis=1)
    outs = [
        jnp.where(first, oblk_ref[0], moved[:, 0:LANES]),
        jnp.where(first, oblk_ref[2], moved[:, LANES:2 * LANES]),
        jnp.where(first, moved[:, 2 * LANES:3 * LANES], oblk_ref[5]),
        jnp.where(first, moved[:, 3 * LANES:4 * LANES], oblk_ref[7]),
    ]
    for c in range(SWA_WIDTH // LANES):
        mix_ref[:, c * LANES:(c + 1) * LANES] = outs[c].astype(BF16)


def _decode_attn(sinks, qn, kn, vs, k_t, v_t, dec_seq):
    nb = k_t.shape[0]
    assert DEC_GROUP * dec_seq == DEC_ROWS and nb % DEC_GROUP == 0 and dec_seq & (dec_seq - 1) == 0
    assert k_t.shape[1:] == (SWA_KV_WIDTH, WINDOW)
    row = lambda w: pl.BlockSpec((DEC_ROWS, w), lambda i: (i, 0))
    cache = pl.BlockSpec((DEC_GROUP, SWA_KV_WIDTH, WINDOW), lambda i: (i, 0, 0))
    return pl.pallas_call(
        functools.partial(_decode_attn_kernel, dec_seq),
        grid=(nb // DEC_GROUP,),
        in_specs=[pl.BlockSpec(memory_space=pltpu.SMEM),
                  row(SWA_WIDTH), row(SWA_KV_WIDTH), row(SWA_KV_WIDTH), cache, cache],
        out_specs=[row(SWA_WIDTH), cache, cache],
        out_shape=[jax.ShapeDtypeStruct((nb * dec_seq, SWA_WIDTH), BF16),
                   jax.ShapeDtypeStruct(k_t.shape, F32),
                   jax.ShapeDtypeStruct(v_t.shape, F32)],
        scratch_shapes=[
            pltpu.VMEM((2, N_SWA_HEADS * dec_seq, WINDOW), F32),
            pltpu.VMEM((N_SWA_HEADS, DEC_ROWS, LANES), F32),
            pltpu.VMEM((N_SWA_HEADS, DEC_ROWS, LANES), F32),
            pltpu.VMEM((DEC_GROUP, SWA_KV_WIDTH, LANES), F32),
            pltpu.VMEM((DEC_GROUP, SWA_KV_WIDTH, LANES), F32),
            pltpu.VMEM((SWA_KV_WIDTH, DEC_ROWS), BF16),
            pltpu.VMEM((SWA_KV_WIDTH, DEC_ROWS), BF16),
        ],
        compiler_params=pltpu.CompilerParams(
            dimension_semantics=("arbitrary",), vmem_limit_bytes=VMEM_LIMIT),
        name="decode_attn",
    )(sinks, qn, kn, vs, k_t, v_t)


def kernel(x_prompt, x_sample, state_ret, cache_swa_k, cache_swa_v, norm_mix_gain, w_in, q_norm_gain,
           k_norm_gain, attn_sinks, w_out, norm_ffn_gain, w_up, w_down):
    batch, seq, d = x_prompt.shape
    nb, dec_seq, _ = x_sample.shape
    wb = cache_swa_k.shape[1]
    assert d == D_MODEL and seq % PROMPT_TILE == 0 and wb == WINDOW

    w_in_bf = w_in.astype(BF16)
    w_out_bf = w_out.astype(BF16)
    w_up_bf = w_up.astype(BF16)
    w_down_bf = w_down.astype(BF16)
    gain_mix = norm_mix_gain.reshape(1, D_MODEL)
    gain_ffn = norm_ffn_gain.reshape(1, D_MODEL)
    qg = jnp.tile(q_norm_gain, 2).reshape(1, LANES)
    kg = jnp.tile(k_norm_gain, 2).reshape(1, LANES)

    xp = x_prompt.reshape(batch * seq, D_MODEL)
    y_p, ret_p, kwin_t, vwin_t = _prompt_layer(attn_sinks, xp, gain_mix, w_in_bf, qg, kg, w_out_bf, gain_ffn,
                                               w_up_bf, w_down_bf, batch, seq)
    y_p = y_p.reshape(batch, seq, D_MODEL)

    def from_key_minor(a_t):
        return jnp.transpose(a_t.reshape(a_t.shape[0], N_SWA_KV, HEAD_DIM, WINDOW), (0, 3, 1, 2))

    def to_key_minor(a):
        return jnp.transpose(a, (0, 2, 3, 1)).reshape(a.shape[0], SWA_KV_WIDTH, WINDOW)

    xs = x_sample.reshape(nb * dec_seq, D_MODEL)
    main_s, qn_s, kn_s, vs_s = _in_proj(xs, gain_mix, w_in_bf, qg, kg)
    mix_ret_s, state_t = _decode_ret(main_s, jnp.transpose(state_ret, (1, 2, 3, 0)), dec_seq)
    mix_swa_s, k_t, v_t = _decode_attn(attn_sinks, qn_s, kn_s, vs_s,
                                       to_key_minor(cache_swa_k), to_key_minor(cache_swa_v), dec_seq)
    y_s = _out_mlp(mix_ret_s, 0, mix_swa_s, 0, xs, w_out_bf, gain_ffn, w_up_bf, w_down_bf)

    return (y_p, y_s.reshape(nb, dec_seq, D_MODEL), ret_p, from_key_minor(kwin_t), from_key_minor(vwin_t),
            jnp.transpose(state_t, (3, 0, 1, 2)), from_key_minor(k_t), from_key_minor(v_t))
```

```python
import functools
import math

import jax
import jax.numpy as jnp
from jax import lax
from jax.experimental import pallas as pl
from jax.experimental.pallas import tpu as pltpu

F32 = jnp.float32
BF16 = jnp.bfloat16

D_MODEL = 1024
HEAD_DIM = 64
N_RET_HEADS = 8
N_SWA_HEADS = 8
N_SWA_KV = 2
SWA_GROUP = N_SWA_HEADS // N_SWA_KV
RET_WIDTH = N_RET_HEADS * HEAD_DIM
SWA_WIDTH = N_SWA_HEADS * HEAD_DIM
SWA_KV_WIDTH = N_SWA_KV * HEAD_DIM
MAIN_WIDTH = 4 * RET_WIDTH
IN_WIDTH = MAIN_WIDTH + SWA_WIDTH + 2 * SWA_KV_WIDTH
MIX_WIDTH = RET_WIDTH + SWA_WIDTH
D_FF = 4 * D_MODEL
WINDOW = 128
RET_CHUNK = 128
EPS = 1e-6
NEG_INF = -1e30

LANES = 128
N_PAIRS = N_RET_HEADS // 2
LOG_DECAY = [math.log(1.0 - 2.0 ** (-5.0 - h)) for h in range(N_RET_HEADS)]
ALIBI_SLOPES = [2.0 ** (-8.0 * (h + 1) / N_SWA_HEADS) for h in range(N_SWA_HEADS)]
K_SCALE = HEAD_DIM ** -0.5

PROMPT_TILE = 512
DEC_GROUP = 16
VMEM_LIMIT = 56 * 1024 * 1024
PROMPT_VMEM_LIMIT = 62 * 1024 * 1024


def _dot(a, b):
    return jnp.dot(a, b, preferred_element_type=F32)


def _dot_nt(a, b):
    return lax.dot_general(a, b, (((1,), (1,)), ((), ())), preferred_element_type=F32)


def _iota(shape, dim):
    return lax.broadcasted_iota(jnp.int32, shape, dim)


def _ones_block_diag():
    same = ((_iota((2 * LANES, LANES), 0) >> 6) & 1) == (_iota((2 * LANES, LANES), 1) >> 6)
    return jnp.where(same, 1.0, 0.0).astype(BF16)


def _head_sumsq(x, ones_bd):
    x2 = x * x
    hi = x2.astype(BF16)
    lo = (x2 - hi.astype(F32)).astype(BF16)
    return _dot(jnp.concatenate([hi, lo], axis=1), ones_bd)


def _head_rms_scale(x, ones_bd):
    return lax.rsqrt(_head_sumsq(x, ones_bd) * (1.0 / HEAD_DIM) + EPS)


def _head_rms_scales(xs, ones_bd):
    if not xs:
        return []
    parts = []
    for x in xs:
        x2 = x * x
        hi = x2.astype(BF16)
        parts.append(jnp.concatenate([hi, (x2 - hi.astype(F32)).astype(BF16)], axis=1))
    total = _dot(jnp.concatenate(parts, axis=0) if len(parts) > 1 else parts[0], ones_bd)
    out, lo = [], 0
    for x in xs:
        out.append(lax.rsqrt(total[lo:lo + x.shape[0]] * (1.0 / HEAD_DIM) + EPS))
        lo += x.shape[0]
    return out


def _first_half():
    return _iota((1, LANES), 1) < HEAD_DIM


def _pair_const(values, pair, lane_is_second):
    return jnp.where(lane_is_second, values[2 * pair + 1], values[2 * pair]).astype(F32)


def _silu(g):
    return g * (1.0 / (1.0 + jnp.exp(-g)))


def _in_proj_kernel(x_ref, gain_ref, w_ref, qg_ref, kg_ref, main_ref, qn_ref, kn_ref, vs_ref):
    x = x_ref[...]
    ms = jnp.mean(x * x, axis=-1, keepdims=True)
    hb = ((x * lax.rsqrt(ms + EPS)) * gain_ref[...]).astype(BF16)
    ones_bd = _ones_block_diag()
    for c in range(MAIN_WIDTH // 512):
        main_ref[:, c * 512:(c + 1) * 512] = _dot(hb, w_ref[:, c * 512:(c + 1) * 512])
    qs = _dot(hb, w_ref[:, MAIN_WIDTH:MAIN_WIDTH + SWA_WIDTH])
    k0 = MAIN_WIDTH + SWA_WIDTH
    ks = _dot(hb, w_ref[:, k0:k0 + SWA_KV_WIDTH])
    vs_ref[...] = _dot(hb, w_ref[:, k0 + SWA_KV_WIDTH:k0 + 2 * SWA_KV_WIDTH])
    q_cols = [qs[:, c * LANES:(c + 1) * LANES] for c in range(SWA_WIDTH // LANES)]
    scales = _head_rms_scales(q_cols + [ks], ones_bd)
    for c, xc in enumerate(q_cols):
        qn_ref[:, c * LANES:(c + 1) * LANES] = (xc * scales[c]) * qg_ref[...]
    kn_ref[...] = (ks * scales[-1]) * kg_ref[...]


ROW_TILE = 512


W_IN_BLOCKS = [(0, 768), (768, 768), (1536, 768), (2304, 512)]
assert sum(w for _, w in W_IN_BLOCKS) == IN_WIDTH


def _in_proj_cast_kernel(x_ref, gain_ref, w_hbm, qg_ref, kg_ref,
                         main_ref, qn_ref, kn_ref, vs_ref, w_bf_hbm,
                         w_v, stage, in_sem, out_sem):
    step = pl.program_id(0)
    block_in = [pltpu.make_async_copy(w_hbm.at[:, pl.ds(lo, width)], stage.at[:, pl.ds(lo, width)], in_sem.at[i])
                for i, (lo, width) in enumerate(W_IN_BLOCKS)]
    writeback = pltpu.make_async_copy(w_v, w_bf_hbm, out_sem.at[0])

    @pl.when(step == 0)
    def _():
        for copy in block_in:
            copy.start()
        for copy, (lo, width) in zip(block_in, W_IN_BLOCKS):
            copy.wait()
            w_v[:, lo:lo + width] = stage[:, lo:lo + width].astype(BF16)
        writeback.start()

    _in_proj_kernel(x_ref, gain_ref, w_v, qg_ref, kg_ref, main_ref, qn_ref, kn_ref, vs_ref)

    @pl.when(step == pl.num_programs(0) - 1)
    def _():
        writeback.wait()


def _in_proj_cast(x2d, gain, w_in, qg, kg):
    m = x2d.shape[0]
    tm = min(ROW_TILE, m)
    row = lambda w: pl.BlockSpec((tm, w), lambda i: (i, 0))
    full = lambda a: pl.BlockSpec(a.shape, lambda i: (0, 0), pipeline_mode=pl.Buffered(1))
    hbm = pl.BlockSpec(memory_space=pl.ANY)
    return pl.pallas_call(
        _in_proj_cast_kernel,
        grid=(m // tm,),
        in_specs=[row(D_MODEL), full(gain), hbm, full(qg), full(kg)],
        out_specs=[row(MAIN_WIDTH), row(SWA_WIDTH), row(SWA_KV_WIDTH), row(SWA_KV_WIDTH), hbm],
        out_shape=[jax.ShapeDtypeStruct((m, MAIN_WIDTH), F32),
                   jax.ShapeDtypeStruct((m, SWA_WIDTH), F32),
                   jax.ShapeDtypeStruct((m, SWA_KV_WIDTH), F32),
                   jax.ShapeDtypeStruct((m, SWA_KV_WIDTH), F32),
                   jax.ShapeDtypeStruct(w_in.shape, BF16)],
        scratch_shapes=[pltpu.VMEM(w_in.shape, BF16), pltpu.VMEM(w_in.shape, F32),
                        pltpu.SemaphoreType.DMA((len(W_IN_BLOCKS),)), pltpu.SemaphoreType.DMA((1,))],
        compiler_params=pltpu.CompilerParams(
            dimension_semantics=("arbitrary",), vmem_limit_bytes=VMEM_LIMIT),
        name="in_proj_cast",
    )(x2d, gain, w_in, qg, kg)


FF_CHUNK = 1024


def _out_mlp_kernel(mix_ret_ref, mix_swa_ref, x_ref, w_out_ref, gain_ref, w_up_ref, w_down_ref, y_ref):
    h = x_ref[...] + (_dot(mix_ret_ref[...].astype(BF16), w_out_ref[:RET_WIDTH, :])
                      + _dot(mix_swa_ref[...].astype(BF16), w_out_ref[RET_WIDTH:, :]))
    ms = jnp.mean(h * h, axis=-1, keepdims=True)
    hf = ((h * lax.rsqrt(ms + EPS)) * gain_ref[...]).astype(BF16)
    ff = None
    for c in range(D_FF // FF_CHUNK):
        u = _dot(hf, w_up_ref[:, c * FF_CHUNK:(c + 1) * FF_CHUNK])
        a = jnp.maximum(u, 0.0)
        d = _dot((a * a).astype(BF16), w_down_ref[c * FF_CHUNK:(c + 1) * FF_CHUNK, :])
        ff = d if ff is None else ff + d
    y_ref[...] = h + ff


CAST_CHUNK = 512
N_FF_CHUNKS = D_FF // CAST_CHUNK


def _out_mlp_cast_kernel(mix_ret_ref, mix_swa_ref, x_ref, gain_ref, w_out_hbm, w_up_hbm, w_down_hbm,
                         y_ref, w_out_bf_hbm, w_up_bf_hbm, w_down_bf_hbm,
                         w_out_v, w_up_v, w_down_v, stage_out, stage_up, stage_down, in_sem, out_sem):
    step = pl.program_id(0)

    def out_in(i):
        return pltpu.make_async_copy(w_out_hbm.at[pl.ds(i * CAST_CHUNK, CAST_CHUNK), :], stage_out.at[i],
                                     in_sem.at[0, i])

    def up_in(c):
        return pltpu.make_async_copy(w_up_hbm.at[:, pl.ds(c * CAST_CHUNK, CAST_CHUNK)], stage_up.at[c % 2],
                                     in_sem.at[1, c % 2])

    def down_in(c):
        return pltpu.make_async_copy(w_down_hbm.at[pl.ds(c * CAST_CHUNK, CAST_CHUNK), :], stage_down.at[c % 2],
                                     in_sem.at[2, c % 2])

    writebacks = [pltpu.make_async_copy(src, dst, out_sem.at[i]) for i, (src, dst) in enumerate(
        [(w_out_v, w_out_bf_hbm), (w_up_v, w_up_bf_hbm), (w_down_v, w_down_bf_hbm)])]

    @pl.when(step == 0)
    def _():
        n_out = D_MODEL // CAST_CHUNK
        for i in range(n_out):
            out_in(i).start()
        up_in(0).start()
        down_in(0).start()
        for i in range(n_out):
            out_in(i).wait()
            w_out_v[i * CAST_CHUNK:(i + 1) * CAST_CHUNK, :] = stage_out[i].astype(BF16)
        writebacks[0].start()
        h = x_ref[...] + (_dot(mix_ret_ref[...].astype(BF16), w_out_v[:RET_WIDTH, :])
                          + _dot(mix_swa_ref[...].astype(BF16), w_out_v[RET_WIDTH:, :]))
        ms = jnp.mean(h * h, axis=-1, keepdims=True)
        hf = ((h * lax.rsqrt(ms + EPS)) * gain_ref[...]).astype(BF16)
        ff = None
        for c in range(N_FF_CHUNKS):
            chunk = slice(c * CAST_CHUNK, (c + 1) * CAST_CHUNK)
            if c + 1 < N_FF_CHUNKS:
                up_in(c + 1).start()
                down_in(c + 1).start()
            up_in(c).wait()
            down_in(c).wait()
            w_up_v[:, chunk] = stage_up[c % 2].astype(BF16)
            w_down_v[chunk, :] = stage_down[c % 2].astype(BF16)
            a = jnp.maximum(_dot(hf, w_up_v[:, chunk]), 0.0)
            d = _dot((a * a).astype(BF16), w_down_v[chunk, :])
            ff = d if ff is None else ff + d
        writebacks[1].start()
        writebacks[2].start()
        y_ref[...] = h + ff

    @pl.when(step > 0)
    def _():
        _out_mlp_kernel(mix_ret_ref, mix_swa_ref, x_ref, w_out_v, gain_ref, w_up_v, w_down_v, y_ref)

    @pl.when(step == pl.num_programs(0) - 1)
    def _():
        for wb in writebacks:
            wb.wait()


def _out_mlp_cast(mix_ret, mix_swa, x2d, gain, w_out, w_up, w_down):
    m = x2d.shape[0]
    tm = min(ROW_TILE, m)
    assert D_MODEL % CAST_CHUNK == 0 and D_FF % CAST_CHUNK == 0
    row = lambda w: pl.BlockSpec((tm, w), lambda i: (i, 0))
    full = lambda a: pl.BlockSpec(a.shape, lambda i: (0, 0), pipeline_mode=pl.Buffered(1))
    hbm = pl.BlockSpec(memory_space=pl.ANY)
    return pl.pallas_call(
        _out_mlp_cast_kernel,
        grid=(m // tm,),
        in_specs=[row(RET_WIDTH), row(SWA_WIDTH), row(D_MODEL), full(gain), hbm, hbm, hbm],
        out_specs=[row(D_MODEL), hbm, hbm, hbm],
        out_shape=[jax.ShapeDtypeStruct((m, D_MODEL), F32),
                   jax.ShapeDtypeStruct(w_out.shape, BF16),
                   jax.ShapeDtypeStruct(w_up.shape, BF16),
                   jax.ShapeDtypeStruct(w_down.shape, BF16)],
        scratch_shapes=[
            pltpu.VMEM(w_out.shape, BF16), pltpu.VMEM(w_up.shape, BF16), pltpu.VMEM(w_down.shape, BF16),
            pltpu.VMEM((D_MODEL // CAST_CHUNK, CAST_CHUNK, D_MODEL), F32),
            pltpu.VMEM((2, D_MODEL, CAST_CHUNK), F32),
            pltpu.VMEM((2, CAST_CHUNK, D_MODEL), F32),
            pltpu.SemaphoreType.DMA((3, 2)),
            pltpu.SemaphoreType.DMA((3,)),
        ],
        compiler_params=pltpu.CompilerParams(
            dimension_semantics=("arbitrary",), vmem_limit_bytes=VMEM_LIMIT),
        name="out_mlp_cast",
    )(mix_ret, mix_swa, x2d, gain, w_out, w_up, w_down)


def _split_pair_rows(x, first):
    return jnp.concatenate([jnp.where(first, x, 0.0), jnp.where(first, 0.0, x)], axis=0).astype(BF16)


def _softmax_sink_pv(s, sink_wide, v_t):
    m = jnp.maximum(jnp.max(s, axis=-1, keepdims=True), sink_wide)
    p = jnp.exp(s - jnp.concatenate([m, m], axis=1))
    denom = jnp.sum(p, axis=-1, keepdims=True) + jnp.exp(sink_wide - m)
    return _dot_nt(p.astype(BF16), v_t) / denom


def _softmax_sink_pv_t(s_t, sink_lanes, v_t):
    m = jnp.maximum(jnp.max(s_t, axis=0, keepdims=True), sink_lanes)
    p = jnp.exp(s_t - m)
    denom = jnp.sum(p, axis=0, keepdims=True) + jnp.exp(sink_lanes - m)
    return _dot(v_t, p.astype(BF16)) / denom


def _prompt_consts(intra_ref, qdec_ref, kdec_ref, sdec_ref, bias_ref):
    r = _iota((LANES, LANES), 0)
    lane2 = _iota((LANES, LANES), 1) >= HEAD_DIM
    rf = r.astype(F32)
    ri = _iota((LANES, 2 * LANES), 0)
    ci = _iota((LANES, 2 * LANES), 1)
    diff = (ri - (ci & (LANES - 1))).astype(F32)
    for p in range(N_PAIRS):
        lg = _pair_const(LOG_DECAY, p, lane2)
        qdec_ref[p] = jnp.exp(lg * (rf + 1.0))
        kdec_ref[p] = jnp.exp(lg * (RET_CHUNK - 1.0 - rf))
        sdec_ref[p] = jnp.exp(_pair_const(LOG_DECAY, p, r >= HEAD_DIM) * float(RET_CHUNK))
        lg2 = _pair_const(LOG_DECAY, p, ci >= LANES)
        intra_ref[p] = jnp.where(diff >= 0.0, jnp.exp(lg2 * jnp.maximum(diff, 0.0)), 0.0)
    cols = SWA_GROUP * WINDOW
    kb = _iota((2 * WINDOW, cols), 0)
    cb = _iota((2 * WINDOW, cols), 1)
    grp = cb >> 7
    dist = WINDOW + (cb & (WINDOW - 1)) - kb
    valid = (dist >= 0) & (dist < WINDOW)
    distf = dist.astype(F32)
    for j in range(N_SWA_KV):
        sl = [ALIBI_SLOPES[SWA_GROUP * j + g] for g in range(SWA_GROUP)]
        slope = jnp.where(grp == 0, sl[0], jnp.where(grp == 1, sl[1], jnp.where(grp == 2, sl[2], sl[3])))
        b = jnp.where(valid, -(slope.astype(F32) * distf), NEG_INF)
        bias_ref[0, j] = b
        bias_ref[1, j] = jnp.where(kb >= WINDOW, b, NEG_INF)


def _prompt_layer_kernel(sinks_ref, x_ref, x_next_ref, gain_mix_ref, w_in_ref, qg_ref, kg_ref,
                         w_out_hbm, gain_ffn_ref, w_up_hbm, w_down_hbm,
                         y_ref, ret_ref, kwin_ref, vwin_ref,
                         main_ref, qn_ref, kn_ref, vs_ref, mix_ref, hb_ref,
                         state_ref, prevk_ref, prevv_ref,
                         intra_ref, qdec_ref, kdec_ref, sdec_ref, bias_ref,
                         w_out_ref, w_up_ref, w_down_ref, w_sem):
    t = pl.program_id(1)
    step = pl.program_id(0) * pl.num_programs(1) + t
    cur = step % 2
    nxt = 1 - cur

    late_weights = [pltpu.make_async_copy(src, dst, w_sem.at[i]) for i, (src, dst) in enumerate(
        [(w_out_hbm, w_out_ref), (w_up_hbm, w_up_ref), (w_down_hbm, w_down_ref)])]

    @pl.when(step == 0)
    def _():
        for copy in late_weights:
            copy.start()
        _prompt_consts(intra_ref, qdec_ref, kdec_ref, sdec_ref, bias_ref)
        _in_proj_kernel(x_ref, gain_mix_ref, w_in_ref, qg_ref, kg_ref,
                        main_ref.at[0], qn_ref.at[0], kn_ref.at[0], vs_ref.at[0])

    @pl.when(t == 0)
    def _():
        state_ref[...] = jnp.zeros_like(state_ref)
        prevk_ref[...] = jnp.zeros_like(prevk_ref)
        prevv_ref[...] = jnp.zeros_like(prevv_ref)

    first = _first_half()
    ones_bd = _ones_block_diag()
    bd_mask = (_iota((LANES, LANES), 0) >= HEAD_DIM) == (_iota((LANES, LANES), 1) >= HEAD_DIM)

    n_chunks = PROMPT_TILE // RET_CHUNK
    assert n_chunks == 4
    plan = [dict(a=[], b=[]),
            dict(a=["m0", "m1"], b=[]),
            dict(a=["m2", "m3"], b=[]),
            dict(a=["v", "k"], b=["q0", "q1"])]
    a0 = MAIN_WIDTH
    attn_cols = {"v": (a0 + SWA_WIDTH + SWA_KV_WIDTH, SWA_KV_WIDTH), "k": (a0 + SWA_WIDTH, SWA_KV_WIDTH)}
    attn_cols.update({"q%d" % i: (a0 + 256 * i, 256) for i in range(SWA_WIDTH // 256)})
    cols = lambda base, p: slice(base + p * LANES, base + (p + 1) * LANES)

    def project(items):
        raw = {}
        for it in items:
            if it[0] == "m":
                cb = int(it[1:])
                main_ref[nxt, :, cb * 512:(cb + 1) * 512] = _dot(hb_ref[...], w_in_ref[:, cb * 512:(cb + 1) * 512])
            else:
                lo, width = attn_cols[it]
                raw[it] = _dot(hb_ref[...], w_in_ref[:, lo:lo + width])
        return raw

    def attn_scales(raw):
        return {name: [_head_rms_scale(val[:, cols(0, cq)], ones_bd) for cq in range(val.shape[1] // LANES)]
                for name, val in raw.items() if name != "v"}

    def store_attn(raw, scales):
        for name, val in raw.items():
            if name == "v":
                vs_ref[nxt] = val
            elif name == "k":
                kn_ref[nxt] = (val * scales[name][0]) * kg_ref[...]
            else:
                base = attn_cols[name][0] - a0
                for cq in range(val.shape[1] // LANES):
                    qn_ref[nxt, :, cols(base, cq)] = (val[:, cols(0, cq)] * scales[name][cq]) * qg_ref[...]

    def chunk(c):
        rows = slice(c * RET_CHUNK, (c + 1) * RET_CHUNK)
        pairs = range(N_PAIRS)
        kvs = range(N_SWA_KV)
        if c == 0:
            xn = x_next_ref[...]
            hb_ref[...] = ((xn * lax.rsqrt(jnp.mean(xn * xn, axis=-1, keepdims=True) + EPS))
                           * gain_mix_ref[...]).astype(BF16)

        q = [main_ref[cur, rows, cols(0, p)] for p in pairs]
        k = [main_ref[cur, rows, cols(RET_WIDTH, p)] * K_SCALE for p in pairs]
        v = [main_ref[cur, rows, cols(2 * RET_WIDTH, p)] for p in pairs]
        state = [state_ref[p] for p in pairs]
        kc = kn_ref[cur, rows, :]
        k_sw = pltpu.roll(kc, HEAD_DIM, axis=1)
        v_t = vs_ref[cur, rows, :].T
        is_first = (t == 0).astype(jnp.int32) if c == 0 else 0
        k_dup =[(jnp.where(first, kc, k_sw) if j == 0 else jnp.where(first, k_sw, kc)).astype(BF16)
                 for j in kvs]
        v_tj = [v_t[j * HEAD_DIM:(j + 1) * HEAD_DIM].astype(BF16) for j in kvs]
        q_st = []
        for j in kvs:
            pieces = []
            for g in range(SWA_GROUP):
                qc = qn_ref[cur, rows, cols(0, 2 * j + g // 2)]
                pieces.append(jnp.where(first, qc, 0.0) if g % 2 == 0 else jnp.where(first, 0.0, qc))
            q_st.append(jnp.concatenate(pieces, axis=0).astype(BF16))

        s = [_dot_nt(q[p].astype(BF16), _split_pair_rows(k[p], first)) for p in pairs]
        s_t = [_dot_nt(jnp.concatenate([prevk_ref[j], k_dup[j]], axis=0), q_st[j]) for j in kvs]
        cross = [_dot((q[p] * qdec_ref[p]).astype(BF16), state[p].astype(BF16)) for p in pairs]
        upd = [_dot((k[p] * kdec_ref[p]).T.astype(BF16), v[p].astype(BF16)) for p in pairs]
        raw = project(plan[c]["a"])

        o = [_dot((s[p] * intra_ref[p]).astype(BF16), _split_pair_rows(v[p], first)) + cross[p] for p in pairs]
        o_t = []
        for j in kvs:
            sink_lanes = jnp.concatenate(
                [jnp.full((1, WINDOW), sinks_ref[SWA_GROUP * j + g], F32) for g in range(SWA_GROUP)],
                axis=1)
            v_cat = jnp.concatenate([prevv_ref[j], v_tj[j]], axis=1)
            o_t.append(_softmax_sink_pv_t(s_t[j] * K_SCALE + bias_ref[is_first, j], sink_lanes, v_cat))
        for p in pairs:
            state_ref[p] = state[p] * sdec_ref[p] + jnp.where(bd_mask, upd[p], 0.0)
        for j in kvs:
            prevk_ref[j] = k_dup[j]
            prevv_ref[j] = v_tj[j]
        raw.update(project(plan[c]["b"]))

        scale = [_head_rms_scale(o[p], ones_bd) for p in pairs]
        raw_scales = attn_scales(raw)
        for p in pairs:
            g = main_ref[cur, rows, cols(3 * RET_WIDTH, p)]
            mix_ref[rows, cols(0, p)] = (o[p] * scale[p] * _silu(g)).astype(BF16)
        store_attn(raw, raw_scales)
        for j in kvs:
            for half in range(2):
                pair_t = jnp.concatenate([o_t[j][:, (2 * half) * WINDOW:(2 * half + 1) * WINDOW],
                                          o_t[j][:, (2 * half + 1) * WINDOW:(2 * half + 2) * WINDOW]], axis=0)
                mix_ref[rows, cols(RET_WIDTH, 2 * j + half)] = pair_t.T.astype(BF16)

    for c in range(n_chunks):
        chunk(c)

    @pl.when(t == pl.num_programs(1) - 1)
    def _():
        for p in range(N_PAIRS):
            s = state_ref[p]
            ret_ref[2 * p] = s[:HEAD_DIM, :HEAD_DIM]
            ret_ref[2 * p + 1] = s[HEAD_DIM:, HEAD_DIM:]
        last = slice(PROMPT_TILE - WINDOW, PROMPT_TILE)
        kwin_ref[...] = kn_ref[cur, last, :].T
        vwin_ref[...] = vs_ref[cur, last, :].T

    @pl.when(step == 0)
    def _():
        for copy in late_weights:
            copy.wait()

    _out_mlp_kernel(mix_ref.at[:, pl.ds(0, RET_WIDTH)], mix_ref.at[:, pl.ds(RET_WIDTH, SWA_WIDTH)], x_ref,
                    w_out_ref, gain_ffn_ref, w_up_ref, w_down_ref, y_ref)


def _prompt_layer(sinks, x2d, gain_mix, w_in_bf, qg, kg, w_out_bf, gain_ffn, w_up_bf, w_down_bf, batch, seq):
    nt = seq // PROMPT_TILE
    last_tile = batch * nt - 1
    row = lambda w: pl.BlockSpec((PROMPT_TILE, w), lambda b, t: (b * nt + t, 0))
    next_row = pl.BlockSpec((PROMPT_TILE, D_MODEL), lambda b, t: (jnp.minimum(b * nt + t + 1, last_tile), 0))
    full = lambda a: pl.BlockSpec(a.shape, lambda b, t: (0, 0), pipeline_mode=pl.Buffered(1))
    in_hbm = pl.BlockSpec(memory_space=pl.ANY)
    return pl.pallas_call(
        _prompt_layer_kernel,
        grid=(batch, nt),
        in_specs=[pl.BlockSpec(memory_space=pltpu.SMEM), row(D_MODEL), next_row,
                  full(gain_mix), full(w_in_bf), full(qg), full(kg),
                  in_hbm, full(gain_ffn), in_hbm, in_hbm],
        out_specs=[row(D_MODEL),
                   pl.BlockSpec((None, N_RET_HEADS, HEAD_DIM, HEAD_DIM), lambda b, t: (b, 0, 0, 0)),
                   pl.BlockSpec((None, SWA_KV_WIDTH, WINDOW), lambda b, t: (b, 0, 0)),
                   pl.BlockSpec((None, SWA_KV_WIDTH, WINDOW), lambda b, t: (b, 0, 0))],
        out_shape=[jax.ShapeDtypeStruct((batch * seq, D_MODEL), F32),
                   jax.ShapeDtypeStruct((batch, N_RET_HEADS, HEAD_DIM, HEAD_DIM), F32),
                   jax.ShapeDtypeStruct((batch, SWA_KV_WIDTH, WINDOW), F32),
                   jax.ShapeDtypeStruct((batch, SWA_KV_WIDTH, WINDOW), F32)],
        scratch_shapes=[
            pltpu.VMEM((2, PROMPT_TILE, MAIN_WIDTH), F32),
            pltpu.VMEM((2, PROMPT_TILE, SWA_WIDTH), F32),
            pltpu.VMEM((2, PROMPT_TILE, SWA_KV_WIDTH), F32),
            pltpu.VMEM((2, PROMPT_TILE, SWA_KV_WIDTH), F32),
            pltpu.VMEM((PROMPT_TILE, MIX_WIDTH), BF16),
            pltpu.VMEM((PROMPT_TILE, D_MODEL), BF16),
            pltpu.VMEM((N_PAIRS, LANES, LANES), F32),
            pltpu.VMEM((N_SWA_KV, WINDOW, LANES), BF16),
            pltpu.VMEM((N_SWA_KV, HEAD_DIM, WINDOW), BF16),
            pltpu.VMEM((N_PAIRS, LANES, 2 * LANES), F32),
            pltpu.VMEM((N_PAIRS, LANES, LANES), F32),
            pltpu.VMEM((N_PAIRS, LANES, LANES), F32),
            pltpu.VMEM((N_PAIRS, LANES, LANES), F32),
            pltpu.VMEM((2, N_SWA_KV, 2 * WINDOW, SWA_GROUP * WINDOW), F32),
            pltpu.VMEM(w_out_bf.shape, BF16),
            pltpu.VMEM(w_up_bf.shape, BF16),
            pltpu.VMEM(w_down_bf.shape, BF16),
            pltpu.SemaphoreType.DMA((3,)),
        ],
        compiler_params=pltpu.CompilerParams(
            dimension_semantics=("arbitrary", "arbitrary"), vmem_limit_bytes=PROMPT_VMEM_LIMIT),
        name="prompt_layer",
    )(sinks, x2d, x2d, gain_mix, w_in_bf, qg, kg, w_out_bf, gain_ffn, w_up_bf, w_down_bf)


def _decode_ret_kernel(dec_seq, nb, qdec_ref, kdec_ref, sdec_ref, intra_ref,
                       q_ref, k_ref, v_ref, g_ref, st_ref,
                       mix_ref, st_out_ref,
                       qt_ref, kt_ref, vt_ref, qdt_ref, kdt_ref, o_ref):
    pair = pl.program_id(0)
    halves = [slice(0, HEAD_DIM), slice(HEAD_DIM, 2 * HEAD_DIM)]
    for l in range(dec_seq):
        rows = pl.ds(l, nb, stride=dec_seq)
        q_t = q_ref[rows, :].T
        k_t = (k_ref[rows, :] * K_SCALE).T
        qt_ref[l] = q_t
        kt_ref[l] = k_t
        vt_ref[l] = v_ref[rows, :].T
        for hh in range(2):
            qdt_ref[l, halves[hh], :] = q_t[halves[hh]] * qdec_ref[2 * pair + hh, l]
            kdt_ref[l, halves[hh], :] = k_t[halves[hh]] * kdec_ref[2 * pair + hh, l]

    e_blk = HEAD_DIM // 2
    for hh in range(2):
        h = 2 * pair + hh
        hs = halves[hh]
        for l in range(dec_seq):
            acc = None
            for m in range(l + 1):
                sc = jnp.sum(qt_ref[l, hs, :] * kt_ref[m, hs, :], axis=0, keepdims=True) * intra_ref[h, l - m]
                term = sc * vt_ref[m, hs, :]
                acc = term if acc is None else acc + term
            o_ref[l, hs, :] = acc
        for eb in range(HEAD_DIM // e_blk):
            es = slice(eb * e_blk, (eb + 1) * e_blk)
            erows = slice(hh * HEAD_DIM + eb * e_blk, hh * HEAD_DIM + (eb + 1) * e_blk)

            def body(d, accs, hh=hh, h=h, es=es, erows=erows):
                s_d = st_ref[hh, d, es, :]
                row = pl.ds(hh * HEAD_DIM + d, 1)
                upd = s_d * sdec_ref[h]
                new_accs = []
                for l in range(dec_seq):
                    new_accs.append(accs[l] + qdt_ref[l, row, :] * s_d)
                    upd = upd + kdt_ref[l, row, :] * vt_ref[l, erows, :]
                st_out_ref[hh, d, es, :] = upd
                return tuple(new_accs)

            zero = jnp.zeros((e_blk, nb), F32)
            accs = lax.fori_loop(0, HEAD_DIM, body, tuple(zero for _ in range(dec_seq)), unroll=2)
            for l in range(dec_seq):
                o_ref[l, erows, :] = o_ref[l, erows, :] + accs[l]

    for l in range(dec_seq):
        o = o_ref[l]
        normed = []
        for hh in range(2):
            oh = o[halves[hh]]
            normed.append(oh * lax.rsqrt(jnp.mean(oh * oh, axis=0, keepdims=True) + EPS))
        rows = pl.ds(l, nb, stride=dec_seq)
        mix_ref[rows, :] = jnp.concatenate(normed, axis=0).T * _silu(g_ref[rows, :])


def _decode_ret(main, state_t, dec_seq):
    nb = state_t.shape[-1]
    m = main.shape[0]
    assert nb == LANES and m == nb * dec_seq
    steps = [j for j in range(dec_seq)]
    tab = lambda f: jnp.asarray([[f(h, j) for j in steps] for h in range(N_RET_HEADS)], F32)
    qdec = tab(lambda h, j: math.exp(LOG_DECAY[h] * (j + 1.0)))
    kdec = tab(lambda h, j: math.exp(LOG_DECAY[h] * (dec_seq - 1.0 - j)))
    intra = tab(lambda h, j: math.exp(LOG_DECAY[h] * j))
    sdec = jnp.asarray([math.exp(LOG_DECAY[h] * dec_seq) for h in range(N_RET_HEADS)], F32)
    smem = pl.BlockSpec(memory_space=pltpu.SMEM)
    col = lambda base: pl.BlockSpec((m, LANES), lambda p: (0, base + p))
    st_spec = pl.BlockSpec((2, HEAD_DIM, HEAD_DIM, nb), lambda p: (p, 0, 0, 0))
    stage = pltpu.VMEM((dec_seq, LANES, nb), F32)
    return pl.pallas_call(
        functools.partial(_decode_ret_kernel, dec_seq, nb),
        grid=(N_PAIRS,),
        in_specs=[smem, smem, smem, smem,
                  col(0), col(N_PAIRS), col(2 * N_PAIRS), col(3 * N_PAIRS), st_spec],
        out_specs=[pl.BlockSpec((m, LANES), lambda p: (0, p)), st_spec],
        out_shape=[jax.ShapeDtypeStruct((m, RET_WIDTH), F32),
                   jax.ShapeDtypeStruct(state_t.shape, F32)],
        scratch_shapes=[stage, stage, stage, stage, stage, stage],
        compiler_params=pltpu.CompilerParams(
            dimension_semantics=("arbitrary",), vmem_limit_bytes=VMEM_LIMIT),
        name="decode_ret",
    )(qdec, kdec, sdec, intra, main, main, main, main, state_t)


DEC_ROWS = 128
DEC_UNROLL = 4


def _decode_attn_consts(dec_seq, bias_ref):
    shift = dec_seq.bit_length() - 1
    rows = N_SWA_HEADS * dec_seq
    rb = _iota((rows, WINDOW), 0)
    cb = _iota((rows, WINDOW), 1)
    head = rb >> shift
    i = rb & (dec_seq - 1)
    slope = jnp.zeros((rows, WINDOW), F32)
    for h in range(N_SWA_HEADS):
        slope = jnp.where(head == h, ALIBI_SLOPES[h], slope)
    bias_ref[0] = jnp.where(cb > i, -(slope * (WINDOW + i - cb).astype(F32)), NEG_INF)
    m = cb & (dec_seq - 1)
    bias_ref[1] = jnp.where(m <= i, -(slope * (i - m).astype(F32)), NEG_INF)


def _decode_attn_kernel(dec_seq, sinks_ref, qn_ref, kn_ref, vs_ref, kt_ref, vt_ref,
                        mix_ref, kt_out_ref, vt_out_ref,
                        bias_ref, qbd_ref, oblk_ref, knew_ref, vnew_ref, knt_ref, vst_ref):
    @pl.when(pl.program_id(0) == 0)
    def _():
        _decode_attn_consts(dec_seq, bias_ref)

    first = _first_half()
    shift = dec_seq.bit_length() - 1

    kn_t = kn_ref[...].T
    vs_t = vs_ref[...].T
    knt_ref[...] = kn_t.astype(BF16)
    vst_ref[...] = vs_t.astype(BF16)
    for bb in range(DEC_GROUP):
        sh = (WINDOW - dec_seq - bb * dec_seq) % LANES
        knew_ref[bb] = pltpu.roll(kn_t, sh, axis=1) if sh else kn_t
        vnew_ref[bb] = pltpu.roll(vs_t, sh, axis=1) if sh else vs_t
    qn = qn_ref[...]
    qn_sw = pltpu.roll(qn, HEAD_DIM, axis=1)
    for h in range(N_SWA_HEADS):
        kv_half = h // SWA_GROUP
        if (h % 2) == kv_half:
            src = qn[:, (h // 2) * LANES:(h // 2 + 1) * LANES]
        else:
            col = (h + 1) // 2
            src = qn_sw[:, col * LANES:(col + 1) * LANES]
        qbd_ref[h] = jnp.where(first, src, 0.0) if kv_half == 0 else jnp.where(first, 0.0, src)

    sink_rows = jnp.concatenate(
        [jnp.full((dec_seq, LANES), sinks_ref[h], F32) for h in range(N_SWA_HEADS)], axis=0)
    col_batch = _iota((N_SWA_HEADS * dec_seq, LANES), 1) >> shift
    keep_old = _iota((1, LANES), 1) < WINDOW - dec_seq

    def per_batches(i, carry):
        bs = [i * DEC_UNROLL + u for u in range(DEC_UNROLL)]
        rows = [pl.ds(pl.multiple_of(b * dec_seq, dec_seq), dec_seq) for b in bs]
        k_old = [kt_ref[b] for b in bs]
        v_old = [vt_ref[b] for b in bs]
        q_st = [jnp.concatenate([qbd_ref[h, r, :] for h in range(N_SWA_HEADS)], axis=0).astype(BF16)
                for r in rows]
        s = [_dot(q_st[u], jnp.concatenate([k_old[u].astype(BF16), knt_ref[...]], axis=1))
             for u in range(DEC_UNROLL)]
        o = []
        for u, b in enumerate(bs):
            bias = jnp.concatenate([bias_ref[0], jnp.where(col_batch == b, bias_ref[1], NEG_INF)], axis=1)
            w_v = jnp.concatenate([v_old[u].astype(BF16), vst_ref[...]], axis=1)
            o.append(_softmax_sink_pv(s[u] * K_SCALE + bias, sink_rows, w_v))
        for u, b in enumerate(bs):
            for h in range(N_SWA_HEADS):
                oblk_ref[h, rows[u], :] = o[u][h * dec_seq:(h + 1) * dec_seq]
            kt_out_ref[b] = jnp.where(keep_old, pltpu.roll(k_old[u], LANES - dec_seq, axis=1), knew_ref[b])
            vt_out_ref[b] = jnp.where(keep_old, pltpu.roll(v_old[u], LANES - dec_seq, axis=1), vnew_ref[b])
        return carry

    lax.fori_loop(0, DEC_GROUP // DEC_UNROLL, per_batches, 0)

    y1 = jnp.where(first, oblk_ref[3], oblk_ref[4])
    moved = pltpu.roll(jnp.concatenate([oblk_ref[1], y1, oblk_ref[6], oblk_ref[6]], axis=1), HEAD_DIM, axis=1)
    outs = [
        jnp.where(first, oblk_ref[0], moved[:, 0:LANES]),
        jnp.where(first, oblk_ref[2], moved[:, LANES:2 * LANES]),
        jnp.where(first, moved[:, 2 * LANES:3 * LANES], oblk_ref[5]),
        jnp.where(first, moved[:, 3 * LANES:4 * LANES], oblk_ref[7]),
    ]
    for c in range(SWA_WIDTH // LANES):
        mix_ref[:, c * LANES:(c + 1) * LANES] = outs[c].astype(BF16)


def _decode_attn(sinks, qn, kn, vs, k_t, v_t, dec_seq):
    nb = k_t.shape[0]
    assert DEC_GROUP * dec_seq == DEC_ROWS and nb % DEC_GROUP == 0 and dec_seq & (dec_seq - 1) == 0
    assert k_t.shape[1:] == (SWA_KV_WIDTH, WINDOW)
    row = lambda w: pl.BlockSpec((DEC_ROWS, w), lambda i: (i, 0))
    cache = pl.BlockSpec((DEC_GROUP, SWA_KV_WIDTH, WINDOW), lambda i: (i, 0, 0))
    return pl.pallas_call(
        functools.partial(_decode_attn_kernel, dec_seq),
        grid=(nb // DEC_GROUP,),
        in_specs=[pl.BlockSpec(memory_space=pltpu.SMEM),
                  row(SWA_WIDTH), row(SWA_KV_WIDTH), row(SWA_KV_WIDTH), cache, cache],
        out_specs=[row(SWA_WIDTH), cache, cache],
        out_shape=[jax.ShapeDtypeStruct((nb * dec_seq, SWA_WIDTH), BF16),
                   jax.ShapeDtypeStruct(k_t.shape, F32),
                   jax.ShapeDtypeStruct(v_t.shape, F32)],
        scratch_shapes=[
            pltpu.VMEM((2, N_SWA_HEADS * dec_seq, WINDOW), F32),
            pltpu.VMEM((N_SWA_HEADS, DEC_ROWS, LANES), F32),
            pltpu.VMEM((N_SWA_HEADS, DEC_ROWS, LANES), F32),
            pltpu.VMEM((DEC_GROUP, SWA_KV_WIDTH, LANES), F32),
            pltpu.VMEM((DEC_GROUP, SWA_KV_WIDTH, LANES), F32),
            pltpu.VMEM((SWA_KV_WIDTH, DEC_ROWS), BF16),
            pltpu.VMEM((SWA_KV_WIDTH, DEC_ROWS), BF16),
        ],
        compiler_params=pltpu.CompilerParams(
            dimension_semantics=("arbitrary",), vmem_limit_bytes=VMEM_LIMIT),
        name="decode_attn",
    )(sinks, qn, kn, vs, k_t, v_t)


def kernel(x_prompt, x_sample, state_ret, cache_swa_k, cache_swa_v, norm_mix_gain, w_in, q_norm_gain,
           k_norm_gain, attn_sinks, w_out, norm_ffn_gain, w_up, w_down):
    batch, seq, d = x_prompt.shape
    nb, dec_seq, _ = x_sample.shape
    wb = cache_swa_k.shape[1]
    assert d == D_MODEL and seq % PROMPT_TILE == 0 and wb == WINDOW

    gain_mix = norm_mix_gain.reshape(1, D_MODEL)
    gain_ffn = norm_ffn_gain.reshape(1, D_MODEL)
    qg = jnp.tile(q_norm_gain, 2).reshape(1, LANES)
    kg = jnp.tile(k_norm_gain, 2).reshape(1, LANES)

    def from_key_minor(a_t):
        return jnp.transpose(a_t.reshape(a_t.shape[0], N_SWA_KV, HEAD_DIM, WINDOW), (0, 3, 1, 2))

    def to_key_minor(a):
        return jnp.transpose(a, (0, 2, 3, 1)).reshape(a.shape[0], SWA_KV_WIDTH, WINDOW)

    xs = x_sample.reshape(nb * dec_seq, D_MODEL)
    main_s, qn_s, kn_s, vs_s, w_in_bf = _in_proj_cast(xs, gain_mix, w_in, qg, kg)
    mix_ret_s, state_t = _decode_ret(main_s, jnp.transpose(state_ret, (1, 2, 3, 0)), dec_seq)
    mix_swa_s, k_t, v_t = _decode_attn(attn_sinks, qn_s, kn_s, vs_s,
                                       to_key_minor(cache_swa_k), to_key_minor(cache_swa_v), dec_seq)
    y_s, w_out_bf, w_up_bf, w_down_bf = _out_mlp_cast(mix_ret_s, mix_swa_s, xs, gain_ffn, w_out, w_up, w_down)

    xp = x_prompt.reshape(batch * seq, D_MODEL)
    y_p, ret_p, kwin_t, vwin_t = _prompt_layer(attn_sinks, xp, gain_mix, w_in_bf, qg, kg, w_out_bf, gain_ffn,
                                               w_up_bf, w_down_bf, batch, seq)
    y_p = y_p.reshape(batch, seq, D_MODEL)

    return (y_p, y_s.reshape(nb, dec_seq, D_MODEL), ret_p, from_key_minor(kwin_t), from_key_minor(vwin_t),
            jnp.transpose(state_t, (3, 0, 1, 2)), from_key_minor(k_t), from_key_minor(v_t))
```

```python
import functools
import math

import jax
import jax.numpy as jnp
from jax import lax
from jax.experimental import pallas as pl
from jax.experimental.pallas import tpu as pltpu

F32 = jnp.float32
BF16 = jnp.bfloat16

D_MODEL = 1024
HEAD_DIM = 64
N_RET_HEADS = 8
N_SWA_HEADS = 8
N_SWA_KV = 2
SWA_GROUP = N_SWA_HEADS // N_SWA_KV
RET_WIDTH = N_RET_HEADS * HEAD_DIM
SWA_WIDTH = N_SWA_HEADS * HEAD_DIM
SWA_KV_WIDTH = N_SWA_KV * HEAD_DIM
MAIN_WIDTH = 4 * RET_WIDTH
IN_WIDTH = MAIN_WIDTH + SWA_WIDTH + 2 * SWA_KV_WIDTH
MIX_WIDTH = RET_WIDTH + SWA_WIDTH
D_FF = 4 * D_MODEL
WINDOW = 128
RET_CHUNK = 128
EPS = 1e-6
NEG_INF = -1e30

LANES = 128
N_PAIRS = N_RET_HEADS // 2
LOG_DECAY = [math.log(1.0 - 2.0 ** (-5.0 - h)) for h in range(N_RET_HEADS)]
ALIBI_SLOPES = [2.0 ** (-8.0 * (h + 1) / N_SWA_HEADS) for h in range(N_SWA_HEADS)]
K_SCALE = HEAD_DIM ** -0.5

PROMPT_TILE = 512
DEC_GROUP = 16
VMEM_LIMIT = 56 * 1024 * 1024
PROMPT_VMEM_LIMIT = 62 * 1024 * 1024


def _dot(a, b):
    return jnp.dot(a, b, preferred_element_type=F32)


def _dot_nt(a, b):
    return lax.dot_general(a, b, (((1,), (1,)), ((), ())), preferred_element_type=F32)


def _iota(shape, dim):
    return lax.broadcasted_iota(jnp.int32, shape, dim)


def _ones_block_diag():
    same = ((_iota((2 * LANES, LANES), 0) >> 6) & 1) == (_iota((2 * LANES, LANES), 1) >> 6)
    return jnp.where(same, 1.0, 0.0).astype(BF16)


def _head_sumsq(x, ones_bd):
    x2 = x * x
    hi = x2.astype(BF16)
    lo = (x2 - hi.astype(F32)).astype(BF16)
    return _dot(jnp.concatenate([hi, lo], axis=1), ones_bd)


def _head_rms_scale(x, ones_bd):
    return lax.rsqrt(_head_sumsq(x, ones_bd) * (1.0 / HEAD_DIM) + EPS)


def _head_rms_scales(xs, ones_bd):
    if not xs:
        return []
    parts = []
    for x in xs:
        x2 = x * x
        hi = x2.astype(BF16)
        parts.append(jnp.concatenate([hi, (x2 - hi.astype(F32)).astype(BF16)], axis=1))
    total = _dot(jnp.concatenate(parts, axis=0) if len(parts) > 1 else parts[0], ones_bd)
    out, lo = [], 0
    for x in xs:
        out.append(lax.rsqrt(total[lo:lo + x.shape[0]] * (1.0 / HEAD_DIM) + EPS))
        lo += x.shape[0]
    return out


def _first_half():
    return _iota((1, LANES), 1) < HEAD_DIM


def _pair_const(values, pair, lane_is_second):
    return jnp.where(lane_is_second, values[2 * pair + 1], values[2 * pair]).astype(F32)


def _silu(g):
    return g * (1.0 / (1.0 + jnp.exp(-g)))


def _in_proj_kernel(x_ref, gain_ref, w_ref, qg_ref, kg_ref, main_ref, qn_ref, kn_ref, vs_ref, fetch=None):
    x = x_ref[...]
    ms = jnp.mean(x * x, axis=-1, keepdims=True)
    hb = ((x * lax.rsqrt(ms + EPS)) * gain_ref[...]).astype(BF16)
    ones_bd = _ones_block_diag()

    def project(lo, width):
        if fetch is not None:
            fetch(lo, width)
        return _dot(hb, w_ref[:, lo:lo + width])

    for c in range(MAIN_WIDTH // 512):
        main_ref[:, c * 512:(c + 1) * 512] = project(c * 512, 512)
    qs = project(MAIN_WIDTH, SWA_WIDTH)
    k0 = MAIN_WIDTH + SWA_WIDTH
    ks = project(k0, SWA_KV_WIDTH)
    vs_ref[...] = project(k0 + SWA_KV_WIDTH, SWA_KV_WIDTH)
    q_cols = [qs[:, c * LANES:(c + 1) * LANES] for c in range(SWA_WIDTH // LANES)]
    scales = _head_rms_scales(q_cols + [ks], ones_bd)
    for c, xc in enumerate(q_cols):
        qn_ref[:, c * LANES:(c + 1) * LANES] = (xc * scales[c]) * qg_ref[...]
    kn_ref[...] = (ks * scales[-1]) * kg_ref[...]


ROW_TILE = 512


W_IN_BLOCKS = [(c * 512, 512) for c in range(MAIN_WIDTH // 512)] + [(MAIN_WIDTH, SWA_WIDTH),
                                                                    (MAIN_WIDTH + SWA_WIDTH, 2 * SWA_KV_WIDTH)]
assert sum(w for _, w in W_IN_BLOCKS) == IN_WIDTH


def _in_proj_cast_kernel(x_ref, gain_ref, w_hbm, qg_ref, kg_ref,
                         main_ref, qn_ref, kn_ref, vs_ref, w_bf_hbm,
                         w_v, stage, in_sem, out_sem):
    step = pl.program_id(0)
    block_in = [pltpu.make_async_copy(w_hbm.at[:, pl.ds(lo, width)], stage.at[:, pl.ds(lo, width)], in_sem.at[i])
                for i, (lo, width) in enumerate(W_IN_BLOCKS)]
    writeback = pltpu.make_async_copy(w_v, w_bf_hbm, out_sem.at[0])

    @pl.when(step == 0)
    def _():
        for copy in block_in:
            copy.start()
        arrived = set()

        def fetch(lo, width):
            for i, (blo, bwidth) in enumerate(W_IN_BLOCKS):
                if blo <= lo < blo + bwidth and i not in arrived:
                    assert lo + width <= blo + bwidth
                    block_in[i].wait()
                    w_v[:, blo:blo + bwidth] = stage[:, blo:blo + bwidth].astype(BF16)
                    arrived.add(i)

        _in_proj_kernel(x_ref, gain_ref, w_v, qg_ref, kg_ref, main_ref, qn_ref, kn_ref, vs_ref, fetch=fetch)
        assert len(arrived) == len(W_IN_BLOCKS)
        writeback.start()

    @pl.when(step > 0)
    def _():
        _in_proj_kernel(x_ref, gain_ref, w_v, qg_ref, kg_ref, main_ref, qn_ref, kn_ref, vs_ref)

    @pl.when(step == pl.num_programs(0) - 1)
    def _():
        writeback.wait()


def _in_proj_cast(x2d, gain, w_in, qg, kg):
    m = x2d.shape[0]
    tm = min(ROW_TILE, m)
    row = lambda w: pl.BlockSpec((tm, w), lambda i: (i, 0))
    full = lambda a: pl.BlockSpec(a.shape, lambda i: (0, 0), pipeline_mode=pl.Buffered(1))
    hbm = pl.BlockSpec(memory_space=pl.ANY)
    return pl.pallas_call(
        _in_proj_cast_kernel,
        grid=(m // tm,),
        in_specs=[row(D_MODEL), full(gain), hbm, full(qg), full(kg)],
        out_specs=[row(MAIN_WIDTH), row(SWA_WIDTH), row(SWA_KV_WIDTH), row(SWA_KV_WIDTH), hbm],
        out_shape=[jax.ShapeDtypeStruct((m, MAIN_WIDTH), F32),
                   jax.ShapeDtypeStruct((m, SWA_WIDTH), F32),
                   jax.ShapeDtypeStruct((m, SWA_KV_WIDTH), F32),
                   jax.ShapeDtypeStruct((m, SWA_KV_WIDTH), F32),
                   jax.ShapeDtypeStruct(w_in.shape, BF16)],
        scratch_shapes=[pltpu.VMEM(w_in.shape, BF16), pltpu.VMEM(w_in.shape, F32),
                        pltpu.SemaphoreType.DMA((len(W_IN_BLOCKS),)), pltpu.SemaphoreType.DMA((1,))],
        compiler_params=pltpu.CompilerParams(
            dimension_semantics=("arbitrary",), vmem_limit_bytes=VMEM_LIMIT),
        name="in_proj_cast",
    )(x2d, gain, w_in, qg, kg)


FF_CHUNK = 1024


def _out_mlp_kernel(mix_ret_ref, mix_swa_ref, x_ref, w_out_ref, gain_ref, w_up_ref, w_down_ref, y_ref):
    h = x_ref[...] + (_dot(mix_ret_ref[...].astype(BF16), w_out_ref[:RET_WIDTH, :])
                      + _dot(mix_swa_ref[...].astype(BF16), w_out_ref[RET_WIDTH:, :]))
    ms = jnp.mean(h * h, axis=-1, keepdims=True)
    hf = ((h * lax.rsqrt(ms + EPS)) * gain_ref[...]).astype(BF16)
    ff = None
    for c in range(D_FF // FF_CHUNK):
        u = _dot(hf, w_up_ref[:, c * FF_CHUNK:(c + 1) * FF_CHUNK])
        a = jnp.maximum(u, 0.0)
        d = _dot((a * a).astype(BF16), w_down_ref[c * FF_CHUNK:(c + 1) * FF_CHUNK, :])
        ff = d if ff is None else ff + d
    y_ref[...] = h + ff


CAST_CHUNK = 512
N_FF_CHUNKS = D_FF // CAST_CHUNK


def _out_mlp_cast_kernel(mix_ret_ref, mix_swa_ref, x_ref, gain_ref, w_out_hbm, w_up_hbm, w_down_hbm,
                         y_ref, w_out_bf_hbm, w_up_bf_hbm, w_down_bf_hbm,
                         w_out_v, w_up_v, w_down_v, stage_out, stage_up, stage_down, in_sem, out_sem):
    step = pl.program_id(0)

    def out_in(i):
        return pltpu.make_async_copy(w_out_hbm.at[pl.ds(i * CAST_CHUNK, CAST_CHUNK), :], stage_out.at[i],
                                     in_sem.at[0, i])

    def up_in(c):
        return pltpu.make_async_copy(w_up_hbm.at[:, pl.ds(c * CAST_CHUNK, CAST_CHUNK)], stage_up.at[c % 2],
                                     in_sem.at[1, c % 2])

    def down_in(c):
        return pltpu.make_async_copy(w_down_hbm.at[pl.ds(c * CAST_CHUNK, CAST_CHUNK), :], stage_down.at[c % 2],
                                     in_sem.at[2, c % 2])

    writebacks = [pltpu.make_async_copy(src, dst, out_sem.at[i]) for i, (src, dst) in enumerate(
        [(w_out_v, w_out_bf_hbm), (w_up_v, w_up_bf_hbm), (w_down_v, w_down_bf_hbm)])]

    @pl.when(step == 0)
    def _():
        n_out = D_MODEL // CAST_CHUNK
        for i in range(n_out):
            out_in(i).start()
        up_in(0).start()
        down_in(0).start()
        for i in range(n_out):
            out_in(i).wait()
            w_out_v[i * CAST_CHUNK:(i + 1) * CAST_CHUNK, :] = stage_out[i].astype(BF16)
        writebacks[0].start()
        h = x_ref[...] + (_dot(mix_ret_ref[...].astype(BF16), w_out_v[:RET_WIDTH, :])
                          + _dot(mix_swa_ref[...].astype(BF16), w_out_v[RET_WIDTH:, :]))
        ms = jnp.mean(h * h, axis=-1, keepdims=True)
        hf = ((h * lax.rsqrt(ms + EPS)) * gain_ref[...]).astype(BF16)
        ff = None
        for c in range(N_FF_CHUNKS):
            chunk = slice(c * CAST_CHUNK, (c + 1) * CAST_CHUNK)
            if c + 1 < N_FF_CHUNKS:
                up_in(c + 1).start()
                down_in(c + 1).start()
            up_in(c).wait()
            down_in(c).wait()
            w_up_v[:, chunk] = stage_up[c % 2].astype(BF16)
            w_down_v[chunk, :] = stage_down[c % 2].astype(BF16)
            a = jnp.maximum(_dot(hf, w_up_v[:, chunk]), 0.0)
            d = _dot((a * a).astype(BF16), w_down_v[chunk, :])
            ff = d if ff is None else ff + d
        writebacks[1].start()
        writebacks[2].start()
        y_ref[...] = h + ff

    @pl.when(step > 0)
    def _():
        _out_mlp_kernel(mix_ret_ref, mix_swa_ref, x_ref, w_out_v, gain_ref, w_up_v, w_down_v, y_ref)

    @pl.when(step == pl.num_programs(0) - 1)
    def _():
        for wb in writebacks:
            wb.wait()


def _out_mlp_cast(mix_ret, mix_swa, x2d, gain, w_out, w_up, w_down):
    m = x2d.shape[0]
    tm = min(ROW_TILE, m)
    assert D_MODEL % CAST_CHUNK == 0 and D_FF % CAST_CHUNK == 0
    row = lambda w: pl.BlockSpec((tm, w), lambda i: (i, 0))
    full = lambda a: pl.BlockSpec(a.shape, lambda i: (0, 0), pipeline_mode=pl.Buffered(1))
    hbm = pl.BlockSpec(memory_space=pl.ANY)
    return pl.pallas_call(
        _out_mlp_cast_kernel,
        grid=(m // tm,),
        in_specs=[row(RET_WIDTH), row(SWA_WIDTH), row(D_MODEL), full(gain), hbm, hbm, hbm],
        out_specs=[row(D_MODEL), hbm, hbm, hbm],
        out_shape=[jax.ShapeDtypeStruct((m, D_MODEL), F32),
                   jax.ShapeDtypeStruct(w_out.shape, BF16),
                   jax.ShapeDtypeStruct(w_up.shape, BF16),
                   jax.ShapeDtypeStruct(w_down.shape, BF16)],
        scratch_shapes=[
            pltpu.VMEM(w_out.shape, BF16), pltpu.VMEM(w_up.shape, BF16), pltpu.VMEM(w_down.shape, BF16),
            pltpu.VMEM((D_MODEL // CAST_CHUNK, CAST_CHUNK, D_MODEL), F32),
            pltpu.VMEM((2, D_MODEL, CAST_CHUNK), F32),
            pltpu.VMEM((2, CAST_CHUNK, D_MODEL), F32),
            pltpu.SemaphoreType.DMA((3, 2)),
            pltpu.SemaphoreType.DMA((3,)),
        ],
        compiler_params=pltpu.CompilerParams(
            dimension_semantics=("arbitrary",), vmem_limit_bytes=VMEM_LIMIT),
        name="out_mlp_cast",
    )(mix_ret, mix_swa, x2d, gain, w_out, w_up, w_down)


def _split_pair_rows(x, first):
    return jnp.concatenate([jnp.where(first, x, 0.0), jnp.where(first, 0.0, x)], axis=0).astype(BF16)


def _softmax_sink_pv(s, sink_wide, v_t):
    m = jnp.maximum(jnp.max(s, axis=-1, keepdims=True), sink_wide)
    p = jnp.exp(s - jnp.concatenate([m, m], axis=1))
    denom = jnp.sum(p, axis=-1, keepdims=True) + jnp.exp(sink_wide - m)
    return _dot_nt(p.astype(BF16), v_t) / denom


def _softmax_sink_pv_t(s_t, sink_lanes, v_t):
    m = jnp.maximum(jnp.max(s_t, axis=0, keepdims=True), sink_lanes)
    p = jnp.exp(s_t - m)
    denom = jnp.sum(p, axis=0, keepdims=True) + jnp.exp(sink_lanes - m)
    return _dot(v_t, p.astype(BF16)) / denom


def _prompt_consts(intra_ref, qdec_ref, kdec_ref, sdec_ref, bias_ref):
    r = _iota((LANES, LANES), 0)
    lane2 = _iota((LANES, LANES), 1) >= HEAD_DIM
    rf = r.astype(F32)
    ri = _iota((LANES, 2 * LANES), 0)
    ci = _iota((LANES, 2 * LANES), 1)
    diff = (ri - (ci & (LANES - 1))).astype(F32)
    for p in range(N_PAIRS):
        lg = _pair_const(LOG_DECAY, p, lane2)
        qdec_ref[p] = jnp.exp(lg * (rf + 1.0))
        kdec_ref[p] = jnp.exp(lg * (RET_CHUNK - 1.0 - rf))
        sdec_ref[p] = jnp.exp(_pair_const(LOG_DECAY, p, r >= HEAD_DIM) * float(RET_CHUNK))
        lg2 = _pair_const(LOG_DECAY, p, ci >= LANES)
        intra_ref[p] = jnp.where(diff >= 0.0, jnp.exp(lg2 * jnp.maximum(diff, 0.0)), 0.0)
    cols = SWA_GROUP * WINDOW
    kb = _iota((2 * WINDOW, cols), 0)
    cb = _iota((2 * WINDOW, cols), 1)
    grp = cb >> 7
    dist = WINDOW + (cb & (WINDOW - 1)) - kb
    valid = (dist >= 0) & (dist < WINDOW)
    distf = dist.astype(F32)
    for j in range(N_SWA_KV):
        sl = [ALIBI_SLOPES[SWA_GROUP * j + g] for g in range(SWA_GROUP)]
        slope = jnp.where(grp == 0, sl[0], jnp.where(grp == 1, sl[1], jnp.where(grp == 2, sl[2], sl[3])))
        b = jnp.where(valid, -(slope.astype(F32) * distf), NEG_INF)
        bias_ref[0, j] = b
        bias_ref[1, j] = jnp.where(kb >= WINDOW, b, NEG_INF)


def _prompt_layer_kernel(sinks_ref, x_ref, x_next_ref, gain_mix_ref, w_in_ref, qg_ref, kg_ref,
                         w_out_hbm, gain_ffn_ref, w_up_hbm, w_down_hbm,
                         y_ref, ret_ref, kwin_ref, vwin_ref,
                         main_ref, qn_ref, kn_ref, vs_ref, mix_ref, hb_ref,
                         state_ref, prevk_ref, prevv_ref,
                         intra_ref, qdec_ref, kdec_ref, sdec_ref, bias_ref,
                         w_out_ref, w_up_ref, w_down_ref, w_sem):
    t = pl.program_id(1)
    step = pl.program_id(0) * pl.num_programs(1) + t
    cur = step % 2
    nxt = 1 - cur

    late_weights = [pltpu.make_async_copy(src, dst, w_sem.at[i]) for i, (src, dst) in enumerate(
        [(w_out_hbm, w_out_ref), (w_up_hbm, w_up_ref), (w_down_hbm, w_down_ref)])]

    @pl.when(step == 0)
    def _():
        for copy in late_weights:
            copy.start()
        _prompt_consts(intra_ref, qdec_ref, kdec_ref, sdec_ref, bias_ref)
        _in_proj_kernel(x_ref, gain_mix_ref, w_in_ref, qg_ref, kg_ref,
                        main_ref.at[0], qn_ref.at[0], kn_ref.at[0], vs_ref.at[0])

    @pl.when(t == 0)
    def _():
        state_ref[...] = jnp.zeros_like(state_ref)
        prevk_ref[...] = jnp.zeros_like(prevk_ref)
        prevv_ref[...] = jnp.zeros_like(prevv_ref)

    first = _first_half()
    ones_bd = _ones_block_diag()
    bd_mask = (_iota((LANES, LANES), 0) >= HEAD_DIM) == (_iota((LANES, LANES), 1) >= HEAD_DIM)

    n_chunks = PROMPT_TILE // RET_CHUNK
    assert n_chunks == 4
    plan = [dict(a=[], b=[]),
            dict(a=["m0", "m1"], b=[]),
            dict(a=["m2", "m3"], b=[]),
            dict(a=["v", "k"], b=["q0", "q1"])]
    a0 = MAIN_WIDTH
    attn_cols = {"v": (a0 + SWA_WIDTH + SWA_KV_WIDTH, SWA_KV_WIDTH), "k": (a0 + SWA_WIDTH, SWA_KV_WIDTH)}
    attn_cols.update({"q%d" % i: (a0 + 256 * i, 256) for i in range(SWA_WIDTH // 256)})
    cols = lambda base, p: slice(base + p * LANES, base + (p + 1) * LANES)

    def project(items):
        raw = {}
        for it in items:
            if it[0] == "m":
                cb = int(it[1:])
                main_ref[nxt, :, cb * 512:(cb + 1) * 512] = _dot(hb_ref[...], w_in_ref[:, cb * 512:(cb + 1) * 512])
            else:
                lo, width = attn_cols[it]
                raw[it] = _dot(hb_ref[...], w_in_ref[:, lo:lo + width])
        return raw

    def attn_scales(raw):
        return {name: [_head_rms_scale(val[:, cols(0, cq)], ones_bd) for cq in range(val.shape[1] // LANES)]
                for name, val in raw.items() if name != "v"}

    def store_attn(raw, scales):
        for name, val in raw.items():
            if name == "v":
                vs_ref[nxt] = val
            elif name == "k":
                kn_ref[nxt] = (val * scales[name][0]) * kg_ref[...]
            else:
                base = attn_cols[name][0] - a0
                for cq in range(val.shape[1] // LANES):
                    qn_ref[nxt, :, cols(base, cq)] = (val[:, cols(0, cq)] * scales[name][cq]) * qg_ref[...]

    def chunk(c):
        rows = slice(c * RET_CHUNK, (c + 1) * RET_CHUNK)
        pairs = range(N_PAIRS)
        kvs = range(N_SWA_KV)
        if c == 0:
            xn = x_next_ref[...]
            hb_ref[...] = ((xn * lax.rsqrt(jnp.mean(xn * xn, axis=-1, keepdims=True) + EPS))
                           * gain_mix_ref[...]).astype(BF16)

        q = [main_ref[cur, rows, cols(0, p)] for p in pairs]
        k = [main_ref[cur, rows, cols(RET_WIDTH, p)] * K_SCALE for p in pairs]
        v = [main_ref[cur, rows, cols(2 * RET_WIDTH, p)] for p in pairs]
        state = [state_ref[p] for p in pairs]
        kc = kn_ref[cur, rows, :]
        k_sw = pltpu.roll(kc, HEAD_DIM, axis=1)
        v_t = vs_ref[cur, rows, :].T
        is_first = (t == 0).astype(jnp.int32) if c == 0 else 0
        k_dup =[(jnp.where(first, kc, k_sw) if j == 0 else jnp.where(first, k_sw, kc)).astype(BF16)
                 for j in kvs]
        v_tj = [v_t[j * HEAD_DIM:(j + 1) * HEAD_DIM].astype(BF16) for j in kvs]
        q_st = []
        for j in kvs:
            pieces = []
            for g in range(SWA_GROUP):
                qc = qn_ref[cur, rows, cols(0, 2 * j + g // 2)]
                pieces.append(jnp.where(first, qc, 0.0) if g % 2 == 0 else jnp.where(first, 0.0, qc))
            q_st.append(jnp.concatenate(pieces, axis=0).astype(BF16))

        s = [_dot_nt(q[p].astype(BF16), _split_pair_rows(k[p], first)) for p in pairs]
        s_t = [_dot_nt(jnp.concatenate([prevk_ref[j], k_dup[j]], axis=0), q_st[j]) for j in kvs]
        cross = [_dot((q[p] * qdec_ref[p]).astype(BF16), state[p].astype(BF16)) for p in pairs]
        upd = [_dot((k[p] * kdec_ref[p]).T.astype(BF16), v[p].astype(BF16)) for p in pairs]
        raw = project(plan[c]["a"])

        o = [_dot((s[p] * intra_ref[p]).astype(BF16), _split_pair_rows(v[p], first)) + cross[p] for p in pairs]
        o_t = []
        for j in kvs:
            sink_lanes = jnp.concatenate(
                [jnp.full((1, WINDOW), sinks_ref[SWA_GROUP * j + g], F32) for g in range(SWA_GROUP)],
                axis=1)
            v_cat = jnp.concatenate([prevv_ref[j], v_tj[j]], axis=1)
            o_t.append(_softmax_sink_pv_t(s_t[j] * K_SCALE + bias_ref[is_first, j], sink_lanes, v_cat))
        for p in pairs:
            state_ref[p] = state[p] * sdec_ref[p] + jnp.where(bd_mask, upd[p], 0.0)
        for j in kvs:
            prevk_ref[j] = k_dup[j]
            prevv_ref[j] = v_tj[j]
        raw.update(project(plan[c]["b"]))

        scale = [_head_rms_scale(o[p], ones_bd) for p in pairs]
        raw_scales = attn_scales(raw)
        for p in pairs:
            g = main_ref[cur, rows, cols(3 * RET_WIDTH, p)]
            mix_ref[rows, cols(0, p)] = (o[p] * scale[p] * _silu(g)).astype(BF16)
        store_attn(raw, raw_scales)
        for j in kvs:
            for half in range(2):
                pair_t = jnp.concatenate([o_t[j][:, (2 * half) * WINDOW:(2 * half + 1) * WINDOW],
                                          o_t[j][:, (2 * half + 1) * WINDOW:(2 * half + 2) * WINDOW]], axis=0)
                mix_ref[rows, cols(RET_WIDTH, 2 * j + half)] = pair_t.T.astype(BF16)

    for c in range(n_chunks):
        chunk(c)

    @pl.when(t == pl.num_programs(1) - 1)
    def _():
        for p in range(N_PAIRS):
            s = state_ref[p]
            ret_ref[2 * p] = s[:HEAD_DIM, :HEAD_DIM]
            ret_ref[2 * p + 1] = s[HEAD_DIM:, HEAD_DIM:]
        last = slice(PROMPT_TILE - WINDOW, PROMPT_TILE)
        kwin_ref[...] = kn_ref[cur, last, :].T
        vwin_ref[...] = vs_ref[cur, last, :].T

    @pl.when(step == 0)
    def _():
        for copy in late_weights:
            copy.wait()

    _out_mlp_kernel(mix_ref.at[:, pl.ds(0, RET_WIDTH)], mix_ref.at[:, pl.ds(RET_WIDTH, SWA_WIDTH)], x_ref,
                    w_out_ref, gain_ffn_ref, w_up_ref, w_down_ref, y_ref)


def _prompt_layer(sinks, x2d, gain_mix, w_in_bf, qg, kg, w_out_bf, gain_ffn, w_up_bf, w_down_bf, batch, seq):
    nt = seq // PROMPT_TILE
    last_tile = batch * nt - 1
    row = lambda w: pl.BlockSpec((PROMPT_TILE, w), lambda b, t: (b * nt + t, 0))
    next_row = pl.BlockSpec((PROMPT_TILE, D_MODEL), lambda b, t: (jnp.minimum(b * nt + t + 1, last_tile), 0))
    full = lambda a: pl.BlockSpec(a.shape, lambda b, t: (0, 0), pipeline_mode=pl.Buffered(1))
    in_hbm = pl.BlockSpec(memory_space=pl.ANY)
    return pl.pallas_call(
        _prompt_layer_kernel,
        grid=(batch, nt),
        in_specs=[pl.BlockSpec(memory_space=pltpu.SMEM), row(D_MODEL), next_row,
                  full(gain_mix), full(w_in_bf), full(qg), full(kg),
                  in_hbm, full(gain_ffn), in_hbm, in_hbm],
        out_specs=[row(D_MODEL),
                   pl.BlockSpec((None, N_RET_HEADS, HEAD_DIM, HEAD_DIM), lambda b, t: (b, 0, 0, 0)),
                   pl.BlockSpec((None, SWA_KV_WIDTH, WINDOW), lambda b, t: (b, 0, 0)),
                   pl.BlockSpec((None, SWA_KV_WIDTH, WINDOW), lambda b, t: (b, 0, 0))],
        out_shape=[jax.ShapeDtypeStruct((batch * seq, D_MODEL), F32),
                   jax.ShapeDtypeStruct((batch, N_RET_HEADS, HEAD_DIM, HEAD_DIM), F32),
                   jax.ShapeDtypeStruct((batch, SWA_KV_WIDTH, WINDOW), F32),
                   jax.ShapeDtypeStruct((batch, SWA_KV_WIDTH, WINDOW), F32)],
        scratch_shapes=[
            pltpu.VMEM((2, PROMPT_TILE, MAIN_WIDTH), F32),
            pltpu.VMEM((2, PROMPT_TILE, SWA_WIDTH), F32),
            pltpu.VMEM((2, PROMPT_TILE, SWA_KV_WIDTH), F32),
            pltpu.VMEM((2, PROMPT_TILE, SWA_KV_WIDTH), F32),
            pltpu.VMEM((PROMPT_TILE, MIX_WIDTH), BF16),
            pltpu.VMEM((PROMPT_TILE, D_MODEL), BF16),
            pltpu.VMEM((N_PAIRS, LANES, LANES), F32),
            pltpu.VMEM((N_SWA_KV, WINDOW, LANES), BF16),
            pltpu.VMEM((N_SWA_KV, HEAD_DIM, WINDOW), BF16),
            pltpu.VMEM((N_PAIRS, LANES, 2 * LANES), F32),
            pltpu.VMEM((N_PAIRS, LANES, LANES), F32),
            pltpu.VMEM((N_PAIRS, LANES, LANES), F32),
            pltpu.VMEM((N_PAIRS, LANES, LANES), F32),
            pltpu.VMEM((2, N_SWA_KV, 2 * WINDOW, SWA_GROUP * WINDOW), F32),
            pltpu.VMEM(w_out_bf.shape, BF16),
            pltpu.VMEM(w_up_bf.shape, BF16),
            pltpu.VMEM(w_down_bf.shape, BF16),
            pltpu.SemaphoreType.DMA((3,)),
        ],
        compiler_params=pltpu.CompilerParams(
            dimension_semantics=("arbitrary", "arbitrary"), vmem_limit_bytes=PROMPT_VMEM_LIMIT),
        name="prompt_layer",
    )(sinks, x2d, x2d, gain_mix, w_in_bf, qg, kg, w_out_bf, gain_ffn, w_up_bf, w_down_bf)


def _decode_ret_kernel(dec_seq, nb, qdec_ref, kdec_ref, sdec_ref, intra_ref,
                       q_ref, k_ref, v_ref, g_ref, st_ref,
                       mix_ref, st_out_ref,
                       qt_ref, kt_ref, vt_ref, qdt_ref, kdt_ref, o_ref):
    pair = pl.program_id(0)
    halves = [slice(0, HEAD_DIM), slice(HEAD_DIM, 2 * HEAD_DIM)]
    for l in range(dec_seq):
        rows = pl.ds(l, nb, stride=dec_seq)
        q_t = q_ref[rows, :].T
        k_t = (k_ref[rows, :] * K_SCALE).T
        qt_ref[l] = q_t
        kt_ref[l] = k_t
        vt_ref[l] = v_ref[rows, :].T
        for hh in range(2):
            qdt_ref[l, halves[hh], :] = q_t[halves[hh]] * qdec_ref[2 * pair + hh, l]
            kdt_ref[l, halves[hh], :] = k_t[halves[hh]] * kdec_ref[2 * pair + hh, l]

    e_blk = HEAD_DIM // 2
    for hh in range(2):
        h = 2 * pair + hh
        hs = halves[hh]
        for l in range(dec_seq):
            acc = None
            for m in range(l + 1):
                sc = jnp.sum(qt_ref[l, hs, :] * kt_ref[m, hs, :], axis=0, keepdims=True) * intra_ref[h, l - m]
                term = sc * vt_ref[m, hs, :]
                acc = term if acc is None else acc + term
            o_ref[l, hs, :] = acc
        for eb in range(HEAD_DIM // e_blk):
            es = slice(eb * e_blk, (eb + 1) * e_blk)
            erows = slice(hh * HEAD_DIM + eb * e_blk, hh * HEAD_DIM + (eb + 1) * e_blk)

            def body(d, accs, hh=hh, h=h, es=es, erows=erows):
                s_d = st_ref[hh, d, es, :]
                row = pl.ds(hh * HEAD_DIM + d, 1)
                upd = s_d * sdec_ref[h]
                new_accs = []
                for l in range(dec_seq):
                    new_accs.append(accs[l] + qdt_ref[l, row, :] * s_d)
                    upd = upd + kdt_ref[l, row, :] * vt_ref[l, erows, :]
                st_out_ref[hh, d, es, :] = upd
                return tuple(new_accs)

            zero = jnp.zeros((e_blk, nb), F32)
            accs = lax.fori_loop(0, HEAD_DIM, body, tuple(zero for _ in range(dec_seq)), unroll=2)
            for l in range(dec_seq):
                o_ref[l, erows, :] = o_ref[l, erows, :] + accs[l]

    for l in range(dec_seq):
        o = o_ref[l]
        normed = []
        for hh in range(2):
            oh = o[halves[hh]]
            normed.append(oh * lax.rsqrt(jnp.mean(oh * oh, axis=0, keepdims=True) + EPS))
        rows = pl.ds(l, nb, stride=dec_seq)
        mix_ref[rows, :] = jnp.concatenate(normed, axis=0).T * _silu(g_ref[rows, :])


def _decode_ret(main, state_t, dec_seq):
    nb = state_t.shape[-1]
    m = main.shape[0]
    assert nb == LANES and m == nb * dec_seq
    steps = [j for j in range(dec_seq)]
    tab = lambda f: jnp.asarray([[f(h, j) for j in steps] for h in range(N_RET_HEADS)], F32)
    qdec = tab(lambda h, j: math.exp(LOG_DECAY[h] * (j + 1.0)))
    kdec = tab(lambda h, j: math.exp(LOG_DECAY[h] * (dec_seq - 1.0 - j)))
    intra = tab(lambda h, j: math.exp(LOG_DECAY[h] * j))
    sdec = jnp.asarray([math.exp(LOG_DECAY[h] * dec_seq) for h in range(N_RET_HEADS)], F32)
    smem = pl.BlockSpec(memory_space=pltpu.SMEM)
    col = lambda base: pl.BlockSpec((m, LANES), lambda p: (0, base + p))
    st_spec = pl.BlockSpec((2, HEAD_DIM, HEAD_DIM, nb), lambda p: (p, 0, 0, 0))
    stage = pltpu.VMEM((dec_seq, LANES, nb), F32)
    return pl.pallas_call(
        functools.partial(_decode_ret_kernel, dec_seq, nb),
        grid=(N_PAIRS,),
        in_specs=[smem, smem, smem, smem,
                  col(0), col(N_PAIRS), col(2 * N_PAIRS), col(3 * N_PAIRS), st_spec],
        out_specs=[pl.BlockSpec((m, LANES), lambda p: (0, p)), st_spec],
        out_shape=[jax.ShapeDtypeStruct((m, RET_WIDTH), F32),
                   jax.ShapeDtypeStruct(state_t.shape, F32)],
        scratch_shapes=[stage, stage, stage, stage, stage, stage],
        compiler_params=pltpu.CompilerParams(
            dimension_semantics=("arbitrary",), vmem_limit_bytes=VMEM_LIMIT),
        name="decode_ret",
    )(qdec, kdec, sdec, intra, main, main, main, main, state_t)


DEC_ROWS = 128
DEC_UNROLL = 4


def _decode_attn_consts(dec_seq, bias_ref):
    shift = dec_seq.bit_length() - 1
    rows = N_SWA_HEADS * dec_seq
    rb = _iota((rows, WINDOW), 0)
    cb = _iota((rows, WINDOW), 1)
    head = rb >> shift
    i = rb & (dec_seq - 1)
    slope = jnp.zeros((rows, WINDOW), F32)
    for h in range(N_SWA_HEADS):
        slope = jnp.where(head == h, ALIBI_SLOPES[h], slope)
    bias_ref[0] = jnp.where(cb > i, -(slope * (WINDOW + i - cb).astype(F32)), NEG_INF)
    m = cb & (dec_seq - 1)
    bias_ref[1] = jnp.where(m <= i, -(slope * (i - m).astype(F32)), NEG_INF)


def _decode_attn_kernel(dec_seq, sinks_ref, qn_ref, kn_ref, vs_ref, kt_ref, vt_ref,
                        mix_ref, kt_out_ref, vt_out_ref,
                        bias_ref, qbd_ref, oblk_ref, knew_ref, vnew_ref, knt_ref, vst_ref):
    @pl.when(pl.program_id(0) == 0)
    def _():
        _decode_attn_consts(dec_seq, bias_ref)

    first = _first_half()
    shift = dec_seq.bit_length() - 1

    kn_t = kn_ref[...].T
    vs_t = vs_ref[...].T
    knt_ref[...] = kn_t.astype(BF16)
    vst_ref[...] = vs_t.astype(BF16)
    for bb in range(DEC_GROUP):
        sh = (WINDOW - dec_seq - bb * dec_seq) % LANES
        knew_ref[bb] = pltpu.roll(kn_t, sh, axis=1) if sh else kn_t
        vnew_ref[bb] = pltpu.roll(vs_t, sh, axis=1) if sh else vs_t
    qn = qn_ref[...]
    qn_sw = pltpu.roll(qn, HEAD_DIM, axis=1)
    for h in range(N_SWA_HEADS):
        kv_half = h // SWA_GROUP
        if (h % 2) == kv_half:
            src = qn[:, (h // 2) * LANES:(h // 2 + 1) * LANES]
        else:
            col = (h + 1) // 2
            src = qn_sw[:, col * LANES:(col + 1) * LANES]
        qbd_ref[h] = jnp.where(first, src, 0.0) if kv_half == 0 else jnp.where(first, 0.0, src)

    sink_rows = jnp.concatenate(
        [jnp.full((dec_seq, LANES), sinks_ref[h], F32) for h in range(N_SWA_HEADS)], axis=0)
    col_batch = _iota((N_SWA_HEADS * dec_seq, LANES), 1) >> shift
    keep_old = _iota((1, LANES), 1) < WINDOW - dec_seq

    def per_batches(i, carry):
        bs = [i * DEC_UNROLL + u for u in range(DEC_UNROLL)]
        rows = [pl.ds(pl.multiple_of(b * dec_seq, dec_seq), dec_seq) for b in bs]
        k_old = [kt_ref[b] for b in bs]
        v_old = [vt_ref[b] for b in bs]
        q_st = [jnp.concatenate([qbd_ref[h, r, :] for h in range(N_SWA_HEADS)], axis=0).astype(BF16)
                for r in rows]
        s = [_dot(q_st[u], jnp.concatenate([k_old[u].astype(BF16), knt_ref[...]], axis=1))
             for u in range(DEC_UNROLL)]
        o = []
        for u, b in enumerate(bs):
            bias = jnp.concatenate([bias_ref[0], jnp.where(col_batch == b, bias_ref[1], NEG_INF)], axis=1)
            w_v = jnp.concatenate([v_old[u].astype(BF16), vst_ref[...]], axis=1)
            o.append(_softmax_sink_pv(s[u] * K_SCALE + bias, sink_rows, w_v))
        for u, b in enumerate(bs):
            for h in range(N_SWA_HEADS):
                oblk_ref[h, rows[u], :] = o[u][h * dec_seq:(h + 1) * dec_seq]
            kt_out_ref[b] = jnp.where(keep_old, pltpu.roll(k_old[u], LANES - dec_seq, axis=1), knew_ref[b])
            vt_out_ref[b] = jnp.where(keep_old, pltpu.roll(v_old[u], LANES - dec_seq, axis=1), vnew_ref[b])
        return carry

    lax.fori_loop(0, DEC_GROUP // DEC_UNROLL, per_batches, 0)

    y1 = jnp.where(first, oblk_ref[3], oblk_ref[4])
    moved = pltpu.roll(jnp.concatenate([oblk_ref[1], y1, oblk_ref[6], oblk_ref[6]], axis=1), HEAD_DIM, axis=1)
    outs = [
        jnp.where(first, oblk_ref[0], moved[:, 0:LANES]),
        jnp.where(first, oblk_ref[2], moved[:, LANES:2 * LANES]),
        jnp.where(first, moved[:, 2 * LANES:3 * LANES], oblk_ref[5]),
        jnp.where(first, moved[:, 3 * LANES:4 * LANES], oblk_ref[7]),
    ]
    for c in range(SWA_WIDTH // LANES):
        mix_ref[:, c * LANES:(c + 1) * LANES] = outs[c].astype(BF16)


def _decode_attn(sinks, qn, kn, vs, k_t, v_t, dec_seq):
    nb = k_t.shape[0]
    assert DEC_GROUP * dec_seq == DEC_ROWS and nb % DEC_GROUP == 0 and dec_seq & (dec_seq - 1) == 0
    assert k_t.shape[1:] == (SWA_KV_WIDTH, WINDOW)
    row = lambda w: pl.BlockSpec((DEC_ROWS, w), lambda i: (i, 0))
    cache = pl.BlockSpec((DEC_GROUP, SWA_KV_WIDTH, WINDOW), lambda i: (i, 0, 0))
    return pl.pallas_call(
        functools.partial(_decode_attn_kernel, dec_seq),
        grid=(nb // DEC_GROUP,),
        in_specs=[pl.BlockSpec(memory_space=pltpu.SMEM),
                  row(SWA_WIDTH), row(SWA_KV_WIDTH), row(SWA_KV_WIDTH), cache, cache],
        out_specs=[row(SWA_WIDTH), cache, cache],
        out_shape=[jax.ShapeDtypeStruct((nb * dec_seq, SWA_WIDTH), BF16),
                   jax.ShapeDtypeStruct(k_t.shape, F32),
                   jax.ShapeDtypeStruct(v_t.shape, F32)],
        scratch_shapes=[
            pltpu.VMEM((2, N_SWA_HEADS * dec_seq, WINDOW), F32),
            pltpu.VMEM((N_SWA_HEADS, DEC_ROWS, LANES), F32),
            pltpu.VMEM((N_SWA_HEADS, DEC_ROWS, LANES), F32),
            pltpu.VMEM((DEC_GROUP, SWA_KV_WIDTH, LANES), F32),
            pltpu.VMEM((DEC_GROUP, SWA_KV_WIDTH, LANES), F32),
            pltpu.VMEM((SWA_KV_WIDTH, DEC_ROWS), BF16),
            pltpu.VMEM((SWA_KV_WIDTH, DEC_ROWS), BF16),
        ],
        compiler_params=pltpu.CompilerParams(
            dimension_semantics=("arbitrary",), vmem_limit_bytes=VMEM_LIMIT),
        name="decode_attn",
    )(sinks, qn, kn, vs, k_t, v_t)


def kernel(x_prompt, x_sample, state_ret, cache_swa_k, cache_swa_v, norm_mix_gain, w_in, q_norm_gain,
           k_norm_gain, attn_sinks, w_out, norm_ffn_gain, w_up, w_down):
    batch, seq, d = x_prompt.shape
    nb, dec_seq, _ = x_sample.shape
    wb = cache_swa_k.shape[1]
    assert d == D_MODEL and seq % PROMPT_TILE == 0 and wb == WINDOW

    gain_mix = norm_mix_gain.reshape(1, D_MODEL)
    gain_ffn = norm_ffn_gain.reshape(1, D_MODEL)
    qg = jnp.tile(q_norm_gain, 2).reshape(1, LANES)
    kg = jnp.tile(k_norm_gain, 2).reshape(1, LANES)

    def from_key_minor(a_t):
        return jnp.transpose(a_t.reshape(a_t.shape[0], N_SWA_KV, HEAD_DIM, WINDOW), (0, 3, 1, 2))

    def to_key_minor(a):
        return jnp.transpose(a, (0, 2, 3, 1)).reshape(a.shape[0], SWA_KV_WIDTH, WINDOW)

    xs = x_sample.reshape(nb * dec_seq, D_MODEL)
    main_s, qn_s, kn_s, vs_s, w_in_bf = _in_proj_cast(xs, gain_mix, w_in, qg, kg)
    mix_ret_s, state_t = _decode_ret(main_s, jnp.transpose(state_ret, (1, 2, 3, 0)), dec_seq)
    mix_swa_s, k_t, v_t = _decode_attn(attn_sinks, qn_s, kn_s, vs_s,
                                       to_key_minor(cache_swa_k), to_key_minor(cache_swa_v), dec_seq)
    y_s, w_out_bf, w_up_bf, w_down_bf = _out_mlp_cast(mix_ret_s, mix_swa_s, xs, gain_ffn, w_out, w_up, w_down)

    xp = x_prompt.reshape(batch * seq, D_MODEL)
    y_p, ret_p, kwin_t, vwin_t = _prompt_layer(attn_sinks, xp, gain_mix, w_in_bf, qg, kg, w_out_bf, gain_ffn,
                                               w_up_bf, w_down_bf, batch, seq)
    y_p = y_p.reshape(batch, seq, D_MODEL)

    return (y_p, y_s.reshape(nb, dec_seq, D_MODEL), ret_p, from_key_minor(kwin_t), from_key_minor(vwin_t),
            jnp.transpose(state_t, (3, 0, 1, 2)), from_key_minor(k_t), from_key_minor(v_t))
```

```python
import functools
import math

import jax
import jax.numpy as jnp
from jax import lax
from jax.experimental import pallas as pl
from jax.experimental.pallas import tpu as pltpu

F32 = jnp.float32
BF16 = jnp.bfloat16

D_MODEL = 1024
HEAD_DIM = 64
N_RET_HEADS = 8
N_SWA_HEADS = 8
N_SWA_KV = 2
SWA_GROUP = N_SWA_HEADS // N_SWA_KV
RET_WIDTH = N_RET_HEADS * HEAD_DIM
SWA_WIDTH = N_SWA_HEADS * HEAD_DIM
SWA_KV_WIDTH = N_SWA_KV * HEAD_DIM
MAIN_WIDTH = 4 * RET_WIDTH
IN_WIDTH = MAIN_WIDTH + SWA_WIDTH + 2 * SWA_KV_WIDTH
MIX_WIDTH = RET_WIDTH + SWA_WIDTH
D_FF = 4 * D_MODEL
WINDOW = 128
RET_CHUNK = 128
EPS = 1e-6
NEG_INF = -1e30

LANES = 128
N_PAIRS = N_RET_HEADS // 2
LOG_DECAY = [math.log(1.0 - 2.0 ** (-5.0 - h)) for h in range(N_RET_HEADS)]
ALIBI_SLOPES = [2.0 ** (-8.0 * (h + 1) / N_SWA_HEADS) for h in range(N_SWA_HEADS)]
K_SCALE = HEAD_DIM ** -0.5

PROMPT_TILE = 512
DEC_GROUP = 16
VMEM_LIMIT = 56 * 1024 * 1024
PROMPT_VMEM_LIMIT = 62 * 1024 * 1024


def _dot(a, b):
    return jnp.dot(a, b, preferred_element_type=F32)


def _dot_nt(a, b):
    return lax.dot_general(a, b, (((1,), (1,)), ((), ())), preferred_element_type=F32)


def _iota(shape, dim):
    return lax.broadcasted_iota(jnp.int32, shape, dim)


def _ones_block_diag():
    same = ((_iota((2 * LANES, LANES), 0) >> 6) & 1) == (_iota((2 * LANES, LANES), 1) >> 6)
    return jnp.where(same, 1.0, 0.0).astype(BF16)


def _head_sumsq(x, ones_bd):
    x2 = x * x
    hi = x2.astype(BF16)
    lo = (x2 - hi.astype(F32)).astype(BF16)
    return _dot(jnp.concatenate([hi, lo], axis=1), ones_bd)


def _head_rms_scale(x, ones_bd):
    return lax.rsqrt(_head_sumsq(x, ones_bd) * (1.0 / HEAD_DIM) + EPS)


def _head_rms_scales(xs, ones_bd):
    if not xs:
        return []
    parts = []
    for x in xs:
        x2 = x * x
        hi = x2.astype(BF16)
        parts.append(jnp.concatenate([hi, (x2 - hi.astype(F32)).astype(BF16)], axis=1))
    total = _dot(jnp.concatenate(parts, axis=0) if len(parts) > 1 else parts[0], ones_bd)
    out, lo = [], 0
    for x in xs:
        out.append(lax.rsqrt(total[lo:lo + x.shape[0]] * (1.0 / HEAD_DIM) + EPS))
        lo += x.shape[0]
    return out


def _first_half():
    return _iota((1, LANES), 1) < HEAD_DIM


def _pair_const(values, pair, lane_is_second):
    return jnp.where(lane_is_second, values[2 * pair + 1], values[2 * pair]).astype(F32)


def _silu(g):
    return g * (1.0 / (1.0 + jnp.exp(-g)))


def _in_proj_kernel(x_ref, gain_ref, w_ref, qg_ref, kg_ref, main_ref, qn_ref, kn_ref, vs_ref, fetch=None):
    x = x_ref[...]
    ms = jnp.mean(x * x, axis=-1, keepdims=True)
    hb = ((x * lax.rsqrt(ms + EPS)) * gain_ref[...]).astype(BF16)
    ones_bd = _ones_block_diag()

    def project(lo, width):
        if fetch is not None:
            fetch(lo, width)
        return _dot(hb, w_ref[:, lo:lo + width])

    for c in range(MAIN_WIDTH // 512):
        main_ref[:, c * 512:(c + 1) * 512] = project(c * 512, 512)
    qs = project(MAIN_WIDTH, SWA_WIDTH)
    k0 = MAIN_WIDTH + SWA_WIDTH
    ks = project(k0, SWA_KV_WIDTH)
    vs_ref[...] = project(k0 + SWA_KV_WIDTH, SWA_KV_WIDTH)
    q_cols = [qs[:, c * LANES:(c + 1) * LANES] for c in range(SWA_WIDTH // LANES)]
    scales = _head_rms_scales(q_cols + [ks], ones_bd)
    for c, xc in enumerate(q_cols):
        qn_ref[:, c * LANES:(c + 1) * LANES] = (xc * scales[c]) * qg_ref[...]
    kn_ref[...] = (ks * scales[-1]) * kg_ref[...]


ROW_TILE = 512


W_IN_BLOCKS = [(c * 512, 512) for c in range(MAIN_WIDTH // 512)] + [(MAIN_WIDTH, SWA_WIDTH),
                                                                    (MAIN_WIDTH + SWA_WIDTH, 2 * SWA_KV_WIDTH)]
assert sum(w for _, w in W_IN_BLOCKS) == IN_WIDTH


def _in_proj_cast_kernel(x_ref, gain_ref, w_hbm, qg_ref, kg_ref,
                         main_ref, qn_ref, kn_ref, vs_ref, w_bf_hbm,
                         w_v, stage, in_sem, out_sem):
    step = pl.program_id(0)
    block_in = [pltpu.make_async_copy(w_hbm.at[:, pl.ds(lo, width)], stage.at[:, pl.ds(lo, width)], in_sem.at[i])
                for i, (lo, width) in enumerate(W_IN_BLOCKS)]
    writeback = pltpu.make_async_copy(w_v, w_bf_hbm, out_sem.at[0])

    @pl.when(step == 0)
    def _():
        for copy in block_in:
            copy.start()
        arrived = set()

        def fetch(lo, width):
            for i, (blo, bwidth) in enumerate(W_IN_BLOCKS):
                if blo <= lo < blo + bwidth and i not in arrived:
                    assert lo + width <= blo + bwidth
                    block_in[i].wait()
                    w_v[:, blo:blo + bwidth] = stage[:, blo:blo + bwidth].astype(BF16)
                    arrived.add(i)

        _in_proj_kernel(x_ref, gain_ref, w_v, qg_ref, kg_ref, main_ref, qn_ref, kn_ref, vs_ref, fetch=fetch)
        assert len(arrived) == len(W_IN_BLOCKS)
        writeback.start()

    @pl.when(step > 0)
    def _():
        _in_proj_kernel(x_ref, gain_ref, w_v, qg_ref, kg_ref, main_ref, qn_ref, kn_ref, vs_ref)

    @pl.when(step == pl.num_programs(0) - 1)
    def _():
        writeback.wait()


def _in_proj_cast(x2d, gain, w_in, qg, kg):
    m = x2d.shape[0]
    tm = min(ROW_TILE, m)
    row = lambda w: pl.BlockSpec((tm, w), lambda i: (i, 0))
    full = lambda a: pl.BlockSpec(a.shape, lambda i: (0, 0), pipeline_mode=pl.Buffered(1))
    hbm = pl.BlockSpec(memory_space=pl.ANY)
    return pl.pallas_call(
        _in_proj_cast_kernel,
        grid=(m // tm,),
        in_specs=[row(D_MODEL), full(gain), hbm, full(qg), full(kg)],
        out_specs=[row(MAIN_WIDTH), row(SWA_WIDTH), row(SWA_KV_WIDTH), row(SWA_KV_WIDTH), hbm],
        out_shape=[jax.ShapeDtypeStruct((m, MAIN_WIDTH), F32),
                   jax.ShapeDtypeStruct((m, SWA_WIDTH), F32),
                   jax.ShapeDtypeStruct((m, SWA_KV_WIDTH), F32),
                   jax.ShapeDtypeStruct((m, SWA_KV_WIDTH), F32),
                   jax.ShapeDtypeStruct(w_in.shape, BF16)],
        scratch_shapes=[pltpu.VMEM(w_in.shape, BF16), pltpu.VMEM(w_in.shape, F32),
                        pltpu.SemaphoreType.DMA((len(W_IN_BLOCKS),)), pltpu.SemaphoreType.DMA((1,))],
        compiler_params=pltpu.CompilerParams(
            dimension_semantics=("arbitrary",), vmem_limit_bytes=VMEM_LIMIT),
        name="in_proj_cast",
    )(x2d, gain, w_in, qg, kg)


FF_CHUNK = 1024


def _out_mlp_kernel(mix_ret_ref, mix_swa_ref, x_ref, w_out_ref, gain_ref, w_up_ref, w_down_ref, y_ref):
    h = x_ref[...] + (_dot(mix_ret_ref[...].astype(BF16), w_out_ref[:RET_WIDTH, :])
                      + _dot(mix_swa_ref[...].astype(BF16), w_out_ref[RET_WIDTH:, :]))
    ms = jnp.mean(h * h, axis=-1, keepdims=True)
    hf = ((h * lax.rsqrt(ms + EPS)) * gain_ref[...]).astype(BF16)
    ff = None
    for c in range(D_FF // FF_CHUNK):
        u = _dot(hf, w_up_ref[:, c * FF_CHUNK:(c + 1) * FF_CHUNK])
        a = jnp.maximum(u, 0.0)
        d = _dot((a * a).astype(BF16), w_down_ref[c * FF_CHUNK:(c + 1) * FF_CHUNK, :])
        ff = d if ff is None else ff + d
    y_ref[...] = h + ff


OUT_MLP_ROWS = 1024
CAST_CHUNK = 512
N_FF_CHUNKS = D_FF // CAST_CHUNK


def _out_mlp_cast_kernel(mix_ret_ref, mix_swa_ref, x_ref, gain_ref, w_out_hbm, w_up_hbm, w_down_hbm,
                         y_ref, w_out_bf_hbm, w_up_bf_hbm, w_down_bf_hbm,
                         w_out_v, w_up_v, w_down_v, stage_out, stage_up, stage_down, hf_ref, in_sem, out_sem):
    step = pl.program_id(0)

    def out_in(i):
        return pltpu.make_async_copy(w_out_hbm.at[pl.ds(i * CAST_CHUNK, CAST_CHUNK), :], stage_out.at[i],
                                     in_sem.at[0, i])

    def up_in(c):
        return pltpu.make_async_copy(w_up_hbm.at[:, pl.ds(c * CAST_CHUNK, CAST_CHUNK)], stage_up.at[c % 2],
                                     in_sem.at[1, c % 2])

    def down_in(c):
        return pltpu.make_async_copy(w_down_hbm.at[pl.ds(c * CAST_CHUNK, CAST_CHUNK), :], stage_down.at[c % 2],
                                     in_sem.at[2, c % 2])

    writebacks = [pltpu.make_async_copy(src, dst, out_sem.at[i]) for i, (src, dst) in enumerate(
        [(w_out_v, w_out_bf_hbm), (w_up_v, w_up_bf_hbm), (w_down_v, w_down_bf_hbm)])]

    @pl.when(step == 0)
    def _():
        n_out = D_MODEL // CAST_CHUNK
        for i in range(n_out):
            out_in(i).start()
        up_in(0).start()
        down_in(0).start()
        for i in range(n_out):
            out_in(i).wait()
            w_out_v[i * CAST_CHUNK:(i + 1) * CAST_CHUNK, :] = stage_out[i].astype(BF16)
        writebacks[0].start()
        h = x_ref[...] + (_dot(mix_ret_ref[...].astype(BF16), w_out_v[:RET_WIDTH, :])
                          + _dot(mix_swa_ref[...].astype(BF16), w_out_v[RET_WIDTH:, :]))
        ms = jnp.mean(h * h, axis=-1, keepdims=True)
        hf_ref[...] = ((h * lax.rsqrt(ms + EPS)) * gain_ref[...]).astype(BF16)
        y_ref[...] = h
        for c in range(N_FF_CHUNKS):
            chunk = slice(c * CAST_CHUNK, (c + 1) * CAST_CHUNK)
            if c + 1 < N_FF_CHUNKS:
                up_in(c + 1).start()
                down_in(c + 1).start()
            up_in(c).wait()
            down_in(c).wait()
            w_up_v[:, chunk] = stage_up[c % 2].astype(BF16)
            w_down_v[chunk, :] = stage_down[c % 2].astype(BF16)
            a = jnp.maximum(_dot(hf_ref[...], w_up_v[:, chunk]), 0.0)
            y_ref[...] += _dot((a * a).astype(BF16), w_down_v[chunk, :])
        writebacks[1].start()
        writebacks[2].start()

    @pl.when(step > 0)
    def _():
        _out_mlp_kernel(mix_ret_ref, mix_swa_ref, x_ref, w_out_v, gain_ref, w_up_v, w_down_v, y_ref)

    @pl.when(step == pl.num_programs(0) - 1)
    def _():
        for wb in writebacks:
            wb.wait()


def _out_mlp_cast(mix_ret, mix_swa, x2d, gain, w_out, w_up, w_down):
    m = x2d.shape[0]
    tm = min(OUT_MLP_ROWS, m)
    assert D_MODEL % CAST_CHUNK == 0 and D_FF % CAST_CHUNK == 0
    single = m == tm
    mode = dict(pipeline_mode=pl.Buffered(1)) if single else {}
    row = lambda w: pl.BlockSpec((tm, w), lambda i: (i, 0), **mode)
    full = lambda a: pl.BlockSpec(a.shape, lambda i: (0, 0), pipeline_mode=pl.Buffered(1))
    hbm = pl.BlockSpec(memory_space=pl.ANY)
    return pl.pallas_call(
        _out_mlp_cast_kernel,
        grid=(m // tm,),
        in_specs=[row(RET_WIDTH), row(SWA_WIDTH), row(D_MODEL), full(gain), hbm, hbm, hbm],
        out_specs=[row(D_MODEL), hbm, hbm, hbm],
        out_shape=[jax.ShapeDtypeStruct((m, D_MODEL), F32),
                   jax.ShapeDtypeStruct(w_out.shape, BF16),
                   jax.ShapeDtypeStruct(w_up.shape, BF16),
                   jax.ShapeDtypeStruct(w_down.shape, BF16)],
        scratch_shapes=[
            pltpu.VMEM(w_out.shape, BF16), pltpu.VMEM(w_up.shape, BF16), pltpu.VMEM(w_down.shape, BF16),
            pltpu.VMEM((D_MODEL // CAST_CHUNK, CAST_CHUNK, D_MODEL), F32),
            pltpu.VMEM((2, D_MODEL, CAST_CHUNK), F32),
            pltpu.VMEM((2, CAST_CHUNK, D_MODEL), F32),
            pltpu.VMEM((tm, D_MODEL), BF16),
            pltpu.SemaphoreType.DMA((3, 2)),
            pltpu.SemaphoreType.DMA((3,)),
        ],
        compiler_params=pltpu.CompilerParams(
            dimension_semantics=("arbitrary",), vmem_limit_bytes=VMEM_LIMIT),
        name="out_mlp_cast",
    )(mix_ret, mix_swa, x2d, gain, w_out, w_up, w_down)


def _split_pair_rows(x, first):
    return jnp.concatenate([jnp.where(first, x, 0.0), jnp.where(first, 0.0, x)], axis=0).astype(BF16)


def _softmax_sink_pv(s, sink_wide, v_t):
    m = jnp.maximum(jnp.max(s, axis=-1, keepdims=True), sink_wide)
    p = jnp.exp(s - jnp.concatenate([m, m], axis=1))
    denom = jnp.sum(p, axis=-1, keepdims=True) + jnp.exp(sink_wide - m)
    return _dot_nt(p.astype(BF16), v_t) / denom


def _softmax_sink_pv_t(s_t, sink_lanes, v_t):
    m = jnp.maximum(jnp.max(s_t, axis=0, keepdims=True), sink_lanes)
    p = jnp.exp(s_t - m)
    denom = jnp.sum(p, axis=0, keepdims=True) + jnp.exp(sink_lanes - m)
    return _dot(v_t, p.astype(BF16)) / denom


def _prompt_consts(intra_ref, qdec_ref, kdec_ref, sdec_ref, bias_ref):
    r = _iota((LANES, LANES), 0)
    lane2 = _iota((LANES, LANES), 1) >= HEAD_DIM
    rf = r.astype(F32)
    ri = _iota((LANES, 2 * LANES), 0)
    ci = _iota((LANES, 2 * LANES), 1)
    diff = (ri - (ci & (LANES - 1))).astype(F32)
    for p in range(N_PAIRS):
        lg = _pair_const(LOG_DECAY, p, lane2)
        qdec_ref[p] = jnp.exp(lg * (rf + 1.0))
        kdec_ref[p] = jnp.exp(lg * (RET_CHUNK - 1.0 - rf))
        sdec_ref[p] = jnp.exp(_pair_const(LOG_DECAY, p, r >= HEAD_DIM) * float(RET_CHUNK))
        lg2 = _pair_const(LOG_DECAY, p, ci >= LANES)
        intra_ref[p] = jnp.where(diff >= 0.0, jnp.exp(lg2 * jnp.maximum(diff, 0.0)), 0.0)
    cols = SWA_GROUP * WINDOW
    kb = _iota((2 * WINDOW, cols), 0)
    cb = _iota((2 * WINDOW, cols), 1)
    grp = cb >> 7
    dist = WINDOW + (cb & (WINDOW - 1)) - kb
    valid = (dist >= 0) & (dist < WINDOW)
    distf = dist.astype(F32)
    for j in range(N_SWA_KV):
        sl = [ALIBI_SLOPES[SWA_GROUP * j + g] for g in range(SWA_GROUP)]
        slope = jnp.where(grp == 0, sl[0], jnp.where(grp == 1, sl[1], jnp.where(grp == 2, sl[2], sl[3])))
        b = jnp.where(valid, -(slope.astype(F32) * distf), NEG_INF)
        bias_ref[0, j] = b
        bias_ref[1, j] = jnp.where(kb >= WINDOW, b, NEG_INF)


def _prompt_layer_kernel(sinks_ref, x_ref, x_next_ref, gain_mix_ref, w_in_ref, qg_ref, kg_ref,
                         w_out_hbm, gain_ffn_ref, w_up_hbm, w_down_hbm,
                         y_ref, ret_ref, kwin_ref, vwin_ref,
                         main_ref, qn_ref, kn_ref, vs_ref, mix_ref, hb_ref,
                         state_ref, prevk_ref, prevv_ref,
                         intra_ref, qdec_ref, kdec_ref, sdec_ref, bias_ref,
                         w_out_ref, w_up_ref, w_down_ref, w_sem):
    t = pl.program_id(1)
    step = pl.program_id(0) * pl.num_programs(1) + t
    cur = step % 2
    nxt = 1 - cur

    late_weights = [pltpu.make_async_copy(src, dst, w_sem.at[i]) for i, (src, dst) in enumerate(
        [(w_out_hbm, w_out_ref), (w_up_hbm, w_up_ref), (w_down_hbm, w_down_ref)])]

    @pl.when(step == 0)
    def _():
        for copy in late_weights:
            copy.start()
        _prompt_consts(intra_ref, qdec_ref, kdec_ref, sdec_ref, bias_ref)
        _in_proj_kernel(x_ref, gain_mix_ref, w_in_ref, qg_ref, kg_ref,
                        main_ref.at[0], qn_ref.at[0], kn_ref.at[0], vs_ref.at[0])

    @pl.when(t == 0)
    def _():
        state_ref[...] = jnp.zeros_like(state_ref)
        prevk_ref[...] = jnp.zeros_like(prevk_ref)
        prevv_ref[...] = jnp.zeros_like(prevv_ref)

    first = _first_half()
    ones_bd = _ones_block_diag()
    bd_mask = (_iota((LANES, LANES), 0) >= HEAD_DIM) == (_iota((LANES, LANES), 1) >= HEAD_DIM)

    n_chunks = PROMPT_TILE // RET_CHUNK
    assert n_chunks == 4
    plan = [dict(a=[], b=[]),
            dict(a=["m0", "m1"], b=[]),
            dict(a=["m2", "m3"], b=[]),
            dict(a=["v", "k"], b=["q0", "q1"])]
    a0 = MAIN_WIDTH
    attn_cols = {"v": (a0 + SWA_WIDTH + SWA_KV_WIDTH, SWA_KV_WIDTH), "k": (a0 + SWA_WIDTH, SWA_KV_WIDTH)}
    attn_cols.update({"q%d" % i: (a0 + 256 * i, 256) for i in range(SWA_WIDTH // 256)})
    cols = lambda base, p: slice(base + p * LANES, base + (p + 1) * LANES)

    def project(items):
        raw = {}
        for it in items:
            if it[0] == "m":
                cb = int(it[1:])
                main_ref[nxt, :, cb * 512:(cb + 1) * 512] = _dot(hb_ref[...], w_in_ref[:, cb * 512:(cb + 1) * 512])
            else:
                lo, width = attn_cols[it]
                raw[it] = _dot(hb_ref[...], w_in_ref[:, lo:lo + width])
        return raw

    def attn_scales(raw):
        return {name: [_head_rms_scale(val[:, cols(0, cq)], ones_bd) for cq in range(val.shape[1] // LANES)]
                for name, val in raw.items() if name != "v"}

    def store_attn(raw, scales):
        for name, val in raw.items():
            if name == "v":
                vs_ref[nxt] = val
            elif name == "k":
                kn_ref[nxt] = (val * scales[name][0]) * kg_ref[...]
            else:
                base = attn_cols[name][0] - a0
                for cq in range(val.shape[1] // LANES):
                    qn_ref[nxt, :, cols(base, cq)] = (val[:, cols(0, cq)] * scales[name][cq]) * qg_ref[...]

    pairs = range(N_PAIRS)
    kvs = range(N_SWA_KV)

    def stage1(c):
        rows = slice(c * RET_CHUNK, (c + 1) * RET_CHUNK)
        q = [main_ref[cur, rows, cols(0, p)] for p in pairs]
        k = [main_ref[cur, rows, cols(RET_WIDTH, p)] * K_SCALE for p in pairs]
        v = [main_ref[cur, rows, cols(2 * RET_WIDTH, p)] for p in pairs]
        state = [state_ref[p] for p in pairs]
        kc = kn_ref[cur, rows, :]
        k_sw = pltpu.roll(kc, HEAD_DIM, axis=1)
        v_t = vs_ref[cur, rows, :].T
        is_first = (t == 0).astype(jnp.int32) if c == 0 else 0
        k_dup =[(jnp.where(first, kc, k_sw) if j == 0 else jnp.where(first, k_sw, kc)).astype(BF16)
                 for j in kvs]
        v_tj = [v_t[j * HEAD_DIM:(j + 1) * HEAD_DIM].astype(BF16) for j in kvs]
        q_st = []
        for j in kvs:
            pieces = []
            for g in range(SWA_GROUP):
                qc = qn_ref[cur, rows, cols(0, 2 * j + g // 2)]
                pieces.append(jnp.where(first, qc, 0.0) if g % 2 == 0 else jnp.where(first, 0.0, qc))
            q_st.append(jnp.concatenate(pieces, axis=0).astype(BF16))

        s = [_dot_nt(q[p].astype(BF16), _split_pair_rows(k[p], first)) for p in pairs]
        s_t = [_dot_nt(jnp.concatenate([prevk_ref[j], k_dup[j]], axis=0), q_st[j]) for j in kvs]
        cross = [_dot((q[p] * qdec_ref[p]).astype(BF16), state[p].astype(BF16)) for p in pairs]
        upd = [_dot((k[p] * kdec_ref[p]).T.astype(BF16), v[p].astype(BF16)) for p in pairs]
        return dict(rows=rows, v=v, state=state, k_dup=k_dup, v_tj=v_tj, is_first=is_first,
                    s=s, s_t=s_t, cross=cross, upd=upd)

    def stage2(st):
        st["o"] = [_dot((st["s"][p] * intra_ref[p]).astype(BF16), _split_pair_rows(st["v"][p], first))
                   + st["cross"][p] for p in pairs]
        st["o_t"] = []
        for j in kvs:
            sink_lanes = jnp.concatenate(
                [jnp.full((1, WINDOW), sinks_ref[SWA_GROUP * j + g], F32) for g in range(SWA_GROUP)],
                axis=1)
            v_cat = jnp.concatenate([prevv_ref[j], st["v_tj"][j]], axis=1)
            st["o_t"].append(_softmax_sink_pv_t(st["s_t"][j] * K_SCALE + bias_ref[st["is_first"], j],
                                                sink_lanes, v_cat))
        for p in pairs:
            state_ref[p] = st["state"][p] * sdec_ref[p] + jnp.where(bd_mask, st["upd"][p], 0.0)
        for j in kvs:
            prevk_ref[j] = st["k_dup"][j]
            prevv_ref[j] = st["v_tj"][j]

    def stage3(st, raw):
        rows = st["rows"]
        scale = [_head_rms_scale(st["o"][p], ones_bd) for p in pairs]
        raw_scales = attn_scales(raw)
        for p in pairs:
            g = main_ref[cur, rows, cols(3 * RET_WIDTH, p)]
            mix_ref[rows, cols(0, p)] = (st["o"][p] * scale[p] * _silu(g)).astype(BF16)
        store_attn(raw, raw_scales)
        for j in kvs:
            o_t = st["o_t"][j]
            for half in range(2):
                pair_t = jnp.concatenate([o_t[:, (2 * half) * WINDOW:(2 * half + 1) * WINDOW],
                                          o_t[:, (2 * half + 1) * WINDOW:(2 * half + 2) * WINDOW]], axis=0)
                mix_ref[rows, cols(RET_WIDTH, 2 * j + half)] = pair_t.T.astype(BF16)

    xn = x_next_ref[...]
    hb_ref[...] = ((xn * lax.rsqrt(jnp.mean(xn * xn, axis=-1, keepdims=True) + EPS)) * gain_mix_ref[...]).astype(BF16)
    for c in range(n_chunks):
        st = stage1(c)
        raw = project(plan[c]["a"])
        stage2(st)
        raw.update(project(plan[c]["b"]))
        stage3(st, raw)

    @pl.when(t == pl.num_programs(1) - 1)
    def _():
        for p in range(N_PAIRS):
            s = state_ref[p]
            ret_ref[2 * p] = s[:HEAD_DIM, :HEAD_DIM]
            ret_ref[2 * p + 1] = s[HEAD_DIM:, HEAD_DIM:]
        last = slice(PROMPT_TILE - WINDOW, PROMPT_TILE)
        kwin_ref[...] = kn_ref[cur, last, :].T
        vwin_ref[...] = vs_ref[cur, last, :].T

    @pl.when(step == 0)
    def _():
        for copy in late_weights:
            copy.wait()

    _out_mlp_kernel(mix_ref.at[:, pl.ds(0, RET_WIDTH)], mix_ref.at[:, pl.ds(RET_WIDTH, SWA_WIDTH)], x_ref,
                    w_out_ref, gain_ffn_ref, w_up_ref, w_down_ref, y_ref)


def _prompt_layer(sinks, x2d, gain_mix, w_in_bf, qg, kg, w_out_bf, gain_ffn, w_up_bf, w_down_bf, batch, seq):
    nt = seq // PROMPT_TILE
    last_tile = batch * nt - 1
    row = lambda w: pl.BlockSpec((PROMPT_TILE, w), lambda b, t: (b * nt + t, 0))
    next_row = pl.BlockSpec((PROMPT_TILE, D_MODEL), lambda b, t: (jnp.minimum(b * nt + t + 1, last_tile), 0))
    full = lambda a: pl.BlockSpec(a.shape, lambda b, t: (0, 0), pipeline_mode=pl.Buffered(1))
    in_hbm = pl.BlockSpec(memory_space=pl.ANY)
    return pl.pallas_call(
        _prompt_layer_kernel,
        grid=(batch, nt),
        in_specs=[pl.BlockSpec(memory_space=pltpu.SMEM), row(D_MODEL), next_row,
                  full(gain_mix), full(w_in_bf), full(qg), full(kg),
                  in_hbm, full(gain_ffn), in_hbm, in_hbm],
        out_specs=[row(D_MODEL),
                   pl.BlockSpec((None, N_RET_HEADS, HEAD_DIM, HEAD_DIM), lambda b, t: (b, 0, 0, 0)),
                   pl.BlockSpec((None, SWA_KV_WIDTH, WINDOW), lambda b, t: (b, 0, 0)),
                   pl.BlockSpec((None, SWA_KV_WIDTH, WINDOW), lambda b, t: (b, 0, 0))],
        out_shape=[jax.ShapeDtypeStruct((batch * seq, D_MODEL), F32),
                   jax.ShapeDtypeStruct((batch, N_RET_HEADS, HEAD_DIM, HEAD_DIM), F32),
                   jax.ShapeDtypeStruct((batch, SWA_KV_WIDTH, WINDOW), F32),
                   jax.ShapeDtypeStruct((batch, SWA_KV_WIDTH, WINDOW), F32)],
        scratch_shapes=[
            pltpu.VMEM((2, PROMPT_TILE, MAIN_WIDTH), F32),
            pltpu.VMEM((2, PROMPT_TILE, SWA_WIDTH), F32),
            pltpu.VMEM((2, PROMPT_TILE, SWA_KV_WIDTH), F32),
            pltpu.VMEM((2, PROMPT_TILE, SWA_KV_WIDTH), F32),
            pltpu.VMEM((PROMPT_TILE, MIX_WIDTH), BF16),
            pltpu.VMEM((PROMPT_TILE, D_MODEL), BF16),
            pltpu.VMEM((N_PAIRS, LANES, LANES), F32),
            pltpu.VMEM((N_SWA_KV, WINDOW, LANES), BF16),
            pltpu.VMEM((N_SWA_KV, HEAD_DIM, WINDOW), BF16),
            pltpu.VMEM((N_PAIRS, LANES, 2 * LANES), F32),
            pltpu.VMEM((N_PAIRS, LANES, LANES), F32),
            pltpu.VMEM((N_PAIRS, LANES, LANES), F32),
            pltpu.VMEM((N_PAIRS, LANES, LANES), F32),
            pltpu.VMEM((2, N_SWA_KV, 2 * WINDOW, SWA_GROUP * WINDOW), F32),
            pltpu.VMEM(w_out_bf.shape, BF16),
            pltpu.VMEM(w_up_bf.shape, BF16),
            pltpu.VMEM(w_down_bf.shape, BF16),
            pltpu.SemaphoreType.DMA((3,)),
        ],
        compiler_params=pltpu.CompilerParams(
            dimension_semantics=("arbitrary", "arbitrary"), vmem_limit_bytes=PROMPT_VMEM_LIMIT),
        name="prompt_layer",
    )(sinks, x2d, x2d, gain_mix, w_in_bf, qg, kg, w_out_bf, gain_ffn, w_up_bf, w_down_bf)


def _decode_ret_kernel(dec_seq, nb, qdec_ref, kdec_ref, sdec_ref, intra_ref,
                       q_ref, k_ref, v_ref, g_ref, st_ref,
                       mix_ref, st_out_ref,
                       qt_ref, kt_ref, vt_ref, qdt_ref, kdt_ref, o_ref):
    pair = pl.program_id(0)
    halves = [slice(0, HEAD_DIM), slice(HEAD_DIM, 2 * HEAD_DIM)]
    for l in range(dec_seq):
        rows = pl.ds(l, nb, stride=dec_seq)
        q_t = q_ref[rows, :].T
        k_t = (k_ref[rows, :] * K_SCALE).T
        qt_ref[l] = q_t
        kt_ref[l] = k_t
        vt_ref[l] = v_ref[rows, :].T
        for hh in range(2):
            qdt_ref[l, halves[hh], :] = q_t[halves[hh]] * qdec_ref[2 * pair + hh, l]
            kdt_ref[l, halves[hh], :] = k_t[halves[hh]] * kdec_ref[2 * pair + hh, l]

    e_blk = HEAD_DIM // 2
    for hh in range(2):
        h = 2 * pair + hh
        hs = halves[hh]
        for l in range(dec_seq):
            acc = None
            for m in range(l + 1):
                sc = jnp.sum(qt_ref[l, hs, :] * kt_ref[m, hs, :], axis=0, keepdims=True) * intra_ref[h, l - m]
                term = sc * vt_ref[m, hs, :]
                acc = term if acc is None else acc + term
            o_ref[l, hs, :] = acc
        for eb in range(HEAD_DIM // e_blk):
            es = slice(eb * e_blk, (eb + 1) * e_blk)
            erows = slice(hh * HEAD_DIM + eb * e_blk, hh * HEAD_DIM + (eb + 1) * e_blk)

            def body(d, accs, hh=hh, h=h, es=es, erows=erows):
                s_d = st_ref[hh, d, es, :]
                row = pl.ds(hh * HEAD_DIM + d, 1)
                upd = s_d * sdec_ref[h]
                new_accs = []
                for l in range(dec_seq):
                    new_accs.append(accs[l] + qdt_ref[l, row, :] * s_d)
                    upd = upd + kdt_ref[l, row, :] * vt_ref[l, erows, :]
                st_out_ref[hh, d, es, :] = upd
                return tuple(new_accs)

            zero = jnp.zeros((e_blk, nb), F32)
            accs = lax.fori_loop(0, HEAD_DIM, body, tuple(zero for _ in range(dec_seq)), unroll=2)
            for l in range(dec_seq):
                o_ref[l, erows, :] = o_ref[l, erows, :] + accs[l]

    for l in range(dec_seq):
        o = o_ref[l]
        normed = []
        for hh in range(2):
            oh = o[halves[hh]]
            normed.append(oh * lax.rsqrt(jnp.mean(oh * oh, axis=0, keepdims=True) + EPS))
        rows = pl.ds(l, nb, stride=dec_seq)
        mix_ref[rows, :] = jnp.concatenate(normed, axis=0).T * _silu(g_ref[rows, :])


def _decode_ret(main, state_t, dec_seq):
    nb = state_t.shape[-1]
    m = main.shape[0]
    assert nb == LANES and m == nb * dec_seq
    steps = [j for j in range(dec_seq)]
    tab = lambda f: jnp.asarray([[f(h, j) for j in steps] for h in range(N_RET_HEADS)], F32)
    qdec = tab(lambda h, j: math.exp(LOG_DECAY[h] * (j + 1.0)))
    kdec = tab(lambda h, j: math.exp(LOG_DECAY[h] * (dec_seq - 1.0 - j)))
    intra = tab(lambda h, j: math.exp(LOG_DECAY[h] * j))
    sdec = jnp.asarray([math.exp(LOG_DECAY[h] * dec_seq) for h in range(N_RET_HEADS)], F32)
    smem = pl.BlockSpec(memory_space=pltpu.SMEM)
    col = lambda base: pl.BlockSpec((m, LANES), lambda p: (0, base + p))
    st_spec = pl.BlockSpec((2, HEAD_DIM, HEAD_DIM, nb), lambda p: (p, 0, 0, 0))
    stage = pltpu.VMEM((dec_seq, LANES, nb), F32)
    return pl.pallas_call(
        functools.partial(_decode_ret_kernel, dec_seq, nb),
        grid=(N_PAIRS,),
        in_specs=[smem, smem, smem, smem,
                  col(0), col(N_PAIRS), col(2 * N_PAIRS), col(3 * N_PAIRS), st_spec],
        out_specs=[pl.BlockSpec((m, LANES), lambda p: (0, p)), st_spec],
        out_shape=[jax.ShapeDtypeStruct((m, RET_WIDTH), F32),
                   jax.ShapeDtypeStruct(state_t.shape, F32)],
        scratch_shapes=[stage, stage, stage, stage, stage, stage],
        compiler_params=pltpu.CompilerParams(
            dimension_semantics=("arbitrary",), vmem_limit_bytes=VMEM_LIMIT),
        name="decode_ret",
    )(qdec, kdec, sdec, intra, main, main, main, main, state_t)


DEC_ROWS = 128
DEC_UNROLL = 4


def _decode_attn_consts(dec_seq, bias_ref):
    shift = dec_seq.bit_length() - 1
    rows = N_SWA_HEADS * dec_seq
    rb = _iota((rows, WINDOW), 0)
    cb = _iota((rows, WINDOW), 1)
    head = rb >> shift
    i = rb & (dec_seq - 1)
    slope = jnp.zeros((rows, WINDOW), F32)
    for h in range(N_SWA_HEADS):
        slope = jnp.where(head == h, ALIBI_SLOPES[h], slope)
    bias_ref[0] = jnp.where(cb > i, -(slope * (WINDOW + i - cb).astype(F32)), NEG_INF)
    m = cb & (dec_seq - 1)
    bias_ref[1] = jnp.where(m <= i, -(slope * (i - m).astype(F32)), NEG_INF)


def _decode_attn_kernel(dec_seq, sinks_ref, qn_ref, kn_ref, vs_ref, kt_ref, vt_ref,
                        mix_ref, kt_out_ref, vt_out_ref,
                        bias_ref, qbd_ref, oblk_ref, knew_ref, vnew_ref, knt_ref, vst_ref):
    @pl.when(pl.program_id(0) == 0)
    def _():
        _decode_attn_consts(dec_seq, bias_ref)

    first = _first_half()
    shift = dec_seq.bit_length() - 1

    kn_t = kn_ref[...].T
    vs_t = vs_ref[...].T
    knt_ref[...] = kn_t.astype(BF16)
    vst_ref[...] = vs_t.astype(BF16)
    for bb in range(DEC_GROUP):
        sh = (WINDOW - dec_seq - bb * dec_seq) % LANES
        knew_ref[bb] = pltpu.roll(kn_t, sh, axis=1) if sh else kn_t
        vnew_ref[bb] = pltpu.roll(vs_t, sh, axis=1) if sh else vs_t
    qn = qn_ref[...]
    qn_sw = pltpu.roll(qn, HEAD_DIM, axis=1)
    for h in range(N_SWA_HEADS):
        kv_half = h // SWA_GROUP
        if (h % 2) == kv_half:
            src = qn[:, (h // 2) * LANES:(h // 2 + 1) * LANES]
        else:
            col = (h + 1) // 2
            src = qn_sw[:, col * LANES:(col + 1) * LANES]
        qbd_ref[h] = jnp.where(first, src, 0.0) if kv_half == 0 else jnp.where(first, 0.0, src)

    sink_rows = jnp.concatenate(
        [jnp.full((dec_seq, LANES), sinks_ref[h], F32) for h in range(N_SWA_HEADS)], axis=0)
    col_batch = _iota((N_SWA_HEADS * dec_seq, LANES), 1) >> shift
    keep_old = _iota((1, LANES), 1) < WINDOW - dec_seq

    def per_batches(i, carry):
        bs = [i * DEC_UNROLL + u for u in range(DEC_UNROLL)]
        rows = [pl.ds(pl.multiple_of(b * dec_seq, dec_seq), dec_seq) for b in bs]
        k_old = [kt_ref[b] for b in bs]
        v_old = [vt_ref[b] for b in bs]
        q_st = [jnp.concatenate([qbd_ref[h, r, :] for h in range(N_SWA_HEADS)], axis=0).astype(BF16)
                for r in rows]
        s = [_dot(q_st[u], jnp.concatenate([k_old[u].astype(BF16), knt_ref[...]], axis=1))
             for u in range(DEC_UNROLL)]
        o = []
        for u, b in enumerate(bs):
            bias = jnp.concatenate([bias_ref[0], jnp.where(col_batch == b, bias_ref[1], NEG_INF)], axis=1)
            w_v = jnp.concatenate([v_old[u].astype(BF16), vst_ref[...]], axis=1)
            o.append(_softmax_sink_pv(s[u] * K_SCALE + bias, sink_rows, w_v))
        for u, b in enumerate(bs):
            for h in range(N_SWA_HEADS):
                oblk_ref[h, rows[u], :] = o[u][h * dec_seq:(h + 1) * dec_seq]
            kt_out_ref[b] = jnp.where(keep_old, pltpu.roll(k_old[u], LANES - dec_seq, axis=1), knew_ref[b])
            vt_out_ref[b] = jnp.where(keep_old, pltpu.roll(v_old[u], LANES - dec_seq, axis=1), vnew_ref[b])
        return carry

    lax.fori_loop(0, DEC_GROUP // DEC_UNROLL, per_batches, 0)

    y1 = jnp.where(first, oblk_ref[3], oblk_ref[4])
    moved = pltpu.roll(jnp.concatenate([oblk_ref[1], y1, oblk_ref[6], oblk_ref[6]], axis=1), HEAD_DIM, axis=1)
    outs = [
        jnp.where(first, oblk_ref[0], moved[:, 0:LANES]),
        jnp.where(first, oblk_ref[2], moved[:, LANES:2 * LANES]),
        jnp.where(first, moved[:, 2 * LANES:3 * LANES], oblk_ref[5]),
        jnp.where(first, moved[:, 3 * LANES:4 * LANES], oblk_ref[7]),
    ]
    for c in range(SWA_WIDTH // LANES):
        mix_ref[:, c * LANES:(c + 1) * LANES] = outs[c].astype(BF16)


def _decode_attn(sinks, qn, kn, vs, k_t, v_t, dec_seq):
    nb = k_t.shape[0]
    assert DEC_GROUP * dec_seq == DEC_ROWS and nb % DEC_GROUP == 0 and dec_seq & (dec_seq - 1) == 0
    assert k_t.shape[1:] == (SWA_KV_WIDTH, WINDOW)
    row = lambda w: pl.BlockSpec((DEC_ROWS, w), lambda i: (i, 0))
    cache = pl.BlockSpec((DEC_GROUP, SWA_KV_WIDTH, WINDOW), lambda i: (i, 0, 0))
    return pl.pallas_call(
        functools.partial(_decode_attn_kernel, dec_seq),
        grid=(nb // DEC_GROUP,),
        in_specs=[pl.BlockSpec(memory_space=pltpu.SMEM),
                  row(SWA_WIDTH), row(SWA_KV_WIDTH), row(SWA_KV_WIDTH), cache, cache],
        out_specs=[row(SWA_WIDTH), cache, cache],
        out_shape=[jax.ShapeDtypeStruct((nb * dec_seq, SWA_WIDTH), BF16),
                   jax.ShapeDtypeStruct(k_t.shape, F32),
                   jax.ShapeDtypeStruct(v_t.shape, F32)],
        scratch_shapes=[
            pltpu.VMEM((2, N_SWA_HEADS * dec_seq, WINDOW), F32),
            pltpu.VMEM((N_SWA_HEADS, DEC_ROWS, LANES), F32),
            pltpu.VMEM((N_SWA_HEADS, DEC_ROWS, LANES), F32),
            pltpu.VMEM((DEC_GROUP, SWA_KV_WIDTH, LANES), F32),
            pltpu.VMEM((DEC_GROUP, SWA_KV_WIDTH, LANES), F32),
            pltpu.VMEM((SWA_KV_WIDTH, DEC_ROWS), BF16),
            pltpu.VMEM((SWA_KV_WIDTH, DEC_ROWS), BF16),
        ],
        compiler_params=pltpu.CompilerParams(
            dimension_semantics=("arbitrary",), vmem_limit_bytes=VMEM_LIMIT),
        name="decode_attn",
    )(sinks, qn, kn, vs, k_t, v_t)


def kernel(x_prompt, x_sample, state_ret, cache_swa_k, cache_swa_v, norm_mix_gain, w_in, q_norm_gain,
           k_norm_gain, attn_sinks, w_out, norm_ffn_gain, w_up, w_down):
    batch, seq, d = x_prompt.shape
    nb, dec_seq, _ = x_sample.shape
    wb = cache_swa_k.shape[1]
    assert d == D_MODEL and seq % PROMPT_TILE == 0 and wb == WINDOW

    gain_mix = norm_mix_gain.reshape(1, D_MODEL)
    gain_ffn = norm_ffn_gain.reshape(1, D_MODEL)
    qg = jnp.tile(q_norm_gain, 2).reshape(1, LANES)
    kg = jnp.tile(k_norm_gain, 2).reshape(1, LANES)

    def from_key_minor(a_t):
        return jnp.transpose(a_t.reshape(a_t.shape[0], N_SWA_KV, HEAD_DIM, WINDOW), (0, 3, 1, 2))

    def to_key_minor(a):
        return jnp.transpose(a, (0, 2, 3, 1)).reshape(a.shape[0], SWA_KV_WIDTH, WINDOW)

    xs = x_sample.reshape(nb * dec_seq, D_MODEL)
    main_s, qn_s, kn_s, vs_s, w_in_bf = _in_proj_cast(xs, gain_mix, w_in, qg, kg)
    mix_ret_s, state_t = _decode_ret(main_s, jnp.transpose(state_ret, (1, 2, 3, 0)), dec_seq)
    mix_swa_s, k_t, v_t = _decode_attn(attn_sinks, qn_s, kn_s, vs_s,
                                       to_key_minor(cache_swa_k), to_key_minor(cache_swa_v), dec_seq)
    y_s, w_out_bf, w_up_bf, w_down_bf = _out_mlp_cast(mix_ret_s, mix_swa_s, xs, gain_ffn, w_out, w_up, w_down)

    xp = x_prompt.reshape(batch * seq, D_MODEL)
    y_p, ret_p, kwin_t, vwin_t = _prompt_layer(attn_sinks, xp, gain_mix, w_in_bf, qg, kg, w_out_bf, gain_ffn,
                                               w_up_bf, w_down_bf, batch, seq)
    y_p = y_p.reshape(batch, seq, D_MODEL)

    return (y_p, y_s.reshape(nb, dec_seq, D_MODEL), ret_p, from_key_minor(kwin_t), from_key_minor(vwin_t),
            jnp.transpose(state_t, (3, 0, 1, 2)), from_key_minor(k_t), from_key_minor(v_t))
```

```python
import functools
import math

import jax
import jax.numpy as jnp
from jax import lax
from jax.experimental import pallas as pl
from jax.experimental.pallas import tpu as pltpu

F32 = jnp.float32
BF16 = jnp.bfloat16

D_MODEL = 1024
HEAD_DIM = 64
N_RET_HEADS = 8
N_SWA_HEADS = 8
N_SWA_KV = 2
SWA_GROUP = N_SWA_HEADS // N_SWA_KV
RET_WIDTH = N_RET_HEADS * HEAD_DIM
SWA_WIDTH = N_SWA_HEADS * HEAD_DIM
SWA_KV_WIDTH = N_SWA_KV * HEAD_DIM
MAIN_WIDTH = 4 * RET_WIDTH
IN_WIDTH = MAIN_WIDTH + SWA_WIDTH + 2 * SWA_KV_WIDTH
MIX_WIDTH = RET_WIDTH + SWA_WIDTH
D_FF = 4 * D_MODEL
WINDOW = 128
RET_CHUNK = 128
EPS = 1e-6
NEG_INF = -1e30

LANES = 128
N_PAIRS = N_RET_HEADS // 2
LOG_DECAY = [math.log(1.0 - 2.0 ** (-5.0 - h)) for h in range(N_RET_HEADS)]
ALIBI_SLOPES = [2.0 ** (-8.0 * (h + 1) / N_SWA_HEADS) for h in range(N_SWA_HEADS)]
K_SCALE = HEAD_DIM ** -0.5

MXU_WIDTH = 256
PROJ_BLOCK = 2 * MXU_WIDTH
Q_BLOCK = MXU_WIDTH
PROMPT_TILE = 512
DEC_GROUP = 16
VMEM_LIMIT = 56 * 1024 * 1024
PROMPT_VMEM_LIMIT = 62 * 1024 * 1024


def _dot(a, b):
    return jnp.dot(a, b, preferred_element_type=F32)


def _dot_nt(a, b):
    return lax.dot_general(a, b, (((1,), (1,)), ((), ())), preferred_element_type=F32)


def _iota(shape, dim):
    return lax.broadcasted_iota(jnp.int32, shape, dim)


def _ones_block_diag():
    same = ((_iota((2 * LANES, LANES), 0) >> 6) & 1) == (_iota((2 * LANES, LANES), 1) >> 6)
    return jnp.where(same, 1.0, 0.0).astype(BF16)


def _head_sumsq(x, ones_bd):
    x2 = x * x
    hi = x2.astype(BF16)
    lo = (x2 - hi.astype(F32)).astype(BF16)
    return _dot(jnp.concatenate([hi, lo], axis=1), ones_bd)


def _head_rms_scale(x, ones_bd):
    return lax.rsqrt(_head_sumsq(x, ones_bd) * (1.0 / HEAD_DIM) + EPS)


def _head_rms_scales(xs, ones_bd):
    if not xs:
        return []
    parts = []
    for x in xs:
        x2 = x * x
        hi = x2.astype(BF16)
        parts.append(jnp.concatenate([hi, (x2 - hi.astype(F32)).astype(BF16)], axis=1))
    total = _dot(jnp.concatenate(parts, axis=0) if len(parts) > 1 else parts[0], ones_bd)
    out, lo = [], 0
    for x in xs:
        out.append(lax.rsqrt(total[lo:lo + x.shape[0]] * (1.0 / HEAD_DIM) + EPS))
        lo += x.shape[0]
    return out


def _first_half():
    return _iota((1, LANES), 1) < HEAD_DIM


def _pair_const(values, pair, lane_is_second):
    return jnp.where(lane_is_second, values[2 * pair + 1], values[2 * pair]).astype(F32)


def _silu(g):
    return g * (1.0 / (1.0 + jnp.exp(-g)))


def _in_proj_kernel(x_ref, gain_ref, w_ref, qg_ref, kg_ref, main_ref, qn_ref, kn_ref, vs_ref, fetch=None):
    x = x_ref[...]
    ms = jnp.mean(x * x, axis=-1, keepdims=True)
    hb = ((x * lax.rsqrt(ms + EPS)) * gain_ref[...]).astype(BF16)
    ones_bd = _ones_block_diag()

    def project(lo, width):
        if fetch is not None:
            fetch(lo, width)
        return _dot(hb, w_ref[:, lo:lo + width])

    for c in range(MAIN_WIDTH // PROJ_BLOCK):
        main_ref[:, c * PROJ_BLOCK:(c + 1) * PROJ_BLOCK] = project(c * PROJ_BLOCK, PROJ_BLOCK)
    qs = project(MAIN_WIDTH, SWA_WIDTH)
    k0 = MAIN_WIDTH + SWA_WIDTH
    ks = project(k0, SWA_KV_WIDTH)
    vs_ref[...] = project(k0 + SWA_KV_WIDTH, SWA_KV_WIDTH)
    q_cols = [qs[:, c * LANES:(c + 1) * LANES] for c in range(SWA_WIDTH // LANES)]
    scales = _head_rms_scales(q_cols + [ks], ones_bd)
    for c, xc in enumerate(q_cols):
        qn_ref[:, c * LANES:(c + 1) * LANES] = (xc * scales[c]) * qg_ref[...]
    kn_ref[...] = (ks * scales[-1]) * kg_ref[...]


ROW_TILE = 512


W_IN_BLOCKS = [(c * PROJ_BLOCK, PROJ_BLOCK) for c in range(MAIN_WIDTH // PROJ_BLOCK)] + [(MAIN_WIDTH, SWA_WIDTH),
                                                                    (MAIN_WIDTH + SWA_WIDTH, 2 * SWA_KV_WIDTH)]
assert sum(w for _, w in W_IN_BLOCKS) == IN_WIDTH


def _in_proj_cast_kernel(x_ref, gain_ref, w_hbm, qg_ref, kg_ref,
                         main_ref, qn_ref, kn_ref, vs_ref, w_bf_hbm,
                         w_v, stage, in_sem, out_sem):
    step = pl.program_id(0)
    block_in = [pltpu.make_async_copy(w_hbm.at[:, pl.ds(lo, width)], stage.at[:, pl.ds(lo, width)], in_sem.at[i])
                for i, (lo, width) in enumerate(W_IN_BLOCKS)]
    writeback = pltpu.make_async_copy(w_v, w_bf_hbm, out_sem.at[0])

    @pl.when(step == 0)
    def _():
        for copy in block_in:
            copy.start()
        arrived = set()

        def fetch(lo, width):
            for i, (blo, bwidth) in enumerate(W_IN_BLOCKS):
                if blo <= lo < blo + bwidth and i not in arrived:
                    assert lo + width <= blo + bwidth
                    block_in[i].wait()
                    w_v[:, blo:blo + bwidth] = stage[:, blo:blo + bwidth].astype(BF16)
                    arrived.add(i)

        _in_proj_kernel(x_ref, gain_ref, w_v, qg_ref, kg_ref, main_ref, qn_ref, kn_ref, vs_ref, fetch=fetch)
        assert len(arrived) == len(W_IN_BLOCKS)
        writeback.start()

    @pl.when(step > 0)
    def _():
        _in_proj_kernel(x_ref, gain_ref, w_v, qg_ref, kg_ref, main_ref, qn_ref, kn_ref, vs_ref)

    @pl.when(step == pl.num_programs(0) - 1)
    def _():
        writeback.wait()


def _in_proj_cast(x2d, gain, w_in, qg, kg):
    m = x2d.shape[0]
    tm = min(ROW_TILE, m)
    row = lambda w: pl.BlockSpec((tm, w), lambda i: (i, 0))
    full = lambda a: pl.BlockSpec(a.shape, lambda i: (0, 0), pipeline_mode=pl.Buffered(1))
    hbm = pl.BlockSpec(memory_space=pl.ANY)
    return pl.pallas_call(
        _in_proj_cast_kernel,
        grid=(m // tm,),
        in_specs=[row(D_MODEL), full(gain), hbm, full(qg), full(kg)],
        out_specs=[row(MAIN_WIDTH), row(SWA_WIDTH), row(SWA_KV_WIDTH), row(SWA_KV_WIDTH), hbm],
        out_shape=[jax.ShapeDtypeStruct((m, MAIN_WIDTH), F32),
                   jax.ShapeDtypeStruct((m, SWA_WIDTH), F32),
                   jax.ShapeDtypeStruct((m, SWA_KV_WIDTH), F32),
                   jax.ShapeDtypeStruct((m, SWA_KV_WIDTH), F32),
                   jax.ShapeDtypeStruct(w_in.shape, BF16)],
        scratch_shapes=[pltpu.VMEM(w_in.shape, BF16), pltpu.VMEM(w_in.shape, F32),
                        pltpu.SemaphoreType.DMA((len(W_IN_BLOCKS),)), pltpu.SemaphoreType.DMA((1,))],
        compiler_params=pltpu.CompilerParams(
            dimension_semantics=("arbitrary",), vmem_limit_bytes=VMEM_LIMIT),
        name="in_proj_cast",
    )(x2d, gain, w_in, qg, kg)


FF_CHUNK = 1024


def _out_mlp_kernel(mix_ret_ref, mix_swa_ref, x_ref, w_out_ref, gain_ref, w_up_ref, w_down_ref, y_ref):
    h = x_ref[...] + (_dot(mix_ret_ref[...].astype(BF16), w_out_ref[:RET_WIDTH, :])
                      + _dot(mix_swa_ref[...].astype(BF16), w_out_ref[RET_WIDTH:, :]))
    ms = jnp.mean(h * h, axis=-1, keepdims=True)
    hf = ((h * lax.rsqrt(ms + EPS)) * gain_ref[...]).astype(BF16)
    ff = None
    for c in range(D_FF // FF_CHUNK):
        u = _dot(hf, w_up_ref[:, c * FF_CHUNK:(c + 1) * FF_CHUNK])
        a = jnp.maximum(u, 0.0)
        d = _dot((a * a).astype(BF16), w_down_ref[c * FF_CHUNK:(c + 1) * FF_CHUNK, :])
        ff = d if ff is None else ff + d
    y_ref[...] = h + ff


CAST_CHUNK = 512
N_FF_CHUNKS = D_FF // CAST_CHUNK


def _out_mlp_cast_kernel(mix_ret_ref, mix_swa_ref, x_ref, gain_ref, w_out_hbm, w_up_hbm, w_down_hbm,
                         y_ref, w_out_bf_hbm, w_up_bf_hbm, w_down_bf_hbm,
                         w_out_v, w_up_v, w_down_v, stage_out, stage_up, stage_down, in_sem, out_sem):
    step = pl.program_id(0)

    def out_in(i):
        return pltpu.make_async_copy(w_out_hbm.at[pl.ds(i * CAST_CHUNK, CAST_CHUNK), :], stage_out.at[i],
                                     in_sem.at[0, i])

    def up_in(c):
        return pltpu.make_async_copy(w_up_hbm.at[:, pl.ds(c * CAST_CHUNK, CAST_CHUNK)], stage_up.at[c % 2],
                                     in_sem.at[1, c % 2])

    def down_in(c):
        return pltpu.make_async_copy(w_down_hbm.at[pl.ds(c * CAST_CHUNK, CAST_CHUNK), :], stage_down.at[c % 2],
                                     in_sem.at[2, c % 2])

    writebacks = [pltpu.make_async_copy(src, dst, out_sem.at[i]) for i, (src, dst) in enumerate(
        [(w_out_v, w_out_bf_hbm), (w_up_v, w_up_bf_hbm), (w_down_v, w_down_bf_hbm)])]

    @pl.when(step == 0)
    def _():
        n_out = D_MODEL // CAST_CHUNK
        for i in range(n_out):
            out_in(i).start()
        up_in(0).start()
        down_in(0).start()
        for i in range(n_out):
            out_in(i).wait()
            w_out_v[i * CAST_CHUNK:(i + 1) * CAST_CHUNK, :] = stage_out[i].astype(BF16)
        writebacks[0].start()
        h = x_ref[...] + (_dot(mix_ret_ref[...].astype(BF16), w_out_v[:RET_WIDTH, :])
                          + _dot(mix_swa_ref[...].astype(BF16), w_out_v[RET_WIDTH:, :]))
        ms = jnp.mean(h * h, axis=-1, keepdims=True)
        hf = ((h * lax.rsqrt(ms + EPS)) * gain_ref[...]).astype(BF16)
        ff = None
        for c in range(N_FF_CHUNKS):
            chunk = slice(c * CAST_CHUNK, (c + 1) * CAST_CHUNK)
            if c + 1 < N_FF_CHUNKS:
                up_in(c + 1).start()
                down_in(c + 1).start()
            up_in(c).wait()
            down_in(c).wait()
            w_up_v[:, chunk] = stage_up[c % 2].astype(BF16)
            w_down_v[chunk, :] = stage_down[c % 2].astype(BF16)
            a = jnp.maximum(_dot(hf, w_up_v[:, chunk]), 0.0)
            d = _dot((a * a).astype(BF16), w_down_v[chunk, :])
            ff = d if ff is None else ff + d
        writebacks[1].start()
        writebacks[2].start()
        y_ref[...] = h + ff

    @pl.when(step > 0)
    def _():
        _out_mlp_kernel(mix_ret_ref, mix_swa_ref, x_ref, w_out_v, gain_ref, w_up_v, w_down_v, y_ref)

    @pl.when(step == pl.num_programs(0) - 1)
    def _():
        for wb in writebacks:
            wb.wait()


def _out_mlp_cast(mix_ret, mix_swa, x2d, gain, w_out, w_up, w_down):
    m = x2d.shape[0]
    tm = min(ROW_TILE, m)
    assert D_MODEL % CAST_CHUNK == 0 and D_FF % CAST_CHUNK == 0
    row = lambda w: pl.BlockSpec((tm, w), lambda i: (i, 0))
    full = lambda a: pl.BlockSpec(a.shape, lambda i: (0, 0), pipeline_mode=pl.Buffered(1))
    hbm = pl.BlockSpec(memory_space=pl.ANY)
    return pl.pallas_call(
        _out_mlp_cast_kernel,
        grid=(m // tm,),
        in_specs=[row(RET_WIDTH), row(SWA_WIDTH), row(D_MODEL), full(gain), hbm, hbm, hbm],
        out_specs=[row(D_MODEL), hbm, hbm, hbm],
        out_shape=[jax.ShapeDtypeStruct((m, D_MODEL), F32),
                   jax.ShapeDtypeStruct(w_out.shape, BF16),
                   jax.ShapeDtypeStruct(w_up.shape, BF16),
                   jax.ShapeDtypeStruct(w_down.shape, BF16)],
        scratch_shapes=[
            pltpu.VMEM(w_out.shape, BF16), pltpu.VMEM(w_up.shape, BF16), pltpu.VMEM(w_down.shape, BF16),
            pltpu.VMEM((D_MODEL // CAST_CHUNK, CAST_CHUNK, D_MODEL), F32),
            pltpu.VMEM((2, D_MODEL, CAST_CHUNK), F32),
            pltpu.VMEM((2, CAST_CHUNK, D_MODEL), F32),
            pltpu.SemaphoreType.DMA((3, 2)),
            pltpu.SemaphoreType.DMA((3,)),
        ],
        compiler_params=pltpu.CompilerParams(
            dimension_semantics=("arbitrary",), vmem_limit_bytes=VMEM_LIMIT),
        name="out_mlp_cast",
    )(mix_ret, mix_swa, x2d, gain, w_out, w_up, w_down)


def _split_pair_rows(x, first):
    return jnp.concatenate([jnp.where(first, x, 0.0), jnp.where(first, 0.0, x)], axis=0).astype(BF16)


def _softmax_sink_pv(s, sink_wide, v_t):
    m = jnp.maximum(jnp.max(s, axis=-1, keepdims=True), sink_wide)
    p = jnp.exp(s - jnp.concatenate([m, m], axis=1))
    denom = jnp.sum(p, axis=-1, keepdims=True) + jnp.exp(sink_wide - m)
    return _dot_nt(p.astype(BF16), v_t) / denom


def _softmax_sink_pv_t(s_t, sink_lanes, v_t):
    m = jnp.maximum(jnp.max(s_t, axis=0, keepdims=True), sink_lanes)
    p = jnp.exp(s_t - m)
    denom = jnp.sum(p, axis=0, keepdims=True) + jnp.exp(sink_lanes - m)
    return _dot(v_t, p.astype(BF16)) / denom


def _prompt_consts(intra_ref, qdec_ref, kdec_ref, sdec_ref, bias_ref):
    r = _iota((LANES, LANES), 0)
    lane2 = _iota((LANES, LANES), 1) >= HEAD_DIM
    rf = r.astype(F32)
    ri = _iota((LANES, 2 * LANES), 0)
    ci = _iota((LANES, 2 * LANES), 1)
    diff = (ri - (ci & (LANES - 1))).astype(F32)
    for p in range(N_PAIRS):
        lg = _pair_const(LOG_DECAY, p, lane2)
        qdec_ref[p] = jnp.exp(lg * (rf + 1.0))
        kdec_ref[p] = jnp.exp(lg * (RET_CHUNK - 1.0 - rf))
        sdec_ref[p] = jnp.exp(_pair_const(LOG_DECAY, p, r >= HEAD_DIM) * float(RET_CHUNK))
        lg2 = _pair_const(LOG_DECAY, p, ci >= LANES)
        intra_ref[p] = jnp.where(diff >= 0.0, jnp.exp(lg2 * jnp.maximum(diff, 0.0)), 0.0)
    cols = SWA_GROUP * WINDOW
    kb = _iota((2 * WINDOW, cols), 0)
    cb = _iota((2 * WINDOW, cols), 1)
    grp = cb >> 7
    dist = WINDOW + (cb & (WINDOW - 1)) - kb
    valid = (dist >= 0) & (dist < WINDOW)
    distf = dist.astype(F32)
    for j in range(N_SWA_KV):
        sl = [ALIBI_SLOPES[SWA_GROUP * j + g] for g in range(SWA_GROUP)]
        slope = jnp.where(grp == 0, sl[0], jnp.where(grp == 1, sl[1], jnp.where(grp == 2, sl[2], sl[3])))
        b = jnp.where(valid, -(slope.astype(F32) * distf), NEG_INF)
        bias_ref[0, j] = b
        bias_ref[1, j] = jnp.where(kb >= WINDOW, b, NEG_INF)


def _prompt_layer_kernel(sinks_ref, x_ref, x_next_ref, gain_mix_ref, w_in_ref, qg_ref, kg_ref,
                         w_out_hbm, gain_ffn_ref, w_up_hbm, w_down_hbm,
                         y_ref, ret_ref, kwin_ref, vwin_ref,
                         main_ref, qn_ref, kn_ref, vs_ref, mix_ref, hb_ref,
                         state_ref, prevk_ref, prevv_ref,
                         intra_ref, qdec_ref, kdec_ref, sdec_ref, bias_ref,
                         w_out_ref, w_up_ref, w_down_ref, w_sem):
    t = pl.program_id(1)
    step = pl.program_id(0) * pl.num_programs(1) + t
    cur = step % 2
    nxt = 1 - cur

    late_weights = [pltpu.make_async_copy(src, dst, w_sem.at[i]) for i, (src, dst) in enumerate(
        [(w_out_hbm, w_out_ref), (w_up_hbm, w_up_ref), (w_down_hbm, w_down_ref)])]

    @pl.when(step == 0)
    def _():
        for copy in late_weights:
            copy.start()
        _prompt_consts(intra_ref, qdec_ref, kdec_ref, sdec_ref, bias_ref)
        _in_proj_kernel(x_ref, gain_mix_ref, w_in_ref, qg_ref, kg_ref,
                        main_ref.at[0], qn_ref.at[0], kn_ref.at[0], vs_ref.at[0])

    @pl.when(t == 0)
    def _():
        state_ref[...] = jnp.zeros_like(state_ref)
        prevk_ref[...] = jnp.zeros_like(prevk_ref)
        prevv_ref[...] = jnp.zeros_like(prevv_ref)

    first = _first_half()
    ones_bd = _ones_block_diag()
    bd_mask = (_iota((LANES, LANES), 0) >= HEAD_DIM) == (_iota((LANES, LANES), 1) >= HEAD_DIM)

    n_chunks = PROMPT_TILE // RET_CHUNK
    assert n_chunks == 4
    plan = [dict(a=[], b=[]),
            dict(a=["m0", "m1"], b=[]),
            dict(a=["m2", "m3"], b=[]),
            dict(a=["v", "k"], b=["q0", "q1"])]
    a0 = MAIN_WIDTH
    attn_cols = {"v": (a0 + SWA_WIDTH + SWA_KV_WIDTH, SWA_KV_WIDTH), "k": (a0 + SWA_WIDTH, SWA_KV_WIDTH)}
    attn_cols.update({"q%d" % i: (a0 + Q_BLOCK * i, Q_BLOCK) for i in range(SWA_WIDTH // Q_BLOCK)})
    cols = lambda base, p: slice(base + p * LANES, base + (p + 1) * LANES)

    def project(items):
        raw = {}
        for it in items:
            if it[0] == "m":
                cb = int(it[1:])
                blk = slice(cb * PROJ_BLOCK, (cb + 1) * PROJ_BLOCK)
                main_ref[nxt, :, blk] = _dot(hb_ref[...], w_in_ref[:, blk])
            else:
                lo, width = attn_cols[it]
                raw[it] = _dot(hb_ref[...], w_in_ref[:, lo:lo + width])
        return raw

    def attn_scales(raw):
        return {name: [_head_rms_scale(val[:, cols(0, cq)], ones_bd) for cq in range(val.shape[1] // LANES)]
                for name, val in raw.items() if name != "v"}

    def store_attn(raw, scales):
        for name, val in raw.items():
            if name == "v":
                vs_ref[nxt] = val
            elif name == "k":
                kn_ref[nxt] = (val * scales[name][0]) * kg_ref[...]
            else:
                base = attn_cols[name][0] - a0
                for cq in range(val.shape[1] // LANES):
                    qn_ref[nxt, :, cols(base, cq)] = (val[:, cols(0, cq)] * scales[name][cq]) * qg_ref[...]

    pairs = range(N_PAIRS)
    kvs = range(N_SWA_KV)

    def stage1(c):
        rows = slice(c * RET_CHUNK, (c + 1) * RET_CHUNK)
        q = [main_ref[cur, rows, cols(0, p)] for p in pairs]
        k = [main_ref[cur, rows, cols(RET_WIDTH, p)] * K_SCALE for p in pairs]
        v = [main_ref[cur, rows, cols(2 * RET_WIDTH, p)] for p in pairs]
        state = [state_ref[p] for p in pairs]
        kc = kn_ref[cur, rows, :]
        k_sw = pltpu.roll(kc, HEAD_DIM, axis=1)
        v_t = vs_ref[cur, rows, :].T
        is_first = (t == 0).astype(jnp.int32) if c == 0 else 0
        k_dup =[(jnp.where(first, kc, k_sw) if j == 0 else jnp.where(first, k_sw, kc)).astype(BF16)
                 for j in kvs]
        v_tj = [v_t[j * HEAD_DIM:(j + 1) * HEAD_DIM].astype(BF16) for j in kvs]
        q_st = []
        for j in kvs:
            pieces = []
            for g in range(SWA_GROUP):
                qc = qn_ref[cur, rows, cols(0, 2 * j + g // 2)]
                pieces.append(jnp.where(first, qc, 0.0) if g % 2 == 0 else jnp.where(first, 0.0, qc))
            q_st.append(jnp.concatenate(pieces, axis=0).astype(BF16))

        s = [_dot_nt(q[p].astype(BF16), _split_pair_rows(k[p], first)) for p in pairs]
        s_t = [_dot_nt(jnp.concatenate([prevk_ref[j], k_dup[j]], axis=0), q_st[j]) for j in kvs]
        cross = [_dot((q[p] * qdec_ref[p]).astype(BF16), state[p].astype(BF16)) for p in pairs]
        upd = [_dot((k[p] * kdec_ref[p]).T.astype(BF16), v[p].astype(BF16)) for p in pairs]
        return dict(rows=rows, v=v, state=state, k_dup=k_dup, v_tj=v_tj, is_first=is_first,
                    s=s, s_t=s_t, cross=cross, upd=upd)

    def stage2(st):
        st["o"] = [_dot((st["s"][p] * intra_ref[p]).astype(BF16), _split_pair_rows(st["v"][p], first))
                   + st["cross"][p] for p in pairs]
        st["o_t"] = []
        for j in kvs:
            sink_lanes = jnp.concatenate(
                [jnp.full((1, WINDOW), sinks_ref[SWA_GROUP * j + g], F32) for g in range(SWA_GROUP)],
                axis=1)
            v_cat = jnp.concatenate([prevv_ref[j], st["v_tj"][j]], axis=1)
            st["o_t"].append(_softmax_sink_pv_t(st["s_t"][j] * K_SCALE + bias_ref[st["is_first"], j],
                                                sink_lanes, v_cat))
        for p in pairs:
            state_ref[p] = st["state"][p] * sdec_ref[p] + jnp.where(bd_mask, st["upd"][p], 0.0)
        for j in kvs:
            prevk_ref[j] = st["k_dup"][j]
            prevv_ref[j] = st["v_tj"][j]

    def stage3(st, raw):
        rows = st["rows"]
        scale = [_head_rms_scale(st["o"][p], ones_bd) for p in pairs]
        raw_scales = attn_scales(raw)
        for p in pairs:
            g = main_ref[cur, rows, cols(3 * RET_WIDTH, p)]
            mix_ref[rows, cols(0, p)] = (st["o"][p] * scale[p] * _silu(g)).astype(BF16)
        store_attn(raw, raw_scales)
        for j in kvs:
            o_t = st["o_t"][j]
            for half in range(2):
                pair_t = jnp.concatenate([o_t[:, (2 * half) * WINDOW:(2 * half + 1) * WINDOW],
                                          o_t[:, (2 * half + 1) * WINDOW:(2 * half + 2) * WINDOW]], axis=0)
                mix_ref[rows, cols(RET_WIDTH, 2 * j + half)] = pair_t.T.astype(BF16)

    xn = x_next_ref[...]
    hb_ref[...] = ((xn * lax.rsqrt(jnp.mean(xn * xn, axis=-1, keepdims=True) + EPS)) * gain_mix_ref[...]).astype(BF16)
    for c in range(n_chunks):
        st = stage1(c)
        raw = project(plan[c]["a"])
        stage2(st)
        raw.update(project(plan[c]["b"]))
        stage3(st, raw)

    @pl.when(t == pl.num_programs(1) - 1)
    def _():
        for p in range(N_PAIRS):
            s = state_ref[p]
            ret_ref[2 * p] = s[:HEAD_DIM, :HEAD_DIM]
            ret_ref[2 * p + 1] = s[HEAD_DIM:, HEAD_DIM:]
        last = slice(PROMPT_TILE - WINDOW, PROMPT_TILE)
        kwin_ref[...] = kn_ref[cur, last, :].T
        vwin_ref[...] = vs_ref[cur, last, :].T

    @pl.when(step == 0)
    def _():
        for copy in late_weights:
            copy.wait()

    _out_mlp_kernel(mix_ref.at[:, pl.ds(0, RET_WIDTH)], mix_ref.at[:, pl.ds(RET_WIDTH, SWA_WIDTH)], x_ref,
                    w_out_ref, gain_ffn_ref, w_up_ref, w_down_ref, y_ref)


def _prompt_layer(sinks, x2d, gain_mix, w_in_bf, qg, kg, w_out_bf, gain_ffn, w_up_bf, w_down_bf, batch, seq):
    nt = seq // PROMPT_TILE
    last_tile = batch * nt - 1
    row = lambda w: pl.BlockSpec((PROMPT_TILE, w), lambda b, t: (b * nt + t, 0))
    next_row = pl.BlockSpec((PROMPT_TILE, D_MODEL), lambda b, t: (jnp.minimum(b * nt + t + 1, last_tile), 0))
    full = lambda a: pl.BlockSpec(a.shape, lambda b, t: (0, 0), pipeline_mode=pl.Buffered(1))
    in_hbm = pl.BlockSpec(memory_space=pl.ANY)
    return pl.pallas_call(
        _prompt_layer_kernel,
        grid=(batch, nt),
        in_specs=[pl.BlockSpec(memory_space=pltpu.SMEM), row(D_MODEL), next_row,
                  full(gain_mix), full(w_in_bf), full(qg), full(kg),
                  in_hbm, full(gain_ffn), in_hbm, in_hbm],
        out_specs=[row(D_MODEL),
                   pl.BlockSpec((None, N_RET_HEADS, HEAD_DIM, HEAD_DIM), lambda b, t: (b, 0, 0, 0)),
                   pl.BlockSpec((None, SWA_KV_WIDTH, WINDOW), lambda b, t: (b, 0, 0)),
                   pl.BlockSpec((None, SWA_KV_WIDTH, WINDOW), lambda b, t: (b, 0, 0))],
        out_shape=[jax.ShapeDtypeStruct((batch * seq, D_MODEL), F32),
                   jax.ShapeDtypeStruct((batch, N_RET_HEADS, HEAD_DIM, HEAD_DIM), F32),
                   jax.ShapeDtypeStruct((batch, SWA_KV_WIDTH, WINDOW), F32),
                   jax.ShapeDtypeStruct((batch, SWA_KV_WIDTH, WINDOW), F32)],
        scratch_shapes=[
            pltpu.VMEM((2, PROMPT_TILE, MAIN_WIDTH), F32),
            pltpu.VMEM((2, PROMPT_TILE, SWA_WIDTH), F32),
            pltpu.VMEM((2, PROMPT_TILE, SWA_KV_WIDTH), F32),
            pltpu.VMEM((2, PROMPT_TILE, SWA_KV_WIDTH), F32),
            pltpu.VMEM((PROMPT_TILE, MIX_WIDTH), BF16),
            pltpu.VMEM((PROMPT_TILE, D_MODEL), BF16),
            pltpu.VMEM((N_PAIRS, LANES, LANES), F32),
            pltpu.VMEM((N_SWA_KV, WINDOW, LANES), BF16),
            pltpu.VMEM((N_SWA_KV, HEAD_DIM, WINDOW), BF16),
            pltpu.VMEM((N_PAIRS, LANES, 2 * LANES), F32),
            pltpu.VMEM((N_PAIRS, LANES, LANES), F32),
            pltpu.VMEM((N_PAIRS, LANES, LANES), F32),
            pltpu.VMEM((N_PAIRS, LANES, LANES), F32),
            pltpu.VMEM((2, N_SWA_KV, 2 * WINDOW, SWA_GROUP * WINDOW), F32),
            pltpu.VMEM(w_out_bf.shape, BF16),
            pltpu.VMEM(w_up_bf.shape, BF16),
            pltpu.VMEM(w_down_bf.shape, BF16),
            pltpu.SemaphoreType.DMA((3,)),
        ],
        compiler_params=pltpu.CompilerParams(
            dimension_semantics=("arbitrary", "arbitrary"), vmem_limit_bytes=PROMPT_VMEM_LIMIT),
        name="prompt_layer",
    )(sinks, x2d, x2d, gain_mix, w_in_bf, qg, kg, w_out_bf, gain_ffn, w_up_bf, w_down_bf)


def _decode_ret_kernel(dec_seq, nb, qdec_ref, kdec_ref, sdec_ref, intra_ref,
                       q_ref, k_ref, v_ref, g_ref, st_ref,
                       mix_ref, st_out_ref,
                       qt_ref, kt_ref, vt_ref, qdt_ref, kdt_ref, o_ref):
    pair = pl.program_id(0)
    halves = [slice(0, HEAD_DIM), slice(HEAD_DIM, 2 * HEAD_DIM)]
    for l in range(dec_seq):
        rows = pl.ds(l, nb, stride=dec_seq)
        q_t = q_ref[rows, :].T
        k_t = (k_ref[rows, :] * K_SCALE).T
        qt_ref[l] = q_t
        kt_ref[l] = k_t
        vt_ref[l] = v_ref[rows, :].T
        for hh in range(2):
            qdt_ref[l, halves[hh], :] = q_t[halves[hh]] * qdec_ref[2 * pair + hh, l]
            kdt_ref[l, halves[hh], :] = k_t[halves[hh]] * kdec_ref[2 * pair + hh, l]

    e_blk = HEAD_DIM // 2
    for hh in range(2):
        h = 2 * pair + hh
        hs = halves[hh]
        for l in range(dec_seq):
            acc = None
            for m in range(l + 1):
                sc = jnp.sum(qt_ref[l, hs, :] * kt_ref[m, hs, :], axis=0, keepdims=True) * intra_ref[h, l - m]
                term = sc * vt_ref[m, hs, :]
                acc = term if acc is None else acc + term
            o_ref[l, hs, :] = acc
        for eb in range(HEAD_DIM // e_blk):
            es = slice(eb * e_blk, (eb + 1) * e_blk)
            erows = slice(hh * HEAD_DIM + eb * e_blk, hh * HEAD_DIM + (eb + 1) * e_blk)

            def body(d, accs, hh=hh, h=h, es=es, erows=erows):
                s_d = st_ref[hh, d, es, :]
                row = pl.ds(hh * HEAD_DIM + d, 1)
                upd = s_d * sdec_ref[h]
                new_accs = []
                for l in range(dec_seq):
                    new_accs.append(accs[l] + qdt_ref[l, row, :] * s_d)
                    upd = upd + kdt_ref[l, row, :] * vt_ref[l, erows, :]
                st_out_ref[hh, d, es, :] = upd
                return tuple(new_accs)

            zero = jnp.zeros((e_blk, nb), F32)
            accs = lax.fori_loop(0, HEAD_DIM, body, tuple(zero for _ in range(dec_seq)), unroll=2)
            for l in range(dec_seq):
                o_ref[l, erows, :] = o_ref[l, erows, :] + accs[l]

    for l in range(dec_seq):
        o = o_ref[l]
        normed = []
        for hh in range(2):
            oh = o[halves[hh]]
            normed.append(oh * lax.rsqrt(jnp.mean(oh * oh, axis=0, keepdims=True) + EPS))
        rows = pl.ds(l, nb, stride=dec_seq)
        mix_ref[rows, :] = jnp.concatenate(normed, axis=0).T * _silu(g_ref[rows, :])


def _decode_ret(main, state_t, dec_seq):
    nb = state_t.shape[-1]
    m = main.shape[0]
    assert nb == LANES and m == nb * dec_seq
    steps = [j for j in range(dec_seq)]
    tab = lambda f: jnp.asarray([[f(h, j) for j in steps] for h in range(N_RET_HEADS)], F32)
    qdec = tab(lambda h, j: math.exp(LOG_DECAY[h] * (j + 1.0)))
    kdec = tab(lambda h, j: math.exp(LOG_DECAY[h] * (dec_seq - 1.0 - j)))
    intra = tab(lambda h, j: math.exp(LOG_DECAY[h] * j))
    sdec = jnp.asarray([math.exp(LOG_DECAY[h] * dec_seq) for h in range(N_RET_HEADS)], F32)
    smem = pl.BlockSpec(memory_space=pltpu.SMEM)
    col = lambda base: pl.BlockSpec((m, LANES), lambda p: (0, base + p))
    st_spec = pl.BlockSpec((2, HEAD_DIM, HEAD_DIM, nb), lambda p: (p, 0, 0, 0))
    stage = pltpu.VMEM((dec_seq, LANES, nb), F32)
    return pl.pallas_call(
        functools.partial(_decode_ret_kernel, dec_seq, nb),
        grid=(N_PAIRS,),
        in_specs=[smem, smem, smem, smem,
                  col(0), col(N_PAIRS), col(2 * N_PAIRS), col(3 * N_PAIRS), st_spec],
        out_specs=[pl.BlockSpec((m, LANES), lambda p: (0, p)), st_spec],
        out_shape=[jax.ShapeDtypeStruct((m, RET_WIDTH), F32),
                   jax.ShapeDtypeStruct(state_t.shape, F32)],
        scratch_shapes=[stage, stage, stage, stage, stage, stage],
        compiler_params=pltpu.CompilerParams(
            dimension_semantics=("arbitrary",), vmem_limit_bytes=VMEM_LIMIT),
        name="decode_ret",
    )(qdec, kdec, sdec, intra, main, main, main, main, state_t)


DEC_ROWS = 128
DEC_UNROLL = 8


def _decode_attn_consts(dec_seq, bias_ref):
    shift = dec_seq.bit_length() - 1
    rows = N_SWA_HEADS * dec_seq
    rb = _iota((rows, WINDOW), 0)
    cb = _iota((rows, WINDOW), 1)
    head = rb >> shift
    i = rb & (dec_seq - 1)
    slope = jnp.zeros((rows, WINDOW), F32)
    for h in range(N_SWA_HEADS):
        slope = jnp.where(head == h, ALIBI_SLOPES[h], slope)
    bias_ref[0] = jnp.where(cb > i, -(slope * (WINDOW + i - cb).astype(F32)), NEG_INF)
    m = cb & (dec_seq - 1)
    bias_ref[1] = jnp.where(m <= i, -(slope * (i - m).astype(F32)), NEG_INF)


def _decode_attn_kernel(dec_seq, sinks_ref, qn_ref, kn_ref, vs_ref, kt_ref, vt_ref,
                        mix_ref, kt_out_ref, vt_out_ref,
                        bias_ref, qbd_ref, oblk_ref, knew_ref, vnew_ref, knt_ref, vst_ref):
    @pl.when(pl.program_id(0) == 0)
    def _():
        _decode_attn_consts(dec_seq, bias_ref)

    first = _first_half()
    shift = dec_seq.bit_length() - 1

    kn_t = kn_ref[...].T
    vs_t = vs_ref[...].T
    knt_ref[...] = kn_t.astype(BF16)
    vst_ref[...] = vs_t.astype(BF16)
    for bb in range(DEC_GROUP):
        sh = (WINDOW - dec_seq - bb * dec_seq) % LANES
        knew_ref[bb] = pltpu.roll(kn_t, sh, axis=1) if sh else kn_t
        vnew_ref[bb] = pltpu.roll(vs_t, sh, axis=1) if sh else vs_t
    qn = qn_ref[...]
    qn_sw = pltpu.roll(qn, HEAD_DIM, axis=1)
    for h in range(N_SWA_HEADS):
        kv_half = h // SWA_GROUP
        if (h % 2) == kv_half:
            src = qn[:, (h // 2) * LANES:(h // 2 + 1) * LANES]
        else:
            col = (h + 1) // 2
            src = qn_sw[:, col * LANES:(col + 1) * LANES]
        qbd_ref[h] = jnp.where(first, src, 0.0) if kv_half == 0 else jnp.where(first, 0.0, src)

    sink_rows = jnp.concatenate(
        [jnp.full((dec_seq, LANES), sinks_ref[h], F32) for h in range(N_SWA_HEADS)], axis=0)
    col_batch = _iota((N_SWA_HEADS * dec_seq, LANES), 1) >> shift
    keep_old = _iota((1, LANES), 1) < WINDOW - dec_seq

    def per_batches(i, carry):
        bs = [i * DEC_UNROLL + u for u in range(DEC_UNROLL)]
        rows = [pl.ds(pl.multiple_of(b * dec_seq, dec_seq), dec_seq) for b in bs]
        k_old = [kt_ref[b] for b in bs]
        v_old = [vt_ref[b] for b in bs]
        q_st = [jnp.concatenate([qbd_ref[h, r, :] for h in range(N_SWA_HEADS)], axis=0).astype(BF16)
                for r in rows]
        s = [_dot(q_st[u], jnp.concatenate([k_old[u].astype(BF16), knt_ref[...]], axis=1))
             for u in range(DEC_UNROLL)]
        o = []
        for u, b in enumerate(bs):
            bias = jnp.concatenate([bias_ref[0], jnp.where(col_batch == b, bias_ref[1], NEG_INF)], axis=1)
            w_v = jnp.concatenate([v_old[u].astype(BF16), vst_ref[...]], axis=1)
            o.append(_softmax_sink_pv(s[u] * K_SCALE + bias, sink_rows, w_v))
        for u, b in enumerate(bs):
            for h in range(N_SWA_HEADS):
                oblk_ref[h, rows[u], :] = o[u][h * dec_seq:(h + 1) * dec_seq]
            kt_out_ref[b] = jnp.where(keep_old, pltpu.roll(k_old[u], LANES - dec_seq, axis=1), knew_ref[b])
            vt_out_ref[b] = jnp.where(keep_old, pltpu.roll(v_old[u], LANES - dec_seq, axis=1), vnew_ref[b])
        return carry

    lax.fori_loop(0, DEC_GROUP // DEC_UNROLL, per_batches, 0)

    y1 = jnp.where(first, oblk_ref[3], oblk_ref[4])
    moved = pltpu.roll(jnp.concatenate([oblk_ref[1], y1, oblk_ref[6], oblk_ref[6]], axis=1), HEAD_DIM, axis=1)
    outs = [
        jnp.where(first, oblk_ref[0], moved[:, 0:LANES]),
        jnp.where(first, oblk_ref[2], moved[:, LANES:2 * LANES]),
        jnp.where(first, moved[:, 2 * LANES:3 * LANES], oblk_ref[5]),
        jnp.where(first, moved[:, 3 * LANES:4 * LANES], oblk_ref[7]),
    ]
    for c in range(SWA_WIDTH // LANES):
        mix_ref[:, c * LANES:(c + 1) * LANES] = outs[c].astype(BF16)


def _decode_attn(sinks, qn, kn, vs, k_t, v_t, dec_seq):
    nb = k_t.shape[0]
    assert DEC_GROUP * dec_seq == DEC_ROWS and nb % DEC_GROUP == 0 and dec_seq & (dec_seq - 1) == 0
    assert k_t.shape[1:] == (SWA_KV_WIDTH, WINDOW)
    row = lambda w: pl.BlockSpec((DEC_ROWS, w), lambda i: (i, 0))
    cache = pl.BlockSpec((DEC_GROUP, SWA_KV_WIDTH, WINDOW), lambda i: (i, 0, 0))
    return pl.pallas_call(
        functools.partial(_decode_attn_kernel, dec_seq),
        grid=(nb // DEC_GROUP,),
        in_specs=[pl.BlockSpec(memory_space=pltpu.SMEM),
                  row(SWA_WIDTH), row(SWA_KV_WIDTH), row(SWA_KV_WIDTH), cache, cache],
        out_specs=[row(SWA_WIDTH), cache, cache],
        out_shape=[jax.ShapeDtypeStruct((nb * dec_seq, SWA_WIDTH), BF16),
                   jax.ShapeDtypeStruct(k_t.shape, F32),
                   jax.ShapeDtypeStruct(v_t.shape, F32)],
        scratch_shapes=[
            pltpu.VMEM((2, N_SWA_HEADS * dec_seq, WINDOW), F32),
            pltpu.VMEM((N_SWA_HEADS, DEC_ROWS, LANES), F32),
            pltpu.VMEM((N_SWA_HEADS, DEC_ROWS, LANES), F32),
            pltpu.VMEM((DEC_GROUP, SWA_KV_WIDTH, LANES), F32),
            pltpu.VMEM((DEC_GROUP, SWA_KV_WIDTH, LANES), F32),
            pltpu.VMEM((SWA_KV_WIDTH, DEC_ROWS), BF16),
            pltpu.VMEM((SWA_KV_WIDTH, DEC_ROWS), BF16),
        ],
        compiler_params=pltpu.CompilerParams(
            dimension_semantics=("arbitrary",), vmem_limit_bytes=VMEM_LIMIT),
        name="decode_attn",
    )(sinks, qn, kn, vs, k_t, v_t)


def kernel(x_prompt, x_sample, state_ret, cache_swa_k, cache_swa_v, norm_mix_gain, w_in, q_norm_gain,
           k_norm_gain, attn_sinks, w_out, norm_ffn_gain, w_up, w_down):
    batch, seq, d = x_prompt.shape
    nb, dec_seq, _ = x_sample.shape
    wb = cache_swa_k.shape[1]
    assert d == D_MODEL and seq % PROMPT_TILE == 0 and wb == WINDOW

    gain_mix = norm_mix_gain.reshape(1, D_MODEL)
    gain_ffn = norm_ffn_gain.reshape(1, D_MODEL)
    qg = jnp.tile(q_norm_gain, 2).reshape(1, LANES)
    kg = jnp.tile(k_norm_gain, 2).reshape(1, LANES)

    def from_key_minor(a_t):
        return jnp.transpose(a_t.reshape(a_t.shape[0], N_SWA_KV, HEAD_DIM, WINDOW), (0, 3, 1, 2))

    def to_key_minor(a):
        return jnp.transpose(a, (0, 2, 3, 1)).reshape(a.shape[0], SWA_KV_WIDTH, WINDOW)

    xs = x_sample.reshape(nb * dec_seq, D_MODEL)
    main_s, qn_s, kn_s, vs_s, w_in_bf = _in_proj_cast(xs, gain_mix, w_in, qg, kg)
    mix_ret_s, state_t = _decode_ret(main_s, jnp.transpose(state_ret, (1, 2, 3, 0)), dec_seq)
    mix_swa_s, k_t, v_t = _decode_attn(attn_sinks, qn_s, kn_s, vs_s,
                                       to_key_minor(cache_swa_k), to_key_minor(cache_swa_v), dec_seq)
    y_s, w_out_bf, w_up_bf, w_down_bf = _out_mlp_cast(mix_ret_s, mix_swa_s, xs, gain_ffn, w_out, w_up, w_down)

    xp = x_prompt.reshape(batch * seq, D_MODEL)
    y_p, ret_p, kwin_t, vwin_t = _prompt_layer(attn_sinks, xp, gain_mix, w_in_bf, qg, kg, w_out_bf, gain_ffn,
                                               w_up_bf, w_down_bf, batch, seq)
    y_p = y_p.reshape(batch, seq, D_MODEL)

    return (y_p, y_s.reshape(nb, dec_seq, D_MODEL), ret_p, from_key_minor(kwin_t), from_key_minor(vwin_t),
            jnp.transpose(state_t, (3, 0, 1, 2)), from_key_minor(k_t), from_key_minor(v_t))
```

```python
import functools
import math

import jax
import jax.numpy as jnp
from jax import lax
from jax.experimental import pallas as pl
from jax.experimental.pallas import tpu as pltpu

F32 = jnp.float32
BF16 = jnp.bfloat16

D_MODEL = 1024
HEAD_DIM = 64
N_RET_HEADS = 8
N_SWA_HEADS = 8
N_SWA_KV = 2
SWA_GROUP = N_SWA_HEADS // N_SWA_KV
RET_WIDTH = N_RET_HEADS * HEAD_DIM
SWA_WIDTH = N_SWA_HEADS * HEAD_DIM
SWA_KV_WIDTH = N_SWA_KV * HEAD_DIM
MAIN_WIDTH = 4 * RET_WIDTH
IN_WIDTH = MAIN_WIDTH + SWA_WIDTH + 2 * SWA_KV_WIDTH
MIX_WIDTH = RET_WIDTH + SWA_WIDTH
D_FF = 4 * D_MODEL
WINDOW = 128
RET_CHUNK = 128
EPS = 1e-6
NEG_INF = -1e30

LANES = 128
N_PAIRS = N_RET_HEADS // 2
LOG_DECAY = [math.log(1.0 - 2.0 ** (-5.0 - h)) for h in range(N_RET_HEADS)]
ALIBI_SLOPES = [2.0 ** (-8.0 * (h + 1) / N_SWA_HEADS) for h in range(N_SWA_HEADS)]
K_SCALE = HEAD_DIM ** -0.5

MXU_WIDTH = 256
PROJ_BLOCK = 2 * MXU_WIDTH
Q_BLOCK = MXU_WIDTH
PROMPT_TILE = 512
DEC_GROUP = 16
VMEM_LIMIT = 56 * 1024 * 1024
PROMPT_VMEM_LIMIT = 62 * 1024 * 1024


def _dot(a, b):
    return jnp.dot(a, b, preferred_element_type=F32)


def _dot_nt(a, b):
    return lax.dot_general(a, b, (((1,), (1,)), ((), ())), preferred_element_type=F32)


def _iota(shape, dim):
    return lax.broadcasted_iota(jnp.int32, shape, dim)


def _ones_block_diag():
    same = ((_iota((2 * LANES, LANES), 0) >> 6) & 1) == (_iota((2 * LANES, LANES), 1) >> 6)
    return jnp.where(same, 1.0, 0.0).astype(BF16)


def _head_sumsq(x, ones_bd):
    x2 = x * x
    hi = x2.astype(BF16)
    lo = (x2 - hi.astype(F32)).astype(BF16)
    return _dot(jnp.concatenate([hi, lo], axis=1), ones_bd)


def _head_rms_scale(x, ones_bd):
    return lax.rsqrt(_head_sumsq(x, ones_bd) * (1.0 / HEAD_DIM) + EPS)


def _head_rms_scales(xs, ones_bd):
    if not xs:
        return []
    parts = []
    for x in xs:
        x2 = x * x
        hi = x2.astype(BF16)
        parts.append(jnp.concatenate([hi, (x2 - hi.astype(F32)).astype(BF16)], axis=1))
    total = _dot(jnp.concatenate(parts, axis=0) if len(parts) > 1 else parts[0], ones_bd)
    out, lo = [], 0
    for x in xs:
        out.append(lax.rsqrt(total[lo:lo + x.shape[0]] * (1.0 / HEAD_DIM) + EPS))
        lo += x.shape[0]
    return out


def _first_half():
    return _iota((1, LANES), 1) < HEAD_DIM


def _pair_const(values, pair, lane_is_second):
    return jnp.where(lane_is_second, values[2 * pair + 1], values[2 * pair]).astype(F32)


def _pair_gain(gain_ref):
    g = gain_ref[...]
    return jnp.concatenate([g, g], axis=1)


def _silu(g):
    return g * (1.0 / (1.0 + jnp.exp(-g)))


def _in_proj_kernel(x_ref, gain_ref, w_ref, qg_ref, kg_ref, main_ref, qn_ref, kn_ref, vs_ref, fetch=None):
    x = x_ref[...]
    ms = jnp.mean(x * x, axis=-1, keepdims=True)
    hb = ((x * lax.rsqrt(ms + EPS)) * gain_ref[...]).astype(BF16)
    ones_bd = _ones_block_diag()

    def project(lo, width):
        if fetch is not None:
            fetch(lo, width)
        return _dot(hb, w_ref[:, lo:lo + width])

    for c in range(MAIN_WIDTH // PROJ_BLOCK):
        main_ref[:, c * PROJ_BLOCK:(c + 1) * PROJ_BLOCK] = project(c * PROJ_BLOCK, PROJ_BLOCK)
    qs = project(MAIN_WIDTH, SWA_WIDTH)
    k0 = MAIN_WIDTH + SWA_WIDTH
    ks = project(k0, SWA_KV_WIDTH)
    vs_ref[...] = project(k0 + SWA_KV_WIDTH, SWA_KV_WIDTH)
    q_cols = [qs[:, c * LANES:(c + 1) * LANES] for c in range(SWA_WIDTH // LANES)]
    scales = _head_rms_scales(q_cols + [ks], ones_bd)
    for c, xc in enumerate(q_cols):
        qn_ref[:, c * LANES:(c + 1) * LANES] = (xc * scales[c]) * _pair_gain(qg_ref)
    kn_ref[...] = (ks * scales[-1]) * _pair_gain(kg_ref)


ROW_TILE = 512


W_IN_BLOCKS = [(c * PROJ_BLOCK, PROJ_BLOCK) for c in range(MAIN_WIDTH // PROJ_BLOCK)] + [(MAIN_WIDTH, SWA_WIDTH),
                                                                    (MAIN_WIDTH + SWA_WIDTH, 2 * SWA_KV_WIDTH)]
assert sum(w for _, w in W_IN_BLOCKS) == IN_WIDTH


def _in_proj_cast_kernel(x_ref, gain_ref, w_hbm, qg_ref, kg_ref,
                         main_ref, qn_ref, kn_ref, vs_ref, w_bf_hbm,
                         w_v, stage, in_sem, out_sem):
    step = pl.program_id(0)
    block_in = [pltpu.make_async_copy(w_hbm.at[:, pl.ds(lo, width)], stage.at[:, pl.ds(lo, width)], in_sem.at[i])
                for i, (lo, width) in enumerate(W_IN_BLOCKS)]
    writeback = pltpu.make_async_copy(w_v, w_bf_hbm, out_sem.at[0])

    @pl.when(step == 0)
    def _():
        for copy in block_in:
            copy.start()
        arrived = set()

        def fetch(lo, width):
            for i, (blo, bwidth) in enumerate(W_IN_BLOCKS):
                if blo <= lo < blo + bwidth and i not in arrived:
                    assert lo + width <= blo + bwidth
                    block_in[i].wait()
                    w_v[:, blo:blo + bwidth] = stage[:, blo:blo + bwidth].astype(BF16)
                    arrived.add(i)

        _in_proj_kernel(x_ref, gain_ref, w_v, qg_ref, kg_ref, main_ref, qn_ref, kn_ref, vs_ref, fetch=fetch)
        assert len(arrived) == len(W_IN_BLOCKS)
        writeback.start()

    @pl.when(step > 0)
    def _():
        _in_proj_kernel(x_ref, gain_ref, w_v, qg_ref, kg_ref, main_ref, qn_ref, kn_ref, vs_ref)

    @pl.when(step == pl.num_programs(0) - 1)
    def _():
        writeback.wait()


def _in_proj_cast(x2d, gain, w_in, qg, kg):
    m = x2d.shape[0]
    tm = min(ROW_TILE, m)
    row = lambda w: pl.BlockSpec((tm, w), lambda i: (i, 0))
    full = lambda a: pl.BlockSpec(a.shape, lambda i: (0, 0), pipeline_mode=pl.Buffered(1))
    hbm = pl.BlockSpec(memory_space=pl.ANY)
    return pl.pallas_call(
        _in_proj_cast_kernel,
        grid=(m // tm,),
        in_specs=[row(D_MODEL), full(gain), hbm, full(qg), full(kg)],
        out_specs=[row(MAIN_WIDTH), row(SWA_WIDTH), row(SWA_KV_WIDTH), row(SWA_KV_WIDTH), hbm],
        out_shape=[jax.ShapeDtypeStruct((m, MAIN_WIDTH), F32),
                   jax.ShapeDtypeStruct((m, SWA_WIDTH), F32),
                   jax.ShapeDtypeStruct((m, SWA_KV_WIDTH), F32),
                   jax.ShapeDtypeStruct((m, SWA_KV_WIDTH), F32),
                   jax.ShapeDtypeStruct(w_in.shape, BF16)],
        scratch_shapes=[pltpu.VMEM(w_in.shape, BF16), pltpu.VMEM(w_in.shape, F32),
                        pltpu.SemaphoreType.DMA((len(W_IN_BLOCKS),)), pltpu.SemaphoreType.DMA((1,))],
        compiler_params=pltpu.CompilerParams(
            dimension_semantics=("arbitrary",), vmem_limit_bytes=VMEM_LIMIT),
        name="in_proj_cast",
    )(x2d, gain, w_in, qg, kg)


FF_CHUNK = 1024


def _out_mlp_kernel(mix_ret_ref, mix_swa_ref, x_ref, w_out_ref, gain_ref, w_up_ref, w_down_ref, y_ref):
    h = x_ref[...] + (_dot(mix_ret_ref[...].astype(BF16), w_out_ref[:RET_WIDTH, :])
                      + _dot(mix_swa_ref[...].astype(BF16), w_out_ref[RET_WIDTH:, :]))
    ms = jnp.mean(h * h, axis=-1, keepdims=True)
    hf = ((h * lax.rsqrt(ms + EPS)) * gain_ref[...]).astype(BF16)
    ff = None
    for c in range(D_FF // FF_CHUNK):
        u = _dot(hf, w_up_ref[:, c * FF_CHUNK:(c + 1) * FF_CHUNK])
        a = jnp.maximum(u, 0.0)
        d = _dot((a * a).astype(BF16), w_down_ref[c * FF_CHUNK:(c + 1) * FF_CHUNK, :])
        ff = d if ff is None else ff + d
    y_ref[...] = h + ff


CAST_CHUNK = 512
N_FF_CHUNKS = D_FF // CAST_CHUNK


def _out_mlp_cast_kernel(mix_ret_ref, mix_swa_ref, x_ref, gain_ref, w_out_hbm, w_up_hbm, w_down_hbm,
                         y_ref, w_out_bf_hbm, w_up_bf_hbm, w_down_bf_hbm,
                         w_out_v, w_up_v, w_down_v, stage_out, stage_up, stage_down, in_sem, out_sem):
    step = pl.program_id(0)

    def out_in(i):
        return pltpu.make_async_copy(w_out_hbm.at[pl.ds(i * CAST_CHUNK, CAST_CHUNK), :], stage_out.at[i],
                                     in_sem.at[0, i])

    def up_in(c):
        return pltpu.make_async_copy(w_up_hbm.at[:, pl.ds(c * CAST_CHUNK, CAST_CHUNK)], stage_up.at[c % 2],
                                     in_sem.at[1, c % 2])

    def down_in(c):
        return pltpu.make_async_copy(w_down_hbm.at[pl.ds(c * CAST_CHUNK, CAST_CHUNK), :], stage_down.at[c % 2],
                                     in_sem.at[2, c % 2])

    writebacks = [pltpu.make_async_copy(src, dst, out_sem.at[i]) for i, (src, dst) in enumerate(
        [(w_out_v, w_out_bf_hbm), (w_up_v, w_up_bf_hbm), (w_down_v, w_down_bf_hbm)])]

    @pl.when(step == 0)
    def _():
        n_out = D_MODEL // CAST_CHUNK
        for i in range(n_out):
            out_in(i).start()
        up_in(0).start()
        down_in(0).start()
        for i in range(n_out):
            out_in(i).wait()
            w_out_v[i * CAST_CHUNK:(i + 1) * CAST_CHUNK, :] = stage_out[i].astype(BF16)
        writebacks[0].start()
        h = x_ref[...] + (_dot(mix_ret_ref[...].astype(BF16), w_out_v[:RET_WIDTH, :])
                          + _dot(mix_swa_ref[...].astype(BF16), w_out_v[RET_WIDTH:, :]))
        ms = jnp.mean(h * h, axis=-1, keepdims=True)
        hf = ((h * lax.rsqrt(ms + EPS)) * gain_ref[...]).astype(BF16)
        ff = None
        for c in range(N_FF_CHUNKS):
            chunk = slice(c * CAST_CHUNK, (c + 1) * CAST_CHUNK)
            if c + 1 < N_FF_CHUNKS:
                up_in(c + 1).start()
                down_in(c + 1).start()
            up_in(c).wait()
            down_in(c).wait()
            w_up_v[:, chunk] = stage_up[c % 2].astype(BF16)
            w_down_v[chunk, :] = stage_down[c % 2].astype(BF16)
            a = jnp.maximum(_dot(hf, w_up_v[:, chunk]), 0.0)
            d = _dot((a * a).astype(BF16), w_down_v[chunk, :])
            ff = d if ff is None else ff + d
        writebacks[1].start()
        writebacks[2].start()
        y_ref[...] = h + ff

    @pl.when(step > 0)
    def _():
        _out_mlp_kernel(mix_ret_ref, mix_swa_ref, x_ref, w_out_v, gain_ref, w_up_v, w_down_v, y_ref)

    @pl.when(step == pl.num_programs(0) - 1)
    def _():
        for wb in writebacks:
            wb.wait()


def _out_mlp_cast(mix_ret, mix_swa, x2d, gain, w_out, w_up, w_down):
    m = x2d.shape[0]
    tm = min(ROW_TILE, m)
    assert D_MODEL % CAST_CHUNK == 0 and D_FF % CAST_CHUNK == 0
    row = lambda w: pl.BlockSpec((tm, w), lambda i: (i, 0))
    full = lambda a: pl.BlockSpec(a.shape, lambda i: (0, 0), pipeline_mode=pl.Buffered(1))
    hbm = pl.BlockSpec(memory_space=pl.ANY)
    return pl.pallas_call(
        _out_mlp_cast_kernel,
        grid=(m // tm,),
        in_specs=[row(RET_WIDTH), row(SWA_WIDTH), row(D_MODEL), full(gain), hbm, hbm, hbm],
        out_specs=[row(D_MODEL), hbm, hbm, hbm],
        out_shape=[jax.ShapeDtypeStruct((m, D_MODEL), F32),
                   jax.ShapeDtypeStruct(w_out.shape, BF16),
                   jax.ShapeDtypeStruct(w_up.shape, BF16),
                   jax.ShapeDtypeStruct(w_down.shape, BF16)],
        scratch_shapes=[
            pltpu.VMEM(w_out.shape, BF16), pltpu.VMEM(w_up.shape, BF16), pltpu.VMEM(w_down.shape, BF16),
            pltpu.VMEM((D_MODEL // CAST_CHUNK, CAST_CHUNK, D_MODEL), F32),
            pltpu.VMEM((2, D_MODEL, CAST_CHUNK), F32),
            pltpu.VMEM((2, CAST_CHUNK, D_MODEL), F32),
            pltpu.SemaphoreType.DMA((3, 2)),
            pltpu.SemaphoreType.DMA((3,)),
        ],
        compiler_params=pltpu.CompilerParams(
            dimension_semantics=("arbitrary",), vmem_limit_bytes=VMEM_LIMIT),
        name="out_mlp_cast",
    )(mix_ret, mix_swa, x2d, gain, w_out, w_up, w_down)


def _split_pair_rows(x, first):
    return jnp.concatenate([jnp.where(first, x, 0.0), jnp.where(first, 0.0, x)], axis=0).astype(BF16)


def _softmax_sink_pv(s, sink_wide, v_t):
    m = jnp.maximum(jnp.max(s, axis=-1, keepdims=True), sink_wide)
    p = jnp.exp(s - jnp.concatenate([m, m], axis=1))
    denom = jnp.sum(p, axis=-1, keepdims=True) + jnp.exp(sink_wide - m)
    return _dot_nt(p.astype(BF16), v_t) / denom


def _softmax_sink_pv_t(s_t, sink_lanes, v_t):
    m = jnp.maximum(jnp.max(s_t, axis=0, keepdims=True), sink_lanes)
    p = jnp.exp(s_t - m)
    denom = jnp.sum(p, axis=0, keepdims=True) + jnp.exp(sink_lanes - m)
    return _dot(v_t, p.astype(BF16)) / denom


def _prompt_consts(intra_ref, qdec_ref, kdec_ref, sdec_ref, bias_ref):
    r = _iota((LANES, LANES), 0)
    lane2 = _iota((LANES, LANES), 1) >= HEAD_DIM
    rf = r.astype(F32)
    ri = _iota((LANES, 2 * LANES), 0)
    ci = _iota((LANES, 2 * LANES), 1)
    diff = (ri - (ci & (LANES - 1))).astype(F32)
    for p in range(N_PAIRS):
        lg = _pair_const(LOG_DECAY, p, lane2)
        qdec_ref[p] = jnp.exp(lg * (rf + 1.0))
        kdec_ref[p] = jnp.exp(lg * (RET_CHUNK - 1.0 - rf))
        sdec_ref[p] = jnp.exp(_pair_const(LOG_DECAY, p, r >= HEAD_DIM) * float(RET_CHUNK))
        lg2 = _pair_const(LOG_DECAY, p, ci >= LANES)
        intra_ref[p] = jnp.where(diff >= 0.0, jnp.exp(lg2 * jnp.maximum(diff, 0.0)), 0.0)
    cols = SWA_GROUP * WINDOW
    kb = _iota((2 * WINDOW, cols), 0)
    cb = _iota((2 * WINDOW, cols), 1)
    grp = cb >> 7
    dist = WINDOW + (cb & (WINDOW - 1)) - kb
    valid = (dist >= 0) & (dist < WINDOW)
    distf = dist.astype(F32)
    for j in range(N_SWA_KV):
        sl = [ALIBI_SLOPES[SWA_GROUP * j + g] for g in range(SWA_GROUP)]
        slope = jnp.where(grp == 0, sl[0], jnp.where(grp == 1, sl[1], jnp.where(grp == 2, sl[2], sl[3])))
        b = jnp.where(valid, -(slope.astype(F32) * distf), NEG_INF)
        bias_ref[0, j] = b
        bias_ref[1, j] = jnp.where(kb >= WINDOW, b, NEG_INF)


def _prompt_layer_kernel(sinks_ref, x_ref, x_next_ref, gain_mix_ref, w_in_ref, qg_ref, kg_ref,
                         w_out_hbm, gain_ffn_ref, w_up_hbm, w_down_hbm,
                         y_ref, ret_ref, kwin_ref, vwin_ref,
                         main_ref, qn_ref, kn_ref, vs_ref, mix_ref, hb_ref,
                         state_ref, prevk_ref, prevv_ref,
                         intra_ref, qdec_ref, kdec_ref, sdec_ref, bias_ref,
                         w_out_ref, w_up_ref, w_down_ref, w_sem):
    t = pl.program_id(1)
    step = pl.program_id(0) * pl.num_programs(1) + t
    cur = step % 2
    nxt = 1 - cur

    late_weights = [pltpu.make_async_copy(src, dst, w_sem.at[i]) for i, (src, dst) in enumerate(
        [(w_out_hbm, w_out_ref), (w_up_hbm, w_up_ref), (w_down_hbm, w_down_ref)])]

    @pl.when(step == 0)
    def _():
        for copy in late_weights:
            copy.start()
        _prompt_consts(intra_ref, qdec_ref, kdec_ref, sdec_ref, bias_ref)
        _in_proj_kernel(x_ref, gain_mix_ref, w_in_ref, qg_ref, kg_ref,
                        main_ref.at[0], qn_ref.at[0], kn_ref.at[0], vs_ref.at[0])

    @pl.when(t == 0)
    def _():
        state_ref[...] = jnp.zeros_like(state_ref)
        prevk_ref[...] = jnp.zeros_like(prevk_ref)
        prevv_ref[...] = jnp.zeros_like(prevv_ref)

    first = _first_half()
    ones_bd = _ones_block_diag()
    bd_mask = (_iota((LANES, LANES), 0) >= HEAD_DIM) == (_iota((LANES, LANES), 1) >= HEAD_DIM)

    n_chunks = PROMPT_TILE // RET_CHUNK
    assert n_chunks == 4
    plan = [dict(a=[], b=[]),
            dict(a=["m0", "m1"], b=[]),
            dict(a=["m2", "m3"], b=[]),
            dict(a=["v", "k"], b=["q0", "q1"])]
    a0 = MAIN_WIDTH
    attn_cols = {"v": (a0 + SWA_WIDTH + SWA_KV_WIDTH, SWA_KV_WIDTH), "k": (a0 + SWA_WIDTH, SWA_KV_WIDTH)}
    attn_cols.update({"q%d" % i: (a0 + Q_BLOCK * i, Q_BLOCK) for i in range(SWA_WIDTH // Q_BLOCK)})
    cols = lambda base, p: slice(base + p * LANES, base + (p + 1) * LANES)

    def project(items):
        raw = {}
        for it in items:
            if it[0] == "m":
                cb = int(it[1:])
                blk = slice(cb * PROJ_BLOCK, (cb + 1) * PROJ_BLOCK)
                main_ref[nxt, :, blk] = _dot(hb_ref[...], w_in_ref[:, blk])
            else:
                lo, width = attn_cols[it]
                raw[it] = _dot(hb_ref[...], w_in_ref[:, lo:lo + width])
        return raw

    def attn_scales(raw):
        return {name: [_head_rms_scale(val[:, cols(0, cq)], ones_bd) for cq in range(val.shape[1] // LANES)]
                for name, val in raw.items() if name != "v"}

    def store_attn(raw, scales):
        for name, val in raw.items():
            if name == "v":
                vs_ref[nxt] = val
            elif name == "k":
                kn_ref[nxt] = (val * scales[name][0]) * _pair_gain(kg_ref)
            else:
                base = attn_cols[name][0] - a0
                for cq in range(val.shape[1] // LANES):
                    qn_ref[nxt, :, cols(base, cq)] = (val[:, cols(0, cq)] * scales[name][cq]) * _pair_gain(qg_ref)

    pairs = range(N_PAIRS)
    kvs = range(N_SWA_KV)

    def stage1(c):
        rows = slice(c * RET_CHUNK, (c + 1) * RET_CHUNK)
        q = [main_ref[cur, rows, cols(0, p)] for p in pairs]
        k = [main_ref[cur, rows, cols(RET_WIDTH, p)] * K_SCALE for p in pairs]
        v = [main_ref[cur, rows, cols(2 * RET_WIDTH, p)] for p in pairs]
        state = [state_ref[p] for p in pairs]
        kc = kn_ref[cur, rows, :]
        k_sw = pltpu.roll(kc, HEAD_DIM, axis=1)
        v_t = vs_ref[cur, rows, :].T
        is_first = (t == 0).astype(jnp.int32) if c == 0 else 0
        k_dup =[(jnp.where(first, kc, k_sw) if j == 0 else jnp.where(first, k_sw, kc)).astype(BF16)
                 for j in kvs]
        v_tj = [v_t[j * HEAD_DIM:(j + 1) * HEAD_DIM].astype(BF16) for j in kvs]
        q_st = []
        for j in kvs:
            pieces = []
            for g in range(SWA_GROUP):
                qc = qn_ref[cur, rows, cols(0, 2 * j + g // 2)]
                pieces.append(jnp.where(first, qc, 0.0) if g % 2 == 0 else jnp.where(first, 0.0, qc))
            q_st.append(jnp.concatenate(pieces, axis=0).astype(BF16))

        s = [_dot_nt(q[p].astype(BF16), _split_pair_rows(k[p], first)) for p in pairs]
        s_t = [_dot_nt(jnp.concatenate([prevk_ref[j], k_dup[j]], axis=0), q_st[j]) for j in kvs]
        cross = [_dot((q[p] * qdec_ref[p]).astype(BF16), state[p].astype(BF16)) for p in pairs]
        upd = [_dot((k[p] * kdec_ref[p]).T.astype(BF16), v[p].astype(BF16)) for p in pairs]
        return dict(rows=rows, v=v, state=state, k_dup=k_dup, v_tj=v_tj, is_first=is_first,
                    s=s, s_t=s_t, cross=cross, upd=upd)

    def stage2(st):
        st["o"] = [_dot((st["s"][p] * intra_ref[p]).astype(BF16), _split_pair_rows(st["v"][p], first))
                   + st["cross"][p] for p in pairs]
        st["o_t"] = []
        for j in kvs:
            sink_lanes = jnp.concatenate(
                [jnp.full((1, WINDOW), sinks_ref[SWA_GROUP * j + g], F32) for g in range(SWA_GROUP)],
                axis=1)
            v_cat = jnp.concatenate([prevv_ref[j], st["v_tj"][j]], axis=1)
            st["o_t"].append(_softmax_sink_pv_t(st["s_t"][j] * K_SCALE + bias_ref[st["is_first"], j],
                                                sink_lanes, v_cat))
        for p in pairs:
            state_ref[p] = st["state"][p] * sdec_ref[p] + jnp.where(bd_mask, st["upd"][p], 0.0)
        for j in kvs:
            prevk_ref[j] = st["k_dup"][j]
            prevv_ref[j] = st["v_tj"][j]

    def stage3(st, raw):
        rows = st["rows"]
        scale = [_head_rms_scale(st["o"][p], ones_bd) for p in pairs]
        raw_scales = attn_scales(raw)
        for p in pairs:
            g = main_ref[cur, rows, cols(3 * RET_WIDTH, p)]
            mix_ref[rows, cols(0, p)] = (st["o"][p] * scale[p] * _silu(g)).astype(BF16)
        store_attn(raw, raw_scales)
        for j in kvs:
            o_t = st["o_t"][j]
            for half in range(2):
                pair_t = jnp.concatenate([o_t[:, (2 * half) * WINDOW:(2 * half + 1) * WINDOW],
                                          o_t[:, (2 * half + 1) * WINDOW:(2 * half + 2) * WINDOW]], axis=0)
                mix_ref[rows, cols(RET_WIDTH, 2 * j + half)] = pair_t.T.astype(BF16)

    xn = x_next_ref[...]
    hb_ref[...] = ((xn * lax.rsqrt(jnp.mean(xn * xn, axis=-1, keepdims=True) + EPS)) * gain_mix_ref[...]).astype(BF16)
    for c in range(n_chunks):
        st = stage1(c)
        raw = project(plan[c]["a"])
        stage2(st)
        raw.update(project(plan[c]["b"]))
        stage3(st, raw)

    @pl.when(t == pl.num_programs(1) - 1)
    def _():
        for p in range(N_PAIRS):
            s = state_ref[p]
            ret_ref[2 * p] = s[:HEAD_DIM, :HEAD_DIM]
            ret_ref[2 * p + 1] = s[HEAD_DIM:, HEAD_DIM:]
        last = slice(PROMPT_TILE - WINDOW, PROMPT_TILE)
        kwin_ref[...] = kn_ref[cur, last, :].T
        vwin_ref[...] = vs_ref[cur, last, :].T

    @pl.when(step == 0)
    def _():
        for copy in late_weights:
            copy.wait()

    _out_mlp_kernel(mix_ref.at[:, pl.ds(0, RET_WIDTH)], mix_ref.at[:, pl.ds(RET_WIDTH, SWA_WIDTH)], x_ref,
                    w_out_ref, gain_ffn_ref, w_up_ref, w_down_ref, y_ref)


def _prompt_layer(sinks, x2d, gain_mix, w_in_bf, qg, kg, w_out_bf, gain_ffn, w_up_bf, w_down_bf, batch, seq):
    nt = seq // PROMPT_TILE
    last_tile = batch * nt - 1
    row = lambda w: pl.BlockSpec((PROMPT_TILE, w), lambda b, t: (b * nt + t, 0))
    next_row = pl.BlockSpec((PROMPT_TILE, D_MODEL), lambda b, t: (jnp.minimum(b * nt + t + 1, last_tile), 0))
    full = lambda a: pl.BlockSpec(a.shape, lambda b, t: (0, 0), pipeline_mode=pl.Buffered(1))
    in_hbm = pl.BlockSpec(memory_space=pl.ANY)
    return pl.pallas_call(
        _prompt_layer_kernel,
        grid=(batch, nt),
        in_specs=[pl.BlockSpec(memory_space=pltpu.SMEM), row(D_MODEL), next_row,
                  full(gain_mix), full(w_in_bf), full(qg), full(kg),
                  in_hbm, full(gain_ffn), in_hbm, in_hbm],
        out_specs=[row(D_MODEL),
                   pl.BlockSpec((None, N_RET_HEADS, HEAD_DIM, HEAD_DIM), lambda b, t: (b, 0, 0, 0)),
                   pl.BlockSpec((None, SWA_KV_WIDTH, WINDOW), lambda b, t: (b, 0, 0)),
                   pl.BlockSpec((None, SWA_KV_WIDTH, WINDOW), lambda b, t: (b, 0, 0))],
        out_shape=[jax.ShapeDtypeStruct((batch * seq, D_MODEL), F32),
                   jax.ShapeDtypeStruct((batch, N_RET_HEADS, HEAD_DIM, HEAD_DIM), F32),
                   jax.ShapeDtypeStruct((batch, SWA_KV_WIDTH, WINDOW), F32),
                   jax.ShapeDtypeStruct((batch, SWA_KV_WIDTH, WINDOW), F32)],
        scratch_shapes=[
            pltpu.VMEM((2, PROMPT_TILE, MAIN_WIDTH), F32),
            pltpu.VMEM((2, PROMPT_TILE, SWA_WIDTH), F32),
            pltpu.VMEM((2, PROMPT_TILE, SWA_KV_WIDTH), F32),
            pltpu.VMEM((2, PROMPT_TILE, SWA_KV_WIDTH), F32),
            pltpu.VMEM((PROMPT_TILE, MIX_WIDTH), BF16),
            pltpu.VMEM((PROMPT_TILE, D_MODEL), BF16),
            pltpu.VMEM((N_PAIRS, LANES, LANES), F32),
            pltpu.VMEM((N_SWA_KV, WINDOW, LANES), BF16),
            pltpu.VMEM((N_SWA_KV, HEAD_DIM, WINDOW), BF16),
            pltpu.VMEM((N_PAIRS, LANES, 2 * LANES), F32),
            pltpu.VMEM((N_PAIRS, LANES, LANES), F32),
            pltpu.VMEM((N_PAIRS, LANES, LANES), F32),
            pltpu.VMEM((N_PAIRS, LANES, LANES), F32),
            pltpu.VMEM((2, N_SWA_KV, 2 * WINDOW, SWA_GROUP * WINDOW), F32),
            pltpu.VMEM(w_out_bf.shape, BF16),
            pltpu.VMEM(w_up_bf.shape, BF16),
            pltpu.VMEM(w_down_bf.shape, BF16),
            pltpu.SemaphoreType.DMA((3,)),
        ],
        compiler_params=pltpu.CompilerParams(
            dimension_semantics=("arbitrary", "arbitrary"), vmem_limit_bytes=PROMPT_VMEM_LIMIT),
        name="prompt_layer",
    )(sinks, x2d, x2d, gain_mix, w_in_bf, qg, kg, w_out_bf, gain_ffn, w_up_bf, w_down_bf)


def _decode_ret_kernel(dec_seq, nb, qdec_ref, kdec_ref, sdec_ref, intra_ref,
                       q_ref, k_ref, v_ref, g_ref, st_ref,
                       mix_ref, st_out_ref,
                       qt_ref, kt_ref, vt_ref, qdt_ref, kdt_ref, o_ref):
    pair = pl.program_id(0)
    halves = [slice(0, HEAD_DIM), slice(HEAD_DIM, 2 * HEAD_DIM)]
    for l in range(dec_seq):
        rows = pl.ds(l, nb, stride=dec_seq)
        q_t = q_ref[rows, :].T
        k_t = (k_ref[rows, :] * K_SCALE).T
        qt_ref[l] = q_t
        kt_ref[l] = k_t
        vt_ref[l] = v_ref[rows, :].T
        for hh in range(2):
            qdt_ref[l, halves[hh], :] = q_t[halves[hh]] * qdec_ref[2 * pair + hh, l]
            kdt_ref[l, halves[hh], :] = k_t[halves[hh]] * kdec_ref[2 * pair + hh, l]

    e_blk = HEAD_DIM // 2
    for hh in range(2):
        h = 2 * pair + hh
        hs = halves[hh]
        for l in range(dec_seq):
            acc = None
            for m in range(l + 1):
                sc = jnp.sum(qt_ref[l, hs, :] * kt_ref[m, hs, :], axis=0, keepdims=True) * intra_ref[h, l - m]
                term = sc * vt_ref[m, hs, :]
                acc = term if acc is None else acc + term
            o_ref[l, hs, :] = acc
        for eb in range(HEAD_DIM // e_blk):
            es = slice(eb * e_blk, (eb + 1) * e_blk)
            erows = slice(hh * HEAD_DIM + eb * e_blk, hh * HEAD_DIM + (eb + 1) * e_blk)

            def body(d, accs, hh=hh, h=h, es=es, erows=erows):
                s_d = st_ref[hh, d, es, :]
                row = pl.ds(hh * HEAD_DIM + d, 1)
                upd = s_d * sdec_ref[h]
                new_accs = []
                for l in range(dec_seq):
                    new_accs.append(accs[l] + qdt_ref[l, row, :] * s_d)
                    upd = upd + kdt_ref[l, row, :] * vt_ref[l, erows, :]
                st_out_ref[hh, d, es, :] = upd
                return tuple(new_accs)

            zero = jnp.zeros((e_blk, nb), F32)
            accs = lax.fori_loop(0, HEAD_DIM, body, tuple(zero for _ in range(dec_seq)), unroll=2)
            for l in range(dec_seq):
                o_ref[l, erows, :] = o_ref[l, erows, :] + accs[l]

    for l in range(dec_seq):
        o = o_ref[l]
        normed = []
        for hh in range(2):
            oh = o[halves[hh]]
            normed.append(oh * lax.rsqrt(jnp.mean(oh * oh, axis=0, keepdims=True) + EPS))
        rows = pl.ds(l, nb, stride=dec_seq)
        mix_ref[rows, :] = jnp.concatenate(normed, axis=0).T * _silu(g_ref[rows, :])


def _decode_ret(main, state_t, dec_seq):
    nb = state_t.shape[-1]
    m = main.shape[0]
    assert nb == LANES and m == nb * dec_seq
    steps = [j for j in range(dec_seq)]
    tab = lambda f: jnp.asarray([[f(h, j) for j in steps] for h in range(N_RET_HEADS)], F32)
    qdec = tab(lambda h, j: math.exp(LOG_DECAY[h] * (j + 1.0)))
    kdec = tab(lambda h, j: math.exp(LOG_DECAY[h] * (dec_seq - 1.0 - j)))
    intra = tab(lambda h, j: math.exp(LOG_DECAY[h] * j))
    sdec = jnp.asarray([math.exp(LOG_DECAY[h] * dec_seq) for h in range(N_RET_HEADS)], F32)
    smem = pl.BlockSpec(memory_space=pltpu.SMEM)
    col = lambda base: pl.BlockSpec((m, LANES), lambda p: (0, base + p))
    st_spec = pl.BlockSpec((2, HEAD_DIM, HEAD_DIM, nb), lambda p: (p, 0, 0, 0))
    stage = pltpu.VMEM((dec_seq, LANES, nb), F32)
    return pl.pallas_call(
        functools.partial(_decode_ret_kernel, dec_seq, nb),
        grid=(N_PAIRS,),
        in_specs=[smem, smem, smem, smem,
                  col(0), col(N_PAIRS), col(2 * N_PAIRS), col(3 * N_PAIRS), st_spec],
        out_specs=[pl.BlockSpec((m, LANES), lambda p: (0, p)), st_spec],
        out_shape=[jax.ShapeDtypeStruct((m, RET_WIDTH), F32),
                   jax.ShapeDtypeStruct(state_t.shape, F32)],
        scratch_shapes=[stage, stage, stage, stage, stage, stage],
        compiler_params=pltpu.CompilerParams(
            dimension_semantics=("arbitrary",), vmem_limit_bytes=VMEM_LIMIT),
        name="decode_ret",
    )(qdec, kdec, sdec, intra, main, main, main, main, state_t)


DEC_ROWS = 128
DEC_UNROLL = 8


def _decode_attn_consts(dec_seq, bias_ref):
    shift = dec_seq.bit_length() - 1
    rows = N_SWA_HEADS * dec_seq
    rb = _iota((rows, WINDOW), 0)
    cb = _iota((rows, WINDOW), 1)
    head = rb >> shift
    i = rb & (dec_seq - 1)
    slope = jnp.zeros((rows, WINDOW), F32)
    for h in range(N_SWA_HEADS):
        slope = jnp.where(head == h, ALIBI_SLOPES[h], slope)
    bias_ref[0] = jnp.where(cb > i, -(slope * (WINDOW + i - cb).astype(F32)), NEG_INF)
    m = cb & (dec_seq - 1)
    bias_ref[1] = jnp.where(m <= i, -(slope * (i - m).astype(F32)), NEG_INF)


def _decode_attn_kernel(dec_seq, sinks_ref, qn_ref, kn_ref, vs_ref, kt_ref, vt_ref,
                        mix_ref, kt_out_ref, vt_out_ref,
                        bias_ref, qbd_ref, oblk_ref, knew_ref, vnew_ref, knt_ref, vst_ref):
    @pl.when(pl.program_id(0) == 0)
    def _():
        _decode_attn_consts(dec_seq, bias_ref)

    first = _first_half()
    shift = dec_seq.bit_length() - 1

    kn_t = kn_ref[...].T
    vs_t = vs_ref[...].T
    knt_ref[...] = kn_t.astype(BF16)
    vst_ref[...] = vs_t.astype(BF16)
    for bb in range(DEC_GROUP):
        sh = (WINDOW - dec_seq - bb * dec_seq) % LANES
        knew_ref[bb] = pltpu.roll(kn_t, sh, axis=1) if sh else kn_t
        vnew_ref[bb] = pltpu.roll(vs_t, sh, axis=1) if sh else vs_t
    qn = qn_ref[...]
    qn_sw = pltpu.roll(qn, HEAD_DIM, axis=1)
    for h in range(N_SWA_HEADS):
        kv_half = h // SWA_GROUP
        if (h % 2) == kv_half:
            src = qn[:, (h // 2) * LANES:(h // 2 + 1) * LANES]
        else:
            col = (h + 1) // 2
            src = qn_sw[:, col * LANES:(col + 1) * LANES]
        qbd_ref[h] = jnp.where(first, src, 0.0) if kv_half == 0 else jnp.where(first, 0.0, src)

    sink_rows = jnp.concatenate(
        [jnp.full((dec_seq, LANES), sinks_ref[h], F32) for h in range(N_SWA_HEADS)], axis=0)
    col_batch = _iota((N_SWA_HEADS * dec_seq, LANES), 1) >> shift
    keep_old = _iota((1, LANES), 1) < WINDOW - dec_seq

    def per_batches(i, carry):
        bs = [i * DEC_UNROLL + u for u in range(DEC_UNROLL)]
        rows = [pl.ds(pl.multiple_of(b * dec_seq, dec_seq), dec_seq) for b in bs]
        k_old = [kt_ref[b] for b in bs]
        v_old = [vt_ref[b] for b in bs]
        q_st = [jnp.concatenate([qbd_ref[h, r, :] for h in range(N_SWA_HEADS)], axis=0).astype(BF16)
                for r in rows]
        s = [_dot(q_st[u], jnp.concatenate([k_old[u].astype(BF16), knt_ref[...]], axis=1))
             for u in range(DEC_UNROLL)]
        o = []
        for u, b in enumerate(bs):
            bias = jnp.concatenate([bias_ref[0], jnp.where(col_batch == b, bias_ref[1], NEG_INF)], axis=1)
            w_v = jnp.concatenate([v_old[u].astype(BF16), vst_ref[...]], axis=1)
            o.append(_softmax_sink_pv(s[u] * K_SCALE + bias, sink_rows, w_v))
        for u, b in enumerate(bs):
            for h in range(N_SWA_HEADS):
                oblk_ref[h, rows[u], :] = o[u][h * dec_seq:(h + 1) * dec_seq]
            kt_out_ref[b] = jnp.where(keep_old, pltpu.roll(k_old[u], LANES - dec_seq, axis=1), knew_ref[b])
            vt_out_ref[b] = jnp.where(keep_old, pltpu.roll(v_old[u], LANES - dec_seq, axis=1), vnew_ref[b])
        return carry

    lax.fori_loop(0, DEC_GROUP // DEC_UNROLL, per_batches, 0)

    y1 = jnp.where(first, oblk_ref[3], oblk_ref[4])
    moved = pltpu.roll(jnp.concatenate([oblk_ref[1], y1, oblk_ref[6], oblk_ref[6]], axis=1), HEAD_DIM, axis=1)
    outs = [
        jnp.where(first, oblk_ref[0], moved[:, 0:LANES]),
        jnp.where(first, oblk_ref[2], moved[:, LANES:2 * LANES]),
        jnp.where(first, moved[:, 2 * LANES:3 * LANES], oblk_ref[5]),
        jnp.where(first, moved[:, 3 * LANES:4 * LANES], oblk_ref[7]),
    ]
    for c in range(SWA_WIDTH // LANES):
        mix_ref[:, c * LANES:(c + 1) * LANES] = outs[c].astype(BF16)


def _decode_attn(sinks, qn, kn, vs, k_t, v_t, dec_seq):
    nb = k_t.shape[0]
    assert DEC_GROUP * dec_seq == DEC_ROWS and nb % DEC_GROUP == 0 and dec_seq & (dec_seq - 1) == 0
    assert k_t.shape[1:] == (SWA_KV_WIDTH, WINDOW)
    row = lambda w: pl.BlockSpec((DEC_ROWS, w), lambda i: (i, 0))
    cache = pl.BlockSpec((DEC_GROUP, SWA_KV_WIDTH, WINDOW), lambda i: (i, 0, 0))
    return pl.pallas_call(
        functools.partial(_decode_attn_kernel, dec_seq),
        grid=(nb // DEC_GROUP,),
        in_specs=[pl.BlockSpec(memory_space=pltpu.SMEM),
                  row(SWA_WIDTH), row(SWA_KV_WIDTH), row(SWA_KV_WIDTH), cache, cache],
        out_specs=[row(SWA_WIDTH), cache, cache],
        out_shape=[jax.ShapeDtypeStruct((nb * dec_seq, SWA_WIDTH), BF16),
                   jax.ShapeDtypeStruct(k_t.shape, F32),
                   jax.ShapeDtypeStruct(v_t.shape, F32)],
        scratch_shapes=[
            pltpu.VMEM((2, N_SWA_HEADS * dec_seq, WINDOW), F32),
            pltpu.VMEM((N_SWA_HEADS, DEC_ROWS, LANES), F32),
            pltpu.VMEM((N_SWA_HEADS, DEC_ROWS, LANES), F32),
            pltpu.VMEM((DEC_GROUP, SWA_KV_WIDTH, LANES), F32),
            pltpu.VMEM((DEC_GROUP, SWA_KV_WIDTH, LANES), F32),
            pltpu.VMEM((SWA_KV_WIDTH, DEC_ROWS), BF16),
            pltpu.VMEM((SWA_KV_WIDTH, DEC_ROWS), BF16),
        ],
        compiler_params=pltpu.CompilerParams(
            dimension_semantics=("arbitrary",), vmem_limit_bytes=VMEM_LIMIT),
        name="decode_attn",
    )(sinks, qn, kn, vs, k_t, v_t)


def kernel(x_prompt, x_sample, state_ret, cache_swa_k, cache_swa_v, norm_mix_gain, w_in, q_norm_gain,
           k_norm_gain, attn_sinks, w_out, norm_ffn_gain, w_up, w_down):
    batch, seq, d = x_prompt.shape
    nb, dec_seq, _ = x_sample.shape
    wb = cache_swa_k.shape[1]
    assert d == D_MODEL and seq % PROMPT_TILE == 0 and wb == WINDOW

    gain_mix = norm_mix_gain.reshape(1, D_MODEL)
    gain_ffn = norm_ffn_gain.reshape(1, D_MODEL)
    qg = q_norm_gain.reshape(1, HEAD_DIM)
    kg = k_norm_gain.reshape(1, HEAD_DIM)

    def from_key_minor(a_t):
        return jnp.transpose(a_t.reshape(a_t.shape[0], N_SWA_KV, HEAD_DIM, WINDOW), (0, 3, 1, 2))

    def to_key_minor(a):
        return jnp.transpose(a, (0, 2, 3, 1)).reshape(a.shape[0], SWA_KV_WIDTH, WINDOW)

    xs = x_sample.reshape(nb * dec_seq, D_MODEL)
    main_s, qn_s, kn_s, vs_s, w_in_bf = _in_proj_cast(xs, gain_mix, w_in, qg, kg)
    mix_ret_s, state_t = _decode_ret(main_s, jnp.transpose(state_ret, (1, 2, 3, 0)), dec_seq)
    mix_swa_s, k_t, v_t = _decode_attn(attn_sinks, qn_s, kn_s, vs_s,
                                       to_key_minor(cache_swa_k), to_key_minor(cache_swa_v), dec_seq)
    y_s, w_out_bf, w_up_bf, w_down_bf = _out_mlp_cast(mix_ret_s, mix_swa_s, xs, gain_ffn, w_out, w_up, w_down)

    xp = x_prompt.reshape(batch * seq, D_MODEL)
    y_p, ret_p, kwin_t, vwin_t = _prompt_layer(attn_sinks, xp, gain_mix, w_in_bf, qg, kg, w_out_bf, gain_ffn,
                                               w_up_bf, w_down_bf, batch, seq)
    y_p = y_p.reshape(batch, seq, D_MODEL)

    return (y_p, y_s.reshape(nb, dec_seq, D_MODEL), ret_p, from_key_minor(kwin_t), from_key_minor(vwin_t),
            jnp.transpose(state_t, (3, 0, 1, 2)), from_key_minor(k_t), from_key_minor(v_t))
```

```python
import functools
import math

import jax
import jax.numpy as jnp
from jax import lax
from jax.experimental import pallas as pl
from jax.experimental.pallas import tpu as pltpu

F32 = jnp.float32
BF16 = jnp.bfloat16

D_MODEL = 1024
HEAD_DIM = 64
N_RET_HEADS = 8
N_SWA_HEADS = 8
N_SWA_KV = 2
SWA_GROUP = N_SWA_HEADS // N_SWA_KV
RET_WIDTH = N_RET_HEADS * HEAD_DIM
SWA_WIDTH = N_SWA_HEADS * HEAD_DIM
SWA_KV_WIDTH = N_SWA_KV * HEAD_DIM
MAIN_WIDTH = 4 * RET_WIDTH
IN_WIDTH = MAIN_WIDTH + SWA_WIDTH + 2 * SWA_KV_WIDTH
MIX_WIDTH = RET_WIDTH + SWA_WIDTH
D_FF = 4 * D_MODEL
WINDOW = 128
RET_CHUNK = 128
EPS = 1e-6
NEG_INF = -1e30

LANES = 128
N_PAIRS = N_RET_HEADS // 2
LOG_DECAY = [math.log(1.0 - 2.0 ** (-5.0 - h)) for h in range(N_RET_HEADS)]
ALIBI_SLOPES = [2.0 ** (-8.0 * (h + 1) / N_SWA_HEADS) for h in range(N_SWA_HEADS)]
K_SCALE = HEAD_DIM ** -0.5

MXU_WIDTH = 256
PROJ_BLOCK = 2 * MXU_WIDTH
Q_BLOCK = MXU_WIDTH
PROMPT_TILE = 512
DEC_GROUP = 16
VMEM_LIMIT = 56 * 1024 * 1024
PROMPT_VMEM_LIMIT = 62 * 1024 * 1024


def _dot(a, b):
    return jnp.dot(a, b, preferred_element_type=F32)


def _dot_nt(a, b):
    return lax.dot_general(a, b, (((1,), (1,)), ((), ())), preferred_element_type=F32)


def _iota(shape, dim):
    return lax.broadcasted_iota(jnp.int32, shape, dim)


def _ones_block_diag():
    same = ((_iota((2 * LANES, LANES), 0) >> 6) & 1) == (_iota((2 * LANES, LANES), 1) >> 6)
    return jnp.where(same, 1.0, 0.0).astype(BF16)


def _head_sumsq(x, ones_bd):
    x2 = x * x
    hi = x2.astype(BF16)
    lo = (x2 - hi.astype(F32)).astype(BF16)
    return _dot(jnp.concatenate([hi, lo], axis=1), ones_bd)


def _head_rms_scale(x, ones_bd):
    return lax.rsqrt(_head_sumsq(x, ones_bd) * (1.0 / HEAD_DIM) + EPS)


def _head_rms_scales(xs, ones_bd):
    if not xs:
        return []
    parts = []
    for x in xs:
        x2 = x * x
        hi = x2.astype(BF16)
        parts.append(jnp.concatenate([hi, (x2 - hi.astype(F32)).astype(BF16)], axis=1))
    total = _dot(jnp.concatenate(parts, axis=0) if len(parts) > 1 else parts[0], ones_bd)
    out, lo = [], 0
    for x in xs:
        out.append(lax.rsqrt(total[lo:lo + x.shape[0]] * (1.0 / HEAD_DIM) + EPS))
        lo += x.shape[0]
    return out


def _first_half():
    return _iota((1, LANES), 1) < HEAD_DIM


def _pair_const(values, pair, lane_is_second):
    return jnp.where(lane_is_second, values[2 * pair + 1], values[2 * pair]).astype(F32)


def _pair_gain(gain_ref):
    g = gain_ref[...]
    return jnp.concatenate([g, g], axis=1)


def _silu(g):
    return g * (1.0 / (1.0 + jnp.exp(-g)))


def _in_proj_kernel(x_ref, gain_ref, w_ref, qg_ref, kg_ref, main_ref, qn_ref, kn_ref, vs_ref, fetch=None):
    x = x_ref[...]
    ms = jnp.mean(x * x, axis=-1, keepdims=True)
    hb = ((x * lax.rsqrt(ms + EPS)) * gain_ref[...]).astype(BF16)
    ones_bd = _ones_block_diag()

    def project(lo, width):
        if fetch is not None:
            fetch(lo, width)
        return _dot(hb, w_ref[:, lo:lo + width])

    for c in range(MAIN_WIDTH // PROJ_BLOCK):
        main_ref[:, c * PROJ_BLOCK:(c + 1) * PROJ_BLOCK] = project(c * PROJ_BLOCK, PROJ_BLOCK)
    qs = project(MAIN_WIDTH, SWA_WIDTH)
    k0 = MAIN_WIDTH + SWA_WIDTH
    ks = project(k0, SWA_KV_WIDTH)
    vs_ref[...] = project(k0 + SWA_KV_WIDTH, SWA_KV_WIDTH)
    q_cols = [qs[:, c * LANES:(c + 1) * LANES] for c in range(SWA_WIDTH // LANES)]
    scales = _head_rms_scales(q_cols + [ks], ones_bd)
    for c, xc in enumerate(q_cols):
        qn_ref[:, c * LANES:(c + 1) * LANES] = (xc * scales[c]) * _pair_gain(qg_ref)
    kn_ref[...] = (ks * scales[-1]) * _pair_gain(kg_ref)


ROW_TILE = 512


W_IN_BLOCKS = [(c * PROJ_BLOCK, PROJ_BLOCK) for c in range(MAIN_WIDTH // PROJ_BLOCK)] + [(MAIN_WIDTH, SWA_WIDTH),
                                                                    (MAIN_WIDTH + SWA_WIDTH, 2 * SWA_KV_WIDTH)]
assert sum(w for _, w in W_IN_BLOCKS) == IN_WIDTH


def _in_proj_cast_kernel(x_ref, gain_ref, w_hbm, qg_ref, kg_ref,
                         main_ref, qn_ref, kn_ref, vs_ref, w_bf_hbm,
                         w_v, stage, in_sem, out_sem):
    step = pl.program_id(0)
    block_in = [pltpu.make_async_copy(w_hbm.at[:, pl.ds(lo, width)], stage.at[:, pl.ds(lo, width)], in_sem.at[i])
                for i, (lo, width) in enumerate(W_IN_BLOCKS)]
    writeback = pltpu.make_async_copy(w_v, w_bf_hbm, out_sem.at[0])

    @pl.when(step == 0)
    def _():
        for copy in block_in:
            copy.start()
        arrived = set()

        def fetch(lo, width):
            for i, (blo, bwidth) in enumerate(W_IN_BLOCKS):
                if blo <= lo < blo + bwidth and i not in arrived:
                    assert lo + width <= blo + bwidth
                    block_in[i].wait()
                    w_v[:, blo:blo + bwidth] = stage[:, blo:blo + bwidth].astype(BF16)
                    arrived.add(i)

        _in_proj_kernel(x_ref, gain_ref, w_v, qg_ref, kg_ref, main_ref, qn_ref, kn_ref, vs_ref, fetch=fetch)
        assert len(arrived) == len(W_IN_BLOCKS)
        writeback.start()

    @pl.when(step > 0)
    def _():
        _in_proj_kernel(x_ref, gain_ref, w_v, qg_ref, kg_ref, main_ref, qn_ref, kn_ref, vs_ref)

    @pl.when(step == pl.num_programs(0) - 1)
    def _():
        writeback.wait()


def _in_proj_cast(x2d, gain, w_in, qg, kg):
    m = x2d.shape[0]
    tm = min(ROW_TILE, m)
    row = lambda w: pl.BlockSpec((tm, w), lambda i: (i, 0))
    full = lambda a: pl.BlockSpec(a.shape, lambda i: (0, 0), pipeline_mode=pl.Buffered(1))
    hbm = pl.BlockSpec(memory_space=pl.ANY)
    return pl.pallas_call(
        _in_proj_cast_kernel,
        grid=(m // tm,),
        in_specs=[row(D_MODEL), full(gain), hbm, full(qg), full(kg)],
        out_specs=[row(MAIN_WIDTH), row(SWA_WIDTH), row(SWA_KV_WIDTH), row(SWA_KV_WIDTH), hbm],
        out_shape=[jax.ShapeDtypeStruct((m, MAIN_WIDTH), F32),
                   jax.ShapeDtypeStruct((m, SWA_WIDTH), F32),
                   jax.ShapeDtypeStruct((m, SWA_KV_WIDTH), F32),
                   jax.ShapeDtypeStruct((m, SWA_KV_WIDTH), F32),
                   jax.ShapeDtypeStruct(w_in.shape, BF16)],
        scratch_shapes=[pltpu.VMEM(w_in.shape, BF16), pltpu.VMEM(w_in.shape, F32),
                        pltpu.SemaphoreType.DMA((len(W_IN_BLOCKS),)), pltpu.SemaphoreType.DMA((1,))],
        compiler_params=pltpu.CompilerParams(
            dimension_semantics=("arbitrary",), vmem_limit_bytes=VMEM_LIMIT),
        name="in_proj_cast",
    )(x2d, gain, w_in, qg, kg)


FF_CHUNK = 1024


def _out_mlp_kernel(mix_ret_ref, mix_swa_ref, x_ref, w_out_ref, gain_ref, w_up_ref, w_down_ref, y_ref):
    h = x_ref[...] + (_dot(mix_ret_ref[...].astype(BF16), w_out_ref[:RET_WIDTH, :])
                      + _dot(mix_swa_ref[...].astype(BF16), w_out_ref[RET_WIDTH:, :]))
    ms = jnp.mean(h * h, axis=-1, keepdims=True)
    hf = ((h * lax.rsqrt(ms + EPS)) * gain_ref[...]).astype(BF16)
    ff = None
    for c in range(D_FF // FF_CHUNK):
        u = _dot(hf, w_up_ref[:, c * FF_CHUNK:(c + 1) * FF_CHUNK])
        a = jnp.maximum(u, 0.0)
        d = _dot((a * a).astype(BF16), w_down_ref[c * FF_CHUNK:(c + 1) * FF_CHUNK, :])
        ff = d if ff is None else ff + d
    y_ref[...] = h + ff


CAST_CHUNK = 512
N_FF_CHUNKS = D_FF // CAST_CHUNK


def _out_mlp_cast_kernel(mix_ret_ref, mix_swa_ref, x_ref, gain_ref, w_out_hbm, w_up_hbm, w_down_hbm,
                         y_ref, w_out_bf_hbm, w_up_bf_hbm, w_down_bf_hbm,
                         w_out_v, w_up_v, w_down_v, stage_out, stage_up, stage_down, in_sem, out_sem):
    step = pl.program_id(0)

    def out_in(i):
        return pltpu.make_async_copy(w_out_hbm.at[pl.ds(i * CAST_CHUNK, CAST_CHUNK), :], stage_out.at[i],
                                     in_sem.at[0, i])

    def up_in(c):
        return pltpu.make_async_copy(w_up_hbm.at[:, pl.ds(c * CAST_CHUNK, CAST_CHUNK)], stage_up.at[c % 2],
                                     in_sem.at[1, c % 2])

    def down_in(c):
        return pltpu.make_async_copy(w_down_hbm.at[pl.ds(c * CAST_CHUNK, CAST_CHUNK), :], stage_down.at[c % 2],
                                     in_sem.at[2, c % 2])

    writebacks = [pltpu.make_async_copy(src, dst, out_sem.at[i]) for i, (src, dst) in enumerate(
        [(w_out_v, w_out_bf_hbm), (w_up_v, w_up_bf_hbm), (w_down_v, w_down_bf_hbm)])]

    @pl.when(step == 0)
    def _():
        n_out = D_MODEL // CAST_CHUNK
        for i in range(n_out):
            out_in(i).start()
        up_in(0).start()
        down_in(0).start()
        for i in range(n_out):
            out_in(i).wait()
            w_out_v[i * CAST_CHUNK:(i + 1) * CAST_CHUNK, :] = stage_out[i].astype(BF16)
        writebacks[0].start()
        h = x_ref[...] + (_dot(mix_ret_ref[...].astype(BF16), w_out_v[:RET_WIDTH, :])
                          + _dot(mix_swa_ref[...].astype(BF16), w_out_v[RET_WIDTH:, :]))
        ms = jnp.mean(h * h, axis=-1, keepdims=True)
        hf = ((h * lax.rsqrt(ms + EPS)) * gain_ref[...]).astype(BF16)
        ff = None
        for c in range(N_FF_CHUNKS):
            chunk = slice(c * CAST_CHUNK, (c + 1) * CAST_CHUNK)
            if c + 1 < N_FF_CHUNKS:
                up_in(c + 1).start()
                down_in(c + 1).start()
            up_in(c).wait()
            down_in(c).wait()
            w_up_v[:, chunk] = stage_up[c % 2].astype(BF16)
            w_down_v[chunk, :] = stage_down[c % 2].astype(BF16)
            a = jnp.maximum(_dot(hf, w_up_v[:, chunk]), 0.0)
            d = _dot((a * a).astype(BF16), w_down_v[chunk, :])
            ff = d if ff is None else ff + d
        writebacks[1].start()
        writebacks[2].start()
        y_ref[...] = h + ff

    @pl.when(step > 0)
    def _():
        _out_mlp_kernel(mix_ret_ref, mix_swa_ref, x_ref, w_out_v, gain_ref, w_up_v, w_down_v, y_ref)

    @pl.when(step == pl.num_programs(0) - 1)
    def _():
        for wb in writebacks:
            wb.wait()


def _out_mlp_cast(mix_ret, mix_swa, x2d, gain, w_out, w_up, w_down):
    m = x2d.shape[0]
    tm = min(ROW_TILE, m)
    assert D_MODEL % CAST_CHUNK == 0 and D_FF % CAST_CHUNK == 0
    row = lambda w: pl.BlockSpec((tm, w), lambda i: (i, 0))
    full = lambda a: pl.BlockSpec(a.shape, lambda i: (0, 0), pipeline_mode=pl.Buffered(1))
    hbm = pl.BlockSpec(memory_space=pl.ANY)
    return pl.pallas_call(
        _out_mlp_cast_kernel,
        grid=(m // tm,),
        in_specs=[row(RET_WIDTH), row(SWA_WIDTH), row(D_MODEL), full(gain), hbm, hbm, hbm],
        out_specs=[row(D_MODEL), hbm, hbm, hbm],
        out_shape=[jax.ShapeDtypeStruct((m, D_MODEL), F32),
                   jax.ShapeDtypeStruct(w_out.shape, BF16),
                   jax.ShapeDtypeStruct(w_up.shape, BF16),
                   jax.ShapeDtypeStruct(w_down.shape, BF16)],
        scratch_shapes=[
            pltpu.VMEM(w_out.shape, BF16), pltpu.VMEM(w_up.shape, BF16), pltpu.VMEM(w_down.shape, BF16),
            pltpu.VMEM((D_MODEL // CAST_CHUNK, CAST_CHUNK, D_MODEL), F32),
            pltpu.VMEM((2, D_MODEL, CAST_CHUNK), F32),
            pltpu.VMEM((2, CAST_CHUNK, D_MODEL), F32),
            pltpu.SemaphoreType.DMA((3, 2)),
            pltpu.SemaphoreType.DMA((3,)),
        ],
        compiler_params=pltpu.CompilerParams(
            dimension_semantics=("arbitrary",), vmem_limit_bytes=VMEM_LIMIT),
        name="out_mlp_cast",
    )(mix_ret, mix_swa, x2d, gain, w_out, w_up, w_down)


def _split_pair_rows(x, first):
    return jnp.concatenate([jnp.where(first, x, 0.0), jnp.where(first, 0.0, x)], axis=0).astype(BF16)


def _softmax_sink_pv(s, sink_wide, v_t):
    m = jnp.maximum(jnp.max(s, axis=-1, keepdims=True), sink_wide)
    p = jnp.exp(s - jnp.concatenate([m, m], axis=1))
    denom = jnp.sum(p, axis=-1, keepdims=True) + jnp.exp(sink_wide - m)
    return _dot_nt(p.astype(BF16), v_t) / denom


def _softmax_sink_pv_t(s_t, sink_lanes, v_t):
    m = jnp.maximum(jnp.max(s_t, axis=0, keepdims=True), sink_lanes)
    p = jnp.exp(s_t - m)
    denom = jnp.sum(p, axis=0, keepdims=True) + jnp.exp(sink_lanes - m)
    return _dot(v_t, p.astype(BF16)) / denom


def _prompt_consts(intra_ref, qdec_ref, kdec_ref, sdec_ref, bias_ref):
    r = _iota((LANES, LANES), 0)
    lane2 = _iota((LANES, LANES), 1) >= HEAD_DIM
    rf = r.astype(F32)
    ri = _iota((LANES, 2 * LANES), 0)
    ci = _iota((LANES, 2 * LANES), 1)
    diff = (ri - (ci & (LANES - 1))).astype(F32)
    for p in range(N_PAIRS):
        lg = _pair_const(LOG_DECAY, p, lane2)
        qdec_ref[p] = jnp.exp(lg * (rf + 1.0))
        kdec_ref[p] = jnp.exp(lg * (RET_CHUNK - 1.0 - rf))
        sdec_ref[p] = jnp.exp(_pair_const(LOG_DECAY, p, r >= HEAD_DIM) * float(RET_CHUNK))
        lg2 = _pair_const(LOG_DECAY, p, ci >= LANES)
        intra_ref[p] = jnp.where(diff >= 0.0, jnp.exp(lg2 * jnp.maximum(diff, 0.0)), 0.0)
    cols = SWA_GROUP * WINDOW
    kb = _iota((2 * WINDOW, cols), 0)
    cb = _iota((2 * WINDOW, cols), 1)
    grp = cb >> 7
    dist = WINDOW + (cb & (WINDOW - 1)) - kb
    valid = (dist >= 0) & (dist < WINDOW)
    distf = dist.astype(F32)
    for j in range(N_SWA_KV):
        sl = [ALIBI_SLOPES[SWA_GROUP * j + g] for g in range(SWA_GROUP)]
        slope = jnp.where(grp == 0, sl[0], jnp.where(grp == 1, sl[1], jnp.where(grp == 2, sl[2], sl[3])))
        b = jnp.where(valid, -(slope.astype(F32) * distf), NEG_INF)
        bias_ref[0, j] = b
        bias_ref[1, j] = jnp.where(kb >= WINDOW, b, NEG_INF)


def _prompt_layer_kernel(sinks_ref, x_ref, x_next_ref, gain_mix_ref, w_in_ref, qg_ref, kg_ref,
                         w_out_hbm, gain_ffn_ref, w_up_hbm, w_down_hbm,
                         y_ref, ret_ref, kwin_ref, vwin_ref,
                         main_ref, qn_ref, kn_ref, vs_ref, mix_ref, hb_ref,
                         state_ref, prevk_ref, prevv_ref,
                         intra_ref, qdec_ref, kdec_ref, sdec_ref, bias_ref,
                         w_out_ref, w_up_ref, w_down_ref, w_sem):
    t = pl.program_id(1)
    step = pl.program_id(0) * pl.num_programs(1) + t
    cur = step % 2
    nxt = 1 - cur

    late_weights = [pltpu.make_async_copy(src, dst, w_sem.at[i]) for i, (src, dst) in enumerate(
        [(w_out_hbm, w_out_ref), (w_up_hbm, w_up_ref), (w_down_hbm, w_down_ref)])]

    @pl.when(step == 0)
    def _():
        for copy in late_weights:
            copy.start()
        _prompt_consts(intra_ref, qdec_ref, kdec_ref, sdec_ref, bias_ref)
        _in_proj_kernel(x_ref, gain_mix_ref, w_in_ref, qg_ref, kg_ref,
                        main_ref.at[0], qn_ref.at[0], kn_ref.at[0], vs_ref.at[0])

    @pl.when(t == 0)
    def _():
        state_ref[...] = jnp.zeros_like(state_ref)
        prevk_ref[...] = jnp.zeros_like(prevk_ref)
        prevv_ref[...] = jnp.zeros_like(prevv_ref)

    first = _first_half()
    ones_bd = _ones_block_diag()
    bd_mask = (_iota((LANES, LANES), 0) >= HEAD_DIM) == (_iota((LANES, LANES), 1) >= HEAD_DIM)

    n_chunks = PROMPT_TILE // RET_CHUNK
    assert n_chunks == 4
    plan = [dict(a=[], b=[]),
            dict(a=["m0", "m1"], b=[]),
            dict(a=["m2", "m3"], b=[]),
            dict(a=["v", "k"], b=["q0", "q1"])]
    a0 = MAIN_WIDTH
    attn_cols = {"v": (a0 + SWA_WIDTH + SWA_KV_WIDTH, SWA_KV_WIDTH), "k": (a0 + SWA_WIDTH, SWA_KV_WIDTH)}
    attn_cols.update({"q%d" % i: (a0 + Q_BLOCK * i, Q_BLOCK) for i in range(SWA_WIDTH // Q_BLOCK)})
    cols = lambda base, p: slice(base + p * LANES, base + (p + 1) * LANES)

    def project(items):
        raw = {}
        for it in items:
            if it[0] == "m":
                cb = int(it[1:])
                blk = slice(cb * PROJ_BLOCK, (cb + 1) * PROJ_BLOCK)
                main_ref[nxt, :, blk] = _dot(hb_ref[...], w_in_ref[:, blk])
            else:
                lo, width = attn_cols[it]
                raw[it] = _dot(hb_ref[...], w_in_ref[:, lo:lo + width])
        return raw

    def attn_scales(raw):
        return {name: [_head_rms_scale(val[:, cols(0, cq)], ones_bd) for cq in range(val.shape[1] // LANES)]
                for name, val in raw.items() if name != "v"}

    def store_attn(raw, scales):
        for name, val in raw.items():
            if name == "v":
                vs_ref[nxt] = val
            elif name == "k":
                kn_ref[nxt] = (val * scales[name][0]) * _pair_gain(kg_ref)
            else:
                base = attn_cols[name][0] - a0
                for cq in range(val.shape[1] // LANES):
                    qn_ref[nxt, :, cols(base, cq)] = (val[:, cols(0, cq)] * scales[name][cq]) * _pair_gain(qg_ref)

    pairs = range(N_PAIRS)
    kvs = range(N_SWA_KV)

    def stage1(c):
        rows = slice(c * RET_CHUNK, (c + 1) * RET_CHUNK)
        q = [main_ref[cur, rows, cols(0, p)] for p in pairs]
        k = [main_ref[cur, rows, cols(RET_WIDTH, p)] * K_SCALE for p in pairs]
        v = [main_ref[cur, rows, cols(2 * RET_WIDTH, p)] for p in pairs]
        state = [state_ref[p] for p in pairs]
        kc = kn_ref[cur, rows, :]
        k_sw = pltpu.roll(kc, HEAD_DIM, axis=1)
        v_t = vs_ref[cur, rows, :].T
        is_first = (t == 0).astype(jnp.int32) if c == 0 else 0
        k_dup =[(jnp.where(first, kc, k_sw) if j == 0 else jnp.where(first, k_sw, kc)).astype(BF16)
                 for j in kvs]
        v_tj = [v_t[j * HEAD_DIM:(j + 1) * HEAD_DIM].astype(BF16) for j in kvs]
        q_st = []
        for j in kvs:
            pieces = []
            for g in range(SWA_GROUP):
                qc = qn_ref[cur, rows, cols(0, 2 * j + g // 2)]
                pieces.append(jnp.where(first, qc, 0.0) if g % 2 == 0 else jnp.where(first, 0.0, qc))
            q_st.append(jnp.concatenate(pieces, axis=0).astype(BF16))

        s = [_dot_nt(q[p].astype(BF16), _split_pair_rows(k[p], first)) for p in pairs]
        s_t = [_dot_nt(jnp.concatenate([prevk_ref[j], k_dup[j]], axis=0), q_st[j]) for j in kvs]
        cross = [_dot((q[p] * qdec_ref[p]).astype(BF16), state[p].astype(BF16)) for p in pairs]
        upd = [_dot((k[p] * kdec_ref[p]).T.astype(BF16), v[p].astype(BF16)) for p in pairs]
        return dict(rows=rows, v=v, state=state, k_dup=k_dup, v_tj=v_tj, is_first=is_first,
                    s=s, s_t=s_t, cross=cross, upd=upd)

    def stage2(st):
        st["o"] = [_dot((st["s"][p] * intra_ref[p]).astype(BF16), _split_pair_rows(st["v"][p], first))
                   + st["cross"][p] for p in pairs]
        st["o_t"] = []
        for j in kvs:
            sink_lanes = jnp.concatenate(
                [jnp.full((1, WINDOW), sinks_ref[SWA_GROUP * j + g], F32) for g in range(SWA_GROUP)],
                axis=1)
            v_cat = jnp.concatenate([prevv_ref[j], st["v_tj"][j]], axis=1)
            st["o_t"].append(_softmax_sink_pv_t(st["s_t"][j] * K_SCALE + bias_ref[st["is_first"], j],
                                                sink_lanes, v_cat))
        for p in pairs:
            state_ref[p] = st["state"][p] * sdec_ref[p] + jnp.where(bd_mask, st["upd"][p], 0.0)
        for j in kvs:
            prevk_ref[j] = st["k_dup"][j]
            prevv_ref[j] = st["v_tj"][j]

    def stage3(st, raw):
        rows = st["rows"]
        scale = [_head_rms_scale(st["o"][p], ones_bd) for p in pairs]
        raw_scales = attn_scales(raw)
        for p in pairs:
            g = main_ref[cur, rows, cols(3 * RET_WIDTH, p)]
            mix_ref[rows, cols(0, p)] = (st["o"][p] * scale[p] * _silu(g)).astype(BF16)
        store_attn(raw, raw_scales)
        for j in kvs:
            o_t = st["o_t"][j]
            for half in range(2):
                pair_t = jnp.concatenate([o_t[:, (2 * half) * WINDOW:(2 * half + 1) * WINDOW],
                                          o_t[:, (2 * half + 1) * WINDOW:(2 * half + 2) * WINDOW]], axis=0)
                mix_ref[rows, cols(RET_WIDTH, 2 * j + half)] = pair_t.T.astype(BF16)

    xn = x_next_ref[...]
    hb_ref[...] = ((xn * lax.rsqrt(jnp.mean(xn * xn, axis=-1, keepdims=True) + EPS)) * gain_mix_ref[...]).astype(BF16)
    for c in range(n_chunks):
        st = stage1(c)
        raw = project(plan[c]["a"])
        stage2(st)
        raw.update(project(plan[c]["b"]))
        stage3(st, raw)

    @pl.when(t == pl.num_programs(1) - 1)
    def _():
        for p in range(N_PAIRS):
            s = state_ref[p]
            ret_ref[2 * p] = s[:HEAD_DIM, :HEAD_DIM]
            ret_ref[2 * p + 1] = s[HEAD_DIM:, HEAD_DIM:]
        last = slice(PROMPT_TILE - WINDOW, PROMPT_TILE)
        kwin_ref[...] = kn_ref[cur, last, :].T
        vwin_ref[...] = vs_ref[cur, last, :].T

    @pl.when(step == 0)
    def _():
        for copy in late_weights:
            copy.wait()

    _out_mlp_kernel(mix_ref.at[:, pl.ds(0, RET_WIDTH)], mix_ref.at[:, pl.ds(RET_WIDTH, SWA_WIDTH)], x_ref,
                    w_out_ref, gain_ffn_ref, w_up_ref, w_down_ref, y_ref)


def _prompt_layer(sinks, x2d, gain_mix, w_in_bf, qg, kg, w_out_bf, gain_ffn, w_up_bf, w_down_bf, batch, seq):
    nt = seq // PROMPT_TILE
    last_tile = batch * nt - 1
    row = lambda w: pl.BlockSpec((PROMPT_TILE, w), lambda b, t: (b * nt + t, 0))
    next_row = pl.BlockSpec((PROMPT_TILE, D_MODEL), lambda b, t: (jnp.minimum(b * nt + t + 1, last_tile), 0))
    full = lambda a: pl.BlockSpec(a.shape, lambda b, t: (0, 0), pipeline_mode=pl.Buffered(1))
    in_hbm = pl.BlockSpec(memory_space=pl.ANY)
    return pl.pallas_call(
        _prompt_layer_kernel,
        grid=(batch, nt),
        in_specs=[pl.BlockSpec(memory_space=pltpu.SMEM), row(D_MODEL), next_row,
                  full(gain_mix), full(w_in_bf), full(qg), full(kg),
                  in_hbm, full(gain_ffn), in_hbm, in_hbm],
        out_specs=[row(D_MODEL),
                   pl.BlockSpec((None, N_RET_HEADS, HEAD_DIM, HEAD_DIM), lambda b, t: (b, 0, 0, 0)),
                   pl.BlockSpec((None, SWA_KV_WIDTH, WINDOW), lambda b, t: (b, 0, 0)),
                   pl.BlockSpec((None, SWA_KV_WIDTH, WINDOW), lambda b, t: (b, 0, 0))],
        out_shape=[jax.ShapeDtypeStruct((batch * seq, D_MODEL), F32),
                   jax.ShapeDtypeStruct((batch, N_RET_HEADS, HEAD_DIM, HEAD_DIM), F32),
                   jax.ShapeDtypeStruct((batch, SWA_KV_WIDTH, WINDOW), F32),
                   jax.ShapeDtypeStruct((batch, SWA_KV_WIDTH, WINDOW), F32)],
        scratch_shapes=[
            pltpu.VMEM((2, PROMPT_TILE, MAIN_WIDTH), F32),
            pltpu.VMEM((2, PROMPT_TILE, SWA_WIDTH), F32),
            pltpu.VMEM((2, PROMPT_TILE, SWA_KV_WIDTH), F32),
            pltpu.VMEM((2, PROMPT_TILE, SWA_KV_WIDTH), F32),
            pltpu.VMEM((PROMPT_TILE, MIX_WIDTH), BF16),
            pltpu.VMEM((PROMPT_TILE, D_MODEL), BF16),
            pltpu.VMEM((N_PAIRS, LANES, LANES), F32),
            pltpu.VMEM((N_SWA_KV, WINDOW, LANES), BF16),
            pltpu.VMEM((N_SWA_KV, HEAD_DIM, WINDOW), BF16),
            pltpu.VMEM((N_PAIRS, LANES, 2 * LANES), F32),
            pltpu.VMEM((N_PAIRS, LANES, LANES), F32),
            pltpu.VMEM((N_PAIRS, LANES, LANES), F32),
            pltpu.VMEM((N_PAIRS, LANES, LANES), F32),
            pltpu.VMEM((2, N_SWA_KV, 2 * WINDOW, SWA_GROUP * WINDOW), F32),
            pltpu.VMEM(w_out_bf.shape, BF16),
            pltpu.VMEM(w_up_bf.shape, BF16),
            pltpu.VMEM(w_down_bf.shape, BF16),
            pltpu.SemaphoreType.DMA((3,)),
        ],
        compiler_params=pltpu.CompilerParams(
            dimension_semantics=("arbitrary", "arbitrary"), vmem_limit_bytes=PROMPT_VMEM_LIMIT),
        name="prompt_layer",
    )(sinks, x2d, x2d, gain_mix, w_in_bf, qg, kg, w_out_bf, gain_ffn, w_up_bf, w_down_bf)


def _decode_ret_kernel(dec_seq, nb, qdec_ref, kdec_ref, sdec_ref, intra_ref,
                       q_ref, k_ref, v_ref, g_ref, st_ref,
                       mix_ref, st_out_ref,
                       qt_ref, kt_ref, vt_ref, qdt_ref, kdt_ref, o_ref):
    pair = pl.program_id(0)
    halves = [slice(0, HEAD_DIM), slice(HEAD_DIM, 2 * HEAD_DIM)]
    for l in range(dec_seq):
        rows = pl.ds(l, nb, stride=dec_seq)
        q_t = q_ref[rows, :].T
        k_t = (k_ref[rows, :] * K_SCALE).T
        qt_ref[l] = q_t
        kt_ref[l] = k_t
        vt_ref[l] = v_ref[rows, :].T
        for hh in range(2):
            qdt_ref[l, halves[hh], :] = q_t[halves[hh]] * qdec_ref[2 * pair + hh, l]
            kdt_ref[l, halves[hh], :] = k_t[halves[hh]] * kdec_ref[2 * pair + hh, l]

    e_blk = HEAD_DIM // 2
    for hh in range(2):
        h = 2 * pair + hh
        hs = halves[hh]
        for l in range(dec_seq):
            acc = None
            for m in range(l + 1):
                sc = jnp.sum(qt_ref[l, hs, :] * kt_ref[m, hs, :], axis=0, keepdims=True) * intra_ref[h, l - m]
                term = sc * vt_ref[m, hs, :]
                acc = term if acc is None else acc + term
            o_ref[l, hs, :] = acc
        for eb in range(HEAD_DIM // e_blk):
            es = slice(eb * e_blk, (eb + 1) * e_blk)
            erows = slice(hh * HEAD_DIM + eb * e_blk, hh * HEAD_DIM + (eb + 1) * e_blk)

            def body(d, accs, hh=hh, h=h, es=es, erows=erows):
                s_d = st_ref[hh, d, es, :]
                row = pl.ds(hh * HEAD_DIM + d, 1)
                upd = s_d * sdec_ref[h]
                new_accs = []
                for l in range(dec_seq):
                    new_accs.append(accs[l] + qdt_ref[l, row, :] * s_d)
                    upd = upd + kdt_ref[l, row, :] * vt_ref[l, erows, :]
                st_out_ref[hh, d, es, :] = upd
                return tuple(new_accs)

            zero = jnp.zeros((e_blk, nb), F32)
            accs = lax.fori_loop(0, HEAD_DIM, body, tuple(zero for _ in range(dec_seq)), unroll=2)
            for l in range(dec_seq):
                o_ref[l, erows, :] = o_ref[l, erows, :] + accs[l]

    for l in range(dec_seq):
        o = o_ref[l]
        normed = []
        for hh in range(2):
            oh = o[halves[hh]]
            normed.append(oh * lax.rsqrt(jnp.mean(oh * oh, axis=0, keepdims=True) + EPS))
        rows = pl.ds(l, nb, stride=dec_seq)
        mix_ref[rows, :] = jnp.concatenate(normed, axis=0).T * _silu(g_ref[rows, :])


def _decode_ret(main, state_t, dec_seq):
    nb = state_t.shape[-1]
    m = main.shape[0]
    assert nb == LANES and m == nb * dec_seq
    steps = [j for j in range(dec_seq)]
    tab = lambda f: jnp.asarray([[f(h, j) for j in steps] for h in range(N_RET_HEADS)], F32)
    qdec = tab(lambda h, j: math.exp(LOG_DECAY[h] * (j + 1.0)))
    kdec = tab(lambda h, j: math.exp(LOG_DECAY[h] * (dec_seq - 1.0 - j)))
    intra = tab(lambda h, j: math.exp(LOG_DECAY[h] * j))
    sdec = jnp.asarray([math.exp(LOG_DECAY[h] * dec_seq) for h in range(N_RET_HEADS)], F32)
    smem = pl.BlockSpec(memory_space=pltpu.SMEM)
    col = lambda base: pl.BlockSpec((m, LANES), lambda p: (0, base + p))
    st_spec = pl.BlockSpec((2, HEAD_DIM, HEAD_DIM, nb), lambda p: (p, 0, 0, 0))
    stage = pltpu.VMEM((dec_seq, LANES, nb), F32)
    return pl.pallas_call(
        functools.partial(_decode_ret_kernel, dec_seq, nb),
        grid=(N_PAIRS,),
        in_specs=[smem, smem, smem, smem,
                  col(0), col(N_PAIRS), col(2 * N_PAIRS), col(3 * N_PAIRS), st_spec],
        out_specs=[pl.BlockSpec((m, LANES), lambda p: (0, p)), st_spec],
        out_shape=[jax.ShapeDtypeStruct((m, RET_WIDTH), F32),
                   jax.ShapeDtypeStruct(state_t.shape, F32)],
        scratch_shapes=[stage, stage, stage, stage, stage, stage],
        compiler_params=pltpu.CompilerParams(
            dimension_semantics=("arbitrary",), vmem_limit_bytes=VMEM_LIMIT),
        name="decode_ret",
    )(qdec, kdec, sdec, intra, main, main, main, main, state_t)


DEC_ROWS = 128
DEC_UNROLL = 8


def _decode_attn_consts(dec_seq, bias_ref):
    shift = dec_seq.bit_length() - 1
    rows = N_SWA_HEADS * dec_seq
    rb = _iota((rows, WINDOW), 0)
    cb = _iota((rows, WINDOW), 1)
    head = rb >> shift
    i = rb & (dec_seq - 1)
    slope = jnp.zeros((rows, WINDOW), F32)
    for h in range(N_SWA_HEADS):
        slope = jnp.where(head == h, ALIBI_SLOPES[h], slope)
    bias_ref[0] = jnp.where(cb > i, -(slope * (WINDOW + i - cb).astype(F32)), NEG_INF)
    m = cb & (dec_seq - 1)
    bias_ref[1] = jnp.where(m <= i, -(slope * (i - m).astype(F32)), NEG_INF)


def _decode_attn_kernel(dec_seq, sinks_ref, qn_ref, kn_ref, vs_ref, kt_ref, vt_ref,
                        mix_ref, kt_out_ref, vt_out_ref,
                        bias_ref, qbd_ref, oblk_ref, knew_ref, vnew_ref, knt_ref, vst_ref):
    @pl.when(pl.program_id(0) == 0)
    def _():
        _decode_attn_consts(dec_seq, bias_ref)

    first = _first_half()
    shift = dec_seq.bit_length() - 1

    kn_t = kn_ref[...].T
    vs_t = vs_ref[...].T
    knt_ref[...] = kn_t.astype(BF16)
    vst_ref[...] = vs_t.astype(BF16)
    for bb in range(DEC_GROUP):
        sh = (WINDOW - dec_seq - bb * dec_seq) % LANES
        knew_ref[bb] = pltpu.roll(kn_t, sh, axis=1) if sh else kn_t
        vnew_ref[bb] = pltpu.roll(vs_t, sh, axis=1) if sh else vs_t
    qn = qn_ref[...]
    qn_sw = pltpu.roll(qn, HEAD_DIM, axis=1)
    for h in range(N_SWA_HEADS):
        kv_half = h // SWA_GROUP
        if (h % 2) == kv_half:
            src = qn[:, (h // 2) * LANES:(h // 2 + 1) * LANES]
        else:
            col = (h + 1) // 2
            src = qn_sw[:, col * LANES:(col + 1) * LANES]
        qbd_ref[h] = jnp.where(first, src, 0.0) if kv_half == 0 else jnp.where(first, 0.0, src)

    sink_rows = jnp.concatenate(
        [jnp.full((dec_seq, LANES), sinks_ref[h], F32) for h in range(N_SWA_HEADS)], axis=0)
    col_batch = _iota((N_SWA_HEADS * dec_seq, LANES), 1) >> shift
    keep_old = _iota((1, LANES), 1) < WINDOW - dec_seq

    def per_batches(i, carry):
        bs = [i * DEC_UNROLL + u for u in range(DEC_UNROLL)]
        rows = [pl.ds(pl.multiple_of(b * dec_seq, dec_seq), dec_seq) for b in bs]
        k_old = [kt_ref[b] for b in bs]
        v_old = [vt_ref[b] for b in bs]
        q_st = [jnp.concatenate([qbd_ref[h, r, :] for h in range(N_SWA_HEADS)], axis=0).astype(BF16)
                for r in rows]
        s = [_dot(q_st[u], jnp.concatenate([k_old[u].astype(BF16), knt_ref[...]], axis=1))
             for u in range(DEC_UNROLL)]
        o = []
        for u, b in enumerate(bs):
            bias = jnp.concatenate([bias_ref[0], jnp.where(col_batch == b, bias_ref[1], NEG_INF)], axis=1)
            w_v = jnp.concatenate([v_old[u].astype(BF16), vst_ref[...]], axis=1)
            o.append(_softmax_sink_pv(s[u] * K_SCALE + bias, sink_rows, w_v))
        for u, b in enumerate(bs):
            for h in range(N_SWA_HEADS):
                oblk_ref[h, rows[u], :] = o[u][h * dec_seq:(h + 1) * dec_seq]
            kt_out_ref[b] = jnp.where(keep_old, pltpu.roll(k_old[u], LANES - dec_seq, axis=1), knew_ref[b])
            vt_out_ref[b] = jnp.where(keep_old, pltpu.roll(v_old[u], LANES - dec_seq, axis=1), vnew_ref[b])
        return carry

    lax.fori_loop(0, DEC_GROUP // DEC_UNROLL, per_batches, 0)

    y1 = jnp.where(first, oblk_ref[3], oblk_ref[4])
    moved = pltpu.roll(jnp.concatenate([oblk_ref[1], y1, oblk_ref[6], oblk_ref[6]], axis=1), HEAD_DIM, axis=1)
    outs = [
        jnp.where(first, oblk_ref[0], moved[:, 0:LANES]),
        jnp.where(first, oblk_ref[2], moved[:, LANES:2 * LANES]),
        jnp.where(first, moved[:, 2 * LANES:3 * LANES], oblk_ref[5]),
        jnp.where(first, moved[:, 3 * LANES:4 * LANES], oblk_ref[7]),
    ]
    for c in range(SWA_WIDTH // LANES):
        mix_ref[:, c * LANES:(c + 1) * LANES] = outs[c].astype(BF16)


def _decode_attn(sinks, qn, kn, vs, k_t, v_t, dec_seq):
    nb = k_t.shape[0]
    assert DEC_GROUP * dec_seq == DEC_ROWS and nb % DEC_GROUP == 0 and dec_seq & (dec_seq - 1) == 0
    assert k_t.shape[1:] == (SWA_KV_WIDTH, WINDOW)
    row = lambda w: pl.BlockSpec((DEC_ROWS, w), lambda i: (i, 0))
    cache = pl.BlockSpec((DEC_GROUP, SWA_KV_WIDTH, WINDOW), lambda i: (i, 0, 0))
    return pl.pallas_call(
        functools.partial(_decode_attn_kernel, dec_seq),
        grid=(nb // DEC_GROUP,),
        in_specs=[pl.BlockSpec(memory_space=pltpu.SMEM),
                  row(SWA_WIDTH), row(SWA_KV_WIDTH), row(SWA_KV_WIDTH), cache, cache],
        out_specs=[row(SWA_WIDTH), cache, cache],
        out_shape=[jax.ShapeDtypeStruct((nb * dec_seq, SWA_WIDTH), BF16),
                   jax.ShapeDtypeStruct(k_t.shape, F32),
                   jax.ShapeDtypeStruct(v_t.shape, F32)],
        scratch_shapes=[
            pltpu.VMEM((2, N_SWA_HEADS * dec_seq, WINDOW), F32),
            pltpu.VMEM((N_SWA_HEADS, DEC_ROWS, LANES), F32),
            pltpu.VMEM((N_SWA_HEADS, DEC_ROWS, LANES), F32),
            pltpu.VMEM((DEC_GROUP, SWA_KV_WIDTH, LANES), F32),
            pltpu.VMEM((DEC_GROUP, SWA_KV_WIDTH, LANES), F32),
            pltpu.VMEM((SWA_KV_WIDTH, DEC_ROWS), BF16),
            pltpu.VMEM((SWA_KV_WIDTH, DEC_ROWS), BF16),
        ],
        compiler_params=pltpu.CompilerParams(
            dimension_semantics=("arbitrary",), vmem_limit_bytes=VMEM_LIMIT),
        name="decode_attn",
    )(sinks, qn, kn, vs, k_t, v_t)


GROUPS_PER_STEP = 2


def _decode_mixers_kernel(dec_seq, nb, qdec_ref, kdec_ref, sdec_ref, intra_ref, sinks_ref,
                          q_ref, k_ref, v_ref, g_ref, st_ref, qn_ref, kn_ref, vs_ref, kt_ref, vt_ref,
                          mix_ret_ref, st_out_ref, mix_swa_ref, kt_out_ref, vt_out_ref,
                          qt_s, kt_s, vt_s, qdt_s, kdt_s, o_s,
                          bias_ref, qbd_ref, oblk_ref, knew_ref, vnew_ref, knt_ref, vst_ref):
    pair = pl.program_id(0)
    halves = [slice(0, HEAD_DIM), slice(HEAD_DIM, 2 * HEAD_DIM)]
    first = _first_half()
    shift = dec_seq.bit_length() - 1
    e_blk = HEAD_DIM // 2

    @pl.when(pair == 0)
    def _():
        _decode_attn_consts(dec_seq, bias_ref)

    def ret_stage():
        for l in range(dec_seq):
            rows = pl.ds(l, nb, stride=dec_seq)
            q_t = q_ref[rows, :].T
            k_t = (k_ref[rows, :] * K_SCALE).T
            qt_s[l] = q_t
            kt_s[l] = k_t
            vt_s[l] = v_ref[rows, :].T
            for hh in range(2):
                qdt_s[l, halves[hh], :] = q_t[halves[hh]] * qdec_ref[2 * pair + hh, l]
                kdt_s[l, halves[hh], :] = k_t[halves[hh]] * kdec_ref[2 * pair + hh, l]

    def ret_intra(hh):
        h = 2 * pair + hh
        hs = halves[hh]
        for l in range(dec_seq):
            acc = None
            for m in range(l + 1):
                sc = jnp.sum(qt_s[l, hs, :] * kt_s[m, hs, :], axis=0, keepdims=True) * intra_ref[h, l - m]
                term = sc * vt_s[m, hs, :]
                acc = term if acc is None else acc + term
            o_s[l, hs, :] = acc

    def ret_block(hh, eb):
        h = 2 * pair + hh
        es = slice(eb * e_blk, (eb + 1) * e_blk)
        erows = slice(hh * HEAD_DIM + eb * e_blk, hh * HEAD_DIM + (eb + 1) * e_blk)
        accs = [jnp.zeros((e_blk, nb), F32) for _ in range(dec_seq)]
        for d in range(HEAD_DIM):
            s_d = st_ref[hh, d, es, :]
            row = slice(hh * HEAD_DIM + d, hh * HEAD_DIM + d + 1)
            upd = s_d * sdec_ref[h]
            for l in range(dec_seq):
                accs[l] = accs[l] + qdt_s[l, row, :] * s_d
                upd = upd + kdt_s[l, row, :] * vt_s[l, erows, :]
            st_out_ref[hh, d, es, :] = upd
        for l in range(dec_seq):
            o_s[l, erows, :] = o_s[l, erows, :] + accs[l]

    def ret_finish():
        for l in range(dec_seq):
            o = o_s[l]
            normed = []
            for hh in range(2):
                oh = o[halves[hh]]
                normed.append(oh * lax.rsqrt(jnp.mean(oh * oh, axis=0, keepdims=True) + EPS))
            rows = pl.ds(l, nb, stride=dec_seq)
            mix_ret_ref[rows, :] = jnp.concatenate(normed, axis=0).T * _silu(g_ref[rows, :])

    sink_rows = jnp.concatenate(
        [jnp.full((dec_seq, LANES), sinks_ref[h], F32) for h in range(N_SWA_HEADS)], axis=0)
    col_batch = _iota((N_SWA_HEADS * dec_seq, LANES), 1) >> shift
    keep_old = _iota((1, LANES), 1) < WINDOW - dec_seq

    def attn_stage(gi):
        grows = slice(gi * DEC_ROWS, (gi + 1) * DEC_ROWS)
        kn_t = kn_ref[grows, :].T
        vs_t = vs_ref[grows, :].T
        knt_ref[...] = kn_t.astype(BF16)
        vst_ref[...] = vs_t.astype(BF16)
        for bb in range(DEC_GROUP):
            sh = (WINDOW - dec_seq - bb * dec_seq) % LANES
            knew_ref[bb] = pltpu.roll(kn_t, sh, axis=1) if sh else kn_t
            vnew_ref[bb] = pltpu.roll(vs_t, sh, axis=1) if sh else vs_t
        qn = qn_ref[grows, :]
        qn_sw = pltpu.roll(qn, HEAD_DIM, axis=1)
        for h in range(N_SWA_HEADS):
            kv_half = h // SWA_GROUP
            if (h % 2) == kv_half:
                src = qn[:, (h // 2) * LANES:(h // 2 + 1) * LANES]
            else:
                col = (h + 1) // 2
                src = qn_sw[:, col * LANES:(col + 1) * LANES]
            qbd_ref[h] = jnp.where(first, src, 0.0) if kv_half == 0 else jnp.where(first, 0.0, src)

    def attn_block(gi, i):
        bs = [i * DEC_UNROLL + u for u in range(DEC_UNROLL)]
        rows = [slice(b * dec_seq, (b + 1) * dec_seq) for b in bs]
        k_old = [kt_ref[gi * DEC_GROUP + b] for b in bs]
        v_old = [vt_ref[gi * DEC_GROUP + b] for b in bs]
        q_st = [jnp.concatenate([qbd_ref[h, r, :] for h in range(N_SWA_HEADS)], axis=0).astype(BF16)
                for r in rows]
        s = [_dot(q_st[u], jnp.concatenate([k_old[u].astype(BF16), knt_ref[...]], axis=1))
             for u in range(DEC_UNROLL)]
        o = []
        for u, b in enumerate(bs):
            bias = jnp.concatenate([bias_ref[0], jnp.where(col_batch == b, bias_ref[1], NEG_INF)], axis=1)
            w_v = jnp.concatenate([v_old[u].astype(BF16), vst_ref[...]], axis=1)
            o.append(_softmax_sink_pv(s[u] * K_SCALE + bias, sink_rows, w_v))
        for u, b in enumerate(bs):
            for h in range(N_SWA_HEADS):
                oblk_ref[h, rows[u], :] = o[u][h * dec_seq:(h + 1) * dec_seq]
            kt_out_ref[gi * DEC_GROUP + b] = jnp.where(
                keep_old, pltpu.roll(k_old[u], LANES - dec_seq, axis=1), knew_ref[b])
            vt_out_ref[gi * DEC_GROUP + b] = jnp.where(
                keep_old, pltpu.roll(v_old[u], LANES - dec_seq, axis=1), vnew_ref[b])

    def attn_finish(gi):
        grows = slice(gi * DEC_ROWS, (gi + 1) * DEC_ROWS)
        y1 = jnp.where(first, oblk_ref[3], oblk_ref[4])
        moved = pltpu.roll(jnp.concatenate([oblk_ref[1], y1, oblk_ref[6], oblk_ref[6]], axis=1), HEAD_DIM, axis=1)
        outs = [
            jnp.where(first, oblk_ref[0], moved[:, 0:LANES]),
            jnp.where(first, oblk_ref[2], moved[:, LANES:2 * LANES]),
            jnp.where(first, moved[:, 2 * LANES:3 * LANES], oblk_ref[5]),
            jnp.where(first, moved[:, 3 * LANES:4 * LANES], oblk_ref[7]),
        ]
        for c in range(SWA_WIDTH // LANES):
            mix_swa_ref[grows, c * LANES:(c + 1) * LANES] = outs[c].astype(BF16)

    ret_stage()
    for gi in range(GROUPS_PER_STEP):
        attn_stage(gi)
        ret_intra(gi)
        for i in range(DEC_GROUP // DEC_UNROLL):
            ret_block(gi, i)
            attn_block(gi, i)
        attn_finish(gi)
    ret_finish()


def _decode_mixers(sinks, main, qn, kn, vs, state_t, k_t, v_t, dec_seq):
    nb = state_t.shape[-1]
    m = main.shape[0]
    assert nb == LANES and m == nb * dec_seq and k_t.shape == (nb, SWA_KV_WIDTH, WINDOW)
    assert DEC_GROUP * dec_seq == DEC_ROWS and nb == N_PAIRS * GROUPS_PER_STEP * DEC_GROUP
    assert DEC_GROUP // DEC_UNROLL == HEAD_DIM // (HEAD_DIM // 2) and dec_seq & (dec_seq - 1) == 0
    tab = lambda f: jnp.asarray([[f(h, j) for j in range(dec_seq)] for h in range(N_RET_HEADS)], F32)
    qdec = tab(lambda h, j: math.exp(LOG_DECAY[h] * (j + 1.0)))
    kdec = tab(lambda h, j: math.exp(LOG_DECAY[h] * (dec_seq - 1.0 - j)))
    intra = tab(lambda h, j: math.exp(LOG_DECAY[h] * j))
    sdec = jnp.asarray([math.exp(LOG_DECAY[h] * dec_seq) for h in range(N_RET_HEADS)], F32)
    smem = pl.BlockSpec(memory_space=pltpu.SMEM)
    col = lambda base: pl.BlockSpec((m, LANES), lambda p: (0, base + p))
    st_spec = pl.BlockSpec((2, HEAD_DIM, HEAD_DIM, nb), lambda p: (p, 0, 0, 0))
    step_rows = GROUPS_PER_STEP * DEC_ROWS
    row = lambda w: pl.BlockSpec((step_rows, w), lambda p: (p, 0))
    cache = pl.BlockSpec((GROUPS_PER_STEP * DEC_GROUP, SWA_KV_WIDTH, WINDOW), lambda p: (p, 0, 0))
    stage = pltpu.VMEM((dec_seq, LANES, nb), F32)
    return pl.pallas_call(
        functools.partial(_decode_mixers_kernel, dec_seq, nb),
        grid=(N_PAIRS,),
        in_specs=[smem, smem, smem, smem, smem,
                  col(0), col(N_PAIRS), col(2 * N_PAIRS), col(3 * N_PAIRS), st_spec,
                  row(SWA_WIDTH), row(SWA_KV_WIDTH), row(SWA_KV_WIDTH), cache, cache],
        out_specs=[pl.BlockSpec((m, LANES), lambda p: (0, p)), st_spec, row(SWA_WIDTH), cache, cache],
        out_shape=[jax.ShapeDtypeStruct((m, RET_WIDTH), F32),
                   jax.ShapeDtypeStruct(state_t.shape, F32),
                   jax.ShapeDtypeStruct((m, SWA_WIDTH), BF16),
                   jax.ShapeDtypeStruct(k_t.shape, F32),
                   jax.ShapeDtypeStruct(v_t.shape, F32)],
        scratch_shapes=[
            stage, stage, stage, stage, stage, stage,
            pltpu.VMEM((2, N_SWA_HEADS * dec_seq, WINDOW), F32),
            pltpu.VMEM((N_SWA_HEADS, DEC_ROWS, LANES), F32),
            pltpu.VMEM((N_SWA_HEADS, DEC_ROWS, LANES), F32),
            pltpu.VMEM((DEC_GROUP, SWA_KV_WIDTH, LANES), F32),
            pltpu.VMEM((DEC_GROUP, SWA_KV_WIDTH, LANES), F32),
            pltpu.VMEM((SWA_KV_WIDTH, DEC_ROWS), BF16),
            pltpu.VMEM((SWA_KV_WIDTH, DEC_ROWS), BF16),
        ],
        compiler_params=pltpu.CompilerParams(
            dimension_semantics=("arbitrary",), vmem_limit_bytes=VMEM_LIMIT),
        name="decode_mixers",
    )(qdec, kdec, sdec, intra, sinks, main, main, main, main, state_t, qn, kn, vs, k_t, v_t)


def kernel(x_prompt, x_sample, state_ret, cache_swa_k, cache_swa_v, norm_mix_gain, w_in, q_norm_gain,
           k_norm_gain, attn_sinks, w_out, norm_ffn_gain, w_up, w_down):
    batch, seq, d = x_prompt.shape
    nb, dec_seq, _ = x_sample.shape
    wb = cache_swa_k.shape[1]
    assert d == D_MODEL and seq % PROMPT_TILE == 0 and wb == WINDOW

    gain_mix = norm_mix_gain.reshape(1, D_MODEL)
    gain_ffn = norm_ffn_gain.reshape(1, D_MODEL)
    qg = q_norm_gain.reshape(1, HEAD_DIM)
    kg = k_norm_gain.reshape(1, HEAD_DIM)

    def from_key_minor(a_t):
        return jnp.transpose(a_t.reshape(a_t.shape[0], N_SWA_KV, HEAD_DIM, WINDOW), (0, 3, 1, 2))

    def to_key_minor(a):
        return jnp.transpose(a, (0, 2, 3, 1)).reshape(a.shape[0], SWA_KV_WIDTH, WINDOW)

    xs = x_sample.reshape(nb * dec_seq, D_MODEL)
    main_s, qn_s, kn_s, vs_s, w_in_bf = _in_proj_cast(xs, gain_mix, w_in, qg, kg)
    mix_ret_s, state_t, mix_swa_s, k_t, v_t = _decode_mixers(
        attn_sinks, main_s, qn_s, kn_s, vs_s, jnp.transpose(state_ret, (1, 2, 3, 0)),
        to_key_minor(cache_swa_k), to_key_minor(cache_swa_v), dec_seq)
    y_s, w_out_bf, w_up_bf, w_down_bf = _out_mlp_cast(mix_ret_s, mix_swa_s, xs, gain_ffn, w_out, w_up, w_down)

    xp = x_prompt.reshape(batch * seq, D_MODEL)
    y_p, ret_p, kwin_t, vwin_t = _prompt_layer(attn_sinks, xp, gain_mix, w_in_bf, qg, kg, w_out_bf, gain_ffn,
                                               w_up_bf, w_down_bf, batch, seq)
    y_p = y_p.reshape(batch, seq, D_MODEL)

    return (y_p, y_s.reshape(nb, dec_seq, D_MODEL), ret_p, from_key_minor(kwin_t), from_key_minor(vwin_t),
            jnp.transpose(state_t, (3, 0, 1, 2)), from_key_minor(k_t), from_key_minor(v_t))
```

```python
import functools
import math

import jax
import jax.numpy as jnp
from jax import lax
from jax.experimental import pallas as pl
from jax.experimental.pallas import tpu as pltpu

F32 = jnp.float32
BF16 = jnp.bfloat16

D_MODEL = 1024
HEAD_DIM = 64
N_RET_HEADS = 8
N_SWA_HEADS = 8
N_SWA_KV = 2
SWA_GROUP = N_SWA_HEADS // N_SWA_KV
RET_WIDTH = N_RET_HEADS * HEAD_DIM
SWA_WIDTH = N_SWA_HEADS * HEAD_DIM
SWA_KV_WIDTH = N_SWA_KV * HEAD_DIM
MAIN_WIDTH = 4 * RET_WIDTH
IN_WIDTH = MAIN_WIDTH + SWA_WIDTH + 2 * SWA_KV_WIDTH
MIX_WIDTH = RET_WIDTH + SWA_WIDTH
D_FF = 4 * D_MODEL
WINDOW = 128
RET_CHUNK = 128
EPS = 1e-6
NEG_INF = -1e30

LANES = 128
N_PAIRS = N_RET_HEADS // 2
LOG_DECAY = [math.log(1.0 - 2.0 ** (-5.0 - h)) for h in range(N_RET_HEADS)]
ALIBI_SLOPES = [2.0 ** (-8.0 * (h + 1) / N_SWA_HEADS) for h in range(N_SWA_HEADS)]
K_SCALE = HEAD_DIM ** -0.5

MXU_WIDTH = 256
PROJ_BLOCK = 2 * MXU_WIDTH
Q_BLOCK = MXU_WIDTH
PROMPT_TILE = 512
DEC_GROUP = 16
VMEM_LIMIT = 56 * 1024 * 1024
PROMPT_VMEM_LIMIT = 62 * 1024 * 1024


def _dot(a, b):
    return jnp.dot(a, b, preferred_element_type=F32)


def _dot_nt(a, b):
    return lax.dot_general(a, b, (((1,), (1,)), ((), ())), preferred_element_type=F32)


def _iota(shape, dim):
    return lax.broadcasted_iota(jnp.int32, shape, dim)


def _ones_block_diag():
    same = ((_iota((2 * LANES, LANES), 0) >> 6) & 1) == (_iota((2 * LANES, LANES), 1) >> 6)
    return jnp.where(same, 1.0, 0.0).astype(BF16)


def _head_sumsq(x, ones_bd):
    x2 = x * x
    hi = x2.astype(BF16)
    lo = (x2 - hi.astype(F32)).astype(BF16)
    return _dot(jnp.concatenate([hi, lo], axis=1), ones_bd)


def _head_rms_scale(x, ones_bd):
    return lax.rsqrt(_head_sumsq(x, ones_bd) * (1.0 / HEAD_DIM) + EPS)


def _head_rms_scales(xs, ones_bd):
    if not xs:
        return []
    parts = []
    for x in xs:
        x2 = x * x
        hi = x2.astype(BF16)
        parts.append(jnp.concatenate([hi, (x2 - hi.astype(F32)).astype(BF16)], axis=1))
    total = _dot(jnp.concatenate(parts, axis=0) if len(parts) > 1 else parts[0], ones_bd)
    out, lo = [], 0
    for x in xs:
        out.append(lax.rsqrt(total[lo:lo + x.shape[0]] * (1.0 / HEAD_DIM) + EPS))
        lo += x.shape[0]
    return out


def _first_half():
    return _iota((1, LANES), 1) < HEAD_DIM


def _pair_const(values, pair, lane_is_second):
    return jnp.where(lane_is_second, values[2 * pair + 1], values[2 * pair]).astype(F32)


def _pair_gain(gain_ref):
    g = gain_ref[...]
    return jnp.concatenate([g, g], axis=1)


def _silu(g):
    return g * (1.0 / (1.0 + jnp.exp(-g)))


def _in_proj_kernel(x_ref, gain_ref, w_ref, qg_ref, kg_ref, main_ref, qn_ref, kn_ref, vs_ref, fetch=None):
    x = x_ref[...]
    ms = jnp.mean(x * x, axis=-1, keepdims=True)
    hb = ((x * lax.rsqrt(ms + EPS)) * gain_ref[...]).astype(BF16)
    ones_bd = _ones_block_diag()

    def project(lo, width):
        if fetch is not None:
            fetch(lo, width)
        return _dot(hb, w_ref[:, lo:lo + width])

    for c in range(MAIN_WIDTH // PROJ_BLOCK):
        main_ref[:, c * PROJ_BLOCK:(c + 1) * PROJ_BLOCK] = project(c * PROJ_BLOCK, PROJ_BLOCK)
    qs = project(MAIN_WIDTH, SWA_WIDTH)
    k0 = MAIN_WIDTH + SWA_WIDTH
    ks = project(k0, SWA_KV_WIDTH)
    vs_ref[...] = project(k0 + SWA_KV_WIDTH, SWA_KV_WIDTH)
    q_cols = [qs[:, c * LANES:(c + 1) * LANES] for c in range(SWA_WIDTH // LANES)]
    scales = _head_rms_scales(q_cols + [ks], ones_bd)
    for c, xc in enumerate(q_cols):
        qn_ref[:, c * LANES:(c + 1) * LANES] = (xc * scales[c]) * _pair_gain(qg_ref)
    kn_ref[...] = (ks * scales[-1]) * _pair_gain(kg_ref)


ROW_TILE = 512


W_IN_BLOCKS = [(c * PROJ_BLOCK, PROJ_BLOCK) for c in range(MAIN_WIDTH // PROJ_BLOCK)] + [(MAIN_WIDTH, SWA_WIDTH),
                                                                    (MAIN_WIDTH + SWA_WIDTH, 2 * SWA_KV_WIDTH)]
assert sum(w for _, w in W_IN_BLOCKS) == IN_WIDTH


def _in_proj_cast_kernel(x_ref, gain_ref, w_hbm, qg_ref, kg_ref,
                         main_ref, qn_ref, kn_ref, vs_ref, w_bf_hbm,
                         w_v, stage, in_sem, out_sem):
    step = pl.program_id(0)
    block_in = [pltpu.make_async_copy(w_hbm.at[:, pl.ds(lo, width)], stage.at[:, pl.ds(lo, width)], in_sem.at[i])
                for i, (lo, width) in enumerate(W_IN_BLOCKS)]
    writeback = pltpu.make_async_copy(w_v, w_bf_hbm, out_sem.at[0])

    @pl.when(step == 0)
    def _():
        for copy in block_in:
            copy.start()
        arrived = set()

        def fetch(lo, width):
            for i, (blo, bwidth) in enumerate(W_IN_BLOCKS):
                if blo <= lo < blo + bwidth and i not in arrived:
                    assert lo + width <= blo + bwidth
                    block_in[i].wait()
                    w_v[:, blo:blo + bwidth] = stage[:, blo:blo + bwidth].astype(BF16)
                    arrived.add(i)

        _in_proj_kernel(x_ref, gain_ref, w_v, qg_ref, kg_ref, main_ref, qn_ref, kn_ref, vs_ref, fetch=fetch)
        assert len(arrived) == len(W_IN_BLOCKS)
        writeback.start()

    @pl.when(step > 0)
    def _():
        _in_proj_kernel(x_ref, gain_ref, w_v, qg_ref, kg_ref, main_ref, qn_ref, kn_ref, vs_ref)

    @pl.when(step == pl.num_programs(0) - 1)
    def _():
        writeback.wait()


def _in_proj_cast(x2d, gain, w_in, qg, kg):
    m = x2d.shape[0]
    tm = min(ROW_TILE, m)
    row = lambda w: pl.BlockSpec((tm, w), lambda i: (i, 0))
    full = lambda a: pl.BlockSpec(a.shape, lambda i: (0, 0), pipeline_mode=pl.Buffered(1))
    hbm = pl.BlockSpec(memory_space=pl.ANY)
    return pl.pallas_call(
        _in_proj_cast_kernel,
        grid=(m // tm,),
        in_specs=[row(D_MODEL), full(gain), hbm, full(qg), full(kg)],
        out_specs=[row(MAIN_WIDTH), row(SWA_WIDTH), row(SWA_KV_WIDTH), row(SWA_KV_WIDTH), hbm],
        out_shape=[jax.ShapeDtypeStruct((m, MAIN_WIDTH), F32),
                   jax.ShapeDtypeStruct((m, SWA_WIDTH), F32),
                   jax.ShapeDtypeStruct((m, SWA_KV_WIDTH), F32),
                   jax.ShapeDtypeStruct((m, SWA_KV_WIDTH), F32),
                   jax.ShapeDtypeStruct(w_in.shape, BF16)],
        scratch_shapes=[pltpu.VMEM(w_in.shape, BF16), pltpu.VMEM(w_in.shape, F32),
                        pltpu.SemaphoreType.DMA((len(W_IN_BLOCKS),)), pltpu.SemaphoreType.DMA((1,))],
        compiler_params=pltpu.CompilerParams(
            dimension_semantics=("arbitrary",), vmem_limit_bytes=VMEM_LIMIT),
        name="in_proj_cast",
    )(x2d, gain, w_in, qg, kg)


FF_CHUNK = 1024


def _out_mlp_kernel(mix_ret_ref, mix_swa_ref, x_ref, w_out_ref, gain_ref, w_up_ref, w_down_ref, y_ref):
    h = x_ref[...] + (_dot(mix_ret_ref[...].astype(BF16), w_out_ref[:RET_WIDTH, :])
                      + _dot(mix_swa_ref[...].astype(BF16), w_out_ref[RET_WIDTH:, :]))
    ms = jnp.mean(h * h, axis=-1, keepdims=True)
    hf = ((h * lax.rsqrt(ms + EPS)) * gain_ref[...]).astype(BF16)
    ff = None
    for c in range(D_FF // FF_CHUNK):
        u = _dot(hf, w_up_ref[:, c * FF_CHUNK:(c + 1) * FF_CHUNK])
        a = jnp.maximum(u, 0.0)
        d = _dot((a * a).astype(BF16), w_down_ref[c * FF_CHUNK:(c + 1) * FF_CHUNK, :])
        ff = d if ff is None else ff + d
    y_ref[...] = h + ff


CAST_CHUNK = 512
N_FF_CHUNKS = D_FF // CAST_CHUNK


def _out_mlp_cast_kernel(mix_ret_ref, mix_swa_ref, x_ref, gain_ref, w_out_hbm, w_up_hbm, w_down_hbm,
                         y_ref, w_out_bf_hbm, w_up_bf_hbm, w_down_bf_hbm,
                         w_out_v, w_up_v, w_down_v, stage_out, stage_up, stage_down, in_sem, out_sem):
    step = pl.program_id(0)

    def out_in(i):
        return pltpu.make_async_copy(w_out_hbm.at[pl.ds(i * CAST_CHUNK, CAST_CHUNK), :], stage_out.at[i],
                                     in_sem.at[0, i])

    def up_in(c):
        return pltpu.make_async_copy(w_up_hbm.at[:, pl.ds(c * CAST_CHUNK, CAST_CHUNK)], stage_up.at[c % 2],
                                     in_sem.at[1, c % 2])

    def down_in(c):
        return pltpu.make_async_copy(w_down_hbm.at[pl.ds(c * CAST_CHUNK, CAST_CHUNK), :], stage_down.at[c % 2],
                                     in_sem.at[2, c % 2])

    writebacks = [pltpu.make_async_copy(src, dst, out_sem.at[i]) for i, (src, dst) in enumerate(
        [(w_out_v, w_out_bf_hbm), (w_up_v, w_up_bf_hbm), (w_down_v, w_down_bf_hbm)])]

    @pl.when(step == 0)
    def _():
        n_out = D_MODEL // CAST_CHUNK
        for i in range(n_out):
            out_in(i).start()
        up_in(0).start()
        down_in(0).start()
        for i in range(n_out):
            out_in(i).wait()
            w_out_v[i * CAST_CHUNK:(i + 1) * CAST_CHUNK, :] = stage_out[i].astype(BF16)
        writebacks[0].start()
        h = x_ref[...] + (_dot(mix_ret_ref[...].astype(BF16), w_out_v[:RET_WIDTH, :])
                          + _dot(mix_swa_ref[...].astype(BF16), w_out_v[RET_WIDTH:, :]))
        ms = jnp.mean(h * h, axis=-1, keepdims=True)
        hf = ((h * lax.rsqrt(ms + EPS)) * gain_ref[...]).astype(BF16)
        ff = None
        for c in range(N_FF_CHUNKS):
            chunk = slice(c * CAST_CHUNK, (c + 1) * CAST_CHUNK)
            if c + 1 < N_FF_CHUNKS:
                up_in(c + 1).start()
                down_in(c + 1).start()
            up_in(c).wait()
            down_in(c).wait()
            w_up_v[:, chunk] = stage_up[c % 2].astype(BF16)
            w_down_v[chunk, :] = stage_down[c % 2].astype(BF16)
            a = jnp.maximum(_dot(hf, w_up_v[:, chunk]), 0.0)
            d = _dot((a * a).astype(BF16), w_down_v[chunk, :])
            ff = d if ff is None else ff + d
        writebacks[1].start()
        writebacks[2].start()
        y_ref[...] = h + ff

    @pl.when(step > 0)
    def _():
        _out_mlp_kernel(mix_ret_ref, mix_swa_ref, x_ref, w_out_v, gain_ref, w_up_v, w_down_v, y_ref)

    @pl.when(step == pl.num_programs(0) - 1)
    def _():
        for wb in writebacks:
            wb.wait()


def _out_mlp_cast(mix_ret, mix_swa, x2d, gain, w_out, w_up, w_down):
    m = x2d.shape[0]
    tm = min(ROW_TILE, m)
    assert D_MODEL % CAST_CHUNK == 0 and D_FF % CAST_CHUNK == 0
    row = lambda w: pl.BlockSpec((tm, w), lambda i: (i, 0))
    full = lambda a: pl.BlockSpec(a.shape, lambda i: (0, 0), pipeline_mode=pl.Buffered(1))
    hbm = pl.BlockSpec(memory_space=pl.ANY)
    return pl.pallas_call(
        _out_mlp_cast_kernel,
        grid=(m // tm,),
        in_specs=[row(RET_WIDTH), row(SWA_WIDTH), row(D_MODEL), full(gain), hbm, hbm, hbm],
        out_specs=[row(D_MODEL), hbm, hbm, hbm],
        out_shape=[jax.ShapeDtypeStruct((m, D_MODEL), F32),
                   jax.ShapeDtypeStruct(w_out.shape, BF16),
                   jax.ShapeDtypeStruct(w_up.shape, BF16),
                   jax.ShapeDtypeStruct(w_down.shape, BF16)],
        scratch_shapes=[
            pltpu.VMEM(w_out.shape, BF16), pltpu.VMEM(w_up.shape, BF16), pltpu.VMEM(w_down.shape, BF16),
            pltpu.VMEM((D_MODEL // CAST_CHUNK, CAST_CHUNK, D_MODEL), F32),
            pltpu.VMEM((2, D_MODEL, CAST_CHUNK), F32),
            pltpu.VMEM((2, CAST_CHUNK, D_MODEL), F32),
            pltpu.SemaphoreType.DMA((3, 2)),
            pltpu.SemaphoreType.DMA((3,)),
        ],
        compiler_params=pltpu.CompilerParams(
            dimension_semantics=("arbitrary",), vmem_limit_bytes=VMEM_LIMIT),
        name="out_mlp_cast",
    )(mix_ret, mix_swa, x2d, gain, w_out, w_up, w_down)


def _split_pair_rows(x, first):
    return jnp.concatenate([jnp.where(first, x, 0.0), jnp.where(first, 0.0, x)], axis=0).astype(BF16)


def _softmax_sink_pv(s, sink_wide, v_t):
    m = jnp.maximum(jnp.max(s, axis=-1, keepdims=True), sink_wide)
    p = jnp.exp(s - jnp.concatenate([m, m], axis=1))
    denom = jnp.sum(p, axis=-1, keepdims=True) + jnp.exp(sink_wide - m)
    return _dot_nt(p.astype(BF16), v_t) / denom


def _softmax_sink_pv_t(s_t, sink_lanes, v_t):
    m = jnp.maximum(jnp.max(s_t, axis=0, keepdims=True), sink_lanes)
    p = jnp.exp(s_t - m)
    denom = jnp.sum(p, axis=0, keepdims=True) + jnp.exp(sink_lanes - m)
    return _dot(v_t, p.astype(BF16)) / denom


def _prompt_consts(intra_ref, qdec_ref, kdec_ref, sdec_ref, bias_ref):
    r = _iota((LANES, LANES), 0)
    lane2 = _iota((LANES, LANES), 1) >= HEAD_DIM
    rf = r.astype(F32)
    ri = _iota((LANES, 2 * LANES), 0)
    ci = _iota((LANES, 2 * LANES), 1)
    diff = (ri - (ci & (LANES - 1))).astype(F32)
    for p in range(N_PAIRS):
        lg = _pair_const(LOG_DECAY, p, lane2)
        qdec_ref[p] = jnp.exp(lg * (rf + 1.0))
        kdec_ref[p] = jnp.exp(lg * (RET_CHUNK - 1.0 - rf))
        sdec_ref[p] = jnp.exp(_pair_const(LOG_DECAY, p, r >= HEAD_DIM) * float(RET_CHUNK))
        lg2 = _pair_const(LOG_DECAY, p, ci >= LANES)
        intra_ref[p] = jnp.where(diff >= 0.0, jnp.exp(lg2 * jnp.maximum(diff, 0.0)), 0.0)
    cols = SWA_GROUP * WINDOW
    kb = _iota((2 * WINDOW, cols), 0)
    cb = _iota((2 * WINDOW, cols), 1)
    grp = cb >> 7
    dist = WINDOW + (cb & (WINDOW - 1)) - kb
    valid = (dist >= 0) & (dist < WINDOW)
    distf = dist.astype(F32)
    for j in range(N_SWA_KV):
        sl = [ALIBI_SLOPES[SWA_GROUP * j + g] for g in range(SWA_GROUP)]
        slope = jnp.where(grp == 0, sl[0], jnp.where(grp == 1, sl[1], jnp.where(grp == 2, sl[2], sl[3])))
        b = jnp.where(valid, -(slope.astype(F32) * distf), NEG_INF)
        bias_ref[0, j] = b
        bias_ref[1, j] = jnp.where(kb >= WINDOW, b, NEG_INF)


def _prompt_layer_kernel(sinks_ref, x_ref, x_next_ref, gain_mix_ref, w_in_ref, qg_ref, kg_ref,
                         w_out_hbm, gain_ffn_ref, w_up_hbm, w_down_hbm,
                         y_ref, ret_ref, kwin_ref, vwin_ref,
                         main_ref, qn_ref, kn_ref, vs_ref, mix_ref, hb_ref,
                         state_ref, prevk_ref, prevv_ref,
                         intra_ref, qdec_ref, kdec_ref, sdec_ref, bias_ref,
                         w_out_ref, w_up_ref, w_down_ref, w_sem):
    t = pl.program_id(1)
    step = pl.program_id(0) * pl.num_programs(1) + t
    cur = step % 2
    nxt = 1 - cur

    late_weights = [pltpu.make_async_copy(src, dst, w_sem.at[i]) for i, (src, dst) in enumerate(
        [(w_out_hbm, w_out_ref), (w_up_hbm, w_up_ref), (w_down_hbm, w_down_ref)])]

    @pl.when(step == 0)
    def _():
        for copy in late_weights:
            copy.start()
        _prompt_consts(intra_ref, qdec_ref, kdec_ref, sdec_ref, bias_ref)
        _in_proj_kernel(x_ref, gain_mix_ref, w_in_ref, qg_ref, kg_ref,
                        main_ref.at[0], qn_ref.at[0], kn_ref.at[0], vs_ref.at[0])

    @pl.when(t == 0)
    def _():
        state_ref[...] = jnp.zeros_like(state_ref)
        prevk_ref[...] = jnp.zeros_like(prevk_ref)
        prevv_ref[...] = jnp.zeros_like(prevv_ref)

    first = _first_half()
    ones_bd = _ones_block_diag()
    bd_mask = (_iota((LANES, LANES), 0) >= HEAD_DIM) == (_iota((LANES, LANES), 1) >= HEAD_DIM)

    n_chunks = PROMPT_TILE // RET_CHUNK
    assert n_chunks == 4
    plan = [dict(a=[], b=[]),
            dict(a=["m0", "m1"], b=[]),
            dict(a=["m2", "m3"], b=[]),
            dict(a=["v", "k"], b=["q0", "q1"])]
    a0 = MAIN_WIDTH
    attn_cols = {"v": (a0 + SWA_WIDTH + SWA_KV_WIDTH, SWA_KV_WIDTH), "k": (a0 + SWA_WIDTH, SWA_KV_WIDTH)}
    attn_cols.update({"q%d" % i: (a0 + Q_BLOCK * i, Q_BLOCK) for i in range(SWA_WIDTH // Q_BLOCK)})
    cols = lambda base, p: slice(base + p * LANES, base + (p + 1) * LANES)

    def project(items):
        raw = {}
        for it in items:
            if it[0] == "m":
                cb = int(it[1:])
                blk = slice(cb * PROJ_BLOCK, (cb + 1) * PROJ_BLOCK)
                main_ref[nxt, :, blk] = _dot(hb_ref[...], w_in_ref[:, blk])
            else:
                lo, width = attn_cols[it]
                raw[it] = _dot(hb_ref[...], w_in_ref[:, lo:lo + width])
        return raw

    def attn_scales(raw):
        return {name: [_head_rms_scale(val[:, cols(0, cq)], ones_bd) for cq in range(val.shape[1] // LANES)]
                for name, val in raw.items() if name != "v"}

    def store_attn(raw, scales):
        for name, val in raw.items():
            if name == "v":
                vs_ref[nxt] = val
            elif name == "k":
                kn_ref[nxt] = (val * scales[name][0]) * _pair_gain(kg_ref)
            else:
                base = attn_cols[name][0] - a0
                for cq in range(val.shape[1] // LANES):
                    qn_ref[nxt, :, cols(base, cq)] = (val[:, cols(0, cq)] * scales[name][cq]) * _pair_gain(qg_ref)

    pairs = range(N_PAIRS)
    kvs = range(N_SWA_KV)

    def stage1(c):
        rows = slice(c * RET_CHUNK, (c + 1) * RET_CHUNK)
        q = [main_ref[cur, rows, cols(0, p)] for p in pairs]
        k = [main_ref[cur, rows, cols(RET_WIDTH, p)] * K_SCALE for p in pairs]
        v = [main_ref[cur, rows, cols(2 * RET_WIDTH, p)] for p in pairs]
        state = [state_ref[p] for p in pairs]
        kc = kn_ref[cur, rows, :]
        k_sw = pltpu.roll(kc, HEAD_DIM, axis=1)
        v_t = vs_ref[cur, rows, :].T
        is_first = (t == 0).astype(jnp.int32) if c == 0 else 0
        k_dup =[(jnp.where(first, kc, k_sw) if j == 0 else jnp.where(first, k_sw, kc)).astype(BF16)
                 for j in kvs]
        v_tj = [v_t[j * HEAD_DIM:(j + 1) * HEAD_DIM].astype(BF16) for j in kvs]
        q_st = []
        for j in kvs:
            pieces = []
            for g in range(SWA_GROUP):
                qc = qn_ref[cur, rows, cols(0, 2 * j + g // 2)]
                pieces.append(jnp.where(first, qc, 0.0) if g % 2 == 0 else jnp.where(first, 0.0, qc))
            q_st.append(jnp.concatenate(pieces, axis=0).astype(BF16))

        s = [_dot_nt(q[p].astype(BF16), _split_pair_rows(k[p], first)) for p in pairs]
        s_t = [_dot_nt(jnp.concatenate([prevk_ref[j], k_dup[j]], axis=0), q_st[j]) for j in kvs]
        cross = [_dot((q[p] * qdec_ref[p]).astype(BF16), state[p].astype(BF16)) for p in pairs]
        upd = [_dot((k[p] * kdec_ref[p]).T.astype(BF16), v[p].astype(BF16)) for p in pairs]
        return dict(rows=rows, v=v, state=state, k_dup=k_dup, v_tj=v_tj, is_first=is_first,
                    s=s, s_t=s_t, cross=cross, upd=upd)

    def stage2(st):
        st["o"] = [_dot((st["s"][p] * intra_ref[p]).astype(BF16), _split_pair_rows(st["v"][p], first))
                   + st["cross"][p] for p in pairs]
        st["o_t"] = []
        for j in kvs:
            sink_lanes = jnp.concatenate(
                [jnp.full((1, WINDOW), sinks_ref[SWA_GROUP * j + g], F32) for g in range(SWA_GROUP)],
                axis=1)
            v_cat = jnp.concatenate([prevv_ref[j], st["v_tj"][j]], axis=1)
            st["o_t"].append(_softmax_sink_pv_t(st["s_t"][j] * K_SCALE + bias_ref[st["is_first"], j],
                                                sink_lanes, v_cat))
        for p in pairs:
            state_ref[p] = st["state"][p] * sdec_ref[p] + jnp.where(bd_mask, st["upd"][p], 0.0)
        for j in kvs:
            prevk_ref[j] = st["k_dup"][j]
            prevv_ref[j] = st["v_tj"][j]

    def stage3(st, raw):
        rows = st["rows"]
        scale = [_head_rms_scale(st["o"][p], ones_bd) for p in pairs]
        raw_scales = attn_scales(raw)
        for p in pairs:
            g = main_ref[cur, rows, cols(3 * RET_WIDTH, p)]
            mix_ref[rows, cols(0, p)] = (st["o"][p] * scale[p] * _silu(g)).astype(BF16)
        store_attn(raw, raw_scales)
        for j in kvs:
            o_t = st["o_t"][j]
            for half in range(2):
                pair_t = jnp.concatenate([o_t[:, (2 * half) * WINDOW:(2 * half + 1) * WINDOW],
                                          o_t[:, (2 * half + 1) * WINDOW:(2 * half + 2) * WINDOW]], axis=0)
                mix_ref[rows, cols(RET_WIDTH, 2 * j + half)] = pair_t.T.astype(BF16)

    xn = x_next_ref[...]
    hb_ref[...] = ((xn * lax.rsqrt(jnp.mean(xn * xn, axis=-1, keepdims=True) + EPS)) * gain_mix_ref[...]).astype(BF16)
    for c in range(n_chunks):
        st = stage1(c)
        raw = project(plan[c]["a"])
        stage2(st)
        raw.update(project(plan[c]["b"]))
        stage3(st, raw)

    @pl.when(t == pl.num_programs(1) - 1)
    def _():
        for p in range(N_PAIRS):
            s = state_ref[p]
            ret_ref[2 * p] = s[:HEAD_DIM, :HEAD_DIM]
            ret_ref[2 * p + 1] = s[HEAD_DIM:, HEAD_DIM:]
        last = slice(PROMPT_TILE - WINDOW, PROMPT_TILE)
        kwin_ref[...] = kn_ref[cur, last, :].T
        vwin_ref[...] = vs_ref[cur, last, :].T

    @pl.when(step == 0)
    def _():
        for copy in late_weights:
            copy.wait()

    _out_mlp_kernel(mix_ref.at[:, pl.ds(0, RET_WIDTH)], mix_ref.at[:, pl.ds(RET_WIDTH, SWA_WIDTH)], x_ref,
                    w_out_ref, gain_ffn_ref, w_up_ref, w_down_ref, y_ref)


def _prompt_layer(sinks, x2d, gain_mix, w_in_bf, qg, kg, w_out_bf, gain_ffn, w_up_bf, w_down_bf, batch, seq):
    nt = seq // PROMPT_TILE
    last_tile = batch * nt - 1
    row = lambda w: pl.BlockSpec((PROMPT_TILE, w), lambda b, t: (b * nt + t, 0))
    next_row = pl.BlockSpec((PROMPT_TILE, D_MODEL), lambda b, t: (jnp.minimum(b * nt + t + 1, last_tile), 0))
    full = lambda a: pl.BlockSpec(a.shape, lambda b, t: (0, 0), pipeline_mode=pl.Buffered(1))
    in_hbm = pl.BlockSpec(memory_space=pl.ANY)
    return pl.pallas_call(
        _prompt_layer_kernel,
        grid=(batch, nt),
        in_specs=[pl.BlockSpec(memory_space=pltpu.SMEM), row(D_MODEL), next_row,
                  full(gain_mix), full(w_in_bf), full(qg), full(kg),
                  in_hbm, full(gain_ffn), in_hbm, in_hbm],
        out_specs=[row(D_MODEL),
                   pl.BlockSpec((None, N_RET_HEADS, HEAD_DIM, HEAD_DIM), lambda b, t: (b, 0, 0, 0)),
                   pl.BlockSpec((None, SWA_KV_WIDTH, WINDOW), lambda b, t: (b, 0, 0)),
                   pl.BlockSpec((None, SWA_KV_WIDTH, WINDOW), lambda b, t: (b, 0, 0))],
        out_shape=[jax.ShapeDtypeStruct((batch * seq, D_MODEL), F32),
                   jax.ShapeDtypeStruct((batch, N_RET_HEADS, HEAD_DIM, HEAD_DIM), F32),
                   jax.ShapeDtypeStruct((batch, SWA_KV_WIDTH, WINDOW), F32),
                   jax.ShapeDtypeStruct((batch, SWA_KV_WIDTH, WINDOW), F32)],
        scratch_shapes=[
            pltpu.VMEM((2, PROMPT_TILE, MAIN_WIDTH), F32),
            pltpu.VMEM((2, PROMPT_TILE, SWA_WIDTH), F32),
            pltpu.VMEM((2, PROMPT_TILE, SWA_KV_WIDTH), F32),
            pltpu.VMEM((2, PROMPT_TILE, SWA_KV_WIDTH), F32),
            pltpu.VMEM((PROMPT_TILE, MIX_WIDTH), BF16),
            pltpu.VMEM((PROMPT_TILE, D_MODEL), BF16),
            pltpu.VMEM((N_PAIRS, LANES, LANES), F32),
            pltpu.VMEM((N_SWA_KV, WINDOW, LANES), BF16),
            pltpu.VMEM((N_SWA_KV, HEAD_DIM, WINDOW), BF16),
            pltpu.VMEM((N_PAIRS, LANES, 2 * LANES), F32),
            pltpu.VMEM((N_PAIRS, LANES, LANES), F32),
            pltpu.VMEM((N_PAIRS, LANES, LANES), F32),
            pltpu.VMEM((N_PAIRS, LANES, LANES), F32),
            pltpu.VMEM((2, N_SWA_KV, 2 * WINDOW, SWA_GROUP * WINDOW), F32),
            pltpu.VMEM(w_out_bf.shape, BF16),
            pltpu.VMEM(w_up_bf.shape, BF16),
            pltpu.VMEM(w_down_bf.shape, BF16),
            pltpu.SemaphoreType.DMA((3,)),
        ],
        compiler_params=pltpu.CompilerParams(
            dimension_semantics=("arbitrary", "arbitrary"), vmem_limit_bytes=PROMPT_VMEM_LIMIT),
        name="prompt_layer",
    )(sinks, x2d, x2d, gain_mix, w_in_bf, qg, kg, w_out_bf, gain_ffn, w_up_bf, w_down_bf)


DEC_ROWS = 128
DEC_UNROLL = 8


def _decode_attn_consts(dec_seq, bias_ref):
    shift = dec_seq.bit_length() - 1
    rows = N_SWA_HEADS * dec_seq
    rb = _iota((rows, WINDOW), 0)
    cb = _iota((rows, WINDOW), 1)
    head = rb >> shift
    i = rb & (dec_seq - 1)
    slope = jnp.zeros((rows, WINDOW), F32)
    for h in range(N_SWA_HEADS):
        slope = jnp.where(head == h, ALIBI_SLOPES[h], slope)
    bias_ref[0] = jnp.where(cb > i, -(slope * (WINDOW + i - cb).astype(F32)), NEG_INF)
    m = cb & (dec_seq - 1)
    bias_ref[1] = jnp.where(m <= i, -(slope * (i - m).astype(F32)), NEG_INF)


GROUPS_PER_STEP = 2


def _decode_mixers_kernel(dec_seq, nb, qdec_ref, kdec_ref, sdec_ref, intra_ref, sinks_ref,
                          q_ref, k_ref, v_ref, g_ref, st_ref, qn_ref, kn_ref, vs_ref, kt_ref, vt_ref,
                          mix_ret_ref, st_out_ref, mix_swa_ref, kt_out_ref, vt_out_ref,
                          qt_s, kt_s, vt_s, qdt_s, kdt_s, o_s,
                          bias_ref, qbd_ref, oblk_ref, knew_ref, vnew_ref, knt_ref, vst_ref):
    pair = pl.program_id(0)
    halves = [slice(0, HEAD_DIM), slice(HEAD_DIM, 2 * HEAD_DIM)]
    first = _first_half()
    shift = dec_seq.bit_length() - 1
    e_blk = HEAD_DIM // 2

    @pl.when(pair == 0)
    def _():
        _decode_attn_consts(dec_seq, bias_ref)

    def ret_stage():
        for l in range(dec_seq):
            rows = pl.ds(l, nb, stride=dec_seq)
            q_t = q_ref[rows, :].T
            k_t = (k_ref[rows, :] * K_SCALE).T
            qt_s[l] = q_t
            kt_s[l] = k_t
            vt_s[l] = v_ref[rows, :].T
            for hh in range(2):
                qdt_s[l, halves[hh], :] = q_t[halves[hh]] * qdec_ref[2 * pair + hh, l]
                kdt_s[l, halves[hh], :] = k_t[halves[hh]] * kdec_ref[2 * pair + hh, l]

    def ret_intra(hh):
        h = 2 * pair + hh
        hs = halves[hh]
        for l in range(dec_seq):
            acc = None
            for m in range(l + 1):
                sc = jnp.sum(qt_s[l, hs, :] * kt_s[m, hs, :], axis=0, keepdims=True) * intra_ref[h, l - m]
                term = sc * vt_s[m, hs, :]
                acc = term if acc is None else acc + term
            o_s[l, hs, :] = acc

    def ret_block(hh, eb, d_lo, d_hi, accs=None):
        h = 2 * pair + hh
        es = slice(eb * e_blk, (eb + 1) * e_blk)
        erows = slice(hh * HEAD_DIM + eb * e_blk, hh * HEAD_DIM + (eb + 1) * e_blk)
        if accs is None:
            accs = [jnp.zeros((e_blk, nb), F32) for _ in range(dec_seq)]
        for d in range(d_lo, d_hi):
            s_d = st_ref[hh, d, es, :]
            row = slice(hh * HEAD_DIM + d, hh * HEAD_DIM + d + 1)
            upd = s_d * sdec_ref[h]
            for l in range(dec_seq):
                accs[l] = accs[l] + qdt_s[l, row, :] * s_d
                upd = upd + kdt_s[l, row, :] * vt_s[l, erows, :]
            st_out_ref[hh, d, es, :] = upd
        if d_hi == HEAD_DIM:
            for l in range(dec_seq):
                o_s[l, erows, :] = o_s[l, erows, :] + accs[l]
        return accs

    def ret_finish():
        for l in range(dec_seq):
            o = o_s[l]
            normed = []
            for hh in range(2):
                oh = o[halves[hh]]
                normed.append(oh * lax.rsqrt(jnp.mean(oh * oh, axis=0, keepdims=True) + EPS))
            rows = pl.ds(l, nb, stride=dec_seq)
            mix_ret_ref[rows, :] = jnp.concatenate(normed, axis=0).T * _silu(g_ref[rows, :])

    sink_rows = jnp.concatenate(
        [jnp.full((dec_seq, LANES), sinks_ref[h], F32) for h in range(N_SWA_HEADS)], axis=0)
    col_batch = _iota((N_SWA_HEADS * dec_seq, LANES), 1) >> shift
    keep_old = _iota((1, LANES), 1) < WINDOW - dec_seq

    def attn_stage(gi):
        grows = slice(gi * DEC_ROWS, (gi + 1) * DEC_ROWS)
        kn_t = kn_ref[grows, :].T
        vs_t = vs_ref[grows, :].T
        knt_ref[...] = kn_t.astype(BF16)
        vst_ref[...] = vs_t.astype(BF16)
        for bb in range(DEC_GROUP):
            sh = (WINDOW - dec_seq - bb * dec_seq) % LANES
            knew_ref[bb] = pltpu.roll(kn_t, sh, axis=1) if sh else kn_t
            vnew_ref[bb] = pltpu.roll(vs_t, sh, axis=1) if sh else vs_t
        qn = qn_ref[grows, :]
        qn_sw = pltpu.roll(qn, HEAD_DIM, axis=1)
        for h in range(N_SWA_HEADS):
            kv_half = h // SWA_GROUP
            if (h % 2) == kv_half:
                src = qn[:, (h // 2) * LANES:(h // 2 + 1) * LANES]
            else:
                col = (h + 1) // 2
                src = qn_sw[:, col * LANES:(col + 1) * LANES]
            qbd_ref[h] = jnp.where(first, src, 0.0) if kv_half == 0 else jnp.where(first, 0.0, src)

    def attn_block(gi, i):
        bs = [i * DEC_UNROLL + u for u in range(DEC_UNROLL)]
        rows = [slice(b * dec_seq, (b + 1) * dec_seq) for b in bs]
        k_old = [kt_ref[gi * DEC_GROUP + b] for b in bs]
        v_old = [vt_ref[gi * DEC_GROUP + b] for b in bs]
        q_st = [jnp.concatenate([qbd_ref[h, r, :] for h in range(N_SWA_HEADS)], axis=0).astype(BF16)
                for r in rows]
        s = [_dot(q_st[u], jnp.concatenate([k_old[u].astype(BF16), knt_ref[...]], axis=1))
             for u in range(DEC_UNROLL)]
        o = []
        for u, b in enumerate(bs):
            bias = jnp.concatenate([bias_ref[0], jnp.where(col_batch == b, bias_ref[1], NEG_INF)], axis=1)
            w_v = jnp.concatenate([v_old[u].astype(BF16), vst_ref[...]], axis=1)
            o.append(_softmax_sink_pv(s[u] * K_SCALE + bias, sink_rows, w_v))
        for u, b in enumerate(bs):
            for h in range(N_SWA_HEADS):
                oblk_ref[h, rows[u], :] = o[u][h * dec_seq:(h + 1) * dec_seq]
            kt_out_ref[gi * DEC_GROUP + b] = jnp.where(
                keep_old, pltpu.roll(k_old[u], LANES - dec_seq, axis=1), knew_ref[b])
            vt_out_ref[gi * DEC_GROUP + b] = jnp.where(
                keep_old, pltpu.roll(v_old[u], LANES - dec_seq, axis=1), vnew_ref[b])

    def attn_finish(gi):
        grows = slice(gi * DEC_ROWS, (gi + 1) * DEC_ROWS)
        y1 = jnp.where(first, oblk_ref[3], oblk_ref[4])
        moved = pltpu.roll(jnp.concatenate([oblk_ref[1], y1, oblk_ref[6], oblk_ref[6]], axis=1), HEAD_DIM, axis=1)
        outs = [
            jnp.where(first, oblk_ref[0], moved[:, 0:LANES]),
            jnp.where(first, oblk_ref[2], moved[:, LANES:2 * LANES]),
            jnp.where(first, moved[:, 2 * LANES:3 * LANES], oblk_ref[5]),
            jnp.where(first, moved[:, 3 * LANES:4 * LANES], oblk_ref[7]),
        ]
        for c in range(SWA_WIDTH // LANES):
            mix_swa_ref[grows, c * LANES:(c + 1) * LANES] = outs[c].astype(BF16)

    ret_stage()
    for gi in range(GROUPS_PER_STEP):
        attn_stage(gi)
        ret_intra(gi)
        n_blocks = DEC_GROUP // DEC_UNROLL
        per_eb = n_blocks // 2
        d_step = HEAD_DIM // per_eb
        for eb in range(2):
            accs = None
            for part in range(per_eb):
                accs = ret_block(gi, eb, part * d_step, (part + 1) * d_step, accs)
                attn_block(gi, eb * per_eb + part)
        attn_finish(gi)
    ret_finish()


def _decode_mixers(sinks, main, qn, kn, vs, state_t, k_t, v_t, dec_seq):
    nb = state_t.shape[-1]
    m = main.shape[0]
    assert nb == LANES and m == nb * dec_seq and k_t.shape == (nb, SWA_KV_WIDTH, WINDOW)
    assert DEC_GROUP * dec_seq == DEC_ROWS and nb == N_PAIRS * GROUPS_PER_STEP * DEC_GROUP
    assert (DEC_GROUP // DEC_UNROLL) % 2 == 0 and dec_seq & (dec_seq - 1) == 0
    tab = lambda f: jnp.asarray([[f(h, j) for j in range(dec_seq)] for h in range(N_RET_HEADS)], F32)
    qdec = tab(lambda h, j: math.exp(LOG_DECAY[h] * (j + 1.0)))
    kdec = tab(lambda h, j: math.exp(LOG_DECAY[h] * (dec_seq - 1.0 - j)))
    intra = tab(lambda h, j: math.exp(LOG_DECAY[h] * j))
    sdec = jnp.asarray([math.exp(LOG_DECAY[h] * dec_seq) for h in range(N_RET_HEADS)], F32)
    smem = pl.BlockSpec(memory_space=pltpu.SMEM)
    col = lambda base: pl.BlockSpec((m, LANES), lambda p: (0, base + p))
    st_spec = pl.BlockSpec((2, HEAD_DIM, HEAD_DIM, nb), lambda p: (p, 0, 0, 0))
    step_rows = GROUPS_PER_STEP * DEC_ROWS
    row = lambda w: pl.BlockSpec((step_rows, w), lambda p: (p, 0))
    cache = pl.BlockSpec((GROUPS_PER_STEP * DEC_GROUP, SWA_KV_WIDTH, WINDOW), lambda p: (p, 0, 0))
    stage = pltpu.VMEM((dec_seq, LANES, nb), F32)
    return pl.pallas_call(
        functools.partial(_decode_mixers_kernel, dec_seq, nb),
        grid=(N_PAIRS,),
        in_specs=[smem, smem, smem, smem, smem,
                  col(0), col(N_PAIRS), col(2 * N_PAIRS), col(3 * N_PAIRS), st_spec,
                  row(SWA_WIDTH), row(SWA_KV_WIDTH), row(SWA_KV_WIDTH), cache, cache],
        out_specs=[pl.BlockSpec((m, LANES), lambda p: (0, p)), st_spec, row(SWA_WIDTH), cache, cache],
        out_shape=[jax.ShapeDtypeStruct((m, RET_WIDTH), F32),
                   jax.ShapeDtypeStruct(state_t.shape, F32),
                   jax.ShapeDtypeStruct((m, SWA_WIDTH), BF16),
                   jax.ShapeDtypeStruct(k_t.shape, F32),
                   jax.ShapeDtypeStruct(v_t.shape, F32)],
        scratch_shapes=[
            stage, stage, stage, stage, stage, stage,
            pltpu.VMEM((2, N_SWA_HEADS * dec_seq, WINDOW), F32),
            pltpu.VMEM((N_SWA_HEADS, DEC_ROWS, LANES), F32),
            pltpu.VMEM((N_SWA_HEADS, DEC_ROWS, LANES), F32),
            pltpu.VMEM((DEC_GROUP, SWA_KV_WIDTH, LANES), F32),
            pltpu.VMEM((DEC_GROUP, SWA_KV_WIDTH, LANES), F32),
            pltpu.VMEM((SWA_KV_WIDTH, DEC_ROWS), BF16),
            pltpu.VMEM((SWA_KV_WIDTH, DEC_ROWS), BF16),
        ],
        compiler_params=pltpu.CompilerParams(
            dimension_semantics=("arbitrary",), vmem_limit_bytes=VMEM_LIMIT),
        name="decode_mixers",
    )(qdec, kdec, sdec, intra, sinks, main, main, main, main, state_t, qn, kn, vs, k_t, v_t)


def kernel(x_prompt, x_sample, state_ret, cache_swa_k, cache_swa_v, norm_mix_gain, w_in, q_norm_gain,
           k_norm_gain, attn_sinks, w_out, norm_ffn_gain, w_up, w_down):
    batch, seq, d = x_prompt.shape
    nb, dec_seq, _ = x_sample.shape
    wb = cache_swa_k.shape[1]
    assert d == D_MODEL and seq % PROMPT_TILE == 0 and wb == WINDOW

    gain_mix = norm_mix_gain.reshape(1, D_MODEL)
    gain_ffn = norm_ffn_gain.reshape(1, D_MODEL)
    qg = q_norm_gain.reshape(1, HEAD_DIM)
    kg = k_norm_gain.reshape(1, HEAD_DIM)

    def from_key_minor(a_t):
        return jnp.transpose(a_t.reshape(a_t.shape[0], N_SWA_KV, HEAD_DIM, WINDOW), (0, 3, 1, 2))

    def to_key_minor(a):
        return jnp.transpose(a, (0, 2, 3, 1)).reshape(a.shape[0], SWA_KV_WIDTH, WINDOW)

    xs = x_sample.reshape(nb * dec_seq, D_MODEL)
    main_s, qn_s, kn_s, vs_s, w_in_bf = _in_proj_cast(xs, gain_mix, w_in, qg, kg)
    mix_ret_s, state_t, mix_swa_s, k_t, v_t = _decode_mixers(
        attn_sinks, main_s, qn_s, kn_s, vs_s, jnp.transpose(state_ret, (1, 2, 3, 0)),
        to_key_minor(cache_swa_k), to_key_minor(cache_swa_v), dec_seq)
    y_s, w_out_bf, w_up_bf, w_down_bf = _out_mlp_cast(mix_ret_s, mix_swa_s, xs, gain_ffn, w_out, w_up, w_down)

    xp = x_prompt.reshape(batch * seq, D_MODEL)
    y_p, ret_p, kwin_t, vwin_t = _prompt_layer(attn_sinks, xp, gain_mix, w_in_bf, qg, kg, w_out_bf, gain_ffn,
                                               w_up_bf, w_down_bf, batch, seq)
    y_p = y_p.reshape(batch, seq, D_MODEL)

    return (y_p, y_s.reshape(nb, dec_seq, D_MODEL), ret_p, from_key_minor(kwin_t), from_key_minor(vwin_t),
            jnp.transpose(state_t, (3, 0, 1, 2)), from_key_minor(k_t), from_key_minor(v_t))
```

```python
import functools
import math

import jax
import jax.numpy as jnp
from jax import lax
from jax.experimental import pallas as pl
from jax.experimental.pallas import tpu as pltpu

F32 = jnp.float32
BF16 = jnp.bfloat16

D_MODEL = 1024
HEAD_DIM = 64
N_RET_HEADS = 8
N_SWA_HEADS = 8
N_SWA_KV = 2
SWA_GROUP = N_SWA_HEADS // N_SWA_KV
RET_WIDTH = N_RET_HEADS * HEAD_DIM
SWA_WIDTH = N_SWA_HEADS * HEAD_DIM
SWA_KV_WIDTH = N_SWA_KV * HEAD_DIM
MAIN_WIDTH = 4 * RET_WIDTH
IN_WIDTH = MAIN_WIDTH + SWA_WIDTH + 2 * SWA_KV_WIDTH
MIX_WIDTH = RET_WIDTH + SWA_WIDTH
D_FF = 4 * D_MODEL
WINDOW = 128
RET_CHUNK = 128
EPS = 1e-6
NEG_INF = -1e30

LANES = 128
N_PAIRS = N_RET_HEADS // 2
LOG_DECAY = [math.log(1.0 - 2.0 ** (-5.0 - h)) for h in range(N_RET_HEADS)]
ALIBI_SLOPES = [2.0 ** (-8.0 * (h + 1) / N_SWA_HEADS) for h in range(N_SWA_HEADS)]
K_SCALE = HEAD_DIM ** -0.5

MXU_WIDTH = 256
PROJ_BLOCK = 2 * MXU_WIDTH
Q_BLOCK = MXU_WIDTH
PROMPT_TILE = 512
DEC_GROUP = 16
VMEM_LIMIT = 56 * 1024 * 1024
PROMPT_VMEM_LIMIT = 62 * 1024 * 1024


def _dot(a, b):
    return jnp.dot(a, b, preferred_element_type=F32)


def _dot_nt(a, b):
    return lax.dot_general(a, b, (((1,), (1,)), ((), ())), preferred_element_type=F32)


def _iota(shape, dim):
    return lax.broadcasted_iota(jnp.int32, shape, dim)


def _ones_block_diag():
    same = ((_iota((2 * LANES, LANES), 0) >> 6) & 1) == (_iota((2 * LANES, LANES), 1) >> 6)
    return jnp.where(same, 1.0, 0.0).astype(BF16)


def _head_sumsq(x, ones_bd):
    x2 = x * x
    hi = x2.astype(BF16)
    lo = (x2 - hi.astype(F32)).astype(BF16)
    return _dot(jnp.concatenate([hi, lo], axis=1), ones_bd)


def _head_rms_scale(x, ones_bd):
    return lax.rsqrt(_head_sumsq(x, ones_bd) * (1.0 / HEAD_DIM) + EPS)


def _head_rms_scales(xs, ones_bd):
    if not xs:
        return []
    parts = []
    for x in xs:
        x2 = x * x
        hi = x2.astype(BF16)
        parts.append(jnp.concatenate([hi, (x2 - hi.astype(F32)).astype(BF16)], axis=1))
    total = _dot(jnp.concatenate(parts, axis=0) if len(parts) > 1 else parts[0], ones_bd)
    out, lo = [], 0
    for x in xs:
        out.append(lax.rsqrt(total[lo:lo + x.shape[0]] * (1.0 / HEAD_DIM) + EPS))
        lo += x.shape[0]
    return out


def _first_half():
    return _iota((1, LANES), 1) < HEAD_DIM


def _pair_const(values, pair, lane_is_second):
    return jnp.where(lane_is_second, values[2 * pair + 1], values[2 * pair]).astype(F32)


def _pair_gain(gain_ref):
    g = gain_ref[...]
    return jnp.concatenate([g, g], axis=1)


def _silu(g):
    return g * (1.0 / (1.0 + jnp.exp(-g)))


def _in_proj_kernel(x_ref, gain_ref, w_ref, qg_ref, kg_ref, main_ref, qn_ref, kn_ref, vs_ref, fetch=None):
    x = x_ref[...]
    ms = jnp.mean(x * x, axis=-1, keepdims=True)
    hb = ((x * lax.rsqrt(ms + EPS)) * gain_ref[...]).astype(BF16)
    ones_bd = _ones_block_diag()

    def project(lo, width):
        if fetch is not None:
            fetch(lo, width)
        return _dot(hb, w_ref[:, lo:lo + width])

    for c in range(MAIN_WIDTH // PROJ_BLOCK):
        main_ref[:, c * PROJ_BLOCK:(c + 1) * PROJ_BLOCK] = project(c * PROJ_BLOCK, PROJ_BLOCK)
    qs = project(MAIN_WIDTH, SWA_WIDTH)
    k0 = MAIN_WIDTH + SWA_WIDTH
    ks = project(k0, SWA_KV_WIDTH)
    vs_ref[...] = project(k0 + SWA_KV_WIDTH, SWA_KV_WIDTH)
    q_cols = [qs[:, c * LANES:(c + 1) * LANES] for c in range(SWA_WIDTH // LANES)]
    scales = _head_rms_scales(q_cols + [ks], ones_bd)
    for c, xc in enumerate(q_cols):
        qn_ref[:, c * LANES:(c + 1) * LANES] = (xc * scales[c]) * _pair_gain(qg_ref)
    kn_ref[...] = (ks * scales[-1]) * _pair_gain(kg_ref)


ROW_TILE = 512


W_IN_BLOCKS = [(c * PROJ_BLOCK, PROJ_BLOCK) for c in range(MAIN_WIDTH // PROJ_BLOCK)] + [(MAIN_WIDTH, SWA_WIDTH),
                                                                    (MAIN_WIDTH + SWA_WIDTH, 2 * SWA_KV_WIDTH)]
assert sum(w for _, w in W_IN_BLOCKS) == IN_WIDTH


def _in_proj_cast_kernel(x_ref, gain_ref, w_hbm, qg_ref, kg_ref,
                         main_ref, qn_ref, kn_ref, vs_ref, w_bf_hbm,
                         w_v, stage, in_sem, out_sem):
    step = pl.program_id(0)
    block_in = [pltpu.make_async_copy(w_hbm.at[:, pl.ds(lo, width)], stage.at[:, pl.ds(lo, width)], in_sem.at[i])
                for i, (lo, width) in enumerate(W_IN_BLOCKS)]
    writeback = pltpu.make_async_copy(w_v, w_bf_hbm, out_sem.at[0])

    @pl.when(step == 0)
    def _():
        for copy in block_in:
            copy.start()
        arrived = set()

        def fetch(lo, width):
            for i, (blo, bwidth) in enumerate(W_IN_BLOCKS):
                if blo <= lo < blo + bwidth and i not in arrived:
                    assert lo + width <= blo + bwidth
                    block_in[i].wait()
                    w_v[:, blo:blo + bwidth] = stage[:, blo:blo + bwidth].astype(BF16)
                    arrived.add(i)

        _in_proj_kernel(x_ref, gain_ref, w_v, qg_ref, kg_ref, main_ref, qn_ref, kn_ref, vs_ref, fetch=fetch)
        assert len(arrived) == len(W_IN_BLOCKS)
        writeback.start()

    @pl.when(step > 0)
    def _():
        _in_proj_kernel(x_ref, gain_ref, w_v, qg_ref, kg_ref, main_ref, qn_ref, kn_ref, vs_ref)

    @pl.when(step == pl.num_programs(0) - 1)
    def _():
        writeback.wait()


def _in_proj_cast(x2d, gain, w_in, qg, kg):
    m = x2d.shape[0]
    tm = min(ROW_TILE, m)
    row = lambda w: pl.BlockSpec((tm, w), lambda i: (i, 0))
    full = lambda a: pl.BlockSpec(a.shape, lambda i: (0, 0), pipeline_mode=pl.Buffered(1))
    hbm = pl.BlockSpec(memory_space=pl.ANY)
    return pl.pallas_call(
        _in_proj_cast_kernel,
        grid=(m // tm,),
        in_specs=[row(D_MODEL), full(gain), hbm, full(qg), full(kg)],
        out_specs=[row(MAIN_WIDTH), row(SWA_WIDTH), row(SWA_KV_WIDTH), row(SWA_KV_WIDTH), hbm],
        out_shape=[jax.ShapeDtypeStruct((m, MAIN_WIDTH), F32),
                   jax.ShapeDtypeStruct((m, SWA_WIDTH), F32),
                   jax.ShapeDtypeStruct((m, SWA_KV_WIDTH), F32),
                   jax.ShapeDtypeStruct((m, SWA_KV_WIDTH), F32),
                   jax.ShapeDtypeStruct(w_in.shape, BF16)],
        scratch_shapes=[pltpu.VMEM(w_in.shape, BF16), pltpu.VMEM(w_in.shape, F32),
                        pltpu.SemaphoreType.DMA((len(W_IN_BLOCKS),)), pltpu.SemaphoreType.DMA((1,))],
        compiler_params=pltpu.CompilerParams(
            dimension_semantics=("arbitrary",), vmem_limit_bytes=VMEM_LIMIT),
        name="in_proj_cast",
    )(x2d, gain, w_in, qg, kg)


FF_CHUNK = 1024


def _out_mlp_kernel(mix_ret_ref, mix_swa_ref, x_ref, w_out_ref, gain_ref, w_up_ref, w_down_ref, y_ref):
    h = x_ref[...] + (_dot(mix_ret_ref[...].astype(BF16), w_out_ref[:RET_WIDTH, :])
                      + _dot(mix_swa_ref[...].astype(BF16), w_out_ref[RET_WIDTH:, :]))
    ms = jnp.mean(h * h, axis=-1, keepdims=True)
    hf = ((h * lax.rsqrt(ms + EPS)) * gain_ref[...]).astype(BF16)
    ff = None
    for c in range(D_FF // FF_CHUNK):
        u = _dot(hf, w_up_ref[:, c * FF_CHUNK:(c + 1) * FF_CHUNK])
        a = jnp.maximum(u, 0.0)
        d = _dot((a * a).astype(BF16), w_down_ref[c * FF_CHUNK:(c + 1) * FF_CHUNK, :])
        ff = d if ff is None else ff + d
    y_ref[...] = h + ff


CAST_CHUNK = 512
N_FF_CHUNKS = D_FF // CAST_CHUNK


def _out_mlp_cast_kernel(mix_ret_ref, mix_swa_ref, x_ref, gain_ref, w_out_hbm, w_up_hbm, w_down_hbm,
                         y_ref, w_out_bf_hbm, w_up_bf_hbm, w_down_bf_hbm,
                         w_out_v, w_up_v, w_down_v, stage_out, stage_up, stage_down, in_sem, out_sem):
    step = pl.program_id(0)

    def out_in(i):
        return pltpu.make_async_copy(w_out_hbm.at[pl.ds(i * CAST_CHUNK, CAST_CHUNK), :], stage_out.at[i],
                                     in_sem.at[0, i])

    def up_in(c):
        return pltpu.make_async_copy(w_up_hbm.at[:, pl.ds(c * CAST_CHUNK, CAST_CHUNK)], stage_up.at[c % 2],
                                     in_sem.at[1, c % 2])

    def down_in(c):
        return pltpu.make_async_copy(w_down_hbm.at[pl.ds(c * CAST_CHUNK, CAST_CHUNK), :], stage_down.at[c % 2],
                                     in_sem.at[2, c % 2])

    writebacks = [pltpu.make_async_copy(src, dst, out_sem.at[i]) for i, (src, dst) in enumerate(
        [(w_out_v, w_out_bf_hbm), (w_up_v, w_up_bf_hbm), (w_down_v, w_down_bf_hbm)])]

    @pl.when(step == 0)
    def _():
        n_out = D_MODEL // CAST_CHUNK
        for i in range(n_out):
            out_in(i).start()
        up_in(0).start()
        down_in(0).start()
        for i in range(n_out):
            out_in(i).wait()
            w_out_v[i * CAST_CHUNK:(i + 1) * CAST_CHUNK, :] = stage_out[i].astype(BF16)
        writebacks[0].start()
        h = x_ref[...] + (_dot(mix_ret_ref[...].astype(BF16), w_out_v[:RET_WIDTH, :])
                          + _dot(mix_swa_ref[...].astype(BF16), w_out_v[RET_WIDTH:, :]))
        ms = jnp.mean(h * h, axis=-1, keepdims=True)
        hf = ((h * lax.rsqrt(ms + EPS)) * gain_ref[...]).astype(BF16)
        ff = None
        for c in range(N_FF_CHUNKS):
            chunk = slice(c * CAST_CHUNK, (c + 1) * CAST_CHUNK)
            if c + 1 < N_FF_CHUNKS:
                up_in(c + 1).start()
                down_in(c + 1).start()
            up_in(c).wait()
            down_in(c).wait()
            w_up_v[:, chunk] = stage_up[c % 2].astype(BF16)
            w_down_v[chunk, :] = stage_down[c % 2].astype(BF16)
            a = jnp.maximum(_dot(hf, w_up_v[:, chunk]), 0.0)
            d = _dot((a * a).astype(BF16), w_down_v[chunk, :])
            ff = d if ff is None else ff + d
        writebacks[1].start()
        writebacks[2].start()
        y_ref[...] = h + ff

    @pl.when(step > 0)
    def _():
        _out_mlp_kernel(mix_ret_ref, mix_swa_ref, x_ref, w_out_v, gain_ref, w_up_v, w_down_v, y_ref)

    @pl.when(step == pl.num_programs(0) - 1)
    def _():
        for wb in writebacks:
            wb.wait()


def _out_mlp_cast(mix_ret, mix_swa, x2d, gain, w_out, w_up, w_down):
    m = x2d.shape[0]
    tm = min(ROW_TILE, m)
    assert D_MODEL % CAST_CHUNK == 0 and D_FF % CAST_CHUNK == 0
    row = lambda w: pl.BlockSpec((tm, w), lambda i: (i, 0))
    full = lambda a: pl.BlockSpec(a.shape, lambda i: (0, 0), pipeline_mode=pl.Buffered(1))
    hbm = pl.BlockSpec(memory_space=pl.ANY)
    return pl.pallas_call(
        _out_mlp_cast_kernel,
        grid=(m // tm,),
        in_specs=[row(RET_WIDTH), row(SWA_WIDTH), row(D_MODEL), full(gain), hbm, hbm, hbm],
        out_specs=[row(D_MODEL), hbm, hbm, hbm],
        out_shape=[jax.ShapeDtypeStruct((m, D_MODEL), F32),
                   jax.ShapeDtypeStruct(w_out.shape, BF16),
                   jax.ShapeDtypeStruct(w_up.shape, BF16),
                   jax.ShapeDtypeStruct(w_down.shape, BF16)],
        scratch_shapes=[
            pltpu.VMEM(w_out.shape, BF16), pltpu.VMEM(w_up.shape, BF16), pltpu.VMEM(w_down.shape, BF16),
            pltpu.VMEM((D_MODEL // CAST_CHUNK, CAST_CHUNK, D_MODEL), F32),
            pltpu.VMEM((2, D_MODEL, CAST_CHUNK), F32),
            pltpu.VMEM((2, CAST_CHUNK, D_MODEL), F32),
            pltpu.SemaphoreType.DMA((3, 2)),
            pltpu.SemaphoreType.DMA((3,)),
        ],
        compiler_params=pltpu.CompilerParams(
            dimension_semantics=("arbitrary",), vmem_limit_bytes=VMEM_LIMIT),
        name="out_mlp_cast",
    )(mix_ret, mix_swa, x2d, gain, w_out, w_up, w_down)


def _split_pair_rows(x, first):
    return jnp.concatenate([jnp.where(first, x, 0.0), jnp.where(first, 0.0, x)], axis=0).astype(BF16)


def _softmax_sink_pv(s, sink_wide, v_t):
    m = jnp.maximum(jnp.max(s, axis=-1, keepdims=True), sink_wide)
    p = jnp.exp(s - jnp.concatenate([m, m], axis=1))
    denom = jnp.sum(p, axis=-1, keepdims=True) + jnp.exp(sink_wide - m)
    return _dot_nt(p.astype(BF16), v_t) / denom


def _softmax_sink_pv_t(s_t, sink_lanes, v_t):
    m = jnp.maximum(jnp.max(s_t, axis=0, keepdims=True), sink_lanes)
    p = jnp.exp(s_t - m)
    denom = jnp.sum(p, axis=0, keepdims=True) + jnp.exp(sink_lanes - m)
    return _dot(v_t, p.astype(BF16)) / denom


def _prompt_consts(intra_ref, qdec_ref, kdec_ref, sdec_ref, bias_ref):
    r = _iota((LANES, LANES), 0)
    lane2 = _iota((LANES, LANES), 1) >= HEAD_DIM
    rf = r.astype(F32)
    ri = _iota((LANES, 2 * LANES), 0)
    ci = _iota((LANES, 2 * LANES), 1)
    diff = (ri - (ci & (LANES - 1))).astype(F32)
    for p in range(N_PAIRS):
        lg = _pair_const(LOG_DECAY, p, lane2)
        qdec_ref[p] = jnp.exp(lg * (rf + 1.0))
        kdec_ref[p] = jnp.exp(lg * (RET_CHUNK - 1.0 - rf))
        sdec_ref[p] = jnp.exp(_pair_const(LOG_DECAY, p, r >= HEAD_DIM) * float(RET_CHUNK))
        lg2 = _pair_const(LOG_DECAY, p, ci >= LANES)
        intra_ref[p] = jnp.where(diff >= 0.0, jnp.exp(lg2 * jnp.maximum(diff, 0.0)), 0.0)
    cols = SWA_GROUP * WINDOW
    kb = _iota((2 * WINDOW, cols), 0)
    cb = _iota((2 * WINDOW, cols), 1)
    grp = cb >> 7
    dist = WINDOW + (cb & (WINDOW - 1)) - kb
    valid = (dist >= 0) & (dist < WINDOW)
    distf = dist.astype(F32)
    for j in range(N_SWA_KV):
        sl = [ALIBI_SLOPES[SWA_GROUP * j + g] for g in range(SWA_GROUP)]
        slope = jnp.where(grp == 0, sl[0], jnp.where(grp == 1, sl[1], jnp.where(grp == 2, sl[2], sl[3])))
        b = jnp.where(valid, -(slope.astype(F32) * distf), NEG_INF)
        bias_ref[0, j] = b
        bias_ref[1, j] = jnp.where(kb >= WINDOW, b, NEG_INF)


def _prompt_layer_kernel(sinks_ref, x_ref, x_next_ref, gain_mix_ref, w_in_ref, qg_ref, kg_ref,
                         w_out_hbm, gain_ffn_ref, w_up_hbm, w_down_hbm,
                         y_ref, ret_ref, kwin_ref, vwin_ref,
                         main_ref, qn_ref, kn_ref, vs_ref, mix_ref, hb_ref,
                         state_ref, prevk_ref, prevv_ref,
                         intra_ref, qdec_ref, kdec_ref, sdec_ref, bias_ref,
                         w_out_ref, w_up_ref, w_down_ref, w_sem):
    t = pl.program_id(1)
    step = pl.program_id(0) * pl.num_programs(1) + t
    cur = step % 2
    nxt = 1 - cur

    late_weights = [pltpu.make_async_copy(src, dst, w_sem.at[i]) for i, (src, dst) in enumerate(
        [(w_out_hbm, w_out_ref), (w_up_hbm, w_up_ref), (w_down_hbm, w_down_ref)])]

    @pl.when(step == 0)
    def _():
        for copy in late_weights:
            copy.start()
        _prompt_consts(intra_ref, qdec_ref, kdec_ref, sdec_ref, bias_ref)
        _in_proj_kernel(x_ref, gain_mix_ref, w_in_ref, qg_ref, kg_ref,
                        main_ref.at[0], qn_ref.at[0], kn_ref.at[0], vs_ref.at[0])

    @pl.when(t == 0)
    def _():
        state_ref[...] = jnp.zeros_like(state_ref)
        prevk_ref[...] = jnp.zeros_like(prevk_ref)
        prevv_ref[...] = jnp.zeros_like(prevv_ref)

    first = _first_half()
    ones_bd = _ones_block_diag()
    bd_mask = (_iota((LANES, LANES), 0) >= HEAD_DIM) == (_iota((LANES, LANES), 1) >= HEAD_DIM)

    n_chunks = PROMPT_TILE // RET_CHUNK
    assert n_chunks == 4
    plan = [dict(a=[], b=[]),
            dict(a=["m0", "m1"], b=[]),
            dict(a=["m2", "m3"], b=[]),
            dict(a=["v", "k"], b=["q0", "q1"])]
    a0 = MAIN_WIDTH
    attn_cols = {"v": (a0 + SWA_WIDTH + SWA_KV_WIDTH, SWA_KV_WIDTH), "k": (a0 + SWA_WIDTH, SWA_KV_WIDTH)}
    attn_cols.update({"q%d" % i: (a0 + Q_BLOCK * i, Q_BLOCK) for i in range(SWA_WIDTH // Q_BLOCK)})
    cols = lambda base, p: slice(base + p * LANES, base + (p + 1) * LANES)

    def project(items):
        raw = {}
        for it in items:
            if it[0] == "m":
                cb = int(it[1:])
                blk = slice(cb * PROJ_BLOCK, (cb + 1) * PROJ_BLOCK)
                main_ref[nxt, :, blk] = _dot(hb_ref[...], w_in_ref[:, blk])
            else:
                lo, width = attn_cols[it]
                raw[it] = _dot(hb_ref[...], w_in_ref[:, lo:lo + width])
        return raw

    def attn_scales(raw):
        return {name: [_head_rms_scale(val[:, cols(0, cq)], ones_bd) for cq in range(val.shape[1] // LANES)]
                for name, val in raw.items() if name != "v"}

    def store_attn(raw, scales):
        for name, val in raw.items():
            if name == "v":
                vs_ref[nxt] = val
            elif name == "k":
                kn_ref[nxt] = (val * scales[name][0]) * _pair_gain(kg_ref)
            else:
                base = attn_cols[name][0] - a0
                for cq in range(val.shape[1] // LANES):
                    qn_ref[nxt, :, cols(base, cq)] = (val[:, cols(0, cq)] * scales[name][cq]) * _pair_gain(qg_ref)

    pairs = range(N_PAIRS)
    kvs = range(N_SWA_KV)

    def stage1(c):
        rows = slice(c * RET_CHUNK, (c + 1) * RET_CHUNK)
        q = [main_ref[cur, rows, cols(0, p)] for p in pairs]
        k = [main_ref[cur, rows, cols(RET_WIDTH, p)] * K_SCALE for p in pairs]
        v = [main_ref[cur, rows, cols(2 * RET_WIDTH, p)] for p in pairs]
        state = [state_ref[p] for p in pairs]
        kc = kn_ref[cur, rows, :]
        k_sw = pltpu.roll(kc, HEAD_DIM, axis=1)
        v_t = vs_ref[cur, rows, :].T
        is_first = (t == 0).astype(jnp.int32) if c == 0 else 0
        k_dup =[(jnp.where(first, kc, k_sw) if j == 0 else jnp.where(first, k_sw, kc)).astype(BF16)
                 for j in kvs]
        v_tj = [v_t[j * HEAD_DIM:(j + 1) * HEAD_DIM].astype(BF16) for j in kvs]
        q_st = []
        for j in kvs:
            pieces = []
            for g in range(SWA_GROUP):
                qc = qn_ref[cur, rows, cols(0, 2 * j + g // 2)]
                pieces.append(jnp.where(first, qc, 0.0) if g % 2 == 0 else jnp.where(first, 0.0, qc))
            q_st.append(jnp.concatenate(pieces, axis=0).astype(BF16))

        s = [_dot_nt(q[p].astype(BF16), _split_pair_rows(k[p], first)) for p in pairs]
        s_t = [_dot_nt(jnp.concatenate([prevk_ref[j], k_dup[j]], axis=0), q_st[j]) for j in kvs]
        cross = [_dot((q[p] * qdec_ref[p]).astype(BF16), state[p].astype(BF16)) for p in pairs]
        upd = [_dot((k[p] * kdec_ref[p]).T.astype(BF16), v[p].astype(BF16)) for p in pairs]
        return dict(rows=rows, v=v, state=state, k_dup=k_dup, v_tj=v_tj, is_first=is_first,
                    s=s, s_t=s_t, cross=cross, upd=upd)

    def stage2(st):
        st["o"] = [_dot((st["s"][p] * intra_ref[p]).astype(BF16), _split_pair_rows(st["v"][p], first))
                   + st["cross"][p] for p in pairs]
        st["o_t"] = []
        for j in kvs:
            sink_lanes = jnp.concatenate(
                [jnp.full((1, WINDOW), sinks_ref[SWA_GROUP * j + g], F32) for g in range(SWA_GROUP)],
                axis=1)
            v_cat = jnp.concatenate([prevv_ref[j], st["v_tj"][j]], axis=1)
            st["o_t"].append(_softmax_sink_pv_t(st["s_t"][j] * K_SCALE + bias_ref[st["is_first"], j],
                                                sink_lanes, v_cat))
        for p in pairs:
            state_ref[p] = st["state"][p] * sdec_ref[p] + jnp.where(bd_mask, st["upd"][p], 0.0)
        for j in kvs:
            prevk_ref[j] = st["k_dup"][j]
            prevv_ref[j] = st["v_tj"][j]

    def stage3(st, raw):
        rows = st["rows"]
        scale = [_head_rms_scale(st["o"][p], ones_bd) for p in pairs]
        raw_scales = attn_scales(raw)
        for p in pairs:
            g = main_ref[cur, rows, cols(3 * RET_WIDTH, p)]
            mix_ref[rows, cols(0, p)] = (st["o"][p] * scale[p] * _silu(g)).astype(BF16)
        store_attn(raw, raw_scales)
        for j in kvs:
            o_t = st["o_t"][j]
            for half in range(2):
                pair_t = jnp.concatenate([o_t[:, (2 * half) * WINDOW:(2 * half + 1) * WINDOW],
                                          o_t[:, (2 * half + 1) * WINDOW:(2 * half + 2) * WINDOW]], axis=0)
                mix_ref[rows, cols(RET_WIDTH, 2 * j + half)] = pair_t.T.astype(BF16)

    xn = x_next_ref[...]
    hb_ref[...] = ((xn * lax.rsqrt(jnp.mean(xn * xn, axis=-1, keepdims=True) + EPS)) * gain_mix_ref[...]).astype(BF16)
    for c in range(n_chunks):
        st = stage1(c)
        raw = project(plan[c]["a"])
        stage2(st)
        raw.update(project(plan[c]["b"]))
        stage3(st, raw)

    @pl.when(t == pl.num_programs(1) - 1)
    def _():
        for p in range(N_PAIRS):
            s = state_ref[p]
            ret_ref[2 * p] = s[:HEAD_DIM, :HEAD_DIM]
            ret_ref[2 * p + 1] = s[HEAD_DIM:, HEAD_DIM:]
        last = slice(PROMPT_TILE - WINDOW, PROMPT_TILE)
        kwin_ref[...] = kn_ref[cur, last, :].T
        vwin_ref[...] = vs_ref[cur, last, :].T

    @pl.when(step == 0)
    def _():
        for copy in late_weights:
            copy.wait()

    _out_mlp_kernel(mix_ref.at[:, pl.ds(0, RET_WIDTH)], mix_ref.at[:, pl.ds(RET_WIDTH, SWA_WIDTH)], x_ref,
                    w_out_ref, gain_ffn_ref, w_up_ref, w_down_ref, y_ref)


def _prompt_layer(sinks, x2d, gain_mix, w_in_bf, qg, kg, w_out_bf, gain_ffn, w_up_bf, w_down_bf, batch, seq):
    nt = seq // PROMPT_TILE
    last_tile = batch * nt - 1
    row = lambda w: pl.BlockSpec((PROMPT_TILE, w), lambda b, t: (b * nt + t, 0))
    next_row = pl.BlockSpec((PROMPT_TILE, D_MODEL), lambda b, t: (jnp.minimum(b * nt + t + 1, last_tile), 0))
    full = lambda a: pl.BlockSpec(a.shape, lambda b, t: (0, 0), pipeline_mode=pl.Buffered(1))
    in_hbm = pl.BlockSpec(memory_space=pl.ANY)
    return pl.pallas_call(
        _prompt_layer_kernel,
        grid=(batch, nt),
        in_specs=[pl.BlockSpec(memory_space=pltpu.SMEM), row(D_MODEL), next_row,
                  full(gain_mix), full(w_in_bf), full(qg), full(kg),
                  in_hbm, full(gain_ffn), in_hbm, in_hbm],
        out_specs=[row(D_MODEL),
                   pl.BlockSpec((None, N_RET_HEADS, HEAD_DIM, HEAD_DIM), lambda b, t: (b, 0, 0, 0)),
                   pl.BlockSpec((None, SWA_KV_WIDTH, WINDOW), lambda b, t: (b, 0, 0)),
                   pl.BlockSpec((None, SWA_KV_WIDTH, WINDOW), lambda b, t: (b, 0, 0))],
        out_shape=[jax.ShapeDtypeStruct((batch * seq, D_MODEL), F32),
                   jax.ShapeDtypeStruct((batch, N_RET_HEADS, HEAD_DIM, HEAD_DIM), F32),
                   jax.ShapeDtypeStruct((batch, SWA_KV_WIDTH, WINDOW), F32),
                   jax.ShapeDtypeStruct((batch, SWA_KV_WIDTH, WINDOW), F32)],
        scratch_shapes=[
            pltpu.VMEM((2, PROMPT_TILE, MAIN_WIDTH), F32),
            pltpu.VMEM((2, PROMPT_TILE, SWA_WIDTH), F32),
            pltpu.VMEM((2, PROMPT_TILE, SWA_KV_WIDTH), F32),
            pltpu.VMEM((2, PROMPT_TILE, SWA_KV_WIDTH), F32),
            pltpu.VMEM((PROMPT_TILE, MIX_WIDTH), BF16),
            pltpu.VMEM((PROMPT_TILE, D_MODEL), BF16),
            pltpu.VMEM((N_PAIRS, LANES, LANES), F32),
            pltpu.VMEM((N_SWA_KV, WINDOW, LANES), BF16),
            pltpu.VMEM((N_SWA_KV, HEAD_DIM, WINDOW), BF16),
            pltpu.VMEM((N_PAIRS, LANES, 2 * LANES), F32),
            pltpu.VMEM((N_PAIRS, LANES, LANES), F32),
            pltpu.VMEM((N_PAIRS, LANES, LANES), F32),
            pltpu.VMEM((N_PAIRS, LANES, LANES), F32),
            pltpu.VMEM((2, N_SWA_KV, 2 * WINDOW, SWA_GROUP * WINDOW), F32),
            pltpu.VMEM(w_out_bf.shape, BF16),
            pltpu.VMEM(w_up_bf.shape, BF16),
            pltpu.VMEM(w_down_bf.shape, BF16),
            pltpu.SemaphoreType.DMA((3,)),
        ],
        compiler_params=pltpu.CompilerParams(
            dimension_semantics=("arbitrary", "arbitrary"), vmem_limit_bytes=PROMPT_VMEM_LIMIT),
        name="prompt_layer",
    )(sinks, x2d, x2d, gain_mix, w_in_bf, qg, kg, w_out_bf, gain_ffn, w_up_bf, w_down_bf)


DEC_ROWS = 128
DEC_UNROLL = 8


def _decode_attn_consts(dec_seq, bias_ref):
    shift = dec_seq.bit_length() - 1
    rows = N_SWA_HEADS * dec_seq
    rb = _iota((rows, WINDOW), 0)
    cb = _iota((rows, WINDOW), 1)
    head = rb >> shift
    i = rb & (dec_seq - 1)
    slope = jnp.zeros((rows, WINDOW), F32)
    for h in range(N_SWA_HEADS):
        slope = jnp.where(head == h, ALIBI_SLOPES[h], slope)
    bias_ref[0] = jnp.where(cb > i, -(slope * (WINDOW + i - cb).astype(F32)), NEG_INF)
    m = cb & (dec_seq - 1)
    bias_ref[1] = jnp.where(m <= i, -(slope * (i - m).astype(F32)), NEG_INF)


GROUPS_PER_STEP = 2


def _decode_mixers_kernel(dec_seq, nb, qdec_ref, kdec_ref, sdec_ref, intra_ref, sinks_ref,
                          q_ref, k_ref, v_ref, g_ref, st_ref, qn_ref, kn_ref, vs_ref, kt_ref, vt_ref,
                          mix_ret_ref, st_out_ref, mix_swa_ref, kt_out_ref, vt_out_ref,
                          qt_s, kt_s, vt_s, qdt_s, kdt_s, o_s,
                          bias_ref, qbd_ref, oblk_ref, knew_ref, vnew_ref, knt_ref, vst_ref):
    pair = pl.program_id(0)
    halves = [slice(0, HEAD_DIM), slice(HEAD_DIM, 2 * HEAD_DIM)]
    first = _first_half()
    shift = dec_seq.bit_length() - 1
    e_blk = HEAD_DIM // 2

    @pl.when(pair == 0)
    def _():
        _decode_attn_consts(dec_seq, bias_ref)

    def ret_stage():
        for l in range(dec_seq):
            rows = pl.ds(l, nb, stride=dec_seq)
            q_t = q_ref[rows, :].T
            k_t = (k_ref[rows, :] * K_SCALE).T
            qt_s[l] = q_t
            kt_s[l] = k_t
            vt_s[l] = v_ref[rows, :].T
            for hh in range(2):
                qdt_s[l, halves[hh], :] = q_t[halves[hh]] * qdec_ref[2 * pair + hh, l]
                kdt_s[l, halves[hh], :] = k_t[halves[hh]] * kdec_ref[2 * pair + hh, l]

    def ret_intra(hh):
        h = 2 * pair + hh
        hs = halves[hh]
        for l in range(dec_seq):
            acc = None
            for m in range(l + 1):
                sc = jnp.sum(qt_s[l, hs, :] * kt_s[m, hs, :], axis=0, keepdims=True) * intra_ref[h, l - m]
                term = sc * vt_s[m, hs, :]
                acc = term if acc is None else acc + term
            o_s[l, hs, :] = acc

    def ret_block(hh, eb, d_lo, d_hi, accs=None):
        h = 2 * pair + hh
        es = slice(eb * e_blk, (eb + 1) * e_blk)
        erows = slice(hh * HEAD_DIM + eb * e_blk, hh * HEAD_DIM + (eb + 1) * e_blk)
        if accs is None:
            accs = [jnp.zeros((e_blk, nb), F32) for _ in range(dec_seq)]
        for d in range(d_lo, d_hi):
            s_d = st_ref[hh, d, es, :]
            row = slice(hh * HEAD_DIM + d, hh * HEAD_DIM + d + 1)
            upd = s_d * sdec_ref[h]
            for l in range(dec_seq):
                accs[l] = accs[l] + qdt_s[l, row, :] * s_d
                upd = upd + kdt_s[l, row, :] * vt_s[l, erows, :]
            st_out_ref[hh, d, es, :] = upd
        if d_hi == HEAD_DIM:
            for l in range(dec_seq):
                o_s[l, erows, :] = o_s[l, erows, :] + accs[l]
        return accs

    def ret_finish():
        for l in range(dec_seq):
            o = o_s[l]
            normed = []
            for hh in range(2):
                oh = o[halves[hh]]
                normed.append(oh * lax.rsqrt(jnp.mean(oh * oh, axis=0, keepdims=True) + EPS))
            rows = pl.ds(l, nb, stride=dec_seq)
            mix_ret_ref[rows, :] = jnp.concatenate(normed, axis=0).T * _silu(g_ref[rows, :])

    sink_rows = jnp.concatenate(
        [jnp.full((dec_seq, LANES), sinks_ref[h], F32) for h in range(N_SWA_HEADS)], axis=0)
    col_batch = _iota((N_SWA_HEADS * dec_seq, LANES), 1) >> shift
    keep_old = _iota((1, LANES), 1) < WINDOW - dec_seq

    def attn_stage(gi):
        grows = slice(gi * DEC_ROWS, (gi + 1) * DEC_ROWS)
        kn_t = kn_ref[grows, :].T
        vs_t = vs_ref[grows, :].T
        knt_ref[gi] = kn_t.astype(BF16)
        vst_ref[gi] = vs_t.astype(BF16)
        for bb in range(DEC_GROUP):
            sh = (WINDOW - dec_seq - bb * dec_seq) % LANES
            knew_ref[gi, bb] = pltpu.roll(kn_t, sh, axis=1) if sh else kn_t
            vnew_ref[gi, bb] = pltpu.roll(vs_t, sh, axis=1) if sh else vs_t
        qn = qn_ref[grows, :]
        qn_sw = pltpu.roll(qn, HEAD_DIM, axis=1)
        for h in range(N_SWA_HEADS):
            kv_half = h // SWA_GROUP
            if (h % 2) == kv_half:
                src = qn[:, (h // 2) * LANES:(h // 2 + 1) * LANES]
            else:
                col = (h + 1) // 2
                src = qn_sw[:, col * LANES:(col + 1) * LANES]
            qbd_ref[gi, h] = jnp.where(first, src, 0.0) if kv_half == 0 else jnp.where(first, 0.0, src)

    def attn_block(gi, i):
        bs = [i * DEC_UNROLL + u for u in range(DEC_UNROLL)]
        rows = [slice(b * dec_seq, (b + 1) * dec_seq) for b in bs]
        k_old = [kt_ref[gi * DEC_GROUP + b] for b in bs]
        v_old = [vt_ref[gi * DEC_GROUP + b] for b in bs]
        q_st = [jnp.concatenate([qbd_ref[gi, h, r, :] for h in range(N_SWA_HEADS)], axis=0).astype(BF16)
                for r in rows]
        s = [_dot(q_st[u], jnp.concatenate([k_old[u].astype(BF16), knt_ref[gi]], axis=1))
             for u in range(DEC_UNROLL)]
        o = []
        for u, b in enumerate(bs):
            bias = jnp.concatenate([bias_ref[0], jnp.where(col_batch == b, bias_ref[1], NEG_INF)], axis=1)
            w_v = jnp.concatenate([v_old[u].astype(BF16), vst_ref[gi]], axis=1)
            o.append(_softmax_sink_pv(s[u] * K_SCALE + bias, sink_rows, w_v))
        for u, b in enumerate(bs):
            for h in range(N_SWA_HEADS):
                oblk_ref[h, rows[u], :] = o[u][h * dec_seq:(h + 1) * dec_seq]
            kt_out_ref[gi * DEC_GROUP + b] = jnp.where(
                keep_old, pltpu.roll(k_old[u], LANES - dec_seq, axis=1), knew_ref[gi, b])
            vt_out_ref[gi * DEC_GROUP + b] = jnp.where(
                keep_old, pltpu.roll(v_old[u], LANES - dec_seq, axis=1), vnew_ref[gi, b])

    def attn_finish(gi):
        grows = slice(gi * DEC_ROWS, (gi + 1) * DEC_ROWS)
        y1 = jnp.where(first, oblk_ref[3], oblk_ref[4])
        moved = pltpu.roll(jnp.concatenate([oblk_ref[1], y1, oblk_ref[6], oblk_ref[6]], axis=1), HEAD_DIM, axis=1)
        outs = [
            jnp.where(first, oblk_ref[0], moved[:, 0:LANES]),
            jnp.where(first, oblk_ref[2], moved[:, LANES:2 * LANES]),
            jnp.where(first, moved[:, 2 * LANES:3 * LANES], oblk_ref[5]),
            jnp.where(first, moved[:, 3 * LANES:4 * LANES], oblk_ref[7]),
        ]
        for c in range(SWA_WIDTH // LANES):
            mix_swa_ref[grows, c * LANES:(c + 1) * LANES] = outs[c].astype(BF16)

    ret_stage()
    for gi in range(GROUPS_PER_STEP):
        attn_stage(gi)
    for gi in range(GROUPS_PER_STEP):
        ret_intra(gi)
        n_blocks = DEC_GROUP // DEC_UNROLL
        per_eb = n_blocks // 2
        d_step = HEAD_DIM // per_eb
        for eb in range(2):
            accs = None
            for part in range(per_eb):
                accs = ret_block(gi, eb, part * d_step, (part + 1) * d_step, accs)
                attn_block(gi, eb * per_eb + part)
        attn_finish(gi)
    ret_finish()


def _decode_mixers(sinks, main, qn, kn, vs, state_t, k_t, v_t, dec_seq):
    nb = state_t.shape[-1]
    m = main.shape[0]
    assert nb == LANES and m == nb * dec_seq and k_t.shape == (nb, SWA_KV_WIDTH, WINDOW)
    assert DEC_GROUP * dec_seq == DEC_ROWS and nb == N_PAIRS * GROUPS_PER_STEP * DEC_GROUP
    assert (DEC_GROUP // DEC_UNROLL) % 2 == 0 and dec_seq & (dec_seq - 1) == 0
    tab = lambda f: jnp.asarray([[f(h, j) for j in range(dec_seq)] for h in range(N_RET_HEADS)], F32)
    qdec = tab(lambda h, j: math.exp(LOG_DECAY[h] * (j + 1.0)))
    kdec = tab(lambda h, j: math.exp(LOG_DECAY[h] * (dec_seq - 1.0 - j)))
    intra = tab(lambda h, j: math.exp(LOG_DECAY[h] * j))
    sdec = jnp.asarray([math.exp(LOG_DECAY[h] * dec_seq) for h in range(N_RET_HEADS)], F32)
    smem = pl.BlockSpec(memory_space=pltpu.SMEM)
    col = lambda base: pl.BlockSpec((m, LANES), lambda p: (0, base + p))
    st_spec = pl.BlockSpec((2, HEAD_DIM, HEAD_DIM, nb), lambda p: (p, 0, 0, 0))
    step_rows = GROUPS_PER_STEP * DEC_ROWS
    row = lambda w: pl.BlockSpec((step_rows, w), lambda p: (p, 0))
    cache = pl.BlockSpec((GROUPS_PER_STEP * DEC_GROUP, SWA_KV_WIDTH, WINDOW), lambda p: (p, 0, 0))
    stage = pltpu.VMEM((dec_seq, LANES, nb), F32)
    return pl.pallas_call(
        functools.partial(_decode_mixers_kernel, dec_seq, nb),
        grid=(N_PAIRS,),
        in_specs=[smem, smem, smem, smem, smem,
                  col(0), col(N_PAIRS), col(2 * N_PAIRS), col(3 * N_PAIRS), st_spec,
                  row(SWA_WIDTH), row(SWA_KV_WIDTH), row(SWA_KV_WIDTH), cache, cache],
        out_specs=[pl.BlockSpec((m, LANES), lambda p: (0, p)), st_spec, row(SWA_WIDTH), cache, cache],
        out_shape=[jax.ShapeDtypeStruct((m, RET_WIDTH), F32),
                   jax.ShapeDtypeStruct(state_t.shape, F32),
                   jax.ShapeDtypeStruct((m, SWA_WIDTH), BF16),
                   jax.ShapeDtypeStruct(k_t.shape, F32),
                   jax.ShapeDtypeStruct(v_t.shape, F32)],
        scratch_shapes=[
            stage, stage, stage, stage, stage, stage,
            pltpu.VMEM((2, N_SWA_HEADS * dec_seq, WINDOW), F32),
            pltpu.VMEM((GROUPS_PER_STEP, N_SWA_HEADS, DEC_ROWS, LANES), F32),
            pltpu.VMEM((N_SWA_HEADS, DEC_ROWS, LANES), F32),
            pltpu.VMEM((GROUPS_PER_STEP, DEC_GROUP, SWA_KV_WIDTH, LANES), F32),
            pltpu.VMEM((GROUPS_PER_STEP, DEC_GROUP, SWA_KV_WIDTH, LANES), F32),
            pltpu.VMEM((GROUPS_PER_STEP, SWA_KV_WIDTH, DEC_ROWS), BF16),
            pltpu.VMEM((GROUPS_PER_STEP, SWA_KV_WIDTH, DEC_ROWS), BF16),
        ],
        compiler_params=pltpu.CompilerParams(
            dimension_semantics=("arbitrary",), vmem_limit_bytes=VMEM_LIMIT),
        name="decode_mixers",
    )(qdec, kdec, sdec, intra, sinks, main, main, main, main, state_t, qn, kn, vs, k_t, v_t)


def kernel(x_prompt, x_sample, state_ret, cache_swa_k, cache_swa_v, norm_mix_gain, w_in, q_norm_gain,
           k_norm_gain, attn_sinks, w_out, norm_ffn_gain, w_up, w_down):
    batch, seq, d = x_prompt.shape
    nb, dec_seq, _ = x_sample.shape
    wb = cache_swa_k.shape[1]
    assert d == D_MODEL and seq % PROMPT_TILE == 0 and wb == WINDOW

    gain_mix = norm_mix_gain.reshape(1, D_MODEL)
    gain_ffn = norm_ffn_gain.reshape(1, D_MODEL)
    qg = q_norm_gain.reshape(1, HEAD_DIM)
    kg = k_norm_gain.reshape(1, HEAD_DIM)

    def from_key_minor(a_t):
        return jnp.transpose(a_t.reshape(a_t.shape[0], N_SWA_KV, HEAD_DIM, WINDOW), (0, 3, 1, 2))

    def to_key_minor(a):
        return jnp.transpose(a, (0, 2, 3, 1)).reshape(a.shape[0], SWA_KV_WIDTH, WINDOW)

    xs = x_sample.reshape(nb * dec_seq, D_MODEL)
    main_s, qn_s, kn_s, vs_s, w_in_bf = _in_proj_cast(xs, gain_mix, w_in, qg, kg)
    mix_ret_s, state_t, mix_swa_s, k_t, v_t = _decode_mixers(
        attn_sinks, main_s, qn_s, kn_s, vs_s, jnp.transpose(state_ret, (1, 2, 3, 0)),
        to_key_minor(cache_swa_k), to_key_minor(cache_swa_v), dec_seq)
    y_s, w_out_bf, w_up_bf, w_down_bf = _out_mlp_cast(mix_ret_s, mix_swa_s, xs, gain_ffn, w_out, w_up, w_down)

    xp = x_prompt.reshape(batch * seq, D_MODEL)
    y_p, ret_p, kwin_t, vwin_t = _prompt_layer(attn_sinks, xp, gain_mix, w_in_bf, qg, kg, w_out_bf, gain_ffn,
                                               w_up_bf, w_down_bf, batch, seq)
    y_p = y_p.reshape(batch, seq, D_MODEL)

    return (y_p, y_s.reshape(nb, dec_seq, D_MODEL), ret_p, from_key_minor(kwin_t), from_key_minor(vwin_t),
            jnp.transpose(state_t, (3, 0, 1, 2)), from_key_minor(k_t), from_key_minor(v_t))
```

```python
import functools
import math

import jax
import jax.numpy as jnp
from jax import lax
from jax.experimental import pallas as pl
from jax.experimental.pallas import tpu as pltpu

F32 = jnp.float32
BF16 = jnp.bfloat16

D_MODEL = 1024
HEAD_DIM = 64
N_RET_HEADS = 8
N_SWA_HEADS = 8
N_SWA_KV = 2
SWA_GROUP = N_SWA_HEADS // N_SWA_KV
RET_WIDTH = N_RET_HEADS * HEAD_DIM
SWA_WIDTH = N_SWA_HEADS * HEAD_DIM
SWA_KV_WIDTH = N_SWA_KV * HEAD_DIM
MAIN_WIDTH = 4 * RET_WIDTH
IN_WIDTH = MAIN_WIDTH + SWA_WIDTH + 2 * SWA_KV_WIDTH
MIX_WIDTH = RET_WIDTH + SWA_WIDTH
D_FF = 4 * D_MODEL
WINDOW = 128
RET_CHUNK = 128
EPS = 1e-6
NEG_INF = -1e30

LANES = 128
N_PAIRS = N_RET_HEADS // 2
LOG_DECAY = [math.log(1.0 - 2.0 ** (-5.0 - h)) for h in range(N_RET_HEADS)]
ALIBI_SLOPES = [2.0 ** (-8.0 * (h + 1) / N_SWA_HEADS) for h in range(N_SWA_HEADS)]
K_SCALE = HEAD_DIM ** -0.5

MXU_WIDTH = 256
PROJ_BLOCK = 2 * MXU_WIDTH
Q_BLOCK = MXU_WIDTH
PROMPT_TILE = 512
DEC_GROUP = 16
VMEM_LIMIT = 56 * 1024 * 1024
PROMPT_VMEM_LIMIT = 62 * 1024 * 1024


def _dot(a, b):
    return jnp.dot(a, b, preferred_element_type=F32)


def _dot_nt(a, b):
    return lax.dot_general(a, b, (((1,), (1,)), ((), ())), preferred_element_type=F32)


def _iota(shape, dim):
    return lax.broadcasted_iota(jnp.int32, shape, dim)


def _ones_block_diag():
    same = ((_iota((2 * LANES, LANES), 0) >> 6) & 1) == (_iota((2 * LANES, LANES), 1) >> 6)
    return jnp.where(same, 1.0, 0.0).astype(BF16)


def _head_sumsq(x, ones_bd):
    x2 = x * x
    hi = x2.astype(BF16)
    lo = (x2 - hi.astype(F32)).astype(BF16)
    return _dot(jnp.concatenate([hi, lo], axis=1), ones_bd)


def _head_rms_scale(x, ones_bd):
    return lax.rsqrt(_head_sumsq(x, ones_bd) * (1.0 / HEAD_DIM) + EPS)


def _head_rms_scales(xs, ones_bd):
    if not xs:
        return []
    parts = []
    for x in xs:
        x2 = x * x
        hi = x2.astype(BF16)
        parts.append(jnp.concatenate([hi, (x2 - hi.astype(F32)).astype(BF16)], axis=1))
    total = _dot(jnp.concatenate(parts, axis=0) if len(parts) > 1 else parts[0], ones_bd)
    out, lo = [], 0
    for x in xs:
        out.append(lax.rsqrt(total[lo:lo + x.shape[0]] * (1.0 / HEAD_DIM) + EPS))
        lo += x.shape[0]
    return out


def _first_half():
    return _iota((1, LANES), 1) < HEAD_DIM


def _pair_const(values, pair, lane_is_second):
    return jnp.where(lane_is_second, values[2 * pair + 1], values[2 * pair]).astype(F32)


def _pair_gain(gain_ref):
    g = gain_ref[...]
    return jnp.concatenate([g, g], axis=1)


def _silu(g):
    return g * (1.0 / (1.0 + jnp.exp(-g)))


def _in_proj_kernel(x_ref, gain_ref, w_ref, qg_ref, kg_ref, main_ref, qn_ref, kn_ref, vs_ref, fetch=None):
    x = x_ref[...]
    ms = jnp.mean(x * x, axis=-1, keepdims=True)
    hb = ((x * lax.rsqrt(ms + EPS)) * gain_ref[...]).astype(BF16)
    ones_bd = _ones_block_diag()

    def project(lo, width):
        if fetch is not None:
            fetch(lo, width)
        return _dot(hb, w_ref[:, lo:lo + width])

    for c in range(MAIN_WIDTH // PROJ_BLOCK):
        main_ref[:, c * PROJ_BLOCK:(c + 1) * PROJ_BLOCK] = project(c * PROJ_BLOCK, PROJ_BLOCK)
    qs = project(MAIN_WIDTH, SWA_WIDTH)
    k0 = MAIN_WIDTH + SWA_WIDTH
    ks = project(k0, SWA_KV_WIDTH)
    vs_ref[...] = project(k0 + SWA_KV_WIDTH, SWA_KV_WIDTH)
    q_cols = [qs[:, c * LANES:(c + 1) * LANES] for c in range(SWA_WIDTH // LANES)]
    scales = _head_rms_scales(q_cols + [ks], ones_bd)
    for c, xc in enumerate(q_cols):
        qn_ref[:, c * LANES:(c + 1) * LANES] = (xc * scales[c]) * _pair_gain(qg_ref)
    kn_ref[...] = (ks * scales[-1]) * _pair_gain(kg_ref)


ROW_TILE = 256


W_IN_BLOCKS = [(c * PROJ_BLOCK, PROJ_BLOCK) for c in range(MAIN_WIDTH // PROJ_BLOCK)] + [(MAIN_WIDTH, SWA_WIDTH),
                                                                    (MAIN_WIDTH + SWA_WIDTH, 2 * SWA_KV_WIDTH)]
assert sum(w for _, w in W_IN_BLOCKS) == IN_WIDTH


def _in_proj_cast_kernel(x_ref, gain_ref, w_hbm, qg_ref, kg_ref,
                         main_ref, qn_ref, kn_ref, vs_ref, w_bf_hbm,
                         w_v, stage, in_sem, out_sem):
    step = pl.program_id(0)
    block_in = [pltpu.make_async_copy(w_hbm.at[:, pl.ds(lo, width)], stage.at[:, pl.ds(lo, width)], in_sem.at[i])
                for i, (lo, width) in enumerate(W_IN_BLOCKS)]
    writeback = pltpu.make_async_copy(w_v, w_bf_hbm, out_sem.at[0])

    @pl.when(step == 0)
    def _():
        for copy in block_in:
            copy.start()
        arrived = set()

        def fetch(lo, width):
            for i, (blo, bwidth) in enumerate(W_IN_BLOCKS):
                if blo <= lo < blo + bwidth and i not in arrived:
                    assert lo + width <= blo + bwidth
                    block_in[i].wait()
                    w_v[:, blo:blo + bwidth] = stage[:, blo:blo + bwidth].astype(BF16)
                    arrived.add(i)

        _in_proj_kernel(x_ref, gain_ref, w_v, qg_ref, kg_ref, main_ref, qn_ref, kn_ref, vs_ref, fetch=fetch)
        assert len(arrived) == len(W_IN_BLOCKS)
        writeback.start()

    @pl.when(step > 0)
    def _():
        _in_proj_kernel(x_ref, gain_ref, w_v, qg_ref, kg_ref, main_ref, qn_ref, kn_ref, vs_ref)

    @pl.when(step == pl.num_programs(0) - 1)
    def _():
        writeback.wait()


def _in_proj_cast(x2d, gain, w_in, qg, kg):
    m = x2d.shape[0]
    tm = min(ROW_TILE, m)
    row = lambda w: pl.BlockSpec((tm, w), lambda i: (i, 0))
    full = lambda a: pl.BlockSpec(a.shape, lambda i: (0, 0), pipeline_mode=pl.Buffered(1))
    hbm = pl.BlockSpec(memory_space=pl.ANY)
    return pl.pallas_call(
        _in_proj_cast_kernel,
        grid=(m // tm,),
        in_specs=[row(D_MODEL), full(gain), hbm, full(qg), full(kg)],
        out_specs=[row(MAIN_WIDTH), row(SWA_WIDTH), row(SWA_KV_WIDTH), row(SWA_KV_WIDTH), hbm],
        out_shape=[jax.ShapeDtypeStruct((m, MAIN_WIDTH), F32),
                   jax.ShapeDtypeStruct((m, SWA_WIDTH), F32),
                   jax.ShapeDtypeStruct((m, SWA_KV_WIDTH), F32),
                   jax.ShapeDtypeStruct((m, SWA_KV_WIDTH), F32),
                   jax.ShapeDtypeStruct(w_in.shape, BF16)],
        scratch_shapes=[pltpu.VMEM(w_in.shape, BF16), pltpu.VMEM(w_in.shape, F32),
                        pltpu.SemaphoreType.DMA((len(W_IN_BLOCKS),)), pltpu.SemaphoreType.DMA((1,))],
        compiler_params=pltpu.CompilerParams(
            dimension_semantics=("arbitrary",), vmem_limit_bytes=VMEM_LIMIT),
        name="in_proj_cast",
    )(x2d, gain, w_in, qg, kg)


FF_CHUNK = 1024


def _out_mlp_kernel(mix_ret_ref, mix_swa_ref, x_ref, w_out_ref, gain_ref, w_up_ref, w_down_ref, y_ref):
    h = x_ref[...] + (_dot(mix_ret_ref[...].astype(BF16), w_out_ref[:RET_WIDTH, :])
                      + _dot(mix_swa_ref[...].astype(BF16), w_out_ref[RET_WIDTH:, :]))
    ms = jnp.mean(h * h, axis=-1, keepdims=True)
    hf = ((h * lax.rsqrt(ms + EPS)) * gain_ref[...]).astype(BF16)
    ff = None
    for c in range(D_FF // FF_CHUNK):
        u = _dot(hf, w_up_ref[:, c * FF_CHUNK:(c + 1) * FF_CHUNK])
        a = jnp.maximum(u, 0.0)
        d = _dot((a * a).astype(BF16), w_down_ref[c * FF_CHUNK:(c + 1) * FF_CHUNK, :])
        ff = d if ff is None else ff + d
    y_ref[...] = h + ff


CAST_CHUNK = 512
N_FF_CHUNKS = D_FF // CAST_CHUNK


def _out_mlp_cast_kernel(mix_ret_ref, mix_swa_ref, x_ref, gain_ref, w_out_hbm, w_up_hbm, w_down_hbm,
                         y_ref, w_out_bf_hbm, w_up_bf_hbm, w_down_bf_hbm,
                         w_out_v, w_up_v, w_down_v, stage_out, stage_up, stage_down, in_sem, out_sem):
    step = pl.program_id(0)

    def out_in(i):
        return pltpu.make_async_copy(w_out_hbm.at[pl.ds(i * CAST_CHUNK, CAST_CHUNK), :], stage_out.at[i],
                                     in_sem.at[0, i])

    def up_in(c):
        return pltpu.make_async_copy(w_up_hbm.at[:, pl.ds(c * CAST_CHUNK, CAST_CHUNK)], stage_up.at[c % 2],
                                     in_sem.at[1, c % 2])

    def down_in(c):
        return pltpu.make_async_copy(w_down_hbm.at[pl.ds(c * CAST_CHUNK, CAST_CHUNK), :], stage_down.at[c % 2],
                                     in_sem.at[2, c % 2])

    writebacks = [pltpu.make_async_copy(src, dst, out_sem.at[i]) for i, (src, dst) in enumerate(
        [(w_out_v, w_out_bf_hbm), (w_up_v, w_up_bf_hbm), (w_down_v, w_down_bf_hbm)])]

    @pl.when(step == 0)
    def _():
        n_out = D_MODEL // CAST_CHUNK
        for i in range(n_out):
            out_in(i).start()
        up_in(0).start()
        down_in(0).start()
        for i in range(n_out):
            out_in(i).wait()
            w_out_v[i * CAST_CHUNK:(i + 1) * CAST_CHUNK, :] = stage_out[i].astype(BF16)
        writebacks[0].start()
        h = x_ref[...] + (_dot(mix_ret_ref[...].astype(BF16), w_out_v[:RET_WIDTH, :])
                          + _dot(mix_swa_ref[...].astype(BF16), w_out_v[RET_WIDTH:, :]))
        ms = jnp.mean(h * h, axis=-1, keepdims=True)
        hf = ((h * lax.rsqrt(ms + EPS)) * gain_ref[...]).astype(BF16)
        ff = None
        for c in range(N_FF_CHUNKS):
            chunk = slice(c * CAST_CHUNK, (c + 1) * CAST_CHUNK)
            if c + 1 < N_FF_CHUNKS:
                up_in(c + 1).start()
                down_in(c + 1).start()
            up_in(c).wait()
            down_in(c).wait()
            w_up_v[:, chunk] = stage_up[c % 2].astype(BF16)
            w_down_v[chunk, :] = stage_down[c % 2].astype(BF16)
            a = jnp.maximum(_dot(hf, w_up_v[:, chunk]), 0.0)
            d = _dot((a * a).astype(BF16), w_down_v[chunk, :])
            ff = d if ff is None else ff + d
        writebacks[1].start()
        writebacks[2].start()
        y_ref[...] = h + ff

    @pl.when(step > 0)
    def _():
        _out_mlp_kernel(mix_ret_ref, mix_swa_ref, x_ref, w_out_v, gain_ref, w_up_v, w_down_v, y_ref)

    @pl.when(step == pl.num_programs(0) - 1)
    def _():
        for wb in writebacks:
            wb.wait()


def _out_mlp_cast(mix_ret, mix_swa, x2d, gain, w_out, w_up, w_down):
    m = x2d.shape[0]
    tm = min(ROW_TILE, m)
    assert D_MODEL % CAST_CHUNK == 0 and D_FF % CAST_CHUNK == 0
    row = lambda w: pl.BlockSpec((tm, w), lambda i: (i, 0))
    full = lambda a: pl.BlockSpec(a.shape, lambda i: (0, 0), pipeline_mode=pl.Buffered(1))
    hbm = pl.BlockSpec(memory_space=pl.ANY)
    return pl.pallas_call(
        _out_mlp_cast_kernel,
        grid=(m // tm,),
        in_specs=[row(RET_WIDTH), row(SWA_WIDTH), row(D_MODEL), full(gain), hbm, hbm, hbm],
        out_specs=[row(D_MODEL), hbm, hbm, hbm],
        out_shape=[jax.ShapeDtypeStruct((m, D_MODEL), F32),
                   jax.ShapeDtypeStruct(w_out.shape, BF16),
                   jax.ShapeDtypeStruct(w_up.shape, BF16),
                   jax.ShapeDtypeStruct(w_down.shape, BF16)],
        scratch_shapes=[
            pltpu.VMEM(w_out.shape, BF16), pltpu.VMEM(w_up.shape, BF16), pltpu.VMEM(w_down.shape, BF16),
            pltpu.VMEM((D_MODEL // CAST_CHUNK, CAST_CHUNK, D_MODEL), F32),
            pltpu.VMEM((2, D_MODEL, CAST_CHUNK), F32),
            pltpu.VMEM((2, CAST_CHUNK, D_MODEL), F32),
            pltpu.SemaphoreType.DMA((3, 2)),
            pltpu.SemaphoreType.DMA((3,)),
        ],
        compiler_params=pltpu.CompilerParams(
            dimension_semantics=("arbitrary",), vmem_limit_bytes=VMEM_LIMIT),
        name="out_mlp_cast",
    )(mix_ret, mix_swa, x2d, gain, w_out, w_up, w_down)


def _split_pair_rows(x, first):
    return jnp.concatenate([jnp.where(first, x, 0.0), jnp.where(first, 0.0, x)], axis=0).astype(BF16)


def _softmax_sink_pv(s, sink_wide, v_t):
    m = jnp.maximum(jnp.max(s, axis=-1, keepdims=True), sink_wide)
    p = jnp.exp(s - jnp.concatenate([m, m], axis=1))
    denom = jnp.sum(p, axis=-1, keepdims=True) + jnp.exp(sink_wide - m)
    return _dot_nt(p.astype(BF16), v_t) / denom


def _softmax_sink_pv_t(s_t, sink_lanes, v_t):
    m = jnp.maximum(jnp.max(s_t, axis=0, keepdims=True), sink_lanes)
    p = jnp.exp(s_t - m)
    denom = jnp.sum(p, axis=0, keepdims=True) + jnp.exp(sink_lanes - m)
    return _dot(v_t, p.astype(BF16)) / denom


def _prompt_consts(intra_ref, qdec_ref, kdec_ref, sdec_ref, bias_ref):
    r = _iota((LANES, LANES), 0)
    lane2 = _iota((LANES, LANES), 1) >= HEAD_DIM
    rf = r.astype(F32)
    ri = _iota((LANES, 2 * LANES), 0)
    ci = _iota((LANES, 2 * LANES), 1)
    diff = (ri - (ci & (LANES - 1))).astype(F32)
    for p in range(N_PAIRS):
        lg = _pair_const(LOG_DECAY, p, lane2)
        qdec_ref[p] = jnp.exp(lg * (rf + 1.0))
        kdec_ref[p] = jnp.exp(lg * (RET_CHUNK - 1.0 - rf))
        sdec_ref[p] = jnp.exp(_pair_const(LOG_DECAY, p, r >= HEAD_DIM) * float(RET_CHUNK))
        lg2 = _pair_const(LOG_DECAY, p, ci >= LANES)
        intra_ref[p] = jnp.where(diff >= 0.0, jnp.exp(lg2 * jnp.maximum(diff, 0.0)), 0.0)
    cols = SWA_GROUP * WINDOW
    kb = _iota((2 * WINDOW, cols), 0)
    cb = _iota((2 * WINDOW, cols), 1)
    grp = cb >> 7
    dist = WINDOW + (cb & (WINDOW - 1)) - kb
    valid = (dist >= 0) & (dist < WINDOW)
    distf = dist.astype(F32)
    for j in range(N_SWA_KV):
        sl = [ALIBI_SLOPES[SWA_GROUP * j + g] for g in range(SWA_GROUP)]
        slope = jnp.where(grp == 0, sl[0], jnp.where(grp == 1, sl[1], jnp.where(grp == 2, sl[2], sl[3])))
        b = jnp.where(valid, -(slope.astype(F32) * distf), NEG_INF)
        bias_ref[0, j] = b
        bias_ref[1, j] = jnp.where(kb >= WINDOW, b, NEG_INF)


def _prompt_layer_kernel(sinks_ref, x_ref, x_next_ref, gain_mix_ref, w_in_ref, qg_ref, kg_ref,
                         w_out_hbm, gain_ffn_ref, w_up_hbm, w_down_hbm,
                         y_ref, ret_ref, kwin_ref, vwin_ref,
                         main_ref, qn_ref, kn_ref, vs_ref, mix_ref, hb_ref,
                         state_ref, prevk_ref, prevv_ref,
                         intra_ref, qdec_ref, kdec_ref, sdec_ref, bias_ref,
                         w_out_ref, w_up_ref, w_down_ref, w_sem):
    t = pl.program_id(1)
    step = pl.program_id(0) * pl.num_programs(1) + t
    cur = step % 2
    nxt = 1 - cur

    late_weights = [pltpu.make_async_copy(src, dst, w_sem.at[i]) for i, (src, dst) in enumerate(
        [(w_out_hbm, w_out_ref), (w_up_hbm, w_up_ref), (w_down_hbm, w_down_ref)])]

    @pl.when(step == 0)
    def _():
        for copy in late_weights:
            copy.start()
        _prompt_consts(intra_ref, qdec_ref, kdec_ref, sdec_ref, bias_ref)
        _in_proj_kernel(x_ref, gain_mix_ref, w_in_ref, qg_ref, kg_ref,
                        main_ref.at[0], qn_ref.at[0], kn_ref.at[0], vs_ref.at[0])

    @pl.when(t == 0)
    def _():
        state_ref[...] = jnp.zeros_like(state_ref)
        prevk_ref[...] = jnp.zeros_like(prevk_ref)
        prevv_ref[...] = jnp.zeros_like(prevv_ref)

    first = _first_half()
    ones_bd = _ones_block_diag()
    bd_mask = (_iota((LANES, LANES), 0) >= HEAD_DIM) == (_iota((LANES, LANES), 1) >= HEAD_DIM)

    n_chunks = PROMPT_TILE // RET_CHUNK
    assert n_chunks == 4
    plan = [dict(a=[], b=[]),
            dict(a=["m0", "m1"], b=[]),
            dict(a=["m2", "m3"], b=[]),
            dict(a=["v", "k"], b=["q0", "q1"])]
    a0 = MAIN_WIDTH
    attn_cols = {"v": (a0 + SWA_WIDTH + SWA_KV_WIDTH, SWA_KV_WIDTH), "k": (a0 + SWA_WIDTH, SWA_KV_WIDTH)}
    attn_cols.update({"q%d" % i: (a0 + Q_BLOCK * i, Q_BLOCK) for i in range(SWA_WIDTH // Q_BLOCK)})
    cols = lambda base, p: slice(base + p * LANES, base + (p + 1) * LANES)

    def project(items):
        raw = {}
        for it in items:
            if it[0] == "m":
                cb = int(it[1:])
                blk = slice(cb * PROJ_BLOCK, (cb + 1) * PROJ_BLOCK)
                main_ref[nxt, :, blk] = _dot(hb_ref[...], w_in_ref[:, blk])
            else:
                lo, width = attn_cols[it]
                raw[it] = _dot(hb_ref[...], w_in_ref[:, lo:lo + width])
        return raw

    def attn_scales(raw):
        return {name: [_head_rms_scale(val[:, cols(0, cq)], ones_bd) for cq in range(val.shape[1] // LANES)]
                for name, val in raw.items() if name != "v"}

    def store_attn(raw, scales):
        for name, val in raw.items():
            if name == "v":
                vs_ref[nxt] = val
            elif name == "k":
                kn_ref[nxt] = (val * scales[name][0]) * _pair_gain(kg_ref)
            else:
                base = attn_cols[name][0] - a0
                for cq in range(val.shape[1] // LANES):
                    qn_ref[nxt, :, cols(base, cq)] = (val[:, cols(0, cq)] * scales[name][cq]) * _pair_gain(qg_ref)

    pairs = range(N_PAIRS)
    kvs = range(N_SWA_KV)

    def stage1(c):
        rows = slice(c * RET_CHUNK, (c + 1) * RET_CHUNK)
        q = [main_ref[cur, rows, cols(0, p)] for p in pairs]
        k = [main_ref[cur, rows, cols(RET_WIDTH, p)] * K_SCALE for p in pairs]
        v = [main_ref[cur, rows, cols(2 * RET_WIDTH, p)] for p in pairs]
        state = [state_ref[p] for p in pairs]
        kc = kn_ref[cur, rows, :]
        k_sw = pltpu.roll(kc, HEAD_DIM, axis=1)
        v_t = vs_ref[cur, rows, :].T
        is_first = (t == 0).astype(jnp.int32) if c == 0 else 0
        k_dup =[(jnp.where(first, kc, k_sw) if j == 0 else jnp.where(first, k_sw, kc)).astype(BF16)
                 for j in kvs]
        v_tj = [v_t[j * HEAD_DIM:(j + 1) * HEAD_DIM].astype(BF16) for j in kvs]
        q_st = []
        for j in kvs:
            pieces = []
            for g in range(SWA_GROUP):
                qc = qn_ref[cur, rows, cols(0, 2 * j + g // 2)]
                pieces.append(jnp.where(first, qc, 0.0) if g % 2 == 0 else jnp.where(first, 0.0, qc))
            q_st.append(jnp.concatenate(pieces, axis=0).astype(BF16))

        s = [_dot_nt(q[p].astype(BF16), _split_pair_rows(k[p], first)) for p in pairs]
        s_t = [_dot_nt(jnp.concatenate([prevk_ref[j], k_dup[j]], axis=0), q_st[j]) for j in kvs]
        cross = [_dot((q[p] * qdec_ref[p]).astype(BF16), state[p].astype(BF16)) for p in pairs]
        upd = [_dot((k[p] * kdec_ref[p]).T.astype(BF16), v[p].astype(BF16)) for p in pairs]
        return dict(rows=rows, v=v, state=state, k_dup=k_dup, v_tj=v_tj, is_first=is_first,
                    s=s, s_t=s_t, cross=cross, upd=upd)

    def stage2(st):
        st["o"] = [_dot((st["s"][p] * intra_ref[p]).astype(BF16), _split_pair_rows(st["v"][p], first))
                   + st["cross"][p] for p in pairs]
        st["o_t"] = []
        for j in kvs:
            sink_lanes = jnp.concatenate(
                [jnp.full((1, WINDOW), sinks_ref[SWA_GROUP * j + g], F32) for g in range(SWA_GROUP)],
                axis=1)
            v_cat = jnp.concatenate([prevv_ref[j], st["v_tj"][j]], axis=1)
            st["o_t"].append(_softmax_sink_pv_t(st["s_t"][j] * K_SCALE + bias_ref[st["is_first"], j],
                                                sink_lanes, v_cat))
        for p in pairs:
            state_ref[p] = st["state"][p] * sdec_ref[p] + jnp.where(bd_mask, st["upd"][p], 0.0)
        for j in kvs:
            prevk_ref[j] = st["k_dup"][j]
            prevv_ref[j] = st["v_tj"][j]

    def stage3(st, raw):
        rows = st["rows"]
        scale = [_head_rms_scale(st["o"][p], ones_bd) for p in pairs]
        raw_scales = attn_scales(raw)
        for p in pairs:
            g = main_ref[cur, rows, cols(3 * RET_WIDTH, p)]
            mix_ref[rows, cols(0, p)] = (st["o"][p] * scale[p] * _silu(g)).astype(BF16)
        store_attn(raw, raw_scales)
        for j in kvs:
            o_t = st["o_t"][j]
            for half in range(2):
                pair_t = jnp.concatenate([o_t[:, (2 * half) * WINDOW:(2 * half + 1) * WINDOW],
                                          o_t[:, (2 * half + 1) * WINDOW:(2 * half + 2) * WINDOW]], axis=0)
                mix_ref[rows, cols(RET_WIDTH, 2 * j + half)] = pair_t.T.astype(BF16)

    xn = x_next_ref[...]
    hb_ref[...] = ((xn * lax.rsqrt(jnp.mean(xn * xn, axis=-1, keepdims=True) + EPS)) * gain_mix_ref[...]).astype(BF16)
    for c in range(n_chunks):
        st = stage1(c)
        raw = project(plan[c]["a"])
        stage2(st)
        raw.update(project(plan[c]["b"]))
        stage3(st, raw)

    @pl.when(t == pl.num_programs(1) - 1)
    def _():
        for p in range(N_PAIRS):
            s = state_ref[p]
            ret_ref[2 * p] = s[:HEAD_DIM, :HEAD_DIM]
            ret_ref[2 * p + 1] = s[HEAD_DIM:, HEAD_DIM:]
        last = slice(PROMPT_TILE - WINDOW, PROMPT_TILE)
        kwin_ref[...] = kn_ref[cur, last, :].T
        vwin_ref[...] = vs_ref[cur, last, :].T

    @pl.when(step == 0)
    def _():
        for copy in late_weights:
            copy.wait()

    _out_mlp_kernel(mix_ref.at[:, pl.ds(0, RET_WIDTH)], mix_ref.at[:, pl.ds(RET_WIDTH, SWA_WIDTH)], x_ref,
                    w_out_ref, gain_ffn_ref, w_up_ref, w_down_ref, y_ref)


def _prompt_layer(sinks, x2d, gain_mix, w_in_bf, qg, kg, w_out_bf, gain_ffn, w_up_bf, w_down_bf, batch, seq):
    nt = seq // PROMPT_TILE
    last_tile = batch * nt - 1
    row = lambda w: pl.BlockSpec((PROMPT_TILE, w), lambda b, t: (b * nt + t, 0))
    next_row = pl.BlockSpec((PROMPT_TILE, D_MODEL), lambda b, t: (jnp.minimum(b * nt + t + 1, last_tile), 0))
    full = lambda a: pl.BlockSpec(a.shape, lambda b, t: (0, 0), pipeline_mode=pl.Buffered(1))
    in_hbm = pl.BlockSpec(memory_space=pl.ANY)
    return pl.pallas_call(
        _prompt_layer_kernel,
        grid=(batch, nt),
        in_specs=[pl.BlockSpec(memory_space=pltpu.SMEM), row(D_MODEL), next_row,
                  full(gain_mix), full(w_in_bf), full(qg), full(kg),
                  in_hbm, full(gain_ffn), in_hbm, in_hbm],
        out_specs=[row(D_MODEL),
                   pl.BlockSpec((None, N_RET_HEADS, HEAD_DIM, HEAD_DIM), lambda b, t: (b, 0, 0, 0)),
                   pl.BlockSpec((None, SWA_KV_WIDTH, WINDOW), lambda b, t: (b, 0, 0)),
                   pl.BlockSpec((None, SWA_KV_WIDTH, WINDOW), lambda b, t: (b, 0, 0))],
        out_shape=[jax.ShapeDtypeStruct((batch * seq, D_MODEL), F32),
                   jax.ShapeDtypeStruct((batch, N_RET_HEADS, HEAD_DIM, HEAD_DIM), F32),
                   jax.ShapeDtypeStruct((batch, SWA_KV_WIDTH, WINDOW), F32),
                   jax.ShapeDtypeStruct((batch, SWA_KV_WIDTH, WINDOW), F32)],
        scratch_shapes=[
            pltpu.VMEM((2, PROMPT_TILE, MAIN_WIDTH), F32),
            pltpu.VMEM((2, PROMPT_TILE, SWA_WIDTH), F32),
            pltpu.VMEM((2, PROMPT_TILE, SWA_KV_WIDTH), F32),
            pltpu.VMEM((2, PROMPT_TILE, SWA_KV_WIDTH), F32),
            pltpu.VMEM((PROMPT_TILE, MIX_WIDTH), BF16),
            pltpu.VMEM((PROMPT_TILE, D_MODEL), BF16),
            pltpu.VMEM((N_PAIRS, LANES, LANES), F32),
            pltpu.VMEM((N_SWA_KV, WINDOW, LANES), BF16),
            pltpu.VMEM((N_SWA_KV, HEAD_DIM, WINDOW), BF16),
            pltpu.VMEM((N_PAIRS, LANES, 2 * LANES), F32),
            pltpu.VMEM((N_PAIRS, LANES, LANES), F32),
            pltpu.VMEM((N_PAIRS, LANES, LANES), F32),
            pltpu.VMEM((N_PAIRS, LANES, LANES), F32),
            pltpu.VMEM((2, N_SWA_KV, 2 * WINDOW, SWA_GROUP * WINDOW), F32),
            pltpu.VMEM(w_out_bf.shape, BF16),
            pltpu.VMEM(w_up_bf.shape, BF16),
            pltpu.VMEM(w_down_bf.shape, BF16),
            pltpu.SemaphoreType.DMA((3,)),
        ],
        compiler_params=pltpu.CompilerParams(
            dimension_semantics=("arbitrary", "arbitrary"), vmem_limit_bytes=PROMPT_VMEM_LIMIT),
        name="prompt_layer",
    )(sinks, x2d, x2d, gain_mix, w_in_bf, qg, kg, w_out_bf, gain_ffn, w_up_bf, w_down_bf)


DEC_ROWS = 128
DEC_UNROLL = 8


def _decode_attn_consts(dec_seq, bias_ref):
    shift = dec_seq.bit_length() - 1
    rows = N_SWA_HEADS * dec_seq
    rb = _iota((rows, WINDOW), 0)
    cb = _iota((rows, WINDOW), 1)
    head = rb >> shift
    i = rb & (dec_seq - 1)
    slope = jnp.zeros((rows, WINDOW), F32)
    for h in range(N_SWA_HEADS):
        slope = jnp.where(head == h, ALIBI_SLOPES[h], slope)
    bias_ref[0] = jnp.where(cb > i, -(slope * (WINDOW + i - cb).astype(F32)), NEG_INF)
    m = cb & (dec_seq - 1)
    bias_ref[1] = jnp.where(m <= i, -(slope * (i - m).astype(F32)), NEG_INF)


GROUPS_PER_STEP = 2


def _decode_mixers_kernel(dec_seq, nb, qdec_ref, kdec_ref, sdec_ref, intra_ref, sinks_ref,
                          q_ref, k_ref, v_ref, g_ref, st_ref, qn_ref, kn_ref, vs_ref, kt_ref, vt_ref,
                          mix_ret_ref, st_out_ref, mix_swa_ref, kt_out_ref, vt_out_ref,
                          qt_s, kt_s, vt_s, qdt_s, kdt_s, o_s,
                          bias_ref, qbd_ref, oblk_ref, knew_ref, vnew_ref, knt_ref, vst_ref):
    pair = pl.program_id(0)
    halves = [slice(0, HEAD_DIM), slice(HEAD_DIM, 2 * HEAD_DIM)]
    first = _first_half()
    shift = dec_seq.bit_length() - 1
    e_blk = HEAD_DIM // 2

    @pl.when(pair == 0)
    def _():
        _decode_attn_consts(dec_seq, bias_ref)

    def ret_stage():
        for l in range(dec_seq):
            rows = pl.ds(l, nb, stride=dec_seq)
            q_t = q_ref[rows, :].T
            k_t = (k_ref[rows, :] * K_SCALE).T
            qt_s[l] = q_t
            kt_s[l] = k_t
            vt_s[l] = v_ref[rows, :].T
            for hh in range(2):
                qdt_s[l, halves[hh], :] = q_t[halves[hh]] * qdec_ref[2 * pair + hh, l]
                kdt_s[l, halves[hh], :] = k_t[halves[hh]] * kdec_ref[2 * pair + hh, l]

    def ret_intra(hh):
        h = 2 * pair + hh
        hs = halves[hh]
        for l in range(dec_seq):
            acc = None
            for m in range(l + 1):
                sc = jnp.sum(qt_s[l, hs, :] * kt_s[m, hs, :], axis=0, keepdims=True) * intra_ref[h, l - m]
                term = sc * vt_s[m, hs, :]
                acc = term if acc is None else acc + term
            o_s[l, hs, :] = acc

    def ret_block(hh, eb, d_lo, d_hi, accs=None):
        h = 2 * pair + hh
        es = slice(eb * e_blk, (eb + 1) * e_blk)
        erows = slice(hh * HEAD_DIM + eb * e_blk, hh * HEAD_DIM + (eb + 1) * e_blk)
        if accs is None:
            accs = [jnp.zeros((e_blk, nb), F32) for _ in range(dec_seq)]
        for d in range(d_lo, d_hi):
            s_d = st_ref[hh, d, es, :]
            row = slice(hh * HEAD_DIM + d, hh * HEAD_DIM + d + 1)
            upd = s_d * sdec_ref[h]
            for l in range(dec_seq):
                accs[l] = accs[l] + qdt_s[l, row, :] * s_d
                upd = upd + kdt_s[l, row, :] * vt_s[l, erows, :]
            st_out_ref[hh, d, es, :] = upd
        if d_hi == HEAD_DIM:
            for l in range(dec_seq):
                o_s[l, erows, :] = o_s[l, erows, :] + accs[l]
        return accs

    def ret_finish():
        for l in range(dec_seq):
            o = o_s[l]
            normed = []
            for hh in range(2):
                oh = o[halves[hh]]
                normed.append(oh * lax.rsqrt(jnp.mean(oh * oh, axis=0, keepdims=True) + EPS))
            rows = pl.ds(l, nb, stride=dec_seq)
            mix_ret_ref[rows, :] = jnp.concatenate(normed, axis=0).T * _silu(g_ref[rows, :])

    sink_rows = jnp.concatenate(
        [jnp.full((dec_seq, LANES), sinks_ref[h], F32) for h in range(N_SWA_HEADS)], axis=0)
    col_batch = _iota((N_SWA_HEADS * dec_seq, LANES), 1) >> shift
    keep_old = _iota((1, LANES), 1) < WINDOW - dec_seq

    def attn_stage(gi):
        grows = slice(gi * DEC_ROWS, (gi + 1) * DEC_ROWS)
        kn_t = kn_ref[grows, :].T
        vs_t = vs_ref[grows, :].T
        knt_ref[...] = kn_t.astype(BF16)
        vst_ref[...] = vs_t.astype(BF16)
        for bb in range(DEC_GROUP):
            sh = (WINDOW - dec_seq - bb * dec_seq) % LANES
            knew_ref[bb] = pltpu.roll(kn_t, sh, axis=1) if sh else kn_t
            vnew_ref[bb] = pltpu.roll(vs_t, sh, axis=1) if sh else vs_t
        qn = qn_ref[grows, :]
        qn_sw = pltpu.roll(qn, HEAD_DIM, axis=1)
        for h in range(N_SWA_HEADS):
            kv_half = h // SWA_GROUP
            if (h % 2) == kv_half:
                src = qn[:, (h // 2) * LANES:(h // 2 + 1) * LANES]
            else:
                col = (h + 1) // 2
                src = qn_sw[:, col * LANES:(col + 1) * LANES]
            qbd_ref[h] = jnp.where(first, src, 0.0) if kv_half == 0 else jnp.where(first, 0.0, src)

    def attn_block(gi, i):
        bs = [i * DEC_UNROLL + u for u in range(DEC_UNROLL)]
        rows = [slice(b * dec_seq, (b + 1) * dec_seq) for b in bs]
        k_old = [kt_ref[gi * DEC_GROUP + b] for b in bs]
        v_old = [vt_ref[gi * DEC_GROUP + b] for b in bs]
        q_st = [jnp.concatenate([qbd_ref[h, r, :] for h in range(N_SWA_HEADS)], axis=0).astype(BF16)
                for r in rows]
        s = [_dot(q_st[u], jnp.concatenate([k_old[u].astype(BF16), knt_ref[...]], axis=1))
             for u in range(DEC_UNROLL)]
        o = []
        for u, b in enumerate(bs):
            bias = jnp.concatenate([bias_ref[0], jnp.where(col_batch == b, bias_ref[1], NEG_INF)], axis=1)
            w_v = jnp.concatenate([v_old[u].astype(BF16), vst_ref[...]], axis=1)
            o.append(_softmax_sink_pv(s[u] * K_SCALE + bias, sink_rows, w_v))
        for u, b in enumerate(bs):
            for h in range(N_SWA_HEADS):
                oblk_ref[h, rows[u], :] = o[u][h * dec_seq:(h + 1) * dec_seq]
            kt_out_ref[gi * DEC_GROUP + b] = jnp.where(
                keep_old, pltpu.roll(k_old[u], LANES - dec_seq, axis=1), knew_ref[b])
            vt_out_ref[gi * DEC_GROUP + b] = jnp.where(
                keep_old, pltpu.roll(v_old[u], LANES - dec_seq, axis=1), vnew_ref[b])

    def attn_finish(gi):
        grows = slice(gi * DEC_ROWS, (gi + 1) * DEC_ROWS)
        y1 = jnp.where(first, oblk_ref[3], oblk_ref[4])
        moved = pltpu.roll(jnp.concatenate([oblk_ref[1], y1, oblk_ref[6], oblk_ref[6]], axis=1), HEAD_DIM, axis=1)
        outs = [
            jnp.where(first, oblk_ref[0], moved[:, 0:LANES]),
            jnp.where(first, oblk_ref[2], moved[:, LANES:2 * LANES]),
            jnp.where(first, moved[:, 2 * LANES:3 * LANES], oblk_ref[5]),
            jnp.where(first, moved[:, 3 * LANES:4 * LANES], oblk_ref[7]),
        ]
        for c in range(SWA_WIDTH // LANES):
            mix_swa_ref[grows, c * LANES:(c + 1) * LANES] = outs[c].astype(BF16)

    ret_stage()
    for gi in range(GROUPS_PER_STEP):
        attn_stage(gi)
        ret_intra(gi)
        n_blocks = DEC_GROUP // DEC_UNROLL
        per_eb = n_blocks // 2
        d_step = HEAD_DIM // per_eb
        for eb in range(2):
            accs = None
            for part in range(per_eb):
                accs = ret_block(gi, eb, part * d_step, (part + 1) * d_step, accs)
                attn_block(gi, eb * per_eb + part)
        attn_finish(gi)
    ret_finish()


def _decode_mixers(sinks, main, qn, kn, vs, state_t, k_t, v_t, dec_seq):
    nb = state_t.shape[-1]
    m = main.shape[0]
    assert nb == LANES and m == nb * dec_seq and k_t.shape == (nb, SWA_KV_WIDTH, WINDOW)
    assert DEC_GROUP * dec_seq == DEC_ROWS and nb == N_PAIRS * GROUPS_PER_STEP * DEC_GROUP
    assert (DEC_GROUP // DEC_UNROLL) % 2 == 0 and dec_seq & (dec_seq - 1) == 0
    tab = lambda f: jnp.asarray([[f(h, j) for j in range(dec_seq)] for h in range(N_RET_HEADS)], F32)
    qdec = tab(lambda h, j: math.exp(LOG_DECAY[h] * (j + 1.0)))
    kdec = tab(lambda h, j: math.exp(LOG_DECAY[h] * (dec_seq - 1.0 - j)))
    intra = tab(lambda h, j: math.exp(LOG_DECAY[h] * j))
    sdec = jnp.asarray([math.exp(LOG_DECAY[h] * dec_seq) for h in range(N_RET_HEADS)], F32)
    smem = pl.BlockSpec(memory_space=pltpu.SMEM)
    col = lambda base: pl.BlockSpec((m, LANES), lambda p: (0, base + p))
    st_spec = pl.BlockSpec((2, HEAD_DIM, HEAD_DIM, nb), lambda p: (p, 0, 0, 0))
    step_rows = GROUPS_PER_STEP * DEC_ROWS
    row = lambda w: pl.BlockSpec((step_rows, w), lambda p: (p, 0))
    cache = pl.BlockSpec((GROUPS_PER_STEP * DEC_GROUP, SWA_KV_WIDTH, WINDOW), lambda p: (p, 0, 0))
    stage = pltpu.VMEM((dec_seq, LANES, nb), F32)
    return pl.pallas_call(
        functools.partial(_decode_mixers_kernel, dec_seq, nb),
        grid=(N_PAIRS,),
        in_specs=[smem, smem, smem, smem, smem,
                  col(0), col(N_PAIRS), col(2 * N_PAIRS), col(3 * N_PAIRS), st_spec,
                  row(SWA_WIDTH), row(SWA_KV_WIDTH), row(SWA_KV_WIDTH), cache, cache],
        out_specs=[pl.BlockSpec((m, LANES), lambda p: (0, p)), st_spec, row(SWA_WIDTH), cache, cache],
        out_shape=[jax.ShapeDtypeStruct((m, RET_WIDTH), F32),
                   jax.ShapeDtypeStruct(state_t.shape, F32),
                   jax.ShapeDtypeStruct((m, SWA_WIDTH), BF16),
                   jax.ShapeDtypeStruct(k_t.shape, F32),
                   jax.ShapeDtypeStruct(v_t.shape, F32)],
        scratch_shapes=[
            stage, stage, stage, stage, stage, stage,
            pltpu.VMEM((2, N_SWA_HEADS * dec_seq, WINDOW), F32),
            pltpu.VMEM((N_SWA_HEADS, DEC_ROWS, LANES), F32),
            pltpu.VMEM((N_SWA_HEADS, DEC_ROWS, LANES), F32),
            pltpu.VMEM((DEC_GROUP, SWA_KV_WIDTH, LANES), F32),
            pltpu.VMEM((DEC_GROUP, SWA_KV_WIDTH, LANES), F32),
            pltpu.VMEM((SWA_KV_WIDTH, DEC_ROWS), BF16),
            pltpu.VMEM((SWA_KV_WIDTH, DEC_ROWS), BF16),
        ],
        compiler_params=pltpu.CompilerParams(
            dimension_semantics=("arbitrary",), vmem_limit_bytes=VMEM_LIMIT),
        name="decode_mixers",
    )(qdec, kdec, sdec, intra, sinks, main, main, main, main, state_t, qn, kn, vs, k_t, v_t)


def kernel(x_prompt, x_sample, state_ret, cache_swa_k, cache_swa_v, norm_mix_gain, w_in, q_norm_gain,
           k_norm_gain, attn_sinks, w_out, norm_ffn_gain, w_up, w_down):
    batch, seq, d = x_prompt.shape
    nb, dec_seq, _ = x_sample.shape
    wb = cache_swa_k.shape[1]
    assert d == D_MODEL and seq % PROMPT_TILE == 0 and wb == WINDOW

    gain_mix = norm_mix_gain.reshape(1, D_MODEL)
    gain_ffn = norm_ffn_gain.reshape(1, D_MODEL)
    qg = q_norm_gain.reshape(1, HEAD_DIM)
    kg = k_norm_gain.reshape(1, HEAD_DIM)

    def from_key_minor(a_t):
        return jnp.transpose(a_t.reshape(a_t.shape[0], N_SWA_KV, HEAD_DIM, WINDOW), (0, 3, 1, 2))

    def to_key_minor(a):
        return jnp.transpose(a, (0, 2, 3, 1)).reshape(a.shape[0], SWA_KV_WIDTH, WINDOW)

    xs = x_sample.reshape(nb * dec_seq, D_MODEL)
    main_s, qn_s, kn_s, vs_s, w_in_bf = _in_proj_cast(xs, gain_mix, w_in, qg, kg)
    mix_ret_s, state_t, mix_swa_s, k_t, v_t = _decode_mixers(
        attn_sinks, main_s, qn_s, kn_s, vs_s, jnp.transpose(state_ret, (1, 2, 3, 0)),
        to_key_minor(cache_swa_k), to_key_minor(cache_swa_v), dec_seq)
    y_s, w_out_bf, w_up_bf, w_down_bf = _out_mlp_cast(mix_ret_s, mix_swa_s, xs, gain_ffn, w_out, w_up, w_down)

    xp = x_prompt.reshape(batch * seq, D_MODEL)
    y_p, ret_p, kwin_t, vwin_t = _prompt_layer(attn_sinks, xp, gain_mix, w_in_bf, qg, kg, w_out_bf, gain_ffn,
                                               w_up_bf, w_down_bf, batch, seq)
    y_p = y_p.reshape(batch, seq, D_MODEL)

    return (y_p, y_s.reshape(nb, dec_seq, D_MODEL), ret_p, from_key_minor(kwin_t), from_key_minor(vwin_t),
            jnp.transpose(state_t, (3, 0, 1, 2)), from_key_minor(k_t), from_key_minor(v_t))
```

```python
import functools
import math

import jax
import jax.numpy as jnp
from jax import lax
from jax.experimental import pallas as pl
from jax.experimental.pallas import tpu as pltpu

F32 = jnp.float32
BF16 = jnp.bfloat16

D_MODEL = 1024
HEAD_DIM = 64
N_RET_HEADS = 8
N_SWA_HEADS = 8
N_SWA_KV = 2
SWA_GROUP = N_SWA_HEADS // N_SWA_KV
RET_WIDTH = N_RET_HEADS * HEAD_DIM
SWA_WIDTH = N_SWA_HEADS * HEAD_DIM
SWA_KV_WIDTH = N_SWA_KV * HEAD_DIM
MAIN_WIDTH = 4 * RET_WIDTH
IN_WIDTH = MAIN_WIDTH + SWA_WIDTH + 2 * SWA_KV_WIDTH
MIX_WIDTH = RET_WIDTH + SWA_WIDTH
D_FF = 4 * D_MODEL
WINDOW = 128
RET_CHUNK = 128
EPS = 1e-6
NEG_INF = -1e30

LANES = 128
N_PAIRS = N_RET_HEADS // 2
LOG_DECAY = [math.log(1.0 - 2.0 ** (-5.0 - h)) for h in range(N_RET_HEADS)]
ALIBI_SLOPES = [2.0 ** (-8.0 * (h + 1) / N_SWA_HEADS) for h in range(N_SWA_HEADS)]
K_SCALE = HEAD_DIM ** -0.5

MXU_WIDTH = 256
PROJ_BLOCK = 2 * MXU_WIDTH
Q_BLOCK = MXU_WIDTH
PROMPT_TILE = 512
DEC_GROUP = 16
VMEM_LIMIT = 56 * 1024 * 1024
PROMPT_VMEM_LIMIT = 62 * 1024 * 1024


def _dot(a, b):
    return jnp.dot(a, b, preferred_element_type=F32)


def _dot_nt(a, b):
    return lax.dot_general(a, b, (((1,), (1,)), ((), ())), preferred_element_type=F32)


def _iota(shape, dim):
    return lax.broadcasted_iota(jnp.int32, shape, dim)


def _ones_block_diag():
    same = ((_iota((2 * LANES, LANES), 0) >> 6) & 1) == (_iota((2 * LANES, LANES), 1) >> 6)
    return jnp.where(same, 1.0, 0.0).astype(BF16)


def _head_sumsq(x, ones_bd):
    x2 = x * x
    hi = x2.astype(BF16)
    lo = (x2 - hi.astype(F32)).astype(BF16)
    return _dot(jnp.concatenate([hi, lo], axis=1), ones_bd)


def _head_rms_scale(x, ones_bd):
    return lax.rsqrt(_head_sumsq(x, ones_bd) * (1.0 / HEAD_DIM) + EPS)


def _head_rms_scales(xs, ones_bd):
    if not xs:
        return []
    parts = []
    for x in xs:
        x2 = x * x
        hi = x2.astype(BF16)
        parts.append(jnp.concatenate([hi, (x2 - hi.astype(F32)).astype(BF16)], axis=1))
    total = _dot(jnp.concatenate(parts, axis=0) if len(parts) > 1 else parts[0], ones_bd)
    out, lo = [], 0
    for x in xs:
        out.append(lax.rsqrt(total[lo:lo + x.shape[0]] * (1.0 / HEAD_DIM) + EPS))
        lo += x.shape[0]
    return out


def _first_half():
    return _iota((1, LANES), 1) < HEAD_DIM


def _pair_const(values, pair, lane_is_second):
    return jnp.where(lane_is_second, values[2 * pair + 1], values[2 * pair]).astype(F32)


def _pair_gain(gain_ref):
    g = gain_ref[...]
    return jnp.concatenate([g, g], axis=1)


def _silu(g):
    return g * (1.0 / (1.0 + jnp.exp(-g)))


def _in_proj_kernel(x_ref, gain_ref, w_ref, qg_ref, kg_ref, main_ref, qn_ref, kn_ref, vs_ref, fetch=None):
    x = x_ref[...]
    ms = jnp.mean(x * x, axis=-1, keepdims=True)
    hb = ((x * lax.rsqrt(ms + EPS)) * gain_ref[...]).astype(BF16)
    ones_bd = _ones_block_diag()

    def project(lo, width):
        if fetch is not None:
            fetch(lo, width)
        return _dot(hb, w_ref[:, lo:lo + width])

    for c in range(MAIN_WIDTH // PROJ_BLOCK):
        main_ref[:, c * PROJ_BLOCK:(c + 1) * PROJ_BLOCK] = project(c * PROJ_BLOCK, PROJ_BLOCK)
    qs = project(MAIN_WIDTH, SWA_WIDTH)
    k0 = MAIN_WIDTH + SWA_WIDTH
    ks = project(k0, SWA_KV_WIDTH)
    vs_ref[...] = project(k0 + SWA_KV_WIDTH, SWA_KV_WIDTH)
    q_cols = [qs[:, c * LANES:(c + 1) * LANES] for c in range(SWA_WIDTH // LANES)]
    scales = _head_rms_scales(q_cols + [ks], ones_bd)
    for c, xc in enumerate(q_cols):
        qn_ref[:, c * LANES:(c + 1) * LANES] = (xc * scales[c]) * _pair_gain(qg_ref)
    kn_ref[...] = (ks * scales[-1]) * _pair_gain(kg_ref)


ROW_TILE = 512


W_IN_BLOCKS = [(c * PROJ_BLOCK, PROJ_BLOCK) for c in range(MAIN_WIDTH // PROJ_BLOCK)] + [(MAIN_WIDTH, SWA_WIDTH),
                                                                    (MAIN_WIDTH + SWA_WIDTH, 2 * SWA_KV_WIDTH)]
assert sum(w for _, w in W_IN_BLOCKS) == IN_WIDTH


def _in_proj_cast_kernel(x_ref, gain_ref, w_hbm, qg_ref, kg_ref,
                         main_ref, qn_ref, kn_ref, vs_ref, w_bf_hbm,
                         w_v, stage, in_sem, out_sem):
    step = pl.program_id(0)
    block_in = [pltpu.make_async_copy(w_hbm.at[:, pl.ds(lo, width)], stage.at[:, pl.ds(lo, width)], in_sem.at[i])
                for i, (lo, width) in enumerate(W_IN_BLOCKS)]
    writeback = pltpu.make_async_copy(w_v, w_bf_hbm, out_sem.at[0])

    @pl.when(step == 0)
    def _():
        for copy in block_in:
            copy.start()
        arrived = set()

        def fetch(lo, width):
            for i, (blo, bwidth) in enumerate(W_IN_BLOCKS):
                if blo <= lo < blo + bwidth and i not in arrived:
                    assert lo + width <= blo + bwidth
                    block_in[i].wait()
                    w_v[:, blo:blo + bwidth] = stage[:, blo:blo + bwidth].astype(BF16)
                    arrived.add(i)

        _in_proj_kernel(x_ref, gain_ref, w_v, qg_ref, kg_ref, main_ref, qn_ref, kn_ref, vs_ref, fetch=fetch)
        assert len(arrived) == len(W_IN_BLOCKS)
        writeback.start()

    @pl.when(step > 0)
    def _():
        _in_proj_kernel(x_ref, gain_ref, w_v, qg_ref, kg_ref, main_ref, qn_ref, kn_ref, vs_ref)

    @pl.when(step == pl.num_programs(0) - 1)
    def _():
        writeback.wait()


def _in_proj_cast(x2d, gain, w_in, qg, kg):
    m = x2d.shape[0]
    tm = min(ROW_TILE, m)
    row = lambda w: pl.BlockSpec((tm, w), lambda i: (i, 0))
    full = lambda a: pl.BlockSpec(a.shape, lambda i: (0, 0), pipeline_mode=pl.Buffered(1))
    hbm = pl.BlockSpec(memory_space=pl.ANY)
    return pl.pallas_call(
        _in_proj_cast_kernel,
        grid=(m // tm,),
        in_specs=[row(D_MODEL), full(gain), hbm, full(qg), full(kg)],
        out_specs=[row(MAIN_WIDTH), row(SWA_WIDTH), row(SWA_KV_WIDTH), row(SWA_KV_WIDTH), hbm],
        out_shape=[jax.ShapeDtypeStruct((m, MAIN_WIDTH), F32),
                   jax.ShapeDtypeStruct((m, SWA_WIDTH), F32),
                   jax.ShapeDtypeStruct((m, SWA_KV_WIDTH), F32),
                   jax.ShapeDtypeStruct((m, SWA_KV_WIDTH), F32),
                   jax.ShapeDtypeStruct(w_in.shape, BF16)],
        scratch_shapes=[pltpu.VMEM(w_in.shape, BF16), pltpu.VMEM(w_in.shape, F32),
                        pltpu.SemaphoreType.DMA((len(W_IN_BLOCKS),)), pltpu.SemaphoreType.DMA((1,))],
        compiler_params=pltpu.CompilerParams(
            dimension_semantics=("arbitrary",), vmem_limit_bytes=VMEM_LIMIT),
        name="in_proj_cast",
    )(x2d, gain, w_in, qg, kg)


FF_CHUNK = 1024


def _out_mlp_kernel(mix_ret_ref, mix_swa_ref, x_ref, w_out_ref, gain_ref, w_up_ref, w_down_ref, y_ref):
    h = x_ref[...] + (_dot(mix_ret_ref[...].astype(BF16), w_out_ref[:RET_WIDTH, :])
                      + _dot(mix_swa_ref[...].astype(BF16), w_out_ref[RET_WIDTH:, :]))
    ms = jnp.mean(h * h, axis=-1, keepdims=True)
    hf = ((h * lax.rsqrt(ms + EPS)) * gain_ref[...]).astype(BF16)
    ff = None
    for c in range(D_FF // FF_CHUNK):
        u = _dot(hf, w_up_ref[:, c * FF_CHUNK:(c + 1) * FF_CHUNK])
        a = jnp.maximum(u, 0.0)
        d = _dot((a * a).astype(BF16), w_down_ref[c * FF_CHUNK:(c + 1) * FF_CHUNK, :])
        ff = d if ff is None else ff + d
    y_ref[...] = h + ff


CAST_CHUNK = 512
N_FF_CHUNKS = D_FF // CAST_CHUNK
N_STAGE = 4


def _out_mlp_cast_kernel(mix_ret_ref, mix_swa_ref, x_ref, gain_ref, w_out_hbm, w_up_hbm, w_down_hbm,
                         y_ref, w_out_bf_hbm, w_up_bf_hbm, w_down_bf_hbm,
                         w_out_v, w_up_v, w_down_v, stage_out, stage_up, stage_down, in_sem, out_sem):
    step = pl.program_id(0)

    def out_in(i):
        return pltpu.make_async_copy(w_out_hbm.at[pl.ds(i * CAST_CHUNK, CAST_CHUNK), :], stage_out.at[i],
                                     in_sem.at[0, i])

    def up_in(c):
        return pltpu.make_async_copy(w_up_hbm.at[:, pl.ds(c * CAST_CHUNK, CAST_CHUNK)], stage_up.at[c % N_STAGE],
                                     in_sem.at[1, c % N_STAGE])

    def down_in(c):
        return pltpu.make_async_copy(w_down_hbm.at[pl.ds(c * CAST_CHUNK, CAST_CHUNK), :], stage_down.at[c % N_STAGE],
                                     in_sem.at[2, c % N_STAGE])

    writebacks = [pltpu.make_async_copy(src, dst, out_sem.at[i]) for i, (src, dst) in enumerate(
        [(w_out_v, w_out_bf_hbm), (w_up_v, w_up_bf_hbm), (w_down_v, w_down_bf_hbm)])]

    @pl.when(step == 0)
    def _():
        n_out = D_MODEL // CAST_CHUNK
        for i in range(n_out):
            out_in(i).start()
        for c in range(N_STAGE - 1):
            up_in(c).start()
            down_in(c).start()
        for i in range(n_out):
            out_in(i).wait()
            w_out_v[i * CAST_CHUNK:(i + 1) * CAST_CHUNK, :] = stage_out[i].astype(BF16)
        writebacks[0].start()
        h = x_ref[...] + (_dot(mix_ret_ref[...].astype(BF16), w_out_v[:RET_WIDTH, :])
                          + _dot(mix_swa_ref[...].astype(BF16), w_out_v[RET_WIDTH:, :]))
        ms = jnp.mean(h * h, axis=-1, keepdims=True)
        hf = ((h * lax.rsqrt(ms + EPS)) * gain_ref[...]).astype(BF16)
        ff = None
        for c in range(N_FF_CHUNKS):
            chunk = slice(c * CAST_CHUNK, (c + 1) * CAST_CHUNK)
            if c + N_STAGE - 1 < N_FF_CHUNKS:
                up_in(c + N_STAGE - 1).start()
                down_in(c + N_STAGE - 1).start()
            up_in(c).wait()
            down_in(c).wait()
            w_up_v[:, chunk] = stage_up[c % N_STAGE].astype(BF16)
            w_down_v[chunk, :] = stage_down[c % N_STAGE].astype(BF16)
            a = jnp.maximum(_dot(hf, w_up_v[:, chunk]), 0.0)
            d = _dot((a * a).astype(BF16), w_down_v[chunk, :])
            ff = d if ff is None else ff + d
        writebacks[1].start()
        writebacks[2].start()
        y_ref[...] = h + ff

    @pl.when(step > 0)
    def _():
        _out_mlp_kernel(mix_ret_ref, mix_swa_ref, x_ref, w_out_v, gain_ref, w_up_v, w_down_v, y_ref)

    @pl.when(step == pl.num_programs(0) - 1)
    def _():
        for wb in writebacks:
            wb.wait()


def _out_mlp_cast(mix_ret, mix_swa, x2d, gain, w_out, w_up, w_down):
    m = x2d.shape[0]
    tm = min(ROW_TILE, m)
    assert D_MODEL % CAST_CHUNK == 0 and D_FF % CAST_CHUNK == 0
    row = lambda w: pl.BlockSpec((tm, w), lambda i: (i, 0))
    full = lambda a: pl.BlockSpec(a.shape, lambda i: (0, 0), pipeline_mode=pl.Buffered(1))
    hbm = pl.BlockSpec(memory_space=pl.ANY)
    return pl.pallas_call(
        _out_mlp_cast_kernel,
        grid=(m // tm,),
        in_specs=[row(RET_WIDTH), row(SWA_WIDTH), row(D_MODEL), full(gain), hbm, hbm, hbm],
        out_specs=[row(D_MODEL), hbm, hbm, hbm],
        out_shape=[jax.ShapeDtypeStruct((m, D_MODEL), F32),
                   jax.ShapeDtypeStruct(w_out.shape, BF16),
                   jax.ShapeDtypeStruct(w_up.shape, BF16),
                   jax.ShapeDtypeStruct(w_down.shape, BF16)],
        scratch_shapes=[
            pltpu.VMEM(w_out.shape, BF16), pltpu.VMEM(w_up.shape, BF16), pltpu.VMEM(w_down.shape, BF16),
            pltpu.VMEM((D_MODEL // CAST_CHUNK, CAST_CHUNK, D_MODEL), F32),
            pltpu.VMEM((N_STAGE, D_MODEL, CAST_CHUNK), F32),
            pltpu.VMEM((N_STAGE, CAST_CHUNK, D_MODEL), F32),
            pltpu.SemaphoreType.DMA((3, max(N_STAGE, D_MODEL // CAST_CHUNK))),
            pltpu.SemaphoreType.DMA((3,)),
        ],
        compiler_params=pltpu.CompilerParams(
            dimension_semantics=("arbitrary",), vmem_limit_bytes=PROMPT_VMEM_LIMIT),
        name="out_mlp_cast",
    )(mix_ret, mix_swa, x2d, gain, w_out, w_up, w_down)


def _split_pair_rows(x, first):
    return jnp.concatenate([jnp.where(first, x, 0.0), jnp.where(first, 0.0, x)], axis=0).astype(BF16)


def _softmax_sink_pv(s, sink_wide, v_t):
    m = jnp.maximum(jnp.max(s, axis=-1, keepdims=True), sink_wide)
    p = jnp.exp(s - jnp.concatenate([m, m], axis=1))
    denom = jnp.sum(p, axis=-1, keepdims=True) + jnp.exp(sink_wide - m)
    return _dot_nt(p.astype(BF16), v_t) / denom


def _softmax_sink_pv_t(s_t, sink_lanes, v_t):
    m = jnp.maximum(jnp.max(s_t, axis=0, keepdims=True), sink_lanes)
    p = jnp.exp(s_t - m)
    denom = jnp.sum(p, axis=0, keepdims=True) + jnp.exp(sink_lanes - m)
    return _dot(v_t, p.astype(BF16)) / denom


def _prompt_consts(intra_ref, qdec_ref, kdec_ref, sdec_ref, bias_ref):
    r = _iota((LANES, LANES), 0)
    lane2 = _iota((LANES, LANES), 1) >= HEAD_DIM
    rf = r.astype(F32)
    ri = _iota((LANES, 2 * LANES), 0)
    ci = _iota((LANES, 2 * LANES), 1)
    diff = (ri - (ci & (LANES - 1))).astype(F32)
    for p in range(N_PAIRS):
        lg = _pair_const(LOG_DECAY, p, lane2)
        qdec_ref[p] = jnp.exp(lg * (rf + 1.0))
        kdec_ref[p] = jnp.exp(lg * (RET_CHUNK - 1.0 - rf))
        sdec_ref[p] = jnp.exp(_pair_const(LOG_DECAY, p, r >= HEAD_DIM) * float(RET_CHUNK))
        lg2 = _pair_const(LOG_DECAY, p, ci >= LANES)
        intra_ref[p] = jnp.where(diff >= 0.0, jnp.exp(lg2 * jnp.maximum(diff, 0.0)), 0.0)
    cols = SWA_GROUP * WINDOW
    kb = _iota((2 * WINDOW, cols), 0)
    cb = _iota((2 * WINDOW, cols), 1)
    grp = cb >> 7
    dist = WINDOW + (cb & (WINDOW - 1)) - kb
    valid = (dist >= 0) & (dist < WINDOW)
    distf = dist.astype(F32)
    for j in range(N_SWA_KV):
        sl = [ALIBI_SLOPES[SWA_GROUP * j + g] for g in range(SWA_GROUP)]
        slope = jnp.where(grp == 0, sl[0], jnp.where(grp == 1, sl[1], jnp.where(grp == 2, sl[2], sl[3])))
        b = jnp.where(valid, -(slope.astype(F32) * distf), NEG_INF)
        bias_ref[0, j] = b
        bias_ref[1, j] = jnp.where(kb >= WINDOW, b, NEG_INF)


def _prompt_layer_kernel(sinks_ref, x_ref, x_next_ref, gain_mix_ref, w_in_ref, qg_ref, kg_ref,
                         w_out_hbm, gain_ffn_ref, w_up_hbm, w_down_hbm,
                         y_ref, ret_ref, kwin_ref, vwin_ref,
                         main_ref, qn_ref, kn_ref, vs_ref, mix_ref, hb_ref,
                         state_ref, prevk_ref, prevv_ref,
                         intra_ref, qdec_ref, kdec_ref, sdec_ref, bias_ref,
                         w_out_ref, w_up_ref, w_down_ref, w_sem):
    t = pl.program_id(1)
    step = pl.program_id(0) * pl.num_programs(1) + t
    cur = step % 2
    nxt = 1 - cur

    late_weights = [pltpu.make_async_copy(src, dst, w_sem.at[i]) for i, (src, dst) in enumerate(
        [(w_out_hbm, w_out_ref), (w_up_hbm, w_up_ref), (w_down_hbm, w_down_ref)])]

    @pl.when(step == 0)
    def _():
        for copy in late_weights:
            copy.start()
        _prompt_consts(intra_ref, qdec_ref, kdec_ref, sdec_ref, bias_ref)
        _in_proj_kernel(x_ref, gain_mix_ref, w_in_ref, qg_ref, kg_ref,
                        main_ref.at[0], qn_ref.at[0], kn_ref.at[0], vs_ref.at[0])

    @pl.when(t == 0)
    def _():
        state_ref[...] = jnp.zeros_like(state_ref)
        prevk_ref[...] = jnp.zeros_like(prevk_ref)
        prevv_ref[...] = jnp.zeros_like(prevv_ref)

    first = _first_half()
    ones_bd = _ones_block_diag()
    bd_mask = (_iota((LANES, LANES), 0) >= HEAD_DIM) == (_iota((LANES, LANES), 1) >= HEAD_DIM)

    n_chunks = PROMPT_TILE // RET_CHUNK
    assert n_chunks == 4
    plan = [dict(a=[], b=[]),
            dict(a=["m0", "m1"], b=[]),
            dict(a=["m2", "m3"], b=[]),
            dict(a=["v", "k"], b=["q0", "q1"])]
    a0 = MAIN_WIDTH
    attn_cols = {"v": (a0 + SWA_WIDTH + SWA_KV_WIDTH, SWA_KV_WIDTH), "k": (a0 + SWA_WIDTH, SWA_KV_WIDTH)}
    attn_cols.update({"q%d" % i: (a0 + Q_BLOCK * i, Q_BLOCK) for i in range(SWA_WIDTH // Q_BLOCK)})
    cols = lambda base, p: slice(base + p * LANES, base + (p + 1) * LANES)

    def project(items):
        raw = {}
        for it in items:
            if it[0] == "m":
                cb = int(it[1:])
                blk = slice(cb * PROJ_BLOCK, (cb + 1) * PROJ_BLOCK)
                main_ref[nxt, :, blk] = _dot(hb_ref[...], w_in_ref[:, blk])
            else:
                lo, width = attn_cols[it]
                raw[it] = _dot(hb_ref[...], w_in_ref[:, lo:lo + width])
        return raw

    def attn_scales(raw):
        return {name: [_head_rms_scale(val[:, cols(0, cq)], ones_bd) for cq in range(val.shape[1] // LANES)]
                for name, val in raw.items() if name != "v"}

    def store_attn(raw, scales):
        for name, val in raw.items():
            if name == "v":
                vs_ref[nxt] = val
            elif name == "k":
                kn_ref[nxt] = (val * scales[name][0]) * _pair_gain(kg_ref)
            else:
                base = attn_cols[name][0] - a0
                for cq in range(val.shape[1] // LANES):
                    qn_ref[nxt, :, cols(base, cq)] = (val[:, cols(0, cq)] * scales[name][cq]) * _pair_gain(qg_ref)

    pairs = range(N_PAIRS)
    kvs = range(N_SWA_KV)

    def stage1(c):
        rows = slice(c * RET_CHUNK, (c + 1) * RET_CHUNK)
        q = [main_ref[cur, rows, cols(0, p)] for p in pairs]
        k = [main_ref[cur, rows, cols(RET_WIDTH, p)] * K_SCALE for p in pairs]
        v = [main_ref[cur, rows, cols(2 * RET_WIDTH, p)] for p in pairs]
        state = [state_ref[p] for p in pairs]
        kc = kn_ref[cur, rows, :]
        k_sw = pltpu.roll(kc, HEAD_DIM, axis=1)
        v_t = vs_ref[cur, rows, :].T
        is_first = (t == 0).astype(jnp.int32) if c == 0 else 0
        k_dup =[(jnp.where(first, kc, k_sw) if j == 0 else jnp.where(first, k_sw, kc)).astype(BF16)
                 for j in kvs]
        v_tj = [v_t[j * HEAD_DIM:(j + 1) * HEAD_DIM].astype(BF16) for j in kvs]
        q_st = []
        for j in kvs:
            pieces = []
            for g in range(SWA_GROUP):
                qc = qn_ref[cur, rows, cols(0, 2 * j + g // 2)]
                pieces.append(jnp.where(first, qc, 0.0) if g % 2 == 0 else jnp.where(first, 0.0, qc))
            q_st.append(jnp.concatenate(pieces, axis=0).astype(BF16))

        s = [_dot_nt(q[p].astype(BF16), _split_pair_rows(k[p], first)) for p in pairs]
        s_t = [_dot_nt(jnp.concatenate([prevk_ref[j], k_dup[j]], axis=0), q_st[j]) for j in kvs]
        cross = [_dot((q[p] * qdec_ref[p]).astype(BF16), state[p].astype(BF16)) for p in pairs]
        upd = [_dot((k[p] * kdec_ref[p]).T.astype(BF16), v[p].astype(BF16)) for p in pairs]
        return dict(rows=rows, v=v, state=state, k_dup=k_dup, v_tj=v_tj, is_first=is_first,
                    s=s, s_t=s_t, cross=cross, upd=upd)

    def stage2(st):
        st["o"] = [_dot((st["s"][p] * intra_ref[p]).astype(BF16), _split_pair_rows(st["v"][p], first))
                   + st["cross"][p] for p in pairs]
        st["o_t"] = []
        for j in kvs:
            sink_lanes = jnp.concatenate(
                [jnp.full((1, WINDOW), sinks_ref[SWA_GROUP * j + g], F32) for g in range(SWA_GROUP)],
                axis=1)
            v_cat = jnp.concatenate([prevv_ref[j], st["v_tj"][j]], axis=1)
            st["o_t"].append(_softmax_sink_pv_t(st["s_t"][j] * K_SCALE + bias_ref[st["is_first"], j],
                                                sink_lanes, v_cat))
        for p in pairs:
            state_ref[p] = st["state"][p] * sdec_ref[p] + jnp.where(bd_mask, st["upd"][p], 0.0)
        for j in kvs:
            prevk_ref[j] = st["k_dup"][j]
            prevv_ref[j] = st["v_tj"][j]

    def stage3(st, raw):
        rows = st["rows"]
        scale = [_head_rms_scale(st["o"][p], ones_bd) for p in pairs]
        raw_scales = attn_scales(raw)
        for p in pairs:
            g = main_ref[cur, rows, cols(3 * RET_WIDTH, p)]
            mix_ref[rows, cols(0, p)] = (st["o"][p] * scale[p] * _silu(g)).astype(BF16)
        store_attn(raw, raw_scales)
        for j in kvs:
            o_t = st["o_t"][j]
            for half in range(2):
                pair_t = jnp.concatenate([o_t[:, (2 * half) * WINDOW:(2 * half + 1) * WINDOW],
                                          o_t[:, (2 * half + 1) * WINDOW:(2 * half + 2) * WINDOW]], axis=0)
                mix_ref[rows, cols(RET_WIDTH, 2 * j + half)] = pair_t.T.astype(BF16)

    xn = x_next_ref[...]
    hb_ref[...] = ((xn * lax.rsqrt(jnp.mean(xn * xn, axis=-1, keepdims=True) + EPS)) * gain_mix_ref[...]).astype(BF16)
    for c in range(n_chunks):
        st = stage1(c)
        raw = project(plan[c]["a"])
        stage2(st)
        raw.update(project(plan[c]["b"]))
        stage3(st, raw)

    @pl.when(t == pl.num_programs(1) - 1)
    def _():
        for p in range(N_PAIRS):
            s = state_ref[p]
            ret_ref[2 * p] = s[:HEAD_DIM, :HEAD_DIM]
            ret_ref[2 * p + 1] = s[HEAD_DIM:, HEAD_DIM:]
        last = slice(PROMPT_TILE - WINDOW, PROMPT_TILE)
        kwin_ref[...] = kn_ref[cur, last, :].T
        vwin_ref[...] = vs_ref[cur, last, :].T

    @pl.when(step == 0)
    def _():
        for copy in late_weights:
            copy.wait()

    _out_mlp_kernel(mix_ref.at[:, pl.ds(0, RET_WIDTH)], mix_ref.at[:, pl.ds(RET_WIDTH, SWA_WIDTH)], x_ref,
                    w_out_ref, gain_ffn_ref, w_up_ref, w_down_ref, y_ref)


def _prompt_layer(sinks, x2d, gain_mix, w_in_bf, qg, kg, w_out_bf, gain_ffn, w_up_bf, w_down_bf, batch, seq):
    nt = seq // PROMPT_TILE
    last_tile = batch * nt - 1
    row = lambda w: pl.BlockSpec((PROMPT_TILE, w), lambda b, t: (b * nt + t, 0))
    next_row = pl.BlockSpec((PROMPT_TILE, D_MODEL), lambda b, t: (jnp.minimum(b * nt + t + 1, last_tile), 0))
    full = lambda a: pl.BlockSpec(a.shape, lambda b, t: (0, 0), pipeline_mode=pl.Buffered(1))
    in_hbm = pl.BlockSpec(memory_space=pl.ANY)
    return pl.pallas_call(
        _prompt_layer_kernel,
        grid=(batch, nt),
        in_specs=[pl.BlockSpec(memory_space=pltpu.SMEM), row(D_MODEL), next_row,
                  full(gain_mix), full(w_in_bf), full(qg), full(kg),
                  in_hbm, full(gain_ffn), in_hbm, in_hbm],
        out_specs=[row(D_MODEL),
                   pl.BlockSpec((None, N_RET_HEADS, HEAD_DIM, HEAD_DIM), lambda b, t: (b, 0, 0, 0)),
                   pl.BlockSpec((None, SWA_KV_WIDTH, WINDOW), lambda b, t: (b, 0, 0)),
                   pl.BlockSpec((None, SWA_KV_WIDTH, WINDOW), lambda b, t: (b, 0, 0))],
        out_shape=[jax.ShapeDtypeStruct((batch * seq, D_MODEL), F32),
                   jax.ShapeDtypeStruct((batch, N_RET_HEADS, HEAD_DIM, HEAD_DIM), F32),
                   jax.ShapeDtypeStruct((batch, SWA_KV_WIDTH, WINDOW), F32),
                   jax.ShapeDtypeStruct((batch, SWA_KV_WIDTH, WINDOW), F32)],
        scratch_shapes=[
            pltpu.VMEM((2, PROMPT_TILE, MAIN_WIDTH), F32),
            pltpu.VMEM((2, PROMPT_TILE, SWA_WIDTH), F32),
            pltpu.VMEM((2, PROMPT_TILE, SWA_KV_WIDTH), F32),
            pltpu.VMEM((2, PROMPT_TILE, SWA_KV_WIDTH), F32),
            pltpu.VMEM((PROMPT_TILE, MIX_WIDTH), BF16),
            pltpu.VMEM((PROMPT_TILE, D_MODEL), BF16),
            pltpu.VMEM((N_PAIRS, LANES, LANES), F32),
            pltpu.VMEM((N_SWA_KV, WINDOW, LANES), BF16),
            pltpu.VMEM((N_SWA_KV, HEAD_DIM, WINDOW), BF16),
            pltpu.VMEM((N_PAIRS, LANES, 2 * LANES), F32),
            pltpu.VMEM((N_PAIRS, LANES, LANES), F32),
            pltpu.VMEM((N_PAIRS, LANES, LANES), F32),
            pltpu.VMEM((N_PAIRS, LANES, LANES), F32),
            pltpu.VMEM((2, N_SWA_KV, 2 * WINDOW, SWA_GROUP * WINDOW), F32),
            pltpu.VMEM(w_out_bf.shape, BF16),
            pltpu.VMEM(w_up_bf.shape, BF16),
            pltpu.VMEM(w_down_bf.shape, BF16),
            pltpu.SemaphoreType.DMA((3,)),
        ],
        compiler_params=pltpu.CompilerParams(
            dimension_semantics=("arbitrary", "arbitrary"), vmem_limit_bytes=PROMPT_VMEM_LIMIT),
        name="prompt_layer",
    )(sinks, x2d, x2d, gain_mix, w_in_bf, qg, kg, w_out_bf, gain_ffn, w_up_bf, w_down_bf)


DEC_ROWS = 128
DEC_UNROLL = 8


def _decode_attn_consts(dec_seq, bias_ref):
    shift = dec_seq.bit_length() - 1
    rows = N_SWA_HEADS * dec_seq
    rb = _iota((rows, WINDOW), 0)
    cb = _iota((rows, WINDOW), 1)
    head = rb >> shift
    i = rb & (dec_seq - 1)
    slope = jnp.zeros((rows, WINDOW), F32)
    for h in range(N_SWA_HEADS):
        slope = jnp.where(head == h, ALIBI_SLOPES[h], slope)
    bias_ref[0] = jnp.where(cb > i, -(slope * (WINDOW + i - cb).astype(F32)), NEG_INF)
    m = cb & (dec_seq - 1)
    bias_ref[1] = jnp.where(m <= i, -(slope * (i - m).astype(F32)), NEG_INF)


GROUPS_PER_STEP = 2


def _decode_mixers_kernel(dec_seq, nb, qdec_ref, kdec_ref, sdec_ref, intra_ref, sinks_ref,
                          q_ref, k_ref, v_ref, g_ref, st_ref, qn_ref, kn_ref, vs_ref, kt_ref, vt_ref,
                          mix_ret_ref, st_out_ref, mix_swa_ref, kt_out_ref, vt_out_ref,
                          qt_s, kt_s, vt_s, qdt_s, kdt_s, o_s,
                          bias_ref, qbd_ref, oblk_ref, knew_ref, vnew_ref, knt_ref, vst_ref):
    pair = pl.program_id(0)
    halves = [slice(0, HEAD_DIM), slice(HEAD_DIM, 2 * HEAD_DIM)]
    first = _first_half()
    shift = dec_seq.bit_length() - 1
    e_blk = HEAD_DIM // 2

    @pl.when(pair == 0)
    def _():
        _decode_attn_consts(dec_seq, bias_ref)

    def ret_stage():
        for l in range(dec_seq):
            rows = pl.ds(l, nb, stride=dec_seq)
            q_t = q_ref[rows, :].T
            k_t = (k_ref[rows, :] * K_SCALE).T
            qt_s[l] = q_t
            kt_s[l] = k_t
            vt_s[l] = v_ref[rows, :].T
            for hh in range(2):
                qdt_s[l, halves[hh], :] = q_t[halves[hh]] * qdec_ref[2 * pair + hh, l]
                kdt_s[l, halves[hh], :] = k_t[halves[hh]] * kdec_ref[2 * pair + hh, l]

    def ret_intra(hh):
        h = 2 * pair + hh
        hs = halves[hh]
        for l in range(dec_seq):
            acc = None
            for m in range(l + 1):
                sc = jnp.sum(qt_s[l, hs, :] * kt_s[m, hs, :], axis=0, keepdims=True) * intra_ref[h, l - m]
                term = sc * vt_s[m, hs, :]
                acc = term if acc is None else acc + term
            o_s[l, hs, :] = acc

    def ret_block(hh, eb, d_lo, d_hi, accs=None):
        h = 2 * pair + hh
        es = slice(eb * e_blk, (eb + 1) * e_blk)
        erows = slice(hh * HEAD_DIM + eb * e_blk, hh * HEAD_DIM + (eb + 1) * e_blk)
        if accs is None:
            accs = [jnp.zeros((e_blk, nb), F32) for _ in range(dec_seq)]
        for d in range(d_lo, d_hi):
            s_d = st_ref[hh, d, es, :]
            row = slice(hh * HEAD_DIM + d, hh * HEAD_DIM + d + 1)
            upd = s_d * sdec_ref[h]
            for l in range(dec_seq):
                accs[l] = accs[l] + qdt_s[l, row, :] * s_d
                upd = upd + kdt_s[l, row, :] * vt_s[l, erows, :]
            st_out_ref[hh, d, es, :] = upd
        if d_hi == HEAD_DIM:
            for l in range(dec_seq):
                o_s[l, erows, :] = o_s[l, erows, :] + accs[l]
        return accs

    def ret_finish():
        for l in range(dec_seq):
            o = o_s[l]
            normed = []
            for hh in range(2):
                oh = o[halves[hh]]
                normed.append(oh * lax.rsqrt(jnp.mean(oh * oh, axis=0, keepdims=True) + EPS))
            rows = pl.ds(l, nb, stride=dec_seq)
            mix_ret_ref[rows, :] = jnp.concatenate(normed, axis=0).T * _silu(g_ref[rows, :])

    sink_rows = jnp.concatenate(
        [jnp.full((dec_seq, LANES), sinks_ref[h], F32) for h in range(N_SWA_HEADS)], axis=0)
    col_batch = _iota((N_SWA_HEADS * dec_seq, LANES), 1) >> shift
    keep_old = _iota((1, LANES), 1) < WINDOW - dec_seq

    def attn_stage(gi):
        grows = slice(gi * DEC_ROWS, (gi + 1) * DEC_ROWS)
        kn_t = kn_ref[grows, :].T
        vs_t = vs_ref[grows, :].T
        knt_ref[...] = kn_t.astype(BF16)
        vst_ref[...] = vs_t.astype(BF16)
        for bb in range(DEC_GROUP):
            sh = (WINDOW - dec_seq - bb * dec_seq) % LANES
            knew_ref[bb] = pltpu.roll(kn_t, sh, axis=1) if sh else kn_t
            vnew_ref[bb] = pltpu.roll(vs_t, sh, axis=1) if sh else vs_t
        qn = qn_ref[grows, :]
        qn_sw = pltpu.roll(qn, HEAD_DIM, axis=1)
        for h in range(N_SWA_HEADS):
            kv_half = h // SWA_GROUP
            if (h % 2) == kv_half:
                src = qn[:, (h // 2) * LANES:(h // 2 + 1) * LANES]
            else:
                col = (h + 1) // 2
                src = qn_sw[:, col * LANES:(col + 1) * LANES]
            qbd_ref[h] = jnp.where(first, src, 0.0) if kv_half == 0 else jnp.where(first, 0.0, src)

    def attn_block(gi, i):
        bs = [i * DEC_UNROLL + u for u in range(DEC_UNROLL)]
        rows = [slice(b * dec_seq, (b + 1) * dec_seq) for b in bs]
        k_old = [kt_ref[gi * DEC_GROUP + b] for b in bs]
        v_old = [vt_ref[gi * DEC_GROUP + b] for b in bs]
        q_st = [jnp.concatenate([qbd_ref[h, r, :] for h in range(N_SWA_HEADS)], axis=0).astype(BF16)
                for r in rows]
        s = [_dot(q_st[u], jnp.concatenate([k_old[u].astype(BF16), knt_ref[...]], axis=1))
             for u in range(DEC_UNROLL)]
        o = []
        for u, b in enumerate(bs):
            bias = jnp.concatenate([bias_ref[0], jnp.where(col_batch == b, bias_ref[1], NEG_INF)], axis=1)
            w_v = jnp.concatenate([v_old[u].astype(BF16), vst_ref[...]], axis=1)
            o.append(_softmax_sink_pv(s[u] * K_SCALE + bias, sink_rows, w_v))
        for u, b in enumerate(bs):
            for h in range(N_SWA_HEADS):
                oblk_ref[h, rows[u], :] = o[u][h * dec_seq:(h + 1) * dec_seq]
            kt_out_ref[gi * DEC_GROUP + b] = jnp.where(
                keep_old, pltpu.roll(k_old[u], LANES - dec_seq, axis=1), knew_ref[b])
            vt_out_ref[gi * DEC_GROUP + b] = jnp.where(
                keep_old, pltpu.roll(v_old[u], LANES - dec_seq, axis=1), vnew_ref[b])

    def attn_finish(gi):
        grows = slice(gi * DEC_ROWS, (gi + 1) * DEC_ROWS)
        y1 = jnp.where(first, oblk_ref[3], oblk_ref[4])
        moved = pltpu.roll(jnp.concatenate([oblk_ref[1], y1, oblk_ref[6], oblk_ref[6]], axis=1), HEAD_DIM, axis=1)
        outs = [
            jnp.where(first, oblk_ref[0], moved[:, 0:LANES]),
            jnp.where(first, oblk_ref[2], moved[:, LANES:2 * LANES]),
            jnp.where(first, moved[:, 2 * LANES:3 * LANES], oblk_ref[5]),
            jnp.where(first, moved[:, 3 * LANES:4 * LANES], oblk_ref[7]),
        ]
        for c in range(SWA_WIDTH // LANES):
            mix_swa_ref[grows, c * LANES:(c + 1) * LANES] = outs[c].astype(BF16)

    ret_stage()
    for gi in range(GROUPS_PER_STEP):
        attn_stage(gi)
        ret_intra(gi)
        n_blocks = DEC_GROUP // DEC_UNROLL
        per_eb = n_blocks // 2
        d_step = HEAD_DIM // per_eb
        for eb in range(2):
            accs = None
            for part in range(per_eb):
                accs = ret_block(gi, eb, part * d_step, (part + 1) * d_step, accs)
                attn_block(gi, eb * per_eb + part)
        attn_finish(gi)
    ret_finish()


def _decode_mixers(sinks, main, qn, kn, vs, state_t, k_t, v_t, dec_seq):
    nb = state_t.shape[-1]
    m = main.shape[0]
    assert nb == LANES and m == nb * dec_seq and k_t.shape == (nb, SWA_KV_WIDTH, WINDOW)
    assert DEC_GROUP * dec_seq == DEC_ROWS and nb == N_PAIRS * GROUPS_PER_STEP * DEC_GROUP
    assert (DEC_GROUP // DEC_UNROLL) % 2 == 0 and dec_seq & (dec_seq - 1) == 0
    tab = lambda f: jnp.asarray([[f(h, j) for j in range(dec_seq)] for h in range(N_RET_HEADS)], F32)
    qdec = tab(lambda h, j: math.exp(LOG_DECAY[h] * (j + 1.0)))
    kdec = tab(lambda h, j: math.exp(LOG_DECAY[h] * (dec_seq - 1.0 - j)))
    intra = tab(lambda h, j: math.exp(LOG_DECAY[h] * j))
    sdec = jnp.asarray([math.exp(LOG_DECAY[h] * dec_seq) for h in range(N_RET_HEADS)], F32)
    smem = pl.BlockSpec(memory_space=pltpu.SMEM)
    col = lambda base: pl.BlockSpec((m, LANES), lambda p: (0, base + p))
    st_spec = pl.BlockSpec((2, HEAD_DIM, HEAD_DIM, nb), lambda p: (p, 0, 0, 0))
    step_rows = GROUPS_PER_STEP * DEC_ROWS
    row = lambda w: pl.BlockSpec((step_rows, w), lambda p: (p, 0))
    cache = pl.BlockSpec((GROUPS_PER_STEP * DEC_GROUP, SWA_KV_WIDTH, WINDOW), lambda p: (p, 0, 0))
    stage = pltpu.VMEM((dec_seq, LANES, nb), F32)
    return pl.pallas_call(
        functools.partial(_decode_mixers_kernel, dec_seq, nb),
        grid=(N_PAIRS,),
        in_specs=[smem, smem, smem, smem, smem,
                  col(0), col(N_PAIRS), col(2 * N_PAIRS), col(3 * N_PAIRS), st_spec,
                  row(SWA_WIDTH), row(SWA_KV_WIDTH), row(SWA_KV_WIDTH), cache, cache],
        out_specs=[pl.BlockSpec((m, LANES), lambda p: (0, p)), st_spec, row(SWA_WIDTH), cache, cache],
        out_shape=[jax.ShapeDtypeStruct((m, RET_WIDTH), F32),
                   jax.ShapeDtypeStruct(state_t.shape, F32),
                   jax.ShapeDtypeStruct((m, SWA_WIDTH), BF16),
                   jax.ShapeDtypeStruct(k_t.shape, F32),
                   jax.ShapeDtypeStruct(v_t.shape, F32)],
        scratch_shapes=[
            stage, stage, stage, stage, stage, stage,
            pltpu.VMEM((2, N_SWA_HEADS * dec_seq, WINDOW), F32),
            pltpu.VMEM((N_SWA_HEADS, DEC_ROWS, LANES), F32),
            pltpu.VMEM((N_SWA_HEADS, DEC_ROWS, LANES), F32),
            pltpu.VMEM((DEC_GROUP, SWA_KV_WIDTH, LANES), F32),
            pltpu.VMEM((DEC_GROUP, SWA_KV_WIDTH, LANES), F32),
            pltpu.VMEM((SWA_KV_WIDTH, DEC_ROWS), BF16),
            pltpu.VMEM((SWA_KV_WIDTH, DEC_ROWS), BF16),
        ],
        compiler_params=pltpu.CompilerParams(
            dimension_semantics=("arbitrary",), vmem_limit_bytes=VMEM_LIMIT),
        name="decode_mixers",
    )(qdec, kdec, sdec, intra, sinks, main, main, main, main, state_t, qn, kn, vs, k_t, v_t)


def kernel(x_prompt, x_sample, state_ret, cache_swa_k, cache_swa_v, norm_mix_gain, w_in, q_norm_gain,
           k_norm_gain, attn_sinks, w_out, norm_ffn_gain, w_up, w_down):
    batch, seq, d = x_prompt.shape
    nb, dec_seq, _ = x_sample.shape
    wb = cache_swa_k.shape[1]
    assert d == D_MODEL and seq % PROMPT_TILE == 0 and wb == WINDOW

    gain_mix = norm_mix_gain.reshape(1, D_MODEL)
    gain_ffn = norm_ffn_gain.reshape(1, D_MODEL)
    qg = q_norm_gain.reshape(1, HEAD_DIM)
    kg = k_norm_gain.reshape(1, HEAD_DIM)

    def from_key_minor(a_t):
        return jnp.transpose(a_t.reshape(a_t.shape[0], N_SWA_KV, HEAD_DIM, WINDOW), (0, 3, 1, 2))

    def to_key_minor(a):
        return jnp.transpose(a, (0, 2, 3, 1)).reshape(a.shape[0], SWA_KV_WIDTH, WINDOW)

    xs = x_sample.reshape(nb * dec_seq, D_MODEL)
    main_s, qn_s, kn_s, vs_s, w_in_bf = _in_proj_cast(xs, gain_mix, w_in, qg, kg)
    mix_ret_s, state_t, mix_swa_s, k_t, v_t = _decode_mixers(
        attn_sinks, main_s, qn_s, kn_s, vs_s, jnp.transpose(state_ret, (1, 2, 3, 0)),
        to_key_minor(cache_swa_k), to_key_minor(cache_swa_v), dec_seq)
    y_s, w_out_bf, w_up_bf, w_down_bf = _out_mlp_cast(mix_ret_s, mix_swa_s, xs, gain_ffn, w_out, w_up, w_down)

    xp = x_prompt.reshape(batch * seq, D_MODEL)
    y_p, ret_p, kwin_t, vwin_t = _prompt_layer(attn_sinks, xp, gain_mix, w_in_bf, qg, kg, w_out_bf, gain_ffn,
                                               w_up_bf, w_down_bf, batch, seq)
    y_p = y_p.reshape(batch, seq, D_MODEL)

    return (y_p, y_s.reshape(nb, dec_seq, D_MODEL), ret_p, from_key_minor(kwin_t), from_key_minor(vwin_t),
            jnp.transpose(state_t, (3, 0, 1, 2)), from_key_minor(k_t), from_key_minor(v_t))
```

```python
import functools
import math

import jax
import jax.numpy as jnp
from jax import lax
from jax.experimental import pallas as pl
from jax.experimental.pallas import tpu as pltpu

F32 = jnp.float32
BF16 = jnp.bfloat16

D_MODEL = 1024
HEAD_DIM = 64
N_RET_HEADS = 8
N_SWA_HEADS = 8
N_SWA_KV = 2
SWA_GROUP = N_SWA_HEADS // N_SWA_KV
RET_WIDTH = N_RET_HEADS * HEAD_DIM
SWA_WIDTH = N_SWA_HEADS * HEAD_DIM
SWA_KV_WIDTH = N_SWA_KV * HEAD_DIM
MAIN_WIDTH = 4 * RET_WIDTH
IN_WIDTH = MAIN_WIDTH + SWA_WIDTH + 2 * SWA_KV_WIDTH
MIX_WIDTH = RET_WIDTH + SWA_WIDTH
D_FF = 4 * D_MODEL
WINDOW = 128
RET_CHUNK = 128
EPS = 1e-6
NEG_INF = -1e30

LANES = 128
N_PAIRS = N_RET_HEADS // 2
LOG_DECAY = [math.log(1.0 - 2.0 ** (-5.0 - h)) for h in range(N_RET_HEADS)]
ALIBI_SLOPES = [2.0 ** (-8.0 * (h + 1) / N_SWA_HEADS) for h in range(N_SWA_HEADS)]
K_SCALE = HEAD_DIM ** -0.5

MXU_WIDTH = 256
PROJ_BLOCK = 2 * MXU_WIDTH
Q_BLOCK = MXU_WIDTH
PROMPT_TILE = 512
DEC_GROUP = 16
VMEM_LIMIT = 56 * 1024 * 1024
PROMPT_VMEM_LIMIT = 62 * 1024 * 1024


def _dot(a, b):
    return jnp.dot(a, b, preferred_element_type=F32)


def _dot_nt(a, b):
    return lax.dot_general(a, b, (((1,), (1,)), ((), ())), preferred_element_type=F32)


def _iota(shape, dim):
    return lax.broadcasted_iota(jnp.int32, shape, dim)


def _ones_block_diag():
    same = ((_iota((2 * LANES, LANES), 0) >> 6) & 1) == (_iota((2 * LANES, LANES), 1) >> 6)
    return jnp.where(same, 1.0, 0.0).astype(BF16)


def _head_sumsq(x, ones_bd):
    x2 = x * x
    hi = x2.astype(BF16)
    lo = (x2 - hi.astype(F32)).astype(BF16)
    return _dot(jnp.concatenate([hi, lo], axis=1), ones_bd)


def _head_rms_scale(x, ones_bd):
    return lax.rsqrt(_head_sumsq(x, ones_bd) * (1.0 / HEAD_DIM) + EPS)


def _head_rms_scales(xs, ones_bd):
    if not xs:
        return []
    parts = []
    for x in xs:
        x2 = x * x
        hi = x2.astype(BF16)
        parts.append(jnp.concatenate([hi, (x2 - hi.astype(F32)).astype(BF16)], axis=1))
    total = _dot(jnp.concatenate(parts, axis=0) if len(parts) > 1 else parts[0], ones_bd)
    out, lo = [], 0
    for x in xs:
        out.append(lax.rsqrt(total[lo:lo + x.shape[0]] * (1.0 / HEAD_DIM) + EPS))
        lo += x.shape[0]
    return out


def _first_half():
    return _iota((1, LANES), 1) < HEAD_DIM


def _pair_const(values, pair, lane_is_second):
    return jnp.where(lane_is_second, values[2 * pair + 1], values[2 * pair]).astype(F32)


def _pair_gain(gain_ref):
    g = gain_ref[...]
    return jnp.concatenate([g, g], axis=1)


def _silu(g):
    return g * (1.0 / (1.0 + jnp.exp(-g)))


def _in_proj_kernel(x_ref, gain_ref, w_ref, qg_ref, kg_ref, main_ref, qn_ref, kn_ref, vs_ref, fetch=None):
    x = x_ref[...]
    ms = jnp.mean(x * x, axis=-1, keepdims=True)
    hb = ((x * lax.rsqrt(ms + EPS)) * gain_ref[...]).astype(BF16)
    ones_bd = _ones_block_diag()

    def project(lo, width):
        if fetch is not None:
            fetch(lo, width)
        return _dot(hb, w_ref[:, lo:lo + width])

    for c in range(MAIN_WIDTH // PROJ_BLOCK):
        main_ref[:, c * PROJ_BLOCK:(c + 1) * PROJ_BLOCK] = project(c * PROJ_BLOCK, PROJ_BLOCK)
    qs = project(MAIN_WIDTH, SWA_WIDTH)
    k0 = MAIN_WIDTH + SWA_WIDTH
    ks = project(k0, SWA_KV_WIDTH)
    vs_ref[...] = project(k0 + SWA_KV_WIDTH, SWA_KV_WIDTH)
    q_cols = [qs[:, c * LANES:(c + 1) * LANES] for c in range(SWA_WIDTH // LANES)]
    scales = _head_rms_scales(q_cols + [ks], ones_bd)
    for c, xc in enumerate(q_cols):
        qn_ref[:, c * LANES:(c + 1) * LANES] = (xc * scales[c]) * _pair_gain(qg_ref)
    kn_ref[...] = (ks * scales[-1]) * _pair_gain(kg_ref)


ROW_TILE = 512


W_IN_BLOCKS = [(c * PROJ_BLOCK, PROJ_BLOCK) for c in range(MAIN_WIDTH // PROJ_BLOCK)] + [(MAIN_WIDTH, SWA_WIDTH),
                                                                    (MAIN_WIDTH + SWA_WIDTH, 2 * SWA_KV_WIDTH)]
assert sum(w for _, w in W_IN_BLOCKS) == IN_WIDTH
BLOCKS_AHEAD = 2


def _in_proj_cast_kernel(x_ref, gain_ref, w_hbm, qg_ref, kg_ref,
                         main_ref, qn_ref, kn_ref, vs_ref, w_bf_hbm,
                         w_v, stage, in_sem, out_sem):
    step = pl.program_id(0)
    block_in = [pltpu.make_async_copy(w_hbm.at[:, pl.ds(lo, width)], stage.at[:, pl.ds(lo, width)], in_sem.at[i])
                for i, (lo, width) in enumerate(W_IN_BLOCKS)]
    writeback = pltpu.make_async_copy(w_v, w_bf_hbm, out_sem.at[0])

    @pl.when(step == 0)
    def _():
        started, arrived = set(), set()

        def start_through(n):
            for i in range(min(n, len(block_in))):
                if i not in started:
                    block_in[i].start()
                    started.add(i)

        def fetch(lo, width):
            for i, (blo, bwidth) in enumerate(W_IN_BLOCKS):
                if blo <= lo < blo + bwidth and i not in arrived:
                    assert lo + width <= blo + bwidth
                    block_in[i].wait()
                    start_through(i + 1 + BLOCKS_AHEAD)
                    w_v[:, blo:blo + bwidth] = stage[:, blo:blo + bwidth].astype(BF16)
                    arrived.add(i)

        start_through(1)
        _in_proj_kernel(x_ref, gain_ref, w_v, qg_ref, kg_ref, main_ref, qn_ref, kn_ref, vs_ref, fetch=fetch)
        assert len(arrived) == len(started) == len(W_IN_BLOCKS)
        writeback.start()

    @pl.when(step > 0)
    def _():
        _in_proj_kernel(x_ref, gain_ref, w_v, qg_ref, kg_ref, main_ref, qn_ref, kn_ref, vs_ref)

    @pl.when(step == pl.num_programs(0) - 1)
    def _():
        writeback.wait()


def _in_proj_cast(x2d, gain, w_in, qg, kg):
    m = x2d.shape[0]
    tm = min(ROW_TILE, m)
    row = lambda w: pl.BlockSpec((tm, w), lambda i: (i, 0))
    full = lambda a: pl.BlockSpec(a.shape, lambda i: (0, 0), pipeline_mode=pl.Buffered(1))
    hbm = pl.BlockSpec(memory_space=pl.ANY)
    return pl.pallas_call(
        _in_proj_cast_kernel,
        grid=(m // tm,),
        in_specs=[row(D_MODEL), full(gain), hbm, full(qg), full(kg)],
        out_specs=[row(MAIN_WIDTH), row(SWA_WIDTH), row(SWA_KV_WIDTH), row(SWA_KV_WIDTH), hbm],
        out_shape=[jax.ShapeDtypeStruct((m, MAIN_WIDTH), F32),
                   jax.ShapeDtypeStruct((m, SWA_WIDTH), F32),
                   jax.ShapeDtypeStruct((m, SWA_KV_WIDTH), F32),
                   jax.ShapeDtypeStruct((m, SWA_KV_WIDTH), F32),
                   jax.ShapeDtypeStruct(w_in.shape, BF16)],
        scratch_shapes=[pltpu.VMEM(w_in.shape, BF16), pltpu.VMEM(w_in.shape, F32),
                        pltpu.SemaphoreType.DMA((len(W_IN_BLOCKS),)), pltpu.SemaphoreType.DMA((1,))],
        compiler_params=pltpu.CompilerParams(
            dimension_semantics=("arbitrary",), vmem_limit_bytes=VMEM_LIMIT),
        name="in_proj_cast",
    )(x2d, gain, w_in, qg, kg)


FF_CHUNK = 1024


def _out_mlp_kernel(mix_ret_ref, mix_swa_ref, x_ref, w_out_ref, gain_ref, w_up_ref, w_down_ref, y_ref):
    h = x_ref[...] + (_dot(mix_ret_ref[...].astype(BF16), w_out_ref[:RET_WIDTH, :])
                      + _dot(mix_swa_ref[...].astype(BF16), w_out_ref[RET_WIDTH:, :]))
    ms = jnp.mean(h * h, axis=-1, keepdims=True)
    hf = ((h * lax.rsqrt(ms + EPS)) * gain_ref[...]).astype(BF16)
    ff = None
    for c in range(D_FF // FF_CHUNK):
        u = _dot(hf, w_up_ref[:, c * FF_CHUNK:(c + 1) * FF_CHUNK])
        a = jnp.maximum(u, 0.0)
        d = _dot((a * a).astype(BF16), w_down_ref[c * FF_CHUNK:(c + 1) * FF_CHUNK, :])
        ff = d if ff is None else ff + d
    y_ref[...] = h + ff


CAST_CHUNK = 512
N_FF_CHUNKS = D_FF // CAST_CHUNK


def _out_mlp_cast_kernel(mix_ret_ref, mix_swa_ref, x_ref, gain_ref, w_out_hbm, w_up_hbm, w_down_hbm,
                         y_ref, w_out_bf_hbm, w_up_bf_hbm, w_down_bf_hbm,
                         w_out_v, w_up_v, w_down_v, stage_out, stage_up, stage_down, in_sem, out_sem):
    step = pl.program_id(0)

    def out_in(i):
        return pltpu.make_async_copy(w_out_hbm.at[pl.ds(i * CAST_CHUNK, CAST_CHUNK), :], stage_out.at[i],
                                     in_sem.at[0, i])

    def up_in(c):
        return pltpu.make_async_copy(w_up_hbm.at[:, pl.ds(c * CAST_CHUNK, CAST_CHUNK)], stage_up.at[c % 2],
                                     in_sem.at[1, c % 2])

    def down_in(c):
        return pltpu.make_async_copy(w_down_hbm.at[pl.ds(c * CAST_CHUNK, CAST_CHUNK), :], stage_down.at[c % 2],
                                     in_sem.at[2, c % 2])

    writebacks = [pltpu.make_async_copy(src, dst, out_sem.at[i]) for i, (src, dst) in enumerate(
        [(w_out_v, w_out_bf_hbm), (w_up_v, w_up_bf_hbm), (w_down_v, w_down_bf_hbm)])]

    @pl.when(step == 0)
    def _():
        n_out = D_MODEL // CAST_CHUNK
        for i in range(n_out):
            out_in(i).start()
        for i in range(n_out):
            out_in(i).wait()
            if i == 0:
                up_in(0).start()
                down_in(0).start()
            w_out_v[i * CAST_CHUNK:(i + 1) * CAST_CHUNK, :] = stage_out[i].astype(BF16)
        writebacks[0].start()
        h = x_ref[...] + (_dot(mix_ret_ref[...].astype(BF16), w_out_v[:RET_WIDTH, :])
                          + _dot(mix_swa_ref[...].astype(BF16), w_out_v[RET_WIDTH:, :]))
        ms = jnp.mean(h * h, axis=-1, keepdims=True)
        hf = ((h * lax.rsqrt(ms + EPS)) * gain_ref[...]).astype(BF16)
        ff = None
        for c in range(N_FF_CHUNKS):
            chunk = slice(c * CAST_CHUNK, (c + 1) * CAST_CHUNK)
            if c + 1 < N_FF_CHUNKS:
                up_in(c + 1).start()
                down_in(c + 1).start()
            up_in(c).wait()
            down_in(c).wait()
            w_up_v[:, chunk] = stage_up[c % 2].astype(BF16)
            w_down_v[chunk, :] = stage_down[c % 2].astype(BF16)
            a = jnp.maximum(_dot(hf, w_up_v[:, chunk]), 0.0)
            d = _dot((a * a).astype(BF16), w_down_v[chunk, :])
            ff = d if ff is None else ff + d
        writebacks[1].start()
        writebacks[2].start()
        y_ref[...] = h + ff

    @pl.when(step > 0)
    def _():
        _out_mlp_kernel(mix_ret_ref, mix_swa_ref, x_ref, w_out_v, gain_ref, w_up_v, w_down_v, y_ref)

    @pl.when(step == pl.num_programs(0) - 1)
    def _():
        for wb in writebacks:
            wb.wait()


def _out_mlp_cast(mix_ret, mix_swa, x2d, gain, w_out, w_up, w_down):
    m = x2d.shape[0]
    tm = min(ROW_TILE, m)
    assert D_MODEL % CAST_CHUNK == 0 and D_FF % CAST_CHUNK == 0
    row = lambda w: pl.BlockSpec((tm, w), lambda i: (i, 0))
    full = lambda a: pl.BlockSpec(a.shape, lambda i: (0, 0), pipeline_mode=pl.Buffered(1))
    hbm = pl.BlockSpec(memory_space=pl.ANY)
    return pl.pallas_call(
        _out_mlp_cast_kernel,
        grid=(m // tm,),
        in_specs=[row(RET_WIDTH), row(SWA_WIDTH), row(D_MODEL), full(gain), hbm, hbm, hbm],
        out_specs=[row(D_MODEL), hbm, hbm, hbm],
        out_shape=[jax.ShapeDtypeStruct((m, D_MODEL), F32),
                   jax.ShapeDtypeStruct(w_out.shape, BF16),
                   jax.ShapeDtypeStruct(w_up.shape, BF16),
                   jax.ShapeDtypeStruct(w_down.shape, BF16)],
        scratch_shapes=[
            pltpu.VMEM(w_out.shape, BF16), pltpu.VMEM(w_up.shape, BF16), pltpu.VMEM(w_down.shape, BF16),
            pltpu.VMEM((D_MODEL // CAST_CHUNK, CAST_CHUNK, D_MODEL), F32),
            pltpu.VMEM((2, D_MODEL, CAST_CHUNK), F32),
            pltpu.VMEM((2, CAST_CHUNK, D_MODEL), F32),
            pltpu.SemaphoreType.DMA((3, 2)),
            pltpu.SemaphoreType.DMA((3,)),
        ],
        compiler_params=pltpu.CompilerParams(
            dimension_semantics=("arbitrary",), vmem_limit_bytes=VMEM_LIMIT),
        name="out_mlp_cast",
    )(mix_ret, mix_swa, x2d, gain, w_out, w_up, w_down)


def _split_pair_rows(x, first):
    return jnp.concatenate([jnp.where(first, x, 0.0), jnp.where(first, 0.0, x)], axis=0).astype(BF16)


def _softmax_sink_pv(s, sink_wide, v_t):
    m = jnp.maximum(jnp.max(s, axis=-1, keepdims=True), sink_wide)
    p = jnp.exp(s - jnp.concatenate([m, m], axis=1))
    denom = jnp.sum(p, axis=-1, keepdims=True) + jnp.exp(sink_wide - m)
    return _dot_nt(p.astype(BF16), v_t) / denom


def _softmax_sink_pv_t(s_t, sink_lanes, v_t):
    m = jnp.maximum(jnp.max(s_t, axis=0, keepdims=True), sink_lanes)
    p = jnp.exp(s_t - m)
    denom = jnp.sum(p, axis=0, keepdims=True) + jnp.exp(sink_lanes - m)
    return _dot(v_t, p.astype(BF16)) / denom


def _prompt_consts(intra_ref, qdec_ref, kdec_ref, sdec_ref, bias_ref):
    r = _iota((LANES, LANES), 0)
    lane2 = _iota((LANES, LANES), 1) >= HEAD_DIM
    rf = r.astype(F32)
    ri = _iota((LANES, 2 * LANES), 0)
    ci = _iota((LANES, 2 * LANES), 1)
    diff = (ri - (ci & (LANES - 1))).astype(F32)
    for p in range(N_PAIRS):
        lg = _pair_const(LOG_DECAY, p, lane2)
        qdec_ref[p] = jnp.exp(lg * (rf + 1.0))
        kdec_ref[p] = jnp.exp(lg * (RET_CHUNK - 1.0 - rf))
        sdec_ref[p] = jnp.exp(_pair_const(LOG_DECAY, p, r >= HEAD_DIM) * float(RET_CHUNK))
        lg2 = _pair_const(LOG_DECAY, p, ci >= LANES)
        intra_ref[p] = jnp.where(diff >= 0.0, jnp.exp(lg2 * jnp.maximum(diff, 0.0)), 0.0)
    cols = SWA_GROUP * WINDOW
    kb = _iota((2 * WINDOW, cols), 0)
    cb = _iota((2 * WINDOW, cols), 1)
    grp = cb >> 7
    dist = WINDOW + (cb & (WINDOW - 1)) - kb
    valid = (dist >= 0) & (dist < WINDOW)
    distf = dist.astype(F32)
    for j in range(N_SWA_KV):
        sl = [ALIBI_SLOPES[SWA_GROUP * j + g] for g in range(SWA_GROUP)]
        slope = jnp.where(grp == 0, sl[0], jnp.where(grp == 1, sl[1], jnp.where(grp == 2, sl[2], sl[3])))
        b = jnp.where(valid, -(slope.astype(F32) * distf), NEG_INF)
        bias_ref[0, j] = b
        bias_ref[1, j] = jnp.where(kb >= WINDOW, b, NEG_INF)


def _prompt_layer_kernel(sinks_ref, x_ref, x_next_ref, gain_mix_ref, w_in_ref, qg_ref, kg_ref,
                         w_out_hbm, gain_ffn_ref, w_up_hbm, w_down_hbm,
                         y_ref, ret_ref, kwin_ref, vwin_ref,
                         main_ref, qn_ref, kn_ref, vs_ref, mix_ref, hb_ref,
                         state_ref, prevk_ref, prevv_ref,
                         intra_ref, qdec_ref, kdec_ref, sdec_ref, bias_ref,
                         w_out_ref, w_up_ref, w_down_ref, w_sem):
    t = pl.program_id(1)
    step = pl.program_id(0) * pl.num_programs(1) + t
    cur = step % 2
    nxt = 1 - cur

    late_weights = [pltpu.make_async_copy(src, dst, w_sem.at[i]) for i, (src, dst) in enumerate(
        [(w_out_hbm, w_out_ref), (w_up_hbm, w_up_ref), (w_down_hbm, w_down_ref)])]

    @pl.when(step == 0)
    def _():
        for copy in late_weights:
            copy.start()
        _prompt_consts(intra_ref, qdec_ref, kdec_ref, sdec_ref, bias_ref)
        _in_proj_kernel(x_ref, gain_mix_ref, w_in_ref, qg_ref, kg_ref,
                        main_ref.at[0], qn_ref.at[0], kn_ref.at[0], vs_ref.at[0])

    @pl.when(t == 0)
    def _():
        state_ref[...] = jnp.zeros_like(state_ref)
        prevk_ref[...] = jnp.zeros_like(prevk_ref)
        prevv_ref[...] = jnp.zeros_like(prevv_ref)

    first = _first_half()
    ones_bd = _ones_block_diag()
    bd_mask = (_iota((LANES, LANES), 0) >= HEAD_DIM) == (_iota((LANES, LANES), 1) >= HEAD_DIM)

    n_chunks = PROMPT_TILE // RET_CHUNK
    assert n_chunks == 4
    plan = [dict(a=[], b=[]),
            dict(a=["m0", "m1"], b=[]),
            dict(a=["m2", "m3"], b=[]),
            dict(a=["v", "k"], b=["q0", "q1"])]
    a0 = MAIN_WIDTH
    attn_cols = {"v": (a0 + SWA_WIDTH + SWA_KV_WIDTH, SWA_KV_WIDTH), "k": (a0 + SWA_WIDTH, SWA_KV_WIDTH)}
    attn_cols.update({"q%d" % i: (a0 + Q_BLOCK * i, Q_BLOCK) for i in range(SWA_WIDTH // Q_BLOCK)})
    cols = lambda base, p: slice(base + p * LANES, base + (p + 1) * LANES)

    def project(items):
        raw = {}
        for it in items:
            if it[0] == "m":
                cb = int(it[1:])
                blk = slice(cb * PROJ_BLOCK, (cb + 1) * PROJ_BLOCK)
                main_ref[nxt, :, blk] = _dot(hb_ref[...], w_in_ref[:, blk])
            else:
                lo, width = attn_cols[it]
                raw[it] = _dot(hb_ref[...], w_in_ref[:, lo:lo + width])
        return raw

    def attn_scales(raw):
        return {name: [_head_rms_scale(val[:, cols(0, cq)], ones_bd) for cq in range(val.shape[1] // LANES)]
                for name, val in raw.items() if name != "v"}

    def store_attn(raw, scales):
        for name, val in raw.items():
            if name == "v":
                vs_ref[nxt] = val
            elif name == "k":
                kn_ref[nxt] = (val * scales[name][0]) * _pair_gain(kg_ref)
            else:
                base = attn_cols[name][0] - a0
                for cq in range(val.shape[1] // LANES):
                    qn_ref[nxt, :, cols(base, cq)] = (val[:, cols(0, cq)] * scales[name][cq]) * _pair_gain(qg_ref)

    pairs = range(N_PAIRS)
    kvs = range(N_SWA_KV)

    def stage1(c):
        rows = slice(c * RET_CHUNK, (c + 1) * RET_CHUNK)
        q = [main_ref[cur, rows, cols(0, p)] for p in pairs]
        k = [main_ref[cur, rows, cols(RET_WIDTH, p)] * K_SCALE for p in pairs]
        v = [main_ref[cur, rows, cols(2 * RET_WIDTH, p)] for p in pairs]
        state = [state_ref[p] for p in pairs]
        kc = kn_ref[cur, rows, :]
        k_sw = pltpu.roll(kc, HEAD_DIM, axis=1)
        v_t = vs_ref[cur, rows, :].T
        is_first = (t == 0).astype(jnp.int32) if c == 0 else 0
        k_dup =[(jnp.where(first, kc, k_sw) if j == 0 else jnp.where(first, k_sw, kc)).astype(BF16)
                 for j in kvs]
        v_tj = [v_t[j * HEAD_DIM:(j + 1) * HEAD_DIM].astype(BF16) for j in kvs]
        q_st = []
        for j in kvs:
            pieces = []
            for g in range(SWA_GROUP):
                qc = qn_ref[cur, rows, cols(0, 2 * j + g // 2)]
                pieces.append(jnp.where(first, qc, 0.0) if g % 2 == 0 else jnp.where(first, 0.0, qc))
            q_st.append(jnp.concatenate(pieces, axis=0).astype(BF16))

        s = [_dot_nt(q[p].astype(BF16), _split_pair_rows(k[p], first)) for p in pairs]
        s_t = [_dot_nt(jnp.concatenate([prevk_ref[j], k_dup[j]], axis=0), q_st[j]) for j in kvs]
        cross = [_dot((q[p] * qdec_ref[p]).astype(BF16), state[p].astype(BF16)) for p in pairs]
        upd = [_dot((k[p] * kdec_ref[p]).T.astype(BF16), v[p].astype(BF16)) for p in pairs]
        return dict(rows=rows, v=v, state=state, k_dup=k_dup, v_tj=v_tj, is_first=is_first,
                    s=s, s_t=s_t, cross=cross, upd=upd)

    def stage2(st):
        st["o"] = [_dot((st["s"][p] * intra_ref[p]).astype(BF16), _split_pair_rows(st["v"][p], first))
                   + st["cross"][p] for p in pairs]
        st["o_t"] = []
        for j in kvs:
            sink_lanes = jnp.concatenate(
                [jnp.full((1, WINDOW), sinks_ref[SWA_GROUP * j + g], F32) for g in range(SWA_GROUP)],
                axis=1)
            v_cat = jnp.concatenate([prevv_ref[j], st["v_tj"][j]], axis=1)
            st["o_t"].append(_softmax_sink_pv_t(st["s_t"][j] * K_SCALE + bias_ref[st["is_first"], j],
                                                sink_lanes, v_cat))
        for p in pairs:
            state_ref[p] = st["state"][p] * sdec_ref[p] + jnp.where(bd_mask, st["upd"][p], 0.0)
        for j in kvs:
            prevk_ref[j] = st["k_dup"][j]
            prevv_ref[j] = st["v_tj"][j]

    def stage3(st, raw):
        rows = st["rows"]
        scale = [_head_rms_scale(st["o"][p], ones_bd) for p in pairs]
        raw_scales = attn_scales(raw)
        for p in pairs:
            g = main_ref[cur, rows, cols(3 * RET_WIDTH, p)]
            mix_ref[rows, cols(0, p)] = (st["o"][p] * scale[p] * _silu(g)).astype(BF16)
        store_attn(raw, raw_scales)
        for j in kvs:
            o_t = st["o_t"][j]
            for half in range(2):
                pair_t = jnp.concatenate([o_t[:, (2 * half) * WINDOW:(2 * half + 1) * WINDOW],
                                          o_t[:, (2 * half + 1) * WINDOW:(2 * half + 2) * WINDOW]], axis=0)
                mix_ref[rows, cols(RET_WIDTH, 2 * j + half)] = pair_t.T.astype(BF16)

    xn = x_next_ref[...]
    hb_ref[...] = ((xn * lax.rsqrt(jnp.mean(xn * xn, axis=-1, keepdims=True) + EPS)) * gain_mix_ref[...]).astype(BF16)
    for c in range(n_chunks):
        st = stage1(c)
        raw = project(plan[c]["a"])
        stage2(st)
        raw.update(project(plan[c]["b"]))
        stage3(st, raw)

    @pl.when(t == pl.num_programs(1) - 1)
    def _():
        for p in range(N_PAIRS):
            s = state_ref[p]
            ret_ref[2 * p] = s[:HEAD_DIM, :HEAD_DIM]
            ret_ref[2 * p + 1] = s[HEAD_DIM:, HEAD_DIM:]
        last = slice(PROMPT_TILE - WINDOW, PROMPT_TILE)
        kwin_ref[...] = kn_ref[cur, last, :].T
        vwin_ref[...] = vs_ref[cur, last, :].T

    @pl.when(step == 0)
    def _():
        for copy in late_weights:
            copy.wait()

    _out_mlp_kernel(mix_ref.at[:, pl.ds(0, RET_WIDTH)], mix_ref.at[:, pl.ds(RET_WIDTH, SWA_WIDTH)], x_ref,
                    w_out_ref, gain_ffn_ref, w_up_ref, w_down_ref, y_ref)


def _prompt_layer(sinks, x2d, gain_mix, w_in_bf, qg, kg, w_out_bf, gain_ffn, w_up_bf, w_down_bf, batch, seq):
    nt = seq // PROMPT_TILE
    last_tile = batch * nt - 1
    row = lambda w: pl.BlockSpec((PROMPT_TILE, w), lambda b, t: (b * nt + t, 0))
    next_row = pl.BlockSpec((PROMPT_TILE, D_MODEL), lambda b, t: (jnp.minimum(b * nt + t + 1, last_tile), 0))
    full = lambda a: pl.BlockSpec(a.shape, lambda b, t: (0, 0), pipeline_mode=pl.Buffered(1))
    in_hbm = pl.BlockSpec(memory_space=pl.ANY)
    return pl.pallas_call(
        _prompt_layer_kernel,
        grid=(batch, nt),
        in_specs=[pl.BlockSpec(memory_space=pltpu.SMEM), row(D_MODEL), next_row,
                  full(gain_mix), full(w_in_bf), full(qg), full(kg),
                  in_hbm, full(gain_ffn), in_hbm, in_hbm],
        out_specs=[row(D_MODEL),
                   pl.BlockSpec((None, N_RET_HEADS, HEAD_DIM, HEAD_DIM), lambda b, t: (b, 0, 0, 0)),
                   pl.BlockSpec((None, SWA_KV_WIDTH, WINDOW), lambda b, t: (b, 0, 0)),
                   pl.BlockSpec((None, SWA_KV_WIDTH, WINDOW), lambda b, t: (b, 0, 0))],
        out_shape=[jax.ShapeDtypeStruct((batch * seq, D_MODEL), F32),
                   jax.ShapeDtypeStruct((batch, N_RET_HEADS, HEAD_DIM, HEAD_DIM), F32),
                   jax.ShapeDtypeStruct((batch, SWA_KV_WIDTH, WINDOW), F32),
                   jax.ShapeDtypeStruct((batch, SWA_KV_WIDTH, WINDOW), F32)],
        scratch_shapes=[
            pltpu.VMEM((2, PROMPT_TILE, MAIN_WIDTH), F32),
            pltpu.VMEM((2, PROMPT_TILE, SWA_WIDTH), F32),
            pltpu.VMEM((2, PROMPT_TILE, SWA_KV_WIDTH), F32),
            pltpu.VMEM((2, PROMPT_TILE, SWA_KV_WIDTH), F32),
            pltpu.VMEM((PROMPT_TILE, MIX_WIDTH), BF16),
            pltpu.VMEM((PROMPT_TILE, D_MODEL), BF16),
            pltpu.VMEM((N_PAIRS, LANES, LANES), F32),
            pltpu.VMEM((N_SWA_KV, WINDOW, LANES), BF16),
            pltpu.VMEM((N_SWA_KV, HEAD_DIM, WINDOW), BF16),
            pltpu.VMEM((N_PAIRS, LANES, 2 * LANES), F32),
            pltpu.VMEM((N_PAIRS, LANES, LANES), F32),
            pltpu.VMEM((N_PAIRS, LANES, LANES), F32),
            pltpu.VMEM((N_PAIRS, LANES, LANES), F32),
            pltpu.VMEM((2, N_SWA_KV, 2 * WINDOW, SWA_GROUP * WINDOW), F32),
            pltpu.VMEM(w_out_bf.shape, BF16),
            pltpu.VMEM(w_up_bf.shape, BF16),
            pltpu.VMEM(w_down_bf.shape, BF16),
            pltpu.SemaphoreType.DMA((3,)),
        ],
        compiler_params=pltpu.CompilerParams(
            dimension_semantics=("arbitrary", "arbitrary"), vmem_limit_bytes=PROMPT_VMEM_LIMIT),
        name="prompt_layer",
    )(sinks, x2d, x2d, gain_mix, w_in_bf, qg, kg, w_out_bf, gain_ffn, w_up_bf, w_down_bf)


DEC_ROWS = 128
DEC_UNROLL = 8


def _decode_attn_consts(dec_seq, bias_ref):
    shift = dec_seq.bit_length() - 1
    rows = N_SWA_HEADS * dec_seq
    rb = _iota((rows, WINDOW), 0)
    cb = _iota((rows, WINDOW), 1)
    head = rb >> shift
    i = rb & (dec_seq - 1)
    slope = jnp.zeros((rows, WINDOW), F32)
    for h in range(N_SWA_HEADS):
        slope = jnp.where(head == h, ALIBI_SLOPES[h], slope)
    bias_ref[0] = jnp.where(cb > i, -(slope * (WINDOW + i - cb).astype(F32)), NEG_INF)
    m = cb & (dec_seq - 1)
    bias_ref[1] = jnp.where(m <= i, -(slope * (i - m).astype(F32)), NEG_INF)


GROUPS_PER_STEP = 2


def _decode_mixers_kernel(dec_seq, nb, qdec_ref, kdec_ref, sdec_ref, intra_ref, sinks_ref,
                          q_ref, k_ref, v_ref, g_ref, st_ref, qn_ref, kn_ref, vs_ref, kt_ref, vt_ref,
                          mix_ret_ref, st_out_ref, mix_swa_ref, kt_out_ref, vt_out_ref,
                          qt_s, kt_s, vt_s, qdt_s, kdt_s, o_s,
                          bias_ref, qbd_ref, oblk_ref, knew_ref, vnew_ref, knt_ref, vst_ref):
    pair = pl.program_id(0)
    halves = [slice(0, HEAD_DIM), slice(HEAD_DIM, 2 * HEAD_DIM)]
    first = _first_half()
    shift = dec_seq.bit_length() - 1
    e_blk = HEAD_DIM // 2

    @pl.when(pair == 0)
    def _():
        _decode_attn_consts(dec_seq, bias_ref)

    def ret_stage():
        for l in range(dec_seq):
            rows = pl.ds(l, nb, stride=dec_seq)
            q_t = q_ref[rows, :].T
            k_t = (k_ref[rows, :] * K_SCALE).T
            qt_s[l] = q_t
            kt_s[l] = k_t
            vt_s[l] = v_ref[rows, :].T
            for hh in range(2):
                qdt_s[l, halves[hh], :] = q_t[halves[hh]] * qdec_ref[2 * pair + hh, l]
                kdt_s[l, halves[hh], :] = k_t[halves[hh]] * kdec_ref[2 * pair + hh, l]

    def ret_intra(hh):
        h = 2 * pair + hh
        hs = halves[hh]
        for l in range(dec_seq):
            acc = None
            for m in range(l + 1):
                sc = jnp.sum(qt_s[l, hs, :] * kt_s[m, hs, :], axis=0, keepdims=True) * intra_ref[h, l - m]
                term = sc * vt_s[m, hs, :]
                acc = term if acc is None else acc + term
            o_s[l, hs, :] = acc

    def ret_block(hh, eb, d_lo, d_hi, accs=None):
        h = 2 * pair + hh
        es = slice(eb * e_blk, (eb + 1) * e_blk)
        erows = slice(hh * HEAD_DIM + eb * e_blk, hh * HEAD_DIM + (eb + 1) * e_blk)
        if accs is None:
            accs = [jnp.zeros((e_blk, nb), F32) for _ in range(dec_seq)]
        for d in range(d_lo, d_hi):
            s_d = st_ref[hh, d, es, :]
            row = slice(hh * HEAD_DIM + d, hh * HEAD_DIM + d + 1)
            upd = s_d * sdec_ref[h]
            for l in range(dec_seq):
                accs[l] = accs[l] + qdt_s[l, row, :] * s_d
                upd = upd + kdt_s[l, row, :] * vt_s[l, erows, :]
            st_out_ref[hh, d, es, :] = upd
        if d_hi == HEAD_DIM:
            for l in range(dec_seq):
                o_s[l, erows, :] = o_s[l, erows, :] + accs[l]
        return accs

    def ret_finish():
        for l in range(dec_seq):
            o = o_s[l]
            normed = []
            for hh in range(2):
                oh = o[halves[hh]]
                normed.append(oh * lax.rsqrt(jnp.mean(oh * oh, axis=0, keepdims=True) + EPS))
            rows = pl.ds(l, nb, stride=dec_seq)
            mix_ret_ref[rows, :] = jnp.concatenate(normed, axis=0).T * _silu(g_ref[rows, :])

    sink_rows = jnp.concatenate(
        [jnp.full((dec_seq, LANES), sinks_ref[h], F32) for h in range(N_SWA_HEADS)], axis=0)
    col_batch = _iota((N_SWA_HEADS * dec_seq, LANES), 1) >> shift
    keep_old = _iota((1, LANES), 1) < WINDOW - dec_seq

    def attn_stage(gi):
        grows = slice(gi * DEC_ROWS, (gi + 1) * DEC_ROWS)
        kn_t = kn_ref[grows, :].T
        vs_t = vs_ref[grows, :].T
        knt_ref[...] = kn_t.astype(BF16)
        vst_ref[...] = vs_t.astype(BF16)
        for bb in range(DEC_GROUP):
            sh = (WINDOW - dec_seq - bb * dec_seq) % LANES
            knew_ref[bb] = pltpu.roll(kn_t, sh, axis=1) if sh else kn_t
            vnew_ref[bb] = pltpu.roll(vs_t, sh, axis=1) if sh else vs_t
        qn = qn_ref[grows, :]
        qn_sw = pltpu.roll(qn, HEAD_DIM, axis=1)
        for h in range(N_SWA_HEADS):
            kv_half = h // SWA_GROUP
            if (h % 2) == kv_half:
                src = qn[:, (h // 2) * LANES:(h // 2 + 1) * LANES]
            else:
                col = (h + 1) // 2
                src = qn_sw[:, col * LANES:(col + 1) * LANES]
            qbd_ref[h] = jnp.where(first, src, 0.0) if kv_half == 0 else jnp.where(first, 0.0, src)

    def attn_block(gi, i):
        bs = [i * DEC_UNROLL + u for u in range(DEC_UNROLL)]
        rows = [slice(b * dec_seq, (b + 1) * dec_seq) for b in bs]
        k_old = [kt_ref[gi * DEC_GROUP + b] for b in bs]
        v_old = [vt_ref[gi * DEC_GROUP + b] for b in bs]
        q_st = [jnp.concatenate([qbd_ref[h, r, :] for h in range(N_SWA_HEADS)], axis=0).astype(BF16)
                for r in rows]
        s = [_dot(q_st[u], jnp.concatenate([k_old[u].astype(BF16), knt_ref[...]], axis=1))
             for u in range(DEC_UNROLL)]
        o = []
        for u, b in enumerate(bs):
            bias = jnp.concatenate([bias_ref[0], jnp.where(col_batch == b, bias_ref[1], NEG_INF)], axis=1)
            w_v = jnp.concatenate([v_old[u].astype(BF16), vst_ref[...]], axis=1)
            o.append(_softmax_sink_pv(s[u] * K_SCALE + bias, sink_rows, w_v))
        for u, b in enumerate(bs):
            for h in range(N_SWA_HEADS):
                oblk_ref[h, rows[u], :] = o[u][h * dec_seq:(h + 1) * dec_seq]
            kt_out_ref[gi * DEC_GROUP + b] = jnp.where(
                keep_old, pltpu.roll(k_old[u], LANES - dec_seq, axis=1), knew_ref[b])
            vt_out_ref[gi * DEC_GROUP + b] = jnp.where(
                keep_old, pltpu.roll(v_old[u], LANES - dec_seq, axis=1), vnew_ref[b])

    def attn_finish(gi):
        grows = slice(gi * DEC_ROWS, (gi + 1) * DEC_ROWS)
        y1 = jnp.where(first, oblk_ref[3], oblk_ref[4])
        moved = pltpu.roll(jnp.concatenate([oblk_ref[1], y1, oblk_ref[6], oblk_ref[6]], axis=1), HEAD_DIM, axis=1)
        outs = [
            jnp.where(first, oblk_ref[0], moved[:, 0:LANES]),
            jnp.where(first, oblk_ref[2], moved[:, LANES:2 * LANES]),
            jnp.where(first, moved[:, 2 * LANES:3 * LANES], oblk_ref[5]),
            jnp.where(first, moved[:, 3 * LANES:4 * LANES], oblk_ref[7]),
        ]
        for c in range(SWA_WIDTH // LANES):
            mix_swa_ref[grows, c * LANES:(c + 1) * LANES] = outs[c].astype(BF16)

    ret_stage()
    for gi in range(GROUPS_PER_STEP):
        attn_stage(gi)
        ret_intra(gi)
        n_blocks = DEC_GROUP // DEC_UNROLL
        per_eb = n_blocks // 2
        d_step = HEAD_DIM // per_eb
        for eb in range(2):
            accs = None
            for part in range(per_eb):
                accs = ret_block(gi, eb, part * d_step, (part + 1) * d_step, accs)
                attn_block(gi, eb * per_eb + part)
        attn_finish(gi)
    ret_finish()


def _decode_mixers(sinks, main, qn, kn, vs, state_t, k_t, v_t, dec_seq):
    nb = state_t.shape[-1]
    m = main.shape[0]
    assert nb == LANES and m == nb * dec_seq and k_t.shape == (nb, SWA_KV_WIDTH, WINDOW)
    assert DEC_GROUP * dec_seq == DEC_ROWS and nb == N_PAIRS * GROUPS_PER_STEP * DEC_GROUP
    assert (DEC_GROUP // DEC_UNROLL) % 2 == 0 and dec_seq & (dec_seq - 1) == 0
    tab = lambda f: jnp.asarray([[f(h, j) for j in range(dec_seq)] for h in range(N_RET_HEADS)], F32)
    qdec = tab(lambda h, j: math.exp(LOG_DECAY[h] * (j + 1.0)))
    kdec = tab(lambda h, j: math.exp(LOG_DECAY[h] * (dec_seq - 1.0 - j)))
    intra = tab(lambda h, j: math.exp(LOG_DECAY[h] * j))
    sdec = jnp.asarray([math.exp(LOG_DECAY[h] * dec_seq) for h in range(N_RET_HEADS)], F32)
    smem = pl.BlockSpec(memory_space=pltpu.SMEM)
    col = lambda base: pl.BlockSpec((m, LANES), lambda p: (0, base + p))
    st_spec = pl.BlockSpec((2, HEAD_DIM, HEAD_DIM, nb), lambda p: (p, 0, 0, 0))
    step_rows = GROUPS_PER_STEP * DEC_ROWS
    row = lambda w: pl.BlockSpec((step_rows, w), lambda p: (p, 0))
    cache = pl.BlockSpec((GROUPS_PER_STEP * DEC_GROUP, SWA_KV_WIDTH, WINDOW), lambda p: (p, 0, 0))
    stage = pltpu.VMEM((dec_seq, LANES, nb), F32)
    return pl.pallas_call(
        functools.partial(_decode_mixers_kernel, dec_seq, nb),
        grid=(N_PAIRS,),
        in_specs=[smem, smem, smem, smem, smem,
                  col(0), col(N_PAIRS), col(2 * N_PAIRS), col(3 * N_PAIRS), st_spec,
                  row(SWA_WIDTH), row(SWA_KV_WIDTH), row(SWA_KV_WIDTH), cache, cache],
        out_specs=[pl.BlockSpec((m, LANES), lambda p: (0, p)), st_spec, row(SWA_WIDTH), cache, cache],
        out_shape=[jax.ShapeDtypeStruct((m, RET_WIDTH), F32),
                   jax.ShapeDtypeStruct(state_t.shape, F32),
                   jax.ShapeDtypeStruct((m, SWA_WIDTH), BF16),
                   jax.ShapeDtypeStruct(k_t.shape, F32),
                   jax.ShapeDtypeStruct(v_t.shape, F32)],
        scratch_shapes=[
            stage, stage, stage, stage, stage, stage,
            pltpu.VMEM((2, N_SWA_HEADS * dec_seq, WINDOW), F32),
            pltpu.VMEM((N_SWA_HEADS, DEC_ROWS, LANES), F32),
            pltpu.VMEM((N_SWA_HEADS, DEC_ROWS, LANES), F32),
            pltpu.VMEM((DEC_GROUP, SWA_KV_WIDTH, LANES), F32),
            pltpu.VMEM((DEC_GROUP, SWA_KV_WIDTH, LANES), F32),
            pltpu.VMEM((SWA_KV_WIDTH, DEC_ROWS), BF16),
            pltpu.VMEM((SWA_KV_WIDTH, DEC_ROWS), BF16),
        ],
        compiler_params=pltpu.CompilerParams(
            dimension_semantics=("arbitrary",), vmem_limit_bytes=VMEM_LIMIT),
        name="decode_mixers",
    )(qdec, kdec, sdec, intra, sinks, main, main, main, main, state_t, qn, kn, vs, k_t, v_t)


def kernel(x_prompt, x_sample, state_ret, cache_swa_k, cache_swa_v, norm_mix_gain, w_in, q_norm_gain,
           k_norm_gain, attn_sinks, w_out, norm_ffn_gain, w_up, w_down):
    batch, seq, d = x_prompt.shape
    nb, dec_seq, _ = x_sample.shape
    wb = cache_swa_k.shape[1]
    assert d == D_MODEL and seq % PROMPT_TILE == 0 and wb == WINDOW

    gain_mix = norm_mix_gain.reshape(1, D_MODEL)
    gain_ffn = norm_ffn_gain.reshape(1, D_MODEL)
    qg = q_norm_gain.reshape(1, HEAD_DIM)
    kg = k_norm_gain.reshape(1, HEAD_DIM)

    def from_key_minor(a_t):
        return jnp.transpose(a_t.reshape(a_t.shape[0], N_SWA_KV, HEAD_DIM, WINDOW), (0, 3, 1, 2))

    def to_key_minor(a):
        return jnp.transpose(a, (0, 2, 3, 1)).reshape(a.shape[0], SWA_KV_WIDTH, WINDOW)

    xs = x_sample.reshape(nb * dec_seq, D_MODEL)
    main_s, qn_s, kn_s, vs_s, w_in_bf = _in_proj_cast(xs, gain_mix, w_in, qg, kg)
    mix_ret_s, state_t, mix_swa_s, k_t, v_t = _decode_mixers(
        attn_sinks, main_s, qn_s, kn_s, vs_s, jnp.transpose(state_ret, (1, 2, 3, 0)),
        to_key_minor(cache_swa_k), to_key_minor(cache_swa_v), dec_seq)
    y_s, w_out_bf, w_up_bf, w_down_bf = _out_mlp_cast(mix_ret_s, mix_swa_s, xs, gain_ffn, w_out, w_up, w_down)

    xp = x_prompt.reshape(batch * seq, D_MODEL)
    y_p, ret_p, kwin_t, vwin_t = _prompt_layer(attn_sinks, xp, gain_mix, w_in_bf, qg, kg, w_out_bf, gain_ffn,
                                               w_up_bf, w_down_bf, batch, seq)
    y_p = y_p.reshape(batch, seq, D_MODEL)

    return (y_p, y_s.reshape(nb, dec_seq, D_MODEL), ret_p, from_key_minor(kwin_t), from_key_minor(vwin_t),
            jnp.transpose(state_t, (3, 0, 1, 2)), from_key_minor(k_t), from_key_minor(v_t))
```

```python
import functools
import math

import jax
import jax.numpy as jnp
from jax import lax
from jax.experimental import pallas as pl
from jax.experimental.pallas import tpu as pltpu

F32 = jnp.float32
BF16 = jnp.bfloat16

D_MODEL = 1024
HEAD_DIM = 64
N_RET_HEADS = 8
N_SWA_HEADS = 8
N_SWA_KV = 2
SWA_GROUP = N_SWA_HEADS // N_SWA_KV
RET_WIDTH = N_RET_HEADS * HEAD_DIM
SWA_WIDTH = N_SWA_HEADS * HEAD_DIM
SWA_KV_WIDTH = N_SWA_KV * HEAD_DIM
MAIN_WIDTH = 4 * RET_WIDTH
IN_WIDTH = MAIN_WIDTH + SWA_WIDTH + 2 * SWA_KV_WIDTH
MIX_WIDTH = RET_WIDTH + SWA_WIDTH
D_FF = 4 * D_MODEL
WINDOW = 128
RET_CHUNK = 128
EPS = 1e-6
NEG_INF = -1e30

LANES = 128
N_PAIRS = N_RET_HEADS // 2
LOG_DECAY = [math.log(1.0 - 2.0 ** (-5.0 - h)) for h in range(N_RET_HEADS)]
ALIBI_SLOPES = [2.0 ** (-8.0 * (h + 1) / N_SWA_HEADS) for h in range(N_SWA_HEADS)]
K_SCALE = HEAD_DIM ** -0.5

MXU_WIDTH = 256
PROJ_BLOCK = 2 * MXU_WIDTH
Q_BLOCK = MXU_WIDTH
PROMPT_TILE = 512
DEC_GROUP = 16
VMEM_LIMIT = 56 * 1024 * 1024
PROMPT_VMEM_LIMIT = 62 * 1024 * 1024


def _dot(a, b):
    return jnp.dot(a, b, preferred_element_type=F32)


def _dot_nt(a, b):
    return lax.dot_general(a, b, (((1,), (1,)), ((), ())), preferred_element_type=F32)


def _iota(shape, dim):
    return lax.broadcasted_iota(jnp.int32, shape, dim)


def _ones_block_diag():
    same = ((_iota((2 * LANES, LANES), 0) >> 6) & 1) == (_iota((2 * LANES, LANES), 1) >> 6)
    return jnp.where(same, 1.0, 0.0).astype(BF16)


def _head_sumsq(x, ones_bd):
    x2 = x * x
    hi = x2.astype(BF16)
    lo = (x2 - hi.astype(F32)).astype(BF16)
    return _dot(jnp.concatenate([hi, lo], axis=1), ones_bd)


def _head_rms_scale(x, ones_bd):
    return lax.rsqrt(_head_sumsq(x, ones_bd) * (1.0 / HEAD_DIM) + EPS)


def _head_rms_scales(xs, ones_bd):
    if not xs:
        return []
    parts = []
    for x in xs:
        x2 = x * x
        hi = x2.astype(BF16)
        parts.append(jnp.concatenate([hi, (x2 - hi.astype(F32)).astype(BF16)], axis=1))
    total = _dot(jnp.concatenate(parts, axis=0) if len(parts) > 1 else parts[0], ones_bd)
    out, lo = [], 0
    for x in xs:
        out.append(lax.rsqrt(total[lo:lo + x.shape[0]] * (1.0 / HEAD_DIM) + EPS))
        lo += x.shape[0]
    return out


def _first_half():
    return _iota((1, LANES), 1) < HEAD_DIM


def _pair_const(values, pair, lane_is_second):
    return jnp.where(lane_is_second, values[2 * pair + 1], values[2 * pair]).astype(F32)


def _pair_gain(gain_ref):
    g = gain_ref[...]
    return jnp.concatenate([g, g], axis=1)


def _silu(g):
    return g * (1.0 / (1.0 + jnp.exp(-g)))


def _in_proj_kernel(x_ref, gain_ref, w_ref, qg_ref, kg_ref, main_ref, qn_ref, kn_ref, vs_ref, fetch=None):
    x = x_ref[...]
    ms = jnp.mean(x * x, axis=-1, keepdims=True)
    hb = ((x * lax.rsqrt(ms + EPS)) * gain_ref[...]).astype(BF16)
    ones_bd = _ones_block_diag()

    def project(lo, width):
        if fetch is not None:
            fetch(lo, width)
        return _dot(hb, w_ref[:, lo:lo + width])

    for c in range(MAIN_WIDTH // PROJ_BLOCK):
        main_ref[:, c * PROJ_BLOCK:(c + 1) * PROJ_BLOCK] = project(c * PROJ_BLOCK, PROJ_BLOCK)
    qs = project(MAIN_WIDTH, SWA_WIDTH)
    k0 = MAIN_WIDTH + SWA_WIDTH
    ks = project(k0, SWA_KV_WIDTH)
    vs_ref[...] = project(k0 + SWA_KV_WIDTH, SWA_KV_WIDTH)
    q_cols = [qs[:, c * LANES:(c + 1) * LANES] for c in range(SWA_WIDTH // LANES)]
    scales = _head_rms_scales(q_cols + [ks], ones_bd)
    for c, xc in enumerate(q_cols):
        qn_ref[:, c * LANES:(c + 1) * LANES] = (xc * scales[c]) * _pair_gain(qg_ref)
    kn_ref[...] = (ks * scales[-1]) * _pair_gain(kg_ref)


ROW_TILE = 512


W_IN_BLOCKS = [(c * PROJ_BLOCK, PROJ_BLOCK) for c in range(MAIN_WIDTH // PROJ_BLOCK)] + [(MAIN_WIDTH, SWA_WIDTH),
                                                                    (MAIN_WIDTH + SWA_WIDTH, 2 * SWA_KV_WIDTH)]
assert sum(w for _, w in W_IN_BLOCKS) == IN_WIDTH


def _in_proj_cast_kernel(x_ref, gain_ref, w_hbm, qg_ref, kg_ref,
                         main_ref, qn_ref, kn_ref, vs_ref, w_bf_hbm,
                         w_v, stage, in_sem, out_sem):
    step = pl.program_id(0)
    block_in = [pltpu.make_async_copy(w_hbm.at[:, pl.ds(lo, width)], stage.at[:, pl.ds(lo, width)], in_sem.at[i])
                for i, (lo, width) in enumerate(W_IN_BLOCKS)]
    writeback = pltpu.make_async_copy(w_v, w_bf_hbm, out_sem.at[0])

    @pl.when(step == 0)
    def _():
        for copy in block_in:
            copy.start()
        arrived = set()

        def fetch(lo, width):
            for i, (blo, bwidth) in enumerate(W_IN_BLOCKS):
                if blo <= lo < blo + bwidth and i not in arrived:
                    assert lo + width <= blo + bwidth
                    block_in[i].wait()
                    w_v[:, blo:blo + bwidth] = stage[:, blo:blo + bwidth].astype(BF16)
                    arrived.add(i)

        _in_proj_kernel(x_ref, gain_ref, w_v, qg_ref, kg_ref, main_ref, qn_ref, kn_ref, vs_ref, fetch=fetch)
        assert len(arrived) == len(W_IN_BLOCKS)
        writeback.start()

    @pl.when(step > 0)
    def _():
        _in_proj_kernel(x_ref, gain_ref, w_v, qg_ref, kg_ref, main_ref, qn_ref, kn_ref, vs_ref)

    @pl.when(step == pl.num_programs(0) - 1)
    def _():
        writeback.wait()


def _in_proj_cast(x2d, gain, w_in, qg, kg):
    m = x2d.shape[0]
    tm = min(ROW_TILE, m)
    row = lambda w: pl.BlockSpec((tm, w), lambda i: (i, 0))
    full = lambda a: pl.BlockSpec(a.shape, lambda i: (0, 0), pipeline_mode=pl.Buffered(1))
    hbm = pl.BlockSpec(memory_space=pl.ANY)
    return pl.pallas_call(
        _in_proj_cast_kernel,
        grid=(m // tm,),
        in_specs=[row(D_MODEL), full(gain), hbm, full(qg), full(kg)],
        out_specs=[row(MAIN_WIDTH), row(SWA_WIDTH), row(SWA_KV_WIDTH), row(SWA_KV_WIDTH), hbm],
        out_shape=[jax.ShapeDtypeStruct((m, MAIN_WIDTH), F32),
                   jax.ShapeDtypeStruct((m, SWA_WIDTH), F32),
                   jax.ShapeDtypeStruct((m, SWA_KV_WIDTH), F32),
                   jax.ShapeDtypeStruct((m, SWA_KV_WIDTH), F32),
                   jax.ShapeDtypeStruct(w_in.shape, BF16)],
        scratch_shapes=[pltpu.VMEM(w_in.shape, BF16), pltpu.VMEM(w_in.shape, F32),
                        pltpu.SemaphoreType.DMA((len(W_IN_BLOCKS),)), pltpu.SemaphoreType.DMA((1,))],
        compiler_params=pltpu.CompilerParams(
            dimension_semantics=("arbitrary",), vmem_limit_bytes=VMEM_LIMIT),
        name="in_proj_cast",
    )(x2d, gain, w_in, qg, kg)


FF_CHUNK = 1024


def _out_mlp_kernel(mix_ret_ref, mix_swa_ref, x_ref, w_out_ref, gain_ref, w_up_ref, w_down_ref, y_ref):
    h = x_ref[...] + (_dot(mix_ret_ref[...].astype(BF16), w_out_ref[:RET_WIDTH, :])
                      + _dot(mix_swa_ref[...].astype(BF16), w_out_ref[RET_WIDTH:, :]))
    ms = jnp.mean(h * h, axis=-1, keepdims=True)
    hf = ((h * lax.rsqrt(ms + EPS)) * gain_ref[...]).astype(BF16)
    ff = None
    for c in range(D_FF // FF_CHUNK):
        u = _dot(hf, w_up_ref[:, c * FF_CHUNK:(c + 1) * FF_CHUNK])
        a = jnp.maximum(u, 0.0)
        d = _dot((a * a).astype(BF16), w_down_ref[c * FF_CHUNK:(c + 1) * FF_CHUNK, :])
        ff = d if ff is None else ff + d
    y_ref[...] = h + ff


CAST_CHUNK = 512
N_FF_CHUNKS = D_FF // CAST_CHUNK


def _out_mlp_cast_kernel(mix_ret_ref, mix_swa_ref, x_ref, gain_ref, w_out_hbm, w_up_hbm, w_down_hbm,
                         y_ref, w_out_bf_hbm, w_up_bf_hbm, w_down_bf_hbm,
                         w_out_v, w_up_v, w_down_v, stage_out, stage_up, stage_down, in_sem, out_sem):
    step = pl.program_id(0)

    def out_in(i):
        return pltpu.make_async_copy(w_out_hbm.at[pl.ds(i * CAST_CHUNK, CAST_CHUNK), :], stage_out.at[i],
                                     in_sem.at[0, i])

    def up_in(c):
        return pltpu.make_async_copy(w_up_hbm.at[:, pl.ds(c * CAST_CHUNK, CAST_CHUNK)], stage_up.at[c % 2],
                                     in_sem.at[1, c % 2])

    def down_in(c):
        return pltpu.make_async_copy(w_down_hbm.at[pl.ds(c * CAST_CHUNK, CAST_CHUNK), :], stage_down.at[c % 2],
                                     in_sem.at[2, c % 2])

    writebacks = [pltpu.make_async_copy(src, dst, out_sem.at[i]) for i, (src, dst) in enumerate(
        [(w_out_v, w_out_bf_hbm), (w_up_v, w_up_bf_hbm), (w_down_v, w_down_bf_hbm)])]

    @pl.when(step == 0)
    def _():
        n_out = D_MODEL // CAST_CHUNK
        for i in range(n_out):
            out_in(i).start()
        up_in(0).start()
        down_in(0).start()
        for i in range(n_out):
            out_in(i).wait()
            w_out_v[i * CAST_CHUNK:(i + 1) * CAST_CHUNK, :] = stage_out[i].astype(BF16)
        writebacks[0].start()
        h = x_ref[...] + (_dot(mix_ret_ref[...].astype(BF16), w_out_v[:RET_WIDTH, :])
                          + _dot(mix_swa_ref[...].astype(BF16), w_out_v[RET_WIDTH:, :]))
        ms = jnp.mean(h * h, axis=-1, keepdims=True)
        hf = ((h * lax.rsqrt(ms + EPS)) * gain_ref[...]).astype(BF16)
        ff = None
        for c in range(N_FF_CHUNKS):
            chunk = slice(c * CAST_CHUNK, (c + 1) * CAST_CHUNK)
            if c + 1 < N_FF_CHUNKS:
                up_in(c + 1).start()
                down_in(c + 1).start()
            up_in(c).wait()
            down_in(c).wait()
            w_up_v[:, chunk] = stage_up[c % 2].astype(BF16)
            w_down_v[chunk, :] = stage_down[c % 2].astype(BF16)
            a = jnp.maximum(_dot(hf, w_up_v[:, chunk]), 0.0)
            d = _dot((a * a).astype(BF16), w_down_v[chunk, :])
            ff = d if ff is None else ff + d
        writebacks[1].start()
        writebacks[2].start()
        y_ref[...] = h + ff

    @pl.when(step > 0)
    def _():
        _out_mlp_kernel(mix_ret_ref, mix_swa_ref, x_ref, w_out_v, gain_ref, w_up_v, w_down_v, y_ref)

    @pl.when(step == pl.num_programs(0) - 1)
    def _():
        for wb in writebacks:
            wb.wait()


def _out_mlp_cast(mix_ret, mix_swa, x2d, gain, w_out, w_up, w_down):
    m = x2d.shape[0]
    tm = min(ROW_TILE, m)
    assert D_MODEL % CAST_CHUNK == 0 and D_FF % CAST_CHUNK == 0
    row = lambda w: pl.BlockSpec((tm, w), lambda i: (i, 0))
    full = lambda a: pl.BlockSpec(a.shape, lambda i: (0, 0), pipeline_mode=pl.Buffered(1))
    hbm = pl.BlockSpec(memory_space=pl.ANY)
    return pl.pallas_call(
        _out_mlp_cast_kernel,
        grid=(m // tm,),
        in_specs=[row(RET_WIDTH), row(SWA_WIDTH), row(D_MODEL), full(gain), hbm, hbm, hbm],
        out_specs=[row(D_MODEL), hbm, hbm, hbm],
        out_shape=[jax.ShapeDtypeStruct((m, D_MODEL), F32),
                   jax.ShapeDtypeStruct(w_out.shape, BF16),
                   jax.ShapeDtypeStruct(w_up.shape, BF16),
                   jax.ShapeDtypeStruct(w_down.shape, BF16)],
        scratch_shapes=[
            pltpu.VMEM(w_out.shape, BF16), pltpu.VMEM(w_up.shape, BF16), pltpu.VMEM(w_down.shape, BF16),
            pltpu.VMEM((D_MODEL // CAST_CHUNK, CAST_CHUNK, D_MODEL), F32),
            pltpu.VMEM((2, D_MODEL, CAST_CHUNK), F32),
            pltpu.VMEM((2, CAST_CHUNK, D_MODEL), F32),
            pltpu.SemaphoreType.DMA((3, 2)),
            pltpu.SemaphoreType.DMA((3,)),
        ],
        compiler_params=pltpu.CompilerParams(
            dimension_semantics=("arbitrary",), vmem_limit_bytes=VMEM_LIMIT),
        name="out_mlp_cast",
    )(mix_ret, mix_swa, x2d, gain, w_out, w_up, w_down)


def _split_pair_rows(x, first):
    return jnp.concatenate([jnp.where(first, x, 0.0), jnp.where(first, 0.0, x)], axis=0).astype(BF16)


def _softmax_sink_pv(s, sink_wide, v_t):
    m = jnp.maximum(jnp.max(s, axis=-1, keepdims=True), sink_wide)
    p = jnp.exp(s - jnp.concatenate([m, m], axis=1))
    denom = jnp.sum(p, axis=-1, keepdims=True) + jnp.exp(sink_wide - m)
    return _dot_nt(p.astype(BF16), v_t) / denom


def _softmax_sink_pv_t(s_t, sink_lanes, v_t):
    m = jnp.maximum(jnp.max(s_t, axis=0, keepdims=True), sink_lanes)
    p = jnp.exp(s_t - m)
    denom = jnp.sum(p, axis=0, keepdims=True) + jnp.exp(sink_lanes - m)
    return _dot(v_t, p.astype(BF16)) / denom


def _prompt_consts(intra_ref, qdec_ref, kdec_ref, sdec_ref, bias_ref):
    r = _iota((LANES, LANES), 0)
    lane2 = _iota((LANES, LANES), 1) >= HEAD_DIM
    rf = r.astype(F32)
    ri = _iota((LANES, 2 * LANES), 0)
    ci = _iota((LANES, 2 * LANES), 1)
    diff = (ri - (ci & (LANES - 1))).astype(F32)
    for p in range(N_PAIRS):
        lg = _pair_const(LOG_DECAY, p, lane2)
        qdec_ref[p] = jnp.exp(lg * (rf + 1.0))
        kdec_ref[p] = jnp.exp(lg * (RET_CHUNK - 1.0 - rf))
        sdec_ref[p] = jnp.exp(_pair_const(LOG_DECAY, p, r >= HEAD_DIM) * float(RET_CHUNK))
        lg2 = _pair_const(LOG_DECAY, p, ci >= LANES)
        intra_ref[p] = jnp.where(diff >= 0.0, jnp.exp(lg2 * jnp.maximum(diff, 0.0)), 0.0)
    cols = SWA_GROUP * WINDOW
    kb = _iota((2 * WINDOW, cols), 0)
    cb = _iota((2 * WINDOW, cols), 1)
    grp = cb >> 7
    dist = WINDOW + (cb & (WINDOW - 1)) - kb
    valid = (dist >= 0) & (dist < WINDOW)
    distf = dist.astype(F32)
    for j in range(N_SWA_KV):
        sl = [ALIBI_SLOPES[SWA_GROUP * j + g] for g in range(SWA_GROUP)]
        slope = jnp.where(grp == 0, sl[0], jnp.where(grp == 1, sl[1], jnp.where(grp == 2, sl[2], sl[3])))
        b = jnp.where(valid, -(slope.astype(F32) * distf), NEG_INF)
        bias_ref[0, j] = b
        bias_ref[1, j] = jnp.where(kb >= WINDOW, b, NEG_INF)


def _prompt_layer_kernel(sinks_ref, x_ref, x_next_ref, gain_mix_ref, w_in_ref, qg_ref, kg_ref,
                         w_out_hbm, gain_ffn_ref, w_up_hbm, w_down_hbm,
                         y_ref, ret_ref, kwin_ref, vwin_ref,
                         main_ref, qn_ref, kn_ref, vs_ref, mix_ref, hb_ref,
                         state_ref, prevk_ref, prevv_ref,
                         intra_ref, qdec_ref, kdec_ref, sdec_ref, bias_ref,
                         w_out_ref, w_up_ref, w_down_ref, w_sem):
    t = pl.program_id(1)
    step = pl.program_id(0) * pl.num_programs(1) + t
    cur = step % 2
    nxt = 1 - cur

    late_weights = [pltpu.make_async_copy(src, dst, w_sem.at[i]) for i, (src, dst) in enumerate(
        [(w_out_hbm, w_out_ref), (w_up_hbm, w_up_ref), (w_down_hbm, w_down_ref)])]

    @pl.when(step == 0)
    def _():
        for copy in late_weights:
            copy.start()
        _prompt_consts(intra_ref, qdec_ref, kdec_ref, sdec_ref, bias_ref)
        _in_proj_kernel(x_ref, gain_mix_ref, w_in_ref, qg_ref, kg_ref,
                        main_ref.at[0], qn_ref.at[0], kn_ref.at[0], vs_ref.at[0])

    @pl.when(t == 0)
    def _():
        state_ref[...] = jnp.zeros_like(state_ref)
        prevk_ref[...] = jnp.zeros_like(prevk_ref)
        prevv_ref[...] = jnp.zeros_like(prevv_ref)

    first = _first_half()
    ones_bd = _ones_block_diag()
    bd_mask = (_iota((LANES, LANES), 0) >= HEAD_DIM) == (_iota((LANES, LANES), 1) >= HEAD_DIM)

    n_chunks = PROMPT_TILE // RET_CHUNK
    assert n_chunks == 4
    plan = [dict(a=[], b=[]),
            dict(a=["m0", "m1"], b=[]),
            dict(a=["m2", "m3"], b=[]),
            dict(a=["v", "k"], b=["q0", "q1"])]
    a0 = MAIN_WIDTH
    attn_cols = {"v": (a0 + SWA_WIDTH + SWA_KV_WIDTH, SWA_KV_WIDTH), "k": (a0 + SWA_WIDTH, SWA_KV_WIDTH)}
    attn_cols.update({"q%d" % i: (a0 + Q_BLOCK * i, Q_BLOCK) for i in range(SWA_WIDTH // Q_BLOCK)})
    cols = lambda base, p: slice(base + p * LANES, base + (p + 1) * LANES)

    def project(items):
        raw = {}
        for it in items:
            if it[0] == "m":
                cb = int(it[1:])
                blk = slice(cb * PROJ_BLOCK, (cb + 1) * PROJ_BLOCK)
                main_ref[nxt, :, blk] = _dot(hb_ref[...], w_in_ref[:, blk])
            else:
                lo, width = attn_cols[it]
                raw[it] = _dot(hb_ref[...], w_in_ref[:, lo:lo + width])
        return raw

    def attn_scales(raw):
        return {name: [_head_rms_scale(val[:, cols(0, cq)], ones_bd) for cq in range(val.shape[1] // LANES)]
                for name, val in raw.items() if name != "v"}

    def store_attn(raw, scales):
        for name, val in raw.items():
            if name == "v":
                vs_ref[nxt] = val
            elif name == "k":
                kn_ref[nxt] = (val * scales[name][0]) * _pair_gain(kg_ref)
            else:
                base = attn_cols[name][0] - a0
                for cq in range(val.shape[1] // LANES):
                    qn_ref[nxt, :, cols(base, cq)] = (val[:, cols(0, cq)] * scales[name][cq]) * _pair_gain(qg_ref)

    pairs = range(N_PAIRS)
    kvs = range(N_SWA_KV)

    def stage1(c):
        rows = slice(c * RET_CHUNK, (c + 1) * RET_CHUNK)
        q = [main_ref[cur, rows, cols(0, p)] for p in pairs]
        k = [main_ref[cur, rows, cols(RET_WIDTH, p)] * K_SCALE for p in pairs]
        v = [main_ref[cur, rows, cols(2 * RET_WIDTH, p)] for p in pairs]
        state = [state_ref[p] for p in pairs]
        kc = kn_ref[cur, rows, :]
        k_sw = pltpu.roll(kc, HEAD_DIM, axis=1)
        v_t = vs_ref[cur, rows, :].T
        is_first = (t == 0).astype(jnp.int32) if c == 0 else 0
        k_dup =[(jnp.where(first, kc, k_sw) if j == 0 else jnp.where(first, k_sw, kc)).astype(BF16)
                 for j in kvs]
        v_tj = [v_t[j * HEAD_DIM:(j + 1) * HEAD_DIM].astype(BF16) for j in kvs]
        q_st = []
        for j in kvs:
            pieces = []
            for g in range(SWA_GROUP):
                qc = qn_ref[cur, rows, cols(0, 2 * j + g // 2)]
                pieces.append(jnp.where(first, qc, 0.0) if g % 2 == 0 else jnp.where(first, 0.0, qc))
            q_st.append(jnp.concatenate(pieces, axis=0).astype(BF16))

        s = [_dot_nt(q[p].astype(BF16), _split_pair_rows(k[p], first)) for p in pairs]
        s_t = [_dot_nt(jnp.concatenate([prevk_ref[j], k_dup[j]], axis=0), q_st[j]) for j in kvs]
        cross = [_dot((q[p] * qdec_ref[p]).astype(BF16), state[p].astype(BF16)) for p in pairs]
        upd = [_dot((k[p] * kdec_ref[p]).T.astype(BF16), v[p].astype(BF16)) for p in pairs]
        return dict(rows=rows, v=v, state=state, k_dup=k_dup, v_tj=v_tj, is_first=is_first,
                    s=s, s_t=s_t, cross=cross, upd=upd)

    def stage2(st):
        st["o"] = [_dot((st["s"][p] * intra_ref[p]).astype(BF16), _split_pair_rows(st["v"][p], first))
                   + st["cross"][p] for p in pairs]
        st["o_t"] = []
        for j in kvs:
            sink_lanes = jnp.concatenate(
                [jnp.full((1, WINDOW), sinks_ref[SWA_GROUP * j + g], F32) for g in range(SWA_GROUP)],
                axis=1)
            v_cat = jnp.concatenate([prevv_ref[j], st["v_tj"][j]], axis=1)
            st["o_t"].append(_softmax_sink_pv_t(st["s_t"][j] * K_SCALE + bias_ref[st["is_first"], j],
                                                sink_lanes, v_cat))
        for p in pairs:
            state_ref[p] = st["state"][p] * sdec_ref[p] + jnp.where(bd_mask, st["upd"][p], 0.0)
        for j in kvs:
            prevk_ref[j] = st["k_dup"][j]
            prevv_ref[j] = st["v_tj"][j]

    def stage3(st, raw):
        rows = st["rows"]
        scale = [_head_rms_scale(st["o"][p], ones_bd) for p in pairs]
        raw_scales = attn_scales(raw)
        for p in pairs:
            g = main_ref[cur, rows, cols(3 * RET_WIDTH, p)]
            mix_ref[rows, cols(0, p)] = (st["o"][p] * scale[p] * _silu(g)).astype(BF16)
        store_attn(raw, raw_scales)
        for j in kvs:
            o_t = st["o_t"][j]
            for half in range(2):
                pair_t = jnp.concatenate([o_t[:, (2 * half) * WINDOW:(2 * half + 1) * WINDOW],
                                          o_t[:, (2 * half + 1) * WINDOW:(2 * half + 2) * WINDOW]], axis=0)
                mix_ref[rows, cols(RET_WIDTH, 2 * j + half)] = pair_t.T.astype(BF16)

    xn = x_next_ref[...]
    hb_ref[...] = ((xn * lax.rsqrt(jnp.mean(xn * xn, axis=-1, keepdims=True) + EPS)) * gain_mix_ref[...]).astype(BF16)
    for c in range(n_chunks):
        st = stage1(c)
        raw = project(plan[c]["a"])
        stage2(st)
        raw.update(project(plan[c]["b"]))
        stage3(st, raw)

    @pl.when(t == pl.num_programs(1) - 1)
    def _():
        for p in range(N_PAIRS):
            s = state_ref[p]
            ret_ref[2 * p] = s[:HEAD_DIM, :HEAD_DIM]
            ret_ref[2 * p + 1] = s[HEAD_DIM:, HEAD_DIM:]
        last = slice(PROMPT_TILE - WINDOW, PROMPT_TILE)
        kwin_ref[...] = kn_ref[cur, last, :].T
        vwin_ref[...] = vs_ref[cur, last, :].T

    @pl.when(step == 0)
    def _():
        for copy in late_weights:
            copy.wait()

    _out_mlp_kernel(mix_ref.at[:, pl.ds(0, RET_WIDTH)], mix_ref.at[:, pl.ds(RET_WIDTH, SWA_WIDTH)], x_ref,
                    w_out_ref, gain_ffn_ref, w_up_ref, w_down_ref, y_ref)


def _prompt_layer(sinks, x2d, gain_mix, w_in_bf, qg, kg, w_out_bf, gain_ffn, w_up_bf, w_down_bf, batch, seq):
    nt = seq // PROMPT_TILE
    last_tile = batch * nt - 1
    row = lambda w: pl.BlockSpec((PROMPT_TILE, w), lambda b, t: (b * nt + t, 0))
    next_row = pl.BlockSpec((PROMPT_TILE, D_MODEL), lambda b, t: (jnp.minimum(b * nt + t + 1, last_tile), 0))
    full = lambda a: pl.BlockSpec(a.shape, lambda b, t: (0, 0), pipeline_mode=pl.Buffered(1))
    in_hbm = pl.BlockSpec(memory_space=pl.ANY)
    return pl.pallas_call(
        _prompt_layer_kernel,
        grid=(batch, nt),
        in_specs=[pl.BlockSpec(memory_space=pltpu.SMEM), row(D_MODEL), next_row,
                  full(gain_mix), full(w_in_bf), full(qg), full(kg),
                  in_hbm, full(gain_ffn), in_hbm, in_hbm],
        out_specs=[row(D_MODEL),
                   pl.BlockSpec((None, N_RET_HEADS, HEAD_DIM, HEAD_DIM), lambda b, t: (b, 0, 0, 0)),
                   pl.BlockSpec((None, SWA_KV_WIDTH, WINDOW), lambda b, t: (b, 0, 0)),
                   pl.BlockSpec((None, SWA_KV_WIDTH, WINDOW), lambda b, t: (b, 0, 0))],
        out_shape=[jax.ShapeDtypeStruct((batch * seq, D_MODEL), F32),
                   jax.ShapeDtypeStruct((batch, N_RET_HEADS, HEAD_DIM, HEAD_DIM), F32),
                   jax.ShapeDtypeStruct((batch, SWA_KV_WIDTH, WINDOW), F32),
                   jax.ShapeDtypeStruct((batch, SWA_KV_WIDTH, WINDOW), F32)],
        scratch_shapes=[
            pltpu.VMEM((2, PROMPT_TILE, MAIN_WIDTH), F32),
            pltpu.VMEM((2, PROMPT_TILE, SWA_WIDTH), F32),
            pltpu.VMEM((2, PROMPT_TILE, SWA_KV_WIDTH), F32),
            pltpu.VMEM((2, PROMPT_TILE, SWA_KV_WIDTH), F32),
            pltpu.VMEM((PROMPT_TILE, MIX_WIDTH), BF16),
            pltpu.VMEM((PROMPT_TILE, D_MODEL), BF16),
            pltpu.VMEM((N_PAIRS, LANES, LANES), F32),
            pltpu.VMEM((N_SWA_KV, WINDOW, LANES), BF16),
            pltpu.VMEM((N_SWA_KV, HEAD_DIM, WINDOW), BF16),
            pltpu.VMEM((N_PAIRS, LANES, 2 * LANES), F32),
            pltpu.VMEM((N_PAIRS, LANES, LANES), F32),
            pltpu.VMEM((N_PAIRS, LANES, LANES), F32),
            pltpu.VMEM((N_PAIRS, LANES, LANES), F32),
            pltpu.VMEM((2, N_SWA_KV, 2 * WINDOW, SWA_GROUP * WINDOW), F32),
            pltpu.VMEM(w_out_bf.shape, BF16),
            pltpu.VMEM(w_up_bf.shape, BF16),
            pltpu.VMEM(w_down_bf.shape, BF16),
            pltpu.SemaphoreType.DMA((3,)),
        ],
        compiler_params=pltpu.CompilerParams(
            dimension_semantics=("arbitrary", "arbitrary"), vmem_limit_bytes=PROMPT_VMEM_LIMIT),
        name="prompt_layer",
    )(sinks, x2d, x2d, gain_mix, w_in_bf, qg, kg, w_out_bf, gain_ffn, w_up_bf, w_down_bf)


DEC_ROWS = 128
DEC_UNROLL = 8


def _decode_attn_consts(dec_seq, bias_ref):
    shift = dec_seq.bit_length() - 1
    rows = N_SWA_HEADS * dec_seq
    rb = _iota((rows, WINDOW), 0)
    cb = _iota((rows, WINDOW), 1)
    head = rb >> shift
    i = rb & (dec_seq - 1)
    slope = jnp.zeros((rows, WINDOW), F32)
    for h in range(N_SWA_HEADS):
        slope = jnp.where(head == h, ALIBI_SLOPES[h], slope)
    bias_ref[0] = jnp.where(cb > i, -(slope * (WINDOW + i - cb).astype(F32)), NEG_INF)
    m = cb & (dec_seq - 1)
    bias_ref[1] = jnp.where(m <= i, -(slope * (i - m).astype(F32)), NEG_INF)


GROUPS_PER_STEP = 2


def _decode_mixers_kernel(dec_seq, nb, qdec_ref, kdec_ref, sdec_ref, intra_ref, sinks_ref,
                          q_ref, k_ref, v_ref, g_ref, st_hbm, qn_ref, kn_ref, vs_ref, kt_ref, vt_ref,
                          mix_ret_ref, st_out_hbm, mix_swa_ref, kt_out_ref, vt_out_ref,
                          qt_s, kt_s, vt_s, qdt_s, kdt_s, o_s,
                          bias_ref, qbd_ref, oblk_ref, knew_ref, vnew_ref, knt_ref, vst_ref,
                          st_in, st_new, in_sem, out_sem):
    pair = pl.program_id(0)
    slot = pair % 2
    halves = [slice(0, HEAD_DIM), slice(HEAD_DIM, 2 * HEAD_DIM)]
    first = _first_half()
    shift = dec_seq.bit_length() - 1
    e_blk = HEAD_DIM // 2
    n_eb = HEAD_DIM // e_blk

    def state_in(p, hh):
        return pltpu.make_async_copy(st_hbm.at[2 * p + hh], st_in.at[p % 2, hh], in_sem.at[p % 2, hh])

    def state_out(p, hh, eb):
        es = pl.ds(eb * e_blk, e_blk)
        return pltpu.make_async_copy(st_new.at[p % 2, hh, :, es, :], st_out_hbm.at[2 * p + hh, :, es, :],
                                     out_sem.at[p % 2, n_eb * hh + eb])

    def for_state_blocks(p, action):
        for hh in range(2):
            for eb in range(n_eb):
                action(state_out(p, hh, eb))

    @pl.when(pair == 0)
    def _():
        for hh in range(2):
            state_in(pair, hh).start()
        _decode_attn_consts(dec_seq, bias_ref)

    @pl.when(pair + 1 < N_PAIRS)
    def _():
        for hh in range(2):
            state_in(pair + 1, hh).start()

    @pl.when(pair >= 2)
    def _():
        for_state_blocks(pair - 2, lambda copy: copy.wait())

    def ret_stage():
        for l in range(dec_seq):
            rows = pl.ds(l, nb, stride=dec_seq)
            q_t = q_ref[rows, :].T
            k_t = (k_ref[rows, :] * K_SCALE).T
            qt_s[l] = q_t
            kt_s[l] = k_t
            vt_s[l] = v_ref[rows, :].T
            for hh in range(2):
                qdt_s[l, halves[hh], :] = q_t[halves[hh]] * qdec_ref[2 * pair + hh, l]
                kdt_s[l, halves[hh], :] = k_t[halves[hh]] * kdec_ref[2 * pair + hh, l]

    def ret_intra(hh):
        h = 2 * pair + hh
        hs = halves[hh]
        for l in range(dec_seq):
            acc = None
            for m in range(l + 1):
                sc = jnp.sum(qt_s[l, hs, :] * kt_s[m, hs, :], axis=0, keepdims=True) * intra_ref[h, l - m]
                term = sc * vt_s[m, hs, :]
                acc = term if acc is None else acc + term
            o_s[l, hs, :] = acc

    def ret_block(hh, eb, d_lo, d_hi, accs=None):
        h = 2 * pair + hh
        es = slice(eb * e_blk, (eb + 1) * e_blk)
        erows = slice(hh * HEAD_DIM + eb * e_blk, hh * HEAD_DIM + (eb + 1) * e_blk)
        if accs is None:
            accs = [jnp.zeros((e_blk, nb), F32) for _ in range(dec_seq)]
        if eb == 0 and d_lo == 0:
            state_in(pair, hh).wait()
        for d in range(d_lo, d_hi):
            s_d = st_in[slot, hh, d, es, :]
            row = slice(hh * HEAD_DIM + d, hh * HEAD_DIM + d + 1)
            upd = s_d * sdec_ref[h]
            for l in range(dec_seq):
                accs[l] = accs[l] + qdt_s[l, row, :] * s_d
                upd = upd + kdt_s[l, row, :] * vt_s[l, erows, :]
            st_new[slot, hh, d, es, :] = upd
        if d_hi == HEAD_DIM:
            state_out(pair, hh, eb).start()
            for l in range(dec_seq):
                o_s[l, erows, :] = o_s[l, erows, :] + accs[l]
        return accs

    def ret_finish():
        for l in range(dec_seq):
            o = o_s[l]
            normed = []
            for hh in range(2):
                oh = o[halves[hh]]
                normed.append(oh * lax.rsqrt(jnp.mean(oh * oh, axis=0, keepdims=True) + EPS))
            rows = pl.ds(l, nb, stride=dec_seq)
            mix_ret_ref[rows, :] = jnp.concatenate(normed, axis=0).T * _silu(g_ref[rows, :])

    sink_rows = jnp.concatenate(
        [jnp.full((dec_seq, LANES), sinks_ref[h], F32) for h in range(N_SWA_HEADS)], axis=0)
    col_batch = _iota((N_SWA_HEADS * dec_seq, LANES), 1) >> shift
    keep_old = _iota((1, LANES), 1) < WINDOW - dec_seq

    def attn_stage(gi):
        grows = slice(gi * DEC_ROWS, (gi + 1) * DEC_ROWS)
        kn_t = kn_ref[grows, :].T
        vs_t = vs_ref[grows, :].T
        knt_ref[...] = kn_t.astype(BF16)
        vst_ref[...] = vs_t.astype(BF16)
        for bb in range(DEC_GROUP):
            sh = (WINDOW - dec_seq - bb * dec_seq) % LANES
            knew_ref[bb] = pltpu.roll(kn_t, sh, axis=1) if sh else kn_t
            vnew_ref[bb] = pltpu.roll(vs_t, sh, axis=1) if sh else vs_t
        qn = qn_ref[grows, :]
        qn_sw = pltpu.roll(qn, HEAD_DIM, axis=1)
        for h in range(N_SWA_HEADS):
            kv_half = h // SWA_GROUP
            if (h % 2) == kv_half:
                src = qn[:, (h // 2) * LANES:(h // 2 + 1) * LANES]
            else:
                col = (h + 1) // 2
                src = qn_sw[:, col * LANES:(col + 1) * LANES]
            qbd_ref[h] = jnp.where(first, src, 0.0) if kv_half == 0 else jnp.where(first, 0.0, src)

    def attn_block(gi, i):
        bs = [i * DEC_UNROLL + u for u in range(DEC_UNROLL)]
        rows = [slice(b * dec_seq, (b + 1) * dec_seq) for b in bs]
        k_old = [kt_ref[gi * DEC_GROUP + b] for b in bs]
        v_old = [vt_ref[gi * DEC_GROUP + b] for b in bs]
        q_st = [jnp.concatenate([qbd_ref[h, r, :] for h in range(N_SWA_HEADS)], axis=0).astype(BF16)
                for r in rows]
        s = [_dot(q_st[u], jnp.concatenate([k_old[u].astype(BF16), knt_ref[...]], axis=1))
             for u in range(DEC_UNROLL)]
        o = []
        for u, b in enumerate(bs):
            bias = jnp.concatenate([bias_ref[0], jnp.where(col_batch == b, bias_ref[1], NEG_INF)], axis=1)
            w_v = jnp.concatenate([v_old[u].astype(BF16), vst_ref[...]], axis=1)
            o.append(_softmax_sink_pv(s[u] * K_SCALE + bias, sink_rows, w_v))
        for u, b in enumerate(bs):
            for h in range(N_SWA_HEADS):
                oblk_ref[h, rows[u], :] = o[u][h * dec_seq:(h + 1) * dec_seq]
            kt_out_ref[gi * DEC_GROUP + b] = jnp.where(
                keep_old, pltpu.roll(k_old[u], LANES - dec_seq, axis=1), knew_ref[b])
            vt_out_ref[gi * DEC_GROUP + b] = jnp.where(
                keep_old, pltpu.roll(v_old[u], LANES - dec_seq, axis=1), vnew_ref[b])

    def attn_finish(gi):
        grows = slice(gi * DEC_ROWS, (gi + 1) * DEC_ROWS)
        y1 = jnp.where(first, oblk_ref[3], oblk_ref[4])
        moved = pltpu.roll(jnp.concatenate([oblk_ref[1], y1, oblk_ref[6], oblk_ref[6]], axis=1), HEAD_DIM, axis=1)
        outs = [
            jnp.where(first, oblk_ref[0], moved[:, 0:LANES]),
            jnp.where(first, oblk_ref[2], moved[:, LANES:2 * LANES]),
            jnp.where(first, moved[:, 2 * LANES:3 * LANES], oblk_ref[5]),
            jnp.where(first, moved[:, 3 * LANES:4 * LANES], oblk_ref[7]),
        ]
        for c in range(SWA_WIDTH // LANES):
            mix_swa_ref[grows, c * LANES:(c + 1) * LANES] = outs[c].astype(BF16)

    ret_stage()
    for gi in range(GROUPS_PER_STEP):
        attn_stage(gi)
        ret_intra(gi)
        n_blocks = DEC_GROUP // DEC_UNROLL
        per_eb = n_blocks // 2
        d_step = HEAD_DIM // per_eb
        for eb in range(2):
            accs = None
            for part in range(per_eb):
                accs = ret_block(gi, eb, part * d_step, (part + 1) * d_step, accs)
                attn_block(gi, eb * per_eb + part)
        attn_finish(gi)
    ret_finish()

    @pl.when(pair == N_PAIRS - 1)
    def _():
        if N_PAIRS >= 2:
            for_state_blocks(pair - 1, lambda copy: copy.wait())
        for_state_blocks(pair, lambda copy: copy.wait())


def _decode_mixers(sinks, main, qn, kn, vs, state_t, k_t, v_t, dec_seq):
    nb = state_t.shape[-1]
    m = main.shape[0]
    assert nb == LANES and m == nb * dec_seq and k_t.shape == (nb, SWA_KV_WIDTH, WINDOW)
    assert DEC_GROUP * dec_seq == DEC_ROWS and nb == N_PAIRS * GROUPS_PER_STEP * DEC_GROUP
    assert (DEC_GROUP // DEC_UNROLL) % 2 == 0 and dec_seq & (dec_seq - 1) == 0
    tab = lambda f: jnp.asarray([[f(h, j) for j in range(dec_seq)] for h in range(N_RET_HEADS)], F32)
    qdec = tab(lambda h, j: math.exp(LOG_DECAY[h] * (j + 1.0)))
    kdec = tab(lambda h, j: math.exp(LOG_DECAY[h] * (dec_seq - 1.0 - j)))
    intra = tab(lambda h, j: math.exp(LOG_DECAY[h] * j))
    sdec = jnp.asarray([math.exp(LOG_DECAY[h] * dec_seq) for h in range(N_RET_HEADS)], F32)
    smem = pl.BlockSpec(memory_space=pltpu.SMEM)
    col = lambda base: pl.BlockSpec((m, LANES), lambda p: (0, base + p))
    st_spec = pl.BlockSpec(memory_space=pl.ANY)
    st_pairs = pltpu.VMEM((2, 2, HEAD_DIM, HEAD_DIM, nb), F32)
    step_rows = GROUPS_PER_STEP * DEC_ROWS
    row = lambda w: pl.BlockSpec((step_rows, w), lambda p: (p, 0))
    cache = pl.BlockSpec((GROUPS_PER_STEP * DEC_GROUP, SWA_KV_WIDTH, WINDOW), lambda p: (p, 0, 0))
    stage = pltpu.VMEM((dec_seq, LANES, nb), F32)
    return pl.pallas_call(
        functools.partial(_decode_mixers_kernel, dec_seq, nb),
        grid=(N_PAIRS,),
        in_specs=[smem, smem, smem, smem, smem,
                  col(0), col(N_PAIRS), col(2 * N_PAIRS), col(3 * N_PAIRS), st_spec,
                  row(SWA_WIDTH), row(SWA_KV_WIDTH), row(SWA_KV_WIDTH), cache, cache],
        out_specs=[pl.BlockSpec((m, LANES), lambda p: (0, p)), st_spec, row(SWA_WIDTH), cache, cache],
        out_shape=[jax.ShapeDtypeStruct((m, RET_WIDTH), F32),
                   jax.ShapeDtypeStruct(state_t.shape, F32),
                   jax.ShapeDtypeStruct((m, SWA_WIDTH), BF16),
                   jax.ShapeDtypeStruct(k_t.shape, F32),
                   jax.ShapeDtypeStruct(v_t.shape, F32)],
        scratch_shapes=[
            stage, stage, stage, stage, stage, stage,
            pltpu.VMEM((2, N_SWA_HEADS * dec_seq, WINDOW), F32),
            pltpu.VMEM((N_SWA_HEADS, DEC_ROWS, LANES), F32),
            pltpu.VMEM((N_SWA_HEADS, DEC_ROWS, LANES), F32),
            pltpu.VMEM((DEC_GROUP, SWA_KV_WIDTH, LANES), F32),
            pltpu.VMEM((DEC_GROUP, SWA_KV_WIDTH, LANES), F32),
            pltpu.VMEM((SWA_KV_WIDTH, DEC_ROWS), BF16),
            pltpu.VMEM((SWA_KV_WIDTH, DEC_ROWS), BF16),
            st_pairs, st_pairs,
            pltpu.SemaphoreType.DMA((2, 2)), pltpu.SemaphoreType.DMA((2, 4)),
        ],
        compiler_params=pltpu.CompilerParams(
            dimension_semantics=("arbitrary",), vmem_limit_bytes=VMEM_LIMIT),
        name="decode_mixers",
    )(qdec, kdec, sdec, intra, sinks, main, main, main, main, state_t, qn, kn, vs, k_t, v_t)


def kernel(x_prompt, x_sample, state_ret, cache_swa_k, cache_swa_v, norm_mix_gain, w_in, q_norm_gain,
           k_norm_gain, attn_sinks, w_out, norm_ffn_gain, w_up, w_down):
    batch, seq, d = x_prompt.shape
    nb, dec_seq, _ = x_sample.shape
    wb = cache_swa_k.shape[1]
    assert d == D_MODEL and seq % PROMPT_TILE == 0 and wb == WINDOW

    gain_mix = norm_mix_gain.reshape(1, D_MODEL)
    gain_ffn = norm_ffn_gain.reshape(1, D_MODEL)
    qg = q_norm_gain.reshape(1, HEAD_DIM)
    kg = k_norm_gain.reshape(1, HEAD_DIM)

    def from_key_minor(a_t):
        return jnp.transpose(a_t.reshape(a_t.shape[0], N_SWA_KV, HEAD_DIM, WINDOW), (0, 3, 1, 2))

    def to_key_minor(a):
        return jnp.transpose(a, (0, 2, 3, 1)).reshape(a.shape[0], SWA_KV_WIDTH, WINDOW)

    xs = x_sample.reshape(nb * dec_seq, D_MODEL)
    main_s, qn_s, kn_s, vs_s, w_in_bf = _in_proj_cast(xs, gain_mix, w_in, qg, kg)
    mix_ret_s, state_t, mix_swa_s, k_t, v_t = _decode_mixers(
        attn_sinks, main_s, qn_s, kn_s, vs_s, jnp.transpose(state_ret, (1, 2, 3, 0)),
        to_key_minor(cache_swa_k), to_key_minor(cache_swa_v), dec_seq)
    y_s, w_out_bf, w_up_bf, w_down_bf = _out_mlp_cast(mix_ret_s, mix_swa_s, xs, gain_ffn, w_out, w_up, w_down)

    xp = x_prompt.reshape(batch * seq, D_MODEL)
    y_p, ret_p, kwin_t, vwin_t = _prompt_layer(attn_sinks, xp, gain_mix, w_in_bf, qg, kg, w_out_bf, gain_ffn,
                                               w_up_bf, w_down_bf, batch, seq)
    y_p = y_p.reshape(batch, seq, D_MODEL)

    return (y_p, y_s.reshape(nb, dec_seq, D_MODEL), ret_p, from_key_minor(kwin_t), from_key_minor(vwin_t),
            jnp.transpose(state_t, (3, 0, 1, 2)), from_key_minor(k_t), from_key_minor(v_t))
```

```python
import functools
import math

import jax
import jax.numpy as jnp
from jax import lax
from jax.experimental import pallas as pl
from jax.experimental.pallas import tpu as pltpu

F32 = jnp.float32
BF16 = jnp.bfloat16

D_MODEL = 1024
HEAD_DIM = 64
N_RET_HEADS = 8
N_SWA_HEADS = 8
N_SWA_KV = 2
SWA_GROUP = N_SWA_HEADS // N_SWA_KV
RET_WIDTH = N_RET_HEADS * HEAD_DIM
SWA_WIDTH = N_SWA_HEADS * HEAD_DIM
SWA_KV_WIDTH = N_SWA_KV * HEAD_DIM
MAIN_WIDTH = 4 * RET_WIDTH
IN_WIDTH = MAIN_WIDTH + SWA_WIDTH + 2 * SWA_KV_WIDTH
MIX_WIDTH = RET_WIDTH + SWA_WIDTH
D_FF = 4 * D_MODEL
WINDOW = 128
RET_CHUNK = 128
EPS = 1e-6
NEG_INF = -1e30

LANES = 128
N_PAIRS = N_RET_HEADS // 2
LOG_DECAY = [math.log(1.0 - 2.0 ** (-5.0 - h)) for h in range(N_RET_HEADS)]
ALIBI_SLOPES = [2.0 ** (-8.0 * (h + 1) / N_SWA_HEADS) for h in range(N_SWA_HEADS)]
K_SCALE = HEAD_DIM ** -0.5

MXU_WIDTH = 256
PROJ_BLOCK = 2 * MXU_WIDTH
Q_BLOCK = MXU_WIDTH
PROMPT_TILE = 512
DEC_GROUP = 16
VMEM_LIMIT = 56 * 1024 * 1024
PROMPT_VMEM_LIMIT = 62 * 1024 * 1024


def _dot(a, b):
    return jnp.dot(a, b, preferred_element_type=F32)


def _dot_nt(a, b):
    return lax.dot_general(a, b, (((1,), (1,)), ((), ())), preferred_element_type=F32)


def _iota(shape, dim):
    return lax.broadcasted_iota(jnp.int32, shape, dim)


def _ones_block_diag():
    same = ((_iota((2 * LANES, LANES), 0) >> 6) & 1) == (_iota((2 * LANES, LANES), 1) >> 6)
    return jnp.where(same, 1.0, 0.0).astype(BF16)


def _head_sumsq(x, ones_bd):
    x2 = x * x
    hi = x2.astype(BF16)
    lo = (x2 - hi.astype(F32)).astype(BF16)
    return _dot(jnp.concatenate([hi, lo], axis=1), ones_bd)


def _head_rms_scale(x, ones_bd):
    return lax.rsqrt(_head_sumsq(x, ones_bd) * (1.0 / HEAD_DIM) + EPS)


def _head_rms_scales(xs, ones_bd):
    if not xs:
        return []
    parts = []
    for x in xs:
        x2 = x * x
        hi = x2.astype(BF16)
        parts.append(jnp.concatenate([hi, (x2 - hi.astype(F32)).astype(BF16)], axis=1))
    total = _dot(jnp.concatenate(parts, axis=0) if len(parts) > 1 else parts[0], ones_bd)
    out, lo = [], 0
    for x in xs:
        out.append(lax.rsqrt(total[lo:lo + x.shape[0]] * (1.0 / HEAD_DIM) + EPS))
        lo += x.shape[0]
    return out


def _first_half():
    return _iota((1, LANES), 1) < HEAD_DIM


def _pair_const(values, pair, lane_is_second):
    return jnp.where(lane_is_second, values[2 * pair + 1], values[2 * pair]).astype(F32)


def _pair_gain(gain_ref):
    g = gain_ref[...]
    return jnp.concatenate([g, g], axis=1)


def _silu(g):
    return g * (1.0 / (1.0 + jnp.exp(-g)))


def _in_proj_kernel(x_ref, gain_ref, w_ref, qg_ref, kg_ref, main_ref, qn_ref, kn_ref, vs_ref, fetch=None):
    x = x_ref[...]
    ms = jnp.mean(x * x, axis=-1, keepdims=True)
    hb = ((x * lax.rsqrt(ms + EPS)) * gain_ref[...]).astype(BF16)
    ones_bd = _ones_block_diag()

    def project(lo, width):
        if fetch is not None:
            fetch(lo, width)
        return _dot(hb, w_ref[:, lo:lo + width])

    for c in range(MAIN_WIDTH // PROJ_BLOCK):
        main_ref[:, c * PROJ_BLOCK:(c + 1) * PROJ_BLOCK] = project(c * PROJ_BLOCK, PROJ_BLOCK)
    qs = project(MAIN_WIDTH, SWA_WIDTH)
    k0 = MAIN_WIDTH + SWA_WIDTH
    ks = project(k0, SWA_KV_WIDTH)
    vs_ref[...] = project(k0 + SWA_KV_WIDTH, SWA_KV_WIDTH)
    q_cols = [qs[:, c * LANES:(c + 1) * LANES] for c in range(SWA_WIDTH // LANES)]
    scales = _head_rms_scales(q_cols + [ks], ones_bd)
    for c, xc in enumerate(q_cols):
        qn_ref[:, c * LANES:(c + 1) * LANES] = (xc * scales[c]) * _pair_gain(qg_ref)
    kn_ref[...] = (ks * scales[-1]) * _pair_gain(kg_ref)


ROW_TILE = 512


W_IN_BLOCKS = [(c * PROJ_BLOCK, PROJ_BLOCK) for c in range(MAIN_WIDTH // PROJ_BLOCK)] + [(MAIN_WIDTH, SWA_WIDTH),
                                                                    (MAIN_WIDTH + SWA_WIDTH, 2 * SWA_KV_WIDTH)]
assert sum(w for _, w in W_IN_BLOCKS) == IN_WIDTH
WEIGHT_DMA_PRIORITY = 1


def _in_proj_cast_kernel(x_ref, gain_ref, w_hbm, qg_ref, kg_ref,
                         main_ref, qn_ref, kn_ref, vs_ref, w_bf_hbm,
                         w_v, stage, in_sem, out_sem):
    step = pl.program_id(0)
    block_in = [pltpu.make_async_copy(w_hbm.at[:, pl.ds(lo, width)], stage.at[:, pl.ds(lo, width)], in_sem.at[i])
                for i, (lo, width) in enumerate(W_IN_BLOCKS)]
    writeback = pltpu.make_async_copy(w_v, w_bf_hbm, out_sem.at[0])

    @pl.when(step == 0)
    def _():
        for copy in block_in:
            copy.start(priority=WEIGHT_DMA_PRIORITY)
        arrived = set()

        def fetch(lo, width):
            for i, (blo, bwidth) in enumerate(W_IN_BLOCKS):
                if blo <= lo < blo + bwidth and i not in arrived:
                    assert lo + width <= blo + bwidth
                    block_in[i].wait()
                    w_v[:, blo:blo + bwidth] = stage[:, blo:blo + bwidth].astype(BF16)
                    arrived.add(i)

        _in_proj_kernel(x_ref, gain_ref, w_v, qg_ref, kg_ref, main_ref, qn_ref, kn_ref, vs_ref, fetch=fetch)
        assert len(arrived) == len(W_IN_BLOCKS)
        writeback.start()

    @pl.when(step > 0)
    def _():
        _in_proj_kernel(x_ref, gain_ref, w_v, qg_ref, kg_ref, main_ref, qn_ref, kn_ref, vs_ref)

    @pl.when(step == pl.num_programs(0) - 1)
    def _():
        writeback.wait()


def _in_proj_cast(x2d, gain, w_in, qg, kg):
    m = x2d.shape[0]
    tm = min(ROW_TILE, m)
    row = lambda w: pl.BlockSpec((tm, w), lambda i: (i, 0))
    full = lambda a: pl.BlockSpec(a.shape, lambda i: (0, 0), pipeline_mode=pl.Buffered(1))
    hbm = pl.BlockSpec(memory_space=pl.ANY)
    return pl.pallas_call(
        _in_proj_cast_kernel,
        grid=(m // tm,),
        in_specs=[row(D_MODEL), full(gain), hbm, full(qg), full(kg)],
        out_specs=[row(MAIN_WIDTH), row(SWA_WIDTH), row(SWA_KV_WIDTH), row(SWA_KV_WIDTH), hbm],
        out_shape=[jax.ShapeDtypeStruct((m, MAIN_WIDTH), F32),
                   jax.ShapeDtypeStruct((m, SWA_WIDTH), F32),
                   jax.ShapeDtypeStruct((m, SWA_KV_WIDTH), F32),
                   jax.ShapeDtypeStruct((m, SWA_KV_WIDTH), F32),
                   jax.ShapeDtypeStruct(w_in.shape, BF16)],
        scratch_shapes=[pltpu.VMEM(w_in.shape, BF16), pltpu.VMEM(w_in.shape, F32),
                        pltpu.SemaphoreType.DMA((len(W_IN_BLOCKS),)), pltpu.SemaphoreType.DMA((1,))],
        compiler_params=pltpu.CompilerParams(
            dimension_semantics=("arbitrary",), vmem_limit_bytes=VMEM_LIMIT),
        name="in_proj_cast",
    )(x2d, gain, w_in, qg, kg)


FF_CHUNK = 1024


def _out_mlp_kernel(mix_ret_ref, mix_swa_ref, x_ref, w_out_ref, gain_ref, w_up_ref, w_down_ref, y_ref):
    h = x_ref[...] + (_dot(mix_ret_ref[...].astype(BF16), w_out_ref[:RET_WIDTH, :])
                      + _dot(mix_swa_ref[...].astype(BF16), w_out_ref[RET_WIDTH:, :]))
    ms = jnp.mean(h * h, axis=-1, keepdims=True)
    hf = ((h * lax.rsqrt(ms + EPS)) * gain_ref[...]).astype(BF16)
    ff = None
    for c in range(D_FF // FF_CHUNK):
        u = _dot(hf, w_up_ref[:, c * FF_CHUNK:(c + 1) * FF_CHUNK])
        a = jnp.maximum(u, 0.0)
        d = _dot((a * a).astype(BF16), w_down_ref[c * FF_CHUNK:(c + 1) * FF_CHUNK, :])
        ff = d if ff is None else ff + d
    y_ref[...] = h + ff


CAST_CHUNK = 512
N_FF_CHUNKS = D_FF // CAST_CHUNK


def _out_mlp_cast_kernel(mix_ret_ref, mix_swa_ref, x_ref, gain_ref, w_out_hbm, w_up_hbm, w_down_hbm,
                         y_ref, w_out_bf_hbm, w_up_bf_hbm, w_down_bf_hbm,
                         w_out_v, w_up_v, w_down_v, stage_out, stage_up, stage_down, in_sem, out_sem):
    step = pl.program_id(0)

    def out_in(i):
        return pltpu.make_async_copy(w_out_hbm.at[pl.ds(i * CAST_CHUNK, CAST_CHUNK), :], stage_out.at[i],
                                     in_sem.at[0, i])

    def up_in(c):
        return pltpu.make_async_copy(w_up_hbm.at[:, pl.ds(c * CAST_CHUNK, CAST_CHUNK)], stage_up.at[c % 2],
                                     in_sem.at[1, c % 2])

    def down_in(c):
        return pltpu.make_async_copy(w_down_hbm.at[pl.ds(c * CAST_CHUNK, CAST_CHUNK), :], stage_down.at[c % 2],
                                     in_sem.at[2, c % 2])

    writebacks = [pltpu.make_async_copy(src, dst, out_sem.at[i]) for i, (src, dst) in enumerate(
        [(w_out_v, w_out_bf_hbm), (w_up_v, w_up_bf_hbm), (w_down_v, w_down_bf_hbm)])]

    @pl.when(step == 0)
    def _():
        n_out = D_MODEL // CAST_CHUNK
        for i in range(n_out):
            out_in(i).start(priority=WEIGHT_DMA_PRIORITY)
        up_in(0).start(priority=WEIGHT_DMA_PRIORITY)
        down_in(0).start(priority=WEIGHT_DMA_PRIORITY)
        for i in range(n_out):
            out_in(i).wait()
            w_out_v[i * CAST_CHUNK:(i + 1) * CAST_CHUNK, :] = stage_out[i].astype(BF16)
        writebacks[0].start()
        h = x_ref[...] + (_dot(mix_ret_ref[...].astype(BF16), w_out_v[:RET_WIDTH, :])
                          + _dot(mix_swa_ref[...].astype(BF16), w_out_v[RET_WIDTH:, :]))
        ms = jnp.mean(h * h, axis=-1, keepdims=True)
        hf = ((h * lax.rsqrt(ms + EPS)) * gain_ref[...]).astype(BF16)
        ff = None
        for c in range(N_FF_CHUNKS):
            chunk = slice(c * CAST_CHUNK, (c + 1) * CAST_CHUNK)
            if c + 1 < N_FF_CHUNKS:
                up_in(c + 1).start(priority=WEIGHT_DMA_PRIORITY)
                down_in(c + 1).start(priority=WEIGHT_DMA_PRIORITY)
            up_in(c).wait()
            down_in(c).wait()
            w_up_v[:, chunk] = stage_up[c % 2].astype(BF16)
            w_down_v[chunk, :] = stage_down[c % 2].astype(BF16)
            a = jnp.maximum(_dot(hf, w_up_v[:, chunk]), 0.0)
            d = _dot((a * a).astype(BF16), w_down_v[chunk, :])
            ff = d if ff is None else ff + d
        writebacks[1].start()
        writebacks[2].start()
        y_ref[...] = h + ff

    @pl.when(step > 0)
    def _():
        _out_mlp_kernel(mix_ret_ref, mix_swa_ref, x_ref, w_out_v, gain_ref, w_up_v, w_down_v, y_ref)

    @pl.when(step == pl.num_programs(0) - 1)
    def _():
        for wb in writebacks:
            wb.wait()


def _out_mlp_cast(mix_ret, mix_swa, x2d, gain, w_out, w_up, w_down):
    m = x2d.shape[0]
    tm = min(ROW_TILE, m)
    assert D_MODEL % CAST_CHUNK == 0 and D_FF % CAST_CHUNK == 0
    row = lambda w: pl.BlockSpec((tm, w), lambda i: (i, 0))
    full = lambda a: pl.BlockSpec(a.shape, lambda i: (0, 0), pipeline_mode=pl.Buffered(1))
    hbm = pl.BlockSpec(memory_space=pl.ANY)
    return pl.pallas_call(
        _out_mlp_cast_kernel,
        grid=(m // tm,),
        in_specs=[row(RET_WIDTH), row(SWA_WIDTH), row(D_MODEL), full(gain), hbm, hbm, hbm],
        out_specs=[row(D_MODEL), hbm, hbm, hbm],
        out_shape=[jax.ShapeDtypeStruct((m, D_MODEL), F32),
                   jax.ShapeDtypeStruct(w_out.shape, BF16),
                   jax.ShapeDtypeStruct(w_up.shape, BF16),
                   jax.ShapeDtypeStruct(w_down.shape, BF16)],
        scratch_shapes=[
            pltpu.VMEM(w_out.shape, BF16), pltpu.VMEM(w_up.shape, BF16), pltpu.VMEM(w_down.shape, BF16),
            pltpu.VMEM((D_MODEL // CAST_CHUNK, CAST_CHUNK, D_MODEL), F32),
            pltpu.VMEM((2, D_MODEL, CAST_CHUNK), F32),
            pltpu.VMEM((2, CAST_CHUNK, D_MODEL), F32),
            pltpu.SemaphoreType.DMA((3, 2)),
            pltpu.SemaphoreType.DMA((3,)),
        ],
        compiler_params=pltpu.CompilerParams(
            dimension_semantics=("arbitrary",), vmem_limit_bytes=VMEM_LIMIT),
        name="out_mlp_cast",
    )(mix_ret, mix_swa, x2d, gain, w_out, w_up, w_down)


def _split_pair_rows(x, first):
    return jnp.concatenate([jnp.where(first, x, 0.0), jnp.where(first, 0.0, x)], axis=0).astype(BF16)


def _softmax_sink_pv(s, sink_wide, v_t):
    m = jnp.maximum(jnp.max(s, axis=-1, keepdims=True), sink_wide)
    p = jnp.exp(s - jnp.concatenate([m, m], axis=1))
    denom = jnp.sum(p, axis=-1, keepdims=True) + jnp.exp(sink_wide - m)
    return _dot_nt(p.astype(BF16), v_t) / denom


def _softmax_sink_pv_t(s_t, sink_lanes, v_t):
    m = jnp.maximum(jnp.max(s_t, axis=0, keepdims=True), sink_lanes)
    p = jnp.exp(s_t - m)
    denom = jnp.sum(p, axis=0, keepdims=True) + jnp.exp(sink_lanes - m)
    return _dot(v_t, p.astype(BF16)) / denom


def _prompt_consts(intra_ref, qdec_ref, kdec_ref, sdec_ref, bias_ref):
    r = _iota((LANES, LANES), 0)
    lane2 = _iota((LANES, LANES), 1) >= HEAD_DIM
    rf = r.astype(F32)
    ri = _iota((LANES, 2 * LANES), 0)
    ci = _iota((LANES, 2 * LANES), 1)
    diff = (ri - (ci & (LANES - 1))).astype(F32)
    for p in range(N_PAIRS):
        lg = _pair_const(LOG_DECAY, p, lane2)
        qdec_ref[p] = jnp.exp(lg * (rf + 1.0))
        kdec_ref[p] = jnp.exp(lg * (RET_CHUNK - 1.0 - rf))
        sdec_ref[p] = jnp.exp(_pair_const(LOG_DECAY, p, r >= HEAD_DIM) * float(RET_CHUNK))
        lg2 = _pair_const(LOG_DECAY, p, ci >= LANES)
        intra_ref[p] = jnp.where(diff >= 0.0, jnp.exp(lg2 * jnp.maximum(diff, 0.0)), 0.0)
    cols = SWA_GROUP * WINDOW
    kb = _iota((2 * WINDOW, cols), 0)
    cb = _iota((2 * WINDOW, cols), 1)
    grp = cb >> 7
    dist = WINDOW + (cb & (WINDOW - 1)) - kb
    valid = (dist >= 0) & (dist < WINDOW)
    distf = dist.astype(F32)
    for j in range(N_SWA_KV):
        sl = [ALIBI_SLOPES[SWA_GROUP * j + g] for g in range(SWA_GROUP)]
        slope = jnp.where(grp == 0, sl[0], jnp.where(grp == 1, sl[1], jnp.where(grp == 2, sl[2], sl[3])))
        b = jnp.where(valid, -(slope.astype(F32) * distf), NEG_INF)
        bias_ref[0, j] = b
        bias_ref[1, j] = jnp.where(kb >= WINDOW, b, NEG_INF)


def _prompt_layer_kernel(sinks_ref, x_ref, x_next_ref, gain_mix_ref, w_in_ref, qg_ref, kg_ref,
                         w_out_hbm, gain_ffn_ref, w_up_hbm, w_down_hbm,
                         y_ref, ret_ref, kwin_ref, vwin_ref,
                         main_ref, qn_ref, kn_ref, vs_ref, mix_ref, hb_ref,
                         state_ref, prevk_ref, prevv_ref,
                         intra_ref, qdec_ref, kdec_ref, sdec_ref, bias_ref,
                         w_out_ref, w_up_ref, w_down_ref, w_sem):
    t = pl.program_id(1)
    step = pl.program_id(0) * pl.num_programs(1) + t
    cur = step % 2
    nxt = 1 - cur

    late_weights = [pltpu.make_async_copy(src, dst, w_sem.at[i]) for i, (src, dst) in enumerate(
        [(w_out_hbm, w_out_ref), (w_up_hbm, w_up_ref), (w_down_hbm, w_down_ref)])]

    @pl.when(step == 0)
    def _():
        for copy in late_weights:
            copy.start()
        _prompt_consts(intra_ref, qdec_ref, kdec_ref, sdec_ref, bias_ref)
        _in_proj_kernel(x_ref, gain_mix_ref, w_in_ref, qg_ref, kg_ref,
                        main_ref.at[0], qn_ref.at[0], kn_ref.at[0], vs_ref.at[0])

    @pl.when(t == 0)
    def _():
        state_ref[...] = jnp.zeros_like(state_ref)
        prevk_ref[...] = jnp.zeros_like(prevk_ref)
        prevv_ref[...] = jnp.zeros_like(prevv_ref)

    first = _first_half()
    ones_bd = _ones_block_diag()
    bd_mask = (_iota((LANES, LANES), 0) >= HEAD_DIM) == (_iota((LANES, LANES), 1) >= HEAD_DIM)

    n_chunks = PROMPT_TILE // RET_CHUNK
    assert n_chunks == 4
    plan = [dict(a=[], b=[]),
            dict(a=["m0", "m1"], b=[]),
            dict(a=["m2", "m3"], b=[]),
            dict(a=["v", "k"], b=["q0", "q1"])]
    a0 = MAIN_WIDTH
    attn_cols = {"v": (a0 + SWA_WIDTH + SWA_KV_WIDTH, SWA_KV_WIDTH), "k": (a0 + SWA_WIDTH, SWA_KV_WIDTH)}
    attn_cols.update({"q%d" % i: (a0 + Q_BLOCK * i, Q_BLOCK) for i in range(SWA_WIDTH // Q_BLOCK)})
    cols = lambda base, p: slice(base + p * LANES, base + (p + 1) * LANES)

    def project(items):
        raw = {}
        for it in items:
            if it[0] == "m":
                cb = int(it[1:])
                blk = slice(cb * PROJ_BLOCK, (cb + 1) * PROJ_BLOCK)
                main_ref[nxt, :, blk] = _dot(hb_ref[...], w_in_ref[:, blk])
            else:
                lo, width = attn_cols[it]
                raw[it] = _dot(hb_ref[...], w_in_ref[:, lo:lo + width])
        return raw

    def attn_scales(raw):
        return {name: [_head_rms_scale(val[:, cols(0, cq)], ones_bd) for cq in range(val.shape[1] // LANES)]
                for name, val in raw.items() if name != "v"}

    def store_attn(raw, scales):
        for name, val in raw.items():
            if name == "v":
                vs_ref[nxt] = val
            elif name == "k":
                kn_ref[nxt] = (val * scales[name][0]) * _pair_gain(kg_ref)
            else:
                base = attn_cols[name][0] - a0
                for cq in range(val.shape[1] // LANES):
                    qn_ref[nxt, :, cols(base, cq)] = (val[:, cols(0, cq)] * scales[name][cq]) * _pair_gain(qg_ref)

    pairs = range(N_PAIRS)
    kvs = range(N_SWA_KV)

    def stage1(c):
        rows = slice(c * RET_CHUNK, (c + 1) * RET_CHUNK)
        q = [main_ref[cur, rows, cols(0, p)] for p in pairs]
        k = [main_ref[cur, rows, cols(RET_WIDTH, p)] * K_SCALE for p in pairs]
        v = [main_ref[cur, rows, cols(2 * RET_WIDTH, p)] for p in pairs]
        state = [state_ref[p] for p in pairs]
        kc = kn_ref[cur, rows, :]
        k_sw = pltpu.roll(kc, HEAD_DIM, axis=1)
        v_t = vs_ref[cur, rows, :].T
        is_first = (t == 0).astype(jnp.int32) if c == 0 else 0
        k_dup =[(jnp.where(first, kc, k_sw) if j == 0 else jnp.where(first, k_sw, kc)).astype(BF16)
                 for j in kvs]
        v_tj = [v_t[j * HEAD_DIM:(j + 1) * HEAD_DIM].astype(BF16) for j in kvs]
        q_st = []
        for j in kvs:
            pieces = []
            for g in range(SWA_GROUP):
                qc = qn_ref[cur, rows, cols(0, 2 * j + g // 2)]
                pieces.append(jnp.where(first, qc, 0.0) if g % 2 == 0 else jnp.where(first, 0.0, qc))
            q_st.append(jnp.concatenate(pieces, axis=0).astype(BF16))

        s = [_dot_nt(q[p].astype(BF16), _split_pair_rows(k[p], first)) for p in pairs]
        s_t = [_dot_nt(jnp.concatenate([prevk_ref[j], k_dup[j]], axis=0), q_st[j]) for j in kvs]
        cross = [_dot((q[p] * qdec_ref[p]).astype(BF16), state[p].astype(BF16)) for p in pairs]
        upd = [_dot((k[p] * kdec_ref[p]).T.astype(BF16), v[p].astype(BF16)) for p in pairs]
        return dict(rows=rows, v=v, state=state, k_dup=k_dup, v_tj=v_tj, is_first=is_first,
                    s=s, s_t=s_t, cross=cross, upd=upd)

    def stage2(st):
        st["o"] = [_dot((st["s"][p] * intra_ref[p]).astype(BF16), _split_pair_rows(st["v"][p], first))
                   + st["cross"][p] for p in pairs]
        st["o_t"] = []
        for j in kvs:
            sink_lanes = jnp.concatenate(
                [jnp.full((1, WINDOW), sinks_ref[SWA_GROUP * j + g], F32) for g in range(SWA_GROUP)],
                axis=1)
            v_cat = jnp.concatenate([prevv_ref[j], st["v_tj"][j]], axis=1)
            st["o_t"].append(_softmax_sink_pv_t(st["s_t"][j] * K_SCALE + bias_ref[st["is_first"], j],
                                                sink_lanes, v_cat))
        for p in pairs:
            state_ref[p] = st["state"][p] * sdec_ref[p] + jnp.where(bd_mask, st["upd"][p], 0.0)
        for j in kvs:
            prevk_ref[j] = st["k_dup"][j]
            prevv_ref[j] = st["v_tj"][j]

    def stage3(st, raw):
        rows = st["rows"]
        scale = [_head_rms_scale(st["o"][p], ones_bd) for p in pairs]
        raw_scales = attn_scales(raw)
        for p in pairs:
            g = main_ref[cur, rows, cols(3 * RET_WIDTH, p)]
            mix_ref[rows, cols(0, p)] = (st["o"][p] * scale[p] * _silu(g)).astype(BF16)
        store_attn(raw, raw_scales)
        for j in kvs:
            o_t = st["o_t"][j]
            for half in range(2):
                pair_t = jnp.concatenate([o_t[:, (2 * half) * WINDOW:(2 * half + 1) * WINDOW],
                                          o_t[:, (2 * half + 1) * WINDOW:(2 * half + 2) * WINDOW]], axis=0)
                mix_ref[rows, cols(RET_WIDTH, 2 * j + half)] = pair_t.T.astype(BF16)

    xn = x_next_ref[...]
    hb_ref[...] = ((xn * lax.rsqrt(jnp.mean(xn * xn, axis=-1, keepdims=True) + EPS)) * gain_mix_ref[...]).astype(BF16)
    for c in range(n_chunks):
        st = stage1(c)
        raw = project(plan[c]["a"])
        stage2(st)
        raw.update(project(plan[c]["b"]))
        stage3(st, raw)

    @pl.when(t == pl.num_programs(1) - 1)
    def _():
        for p in range(N_PAIRS):
            s = state_ref[p]
            ret_ref[2 * p] = s[:HEAD_DIM, :HEAD_DIM]
            ret_ref[2 * p + 1] = s[HEAD_DIM:, HEAD_DIM:]
        last = slice(PROMPT_TILE - WINDOW, PROMPT_TILE)
        kwin_ref[...] = kn_ref[cur, last, :].T
        vwin_ref[...] = vs_ref[cur, last, :].T

    @pl.when(step == 0)
    def _():
        for copy in late_weights:
            copy.wait()

    _out_mlp_kernel(mix_ref.at[:, pl.ds(0, RET_WIDTH)], mix_ref.at[:, pl.ds(RET_WIDTH, SWA_WIDTH)], x_ref,
                    w_out_ref, gain_ffn_ref, w_up_ref, w_down_ref, y_ref)


def _prompt_layer(sinks, x2d, gain_mix, w_in_bf, qg, kg, w_out_bf, gain_ffn, w_up_bf, w_down_bf, batch, seq):
    nt = seq // PROMPT_TILE
    last_tile = batch * nt - 1
    row = lambda w: pl.BlockSpec((PROMPT_TILE, w), lambda b, t: (b * nt + t, 0))
    next_row = pl.BlockSpec((PROMPT_TILE, D_MODEL), lambda b, t: (jnp.minimum(b * nt + t + 1, last_tile), 0))
    full = lambda a: pl.BlockSpec(a.shape, lambda b, t: (0, 0), pipeline_mode=pl.Buffered(1))
    in_hbm = pl.BlockSpec(memory_space=pl.ANY)
    return pl.pallas_call(
        _prompt_layer_kernel,
        grid=(batch, nt),
        in_specs=[pl.BlockSpec(memory_space=pltpu.SMEM), row(D_MODEL), next_row,
                  full(gain_mix), full(w_in_bf), full(qg), full(kg),
                  in_hbm, full(gain_ffn), in_hbm, in_hbm],
        out_specs=[row(D_MODEL),
                   pl.BlockSpec((None, N_RET_HEADS, HEAD_DIM, HEAD_DIM), lambda b, t: (b, 0, 0, 0)),
                   pl.BlockSpec((None, SWA_KV_WIDTH, WINDOW), lambda b, t: (b, 0, 0)),
                   pl.BlockSpec((None, SWA_KV_WIDTH, WINDOW), lambda b, t: (b, 0, 0))],
        out_shape=[jax.ShapeDtypeStruct((batch * seq, D_MODEL), F32),
                   jax.ShapeDtypeStruct((batch, N_RET_HEADS, HEAD_DIM, HEAD_DIM), F32),
                   jax.ShapeDtypeStruct((batch, SWA_KV_WIDTH, WINDOW), F32),
                   jax.ShapeDtypeStruct((batch, SWA_KV_WIDTH, WINDOW), F32)],
        scratch_shapes=[
            pltpu.VMEM((2, PROMPT_TILE, MAIN_WIDTH), F32),
            pltpu.VMEM((2, PROMPT_TILE, SWA_WIDTH), F32),
            pltpu.VMEM((2, PROMPT_TILE, SWA_KV_WIDTH), F32),
            pltpu.VMEM((2, PROMPT_TILE, SWA_KV_WIDTH), F32),
            pltpu.VMEM((PROMPT_TILE, MIX_WIDTH), BF16),
            pltpu.VMEM((PROMPT_TILE, D_MODEL), BF16),
            pltpu.VMEM((N_PAIRS, LANES, LANES), F32),
            pltpu.VMEM((N_SWA_KV, WINDOW, LANES), BF16),
            pltpu.VMEM((N_SWA_KV, HEAD_DIM, WINDOW), BF16),
            pltpu.VMEM((N_PAIRS, LANES, 2 * LANES), F32),
            pltpu.VMEM((N_PAIRS, LANES, LANES), F32),
            pltpu.VMEM((N_PAIRS, LANES, LANES), F32),
            pltpu.VMEM((N_PAIRS, LANES, LANES), F32),
            pltpu.VMEM((2, N_SWA_KV, 2 * WINDOW, SWA_GROUP * WINDOW), F32),
            pltpu.VMEM(w_out_bf.shape, BF16),
            pltpu.VMEM(w_up_bf.shape, BF16),
            pltpu.VMEM(w_down_bf.shape, BF16),
            pltpu.SemaphoreType.DMA((3,)),
        ],
        compiler_params=pltpu.CompilerParams(
            dimension_semantics=("arbitrary", "arbitrary"), vmem_limit_bytes=PROMPT_VMEM_LIMIT),
        name="prompt_layer",
    )(sinks, x2d, x2d, gain_mix, w_in_bf, qg, kg, w_out_bf, gain_ffn, w_up_bf, w_down_bf)


DEC_ROWS = 128
DEC_UNROLL = 8


def _decode_attn_consts(dec_seq, bias_ref):
    shift = dec_seq.bit_length() - 1
    rows = N_SWA_HEADS * dec_seq
    rb = _iota((rows, WINDOW), 0)
    cb = _iota((rows, WINDOW), 1)
    head = rb >> shift
    i = rb & (dec_seq - 1)
    slope = jnp.zeros((rows, WINDOW), F32)
    for h in range(N_SWA_HEADS):
        slope = jnp.where(head == h, ALIBI_SLOPES[h], slope)
    bias_ref[0] = jnp.where(cb > i, -(slope * (WINDOW + i - cb).astype(F32)), NEG_INF)
    m = cb & (dec_seq - 1)
    bias_ref[1] = jnp.where(m <= i, -(slope * (i - m).astype(F32)), NEG_INF)


GROUPS_PER_STEP = 2


def _decode_mixers_kernel(dec_seq, nb, qdec_ref, kdec_ref, sdec_ref, intra_ref, sinks_ref,
                          q_ref, k_ref, v_ref, g_ref, st_ref, qn_ref, kn_ref, vs_ref, kt_ref, vt_ref,
                          mix_ret_ref, st_out_ref, mix_swa_ref, kt_out_ref, vt_out_ref,
                          qt_s, kt_s, vt_s, qdt_s, kdt_s, o_s,
                          bias_ref, qbd_ref, oblk_ref, knew_ref, vnew_ref, knt_ref, vst_ref):
    pair = pl.program_id(0)
    halves = [slice(0, HEAD_DIM), slice(HEAD_DIM, 2 * HEAD_DIM)]
    first = _first_half()
    shift = dec_seq.bit_length() - 1
    e_blk = HEAD_DIM // 2

    @pl.when(pair == 0)
    def _():
        _decode_attn_consts(dec_seq, bias_ref)

    def ret_stage():
        for l in range(dec_seq):
            rows = pl.ds(l, nb, stride=dec_seq)
            q_t = q_ref[rows, :].T
            k_t = (k_ref[rows, :] * K_SCALE).T
            qt_s[l] = q_t
            kt_s[l] = k_t
            vt_s[l] = v_ref[rows, :].T
            for hh in range(2):
                qdt_s[l, halves[hh], :] = q_t[halves[hh]] * qdec_ref[2 * pair + hh, l]
                kdt_s[l, halves[hh], :] = k_t[halves[hh]] * kdec_ref[2 * pair + hh, l]

    def ret_intra(hh):
        h = 2 * pair + hh
        hs = halves[hh]
        for l in range(dec_seq):
            acc = None
            for m in range(l + 1):
                sc = jnp.sum(qt_s[l, hs, :] * kt_s[m, hs, :], axis=0, keepdims=True) * intra_ref[h, l - m]
                term = sc * vt_s[m, hs, :]
                acc = term if acc is None else acc + term
            o_s[l, hs, :] = acc

    def ret_block(hh, eb, d_lo, d_hi, accs=None):
        h = 2 * pair + hh
        es = slice(eb * e_blk, (eb + 1) * e_blk)
        erows = slice(hh * HEAD_DIM + eb * e_blk, hh * HEAD_DIM + (eb + 1) * e_blk)
        if accs is None:
            accs = [jnp.zeros((e_blk, nb), F32) for _ in range(dec_seq)]
        for d in range(d_lo, d_hi):
            s_d = st_ref[hh, d, es, :]
            row = slice(hh * HEAD_DIM + d, hh * HEAD_DIM + d + 1)
            upd = s_d * sdec_ref[h]
            for l in range(dec_seq):
                accs[l] = accs[l] + qdt_s[l, row, :] * s_d
                upd = upd + kdt_s[l, row, :] * vt_s[l, erows, :]
            st_out_ref[hh, d, es, :] = upd
        if d_hi == HEAD_DIM:
            for l in range(dec_seq):
                o_s[l, erows, :] = o_s[l, erows, :] + accs[l]
        return accs

    def ret_finish():
        for l in range(dec_seq):
            o = o_s[l]
            normed = []
            for hh in range(2):
                oh = o[halves[hh]]
                normed.append(oh * lax.rsqrt(jnp.mean(oh * oh, axis=0, keepdims=True) + EPS))
            rows = pl.ds(l, nb, stride=dec_seq)
            mix_ret_ref[rows, :] = jnp.concatenate(normed, axis=0).T * _silu(g_ref[rows, :])

    sink_rows = jnp.concatenate(
        [jnp.full((dec_seq, LANES), sinks_ref[h], F32) for h in range(N_SWA_HEADS)], axis=0)
    col_batch = _iota((N_SWA_HEADS * dec_seq, LANES), 1) >> shift
    keep_old = _iota((1, LANES), 1) < WINDOW - dec_seq

    def attn_stage(gi):
        grows = slice(gi * DEC_ROWS, (gi + 1) * DEC_ROWS)
        kn_t = kn_ref[grows, :].T
        vs_t = vs_ref[grows, :].T
        knt_ref[...] = kn_t.astype(BF16)
        vst_ref[...] = vs_t.astype(BF16)
        for bb in range(DEC_GROUP):
            sh = (WINDOW - dec_seq - bb * dec_seq) % LANES
            knew_ref[bb] = pltpu.roll(kn_t, sh, axis=1) if sh else kn_t
            vnew_ref[bb] = pltpu.roll(vs_t, sh, axis=1) if sh else vs_t
        qn = qn_ref[grows, :]
        qn_sw = pltpu.roll(qn, HEAD_DIM, axis=1)
        for h in range(N_SWA_HEADS):
            kv_half = h // SWA_GROUP
            if (h % 2) == kv_half:
                src = qn[:, (h // 2) * LANES:(h // 2 + 1) * LANES]
            else:
                col = (h + 1) // 2
                src = qn_sw[:, col * LANES:(col + 1) * LANES]
            qbd_ref[h] = jnp.where(first, src, 0.0) if kv_half == 0 else jnp.where(first, 0.0, src)

    def attn_block(gi, i):
        bs = [i * DEC_UNROLL + u for u in range(DEC_UNROLL)]
        rows = [slice(b * dec_seq, (b + 1) * dec_seq) for b in bs]
        k_old = [kt_ref[gi * DEC_GROUP + b] for b in bs]
        v_old = [vt_ref[gi * DEC_GROUP + b] for b in bs]
        q_st = [jnp.concatenate([qbd_ref[h, r, :] for h in range(N_SWA_HEADS)], axis=0).astype(BF16)
                for r in rows]
        s = [_dot(q_st[u], jnp.concatenate([k_old[u].astype(BF16), knt_ref[...]], axis=1))
             for u in range(DEC_UNROLL)]
        o = []
        for u, b in enumerate(bs):
            bias = jnp.concatenate([bias_ref[0], jnp.where(col_batch == b, bias_ref[1], NEG_INF)], axis=1)
            w_v = jnp.concatenate([v_old[u].astype(BF16), vst_ref[...]], axis=1)
            o.append(_softmax_sink_pv(s[u] * K_SCALE + bias, sink_rows, w_v))
        for u, b in enumerate(bs):
            for h in range(N_SWA_HEADS):
                oblk_ref[h, rows[u], :] = o[u][h * dec_seq:(h + 1) * dec_seq]
            kt_out_ref[gi * DEC_GROUP + b] = jnp.where(
                keep_old, pltpu.roll(k_old[u], LANES - dec_seq, axis=1), knew_ref[b])
            vt_out_ref[gi * DEC_GROUP + b] = jnp.where(
                keep_old, pltpu.roll(v_old[u], LANES - dec_seq, axis=1), vnew_ref[b])

    def attn_finish(gi):
        grows = slice(gi * DEC_ROWS, (gi + 1) * DEC_ROWS)
        y1 = jnp.where(first, oblk_ref[3], oblk_ref[4])
        moved = pltpu.roll(jnp.concatenate([oblk_ref[1], y1, oblk_ref[6], oblk_ref[6]], axis=1), HEAD_DIM, axis=1)
        outs = [
            jnp.where(first, oblk_ref[0], moved[:, 0:LANES]),
            jnp.where(first, oblk_ref[2], moved[:, LANES:2 * LANES]),
            jnp.where(first, moved[:, 2 * LANES:3 * LANES], oblk_ref[5]),
            jnp.where(first, moved[:, 3 * LANES:4 * LANES], oblk_ref[7]),
        ]
        for c in range(SWA_WIDTH // LANES):
            mix_swa_ref[grows, c * LANES:(c + 1) * LANES] = outs[c].astype(BF16)

    ret_stage()
    for gi in range(GROUPS_PER_STEP):
        attn_stage(gi)
        ret_intra(gi)
        n_blocks = DEC_GROUP // DEC_UNROLL
        per_eb = n_blocks // 2
        d_step = HEAD_DIM // per_eb
        for eb in range(2):
            accs = None
            for part in range(per_eb):
                accs = ret_block(gi, eb, part * d_step, (part + 1) * d_step, accs)
                attn_block(gi, eb * per_eb + part)
        attn_finish(gi)
    ret_finish()


def _decode_mixers(sinks, main, qn, kn, vs, state_t, k_t, v_t, dec_seq):
    nb = state_t.shape[-1]
    m = main.shape[0]
    assert nb == LANES and m == nb * dec_seq and k_t.shape == (nb, SWA_KV_WIDTH, WINDOW)
    assert DEC_GROUP * dec_seq == DEC_ROWS and nb == N_PAIRS * GROUPS_PER_STEP * DEC_GROUP
    assert (DEC_GROUP // DEC_UNROLL) % 2 == 0 and dec_seq & (dec_seq - 1) == 0
    tab = lambda f: jnp.asarray([[f(h, j) for j in range(dec_seq)] for h in range(N_RET_HEADS)], F32)
    qdec = tab(lambda h, j: math.exp(LOG_DECAY[h] * (j + 1.0)))
    kdec = tab(lambda h, j: math.exp(LOG_DECAY[h] * (dec_seq - 1.0 - j)))
    intra = tab(lambda h, j: math.exp(LOG_DECAY[h] * j))
    sdec = jnp.asarray([math.exp(LOG_DECAY[h] * dec_seq) for h in range(N_RET_HEADS)], F32)
    smem = pl.BlockSpec(memory_space=pltpu.SMEM)
    col = lambda base: pl.BlockSpec((m, LANES), lambda p: (0, base + p))
    st_spec = pl.BlockSpec((2, HEAD_DIM, HEAD_DIM, nb), lambda p: (p, 0, 0, 0))
    step_rows = GROUPS_PER_STEP * DEC_ROWS
    row = lambda w: pl.BlockSpec((step_rows, w), lambda p: (p, 0))
    cache = pl.BlockSpec((GROUPS_PER_STEP * DEC_GROUP, SWA_KV_WIDTH, WINDOW), lambda p: (p, 0, 0))
    stage = pltpu.VMEM((dec_seq, LANES, nb), F32)
    return pl.pallas_call(
        functools.partial(_decode_mixers_kernel, dec_seq, nb),
        grid=(N_PAIRS,),
        in_specs=[smem, smem, smem, smem, smem,
                  col(0), col(N_PAIRS), col(2 * N_PAIRS), col(3 * N_PAIRS), st_spec,
                  row(SWA_WIDTH), row(SWA_KV_WIDTH), row(SWA_KV_WIDTH), cache, cache],
        out_specs=[pl.BlockSpec((m, LANES), lambda p: (0, p)), st_spec, row(SWA_WIDTH), cache, cache],
        out_shape=[jax.ShapeDtypeStruct((m, RET_WIDTH), F32),
                   jax.ShapeDtypeStruct(state_t.shape, F32),
                   jax.ShapeDtypeStruct((m, SWA_WIDTH), BF16),
                   jax.ShapeDtypeStruct(k_t.shape, F32),
                   jax.ShapeDtypeStruct(v_t.shape, F32)],
        scratch_shapes=[
            stage, stage, stage, stage, stage, stage,
            pltpu.VMEM((2, N_SWA_HEADS * dec_seq, WINDOW), F32),
            pltpu.VMEM((N_SWA_HEADS, DEC_ROWS, LANES), F32),
            pltpu.VMEM((N_SWA_HEADS, DEC_ROWS, LANES), F32),
            pltpu.VMEM((DEC_GROUP, SWA_KV_WIDTH, LANES), F32),
            pltpu.VMEM((DEC_GROUP, SWA_KV_WIDTH, LANES), F32),
            pltpu.VMEM((SWA_KV_WIDTH, DEC_ROWS), BF16),
            pltpu.VMEM((SWA_KV_WIDTH, DEC_ROWS), BF16),
        ],
        compiler_params=pltpu.CompilerParams(
            dimension_semantics=("arbitrary",), vmem_limit_bytes=VMEM_LIMIT),
        name="decode_mixers",
    )(qdec, kdec, sdec, intra, sinks, main, main, main, main, state_t, qn, kn, vs, k_t, v_t)


def kernel(x_prompt, x_sample, state_ret, cache_swa_k, cache_swa_v, norm_mix_gain, w_in, q_norm_gain,
           k_norm_gain, attn_sinks, w_out, norm_ffn_gain, w_up, w_down):
    batch, seq, d = x_prompt.shape
    nb, dec_seq, _ = x_sample.shape
    wb = cache_swa_k.shape[1]
    assert d == D_MODEL and seq % PROMPT_TILE == 0 and wb == WINDOW

    gain_mix = norm_mix_gain.reshape(1, D_MODEL)
    gain_ffn = norm_ffn_gain.reshape(1, D_MODEL)
    qg = q_norm_gain.reshape(1, HEAD_DIM)
    kg = k_norm_gain.reshape(1, HEAD_DIM)

    def from_key_minor(a_t):
        return jnp.transpose(a_t.reshape(a_t.shape[0], N_SWA_KV, HEAD_DIM, WINDOW), (0, 3, 1, 2))

    def to_key_minor(a):
        return jnp.transpose(a, (0, 2, 3, 1)).reshape(a.shape[0], SWA_KV_WIDTH, WINDOW)

    xs = x_sample.reshape(nb * dec_seq, D_MODEL)
    main_s, qn_s, kn_s, vs_s, w_in_bf = _in_proj_cast(xs, gain_mix, w_in, qg, kg)
    mix_ret_s, state_t, mix_swa_s, k_t, v_t = _decode_mixers(
        attn_sinks, main_s, qn_s, kn_s, vs_s, jnp.transpose(state_ret, (1, 2, 3, 0)),
        to_key_minor(cache_swa_k), to_key_minor(cache_swa_v), dec_seq)
    y_s, w_out_bf, w_up_bf, w_down_bf = _out_mlp_cast(mix_ret_s, mix_swa_s, xs, gain_ffn, w_out, w_up, w_down)

    xp = x_prompt.reshape(batch * seq, D_MODEL)
    y_p, ret_p, kwin_t, vwin_t = _prompt_layer(attn_sinks, xp, gain_mix, w_in_bf, qg, kg, w_out_bf, gain_ffn,
                                               w_up_bf, w_down_bf, batch, seq)
    y_p = y_p.reshape(batch, seq, D_MODEL)

    return (y_p, y_s.reshape(nb, dec_seq, D_MODEL), ret_p, from_key_minor(kwin_t), from_key_minor(vwin_t),
            jnp.transpose(state_t, (3, 0, 1, 2)), from_key_minor(k_t), from_key_minor(v_t))
```

```python
import functools
import math

import jax
import jax.numpy as jnp
from jax import lax
from jax.experimental import pallas as pl
from jax.experimental.pallas import tpu as pltpu

F32 = jnp.float32
BF16 = jnp.bfloat16

D_MODEL = 1024
HEAD_DIM = 64
N_RET_HEADS = 8
N_SWA_HEADS = 8
N_SWA_KV = 2
SWA_GROUP = N_SWA_HEADS // N_SWA_KV
RET_WIDTH = N_RET_HEADS * HEAD_DIM
SWA_WIDTH = N_SWA_HEADS * HEAD_DIM
SWA_KV_WIDTH = N_SWA_KV * HEAD_DIM
MAIN_WIDTH = 4 * RET_WIDTH
IN_WIDTH = MAIN_WIDTH + SWA_WIDTH + 2 * SWA_KV_WIDTH
MIX_WIDTH = RET_WIDTH + SWA_WIDTH
D_FF = 4 * D_MODEL
WINDOW = 128
RET_CHUNK = 128
EPS = 1e-6
NEG_INF = -1e30

LANES = 128
N_PAIRS = N_RET_HEADS // 2
LOG_DECAY = [math.log(1.0 - 2.0 ** (-5.0 - h)) for h in range(N_RET_HEADS)]
ALIBI_SLOPES = [2.0 ** (-8.0 * (h + 1) / N_SWA_HEADS) for h in range(N_SWA_HEADS)]
K_SCALE = HEAD_DIM ** -0.5

MXU_WIDTH = 256
PROJ_BLOCK = 2 * MXU_WIDTH
Q_BLOCK = MXU_WIDTH
PROMPT_TILE = 512
DEC_GROUP = 16
VMEM_LIMIT = 56 * 1024 * 1024
PROMPT_VMEM_LIMIT = 62 * 1024 * 1024


def _dot(a, b):
    return jnp.dot(a, b, preferred_element_type=F32)


def _dot_nt(a, b):
    return lax.dot_general(a, b, (((1,), (1,)), ((), ())), preferred_element_type=F32)


def _iota(shape, dim):
    return lax.broadcasted_iota(jnp.int32, shape, dim)


def _ones_block_diag():
    same = ((_iota((2 * LANES, LANES), 0) >> 6) & 1) == (_iota((2 * LANES, LANES), 1) >> 6)
    return jnp.where(same, 1.0, 0.0).astype(BF16)


def _head_sumsq(x, ones_bd):
    x2 = x * x
    hi = x2.astype(BF16)
    lo = (x2 - hi.astype(F32)).astype(BF16)
    return _dot(jnp.concatenate([hi, lo], axis=1), ones_bd)


def _head_rms_scale(x, ones_bd):
    return lax.rsqrt(_head_sumsq(x, ones_bd) * (1.0 / HEAD_DIM) + EPS)


def _head_rms_scales(xs, ones_bd):
    if not xs:
        return []
    parts = []
    for x in xs:
        x2 = x * x
        hi = x2.astype(BF16)
        parts.append(jnp.concatenate([hi, (x2 - hi.astype(F32)).astype(BF16)], axis=1))
    total = _dot(jnp.concatenate(parts, axis=0) if len(parts) > 1 else parts[0], ones_bd)
    out, lo = [], 0
    for x in xs:
        out.append(lax.rsqrt(total[lo:lo + x.shape[0]] * (1.0 / HEAD_DIM) + EPS))
        lo += x.shape[0]
    return out


def _first_half():
    return _iota((1, LANES), 1) < HEAD_DIM


def _pair_const(values, pair, lane_is_second):
    return jnp.where(lane_is_second, values[2 * pair + 1], values[2 * pair]).astype(F32)


def _pair_gain(gain_ref):
    g = gain_ref[...]
    return jnp.concatenate([g, g], axis=1)


def _silu(g):
    return g * (1.0 / (1.0 + jnp.exp(-g)))


def _in_proj_kernel(x_ref, gain_ref, w_ref, qg_ref, kg_ref, main_ref, qn_ref, kn_ref, vs_ref, fetch=None):
    x = x_ref[...]
    ms = jnp.mean(x * x, axis=-1, keepdims=True)
    hb = ((x * lax.rsqrt(ms + EPS)) * gain_ref[...]).astype(BF16)
    ones_bd = _ones_block_diag()

    def project(lo, width):
        if fetch is not None:
            fetch(lo, width)
        return _dot(hb, w_ref[:, lo:lo + width])

    for c in range(MAIN_WIDTH // PROJ_BLOCK):
        main_ref[:, c * PROJ_BLOCK:(c + 1) * PROJ_BLOCK] = project(c * PROJ_BLOCK, PROJ_BLOCK)
    qs = project(MAIN_WIDTH, SWA_WIDTH)
    k0 = MAIN_WIDTH + SWA_WIDTH
    ks = project(k0, SWA_KV_WIDTH)
    vs_ref[...] = project(k0 + SWA_KV_WIDTH, SWA_KV_WIDTH)
    q_cols = [qs[:, c * LANES:(c + 1) * LANES] for c in range(SWA_WIDTH // LANES)]
    scales = _head_rms_scales(q_cols + [ks], ones_bd)
    for c, xc in enumerate(q_cols):
        qn_ref[:, c * LANES:(c + 1) * LANES] = (xc * scales[c]) * _pair_gain(qg_ref)
    kn_ref[...] = (ks * scales[-1]) * _pair_gain(kg_ref)


ROW_TILE = 512


W_IN_BLOCKS = [(c * PROJ_BLOCK, PROJ_BLOCK) for c in range(MAIN_WIDTH // PROJ_BLOCK)] + [(MAIN_WIDTH, SWA_WIDTH),
                                                                    (MAIN_WIDTH + SWA_WIDTH, 2 * SWA_KV_WIDTH)]
assert sum(w for _, w in W_IN_BLOCKS) == IN_WIDTH


def _in_proj_cast_kernel(x_ref, gain_ref, w_hbm, qg_ref, kg_ref,
                         main_ref, qn_ref, kn_ref, vs_ref, w_bf_hbm,
                         w_v, stage, in_sem, out_sem):
    step = pl.program_id(0)
    block_in = [pltpu.make_async_copy(w_hbm.at[:, pl.ds(lo, width)], stage.at[:, pl.ds(lo, width)], in_sem.at[i])
                for i, (lo, width) in enumerate(W_IN_BLOCKS)]
    writeback = pltpu.make_async_copy(w_v, w_bf_hbm, out_sem.at[0])

    @pl.when(step == 0)
    def _():
        for copy in block_in:
            copy.start()
        arrived = set()

        def fetch(lo, width):
            for i, (blo, bwidth) in enumerate(W_IN_BLOCKS):
                if blo <= lo < blo + bwidth and i not in arrived:
                    assert lo + width <= blo + bwidth
                    block_in[i].wait()
                    w_v[:, blo:blo + bwidth] = stage[:, blo:blo + bwidth].astype(BF16)
                    arrived.add(i)

        _in_proj_kernel(x_ref, gain_ref, w_v, qg_ref, kg_ref, main_ref, qn_ref, kn_ref, vs_ref, fetch=fetch)
        assert len(arrived) == len(W_IN_BLOCKS)
        writeback.start()

    @pl.when(step > 0)
    def _():
        _in_proj_kernel(x_ref, gain_ref, w_v, qg_ref, kg_ref, main_ref, qn_ref, kn_ref, vs_ref)

    @pl.when(step == pl.num_programs(0) - 1)
    def _():
        writeback.wait()


def _in_proj_cast(x2d, gain, w_in, qg, kg):
    m = x2d.shape[0]
    tm = min(ROW_TILE, m)
    row = lambda w: pl.BlockSpec((tm, w), lambda i: (i, 0))
    full = lambda a: pl.BlockSpec(a.shape, lambda i: (0, 0), pipeline_mode=pl.Buffered(1))
    hbm = pl.BlockSpec(memory_space=pl.ANY)
    return pl.pallas_call(
        _in_proj_cast_kernel,
        grid=(m // tm,),
        in_specs=[row(D_MODEL), full(gain), hbm, full(qg), full(kg)],
        out_specs=[row(MAIN_WIDTH), row(SWA_WIDTH), row(SWA_KV_WIDTH), row(SWA_KV_WIDTH), hbm],
        out_shape=[jax.ShapeDtypeStruct((m, MAIN_WIDTH), F32),
                   jax.ShapeDtypeStruct((m, SWA_WIDTH), F32),
                   jax.ShapeDtypeStruct((m, SWA_KV_WIDTH), F32),
                   jax.ShapeDtypeStruct((m, SWA_KV_WIDTH), F32),
                   jax.ShapeDtypeStruct(w_in.shape, BF16)],
        scratch_shapes=[pltpu.VMEM(w_in.shape, BF16), pltpu.VMEM(w_in.shape, F32),
                        pltpu.SemaphoreType.DMA((len(W_IN_BLOCKS),)), pltpu.SemaphoreType.DMA((1,))],
        compiler_params=pltpu.CompilerParams(
            dimension_semantics=("arbitrary",), vmem_limit_bytes=VMEM_LIMIT),
        name="in_proj_cast",
    )(x2d, gain, w_in, qg, kg)


FF_CHUNK = 1024


def _out_mlp_kernel(mix_ret_ref, mix_swa_ref, x_ref, w_out_ref, gain_ref, w_up_ref, w_down_ref, y_ref):
    h = x_ref[...] + (_dot(mix_ret_ref[...].astype(BF16), w_out_ref[:RET_WIDTH, :])
                      + _dot(mix_swa_ref[...].astype(BF16), w_out_ref[RET_WIDTH:, :]))
    ms = jnp.mean(h * h, axis=-1, keepdims=True)
    hf = ((h * lax.rsqrt(ms + EPS)) * gain_ref[...]).astype(BF16)
    ff = None
    for c in range(D_FF // FF_CHUNK):
        u = _dot(hf, w_up_ref[:, c * FF_CHUNK:(c + 1) * FF_CHUNK])
        a = jnp.maximum(u, 0.0)
        d = _dot((a * a).astype(BF16), w_down_ref[c * FF_CHUNK:(c + 1) * FF_CHUNK, :])
        ff = d if ff is None else ff + d
    y_ref[...] = h + ff


CAST_CHUNK = 512
N_FF_CHUNKS = D_FF // CAST_CHUNK


def _out_mlp_cast_kernel(mix_ret_ref, mix_swa_ref, x_ref, gain_ref, w_out_hbm, w_up_hbm, w_down_hbm,
                         y_ref, w_out_bf_hbm, w_up_bf_hbm, w_down_bf_hbm,
                         w_out_v, w_up_v, w_down_v, stage_out, stage_up, stage_down, in_sem, out_sem):
    step = pl.program_id(0)

    def out_in(i):
        return pltpu.make_async_copy(w_out_hbm.at[pl.ds(i * CAST_CHUNK, CAST_CHUNK), :], stage_out.at[i],
                                     in_sem.at[0, i])

    def up_in(c):
        return pltpu.make_async_copy(w_up_hbm.at[:, pl.ds(c * CAST_CHUNK, CAST_CHUNK)], stage_up.at[c % 2],
                                     in_sem.at[1, c % 2])

    def down_in(c):
        return pltpu.make_async_copy(w_down_hbm.at[pl.ds(c * CAST_CHUNK, CAST_CHUNK), :], stage_down.at[c % 2],
                                     in_sem.at[2, c % 2])

    writebacks = [pltpu.make_async_copy(src, dst, out_sem.at[i]) for i, (src, dst) in enumerate(
        [(w_out_v, w_out_bf_hbm), (w_up_v, w_up_bf_hbm), (w_down_v, w_down_bf_hbm)])]

    @pl.when(step == 0)
    def _():
        n_out = D_MODEL // CAST_CHUNK
        for i in range(n_out):
            out_in(i).start()
        up_in(0).start()
        down_in(0).start()
        assert n_out == 2 and RET_WIDTH == CAST_CHUNK
        proj = None
        for i, mix_ref in enumerate([mix_ret_ref, mix_swa_ref]):
            rows = slice(i * CAST_CHUNK, (i + 1) * CAST_CHUNK)
            out_in(i).wait()
            w_out_v[rows, :] = stage_out[i].astype(BF16)
            d = _dot(mix_ref[...].astype(BF16), w_out_v[rows, :])
            proj = d if proj is None else proj + d
        writebacks[0].start()
        h = x_ref[...] + proj
        ms = jnp.mean(h * h, axis=-1, keepdims=True)
        hf = ((h * lax.rsqrt(ms + EPS)) * gain_ref[...]).astype(BF16)
        ff = None
        for c in range(N_FF_CHUNKS):
            chunk = slice(c * CAST_CHUNK, (c + 1) * CAST_CHUNK)
            if c + 1 < N_FF_CHUNKS:
                up_in(c + 1).start()
                down_in(c + 1).start()
            up_in(c).wait()
            down_in(c).wait()
            w_up_v[:, chunk] = stage_up[c % 2].astype(BF16)
            w_down_v[chunk, :] = stage_down[c % 2].astype(BF16)
            a = jnp.maximum(_dot(hf, w_up_v[:, chunk]), 0.0)
            d = _dot((a * a).astype(BF16), w_down_v[chunk, :])
            ff = d if ff is None else ff + d
        writebacks[1].start()
        writebacks[2].start()
        y_ref[...] = h + ff

    @pl.when(step > 0)
    def _():
        _out_mlp_kernel(mix_ret_ref, mix_swa_ref, x_ref, w_out_v, gain_ref, w_up_v, w_down_v, y_ref)

    @pl.when(step == pl.num_programs(0) - 1)
    def _():
        for wb in writebacks:
            wb.wait()


def _out_mlp_cast(mix_ret, mix_swa, x2d, gain, w_out, w_up, w_down):
    m = x2d.shape[0]
    tm = min(ROW_TILE, m)
    assert D_MODEL % CAST_CHUNK == 0 and D_FF % CAST_CHUNK == 0
    row = lambda w: pl.BlockSpec((tm, w), lambda i: (i, 0))
    full = lambda a: pl.BlockSpec(a.shape, lambda i: (0, 0), pipeline_mode=pl.Buffered(1))
    hbm = pl.BlockSpec(memory_space=pl.ANY)
    return pl.pallas_call(
        _out_mlp_cast_kernel,
        grid=(m // tm,),
        in_specs=[row(RET_WIDTH), row(SWA_WIDTH), row(D_MODEL), full(gain), hbm, hbm, hbm],
        out_specs=[row(D_MODEL), hbm, hbm, hbm],
        out_shape=[jax.ShapeDtypeStruct((m, D_MODEL), F32),
                   jax.ShapeDtypeStruct(w_out.shape, BF16),
                   jax.ShapeDtypeStruct(w_up.shape, BF16),
                   jax.ShapeDtypeStruct(w_down.shape, BF16)],
        scratch_shapes=[
            pltpu.VMEM(w_out.shape, BF16), pltpu.VMEM(w_up.shape, BF16), pltpu.VMEM(w_down.shape, BF16),
            pltpu.VMEM((D_MODEL // CAST_CHUNK, CAST_CHUNK, D_MODEL), F32),
            pltpu.VMEM((2, D_MODEL, CAST_CHUNK), F32),
            pltpu.VMEM((2, CAST_CHUNK, D_MODEL), F32),
            pltpu.SemaphoreType.DMA((3, 2)),
            pltpu.SemaphoreType.DMA((3,)),
        ],
        compiler_params=pltpu.CompilerParams(
            dimension_semantics=("arbitrary",), vmem_limit_bytes=VMEM_LIMIT),
        name="out_mlp_cast",
    )(mix_ret, mix_swa, x2d, gain, w_out, w_up, w_down)


def _split_pair_rows(x, first):
    return jnp.concatenate([jnp.where(first, x, 0.0), jnp.where(first, 0.0, x)], axis=0).astype(BF16)


def _softmax_sink_pv(s, sink_wide, v_t):
    m = jnp.maximum(jnp.max(s, axis=-1, keepdims=True), sink_wide)
    p = jnp.exp(s - jnp.concatenate([m, m], axis=1))
    denom = jnp.sum(p, axis=-1, keepdims=True) + jnp.exp(sink_wide - m)
    return _dot_nt(p.astype(BF16), v_t) / denom


def _softmax_sink_pv_t(s_t, sink_lanes, v_t):
    m = jnp.maximum(jnp.max(s_t, axis=0, keepdims=True), sink_lanes)
    p = jnp.exp(s_t - m)
    denom = jnp.sum(p, axis=0, keepdims=True) + jnp.exp(sink_lanes - m)
    return _dot(v_t, p.astype(BF16)) / denom


def _prompt_consts(intra_ref, qdec_ref, kdec_ref, sdec_ref, bias_ref):
    r = _iota((LANES, LANES), 0)
    lane2 = _iota((LANES, LANES), 1) >= HEAD_DIM
    rf = r.astype(F32)
    ri = _iota((LANES, 2 * LANES), 0)
    ci = _iota((LANES, 2 * LANES), 1)
    diff = (ri - (ci & (LANES - 1))).astype(F32)
    for p in range(N_PAIRS):
        lg = _pair_const(LOG_DECAY, p, lane2)
        qdec_ref[p] = jnp.exp(lg * (rf + 1.0))
        kdec_ref[p] = jnp.exp(lg * (RET_CHUNK - 1.0 - rf))
        sdec_ref[p] = jnp.exp(_pair_const(LOG_DECAY, p, r >= HEAD_DIM) * float(RET_CHUNK))
        lg2 = _pair_const(LOG_DECAY, p, ci >= LANES)
        intra_ref[p] = jnp.where(diff >= 0.0, jnp.exp(lg2 * jnp.maximum(diff, 0.0)), 0.0)
    cols = SWA_GROUP * WINDOW
    kb = _iota((2 * WINDOW, cols), 0)
    cb = _iota((2 * WINDOW, cols), 1)
    grp = cb >> 7
    dist = WINDOW + (cb & (WINDOW - 1)) - kb
    valid = (dist >= 0) & (dist < WINDOW)
    distf = dist.astype(F32)
    for j in range(N_SWA_KV):
        sl = [ALIBI_SLOPES[SWA_GROUP * j + g] for g in range(SWA_GROUP)]
        slope = jnp.where(grp == 0, sl[0], jnp.where(grp == 1, sl[1], jnp.where(grp == 2, sl[2], sl[3])))
        b = jnp.where(valid, -(slope.astype(F32) * distf), NEG_INF)
        bias_ref[0, j] = b
        bias_ref[1, j] = jnp.where(kb >= WINDOW, b, NEG_INF)


def _prompt_layer_kernel(sinks_ref, x_ref, x_next_ref, gain_mix_ref, w_in_ref, qg_ref, kg_ref,
                         w_out_hbm, gain_ffn_ref, w_up_hbm, w_down_hbm,
                         y_ref, ret_ref, kwin_ref, vwin_ref,
                         main_ref, qn_ref, kn_ref, vs_ref, mix_ref, hb_ref,
                         state_ref, prevk_ref, prevv_ref,
                         intra_ref, qdec_ref, kdec_ref, sdec_ref, bias_ref,
                         w_out_ref, w_up_ref, w_down_ref, w_sem):
    t = pl.program_id(1)
    step = pl.program_id(0) * pl.num_programs(1) + t
    cur = step % 2
    nxt = 1 - cur

    late_weights = [pltpu.make_async_copy(src, dst, w_sem.at[i]) for i, (src, dst) in enumerate(
        [(w_out_hbm, w_out_ref), (w_up_hbm, w_up_ref), (w_down_hbm, w_down_ref)])]

    @pl.when(step == 0)
    def _():
        for copy in late_weights:
            copy.start()
        _prompt_consts(intra_ref, qdec_ref, kdec_ref, sdec_ref, bias_ref)
        _in_proj_kernel(x_ref, gain_mix_ref, w_in_ref, qg_ref, kg_ref,
                        main_ref.at[0], qn_ref.at[0], kn_ref.at[0], vs_ref.at[0])

    @pl.when(t == 0)
    def _():
        state_ref[...] = jnp.zeros_like(state_ref)
        prevk_ref[...] = jnp.zeros_like(prevk_ref)
        prevv_ref[...] = jnp.zeros_like(prevv_ref)

    first = _first_half()
    ones_bd = _ones_block_diag()
    bd_mask = (_iota((LANES, LANES), 0) >= HEAD_DIM) == (_iota((LANES, LANES), 1) >= HEAD_DIM)

    n_chunks = PROMPT_TILE // RET_CHUNK
    assert n_chunks == 4
    plan = [dict(a=[], b=[]),
            dict(a=["m0", "m1"], b=[]),
            dict(a=["m2", "m3"], b=[]),
            dict(a=["v", "k"], b=["q0", "q1"])]
    a0 = MAIN_WIDTH
    attn_cols = {"v": (a0 + SWA_WIDTH + SWA_KV_WIDTH, SWA_KV_WIDTH), "k": (a0 + SWA_WIDTH, SWA_KV_WIDTH)}
    attn_cols.update({"q%d" % i: (a0 + Q_BLOCK * i, Q_BLOCK) for i in range(SWA_WIDTH // Q_BLOCK)})
    cols = lambda base, p: slice(base + p * LANES, base + (p + 1) * LANES)

    def project(items):
        raw = {}
        for it in items:
            if it[0] == "m":
                cb = int(it[1:])
                blk = slice(cb * PROJ_BLOCK, (cb + 1) * PROJ_BLOCK)
                main_ref[nxt, :, blk] = _dot(hb_ref[...], w_in_ref[:, blk])
            else:
                lo, width = attn_cols[it]
                raw[it] = _dot(hb_ref[...], w_in_ref[:, lo:lo + width])
        return raw

    def attn_scales(raw):
        return {name: [_head_rms_scale(val[:, cols(0, cq)], ones_bd) for cq in range(val.shape[1] // LANES)]
                for name, val in raw.items() if name != "v"}

    def store_attn(raw, scales):
        for name, val in raw.items():
            if name == "v":
                vs_ref[nxt] = val
            elif name == "k":
                kn_ref[nxt] = (val * scales[name][0]) * _pair_gain(kg_ref)
            else:
                base = attn_cols[name][0] - a0
                for cq in range(val.shape[1] // LANES):
                    qn_ref[nxt, :, cols(base, cq)] = (val[:, cols(0, cq)] * scales[name][cq]) * _pair_gain(qg_ref)

    pairs = range(N_PAIRS)
    kvs = range(N_SWA_KV)

    def stage1(c):
        rows = slice(c * RET_CHUNK, (c + 1) * RET_CHUNK)
        q = [main_ref[cur, rows, cols(0, p)] for p in pairs]
        k = [main_ref[cur, rows, cols(RET_WIDTH, p)] * K_SCALE for p in pairs]
        v = [main_ref[cur, rows, cols(2 * RET_WIDTH, p)] for p in pairs]
        state = [state_ref[p] for p in pairs]
        kc = kn_ref[cur, rows, :]
        k_sw = pltpu.roll(kc, HEAD_DIM, axis=1)
        v_t = vs_ref[cur, rows, :].T
        is_first = (t == 0).astype(jnp.int32) if c == 0 else 0
        k_dup =[(jnp.where(first, kc, k_sw) if j == 0 else jnp.where(first, k_sw, kc)).astype(BF16)
                 for j in kvs]
        v_tj = [v_t[j * HEAD_DIM:(j + 1) * HEAD_DIM].astype(BF16) for j in kvs]
        q_st = []
        for j in kvs:
            pieces = []
            for g in range(SWA_GROUP):
                qc = qn_ref[cur, rows, cols(0, 2 * j + g // 2)]
                pieces.append(jnp.where(first, qc, 0.0) if g % 2 == 0 else jnp.where(first, 0.0, qc))
            q_st.append(jnp.concatenate(pieces, axis=0).astype(BF16))

        s = [_dot_nt(q[p].astype(BF16), _split_pair_rows(k[p], first)) for p in pairs]
        s_t = [_dot_nt(jnp.concatenate([prevk_ref[j], k_dup[j]], axis=0), q_st[j]) for j in kvs]
        cross = [_dot((q[p] * qdec_ref[p]).astype(BF16), state[p].astype(BF16)) for p in pairs]
        upd = [_dot((k[p] * kdec_ref[p]).T.astype(BF16), v[p].astype(BF16)) for p in pairs]
        return dict(rows=rows, v=v, state=state, k_dup=k_dup, v_tj=v_tj, is_first=is_first,
                    s=s, s_t=s_t, cross=cross, upd=upd)

    def stage2(st):
        st["o"] = [_dot((st["s"][p] * intra_ref[p]).astype(BF16), _split_pair_rows(st["v"][p], first))
                   + st["cross"][p] for p in pairs]
        st["o_t"] = []
        for j in kvs:
            sink_lanes = jnp.concatenate(
                [jnp.full((1, WINDOW), sinks_ref[SWA_GROUP * j + g], F32) for g in range(SWA_GROUP)],
                axis=1)
            v_cat = jnp.concatenate([prevv_ref[j], st["v_tj"][j]], axis=1)
            st["o_t"].append(_softmax_sink_pv_t(st["s_t"][j] * K_SCALE + bias_ref[st["is_first"], j],
                                                sink_lanes, v_cat))
        for p in pairs:
            state_ref[p] = st["state"][p] * sdec_ref[p] + jnp.where(bd_mask, st["upd"][p], 0.0)
        for j in kvs:
            prevk_ref[j] = st["k_dup"][j]
            prevv_ref[j] = st["v_tj"][j]

    def stage3(st, raw):
        rows = st["rows"]
        scale = [_head_rms_scale(st["o"][p], ones_bd) for p in pairs]
        raw_scales = attn_scales(raw)
        for p in pairs:
            g = main_ref[cur, rows, cols(3 * RET_WIDTH, p)]
            mix_ref[rows, cols(0, p)] = (st["o"][p] * scale[p] * _silu(g)).astype(BF16)
        store_attn(raw, raw_scales)
        for j in kvs:
            o_t = st["o_t"][j]
            for half in range(2):
                pair_t = jnp.concatenate([o_t[:, (2 * half) * WINDOW:(2 * half + 1) * WINDOW],
                                          o_t[:, (2 * half + 1) * WINDOW:(2 * half + 2) * WINDOW]], axis=0)
                mix_ref[rows, cols(RET_WIDTH, 2 * j + half)] = pair_t.T.astype(BF16)

    xn = x_next_ref[...]
    hb_ref[...] = ((xn * lax.rsqrt(jnp.mean(xn * xn, axis=-1, keepdims=True) + EPS)) * gain_mix_ref[...]).astype(BF16)
    for c in range(n_chunks):
        st = stage1(c)
        raw = project(plan[c]["a"])
        stage2(st)
        raw.update(project(plan[c]["b"]))
        stage3(st, raw)

    @pl.when(t == pl.num_programs(1) - 1)
    def _():
        for p in range(N_PAIRS):
            s = state_ref[p]
            ret_ref[2 * p] = s[:HEAD_DIM, :HEAD_DIM]
            ret_ref[2 * p + 1] = s[HEAD_DIM:, HEAD_DIM:]
        last = slice(PROMPT_TILE - WINDOW, PROMPT_TILE)
        kwin_ref[...] = kn_ref[cur, last, :].T
        vwin_ref[...] = vs_ref[cur, last, :].T

    @pl.when(step == 0)
    def _():
        for copy in late_weights:
            copy.wait()

    _out_mlp_kernel(mix_ref.at[:, pl.ds(0, RET_WIDTH)], mix_ref.at[:, pl.ds(RET_WIDTH, SWA_WIDTH)], x_ref,
                    w_out_ref, gain_ffn_ref, w_up_ref, w_down_ref, y_ref)


def _prompt_layer(sinks, x2d, gain_mix, w_in_bf, qg, kg, w_out_bf, gain_ffn, w_up_bf, w_down_bf, batch, seq):
    nt = seq // PROMPT_TILE
    last_tile = batch * nt - 1
    row = lambda w: pl.BlockSpec((PROMPT_TILE, w), lambda b, t: (b * nt + t, 0))
    next_row = pl.BlockSpec((PROMPT_TILE, D_MODEL), lambda b, t: (jnp.minimum(b * nt + t + 1, last_tile), 0))
    full = lambda a: pl.BlockSpec(a.shape, lambda b, t: (0, 0), pipeline_mode=pl.Buffered(1))
    in_hbm = pl.BlockSpec(memory_space=pl.ANY)
    return pl.pallas_call(
        _prompt_layer_kernel,
        grid=(batch, nt),
        in_specs=[pl.BlockSpec(memory_space=pltpu.SMEM), row(D_MODEL), next_row,
                  full(gain_mix), full(w_in_bf), full(qg), full(kg),
                  in_hbm, full(gain_ffn), in_hbm, in_hbm],
        out_specs=[row(D_MODEL),
                   pl.BlockSpec((None, N_RET_HEADS, HEAD_DIM, HEAD_DIM), lambda b, t: (b, 0, 0, 0)),
                   pl.BlockSpec((None, SWA_KV_WIDTH, WINDOW), lambda b, t: (b, 0, 0)),
                   pl.BlockSpec((None, SWA_KV_WIDTH, WINDOW), lambda b, t: (b, 0, 0))],
        out_shape=[jax.ShapeDtypeStruct((batch * seq, D_MODEL), F32),
                   jax.ShapeDtypeStruct((batch, N_RET_HEADS, HEAD_DIM, HEAD_DIM), F32),
                   jax.ShapeDtypeStruct((batch, SWA_KV_WIDTH, WINDOW), F32),
                   jax.ShapeDtypeStruct((batch, SWA_KV_WIDTH, WINDOW), F32)],
        scratch_shapes=[
            pltpu.VMEM((2, PROMPT_TILE, MAIN_WIDTH), F32),
            pltpu.VMEM((2, PROMPT_TILE, SWA_WIDTH), F32),
            pltpu.VMEM((2, PROMPT_TILE, SWA_KV_WIDTH), F32),
            pltpu.VMEM((2, PROMPT_TILE, SWA_KV_WIDTH), F32),
            pltpu.VMEM((PROMPT_TILE, MIX_WIDTH), BF16),
            pltpu.VMEM((PROMPT_TILE, D_MODEL), BF16),
            pltpu.VMEM((N_PAIRS, LANES, LANES), F32),
            pltpu.VMEM((N_SWA_KV, WINDOW, LANES), BF16),
            pltpu.VMEM((N_SWA_KV, HEAD_DIM, WINDOW), BF16),
            pltpu.VMEM((N_PAIRS, LANES, 2 * LANES), F32),
            pltpu.VMEM((N_PAIRS, LANES, LANES), F32),
            pltpu.VMEM((N_PAIRS, LANES, LANES), F32),
            pltpu.VMEM((N_PAIRS, LANES, LANES), F32),
            pltpu.VMEM((2, N_SWA_KV, 2 * WINDOW, SWA_GROUP * WINDOW), F32),
            pltpu.VMEM(w_out_bf.shape, BF16),
            pltpu.VMEM(w_up_bf.shape, BF16),
            pltpu.VMEM(w_down_bf.shape, BF16),
            pltpu.SemaphoreType.DMA((3,)),
        ],
        compiler_params=pltpu.CompilerParams(
            dimension_semantics=("arbitrary", "arbitrary"), vmem_limit_bytes=PROMPT_VMEM_LIMIT),
        name="prompt_layer",
    )(sinks, x2d, x2d, gain_mix, w_in_bf, qg, kg, w_out_bf, gain_ffn, w_up_bf, w_down_bf)


DEC_ROWS = 128
DEC_UNROLL = 8


def _decode_attn_consts(dec_seq, bias_ref):
    shift = dec_seq.bit_length() - 1
    rows = N_SWA_HEADS * dec_seq
    rb = _iota((rows, WINDOW), 0)
    cb = _iota((rows, WINDOW), 1)
    head = rb >> shift
    i = rb & (dec_seq - 1)
    slope = jnp.zeros((rows, WINDOW), F32)
    for h in range(N_SWA_HEADS):
        slope = jnp.where(head == h, ALIBI_SLOPES[h], slope)
    bias_ref[0] = jnp.where(cb > i, -(slope * (WINDOW + i - cb).astype(F32)), NEG_INF)
    m = cb & (dec_seq - 1)
    bias_ref[1] = jnp.where(m <= i, -(slope * (i - m).astype(F32)), NEG_INF)


GROUPS_PER_STEP = 2


def _decode_mixers_kernel(dec_seq, nb, qdec_ref, kdec_ref, sdec_ref, intra_ref, sinks_ref,
                          q_ref, k_ref, v_ref, g_ref, st_ref, qn_ref, kn_ref, vs_ref, kt_ref, vt_ref,
                          mix_ret_ref, st_out_ref, mix_swa_ref, kt_out_ref, vt_out_ref,
                          qt_s, kt_s, vt_s, qdt_s, kdt_s, o_s,
                          bias_ref, qbd_ref, oblk_ref, knew_ref, vnew_ref, knt_ref, vst_ref):
    pair = pl.program_id(0)
    halves = [slice(0, HEAD_DIM), slice(HEAD_DIM, 2 * HEAD_DIM)]
    first = _first_half()
    shift = dec_seq.bit_length() - 1
    e_blk = HEAD_DIM // 2

    @pl.when(pair == 0)
    def _():
        _decode_attn_consts(dec_seq, bias_ref)

    def ret_stage():
        for l in range(dec_seq):
            rows = pl.ds(l, nb, stride=dec_seq)
            q_t = q_ref[rows, :].T
            k_t = (k_ref[rows, :] * K_SCALE).T
            qt_s[l] = q_t
            kt_s[l] = k_t
            vt_s[l] = v_ref[rows, :].T
            for hh in range(2):
                qdt_s[l, halves[hh], :] = q_t[halves[hh]] * qdec_ref[2 * pair + hh, l]
                kdt_s[l, halves[hh], :] = k_t[halves[hh]] * kdec_ref[2 * pair + hh, l]

    def ret_intra(hh):
        h = 2 * pair + hh
        hs = halves[hh]
        for l in range(dec_seq):
            acc = None
            for m in range(l + 1):
                sc = jnp.sum(qt_s[l, hs, :] * kt_s[m, hs, :], axis=0, keepdims=True) * intra_ref[h, l - m]
                term = sc * vt_s[m, hs, :]
                acc = term if acc is None else acc + term
            o_s[l, hs, :] = acc

    def ret_block(hh, eb, d_lo, d_hi, accs=None):
        h = 2 * pair + hh
        es = slice(eb * e_blk, (eb + 1) * e_blk)
        erows = slice(hh * HEAD_DIM + eb * e_blk, hh * HEAD_DIM + (eb + 1) * e_blk)
        if accs is None:
            accs = [jnp.zeros((e_blk, nb), F32) for _ in range(dec_seq)]
        for d in range(d_lo, d_hi):
            s_d = st_ref[hh, d, es, :]
            row = slice(hh * HEAD_DIM + d, hh * HEAD_DIM + d + 1)
            upd = s_d * sdec_ref[h]
            for l in range(dec_seq):
                accs[l] = accs[l] + qdt_s[l, row, :] * s_d
                upd = upd + kdt_s[l, row, :] * vt_s[l, erows, :]
            st_out_ref[hh, d, es, :] = upd
        if d_hi == HEAD_DIM:
            for l in range(dec_seq):
                o_s[l, erows, :] = o_s[l, erows, :] + accs[l]
        return accs

    def ret_finish():
        for l in range(dec_seq):
            o = o_s[l]
            normed = []
            for hh in range(2):
                oh = o[halves[hh]]
                normed.append(oh * lax.rsqrt(jnp.mean(oh * oh, axis=0, keepdims=True) + EPS))
            rows = pl.ds(l, nb, stride=dec_seq)
            mix_ret_ref[rows, :] = jnp.concatenate(normed, axis=0).T * _silu(g_ref[rows, :])

    sink_rows = jnp.concatenate(
        [jnp.full((dec_seq, LANES), sinks_ref[h], F32) for h in range(N_SWA_HEADS)], axis=0)
    col_batch = _iota((N_SWA_HEADS * dec_seq, LANES), 1) >> shift
    keep_old = _iota((1, LANES), 1) < WINDOW - dec_seq

    def attn_stage(gi):
        grows = slice(gi * DEC_ROWS, (gi + 1) * DEC_ROWS)
        kn_t = kn_ref[grows, :].T
        vs_t = vs_ref[grows, :].T
        knt_ref[...] = kn_t.astype(BF16)
        vst_ref[...] = vs_t.astype(BF16)
        for bb in range(DEC_GROUP):
            sh = (WINDOW - dec_seq - bb * dec_seq) % LANES
            knew_ref[bb] = pltpu.roll(kn_t, sh, axis=1) if sh else kn_t
            vnew_ref[bb] = pltpu.roll(vs_t, sh, axis=1) if sh else vs_t
        qn = qn_ref[grows, :]
        qn_sw = pltpu.roll(qn, HEAD_DIM, axis=1)
        for h in range(N_SWA_HEADS):
            kv_half = h // SWA_GROUP
            if (h % 2) == kv_half:
                src = qn[:, (h // 2) * LANES:(h // 2 + 1) * LANES]
            else:
                col = (h + 1) // 2
                src = qn_sw[:, col * LANES:(col + 1) * LANES]
            qbd_ref[h] = jnp.where(first, src, 0.0) if kv_half == 0 else jnp.where(first, 0.0, src)

    def attn_block(gi, i):
        bs = [i * DEC_UNROLL + u for u in range(DEC_UNROLL)]
        rows = [slice(b * dec_seq, (b + 1) * dec_seq) for b in bs]
        k_old = [kt_ref[gi * DEC_GROUP + b] for b in bs]
        v_old = [vt_ref[gi * DEC_GROUP + b] for b in bs]
        q_st = [jnp.concatenate([qbd_ref[h, r, :] for h in range(N_SWA_HEADS)], axis=0).astype(BF16)
                for r in rows]
        s = [_dot(q_st[u], jnp.concatenate([k_old[u].astype(BF16), knt_ref[...]], axis=1))
             for u in range(DEC_UNROLL)]
        o = []
        for u, b in enumerate(bs):
            bias = jnp.concatenate([bias_ref[0], jnp.where(col_batch == b, bias_ref[1], NEG_INF)], axis=1)
            w_v = jnp.concatenate([v_old[u].astype(BF16), vst_ref[...]], axis=1)
            o.append(_softmax_sink_pv(s[u] * K_SCALE + bias, sink_rows, w_v))
        for u, b in enumerate(bs):
            for h in range(N_SWA_HEADS):
                oblk_ref[h, rows[u], :] = o[u][h * dec_seq:(h + 1) * dec_seq]
            kt_out_ref[gi * DEC_GROUP + b] = jnp.where(
                keep_old, pltpu.roll(k_old[u], LANES - dec_seq, axis=1), knew_ref[b])
            vt_out_ref[gi * DEC_GROUP + b] = jnp.where(
                keep_old, pltpu.roll(v_old[u], LANES - dec_seq, axis=1), vnew_ref[b])

    def attn_finish(gi):
        grows = slice(gi * DEC_ROWS, (gi + 1) * DEC_ROWS)
        y1 = jnp.where(first, oblk_ref[3], oblk_ref[4])
        moved = pltpu.roll(jnp.concatenate([oblk_ref[1], y1, oblk_ref[6], oblk_ref[6]], axis=1), HEAD_DIM, axis=1)
        outs = [
            jnp.where(first, oblk_ref[0], moved[:, 0:LANES]),
            jnp.where(first, oblk_ref[2], moved[:, LANES:2 * LANES]),
            jnp.where(first, moved[:, 2 * LANES:3 * LANES], oblk_ref[5]),
            jnp.where(first, moved[:, 3 * LANES:4 * LANES], oblk_ref[7]),
        ]
        for c in range(SWA_WIDTH // LANES):
            mix_swa_ref[grows, c * LANES:(c + 1) * LANES] = outs[c].astype(BF16)

    ret_stage()
    for gi in range(GROUPS_PER_STEP):
        attn_stage(gi)
        ret_intra(gi)
        n_blocks = DEC_GROUP // DEC_UNROLL
        per_eb = n_blocks // 2
        d_step = HEAD_DIM // per_eb
        for eb in range(2):
            accs = None
            for part in range(per_eb):
                accs = ret_block(gi, eb, part * d_step, (part + 1) * d_step, accs)
                attn_block(gi, eb * per_eb + part)
        attn_finish(gi)
    ret_finish()


def _decode_mixers(sinks, main, qn, kn, vs, state_t, k_t, v_t, dec_seq):
    nb = state_t.shape[-1]
    m = main.shape[0]
    assert nb == LANES and m == nb * dec_seq and k_t.shape == (nb, SWA_KV_WIDTH, WINDOW)
    assert DEC_GROUP * dec_seq == DEC_ROWS and nb == N_PAIRS * GROUPS_PER_STEP * DEC_GROUP
    assert (DEC_GROUP // DEC_UNROLL) % 2 == 0 and dec_seq & (dec_seq - 1) == 0
    tab = lambda f: jnp.asarray([[f(h, j) for j in range(dec_seq)] for h in range(N_RET_HEADS)], F32)
    qdec = tab(lambda h, j: math.exp(LOG_DECAY[h] * (j + 1.0)))
    kdec = tab(lambda h, j: math.exp(LOG_DECAY[h] * (dec_seq - 1.0 - j)))
    intra = tab(lambda h, j: math.exp(LOG_DECAY[h] * j))
    sdec = jnp.asarray([math.exp(LOG_DECAY[h] * dec_seq) for h in range(N_RET_HEADS)], F32)
    smem = pl.BlockSpec(memory_space=pltpu.SMEM)
    col = lambda base: pl.BlockSpec((m, LANES), lambda p: (0, base + p))
    st_spec = pl.BlockSpec((2, HEAD_DIM, HEAD_DIM, nb), lambda p: (p, 0, 0, 0))
    step_rows = GROUPS_PER_STEP * DEC_ROWS
    row = lambda w: pl.BlockSpec((step_rows, w), lambda p: (p, 0))
    cache = pl.BlockSpec((GROUPS_PER_STEP * DEC_GROUP, SWA_KV_WIDTH, WINDOW), lambda p: (p, 0, 0))
    stage = pltpu.VMEM((dec_seq, LANES, nb), F32)
    return pl.pallas_call(
        functools.partial(_decode_mixers_kernel, dec_seq, nb),
        grid=(N_PAIRS,),
        in_specs=[smem, smem, smem, smem, smem,
                  col(0), col(N_PAIRS), col(2 * N_PAIRS), col(3 * N_PAIRS), st_spec,
                  row(SWA_WIDTH), row(SWA_KV_WIDTH), row(SWA_KV_WIDTH), cache, cache],
        out_specs=[pl.BlockSpec((m, LANES), lambda p: (0, p)), st_spec, row(SWA_WIDTH), cache, cache],
        out_shape=[jax.ShapeDtypeStruct((m, RET_WIDTH), F32),
                   jax.ShapeDtypeStruct(state_t.shape, F32),
                   jax.ShapeDtypeStruct((m, SWA_WIDTH), BF16),
                   jax.ShapeDtypeStruct(k_t.shape, F32),
                   jax.ShapeDtypeStruct(v_t.shape, F32)],
        scratch_shapes=[
            stage, stage, stage, stage, stage, stage,
            pltpu.VMEM((2, N_SWA_HEADS * dec_seq, WINDOW), F32),
            pltpu.VMEM((N_SWA_HEADS, DEC_ROWS, LANES), F32),
            pltpu.VMEM((N_SWA_HEADS, DEC_ROWS, LANES), F32),
            pltpu.VMEM((DEC_GROUP, SWA_KV_WIDTH, LANES), F32),
            pltpu.VMEM((DEC_GROUP, SWA_KV_WIDTH, LANES), F32),
            pltpu.VMEM((SWA_KV_WIDTH, DEC_ROWS), BF16),
            pltpu.VMEM((SWA_KV_WIDTH, DEC_ROWS), BF16),
        ],
        compiler_params=pltpu.CompilerParams(
            dimension_semantics=("arbitrary",), vmem_limit_bytes=VMEM_LIMIT),
        name="decode_mixers",
    )(qdec, kdec, sdec, intra, sinks, main, main, main, main, state_t, qn, kn, vs, k_t, v_t)


def kernel(x_prompt, x_sample, state_ret, cache_swa_k, cache_swa_v, norm_mix_gain, w_in, q_norm_gain,
           k_norm_gain, attn_sinks, w_out, norm_ffn_gain, w_up, w_down):
    batch, seq, d = x_prompt.shape
    nb, dec_seq, _ = x_sample.shape
    wb = cache_swa_k.shape[1]
    assert d == D_MODEL and seq % PROMPT_TILE == 0 and wb == WINDOW

    gain_mix = norm_mix_gain.reshape(1, D_MODEL)
    gain_ffn = norm_ffn_gain.reshape(1, D_MODEL)
    qg = q_norm_gain.reshape(1, HEAD_DIM)
    kg = k_norm_gain.reshape(1, HEAD_DIM)

    def from_key_minor(a_t):
        return jnp.transpose(a_t.reshape(a_t.shape[0], N_SWA_KV, HEAD_DIM, WINDOW), (0, 3, 1, 2))

    def to_key_minor(a):
        return jnp.transpose(a, (0, 2, 3, 1)).reshape(a.shape[0], SWA_KV_WIDTH, WINDOW)

    xs = x_sample.reshape(nb * dec_seq, D_MODEL)
    main_s, qn_s, kn_s, vs_s, w_in_bf = _in_proj_cast(xs, gain_mix, w_in, qg, kg)
    mix_ret_s, state_t, mix_swa_s, k_t, v_t = _decode_mixers(
        attn_sinks, main_s, qn_s, kn_s, vs_s, jnp.transpose(state_ret, (1, 2, 3, 0)),
        to_key_minor(cache_swa_k), to_key_minor(cache_swa_v), dec_seq)
    y_s, w_out_bf, w_up_bf, w_down_bf = _out_mlp_cast(mix_ret_s, mix_swa_s, xs, gain_ffn, w_out, w_up, w_down)

    xp = x_prompt.reshape(batch * seq, D_MODEL)
    y_p, ret_p, kwin_t, vwin_t = _prompt_layer(attn_sinks, xp, gain_mix, w_in_bf, qg, kg, w_out_bf, gain_ffn,
                                               w_up_bf, w_down_bf, batch, seq)
    y_p = y_p.reshape(batch, seq, D_MODEL)

    return (y_p, y_s.reshape(nb, dec_seq, D_MODEL), ret_p, from_key_minor(kwin_t), from_key_minor(vwin_t),
            jnp.transpose(state_t, (3, 0, 1, 2)), from_key_minor(k_t), from_key_minor(v_t))
```
